```python
import jax, jax.numpy as jnp
from jax import lax
import numpy as np

D_MODEL = 1024
BATCH = 8
SEQ = 2048
DEPTH = 2

D_CONV = D_MODEL // 2
CONV_K = 31
D_SG = D_MODEL // 2
SG_GROUPS = 4
SG_CHUNK = 128
N_HEADS = 8
QK_NOPE = 64
QK_ROPE = 32
V_HEAD = 64
Q_LORA = 384
KV_LORA = 256
ROPE_THETA = 10000.0
Q_BLOCK = 128
N_BRANCH = 3
D_FF = 2816
FFN_K = 3
EPS = 1e-6

IN_SPLITS = (2 * D_CONV, 2 * D_SG, Q_LORA, KV_LORA, QK_ROPE, N_BRANCH * D_MODEL)
D_IN = 2 * D_CONV + 2 * D_SG + Q_LORA + KV_LORA + QK_ROPE + N_BRANCH * D_MODEL

kernel_name = "hybrid_gated_conv_sgu_mla_block"


def rmsnorm(x, g):
    x32 = x.astype(jnp.float32)
    y = x32 * lax.rsqrt(jnp.mean(x32 * x32, axis=-1, keepdims=True) + EPS)
    return (y * g.astype(jnp.float32)).astype(x.dtype)


def layernorm(x, g, b):
    x32 = x.astype(jnp.float32)
    mu = jnp.mean(x32, axis=-1, keepdims=True)
    var = jnp.mean(jnp.square(x32 - mu), axis=-1, keepdims=True)
    y = (x32 - mu) * lax.rsqrt(var + EPS) * g.astype(jnp.float32) + b.astype(jnp.float32)
    return y.astype(x.dtype)


def causal_dwconv(x, w, b):
    k, c = w.shape
    y = lax.conv_general_dilated(
        x, w[:, None, :].astype(x.dtype), window_strides=(1,), padding=[(k - 1, 0)],
        dimension_numbers=("NWC", "WIO", "NWC"), feature_group_count=c)
    return y + b.astype(x.dtype)


def rope_tables(positions):
    inv = ROPE_THETA ** (-jnp.arange(0, QK_ROPE, 2, dtype=jnp.float32) / QK_ROPE)
    ang = positions.astype(jnp.float32)[..., None] * inv
    return jnp.cos(ang), jnp.sin(ang)


def apply_rope(x, cos, sin):
    cos = cos.astype(x.dtype)
    sin = sin.astype(x.dtype)
    x1, x2 = jnp.split(x, 2, axis=-1)
    return jnp.concatenate([x1 * cos - x2 * sin, x2 * cos + x1 * sin], axis=-1)


def conv_module(a, w_dw, b_dw, ln_g, ln_b, w_p):
    val, gate = jnp.split(a, 2, axis=-1)
    z = val * jax.nn.sigmoid(gate)
    z = causal_dwconv(z, w_dw, b_dw)
    z = jax.nn.silu(layernorm(z, ln_g, ln_b))
    return z @ w_p


def spatial_gating(a, ln_g, ln_b, w_s, b_s, w_p):
    bsz, s, _ = a.shape
    u, v = jnp.split(jax.nn.gelu(a), 2, axis=-1)
    v = layernorm(v, ln_g, ln_b)
    nc = s // SG_CHUNK
    v = v.reshape(bsz, nc, SG_CHUNK, SG_GROUPS, D_SG // SG_GROUPS)
    causal = jnp.tril(jnp.ones((SG_CHUNK, SG_CHUNK), dtype=bool))
    w = jnp.where(causal[None], w_s, 0.0).astype(v.dtype)
    mixed = jnp.einsum("gts,bnsgc->bntgc", w, v) + b_s.T.astype(v.dtype)[None, None, :, :, None]
    mixed = mixed.reshape(bsz, s, D_SG)
    return (u * mixed) @ w_p


def mla(q_lat, kv_lat, k_rope, cos, sin, q_norm_g, w_uq, kv_norm_g, w_ukv, w_o):
    bsz, s, _ = q_lat.shape
    q = (rmsnorm(q_lat, q_norm_g) @ w_uq).reshape(bsz, s, N_HEADS, QK_NOPE + QK_ROPE)
    q_nope = q[..., :QK_NOPE]
    q_pe = apply_rope(q[..., QK_NOPE:], cos[:, :, None, :], sin[:, :, None, :])
    kv = (rmsnorm(kv_lat, kv_norm_g) @ w_ukv).reshape(bsz, s, N_HEADS, QK_NOPE + V_HEAD)
    k_nope = kv[..., :QK_NOPE]
    v = kv[..., QK_NOPE:]
    k_pe = apply_rope(k_rope, cos, sin)
    scale = (QK_NOPE + QK_ROPE) ** -0.5
    neg = jnp.finfo(jnp.float32).min
    outs = []
    for i in range(s // Q_BLOCK):
        q0, q1 = i * Q_BLOCK, (i + 1) * Q_BLOCK
        sc = (jnp.einsum("bqhd,bkhd->bhqk", q_nope[:, q0:q1], k_nope[:, :q1])
              + jnp.einsum("bqhr,bkr->bhqk", q_pe[:, q0:q1], k_pe[:, :q1])).astype(jnp.float32) * scale
        mask = (q0 + jnp.arange(Q_BLOCK))[:, None] >= jnp.arange(q1)[None, :]
        p = jax.nn.softmax(jnp.where(mask, sc, neg), axis=-1).astype(v.dtype)
        outs.append(jnp.einsum("bhqk,bkhd->bqhd", p, v[:, :q1]))
    o = jnp.concatenate(outs, axis=1).reshape(bsz, s, N_HEADS * V_HEAD)
    return o @ w_o


def conv_ffn(h, w_up, w_dw, b_dw, w_down):
    z = causal_dwconv(h @ w_up, w_dw, b_dw)
    g, v = jnp.split(z, 2, axis=-1)
    return (jax.nn.gelu(g, approximate=True) * v) @ w_down


def _fwd_setup_inputs(seed: int = 0) -> dict:
    key = jax.random.key(seed)
    ks = jax.random.split(key, 32)

    def nrm(k, shape, scale):
        return jax.random.normal(k, shape, dtype=jnp.float32) * scale

    def gain(k, n):
        return 1.0 + nrm(k, (DEPTH, n), 0.1)

    L = DEPTH
    x = nrm(ks[0], (BATCH, SEQ, D_MODEL), 1.0)
    offs = jax.random.randint(ks[1], (BATCH, 1), 0, SEQ, dtype=jnp.int32)
    positions = offs + jnp.arange(SEQ, dtype=jnp.int32)[None, :]
    return {
        "x": x,
        "positions": positions,
        "mix_pre_g": gain(ks[2], D_MODEL),
        "mix_post_g": gain(ks[3], D_MODEL),
        "ffn_pre_g": gain(ks[4], D_MODEL),
        "ffn_post_g": gain(ks[5], D_MODEL),
        "w_in": nrm(ks[6], (L, D_MODEL, D_IN), D_MODEL ** -0.5),
        "conv_dw_w": nrm(ks[7], (L, CONV_K, D_CONV), CONV_K ** -0.5),
        "conv_dw_b": nrm(ks[8], (L, D_CONV), 0.02),
        "conv_ln_g": gain(ks[9], D_CONV),
        "conv_ln_b": nrm(ks[10], (L, D_CONV), 0.02),
        "conv_out_w": nrm(ks[11], (L, D_CONV, D_MODEL), D_CONV ** -0.5),
        "sg_ln_g": gain(ks[12], D_SG),
        "sg_ln_b": nrm(ks[13], (L, D_SG), 0.02),
        "sg_w": nrm(ks[14], (L, SG_GROUPS, SG_CHUNK, SG_CHUNK), SG_CHUNK ** -0.5),
        "sg_b": 1.0 + nrm(ks[15], (L, SG_GROUPS, SG_CHUNK), 0.1),
        "sg_out_w": nrm(ks[16], (L, D_SG, D_MODEL), D_SG ** -0.5),
        "mla_q_norm_g": gain(ks[17], Q_LORA),
        "mla_w_uq": nrm(ks[18], (L, Q_LORA, N_HEADS * (QK_NOPE + QK_ROPE)), Q_LORA ** -0.5),
        "mla_kv_norm_g": gain(ks[19], KV_LORA),
        "mla_w_ukv": nrm(ks[20], (L, KV_LORA, N_HEADS * (QK_NOPE + V_HEAD)), KV_LORA ** -0.5),
        "mla_w_o": nrm(ks[21], (L, N_HEADS * V_HEAD, D_MODEL), (N_HEADS * V_HEAD) ** -0.5),
        "w_out": nrm(ks[22], (L, D_MODEL, D_MODEL), D_MODEL ** -0.5),
        "ffn_w_up": nrm(ks[23], (L, D_MODEL, 2 * D_FF), D_MODEL ** -0.5),
        "ffn_dw_w": nrm(ks[24], (L, FFN_K, 2 * D_FF), FFN_K ** -0.5),
        "ffn_dw_b": nrm(ks[25], (L, 2 * D_FF), 0.02),
        "ffn_w_down": nrm(ks[26], (L, D_FF, D_MODEL), D_FF ** -0.5),
    }


def _fwd_reference(x, positions, mix_pre_g, mix_post_g, ffn_pre_g, ffn_post_g, w_in,
              conv_dw_w, conv_dw_b, conv_ln_g, conv_ln_b, conv_out_w,
              sg_ln_g, sg_ln_b, sg_w, sg_b, sg_out_w,
              mla_q_norm_g, mla_w_uq, mla_kv_norm_g, mla_w_ukv, mla_w_o,
              w_out, ffn_w_up, ffn_dw_w, ffn_dw_b, ffn_w_down):
    bsz, s, d = x.shape
    cos, sin = rope_tables(positions)
    cuts = np.cumsum(IN_SPLITS)[:-1].tolist()
    for l in range(DEPTH):
        h = rmsnorm(x, mix_pre_g[l])
        a_in, b_in, q_lat, kv_lat, k_rope, gates = jnp.split(h @ w_in[l], cuts, axis=-1)
        y_a = conv_module(a_in, conv_dw_w[l], conv_dw_b[l], conv_ln_g[l], conv_ln_b[l], conv_out_w[l])
        y_b = spatial_gating(b_in, sg_ln_g[l], sg_ln_b[l], sg_w[l], sg_b[l], sg_out_w[l])
        y_c = mla(q_lat, kv_lat, k_rope, cos, sin, mla_q_norm_g[l], mla_w_uq[l],
                  mla_kv_norm_g[l], mla_w_ukv[l], mla_w_o[l])
        g = jax.nn.sigmoid(gates.reshape(bsz, s, N_BRANCH, d))
        merged = g[:, :, 0] * y_a + g[:, :, 1] * y_b + g[:, :, 2] * y_c
        x = x + rmsnorm(merged @ w_out[l], mix_post_g[l])
        h = rmsnorm(x, ffn_pre_g[l])
        x = x + rmsnorm(conv_ffn(h, ffn_w_up[l], ffn_dw_w[l], ffn_dw_b[l], ffn_w_down[l]), ffn_post_g[l])
    return x


import jax as _jax
import jax.numpy as _jnp

TWIN_FORMAT = 'train_step'
FWD_PARAMS = ['x', 'positions', 'mix_pre_g', 'mix_post_g', 'ffn_pre_g', 'ffn_post_g', 'w_in', 'conv_dw_w', 'conv_dw_b', 'conv_ln_g', 'conv_ln_b', 'conv_out_w', 'sg_ln_g', 'sg_ln_b', 'sg_w', 'sg_b', 'sg_out_w', 'mla_q_norm_g', 'mla_w_uq', 'mla_kv_norm_g', 'mla_w_ukv', 'mla_w_o', 'w_out', 'ffn_w_up', 'ffn_dw_w', 'ffn_dw_b', 'ffn_w_down']
TWIN_WEIGHTS = ['mix_pre_g', 'mix_post_g', 'ffn_pre_g', 'ffn_post_g', 'w_in', 'conv_dw_w', 'conv_dw_b', 'conv_ln_g', 'conv_ln_b', 'conv_out_w', 'sg_ln_g', 'sg_ln_b', 'sg_w', 'sg_b', 'sg_out_w', 'mla_q_norm_g', 'mla_w_uq', 'mla_kv_norm_g', 'mla_w_ukv', 'mla_w_o', 'w_out', 'ffn_w_up', 'ffn_dw_w', 'ffn_dw_b', 'ffn_w_down']
TWIN_DIFF_INPUT = 'x'
TWIN_INPUTS = ['x', 'positions', 'mix_pre_g', 'mix_post_g', 'ffn_pre_g', 'ffn_post_g', 'w_in', 'conv_dw_w', 'conv_dw_b', 'conv_ln_g', 'conv_ln_b', 'conv_out_w', 'sg_ln_g', 'sg_ln_b', 'sg_w', 'sg_b', 'sg_out_w', 'mla_q_norm_g', 'mla_w_uq', 'mla_kv_norm_g', 'mla_w_ukv', 'mla_w_o', 'w_out', 'ffn_w_up', 'ffn_dw_w', 'ffn_dw_b', 'ffn_w_down', 'loss_target', 'm_mix_pre_g', 'm_mix_post_g', 'm_ffn_pre_g', 'm_ffn_post_g', 'm_w_in', 'm_conv_dw_w', 'm_conv_dw_b', 'm_conv_ln_g', 'm_conv_ln_b', 'm_conv_out_w', 'm_sg_ln_g', 'm_sg_ln_b', 'm_sg_w', 'm_sg_b', 'm_sg_out_w', 'm_mla_q_norm_g', 'm_mla_w_uq', 'm_mla_kv_norm_g', 'm_mla_w_ukv', 'm_mla_w_o', 'm_w_out', 'm_ffn_w_up', 'm_ffn_dw_w', 'm_ffn_dw_b', 'm_ffn_w_down', 'v_mix_pre_g', 'v_mix_post_g', 'v_ffn_pre_g', 'v_ffn_post_g', 'v_w_in', 'v_conv_dw_w', 'v_conv_dw_b', 'v_conv_ln_g', 'v_conv_ln_b', 'v_conv_out_w', 'v_sg_ln_g', 'v_sg_ln_b', 'v_sg_w', 'v_sg_b', 'v_sg_out_w', 'v_mla_q_norm_g', 'v_mla_w_uq', 'v_mla_kv_norm_g', 'v_mla_w_ukv', 'v_mla_w_o', 'v_w_out', 'v_ffn_w_up', 'v_ffn_dw_w', 'v_ffn_dw_b', 'v_ffn_w_down']
TWIN_OUTPUTS = ['loss', 'grad_x', 'grad_mix_pre_g', 'grad_mix_post_g', 'grad_ffn_pre_g', 'grad_ffn_post_g', 'grad_w_in', 'grad_conv_dw_w', 'grad_conv_dw_b', 'grad_conv_ln_g', 'grad_conv_ln_b', 'grad_conv_out_w', 'grad_sg_ln_g', 'grad_sg_ln_b', 'grad_sg_w', 'grad_sg_b', 'grad_sg_out_w', 'grad_mla_q_norm_g', 'grad_mla_w_uq', 'grad_mla_kv_norm_g', 'grad_mla_w_ukv', 'grad_mla_w_o', 'grad_w_out', 'grad_ffn_w_up', 'grad_ffn_dw_w', 'grad_ffn_dw_b', 'grad_ffn_w_down', 'delta_mix_pre_g', 'delta_mix_post_g', 'delta_ffn_pre_g', 'delta_ffn_post_g', 'delta_w_in', 'delta_conv_dw_w', 'delta_conv_dw_b', 'delta_conv_ln_g', 'delta_conv_ln_b', 'delta_conv_out_w', 'delta_sg_ln_g', 'delta_sg_ln_b', 'delta_sg_w', 'delta_sg_b', 'delta_sg_out_w', 'delta_mla_q_norm_g', 'delta_mla_w_uq', 'delta_mla_kv_norm_g', 'delta_mla_w_ukv', 'delta_mla_w_o', 'delta_w_out', 'delta_ffn_w_up', 'delta_ffn_dw_w', 'delta_ffn_dw_b', 'delta_ffn_w_down', 'new_m_mix_pre_g', 'new_m_mix_post_g', 'new_m_ffn_pre_g', 'new_m_ffn_post_g', 'new_m_w_in', 'new_m_conv_dw_w', 'new_m_conv_dw_b', 'new_m_conv_ln_g', 'new_m_conv_ln_b', 'new_m_conv_out_w', 'new_m_sg_ln_g', 'new_m_sg_ln_b', 'new_m_sg_w', 'new_m_sg_b', 'new_m_sg_out_w', 'new_m_mla_q_norm_g', 'new_m_mla_w_uq', 'new_m_mla_kv_norm_g', 'new_m_mla_w_ukv', 'new_m_mla_w_o', 'new_m_w_out', 'new_m_ffn_w_up', 'new_m_ffn_dw_w', 'new_m_ffn_dw_b', 'new_m_ffn_w_down', 'new_v_mix_pre_g', 'new_v_mix_post_g', 'new_v_ffn_pre_g', 'new_v_ffn_post_g', 'new_v_w_in', 'new_v_conv_dw_w', 'new_v_conv_dw_b', 'new_v_conv_ln_g', 'new_v_conv_ln_b', 'new_v_conv_out_w', 'new_v_sg_ln_g', 'new_v_sg_ln_b', 'new_v_sg_w', 'new_v_sg_b', 'new_v_sg_out_w', 'new_v_mla_q_norm_g', 'new_v_mla_w_uq', 'new_v_mla_kv_norm_g', 'new_v_mla_w_ukv', 'new_v_mla_w_o', 'new_v_w_out', 'new_v_ffn_w_up', 'new_v_ffn_dw_w', 'new_v_ffn_dw_b', 'new_v_ffn_w_down']
TWIN_LEAF_KINDS = {'loss': 'loss', 'grad_x': 'grad_x', 'grad_mix_pre_g': 'grad_w', 'grad_mix_post_g': 'grad_w', 'grad_ffn_pre_g': 'grad_w', 'grad_ffn_post_g': 'grad_w', 'grad_w_in': 'grad_w', 'grad_conv_dw_w': 'grad_w', 'grad_conv_dw_b': 'grad_w', 'grad_conv_ln_g': 'grad_w', 'grad_conv_ln_b': 'grad_w', 'grad_conv_out_w': 'grad_w', 'grad_sg_ln_g': 'grad_w', 'grad_sg_ln_b': 'grad_w', 'grad_sg_w': 'grad_w', 'grad_sg_b': 'grad_w', 'grad_sg_out_w': 'grad_w', 'grad_mla_q_norm_g': 'grad_w', 'grad_mla_w_uq': 'grad_w', 'grad_mla_kv_norm_g': 'grad_w', 'grad_mla_w_ukv': 'grad_w', 'grad_mla_w_o': 'grad_w', 'grad_w_out': 'grad_w', 'grad_ffn_w_up': 'grad_w', 'grad_ffn_dw_w': 'grad_w', 'grad_ffn_dw_b': 'grad_w', 'grad_ffn_w_down': 'grad_w', 'delta_mix_pre_g': 'delta_w', 'delta_mix_post_g': 'delta_w', 'delta_ffn_pre_g': 'delta_w', 'delta_ffn_post_g': 'delta_w', 'delta_w_in': 'delta_w', 'delta_conv_dw_w': 'delta_w', 'delta_conv_dw_b': 'delta_w', 'delta_conv_ln_g': 'delta_w', 'delta_conv_ln_b': 'delta_w', 'delta_conv_out_w': 'delta_w', 'delta_sg_ln_g': 'delta_w', 'delta_sg_ln_b': 'delta_w', 'delta_sg_w': 'delta_w', 'delta_sg_b': 'delta_w', 'delta_sg_out_w': 'delta_w', 'delta_mla_q_norm_g': 'delta_w', 'delta_mla_w_uq': 'delta_w', 'delta_mla_kv_norm_g': 'delta_w', 'delta_mla_w_ukv': 'delta_w', 'delta_mla_w_o': 'delta_w', 'delta_w_out': 'delta_w', 'delta_ffn_w_up': 'delta_w', 'delta_ffn_dw_w': 'delta_w', 'delta_ffn_dw_b': 'delta_w', 'delta_ffn_w_down': 'delta_w', 'new_m_mix_pre_g': 'new_m', 'new_m_mix_post_g': 'new_m', 'new_m_ffn_pre_g': 'new_m', 'new_m_ffn_post_g': 'new_m', 'new_m_w_in': 'new_m', 'new_m_conv_dw_w': 'new_m', 'new_m_conv_dw_b': 'new_m', 'new_m_conv_ln_g': 'new_m', 'new_m_conv_ln_b': 'new_m', 'new_m_conv_out_w': 'new_m', 'new_m_sg_ln_g': 'new_m', 'new_m_sg_ln_b': 'new_m', 'new_m_sg_w': 'new_m', 'new_m_sg_b': 'new_m', 'new_m_sg_out_w': 'new_m', 'new_m_mla_q_norm_g': 'new_m', 'new_m_mla_w_uq': 'new_m', 'new_m_mla_kv_norm_g': 'new_m', 'new_m_mla_w_ukv': 'new_m', 'new_m_mla_w_o': 'new_m', 'new_m_w_out': 'new_m', 'new_m_ffn_w_up': 'new_m', 'new_m_ffn_dw_w': 'new_m', 'new_m_ffn_dw_b': 'new_m', 'new_m_ffn_w_down': 'new_m', 'new_v_mix_pre_g': 'new_v', 'new_v_mix_post_g': 'new_v', 'new_v_ffn_pre_g': 'new_v', 'new_v_ffn_post_g': 'new_v', 'new_v_w_in': 'new_v', 'new_v_conv_dw_w': 'new_v', 'new_v_conv_dw_b': 'new_v', 'new_v_conv_ln_g': 'new_v', 'new_v_conv_ln_b': 'new_v', 'new_v_conv_out_w': 'new_v', 'new_v_sg_ln_g': 'new_v', 'new_v_sg_ln_b': 'new_v', 'new_v_sg_w': 'new_v', 'new_v_sg_b': 'new_v', 'new_v_sg_out_w': 'new_v', 'new_v_mla_q_norm_g': 'new_v', 'new_v_mla_w_uq': 'new_v', 'new_v_mla_kv_norm_g': 'new_v', 'new_v_mla_w_ukv': 'new_v', 'new_v_mla_w_o': 'new_v', 'new_v_w_out': 'new_v', 'new_v_ffn_w_up': 'new_v', 'new_v_ffn_dw_w': 'new_v', 'new_v_ffn_dw_b': 'new_v', 'new_v_ffn_w_down': 'new_v'}


def _forward(args):
    return _fwd_reference(*[args[k] for k in FWD_PARAMS])


def _output_shape():
    out = _jax.eval_shape(lambda: _forward(_fwd_setup_inputs(0)))
    return out.shape, out.dtype

N_MICROBATCH = 1
ADAM_LR = 0.001
ADAM_B1 = 0.9
ADAM_B2 = 0.999
ADAM_EPS = 1e-08
ADAM_WD = 0.01
ADAM_STEP = 10
PER_EXAMPLE_BATCH_AXIS = {'x': 0, 'positions': 0, 'loss_target': 0}
SHARED_INPUTS = []
_WEIGHT_DTYPES = {'mix_pre_g': _jnp.float32, 'mix_post_g': _jnp.float32, 'ffn_pre_g': _jnp.float32, 'ffn_post_g': _jnp.float32, 'w_in': _jnp.float32, 'conv_dw_w': _jnp.float32, 'conv_dw_b': _jnp.float32, 'conv_ln_g': _jnp.float32, 'conv_ln_b': _jnp.float32, 'conv_out_w': _jnp.float32, 'sg_ln_g': _jnp.float32, 'sg_ln_b': _jnp.float32, 'sg_w': _jnp.float32, 'sg_b': _jnp.float32, 'sg_out_w': _jnp.float32, 'mla_q_norm_g': _jnp.float32, 'mla_w_uq': _jnp.float32, 'mla_kv_norm_g': _jnp.float32, 'mla_w_ukv': _jnp.float32, 'mla_w_o': _jnp.float32, 'w_out': _jnp.float32, 'ffn_w_up': _jnp.float32, 'ffn_dw_w': _jnp.float32, 'ffn_dw_b': _jnp.float32, 'ffn_w_down': _jnp.float32}
MOMENT_SCALE = {'mix_pre_g': 9.986532e-01, 'mix_post_g': 1.676239e+01, 'ffn_pre_g': 1.111808e+00, 'ffn_post_g': 1.611425e+01, 'w_in': 4.266209e-01, 'conv_dw_w': 7.607947e-01, 'conv_dw_b': 9.283850e+00, 'conv_ln_g': 3.569985e+00, 'conv_ln_b': 5.281458e+00, 'conv_out_w': 1.537227e+00, 'sg_ln_g': 3.402240e-01, 'sg_ln_b': 3.653074e-01, 'sg_w': 3.213757e-01, 'sg_b': 4.600618e-01, 'sg_out_w': 2.044580e+00, 'mla_q_norm_g': 1.483843e-01, 'mla_w_uq': 1.072851e-01, 'mla_kv_norm_g': 1.422504e+00, 'mla_w_ukv': 7.477946e-01, 'mla_w_o': 7.606520e-01, 'w_out': 2.633415e+00, 'ffn_w_up': 4.680195e-01, 'ffn_dw_w': 5.299793e-01, 'ffn_dw_b': 2.024957e+00, 'ffn_w_down': 9.889485e-01}


def _to_microbatches(a, axis):
    t = _jnp.moveaxis(a, axis, 0)
    t = t.reshape((N_MICROBATCH, t.shape[0] // N_MICROBATCH) + t.shape[1:])
    return _jnp.moveaxis(t, 1, axis + 1)


def setup_inputs(seed: int = 0) -> dict:
    inp = _fwd_setup_inputs(seed)
    key = _jax.random.fold_in(_jax.random.key(seed), 7919)
    shape, _ = _output_shape()
    out = dict(inp)
    out["loss_target"] = _jax.random.normal(_jax.random.fold_in(key, 0), shape, _jnp.float32)
    for i, name in enumerate(TWIN_WEIGHTS):
        w = inp[name].astype(_jnp.float32)
        if MOMENT_SCALE is None:
            s = _jnp.sqrt(_jnp.mean(_jnp.square(w)) + 1e-30)
        else:
            s = MOMENT_SCALE[name]
        km, kv = _jax.random.split(_jax.random.fold_in(key, i + 1))
        out[name] = w
        out["m_" + name] = s * _jax.random.normal(km, w.shape, _jnp.float32)
        out["v_" + name] = (s * s) * _jax.random.uniform(kv, w.shape, _jnp.float32, 0.5, 1.5)
    if N_MICROBATCH > 1:
        for name, axis in PER_EXAMPLE_BATCH_AXIS.items():
            out[name] = _to_microbatches(out[name], axis)
    return {'x': out['x'], 'positions': out['positions'], 'mix_pre_g': out['mix_pre_g'], 'mix_post_g': out['mix_post_g'], 'ffn_pre_g': out['ffn_pre_g'], 'ffn_post_g': out['ffn_post_g'], 'w_in': out['w_in'], 'conv_dw_w': out['conv_dw_w'], 'conv_dw_b': out['conv_dw_b'], 'conv_ln_g': out['conv_ln_g'], 'conv_ln_b': out['conv_ln_b'], 'conv_out_w': out['conv_out_w'], 'sg_ln_g': out['sg_ln_g'], 'sg_ln_b': out['sg_ln_b'], 'sg_w': out['sg_w'], 'sg_b': out['sg_b'], 'sg_out_w': out['sg_out_w'], 'mla_q_norm_g': out['mla_q_norm_g'], 'mla_w_uq': out['mla_w_uq'], 'mla_kv_norm_g': out['mla_kv_norm_g'], 'mla_w_ukv': out['mla_w_ukv'], 'mla_w_o': out['mla_w_o'], 'w_out': out['w_out'], 'ffn_w_up': out['ffn_w_up'], 'ffn_dw_w': out['ffn_dw_w'], 'ffn_dw_b': out['ffn_dw_b'], 'ffn_w_down': out['ffn_w_down'], 'loss_target': out['loss_target'], 'm_mix_pre_g': out['m_mix_pre_g'], 'm_mix_post_g': out['m_mix_post_g'], 'm_ffn_pre_g': out['m_ffn_pre_g'], 'm_ffn_post_g': out['m_ffn_post_g'], 'm_w_in': out['m_w_in'], 'm_conv_dw_w': out['m_conv_dw_w'], 'm_conv_dw_b': out['m_conv_dw_b'], 'm_conv_ln_g': out['m_conv_ln_g'], 'm_conv_ln_b': out['m_conv_ln_b'], 'm_conv_out_w': out['m_conv_out_w'], 'm_sg_ln_g': out['m_sg_ln_g'], 'm_sg_ln_b': out['m_sg_ln_b'], 'm_sg_w': out['m_sg_w'], 'm_sg_b': out['m_sg_b'], 'm_sg_out_w': out['m_sg_out_w'], 'm_mla_q_norm_g': out['m_mla_q_norm_g'], 'm_mla_w_uq': out['m_mla_w_uq'], 'm_mla_kv_norm_g': out['m_mla_kv_norm_g'], 'm_mla_w_ukv': out['m_mla_w_ukv'], 'm_mla_w_o': out['m_mla_w_o'], 'm_w_out': out['m_w_out'], 'm_ffn_w_up': out['m_ffn_w_up'], 'm_ffn_dw_w': out['m_ffn_dw_w'], 'm_ffn_dw_b': out['m_ffn_dw_b'], 'm_ffn_w_down': out['m_ffn_w_down'], 'v_mix_pre_g': out['v_mix_pre_g'], 'v_mix_post_g': out['v_mix_post_g'], 'v_ffn_pre_g': out['v_ffn_pre_g'], 'v_ffn_post_g': out['v_ffn_post_g'], 'v_w_in': out['v_w_in'], 'v_conv_dw_w': out['v_conv_dw_w'], 'v_conv_dw_b': out['v_conv_dw_b'], 'v_conv_ln_g': out['v_conv_ln_g'], 'v_conv_ln_b': out['v_conv_ln_b'], 'v_conv_out_w': out['v_conv_out_w'], 'v_sg_ln_g': out['v_sg_ln_g'], 'v_sg_ln_b': out['v_sg_ln_b'], 'v_sg_w': out['v_sg_w'], 'v_sg_b': out['v_sg_b'], 'v_sg_out_w': out['v_sg_out_w'], 'v_mla_q_norm_g': out['v_mla_q_norm_g'], 'v_mla_w_uq': out['v_mla_w_uq'], 'v_mla_kv_norm_g': out['v_mla_kv_norm_g'], 'v_mla_w_ukv': out['v_mla_w_ukv'], 'v_mla_w_o': out['v_mla_w_o'], 'v_w_out': out['v_w_out'], 'v_ffn_w_up': out['v_ffn_w_up'], 'v_ffn_dw_w': out['v_ffn_dw_w'], 'v_ffn_dw_b': out['v_ffn_dw_b'], 'v_ffn_w_down': out['v_ffn_w_down']}


def _loss(weights, diff, rest, loss_target):
    with _jax.named_scope("forward"):
        args = {**rest, TWIN_DIFF_INPUT: diff, **{k: w.astype(_WEIGHT_DTYPES[k]) for k, w in weights.items()}}
        y = _forward(args)
    with _jax.named_scope("loss_head"):
        err = _jnp.square(y.astype(_jnp.float32) - loss_target)
        return 0.5 * _jnp.sum(_jnp.mean(err, axis=-1)) if err.ndim else 0.5 * err


def _adamw(w, g, m, v):
    m = ADAM_B1 * m + (1.0 - ADAM_B1) * g
    v = ADAM_B2 * v + (1.0 - ADAM_B2) * _jnp.square(g)
    m_hat = m / (1.0 - ADAM_B1 ** ADAM_STEP)
    v_hat = v / (1.0 - ADAM_B2 ** ADAM_STEP)
    delta = -ADAM_LR * (m_hat / (_jnp.sqrt(v_hat) + ADAM_EPS) + ADAM_WD * w)
    return delta, m, v


def reference(x, positions, mix_pre_g, mix_post_g, ffn_pre_g, ffn_post_g, w_in, conv_dw_w, conv_dw_b, conv_ln_g, conv_ln_b, conv_out_w, sg_ln_g, sg_ln_b, sg_w, sg_b, sg_out_w, mla_q_norm_g, mla_w_uq, mla_kv_norm_g, mla_w_ukv, mla_w_o, w_out, ffn_w_up, ffn_dw_w, ffn_dw_b, ffn_w_down, loss_target, m_mix_pre_g, m_mix_post_g, m_ffn_pre_g, m_ffn_post_g, m_w_in, m_conv_dw_w, m_conv_dw_b, m_conv_ln_g, m_conv_ln_b, m_conv_out_w, m_sg_ln_g, m_sg_ln_b, m_sg_w, m_sg_b, m_sg_out_w, m_mla_q_norm_g, m_mla_w_uq, m_mla_kv_norm_g, m_mla_w_ukv, m_mla_w_o, m_w_out, m_ffn_w_up, m_ffn_dw_w, m_ffn_dw_b, m_ffn_w_down, v_mix_pre_g, v_mix_post_g, v_ffn_pre_g, v_ffn_post_g, v_w_in, v_conv_dw_w, v_conv_dw_b, v_conv_ln_g, v_conv_ln_b, v_conv_out_w, v_sg_ln_g, v_sg_ln_b, v_sg_w, v_sg_b, v_sg_out_w, v_mla_q_norm_g, v_mla_w_uq, v_mla_kv_norm_g, v_mla_w_ukv, v_mla_w_o, v_w_out, v_ffn_w_up, v_ffn_dw_w, v_ffn_dw_b, v_ffn_w_down):
    given = dict(x=x, positions=positions, mix_pre_g=mix_pre_g, mix_post_g=mix_post_g, ffn_pre_g=ffn_pre_g, ffn_post_g=ffn_post_g, w_in=w_in, conv_dw_w=conv_dw_w, conv_dw_b=conv_dw_b, conv_ln_g=conv_ln_g, conv_ln_b=conv_ln_b, conv_out_w=conv_out_w, sg_ln_g=sg_ln_g, sg_ln_b=sg_ln_b, sg_w=sg_w, sg_b=sg_b, sg_out_w=sg_out_w, mla_q_norm_g=mla_q_norm_g, mla_w_uq=mla_w_uq, mla_kv_norm_g=mla_kv_norm_g, mla_w_ukv=mla_w_ukv, mla_w_o=mla_w_o, w_out=w_out, ffn_w_up=ffn_w_up, ffn_dw_w=ffn_dw_w, ffn_dw_b=ffn_dw_b, ffn_w_down=ffn_w_down, loss_target=loss_target, m_mix_pre_g=m_mix_pre_g, m_mix_post_g=m_mix_post_g, m_ffn_pre_g=m_ffn_pre_g, m_ffn_post_g=m_ffn_post_g, m_w_in=m_w_in, m_conv_dw_w=m_conv_dw_w, m_conv_dw_b=m_conv_dw_b, m_conv_ln_g=m_conv_ln_g, m_conv_ln_b=m_conv_ln_b, m_conv_out_w=m_conv_out_w, m_sg_ln_g=m_sg_ln_g, m_sg_ln_b=m_sg_ln_b, m_sg_w=m_sg_w, m_sg_b=m_sg_b, m_sg_out_w=m_sg_out_w, m_mla_q_norm_g=m_mla_q_norm_g, m_mla_w_uq=m_mla_w_uq, m_mla_kv_norm_g=m_mla_kv_norm_g, m_mla_w_ukv=m_mla_w_ukv, m_mla_w_o=m_mla_w_o, m_w_out=m_w_out, m_ffn_w_up=m_ffn_w_up, m_ffn_dw_w=m_ffn_dw_w, m_ffn_dw_b=m_ffn_dw_b, m_ffn_w_down=m_ffn_w_down, v_mix_pre_g=v_mix_pre_g, v_mix_post_g=v_mix_post_g, v_ffn_pre_g=v_ffn_pre_g, v_ffn_post_g=v_ffn_post_g, v_w_in=v_w_in, v_conv_dw_w=v_conv_dw_w, v_conv_dw_b=v_conv_dw_b, v_conv_ln_g=v_conv_ln_g, v_conv_ln_b=v_conv_ln_b, v_conv_out_w=v_conv_out_w, v_sg_ln_g=v_sg_ln_g, v_sg_ln_b=v_sg_ln_b, v_sg_w=v_sg_w, v_sg_b=v_sg_b, v_sg_out_w=v_sg_out_w, v_mla_q_norm_g=v_mla_q_norm_g, v_mla_w_uq=v_mla_w_uq, v_mla_kv_norm_g=v_mla_kv_norm_g, v_mla_w_ukv=v_mla_w_ukv, v_mla_w_o=v_mla_w_o, v_w_out=v_w_out, v_ffn_w_up=v_ffn_w_up, v_ffn_dw_w=v_ffn_dw_w, v_ffn_dw_b=v_ffn_dw_b, v_ffn_w_down=v_ffn_w_down)
    weights = {n: given[n] for n in TWIN_WEIGHTS}
    shared = {n: given[n] for n in SHARED_INPUTS}
    per_example = {n: given[n] for n in ['x', 'positions']}
    grad_fn = _jax.value_and_grad(_loss, argnums=(0, 1))

    def one_microbatch(ex, loss_target):
        ex = dict(ex)
        diff = ex.pop(TWIN_DIFF_INPUT)
        return grad_fn(weights, diff, {**shared, **ex}, loss_target)

    if N_MICROBATCH == 1:
        loss, (grad_w, grad_x) = one_microbatch(per_example, given["loss_target"])
    else:
        def body(carry, xs):
            loss_sum, grad_sum = carry
            l_k, (gw_k, gx_k) = one_microbatch(xs[0], xs[1])
            with _jax.named_scope("update"):
                return (loss_sum + l_k, _jax.tree.map(_jnp.add, grad_sum, gw_k)), gx_k

        init = (_jnp.zeros((), _jnp.float32), _jax.tree.map(_jnp.zeros_like, weights))
        (loss, grad_w), grad_x = _jax.lax.scan(body, init, (per_example, given["loss_target"]))
    with _jax.named_scope("update"):
        delta_w, new_m, new_v = {}, {}, {}
        for n in TWIN_WEIGHTS:
            delta_w[n], new_m[n], new_v[n] = _adamw(weights[n], grad_w[n], given["m_" + n], given["v_" + n])
    return (loss, grad_x, *[grad_w[n] for n in TWIN_WEIGHTS], *[delta_w[n] for n in TWIN_WEIGHTS],
            *[new_m[n] for n in TWIN_WEIGHTS], *[new_v[n] for n in TWIN_WEIGHTS])
```

```python
import math

import jax
import jax.numpy as jnp
from jax import lax
from jax.experimental import pallas as pl
from jax.experimental.pallas import tpu as pltpu

f32 = jnp.float32
bf16 = jnp.bfloat16

N_DEV = 8
AXES = ("x", "y", "c")
EPS = 1e-6
D_CONV = 512
CONV_K = 31
CONV_HALO = 32
D_SG = 512
SG_GROUPS = 4
SG_CHUNK = 128
N_HEADS = 8
QK_NOPE = 64
QK_ROPE = 32
V_HEAD = 64
HEAD_PAD = 128
Q_LORA = 384
KV_LORA = 256
D_FF = 2816
FFN_K = 3
FFN_HALO = 8
ATT_SCALE = (QK_NOPE + QK_ROPE) ** -0.5
NEG = float(jnp.finfo(jnp.float32).min)

ADAM_LR = 0.001
ADAM_B1 = 0.9
ADAM_B2 = 0.999
ADAM_EPS = 1e-08
ADAM_WD = 0.01
ADAM_STEP = 10

LANES = 128
VMEM_MB = 56

REF_CUTS = (0, 1024, 2048, 2432, 2688, 2720, 5792)
SEGS = ((0, 1024), (1024, 1024), (2048, 384), (2432, 256), (2688, 128), (2816, 3072))
D_IN = 5792
D_IN_PAD = 5888
KR_LO = 64
SEG_INNER = (0, 0, 0, 0, KR_LO, 0)

IN_NAMES = ['x', 'positions', 'mix_pre_g', 'mix_post_g', 'ffn_pre_g', 'ffn_post_g', 'w_in', 'conv_dw_w', 'conv_dw_b', 'conv_ln_g', 'conv_ln_b', 'conv_out_w', 'sg_ln_g', 'sg_ln_b', 'sg_w', 'sg_b', 'sg_out_w', 'mla_q_norm_g', 'mla_w_uq', 'mla_kv_norm_g', 'mla_w_ukv', 'mla_w_o', 'w_out', 'ffn_w_up', 'ffn_dw_w', 'ffn_dw_b', 'ffn_w_down']
WEIGHTS = IN_NAMES[2:]
SHARDED = ("w_in", "conv_dw_w", "conv_out_w", "sg_out_w", "mla_w_uq", "mla_w_ukv", "mla_w_o", "w_out", "ffn_w_up", "ffn_dw_w",
           "ffn_w_down")
WIRE_F32 = ("conv_dw_w", "ffn_dw_w")
REP_VECS = (("mix_pre_g", 0, 1024, 1), ("mix_post_g", 1, 1024, 1), ("ffn_pre_g", 2, 1024, 1), ("ffn_post_g", 3, 1024, 1),
            ("conv_dw_b", 4, 512, 1), ("conv_ln_g", 5, 512, 1), ("conv_ln_b", 6, 512, 1), ("sg_ln_g", 7, 512, 1),
            ("sg_ln_b", 8, 512, 1), ("mla_q_norm_g", 9, 384, 1), ("mla_kv_norm_g", 10, 256, 1), ("ffn_dw_b", 11, 5632, 1),
            ("sg_b", 12, 128, 4))
REP_ROWS = 16
REP_W = 5632


def _cparams(n_axes):
    return pltpu.CompilerParams(dimension_semantics=("arbitrary",) * n_axes, vmem_limit_bytes=VMEM_MB * 2 ** 20)


def _rows(tb, n):
    return pl.BlockSpec((tb, n), lambda i: (i, 0))


def _rows_rev(tb, n, nb):
    return pl.BlockSpec((tb, n), lambda i: (nb - 1 - i, 0))


def _const(shape):
    nd = len(shape)
    return pl.BlockSpec(shape, lambda *_: (0,) * nd, pipeline_mode=pl.Buffered(1))


def _acc(shape):
    nd = len(shape)
    return pl.BlockSpec(shape, lambda *_: (0,) * nd)


def _whole(arr):
    return pl.BlockSpec(arr.shape, lambda *_: (0,) * arr.ndim)


def _dot(a, b):
    return jnp.dot(a, b, preferred_element_type=f32)


def _dot_nt(a, b):
    return lax.dot_general(a, b, (((1,), (1,)), ((), ())), preferred_element_type=f32)


def _dot_tn(a, b):
    return lax.dot_general(a, b, (((0,), (0,)), ((), ())), preferred_element_type=f32)


def _mean(x):
    return jnp.mean(x, axis=-1, keepdims=True)


def _colsum(x):
    return jnp.sum(x, axis=0, keepdims=True)


def _rms_fwd(x, g):
    return x * lax.rsqrt(_mean(x * x) + EPS) * g


def _rms_bwd(x, g, dy):
    r = lax.rsqrt(_mean(x * x) + EPS)
    n = x * r
    dn = dy * g
    return r * (dn - n * _mean(dn * n)), _colsum(dy * n)


def _ln_stats(x):
    mu = _mean(x)
    d = x - mu
    rstd = lax.rsqrt(_mean(d * d) + EPS)
    return d * rstd, rstd


def _ln_bwd(xhat, rstd, g, dy):
    dxh = dy * g
    return rstd * (dxh - _mean(dxh) - xhat * _mean(dxh * xhat))


_GELU_C0 = math.sqrt(2.0 / math.pi)
_GELU_C1 = 0.044715


def _gelu(x):
    t = jnp.tanh(_GELU_C0 * (x + _GELU_C1 * (x * x * x)))
    return 0.5 * x * (1.0 + t)


def _gelu_and_grad(x):
    x2 = x * x
    t = jnp.tanh(_GELU_C0 * (x + _GELU_C1 * (x2 * x)))
    g = 0.5 * x * (1.0 + t)
    dg = 0.5 * (1.0 + t) + 0.5 * x * (1.0 - t * t) * (_GELU_C0 * (1.0 + 3.0 * _GELU_C1 * x2))
    return g, dg


def _swap_rope_halves(x):
    n = x.shape[1]
    half = QK_ROPE // 2
    lane = lax.broadcasted_iota(jnp.int32, x.shape, 1) % HEAD_PAD
    first = (lane >= KR_LO) & (lane < KR_LO + half)
    second = (lane >= KR_LO + half) & (lane < KR_LO + QK_ROPE)
    return jnp.where(first, pltpu.roll(x, n - half, 1), jnp.where(second, pltpu.roll(x, half, 1), 0.0))


def _tb(t):
    return min(256, t)


def _exchange(ops, name):
    n = len(ops)
    out_shape = []
    for arr, mode, lead in ops:
        if mode == "gather":
            shp = (N_DEV,) + (arr.shape if lead is None else arr.shape[1:])
        else:
            shp = arr.shape
        out_shape.append(jax.ShapeDtypeStruct(shp, arr.dtype))

    def body(*refs):
        ins, outs = refs[:n], refs[n:2 * n]
        send_sems, recv_sems, loc_sems = refs[2 * n:]
        x, y, c = lax.axis_index("x"), lax.axis_index("y"), lax.axis_index("c")
        me = 4 * x + 2 * y + c
        copies = []
        for a, (_, mode, lead) in enumerate(ops):
            def src(dst_index, a=a, mode=mode, lead=lead):
                if mode == "scatter":
                    return ins[a].at[dst_index]
                return ins[a] if lead is None else ins[a].at[lead]
            loc = pltpu.make_async_copy(src(me), outs[a].at[me], loc_sems.at[a])
            loc.start()
            copies.append(loc)
            for k in range(1, N_DEV):
                px = 1 - x if (k >> 2) & 1 else x
                py = 1 - y if (k >> 1) & 1 else y
                pc = 1 - c if k & 1 else c
                cp = pltpu.make_async_remote_copy(
                    src_ref=src(4 * px + 2 * py + pc), dst_ref=outs[a].at[me],
                    send_sem=send_sems.at[a * (N_DEV - 1) + k - 1], recv_sem=recv_sems.at[a * (N_DEV - 1) + k - 1],
                    device_id=(px, py, pc), device_id_type=pl.DeviceIdType.MESH)
                cp.start()
                copies.append(cp)
        for cp in copies:
            cp.wait()

    any_spec = pl.BlockSpec(memory_space=pl.ANY)
    return pl.pallas_call(
        body, name=name, out_shape=tuple(out_shape),
        in_specs=[any_spec] * n, out_specs=tuple([any_spec] * n),
        scratch_shapes=[pltpu.SemaphoreType.DMA((n * (N_DEV - 1),)), pltpu.SemaphoreType.DMA((n * (N_DEV - 1),)),
                        pltpu.SemaphoreType.DMA((n,))],
        compiler_params=pltpu.CompilerParams(has_side_effects=True),
    )(*[arr for arr, _, _ in ops])


def _w_in_pieces(ns):
    out = []
    for e in range(N_DEV):
        lo, hi = ns * e, ns * (e + 1)
        for s in range(len(SEGS)):
            a, b = max(lo, REF_CUTS[s]), min(hi, REF_CUTS[s + 1])
            if a < b:
                inner = SEG_INNER[s] + a - REF_CUTS[s]
                out.append((e, a - lo, b - lo, s, inner, inner + b - a))
    return out


def _asm_w_in(g):
    _, d, ns = g.shape
    rb = 256
    pieces = _w_in_pieces(ns)

    def body(g_ref, o_ref):
        kr = SEGS[4][0]
        o_ref[:, kr:kr + KR_LO] = jnp.zeros((rb, KR_LO), g.dtype)
        o_ref[:, kr + KR_LO + QK_ROPE:kr + HEAD_PAD] = jnp.zeros((rb, HEAD_PAD - KR_LO - QK_ROPE), g.dtype)
        for e, s0, s1, seg, d0, d1 in pieces:
            off = SEGS[seg][0]
            o_ref[:, off + d0:off + d1] = g_ref[e, :, s0:s1]

    return pl.pallas_call(
        body, name="asm_w_in", grid=(d // rb,),
        in_specs=[pl.BlockSpec((N_DEV, rb, ns), lambda i: (0, i, 0))],
        out_specs=_rows(rb, D_IN_PAD), out_shape=jax.ShapeDtypeStruct((d, D_IN_PAD), g.dtype),
        compiler_params=_cparams(1),
    )(g)


def _dis_w_in(dsegs, ns):
    d = dsegs[0].shape[0]
    rb = 256
    pieces = _w_in_pieces(ns)

    def body(*refs):
        seg_refs, o_ref = refs[:len(SEGS)], refs[len(SEGS)]
        for e, s0, s1, seg, d0, d1 in pieces:
            o_ref[e, :, s0:s1] = seg_refs[seg][:, d0:d1].astype(bf16)

    return pl.pallas_call(
        body, name="dis_w_in", grid=(d // rb,),
        in_specs=[_rows(rb, wd) for _, wd in SEGS],
        out_specs=pl.BlockSpec((N_DEV, rb, ns), lambda i: (0, i, 0)),
        out_shape=jax.ShapeDtypeStruct((N_DEV, d, ns), bf16),
        compiler_params=_cparams(1),
    )(*dsegs)


def _asm_cols(g, name):
    _, k, ns = g.shape
    rb = 256

    def body(g_ref, o_ref):
        for e in range(N_DEV):
            o_ref[:, e * ns:(e + 1) * ns] = g_ref[e]

    return pl.pallas_call(
        body, name=name, grid=(k // rb,),
        in_specs=[pl.BlockSpec((N_DEV, rb, ns), lambda i: (0, i, 0))],
        out_specs=_rows(rb, N_DEV * ns), out_shape=jax.ShapeDtypeStruct((k, N_DEV * ns), g.dtype),
        compiler_params=_cparams(1),
    )(g)


def _dis_cols(w, name):
    k, n = w.shape
    ns = n // N_DEV
    rb = 256

    def body(w_ref, o_ref):
        for e in range(N_DEV):
            o_ref[e] = w_ref[:, e * ns:(e + 1) * ns].astype(bf16)

    return pl.pallas_call(
        body, name=name, grid=(k // rb,),
        in_specs=[_rows(rb, n)],
        out_specs=pl.BlockSpec((N_DEV, rb, ns), lambda i: (0, i, 0)),
        out_shape=jax.ShapeDtypeStruct((N_DEV, k, ns), bf16),
        compiler_params=_cparams(1),
    )(w)


def _asm_small(g_conv_out, g_sg_out, g_uq, g_ukv, g_wo, g_cdw, g_fdw):
    d = g_conv_out.shape[2] * N_DEV
    hw = N_HEADS * HEAD_PAD
    hq = QK_NOPE + QK_ROPE
    ff2 = g_fdw.shape[2] * N_DEV
    cw, fw = g_cdw.shape[2], g_fdw.shape[2]

    def body(co_ref, so_ref, uq_ref, ukv_ref, wo_ref, cdw_ref, fdw_ref, o_co, o_so, o_uq, o_k, o_v, o_wo, o_cdw, o_fdw):
        o_cdw[CONV_K:CONV_HALO, :] = jnp.zeros((CONV_HALO - CONV_K, D_CONV), f32)
        o_fdw[FFN_K:FFN_HALO, :] = jnp.zeros((FFN_HALO - FFN_K, ff2), f32)
        for e in range(N_DEV):
            cs = e * HEAD_PAD
            o_co[:, cs:cs + HEAD_PAD] = co_ref[e]
            o_so[:, cs:cs + HEAD_PAD] = so_ref[e]
            o_uq[:, cs:cs + hq] = uq_ref[e]
            o_uq[:, cs + hq:cs + HEAD_PAD] = jnp.zeros((Q_LORA, HEAD_PAD - hq), bf16)
            o_k[:, cs:cs + QK_NOPE] = ukv_ref[e, :, 0:QK_NOPE]
            o_k[:, cs + QK_NOPE:cs + HEAD_PAD] = jnp.zeros((KV_LORA, HEAD_PAD - QK_NOPE), bf16)
            o_v[:, cs:cs + V_HEAD] = ukv_ref[e, :, QK_NOPE:QK_NOPE + V_HEAD]
            o_v[:, cs + V_HEAD:cs + HEAD_PAD] = jnp.zeros((KV_LORA, HEAD_PAD - V_HEAD), bf16)
            for h in range(N_HEADS):
                o_wo[h * HEAD_PAD:h * HEAD_PAD + V_HEAD, cs:cs + HEAD_PAD] = wo_ref[e, h * V_HEAD:(h + 1) * V_HEAD, :]
                o_wo[h * HEAD_PAD + V_HEAD:(h + 1) * HEAD_PAD, cs:cs + HEAD_PAD] = jnp.zeros((HEAD_PAD - V_HEAD, HEAD_PAD), bf16)
            o_cdw[0:CONV_K, e * cw:(e + 1) * cw] = cdw_ref[e]
            o_fdw[0:FFN_K, e * fw:(e + 1) * fw] = fdw_ref[e]

    ins = (g_conv_out, g_sg_out, g_uq, g_ukv, g_wo, g_cdw, g_fdw)
    out_shape = (jax.ShapeDtypeStruct((D_CONV, d), bf16), jax.ShapeDtypeStruct((D_SG, d), bf16), jax.ShapeDtypeStruct((Q_LORA, hw), bf16),
                 jax.ShapeDtypeStruct((KV_LORA, hw), bf16), jax.ShapeDtypeStruct((KV_LORA, hw), bf16), jax.ShapeDtypeStruct((hw, d), bf16),
                 jax.ShapeDtypeStruct((CONV_HALO, D_CONV), f32), jax.ShapeDtypeStruct((FFN_HALO, ff2), f32))
    return pl.pallas_call(
        body, name="asm_small", grid=(1,),
        in_specs=[_whole(a) for a in ins], out_specs=tuple(_whole(o) for o in out_shape), out_shape=out_shape,
        compiler_params=_cparams(1),
    )(*ins)


def _dis_small(d_co, d_so, d_uq, d_k, d_v, d_wo, d_cdw, d_fdw):
    d = d_co.shape[1]
    hq = QK_NOPE + QK_ROPE
    cw, fw = D_CONV // N_DEV, d_fdw.shape[1] // N_DEV

    def body(co_ref, so_ref, uq_ref, k_ref, v_ref, wo_ref, cdw_ref, fdw_ref, o_co, o_so, o_uq, o_ukv, o_wo, o_cdw, o_fdw):
        for e in range(N_DEV):
            cs = e * HEAD_PAD
            o_co[e] = co_ref[:, cs:cs + HEAD_PAD].astype(bf16)
            o_so[e] = so_ref[:, cs:cs + HEAD_PAD].astype(bf16)
            o_uq[e] = uq_ref[:, cs:cs + hq].astype(bf16)
            o_ukv[e, :, 0:QK_NOPE] = k_ref[:, cs:cs + QK_NOPE].astype(bf16)
            o_ukv[e, :, QK_NOPE:QK_NOPE + V_HEAD] = v_ref[:, cs:cs + V_HEAD].astype(bf16)
            for h in range(N_HEADS):
                o_wo[e, h * V_HEAD:(h + 1) * V_HEAD, :] = wo_ref[h * HEAD_PAD:h * HEAD_PAD + V_HEAD, cs:cs + HEAD_PAD].astype(bf16)
            o_cdw[e] = cdw_ref[0:CONV_K, e * cw:(e + 1) * cw]
            o_fdw[e] = fdw_ref[0:FFN_K, e * fw:(e + 1) * fw]

    ins = (d_co, d_so, d_uq, d_k, d_v, d_wo, d_cdw, d_fdw)
    out_shape = (jax.ShapeDtypeStruct((N_DEV, D_CONV, d // N_DEV), bf16), jax.ShapeDtypeStruct((N_DEV, D_SG, d // N_DEV), bf16),
                 jax.ShapeDtypeStruct((N_DEV, Q_LORA, hq), bf16), jax.ShapeDtypeStruct((N_DEV, KV_LORA, QK_NOPE + V_HEAD), bf16),
                 jax.ShapeDtypeStruct((N_DEV, N_HEADS * V_HEAD, d // N_DEV), bf16), jax.ShapeDtypeStruct((N_DEV, CONV_K, cw), f32),
                 jax.ShapeDtypeStruct((N_DEV, FFN_K, fw), f32))
    return pl.pallas_call(
        body, name="dis_small", grid=(1,),
        in_specs=[_whole(a) for a in ins], out_specs=tuple(_whole(o) for o in out_shape), out_shape=out_shape,
        compiler_params=_cparams(1),
    )(*ins)


def _pack_rep(vec_grads, dsgb):
    def body(*refs):
        o_ref = refs[-1]
        o_ref[...] = jnp.zeros((REP_ROWS, REP_W), f32)
        for (_, row, width, nrows), ref in zip(REP_VECS, refs[:-1]):
            o_ref[row:row + nrows, 0:width] = ref[0:nrows, :]

    ins = tuple(vec_grads) + (dsgb,)
    return pl.pallas_call(
        body, name="pack_rep", grid=(1,),
        in_specs=[_whole(a) for a in ins], out_specs=pl.BlockSpec((REP_ROWS, REP_W), lambda i: (0, 0)),
        out_shape=jax.ShapeDtypeStruct((REP_ROWS, REP_W), f32), compiler_params=_cparams(1),
    )(*ins)


def _mix_in_fwd(l, x, g1, w_in_p):
    t, d = x.shape
    tb = _tb(t)

    def body(x_ref, g_ref, w_ref, h_ref, *outs):
        h = _rms_fwd(x_ref[...], g_ref[l:l + 1, :]).astype(bf16)
        h_ref[...] = h
        for (off, wd), o_ref in zip(SEGS, outs):
            o_ref[...] = _dot(h, w_ref[:, off:off + wd])

    return pl.pallas_call(
        body, name="mix_in_fwd", grid=(t // tb,),
        in_specs=[_rows(tb, d), _const(g1.shape), _const((d, D_IN_PAD))],
        out_specs=tuple([_rows(tb, d)] + [_rows(tb, wd) for _, wd in SEGS]),
        out_shape=tuple([jax.ShapeDtypeStruct((t, d), bf16)] + [jax.ShapeDtypeStruct((t, wd), f32) for _, wd in SEGS]),
        compiler_params=_cparams(1),
    )(x, g1, w_in_p)


def _conv_fwd(l, a_in, dw_w, dw_b, ln_g, ln_b):
    t = a_in.shape[0]
    tb = _tb(t)

    def body(a_ref, w_ref, b_ref, g_ref, be_ref, s_ref, z1_ref, win):
        @pl.when(pl.program_id(0) == 0)
        def _():
            win[0:CONV_HALO, :] = jnp.zeros((CONV_HALO, D_CONV), f32)

        a = a_ref[...]
        win[CONV_HALO:CONV_HALO + tb, :] = a[:, :D_CONV] * jax.nn.sigmoid(a[:, D_CONV:])
        for r0 in range(0, tb, LANES):
            for c0 in range(0, D_CONV, LANES):
                acc = jnp.broadcast_to(b_ref[l:l + 1, c0:c0 + LANES], (LANES, LANES))
                for j in range(CONV_K):
                    acc = acc + w_ref[j:j + 1, c0:c0 + LANES] * win[pl.ds(r0 + CONV_HALO - CONV_K + 1 + j, LANES), c0:c0 + LANES]
                z1_ref[r0:r0 + LANES, c0:c0 + LANES] = acc
        zh, _ = _ln_stats(z1_ref[...])
        zl = zh * g_ref[l:l + 1, :] + be_ref[l:l + 1, :]
        s_ref[...] = (zl * jax.nn.sigmoid(zl)).astype(bf16)
        win[0:CONV_HALO, :] = win[tb:tb + CONV_HALO, :]

    return pl.pallas_call(
        body, name="conv_fwd", grid=(t // tb,),
        in_specs=[_rows(tb, 2 * D_CONV), _const((CONV_HALO, D_CONV)), _const(dw_b.shape), _const(ln_g.shape), _const(ln_b.shape)],
        out_specs=(_rows(tb, D_CONV), _rows(tb, D_CONV)),
        out_shape=(jax.ShapeDtypeStruct((t, D_CONV), bf16), jax.ShapeDtypeStruct((t, D_CONV), f32)),
        scratch_shapes=[pltpu.VMEM((tb + CONV_HALO, D_CONV), f32)],
        compiler_params=_cparams(1),
    )(a_in, dw_w, dw_b, ln_g, ln_b)


def _tril_mask():
    r = lax.broadcasted_iota(jnp.int32, (SG_CHUNK, SG_CHUNK), 0)
    c = lax.broadcasted_iota(jnp.int32, (SG_CHUNK, SG_CHUNK), 1)
    return r >= c


def _sgu_fwd(l, b_in, ln_g, ln_b, sg_w, bexp):
    t = b_in.shape[0]
    tb = _tb(t)
    gw = D_SG // SG_GROUPS

    def body(b_ref, g_ref, be_ref, w_ref, bexp_ref, um_ref):
        gl = _gelu(b_ref[...])
        u = gl[:, :D_SG]
        vh, _ = _ln_stats(gl[:, D_SG:])
        vn = (vh * g_ref[l:l + 1, :] + be_ref[l:l + 1, :]).astype(bf16)
        tri = _tril_mask()
        for g in range(SG_GROUPS):
            wg = jnp.where(tri, w_ref[l, g], 0.0).astype(bf16)
            cs = slice(g * gw, (g + 1) * gw)
            for r0 in range(0, tb, SG_CHUNK):
                rs = slice(r0, r0 + SG_CHUNK)
                mixed = _dot(wg, vn[rs, cs]) + bexp_ref[:, cs]
                um_ref[rs, cs] = (u[rs, cs] * mixed).astype(bf16)

    return pl.pallas_call(
        body, name="sgu_fwd", grid=(t // tb,),
        in_specs=[_rows(tb, 2 * D_SG), _const(ln_g.shape), _const(ln_b.shape), _const(sg_w.shape), _const((SG_CHUNK, D_SG))],
        out_specs=_rows(tb, D_SG),
        out_shape=jax.ShapeDtypeStruct((t, D_SG), bf16),
        compiler_params=_cparams(1),
    )(b_in, ln_g, ln_b, sg_w, bexp)


def _mla_proj_fwd(l, qlat, kvlat, krope, cos_t, sin_t, gq, wuq_p, gkv, wk_p, wv_p):
    t = qlat.shape[0]
    tb = _tb(t)
    hw = N_HEADS * HEAD_PAD

    def body(ql_ref, kvl_ref, kr_ref, c_ref, s_ref, gq_ref, wq_ref, gkv_ref, wk_ref, wv_ref, q_ref, k_ref, v_ref, qn_ref, kvn_ref):
        cos_b, sin_b = c_ref[...], s_ref[...]
        qn = _rms_fwd(ql_ref[...], gq_ref[l:l + 1, :]).astype(bf16)
        qn_ref[...] = qn
        q = _dot(qn, wq_ref[...])
        sw = _swap_rope_halves(q)
        for h in range(N_HEADS):
            hs = slice(h * HEAD_PAD, (h + 1) * HEAD_PAD)
            q_ref[:, hs] = (q[:, hs] * cos_b + sw[:, hs] * sin_b).astype(bf16)
        kvn = _rms_fwd(kvl_ref[...], gkv_ref[l:l + 1, :]).astype(bf16)
        kvn_ref[...] = kvn
        kr = kr_ref[...]
        kpe = kr * cos_b + _swap_rope_halves(kr) * sin_b
        k = _dot(kvn, wk_ref[...])
        for h in range(N_HEADS):
            hs = slice(h * HEAD_PAD, (h + 1) * HEAD_PAD)
            k_ref[:, hs] = (k[:, hs] + kpe).astype(bf16)
        v_ref[...] = _dot(kvn, wv_ref[...]).astype(bf16)

    return pl.pallas_call(
        body, name="mla_proj_fwd", grid=(t // tb,),
        in_specs=[_rows(tb, Q_LORA), _rows(tb, KV_LORA), _rows(tb, HEAD_PAD), _rows(tb, HEAD_PAD), _rows(tb, HEAD_PAD),
                  _const(gq.shape), _const((Q_LORA, hw)), _const(gkv.shape), _const((KV_LORA, hw)), _const((KV_LORA, hw))],
        out_specs=(_rows(tb, hw), _rows(tb, hw), _rows(tb, hw), _rows(tb, Q_LORA), _rows(tb, KV_LORA)),
        out_shape=(jax.ShapeDtypeStruct((t, hw), bf16), jax.ShapeDtypeStruct((t, hw), bf16), jax.ShapeDtypeStruct((t, hw), bf16),
                   jax.ShapeDtypeStruct((t, Q_LORA), bf16), jax.ShapeDtypeStruct((t, KV_LORA), bf16)),
        compiler_params=_cparams(1),
    )(qlat, kvlat, krope, cos_t, sin_t, gq, wuq_p, gkv, wk_p, wv_p)


def _causal_mask(qi, tq, tk):
    row = qi * tq + lax.broadcasted_iota(jnp.int32, (tq, tk), 0)
    col = lax.broadcasted_iota(jnp.int32, (tq, tk), 1)
    return row >= col


def _attn_fwd(q, k, v):
    t = q.shape[0]
    tq = _tb(t)

    def body(q_ref, k_ref, v_ref, o_ref, lse_ref):
        s = _dot_nt(q_ref[...], k_ref[...]) * ATT_SCALE
        s = jnp.where(_causal_mask(pl.program_id(1), tq, t), s, NEG)
        m = jnp.max(s, axis=-1, keepdims=True)
        p = jnp.exp(s - m)
        lsum = jnp.sum(p, axis=-1, keepdims=True)
        o_ref[...] = (_dot(p.astype(bf16), v_ref[...]) / lsum).astype(bf16)
        lse_ref[...] = m + jnp.log(lsum)

    return pl.pallas_call(
        body, name="attn_fwd", grid=(N_HEADS, t // tq),
        in_specs=[pl.BlockSpec((tq, HEAD_PAD), lambda h, i: (i, h)), pl.BlockSpec((t, HEAD_PAD), lambda h, i: (0, h)),
                  pl.BlockSpec((t, HEAD_PAD), lambda h, i: (0, h))],
        out_specs=(pl.BlockSpec((tq, HEAD_PAD), lambda h, i: (i, h)), pl.BlockSpec((None, tq, 1), lambda h, i: (h, i, 0))),
        out_shape=(jax.ShapeDtypeStruct((t, N_HEADS * HEAD_PAD), bf16), jax.ShapeDtypeStruct((N_HEADS, t, 1), f32)),
        compiler_params=_cparams(2),
    )(q, k, v)


def _merge_out_fwd(l, x, s, um, o, gates, conv_out_w, sg_out_w, wo_p, w_out, g2):
    t, d = x.shape
    tb = _tb(t)

    def body(x_ref, s_ref, um_ref, o_ref, gt_ref, wa_ref, wb_ref, wc_ref, wout_ref, g_ref, x1_ref, mg_ref, om_ref):
        merged = (jax.nn.sigmoid(gt_ref[:, 0:d]) * _dot(s_ref[...], wa_ref[...])
                  + jax.nn.sigmoid(gt_ref[:, d:2 * d]) * _dot(um_ref[...], wb_ref[...])
                  + jax.nn.sigmoid(gt_ref[:, 2 * d:3 * d]) * _dot(o_ref[...], wc_ref[...]))
        mb = merged.astype(bf16)
        mg_ref[...] = mb
        om = _dot(mb, wout_ref[...])
        om_ref[...] = om
        x1_ref[...] = x_ref[...] + _rms_fwd(om, g_ref[l:l + 1, :])

    hw = N_HEADS * HEAD_PAD
    return pl.pallas_call(
        body, name="merge_out_fwd", grid=(t // tb,),
        in_specs=[_rows(tb, d), _rows(tb, D_CONV), _rows(tb, D_SG), _rows(tb, hw), _rows(tb, 3 * d),
                  _const((D_CONV, d)), _const((D_SG, d)), _const((hw, d)), _const((d, d)), _const(g2.shape)],
        out_specs=(_rows(tb, d), _rows(tb, d), _rows(tb, d)),
        out_shape=(jax.ShapeDtypeStruct((t, d), f32), jax.ShapeDtypeStruct((t, d), bf16), jax.ShapeDtypeStruct((t, d), f32)),
        compiler_params=_cparams(1),
    )(x, s, um, o, gates, conv_out_w, sg_out_w, wo_p, w_out, g2)


FF_CHUNK = 1408


def _ffn_conv_cols(zbuf, w_ref, b_ref, l, nrows, c0, c1):
    acc = b_ref[l:l + 1, c0:c1] + w_ref[0:1, c0:c1] * zbuf[pl.ds(FFN_HALO - 2, nrows), c0:c1]
    acc = acc + w_ref[1:2, c0:c1] * zbuf[pl.ds(FFN_HALO - 1, nrows), c0:c1]
    return acc + w_ref[2:3, c0:c1] * zbuf[pl.ds(FFN_HALO, nrows), c0:c1]


def _ffn_fwd(l, x1, g3, w_up, dw_w, dw_b, w_down, g4):
    t, d = x1.shape
    tb = _tb(t)
    ff2 = 2 * D_FF

    def body(x_ref, g3_ref, wup_ref, dww_ref, dwb_ref, wdn_ref, g4_ref, x2_ref, h2_ref, z_ref, act_ref, f_ref, zbuf):
        @pl.when(pl.program_id(0) == 0)
        def _():
            zbuf[0:FFN_HALO, :] = jnp.zeros((FFN_HALO, ff2), f32)

        xv = x_ref[...]
        h2 = _rms_fwd(xv, g3_ref[l:l + 1, :]).astype(bf16)
        h2_ref[...] = h2
        for c0 in range(0, ff2, FF_CHUNK):
            zc = _dot(h2, wup_ref[:, c0:c0 + FF_CHUNK])
            z_ref[:, c0:c0 + FF_CHUNK] = zc
            zbuf[FFN_HALO:FFN_HALO + tb, c0:c0 + FF_CHUNK] = zc
        facc = jnp.zeros((tb, d), f32)
        for c0 in range(0, D_FF, FF_CHUNK):
            gg = _ffn_conv_cols(zbuf, dww_ref, dwb_ref, l, tb, c0, c0 + FF_CHUNK)
            vv = _ffn_conv_cols(zbuf, dww_ref, dwb_ref, l, tb, D_FF + c0, D_FF + c0 + FF_CHUNK)
            a = (_gelu(gg) * vv).astype(bf16)
            act_ref[:, c0:c0 + FF_CHUNK] = a
            facc = facc + _dot(a, wdn_ref[c0:c0 + FF_CHUNK, :])
        f_ref[...] = facc
        x2_ref[...] = xv + _rms_fwd(facc, g4_ref[l:l + 1, :])
        zbuf[0:FFN_HALO, :] = zbuf[tb:tb + FFN_HALO, :]

    return pl.pallas_call(
        body, name="ffn_fwd", grid=(t // tb,),
        in_specs=[_rows(tb, d), _const(g3.shape), _const((d, ff2)), _const((FFN_HALO, ff2)), _const(dw_b.shape), _const((D_FF, d)),
                  _const(g4.shape)],
        out_specs=(_rows(tb, d), _rows(tb, d), _rows(tb, ff2), _rows(tb, D_FF), _rows(tb, d)),
        out_shape=(jax.ShapeDtypeStruct((t, d), f32), jax.ShapeDtypeStruct((t, d), bf16), jax.ShapeDtypeStruct((t, ff2), f32),
                   jax.ShapeDtypeStruct((t, D_FF), bf16), jax.ShapeDtypeStruct((t, d), f32)),
        scratch_shapes=[pltpu.VMEM((tb + FFN_HALO, ff2), f32)],
        compiler_params=_cparams(1),
    )(x1, g3, w_up, dw_w, dw_b, w_down, g4)


def _loss_fwd_bwd(y, target):
    t, d = y.shape
    tb = _tb(t)

    def body(y_ref, t_ref, dy_ref, loss_ref):
        @pl.when(pl.program_id(0) == 0)
        def _():
            loss_ref[...] = jnp.zeros((1, LANES), f32)

        e = y_ref[...] - t_ref[...]
        dy_ref[...] = e * (1.0 / d)
        loss_ref[...] += 0.5 * jnp.sum(_mean(e * e))

    dy, loss = pl.pallas_call(
        body, name="loss", grid=(t // tb,),
        in_specs=[_rows(tb, d), _rows(tb, d)],
        out_specs=(_rows(tb, d), _acc((1, LANES))),
        out_shape=(jax.ShapeDtypeStruct((t, d), f32), jax.ShapeDtypeStruct((1, LANES), f32)),
        compiler_params=_cparams(1),
    )(y, target)
    return dy, loss[0, 0]


def _ffn_bwd(l, dx2, x1, f, z, w_up, dw_w, dw_b, w_down, g3, g4):
    t, d = x1.shape
    tb = min(128, t)
    nb = t // tb
    ff2 = 2 * D_FF
    per8 = tb // FFN_HALO

    def body(dx2_ref, x1_ref, f_ref, z_ref, zp_ref, wup_ref, dww_ref, dwb_ref, wdn_ref, g3_ref, g4_ref,
             dx1_ref, df_ref, dz_ref, dg4_ref, dg3_ref, ddwb_ref, ddww_ref, zbuf, dzc):
        i = pl.program_id(0)
        b = nb - 1 - i

        @pl.when(i == 0)
        def _():
            dg4_ref[...] = jnp.zeros_like(dg4_ref)
            dg3_ref[...] = jnp.zeros_like(dg3_ref)
            ddwb_ref[...] = jnp.zeros_like(ddwb_ref)
            ddww_ref[...] = jnp.zeros_like(ddww_ref)
            dzc[tb:tb + FFN_HALO, :] = jnp.zeros((FFN_HALO, ff2), f32)

        dout = dx2_ref[...]
        df, dg4 = _rms_bwd(f_ref[...], g4_ref[l:l + 1, :], dout)
        dg4_ref[...] += dg4
        dfb = df.astype(bf16)
        df_ref[...] = dfb
        zbuf[0:FFN_HALO, :] = jnp.where(b > 0, zp_ref[...], 0.0)
        zbuf[FFN_HALO:FFN_HALO + tb, :] = z_ref[...]
        for c0 in range(0, D_FF, FF_CHUNK):
            dact = _dot_nt(dfb, wdn_ref[c0:c0 + FF_CHUNK, :])
            gg = _ffn_conv_cols(zbuf, dww_ref, dwb_ref, l, tb, c0, c0 + FF_CHUNK)
            vv = _ffn_conv_cols(zbuf, dww_ref, dwb_ref, l, tb, D_FF + c0, D_FF + c0 + FF_CHUNK)
            gel, dgel = _gelu_and_grad(gg)
            dzc[0:tb, c0:c0 + FF_CHUNK] = dact * vv * dgel
            dzc[0:tb, D_FF + c0:D_FF + c0 + FF_CHUNK] = dact * gel
        dh2 = jnp.zeros((tb, d), f32)
        for c0 in range(0, ff2, FF_CHUNK):
            cs = slice(c0, c0 + FF_CHUNK)
            d0 = dzc[0:tb, cs]
            ddwb_ref[:, cs] += _colsum(d0)
            for j in range(FFN_K):
                ddww_ref[j:j + 1, cs] += _colsum(d0 * zbuf[pl.ds(FFN_HALO - 2 + j, tb), cs])
            dzv = dww_ref[2:3, cs] * d0 + dww_ref[1:2, cs] * dzc[pl.ds(1, tb), cs] + dww_ref[0:1, cs] * dzc[pl.ds(2, tb), cs]
            dzb = dzv.astype(bf16)
            dz_ref[:, cs] = dzb
            dh2 = dh2 + _dot_nt(dzb, wup_ref[:, cs])
        dzc[tb:tb + FFN_HALO, :] = dzc[0:FFN_HALO, :]
        dxn, dg3 = _rms_bwd(x1_ref[...], g3_ref[l:l + 1, :], dh2)
        dg3_ref[...] += dg3
        dx1_ref[...] = dout + dxn

    return pl.pallas_call(
        body, name="ffn_bwd", grid=(nb,),
        in_specs=[_rows_rev(tb, d, nb), _rows_rev(tb, d, nb), _rows_rev(tb, d, nb), _rows_rev(tb, ff2, nb),
                  pl.BlockSpec((FFN_HALO, ff2), lambda i: (jnp.maximum((nb - 1 - i) * per8 - 1, 0), 0)),
                  _const((d, ff2)), _const((FFN_HALO, ff2)), _const(dw_b.shape), _const((D_FF, d)), _const(g3.shape), _const(g4.shape)],
        out_specs=(_rows_rev(tb, d, nb), _rows_rev(tb, d, nb), _rows_rev(tb, ff2, nb), _acc((1, d)), _acc((1, d)), _acc((1, ff2)),
                   _acc((FFN_HALO, ff2))),
        out_shape=(jax.ShapeDtypeStruct((t, d), f32), jax.ShapeDtypeStruct((t, d), bf16), jax.ShapeDtypeStruct((t, ff2), bf16),
                   jax.ShapeDtypeStruct((1, d), f32), jax.ShapeDtypeStruct((1, d), f32), jax.ShapeDtypeStruct((1, ff2), f32),
                   jax.ShapeDtypeStruct((FFN_HALO, ff2), f32)),
        scratch_shapes=[pltpu.VMEM((tb + FFN_HALO, ff2), f32), pltpu.VMEM((tb + FFN_HALO, ff2), f32)],
        compiler_params=_cparams(1),
    )(dx2, x1, f, z, z, w_up, dw_w, dw_b, w_down, g3, g4)


def _merge_out_bwd(l, dx1, om, s, um, o, gates, conv_out_w, sg_out_w, wo_p, w_out, g2):
    t, d = dx1.shape
    tb = _tb(t)
    hw = N_HEADS * HEAD_PAD

    def body(dx_ref, om_ref, s_ref, um_ref, o_ref, gt_ref, wa_ref, wb_ref, wc_ref, wout_ref, g_ref,
             dom_ref, dgt_ref, dya_ref, dyb_ref, dyc_ref, ds_ref, dum_ref, do_ref, dg2_ref):
        @pl.when(pl.program_id(0) == 0)
        def _():
            dg2_ref[...] = jnp.zeros_like(dg2_ref)

        dom, dg2 = _rms_bwd(om_ref[...], g_ref[l:l + 1, :], dx_ref[...])
        dg2_ref[...] += dg2
        domb = dom.astype(bf16)
        dom_ref[...] = domb
        dmerged = _dot_nt(domb, wout_ref[...])
        branches = ((s_ref, wa_ref, dya_ref, ds_ref), (um_ref, wb_ref, dyb_ref, dum_ref), (o_ref, wc_ref, dyc_ref, do_ref))
        for br, (in_ref, w_ref, dy_ref, din_ref) in enumerate(branches):
            yv = _dot(in_ref[...], w_ref[...])
            sg = jax.nn.sigmoid(gt_ref[:, br * d:(br + 1) * d])
            dyb = (dmerged * sg).astype(bf16)
            dy_ref[...] = dyb
            dgt_ref[:, br * d:(br + 1) * d] = (dmerged * yv * sg * (1.0 - sg)).astype(bf16)
            din_ref[...] = _dot_nt(dyb, w_ref[...]).astype(din_ref.dtype)

    return pl.pallas_call(
        body, name="merge_out_bwd", grid=(t // tb,),
        in_specs=[_rows(tb, d), _rows(tb, d), _rows(tb, D_CONV), _rows(tb, D_SG), _rows(tb, hw), _rows(tb, 3 * d),
                  _const((D_CONV, d)), _const((D_SG, d)), _const((hw, d)), _const((d, d)), _const(g2.shape)],
        out_specs=(_rows(tb, d), _rows(tb, 3 * d), _rows(tb, d), _rows(tb, d), _rows(tb, d), _rows(tb, D_CONV), _rows(tb, D_SG),
                   _rows(tb, hw), _acc((1, d))),
        out_shape=(jax.ShapeDtypeStruct((t, d), bf16), jax.ShapeDtypeStruct((t, 3 * d), bf16), jax.ShapeDtypeStruct((t, d), bf16),
                   jax.ShapeDtypeStruct((t, d), bf16), jax.ShapeDtypeStruct((t, d), bf16), jax.ShapeDtypeStruct((t, D_CONV), f32),
                   jax.ShapeDtypeStruct((t, D_SG), f32), jax.ShapeDtypeStruct((t, hw), bf16), jax.ShapeDtypeStruct((1, d), f32)),
        compiler_params=_cparams(1),
    )(dx1, om, s, um, o, gates, conv_out_w, sg_out_w, wo_p, w_out, g2)


def _attn_bwd(q, k, v, o, lse, do):
    t = q.shape[0]
    tq = _tb(t)
    hw = N_HEADS * HEAD_PAD

    def body(q_ref, k_ref, v_ref, o_ref, lse_ref, do_ref, dq_ref, dk_ref, dv_ref):
        qi = pl.program_id(1)

        @pl.when(qi == 0)
        def _():
            dk_ref[...] = jnp.zeros_like(dk_ref)
            dv_ref[...] = jnp.zeros_like(dv_ref)

        qv, kv, dov = q_ref[...], k_ref[...], do_ref[...]
        s = _dot_nt(qv, kv) * ATT_SCALE
        p = jnp.where(_causal_mask(qi, tq, t), jnp.exp(s - lse_ref[...]), 0.0)
        dp = _dot_nt(dov, v_ref[...])
        delta = jnp.sum(dov.astype(f32) * o_ref[...].astype(f32), axis=-1, keepdims=True)
        ds = (p * (dp - delta) * ATT_SCALE).astype(bf16)
        dq_ref[...] = _dot(ds, kv)
        dk_ref[...] += _dot_tn(ds, qv)
        dv_ref[...] += _dot_tn(p.astype(bf16), dov)

    blk_q = pl.BlockSpec((tq, HEAD_PAD), lambda h, i: (i, h))
    blk_kv = pl.BlockSpec((t, HEAD_PAD), lambda h, i: (0, h))
    return pl.pallas_call(
        body, name="attn_bwd", grid=(N_HEADS, t // tq),
        in_specs=[blk_q, blk_kv, blk_kv, blk_q, pl.BlockSpec((None, tq, 1), lambda h, i: (h, i, 0)), blk_q],
        out_specs=(blk_q, blk_kv, blk_kv),
        out_shape=(jax.ShapeDtypeStruct((t, hw), f32), jax.ShapeDtypeStruct((t, hw), f32), jax.ShapeDtypeStruct((t, hw), f32)),
        compiler_params=_cparams(2),
    )(q, k, v, o, lse, do)


def _mla_proj_bwd(l, dq, dk, dv, qlat, kvlat, cos_t, sin_t, gq, wuq_p, gkv, wk_p, wv_p):
    t = qlat.shape[0]
    tb = _tb(t)
    hw = N_HEADS * HEAD_PAD

    def body(dq_ref, dk_ref, dv_ref, ql_ref, kvl_ref, c_ref, s_ref, gq_ref, wq_ref, gkv_ref, wk_ref, wv_ref,
             dqb_ref, dkb_ref, dvb_ref, dql_ref, dkvl_ref, dkr_ref, dgq_ref, dgkv_ref):
        @pl.when(pl.program_id(0) == 0)
        def _():
            dgq_ref[...] = jnp.zeros_like(dgq_ref)
            dgkv_ref[...] = jnp.zeros_like(dgkv_ref)

        cos_b, sin_b = c_ref[...], s_ref[...]
        for h in range(N_HEADS):
            hs = slice(h * HEAD_PAD, (h + 1) * HEAD_PAD)
            dqh = dq_ref[:, hs]
            dqb_ref[:, hs] = (dqh * cos_b + _swap_rope_halves(dqh * sin_b)).astype(bf16)
        dqn = _dot_nt(dqb_ref[...], wq_ref[...])
        dql, dgq = _rms_bwd(ql_ref[...], gq_ref[l:l + 1, :], dqn)
        dgq_ref[...] += dgq
        dql_ref[...] = dql.astype(bf16)
        dkv_full = dk_ref[...]
        dkb = dkv_full.astype(bf16)
        dkb_ref[...] = dkb
        dkpe = dkv_full[:, 0:HEAD_PAD]
        for h in range(1, N_HEADS):
            dkpe = dkpe + dkv_full[:, h * HEAD_PAD:(h + 1) * HEAD_PAD]
        dkr_ref[...] = (dkpe * cos_b + _swap_rope_halves(dkpe * sin_b)).astype(bf16)
        dvb = dv_ref[...].astype(bf16)
        dvb_ref[...] = dvb
        dkvn = _dot_nt(dkb, wk_ref[...]) + _dot_nt(dvb, wv_ref[...])
        dkvl, dgkv = _rms_bwd(kvl_ref[...], gkv_ref[l:l + 1, :], dkvn)
        dgkv_ref[...] += dgkv
        dkvl_ref[...] = dkvl.astype(bf16)

    return pl.pallas_call(
        body, name="mla_proj_bwd", grid=(t // tb,),
        in_specs=[_rows(tb, hw), _rows(tb, hw), _rows(tb, hw), _rows(tb, Q_LORA), _rows(tb, KV_LORA), _rows(tb, HEAD_PAD),
                  _rows(tb, HEAD_PAD), _const(gq.shape), _const((Q_LORA, hw)), _const(gkv.shape), _const((KV_LORA, hw)),
                  _const((KV_LORA, hw))],
        out_specs=(_rows(tb, hw), _rows(tb, hw), _rows(tb, hw), _rows(tb, Q_LORA), _rows(tb, KV_LORA), _rows(tb, HEAD_PAD),
                   _acc((1, Q_LORA)), _acc((1, KV_LORA))),
        out_shape=(jax.ShapeDtypeStruct((t, hw), bf16), jax.ShapeDtypeStruct((t, hw), bf16), jax.ShapeDtypeStruct((t, hw), bf16),
                   jax.ShapeDtypeStruct((t, Q_LORA), bf16), jax.ShapeDtypeStruct((t, KV_LORA), bf16),
                   jax.ShapeDtypeStruct((t, HEAD_PAD), bf16), jax.ShapeDtypeStruct((1, Q_LORA), f32),
                   jax.ShapeDtypeStruct((1, KV_LORA), f32)),
        compiler_params=_cparams(1),
    )(dq, dk, dv, qlat, kvlat, cos_t, sin_t, gq, wuq_p, gkv, wk_p, wv_p)


def _sgu_bwd(l, b_in, dum, ln_g, ln_b, sg_w, bexp):
    t = b_in.shape[0]
    tb = _tb(t)
    gw = D_SG // SG_GROUPS

    def body(b_ref, dum_ref, g_ref, be_ref, w_ref, bexp_ref, db_ref, dw_ref, dsgb_ref, dlg_ref, dlb_ref, dvn_s):
        @pl.when(pl.program_id(0) == 0)
        def _():
            dw_ref[...] = jnp.zeros_like(dw_ref)
            dsgb_ref[...] = jnp.zeros_like(dsgb_ref)
            dlg_ref[...] = jnp.zeros_like(dlg_ref)
            dlb_ref[...] = jnp.zeros_like(dlb_ref)

        gl, dgl = _gelu_and_grad(b_ref[...])
        u = gl[:, :D_SG]
        vh, rstd = _ln_stats(gl[:, D_SG:])
        ln_gain = g_ref[l:l + 1, :]
        vn = (vh * ln_gain + be_ref[l:l + 1, :]).astype(bf16)
        dumv = dum_ref[...]
        tri = _tril_mask()
        ones = jnp.ones((FFN_HALO, gw), f32)
        for g in range(SG_GROUPS):
            wg = jnp.where(tri, w_ref[l, g], 0.0).astype(bf16)
            cs = slice(g * gw, (g + 1) * gw)
            for r0 in range(0, tb, SG_CHUNK):
                rs = slice(r0, r0 + SG_CHUNK)
                vblk = vn[rs, cs]
                mixed = _dot(wg, vblk) + bexp_ref[:, cs]
                db_ref[rs, cs] = (dumv[rs, cs] * mixed * dgl[rs, cs]).astype(bf16)
                dmix = dumv[rs, cs] * u[rs, cs]
                dmb = dmix.astype(bf16)
                dw_ref[g] += jnp.where(tri, _dot_nt(dmb, vblk), 0.0)
                rowsum = lax.dot_general(ones, dmix, (((1,), (1,)), ((), ())), preferred_element_type=f32,
                                         precision=lax.Precision.HIGHEST)
                dsgb_ref[g:g + 1, :] += rowsum[0:1, :]
                dvn_s[rs, cs] = _dot_tn(wg, dmb)
        dvn = dvn_s[...]
        dlg_ref[...] += _colsum(dvn * vh)
        dlb_ref[...] += _colsum(dvn)
        db_ref[:, D_SG:] = (_ln_bwd(vh, rstd, ln_gain, dvn) * dgl[:, D_SG:]).astype(bf16)

    return pl.pallas_call(
        body, name="sgu_bwd", grid=(t // tb,),
        in_specs=[_rows(tb, 2 * D_SG), _rows(tb, D_SG), _const(ln_g.shape), _const(ln_b.shape), _const(sg_w.shape),
                  _const((SG_CHUNK, D_SG))],
        out_specs=(_rows(tb, 2 * D_SG), _acc((SG_GROUPS, SG_CHUNK, SG_CHUNK)), _acc((FFN_HALO, SG_CHUNK)), _acc((1, D_SG)),
                   _acc((1, D_SG))),
        out_shape=(jax.ShapeDtypeStruct((t, 2 * D_SG), bf16), jax.ShapeDtypeStruct((SG_GROUPS, SG_CHUNK, SG_CHUNK), f32),
                   jax.ShapeDtypeStruct((FFN_HALO, SG_CHUNK), f32), jax.ShapeDtypeStruct((1, D_SG), f32),
                   jax.ShapeDtypeStruct((1, D_SG), f32)),
        scratch_shapes=[pltpu.VMEM((tb, D_SG), f32)],
        compiler_params=_cparams(1),
    )(b_in, dum, ln_g, ln_b, sg_w, bexp)


def _conv_bwd(l, a_in, z1, ds, dw_w, ln_g, ln_b):
    t = a_in.shape[0]
    tb = _tb(t)
    nb = t // tb
    per_halo = tb // CONV_HALO

    def body(a_ref, ap_ref, z1_ref, ds_ref, w_ref, g_ref, be_ref, da_ref, ddww_ref, ddwb_ref, dlg_ref, dlb_ref, win, dzb):
        i = pl.program_id(0)
        b = nb - 1 - i

        @pl.when(i == 0)
        def _():
            ddww_ref[...] = jnp.zeros_like(ddww_ref)
            ddwb_ref[...] = jnp.zeros_like(ddwb_ref)
            dlg_ref[...] = jnp.zeros_like(dlg_ref)
            dlb_ref[...] = jnp.zeros_like(dlb_ref)
            dzb[tb:tb + CONV_HALO, :] = jnp.zeros((CONV_HALO, D_CONV), f32)

        a = a_ref[...]
        val = a[:, :D_CONV]
        sg = jax.nn.sigmoid(a[:, D_CONV:])
        ap = ap_ref[...]
        win[0:CONV_HALO, :] = jnp.where(b > 0, ap[:, :D_CONV] * jax.nn.sigmoid(ap[:, D_CONV:]), 0.0)
        win[CONV_HALO:CONV_HALO + tb, :] = val * sg
        zh, rstd = _ln_stats(z1_ref[...])
        ln_gain = g_ref[l:l + 1, :]
        zl = zh * ln_gain + be_ref[l:l + 1, :]
        sgl = jax.nn.sigmoid(zl)
        dzl = ds_ref[...] * (sgl * (1.0 + zl * (1.0 - sgl)))
        dlg_ref[...] += _colsum(dzl * zh)
        dlb_ref[...] += _colsum(dzl)
        dz1 = _ln_bwd(zh, rstd, ln_gain, dzl)
        dzb[0:tb, :] = dz1
        ddwb_ref[...] += _colsum(dz1)
        dgate_f = val * sg * (1.0 - sg)
        for c0 in range(0, D_CONV, LANES):
            cs = slice(c0, c0 + LANES)
            for r0 in range(0, tb, LANES):
                d1 = dzb[r0:r0 + LANES, cs]
                acc = jnp.zeros((LANES, LANES), f32)
                for j in range(CONV_K):
                    ddww_ref[j:j + 1, cs] += _colsum(d1 * win[pl.ds(r0 + CONV_HALO - CONV_K + 1 + j, LANES), cs])
                    acc = acc + w_ref[j:j + 1, cs] * dzb[pl.ds(r0 + CONV_K - 1 - j, LANES), cs]
                da_ref[r0:r0 + LANES, cs] = (acc * sg[r0:r0 + LANES, cs]).astype(bf16)
                da_ref[r0:r0 + LANES, c0 + D_CONV:c0 + D_CONV + LANES] = (acc * dgate_f[r0:r0 + LANES, cs]).astype(bf16)
        dzb[tb:tb + CONV_HALO, :] = dzb[0:CONV_HALO, :]

    return pl.pallas_call(
        body, name="conv_bwd", grid=(nb,),
        in_specs=[_rows_rev(tb, 2 * D_CONV, nb),
                  pl.BlockSpec((CONV_HALO, 2 * D_CONV), lambda i: (jnp.maximum((nb - 1 - i) * per_halo - 1, 0), 0)),
                  _rows_rev(tb, D_CONV, nb), _rows_rev(tb, D_CONV, nb), _const((CONV_HALO, D_CONV)), _const(ln_g.shape),
                  _const(ln_b.shape)],
        out_specs=(_rows_rev(tb, 2 * D_CONV, nb), _acc((CONV_HALO, D_CONV)), _acc((1, D_CONV)), _acc((1, D_CONV)), _acc((1, D_CONV))),
        out_shape=(jax.ShapeDtypeStruct((t, 2 * D_CONV), bf16), jax.ShapeDtypeStruct((CONV_HALO, D_CONV), f32),
                   jax.ShapeDtypeStruct((1, D_CONV), f32), jax.ShapeDtypeStruct((1, D_CONV), f32), jax.ShapeDtypeStruct((1, D_CONV), f32)),
        scratch_shapes=[pltpu.VMEM((tb + CONV_HALO, D_CONV), f32), pltpu.VMEM((tb + CONV_HALO, D_CONV), f32)],
        compiler_params=_cparams(1),
    )(a_in, a_in, z1, ds, dw_w, ln_g, ln_b)


def _mix_in_bwd(l, x, g1, dxres, dsegs, w_in_p):
    t, d = x.shape
    tb = _tb(t)

    def body(x_ref, g_ref, dr_ref, *rest):
        dseg_refs, w_ref, dx_ref, dg_ref = rest[:len(SEGS)], rest[len(SEGS)], rest[len(SEGS) + 1], rest[len(SEGS) + 2]

        @pl.when(pl.program_id(0) == 0)
        def _():
            dg_ref[...] = jnp.zeros_like(dg_ref)

        dh = jnp.zeros((tb, d), f32)
        for (off, wd), ds_ref in zip(SEGS, dseg_refs):
            dh = dh + _dot_nt(ds_ref[...], w_ref[:, off:off + wd])
        dxn, dg = _rms_bwd(x_ref[...], g_ref[l:l + 1, :], dh)
        dg_ref[...] += dg
        dx_ref[...] = dr_ref[...] + dxn

    return pl.pallas_call(
        body, name="mix_in_bwd", grid=(t // tb,),
        in_specs=[_rows(tb, d), _const(g1.shape), _rows(tb, d)] + [_rows(tb, wd) for _, wd in SEGS] + [_const((d, D_IN_PAD))],
        out_specs=(_rows(tb, d), _acc((1, d))),
        out_shape=(jax.ShapeDtypeStruct((t, d), f32), jax.ShapeDtypeStruct((1, d), f32)),
        compiler_params=_cparams(1),
    )(x, g1, dxres, *dsegs, w_in_p)


def _pick_block(n, cap=512):
    for b in (cap, 384, 256, 128):
        if b <= cap and n % b == 0:
            return b
    return n


def _wgrad(a, b, name, out_dtype=f32):
    t, kdim = a.shape
    n = b.shape[1]
    bk, bn = _pick_block(kdim), _pick_block(n)

    def body(a_ref, b_ref, o_ref):
        o_ref[...] = _dot_tn(a_ref[...], b_ref[...]).astype(out_dtype)

    return pl.pallas_call(
        body, name=name, grid=(kdim // bk, n // bn),
        in_specs=[pl.BlockSpec((t, bk), lambda i, j: (0, i)), pl.BlockSpec((t, bn), lambda i, j: (0, j))],
        out_specs=pl.BlockSpec((bk, bn), lambda i, j: (i, j)),
        out_shape=jax.ShapeDtypeStruct((kdim, n), out_dtype),
        compiler_params=_cparams(2),
    )(a, b)


_BC1 = 1.0 - ADAM_B1 ** ADAM_STEP
_BC2 = 1.0 - ADAM_B2 ** ADAM_STEP


def _adam_math(g, w, m, v):
    nm = ADAM_B1 * m + (1.0 - ADAM_B1) * g
    nv = ADAM_B2 * v + (1.0 - ADAM_B2) * (g * g)
    delta = -ADAM_LR * ((nm / _BC1) / (jnp.sqrt(nv / _BC2) + ADAM_EPS) + ADAM_WD * w)
    return delta, nm, nv


def _slot_sum(r_ref, index=()):
    g = r_ref[(0,) + index].astype(f32)
    for s in range(1, N_DEV):
        g = g + r_ref[(s,) + index].astype(f32)
    return g


def _adamw_shard(l, recv, w, m, v, prev, name):
    _, k, ns = recv.shape
    rb = 256 if k % 256 == 0 else k
    blk = pl.BlockSpec((None, rb, ns), lambda i: (l, i, 0))

    def body(r_ref, w_ref, m_ref, v_ref, *rest):
        g_ref, d_ref, nm_ref, nv_ref = rest[-4:]
        g = _slot_sum(r_ref)
        g_ref[...] = g
        d_ref[...], nm_ref[...], nv_ref[...] = _adam_math(g, w_ref[...], m_ref[...], v_ref[...])

    out = jax.ShapeDtypeStruct(w.shape, f32)
    n_prev = 0 if prev is None else 4
    return pl.pallas_call(
        body, name=name, grid=(k // rb,),
        in_specs=[pl.BlockSpec((N_DEV, rb, ns), lambda i: (0, i, 0)), blk, blk, blk] + [pl.BlockSpec(memory_space=pl.ANY)] * n_prev,
        out_specs=(blk, blk, blk, blk), out_shape=(out, out, out, out),
        input_output_aliases={4 + j: j for j in range(n_prev)},
        compiler_params=_cparams(1),
    )(recv, w, m, v, *(prev or ()))


def _adamw_rep(recvs, ws, ms, vs):
    depth = len(recvs)
    nt = len(REP_VECS)

    def body(*refs):
        r_refs = refs[:depth]
        w_refs, m_refs, v_refs = (refs[depth + i * nt:depth + (i + 1) * nt] for i in range(3))
        outs = refs[depth + 3 * nt:]
        for ti, (_, row, width, nrows) in enumerate(REP_VECS):
            for l in range(depth):
                g = r_refs[l][0, row:row + nrows, 0:width]
                for s in range(1, N_DEV):
                    g = g + r_refs[l][s, row:row + nrows, 0:width]
                pick = (lambda ref: ref[l]) if nrows > 1 else (lambda ref: ref[l:l + 1, :])
                delta, nm, nv = _adam_math(g, pick(w_refs[ti]), pick(m_refs[ti]), pick(v_refs[ti]))
                for o_ref, val in zip(outs[4 * ti:4 * ti + 4], (g, delta, nm, nv)):
                    if nrows > 1:
                        o_ref[l] = val
                    else:
                        o_ref[l:l + 1, :] = val

    ins = tuple(recvs) + tuple(ws) + tuple(ms) + tuple(vs)
    out_shape = tuple(jax.ShapeDtypeStruct(w.shape, f32) for w in ws for _ in range(4))
    return pl.pallas_call(
        body, name="adamw_rep", grid=(1,),
        in_specs=[_whole(a) for a in ins], out_specs=tuple(_whole(o) for o in out_shape), out_shape=out_shape,
        compiler_params=_cparams(1),
    )(*ins)


def _adamw_sg_w(recvs, w, m, v):
    depth = len(recvs)

    def body(*refs):
        r_refs = refs[:depth]
        w_ref, m_ref, v_ref = refs[depth:depth + 3]
        outs = refs[depth + 3:]
        for l in range(depth):
            for gi in range(SG_GROUPS):
                g = _slot_sum(r_refs[l], (gi,))
                delta, nm, nv = _adam_math(g, w_ref[l, gi], m_ref[l, gi], v_ref[l, gi])
                for o_ref, val in zip(outs, (g, delta, nm, nv)):
                    o_ref[l, gi] = val

    ins = tuple(recvs) + (w, m, v)
    out = jax.ShapeDtypeStruct(w.shape, f32)
    return pl.pallas_call(
        body, name="adamw_sg_w", grid=(1,),
        in_specs=[_whole(a) for a in ins], out_specs=tuple(_whole(out) for _ in range(4)), out_shape=(out,) * 4,
        compiler_params=_cparams(1),
    )(*ins)


def _rope_tables(positions):
    t = positions.shape[0]
    inv = 10000.0 ** (-jnp.arange(0, QK_ROPE, 2, dtype=f32) / QK_ROPE)
    ang = positions.astype(f32)[:, None] * inv
    cos, sin = jnp.cos(ang), jnp.sin(ang)
    tail = jnp.zeros((t, HEAD_PAD - KR_LO - QK_ROPE), f32)
    cos_t = jnp.concatenate([jnp.ones((t, KR_LO), f32), cos, cos, tail], axis=1)
    sin_t = jnp.concatenate([jnp.zeros((t, KR_LO), f32), -sin, sin, tail], axis=1)
    return cos_t, sin_t


def _bias_over_channels(sg_b_l):
    return jnp.broadcast_to(sg_b_l.T[:, :, None], (SG_CHUNK, SG_GROUPS, D_SG // SG_GROUPS)).reshape(SG_CHUNK, D_SG)


def _layer_weights(gathered, rep, l):
    conv_out_w, sg_out_w, wuq_p, wk_p, wv_p, wo_p, conv_dw_w, ffn_dw_w = _asm_small(
        gathered["conv_out_w"], gathered["sg_out_w"], gathered["mla_w_uq"], gathered["mla_w_ukv"], gathered["mla_w_o"],
        gathered["conv_dw_w"], gathered["ffn_dw_w"])
    g_out, g_down = gathered["w_out"], gathered["ffn_w_down"]
    w = dict(rep)
    w.update(
        l=l, w_in_p=_asm_w_in(gathered["w_in"]), w_up=_asm_cols(gathered["ffn_w_up"], "asm_w_up"),
        conv_out_w=conv_out_w, sg_out_w=sg_out_w, wuq_p=wuq_p, wk_p=wk_p, wv_p=wv_p, wo_p=wo_p, conv_dw_w_p=conv_dw_w, ffn_dw_w_p=ffn_dw_w,
        w_out=g_out.reshape(g_out.shape[0] * g_out.shape[1], g_out.shape[2]),
        w_down=g_down.reshape(g_down.shape[0] * g_down.shape[1], g_down.shape[2]),
        bexp=_bias_over_channels(rep["sg_b"][l]))
    return w


def _layer_fwd(x, w, cos_t, sin_t):
    l = w["l"]
    h, a_in, b_in, qlat, kvlat, krope, gates = _mix_in_fwd(l, x, w["mix_pre_g"], w["w_in_p"])
    s, z1 = _conv_fwd(l, a_in, w["conv_dw_w_p"], w["conv_dw_b"], w["conv_ln_g"], w["conv_ln_b"])
    um = _sgu_fwd(l, b_in, w["sg_ln_g"], w["sg_ln_b"], w["sg_w"], w["bexp"])
    q, k, v, qn, kvn = _mla_proj_fwd(l, qlat, kvlat, krope, cos_t, sin_t, w["mla_q_norm_g"], w["wuq_p"], w["mla_kv_norm_g"],
                                     w["wk_p"], w["wv_p"])
    o, lse = _attn_fwd(q, k, v)
    x1, merged, om = _merge_out_fwd(l, x, s, um, o, gates, w["conv_out_w"], w["sg_out_w"], w["wo_p"], w["w_out"], w["mix_post_g"])
    x2, h2, z, act, f = _ffn_fwd(l, x1, w["ffn_pre_g"], w["w_up"], w["ffn_dw_w_p"], w["ffn_dw_b"], w["w_down"], w["ffn_post_g"])
    saved = dict(x=x, h=h, a_in=a_in, b_in=b_in, qlat=qlat, kvlat=kvlat, gates=gates, s=s, z1=z1, um=um, q=q, k=k, v=v, qn=qn,
                 kvn=kvn, o=o, lse=lse, x1=x1, merged=merged, om=om, h2=h2, z=z, act=act, f=f)
    return x2, saved


def _layer_bwd(dx2, sv, w, cos_t, sin_t, shard_cols):
    l = w["l"]
    vec = {}
    dx1, df, dz, vec["ffn_post_g"], vec["ffn_pre_g"], vec["ffn_dw_b"], d_fdw = _ffn_bwd(
        l, dx2, sv["x1"], sv["f"], sv["z"], w["w_up"], w["ffn_dw_w_p"], w["ffn_dw_b"], w["w_down"], w["ffn_pre_g"], w["ffn_post_g"])
    d_down = _wgrad(sv["act"], df, "wgrad_ffn_down", bf16)
    d_up = _wgrad(sv["h2"], dz, "wgrad_ffn_up")

    dom, dgates, dya, dyb, dyc, ds, dum, do, vec["mix_post_g"] = _merge_out_bwd(
        l, dx1, sv["om"], sv["s"], sv["um"], sv["o"], sv["gates"], w["conv_out_w"], w["sg_out_w"], w["wo_p"], w["w_out"], w["mix_post_g"])
    d_out = _wgrad(sv["merged"], dom, "wgrad_w_out", bf16)
    d_co = _wgrad(sv["s"], dya, "wgrad_conv_out")
    d_so = _wgrad(sv["um"], dyb, "wgrad_sg_out")
    d_wo = _wgrad(sv["o"], dyc, "wgrad_w_o")

    dq, dk, dv = _attn_bwd(sv["q"], sv["k"], sv["v"], sv["o"], sv["lse"], do)
    dqb, dkb, dvb, dqlat, dkvlat, dkrope, vec["mla_q_norm_g"], vec["mla_kv_norm_g"] = _mla_proj_bwd(
        l, dq, dk, dv, sv["qlat"], sv["kvlat"], cos_t, sin_t, w["mla_q_norm_g"], w["wuq_p"], w["mla_kv_norm_g"], w["wk_p"], w["wv_p"])
    d_uq = _wgrad(sv["qn"], dqb, "wgrad_w_uq")
    d_uk = _wgrad(sv["kvn"], dkb, "wgrad_w_uk")
    d_uv = _wgrad(sv["kvn"], dvb, "wgrad_w_uv")

    db_in, dsg_w, dsgb, vec["sg_ln_g"], vec["sg_ln_b"] = _sgu_bwd(l, sv["b_in"], dum, w["sg_ln_g"], w["sg_ln_b"], w["sg_w"], w["bexp"])
    da_in, d_cdw, vec["conv_dw_b"], vec["conv_ln_g"], vec["conv_ln_b"] = _conv_bwd(
        l, sv["a_in"], sv["z1"], ds, w["conv_dw_w_p"], w["conv_ln_g"], w["conv_ln_b"])

    dsegs = (da_in, db_in, dqlat, dkvlat, dkrope, dgates)
    dx, vec["mix_pre_g"] = _mix_in_bwd(l, sv["x"], w["mix_pre_g"], dx1, dsegs, w["w_in_p"])
    names = ("a", "b", "q", "kv", "kr", "g")
    d_in_segs = [_wgrad(sv["h"], dseg, "wgrad_w_in_" + nm) for nm, dseg in zip(names, dsegs)]

    send = {}
    send["w_in"] = _dis_w_in(d_in_segs, shard_cols["w_in"])
    send["ffn_w_up"] = _dis_cols(d_up, "dis_w_up")
    (send["conv_out_w"], send["sg_out_w"], send["mla_w_uq"], send["mla_w_ukv"], send["mla_w_o"], send["conv_dw_w"],
     send["ffn_dw_w"]) = _dis_small(d_co, d_so, d_uq, d_uk, d_uv, d_wo, d_cdw, d_fdw)
    send["w_out"] = d_out.reshape(N_DEV, d_out.shape[0] // N_DEV, d_out.shape[1])
    send["ffn_w_down"] = d_down.reshape(N_DEV, d_down.shape[0] // N_DEV, d_down.shape[1])
    rep_pack = _pack_rep([vec[n] for n, _, _, _ in REP_VECS[:-1]], dsgb)
    return dx, send, rep_pack, dsg_w


def kernel(x, positions, mix_pre_g, mix_post_g, ffn_pre_g, ffn_post_g, w_in, conv_dw_w, conv_dw_b, conv_ln_g, conv_ln_b, conv_out_w, sg_ln_g, sg_ln_b, sg_w, sg_b, sg_out_w, mla_q_norm_g, mla_w_uq, mla_kv_norm_g, mla_w_ukv, mla_w_o, w_out, ffn_w_up, ffn_dw_w, ffn_dw_b, ffn_w_down, loss_target, m_mix_pre_g, m_mix_post_g, m_ffn_pre_g, m_ffn_post_g, m_w_in, m_conv_dw_w, m_conv_dw_b, m_conv_ln_g, m_conv_ln_b, m_conv_out_w, m_sg_ln_g, m_sg_ln_b, m_sg_w, m_sg_b, m_sg_out_w, m_mla_q_norm_g, m_mla_w_uq, m_mla_kv_norm_g, m_mla_w_ukv, m_mla_w_o, m_w_out, m_ffn_w_up, m_ffn_dw_w, m_ffn_dw_b, m_ffn_w_down, v_mix_pre_g, v_mix_post_g, v_ffn_pre_g, v_ffn_post_g, v_w_in, v_conv_dw_w, v_conv_dw_b, v_conv_ln_g, v_conv_ln_b, v_conv_out_w, v_sg_ln_g, v_sg_ln_b, v_sg_w, v_sg_b, v_sg_out_w, v_mla_q_norm_g, v_mla_w_uq, v_mla_kv_norm_g, v_mla_w_ukv, v_mla_w_o, v_w_out, v_ffn_w_up, v_ffn_dw_w, v_ffn_dw_b, v_ffn_w_down):
    args = (x, positions, mix_pre_g, mix_post_g, ffn_pre_g, ffn_post_g, w_in, conv_dw_w, conv_dw_b, conv_ln_g, conv_ln_b, conv_out_w, sg_ln_g, sg_ln_b, sg_w, sg_b, sg_out_w, mla_q_norm_g, mla_w_uq, mla_kv_norm_g, mla_w_ukv, mla_w_o, w_out, ffn_w_up, ffn_dw_w, ffn_dw_b, ffn_w_down, loss_target, m_mix_pre_g, m_mix_post_g, m_ffn_pre_g, m_ffn_post_g, m_w_in, m_conv_dw_w, m_conv_dw_b, m_conv_ln_g, m_conv_ln_b, m_conv_out_w, m_sg_ln_g, m_sg_ln_b, m_sg_w, m_sg_b, m_sg_out_w, m_mla_q_norm_g, m_mla_w_uq, m_mla_kv_norm_g, m_mla_w_ukv, m_mla_w_o, m_w_out, m_ffn_w_up, m_ffn_dw_w, m_ffn_dw_b, m_ffn_w_down, v_mix_pre_g, v_mix_post_g, v_ffn_pre_g, v_ffn_post_g, v_w_in, v_conv_dw_w, v_conv_dw_b, v_conv_ln_g, v_conv_ln_b, v_conv_out_w, v_sg_ln_g, v_sg_ln_b, v_sg_w, v_sg_b, v_sg_out_w, v_mla_q_norm_g, v_mla_w_uq, v_mla_kv_norm_g, v_mla_w_ukv, v_mla_w_o, v_w_out, v_ffn_w_up, v_ffn_dw_w, v_ffn_dw_b, v_ffn_w_down)
    n_in = len(IN_NAMES)
    a = dict(zip(IN_NAMES, args[:n_in]))
    target = args[n_in]
    n_w = len(WEIGHTS)
    m_in = dict(zip(WEIGHTS, args[n_in + 1:n_in + 1 + n_w]))
    v_in = dict(zip(WEIGHTS, args[n_in + 1 + n_w:n_in + 1 + 2 * n_w]))
    depth = a["mix_pre_g"].shape[0]
    rep = {n: a[n] for n in WEIGHTS if n not in SHARDED}
    shard_cols = {n: a[n].shape[2] for n in SHARDED}

    wire = {n: a[n] if n in WIRE_F32 else a[n].astype(bf16) for n in SHARDED}
    ws = []
    for l in range(depth):
        got = _exchange([(wire[n], "gather", l) for n in SHARDED], "gather_weights")
        ws.append(_layer_weights(dict(zip(SHARDED, got)), rep, l))

    cos_t, sin_t = _rope_tables(a["positions"][0])
    xl = a["x"][0]
    saved = []
    for l in range(depth):
        xl, sv = _layer_fwd(xl, ws[l], cos_t, sin_t)
        saved.append(sv)
    dx, loss_part = _loss_fwd_bwd(xl, target[0])
    loss = lax.psum(loss_part, AXES)

    outs = {}
    rep_recvs, sgw_recvs = [None] * depth, [None] * depth
    for l in reversed(range(depth)):
        dx, send, rep_pack, dsg_w = _layer_bwd(dx, saved[l], ws[l], cos_t, sin_t, shard_cols)
        got = _exchange([(send[n], "scatter", None) for n in SHARDED] + [(rep_pack, "gather", None), (dsg_w, "gather", None)],
                        "exchange_grads")
        rep_recvs[l], sgw_recvs[l] = got[len(SHARDED)], got[len(SHARDED) + 1]
        for n, recv in zip(SHARDED, got):
            outs[n] = _adamw_shard(l, recv, a[n], m_in[n], v_in[n], outs.get(n), "adamw_" + n)
    vec_names = [n for n, _, _, _ in REP_VECS]
    rep_outs = _adamw_rep(rep_recvs, [a[n] for n in vec_names], [m_in[n] for n in vec_names], [v_in[n] for n in vec_names])
    for i, n in enumerate(vec_names):
        outs[n] = rep_outs[4 * i:4 * i + 4]
    outs["sg_w"] = _adamw_sg_w(sgw_recvs, a["sg_w"], m_in["sg_w"], v_in["sg_w"])

    grad_w, delta_w, new_m, new_v = ([outs[n][j] for n in WEIGHTS] for j in range(4))
    return (loss, dx[None], *grad_w, *delta_w, *new_m, *new_v)
```

```python
import math

import jax
import jax.numpy as jnp
from jax import lax
from jax.experimental import pallas as pl
from jax.experimental.pallas import tpu as pltpu

f32 = jnp.float32
bf16 = jnp.bfloat16

N_DEV = 8
AXES = ("x", "y", "c")
EPS = 1e-6
D_CONV = 512
CONV_K = 31
CONV_HALO = 32
D_SG = 512
SG_GROUPS = 4
SG_CHUNK = 128
N_HEADS = 8
QK_NOPE = 64
QK_ROPE = 32
V_HEAD = 64
HEAD_PAD = 128
Q_LORA = 384
KV_LORA = 256
D_FF = 2816
FFN_K = 3
FFN_HALO = 8
ATT_SCALE = (QK_NOPE + QK_ROPE) ** -0.5
NEG = float(jnp.finfo(jnp.float32).min)

ADAM_LR = 0.001
ADAM_B1 = 0.9
ADAM_B2 = 0.999
ADAM_EPS = 1e-08
ADAM_WD = 0.01
ADAM_STEP = 10

LANES = 128
VMEM_MB = 56

REF_CUTS = (0, 1024, 2048, 2432, 2688, 2720, 5792)
SEGS = ((0, 1024), (1024, 1024), (2048, 384), (2432, 256), (2688, 128), (2816, 3072))
D_IN = 5792
D_IN_PAD = 5888
KR_LO = 64
SEG_INNER = (0, 0, 0, 0, KR_LO, 0)

IN_NAMES = ['x', 'positions', 'mix_pre_g', 'mix_post_g', 'ffn_pre_g', 'ffn_post_g', 'w_in', 'conv_dw_w', 'conv_dw_b', 'conv_ln_g', 'conv_ln_b', 'conv_out_w', 'sg_ln_g', 'sg_ln_b', 'sg_w', 'sg_b', 'sg_out_w', 'mla_q_norm_g', 'mla_w_uq', 'mla_kv_norm_g', 'mla_w_ukv', 'mla_w_o', 'w_out', 'ffn_w_up', 'ffn_dw_w', 'ffn_dw_b', 'ffn_w_down']
WEIGHTS = IN_NAMES[2:]
SHARDED = ("w_in", "conv_dw_w", "conv_out_w", "sg_out_w", "mla_w_uq", "mla_w_ukv", "mla_w_o", "w_out", "ffn_w_up", "ffn_dw_w",
           "ffn_w_down")
FFN_BIG = ("ffn_w_up", "ffn_w_down")
MIX_GROUP = tuple(n for n in SHARDED if n not in FFN_BIG)
WIRE_F32 = ("conv_dw_w", "ffn_dw_w")
REP_VECS = (("mix_pre_g", 0, 1024, 1), ("mix_post_g", 1, 1024, 1), ("ffn_pre_g", 2, 1024, 1), ("ffn_post_g", 3, 1024, 1),
            ("conv_dw_b", 4, 512, 1), ("conv_ln_g", 5, 512, 1), ("conv_ln_b", 6, 512, 1), ("sg_ln_g", 7, 512, 1),
            ("sg_ln_b", 8, 512, 1), ("mla_q_norm_g", 9, 384, 1), ("mla_kv_norm_g", 10, 256, 1), ("ffn_dw_b", 11, 5632, 1),
            ("sg_b", 12, 128, 4))
REP_ROWS = 16
REP_W = 5632


def _cparams(n_axes):
    return pltpu.CompilerParams(dimension_semantics=("arbitrary",) * n_axes, vmem_limit_bytes=VMEM_MB * 2 ** 20)


def _rows(tb, n):
    return pl.BlockSpec((tb, n), lambda i: (i, 0))


def _rows_rev(tb, n, nb):
    return pl.BlockSpec((tb, n), lambda i: (nb - 1 - i, 0))


def _const(shape):
    nd = len(shape)
    return pl.BlockSpec(shape, lambda *_: (0,) * nd, pipeline_mode=pl.Buffered(1))


def _acc(shape):
    nd = len(shape)
    return pl.BlockSpec(shape, lambda *_: (0,) * nd)


def _whole(arr):
    return pl.BlockSpec(arr.shape, lambda *_: (0,) * arr.ndim)


def _dot(a, b):
    return jnp.dot(a, b, preferred_element_type=f32)


def _dot_nt(a, b):
    return lax.dot_general(a, b, (((1,), (1,)), ((), ())), preferred_element_type=f32)


def _dot_tn(a, b):
    return lax.dot_general(a, b, (((0,), (0,)), ((), ())), preferred_element_type=f32)


def _mean(x):
    return jnp.mean(x, axis=-1, keepdims=True)


def _colsum(x):
    return jnp.sum(x, axis=0, keepdims=True)


def _rms_fwd(x, g):
    return x * lax.rsqrt(_mean(x * x) + EPS) * g


def _rms_bwd(x, g, dy):
    r = lax.rsqrt(_mean(x * x) + EPS)
    n = x * r
    dn = dy * g
    return r * (dn - n * _mean(dn * n)), _colsum(dy * n)


def _ln_stats(x):
    mu = _mean(x)
    d = x - mu
    rstd = lax.rsqrt(_mean(d * d) + EPS)
    return d * rstd, rstd


def _ln_bwd(xhat, rstd, g, dy):
    dxh = dy * g
    return rstd * (dxh - _mean(dxh) - xhat * _mean(dxh * xhat))


_GELU_C0 = math.sqrt(2.0 / math.pi)
_GELU_C1 = 0.044715


def _gelu(x):
    t = jnp.tanh(_GELU_C0 * (x + _GELU_C1 * (x * x * x)))
    return 0.5 * x * (1.0 + t)


def _gelu_and_grad(x):
    x2 = x * x
    t = jnp.tanh(_GELU_C0 * (x + _GELU_C1 * (x2 * x)))
    g = 0.5 * x * (1.0 + t)
    dg = 0.5 * (1.0 + t) + 0.5 * x * (1.0 - t * t) * (_GELU_C0 * (1.0 + 3.0 * _GELU_C1 * x2))
    return g, dg


def _swap_rope_halves(x):
    n = x.shape[1]
    half = QK_ROPE // 2
    lane = lax.broadcasted_iota(jnp.int32, x.shape, 1) % HEAD_PAD
    first = (lane >= KR_LO) & (lane < KR_LO + half)
    second = (lane >= KR_LO + half) & (lane < KR_LO + QK_ROPE)
    return jnp.where(first, pltpu.roll(x, n - half, 1), jnp.where(second, pltpu.roll(x, half, 1), 0.0))


def _tb(t):
    return min(256, t)


_HBM = pl.BlockSpec(memory_space=pltpu.HBM)
_SEM = pl.BlockSpec(memory_space=pltpu.SEMAPHORE)
_ANY = pl.BlockSpec(memory_space=pl.ANY)
_EFFECT = pltpu.SideEffectType.DATAFLOW_SIDE_EFFECTING


def _exchange_copies(modes, ins, lands, send_sems, recv_sems, loc_sems):
    x, y, c = lax.axis_index("x"), lax.axis_index("y"), lax.axis_index("c")
    me = 4 * x + 2 * y + c
    copies = []
    for a, mode in enumerate(modes):
        def src(dst_index, a=a, mode=mode):
            return ins[a].at[dst_index] if mode == "scatter" else ins[a]
        copies.append(pltpu.make_async_copy(src(me), lands[a].at[me], loc_sems.at[a]))
        for k in range(1, N_DEV):
            px = 1 - x if (k >> 2) & 1 else x
            py = 1 - y if (k >> 1) & 1 else y
            pc = 1 - c if k & 1 else c
            copies.append(pltpu.make_async_remote_copy(
                src_ref=src(4 * px + 2 * py + pc), dst_ref=lands[a].at[me],
                send_sem=send_sems.at[a * (N_DEV - 1) + k - 1], recv_sem=recv_sems.at[a * (N_DEV - 1) + k - 1],
                device_id=(px, py, pc), device_id_type=pl.DeviceIdType.MESH))
    return copies


def _exchange_start(ops, name, deps=()):
    n = len(ops)
    arrs = [arr for arr, _ in ops]
    modes = [mode for _, mode in ops]
    lands = [lax.empty((N_DEV,) + arr.shape if mode == "gather" else arr.shape, arr.dtype) for arr, mode in ops]

    def body(*refs):
        ins, land_refs = refs[:n], refs[n:2 * n]
        send_sems, recv_sems, loc_sems = refs[2 * n + len(deps):2 * n + len(deps) + 3]
        for cp in _exchange_copies(modes, ins, land_refs, send_sems, recv_sems, loc_sems):
            cp.start()
        refs[-1][...] = jnp.zeros((8, LANES), f32)

    n_rem = n * (N_DEV - 1)
    res = pl.pallas_call(
        body, name=name,
        out_shape=(pltpu.SemaphoreType.DMA((n_rem,)), pltpu.SemaphoreType.DMA((n_rem,)), pltpu.SemaphoreType.DMA((n,)),
                   *[pltpu.HBM(x.shape, x.dtype) for x in arrs + lands], jax.ShapeDtypeStruct((8, LANES), f32)),
        in_specs=[_HBM] * (2 * n) + [_ANY] * len(deps),
        out_specs=(_SEM, _SEM, _SEM, *[_HBM] * (2 * n), pl.BlockSpec(memory_space=pltpu.VMEM)),
        input_output_aliases={i: 3 + i for i in range(2 * n)},
        compiler_params=pltpu.CompilerParams(has_side_effects=_EFFECT),
    )(*[pltpu.with_memory_space_constraint(x, pltpu.HBM) for x in arrs + lands], *deps)
    return dict(modes=modes, sems=res[:3], thru=res[3:3 + 2 * n], token=res[-1], name=name)


def _exchange_wait(handle, after=()):
    modes, thru = handle["modes"], handle["thru"]
    n = len(modes)

    def body(*refs):
        ins, land_refs = refs[:n], refs[n:2 * n]
        send_sems, recv_sems, loc_sems = refs[2 * n:2 * n + 3]
        for cp in _exchange_copies(modes, ins, land_refs, send_sems, recv_sems, loc_sems):
            cp.wait()

    res = pl.pallas_call(
        body, name=handle["name"] + "_wait",
        out_shape=tuple(pltpu.HBM(x.shape, x.dtype) for x in thru),
        in_specs=[_HBM] * (2 * n) + [_SEM] * 3 + [_ANY] * len(after),
        out_specs=tuple([_HBM] * (2 * n)),
        input_output_aliases={i: i for i in range(2 * n)},
        compiler_params=pltpu.CompilerParams(has_side_effects=_EFFECT),
    )(*thru, *handle["sems"], *after)
    return res[n:]


def _w_in_pieces(ns):
    out = []
    for e in range(N_DEV):
        lo, hi = ns * e, ns * (e + 1)
        for s in range(len(SEGS)):
            a, b = max(lo, REF_CUTS[s]), min(hi, REF_CUTS[s + 1])
            if a < b:
                inner = SEG_INNER[s] + a - REF_CUTS[s]
                out.append((e, a - lo, b - lo, s, inner, inner + b - a))
    return out


def _asm_w_in(g):
    _, d, ns = g.shape
    rb = 256
    pieces = _w_in_pieces(ns)

    def body(g_ref, o_ref):
        kr = SEGS[4][0]
        o_ref[:, kr:kr + KR_LO] = jnp.zeros((rb, KR_LO), g.dtype)
        o_ref[:, kr + KR_LO + QK_ROPE:kr + HEAD_PAD] = jnp.zeros((rb, HEAD_PAD - KR_LO - QK_ROPE), g.dtype)
        for e, s0, s1, seg, d0, d1 in pieces:
            off = SEGS[seg][0]
            o_ref[:, off + d0:off + d1] = g_ref[e, :, s0:s1]

    return pl.pallas_call(
        body, name="asm_w_in", grid=(d // rb,),
        in_specs=[pl.BlockSpec((N_DEV, rb, ns), lambda i: (0, i, 0))],
        out_specs=_rows(rb, D_IN_PAD), out_shape=jax.ShapeDtypeStruct((d, D_IN_PAD), g.dtype),
        compiler_params=_cparams(1),
    )(g)


def _dis_w_in(dsegs, ns):
    d = dsegs[0].shape[0]
    rb = 256
    pieces = _w_in_pieces(ns)

    def body(*refs):
        seg_refs, o_ref = refs[:len(SEGS)], refs[len(SEGS)]
        for e, s0, s1, seg, d0, d1 in pieces:
            o_ref[e, :, s0:s1] = seg_refs[seg][:, d0:d1].astype(bf16)

    return pl.pallas_call(
        body, name="dis_w_in", grid=(d // rb,),
        in_specs=[_rows(rb, wd) for _, wd in SEGS],
        out_specs=pl.BlockSpec((N_DEV, rb, ns), lambda i: (0, i, 0)),
        out_shape=jax.ShapeDtypeStruct((N_DEV, d, ns), bf16),
        compiler_params=_cparams(1),
    )(*dsegs)


def _asm_cols(g, name):
    _, k, ns = g.shape
    rb = 256

    def body(g_ref, o_ref):
        for e in range(N_DEV):
            o_ref[:, e * ns:(e + 1) * ns] = g_ref[e]

    return pl.pallas_call(
        body, name=name, grid=(k // rb,),
        in_specs=[pl.BlockSpec((N_DEV, rb, ns), lambda i: (0, i, 0))],
        out_specs=_rows(rb, N_DEV * ns), out_shape=jax.ShapeDtypeStruct((k, N_DEV * ns), g.dtype),
        compiler_params=_cparams(1),
    )(g)


def _dis_cols(w, name):
    k, n = w.shape
    ns = n // N_DEV
    rb = 256

    def body(w_ref, o_ref):
        for e in range(N_DEV):
            o_ref[e] = w_ref[:, e * ns:(e + 1) * ns].astype(bf16)

    return pl.pallas_call(
        body, name=name, grid=(k // rb,),
        in_specs=[_rows(rb, n)],
        out_specs=pl.BlockSpec((N_DEV, rb, ns), lambda i: (0, i, 0)),
        out_shape=jax.ShapeDtypeStruct((N_DEV, k, ns), bf16),
        compiler_params=_cparams(1),
    )(w)


def _asm_small(g_conv_out, g_sg_out, g_uq, g_ukv, g_wo, g_cdw, g_fdw):
    d = g_conv_out.shape[2] * N_DEV
    hw = N_HEADS * HEAD_PAD
    hq = QK_NOPE + QK_ROPE
    ff2 = g_fdw.shape[2] * N_DEV
    cw, fw = g_cdw.shape[2], g_fdw.shape[2]

    def body(co_ref, so_ref, uq_ref, ukv_ref, wo_ref, cdw_ref, fdw_ref, o_co, o_so, o_uq, o_k, o_v, o_wo, o_cdw, o_fdw):
        o_cdw[CONV_K:CONV_HALO, :] = jnp.zeros((CONV_HALO - CONV_K, D_CONV), f32)
        o_fdw[FFN_K:FFN_HALO, :] = jnp.zeros((FFN_HALO - FFN_K, ff2), f32)
        for e in range(N_DEV):
            cs = e * HEAD_PAD
            o_co[:, cs:cs + HEAD_PAD] = co_ref[e]
            o_so[:, cs:cs + HEAD_PAD] = so_ref[e]
            o_uq[:, cs:cs + hq] = uq_ref[e]
            o_uq[:, cs + hq:cs + HEAD_PAD] = jnp.zeros((Q_LORA, HEAD_PAD - hq), bf16)
            o_k[:, cs:cs + QK_NOPE] = ukv_ref[e, :, 0:QK_NOPE]
            o_k[:, cs + QK_NOPE:cs + HEAD_PAD] = jnp.zeros((KV_LORA, HEAD_PAD - QK_NOPE), bf16)
            o_v[:, cs:cs + V_HEAD] = ukv_ref[e, :, QK_NOPE:QK_NOPE + V_HEAD]
            o_v[:, cs + V_HEAD:cs + HEAD_PAD] = jnp.zeros((KV_LORA, HEAD_PAD - V_HEAD), bf16)
            for h in range(N_HEADS):
                o_wo[h * HEAD_PAD:h * HEAD_PAD + V_HEAD, cs:cs + HEAD_PAD] = wo_ref[e, h * V_HEAD:(h + 1) * V_HEAD, :]
                o_wo[h * HEAD_PAD + V_HEAD:(h + 1) * HEAD_PAD, cs:cs + HEAD_PAD] = jnp.zeros((HEAD_PAD - V_HEAD, HEAD_PAD), bf16)
            o_cdw[0:CONV_K, e * cw:(e + 1) * cw] = cdw_ref[e]
            o_fdw[0:FFN_K, e * fw:(e + 1) * fw] = fdw_ref[e]

    ins = (g_conv_out, g_sg_out, g_uq, g_ukv, g_wo, g_cdw, g_fdw)
    out_shape = (jax.ShapeDtypeStruct((D_CONV, d), bf16), jax.ShapeDtypeStruct((D_SG, d), bf16), jax.ShapeDtypeStruct((Q_LORA, hw), bf16),
                 jax.ShapeDtypeStruct((KV_LORA, hw), bf16), jax.ShapeDtypeStruct((KV_LORA, hw), bf16), jax.ShapeDtypeStruct((hw, d), bf16),
                 jax.ShapeDtypeStruct((CONV_HALO, D_CONV), f32), jax.ShapeDtypeStruct((FFN_HALO, ff2), f32))
    return pl.pallas_call(
        body, name="asm_small", grid=(1,),
        in_specs=[_whole(a) for a in ins], out_specs=tuple(_whole(o) for o in out_shape), out_shape=out_shape,
        compiler_params=_cparams(1),
    )(*ins)


def _dis_small(d_co, d_so, d_uq, d_k, d_v, d_wo, d_cdw, d_fdw):
    d = d_co.shape[1]
    hq = QK_NOPE + QK_ROPE
    cw, fw = D_CONV // N_DEV, d_fdw.shape[1] // N_DEV

    def body(co_ref, so_ref, uq_ref, k_ref, v_ref, wo_ref, cdw_ref, fdw_ref, o_co, o_so, o_uq, o_ukv, o_wo, o_cdw, o_fdw):
        for e in range(N_DEV):
            cs = e * HEAD_PAD
            o_co[e] = co_ref[:, cs:cs + HEAD_PAD].astype(bf16)
            o_so[e] = so_ref[:, cs:cs + HEAD_PAD].astype(bf16)
            o_uq[e] = uq_ref[:, cs:cs + hq].astype(bf16)
            o_ukv[e, :, 0:QK_NOPE] = k_ref[:, cs:cs + QK_NOPE].astype(bf16)
            o_ukv[e, :, QK_NOPE:QK_NOPE + V_HEAD] = v_ref[:, cs:cs + V_HEAD].astype(bf16)
            for h in range(N_HEADS):
                o_wo[e, h * V_HEAD:(h + 1) * V_HEAD, :] = wo_ref[h * HEAD_PAD:h * HEAD_PAD + V_HEAD, cs:cs + HEAD_PAD].astype(bf16)
            o_cdw[e] = cdw_ref[0:CONV_K, e * cw:(e + 1) * cw]
            o_fdw[e] = fdw_ref[0:FFN_K, e * fw:(e + 1) * fw]

    ins = (d_co, d_so, d_uq, d_k, d_v, d_wo, d_cdw, d_fdw)
    out_shape = (jax.ShapeDtypeStruct((N_DEV, D_CONV, d // N_DEV), bf16), jax.ShapeDtypeStruct((N_DEV, D_SG, d // N_DEV), bf16),
                 jax.ShapeDtypeStruct((N_DEV, Q_LORA, hq), bf16), jax.ShapeDtypeStruct((N_DEV, KV_LORA, QK_NOPE + V_HEAD), bf16),
                 jax.ShapeDtypeStruct((N_DEV, N_HEADS * V_HEAD, d // N_DEV), bf16), jax.ShapeDtypeStruct((N_DEV, CONV_K, cw), f32),
                 jax.ShapeDtypeStruct((N_DEV, FFN_K, fw), f32))
    return pl.pallas_call(
        body, name="dis_small", grid=(1,),
        in_specs=[_whole(a) for a in ins], out_specs=tuple(_whole(o) for o in out_shape), out_shape=out_shape,
        compiler_params=_cparams(1),
    )(*ins)


def _pack_rep(vec_grads, dsgb):
    def body(*refs):
        o_ref = refs[-1]
        o_ref[...] = jnp.zeros((REP_ROWS, REP_W), f32)
        for (_, row, width, nrows), ref in zip(REP_VECS, refs[:-1]):
            o_ref[row:row + nrows, 0:width] = ref[0:nrows, :]

    ins = tuple(vec_grads) + (dsgb,)
    return pl.pallas_call(
        body, name="pack_rep", grid=(1,),
        in_specs=[_whole(a) for a in ins], out_specs=pl.BlockSpec((REP_ROWS, REP_W), lambda i: (0, 0)),
        out_shape=jax.ShapeDtypeStruct((REP_ROWS, REP_W), f32), compiler_params=_cparams(1),
    )(*ins)


def _mix_in_fwd(l, x, g1, w_in_p):
    t, d = x.shape
    tb = _tb(t)

    def body(x_ref, g_ref, w_ref, h_ref, *outs):
        h = _rms_fwd(x_ref[...], g_ref[l:l + 1, :]).astype(bf16)
        h_ref[...] = h
        for (off, wd), o_ref in zip(SEGS, outs):
            o_ref[...] = _dot(h, w_ref[:, off:off + wd])

    return pl.pallas_call(
        body, name="mix_in_fwd", grid=(t // tb,),
        in_specs=[_rows(tb, d), _const(g1.shape), _const((d, D_IN_PAD))],
        out_specs=tuple([_rows(tb, d)] + [_rows(tb, wd) for _, wd in SEGS]),
        out_shape=tuple([jax.ShapeDtypeStruct((t, d), bf16)] + [jax.ShapeDtypeStruct((t, wd), f32) for _, wd in SEGS]),
        compiler_params=_cparams(1),
    )(x, g1, w_in_p)


def _conv_fwd(l, a_in, dw_w, dw_b, ln_g, ln_b):
    t = a_in.shape[0]
    tb = _tb(t)

    def body(a_ref, w_ref, b_ref, g_ref, be_ref, s_ref, z1_ref, win):
        @pl.when(pl.program_id(0) == 0)
        def _():
            win[0:CONV_HALO, :] = jnp.zeros((CONV_HALO, D_CONV), f32)

        a = a_ref[...]
        win[CONV_HALO:CONV_HALO + tb, :] = a[:, :D_CONV] * jax.nn.sigmoid(a[:, D_CONV:])
        for r0 in range(0, tb, LANES):
            for c0 in range(0, D_CONV, LANES):
                acc = jnp.broadcast_to(b_ref[l:l + 1, c0:c0 + LANES], (LANES, LANES))
                for j in range(CONV_K):
                    acc = acc + w_ref[j:j + 1, c0:c0 + LANES] * win[pl.ds(r0 + CONV_HALO - CONV_K + 1 + j, LANES), c0:c0 + LANES]
                z1_ref[r0:r0 + LANES, c0:c0 + LANES] = acc
        zh, _ = _ln_stats(z1_ref[...])
        zl = zh * g_ref[l:l + 1, :] + be_ref[l:l + 1, :]
        s_ref[...] = (zl * jax.nn.sigmoid(zl)).astype(bf16)
        win[0:CONV_HALO, :] = win[tb:tb + CONV_HALO, :]

    return pl.pallas_call(
        body, name="conv_fwd", grid=(t // tb,),
        in_specs=[_rows(tb, 2 * D_CONV), _const((CONV_HALO, D_CONV)), _const(dw_b.shape), _const(ln_g.shape), _const(ln_b.shape)],
        out_specs=(_rows(tb, D_CONV), _rows(tb, D_CONV)),
        out_shape=(jax.ShapeDtypeStruct((t, D_CONV), bf16), jax.ShapeDtypeStruct((t, D_CONV), f32)),
        scratch_shapes=[pltpu.VMEM((tb + CONV_HALO, D_CONV), f32)],
        compiler_params=_cparams(1),
    )(a_in, dw_w, dw_b, ln_g, ln_b)


def _tril_mask():
    r = lax.broadcasted_iota(jnp.int32, (SG_CHUNK, SG_CHUNK), 0)
    c = lax.broadcasted_iota(jnp.int32, (SG_CHUNK, SG_CHUNK), 1)
    return r >= c


def _sgu_fwd(l, b_in, ln_g, ln_b, sg_w, bexp):
    t = b_in.shape[0]
    tb = _tb(t)
    gw = D_SG // SG_GROUPS

    def body(b_ref, g_ref, be_ref, w_ref, bexp_ref, um_ref):
        gl = _gelu(b_ref[...])
        u = gl[:, :D_SG]
        vh, _ = _ln_stats(gl[:, D_SG:])
        vn = (vh * g_ref[l:l + 1, :] + be_ref[l:l + 1, :]).astype(bf16)
        tri = _tril_mask()
        for g in range(SG_GROUPS):
            wg = jnp.where(tri, w_ref[l, g], 0.0).astype(bf16)
            cs = slice(g * gw, (g + 1) * gw)
            for r0 in range(0, tb, SG_CHUNK):
                rs = slice(r0, r0 + SG_CHUNK)
                mixed = _dot(wg, vn[rs, cs]) + bexp_ref[:, cs]
                um_ref[rs, cs] = (u[rs, cs] * mixed).astype(bf16)

    return pl.pallas_call(
        body, name="sgu_fwd", grid=(t // tb,),
        in_specs=[_rows(tb, 2 * D_SG), _const(ln_g.shape), _const(ln_b.shape), _const(sg_w.shape), _const((SG_CHUNK, D_SG))],
        out_specs=_rows(tb, D_SG),
        out_shape=jax.ShapeDtypeStruct((t, D_SG), bf16),
        compiler_params=_cparams(1),
    )(b_in, ln_g, ln_b, sg_w, bexp)


def _mla_proj_fwd(l, qlat, kvlat, krope, cos_t, sin_t, gq, wuq_p, gkv, wk_p, wv_p):
    t = qlat.shape[0]
    tb = _tb(t)
    hw = N_HEADS * HEAD_PAD

    def body(ql_ref, kvl_ref, kr_ref, c_ref, s_ref, gq_ref, wq_ref, gkv_ref, wk_ref, wv_ref, q_ref, k_ref, v_ref, qn_ref, kvn_ref):
        cos_b, sin_b = c_ref[...], s_ref[...]
        qn = _rms_fwd(ql_ref[...], gq_ref[l:l + 1, :]).astype(bf16)
        qn_ref[...] = qn
        q = _dot(qn, wq_ref[...])
        sw = _swap_rope_halves(q)
        for h in range(N_HEADS):
            hs = slice(h * HEAD_PAD, (h + 1) * HEAD_PAD)
            q_ref[:, hs] = (q[:, hs] * cos_b + sw[:, hs] * sin_b).astype(bf16)
        kvn = _rms_fwd(kvl_ref[...], gkv_ref[l:l + 1, :]).astype(bf16)
        kvn_ref[...] = kvn
        kr = kr_ref[...]
        kpe = kr * cos_b + _swap_rope_halves(kr) * sin_b
        k = _dot(kvn, wk_ref[...])
        for h in range(N_HEADS):
            hs = slice(h * HEAD_PAD, (h + 1) * HEAD_PAD)
            k_ref[:, hs] = (k[:, hs] + kpe).astype(bf16)
        v_ref[...] = _dot(kvn, wv_ref[...]).astype(bf16)

    return pl.pallas_call(
        body, name="mla_proj_fwd", grid=(t // tb,),
        in_specs=[_rows(tb, Q_LORA), _rows(tb, KV_LORA), _rows(tb, HEAD_PAD), _rows(tb, HEAD_PAD), _rows(tb, HEAD_PAD),
                  _const(gq.shape), _const((Q_LORA, hw)), _const(gkv.shape), _const((KV_LORA, hw)), _const((KV_LORA, hw))],
        out_specs=(_rows(tb, hw), _rows(tb, hw), _rows(tb, hw), _rows(tb, Q_LORA), _rows(tb, KV_LORA)),
        out_shape=(jax.ShapeDtypeStruct((t, hw), bf16), jax.ShapeDtypeStruct((t, hw), bf16), jax.ShapeDtypeStruct((t, hw), bf16),
                   jax.ShapeDtypeStruct((t, Q_LORA), bf16), jax.ShapeDtypeStruct((t, KV_LORA), bf16)),
        compiler_params=_cparams(1),
    )(qlat, kvlat, krope, cos_t, sin_t, gq, wuq_p, gkv, wk_p, wv_p)


def _diag_mask(tq):
    return lax.broadcasted_iota(jnp.int32, (tq, tq), 0) >= lax.broadcasted_iota(jnp.int32, (tq, tq), 1)


def _attn_fwd(q, k, v):
    t = q.shape[0]
    tq = _tb(t)

    def body(q_ref, k_ref, v_ref, o_ref, lse_ref, s_buf):
        qi = pl.program_id(1)
        qv = q_ref[...]

        def scores(j):
            return _dot_nt(qv, k_ref[pl.ds(pl.multiple_of(j * tq, tq), tq), :]) * ATT_SCALE

        def sweep_scores(j, m):
            s = scores(j)
            s_buf[j] = s
            return jnp.maximum(m, jnp.max(s, axis=-1, keepdims=True))

        m = lax.fori_loop(0, qi, sweep_scores, jnp.full((tq, 1), NEG, f32))
        s_diag = jnp.where(_diag_mask(tq), scores(qi), NEG)
        s_buf[qi] = s_diag
        m = jnp.maximum(m, jnp.max(s_diag, axis=-1, keepdims=True))

        def sweep_values(j, carry):
            lsum, acc = carry
            p = jnp.exp(s_buf[j] - m)
            vj = v_ref[pl.ds(pl.multiple_of(j * tq, tq), tq), :]
            return lsum + jnp.sum(p, axis=-1, keepdims=True), acc + _dot(p.astype(bf16), vj)

        lsum, acc = lax.fori_loop(0, qi + 1, sweep_values, (jnp.zeros((tq, 1), f32), jnp.zeros((tq, HEAD_PAD), f32)))
        o_ref[...] = (acc / lsum).astype(bf16)
        lse_ref[...] = m + jnp.log(lsum)

    return pl.pallas_call(
        body, name="attn_fwd", grid=(N_HEADS, t // tq),
        in_specs=[pl.BlockSpec((tq, HEAD_PAD), lambda h, i: (i, h)), pl.BlockSpec((t, HEAD_PAD), lambda h, i: (0, h)),
                  pl.BlockSpec((t, HEAD_PAD), lambda h, i: (0, h))],
        out_specs=(pl.BlockSpec((tq, HEAD_PAD), lambda h, i: (i, h)), pl.BlockSpec((None, tq, 1), lambda h, i: (h, i, 0))),
        out_shape=(jax.ShapeDtypeStruct((t, N_HEADS * HEAD_PAD), bf16), jax.ShapeDtypeStruct((N_HEADS, t, 1), f32)),
        scratch_shapes=[pltpu.VMEM((t // tq, tq, tq), f32)],
        compiler_params=_cparams(2),
    )(q, k, v)


def _merge_out_fwd(l, x, s, um, o, gates, conv_out_w, sg_out_w, wo_p, w_out, g2):
    t, d = x.shape
    tb = _tb(t)

    def body(x_ref, s_ref, um_ref, o_ref, gt_ref, wa_ref, wb_ref, wc_ref, wout_ref, g_ref, x1_ref, mg_ref, om_ref):
        merged = (jax.nn.sigmoid(gt_ref[:, 0:d]) * _dot(s_ref[...], wa_ref[...])
                  + jax.nn.sigmoid(gt_ref[:, d:2 * d]) * _dot(um_ref[...], wb_ref[...])
                  + jax.nn.sigmoid(gt_ref[:, 2 * d:3 * d]) * _dot(o_ref[...], wc_ref[...]))
        mb = merged.astype(bf16)
        mg_ref[...] = mb
        om = _dot(mb, wout_ref[...])
        om_ref[...] = om
        x1_ref[...] = x_ref[...] + _rms_fwd(om, g_ref[l:l + 1, :])

    hw = N_HEADS * HEAD_PAD
    return pl.pallas_call(
        body, name="merge_out_fwd", grid=(t // tb,),
        in_specs=[_rows(tb, d), _rows(tb, D_CONV), _rows(tb, D_SG), _rows(tb, hw), _rows(tb, 3 * d),
                  _const((D_CONV, d)), _const((D_SG, d)), _const((hw, d)), _const((d, d)), _const(g2.shape)],
        out_specs=(_rows(tb, d), _rows(tb, d), _rows(tb, d)),
        out_shape=(jax.ShapeDtypeStruct((t, d), f32), jax.ShapeDtypeStruct((t, d), bf16), jax.ShapeDtypeStruct((t, d), f32)),
        compiler_params=_cparams(1),
    )(x, s, um, o, gates, conv_out_w, sg_out_w, wo_p, w_out, g2)


FF_CHUNK = 1408


def _ffn_conv_cols(zbuf, w_ref, b_ref, l, nrows, c0, c1):
    acc = b_ref[l:l + 1, c0:c1] + w_ref[0:1, c0:c1] * zbuf[pl.ds(FFN_HALO - 2, nrows), c0:c1]
    acc = acc + w_ref[1:2, c0:c1] * zbuf[pl.ds(FFN_HALO - 1, nrows), c0:c1]
    return acc + w_ref[2:3, c0:c1] * zbuf[pl.ds(FFN_HALO, nrows), c0:c1]


def _ffn_fwd(l, x1, g3, w_up, dw_w, dw_b, w_down, g4):
    t, d = x1.shape
    tb = _tb(t)
    ff2 = 2 * D_FF

    def body(x_ref, g3_ref, wup_ref, dww_ref, dwb_ref, wdn_ref, g4_ref, x2_ref, h2_ref, z_ref, act_ref, f_ref, zbuf):
        @pl.when(pl.program_id(0) == 0)
        def _():
            zbuf[0:FFN_HALO, :] = jnp.zeros((FFN_HALO, ff2), f32)

        xv = x_ref[...]
        h2 = _rms_fwd(xv, g3_ref[l:l + 1, :]).astype(bf16)
        h2_ref[...] = h2
        for c0 in range(0, ff2, FF_CHUNK):
            zc = _dot(h2, wup_ref[:, c0:c0 + FF_CHUNK])
            z_ref[:, c0:c0 + FF_CHUNK] = zc
            zbuf[FFN_HALO:FFN_HALO + tb, c0:c0 + FF_CHUNK] = zc
        facc = jnp.zeros((tb, d), f32)
        for c0 in range(0, D_FF, FF_CHUNK):
            gg = _ffn_conv_cols(zbuf, dww_ref, dwb_ref, l, tb, c0, c0 + FF_CHUNK)
            vv = _ffn_conv_cols(zbuf, dww_ref, dwb_ref, l, tb, D_FF + c0, D_FF + c0 + FF_CHUNK)
            a = (_gelu(gg) * vv).astype(bf16)
            act_ref[:, c0:c0 + FF_CHUNK] = a
            facc = facc + _dot(a, wdn_ref[c0:c0 + FF_CHUNK, :])
        f_ref[...] = facc
        x2_ref[...] = xv + _rms_fwd(facc, g4_ref[l:l + 1, :])
        zbuf[0:FFN_HALO, :] = zbuf[tb:tb + FFN_HALO, :]

    return pl.pallas_call(
        body, name="ffn_fwd", grid=(t // tb,),
        in_specs=[_rows(tb, d), _const(g3.shape), _const((d, ff2)), _const((FFN_HALO, ff2)), _const(dw_b.shape), _const((D_FF, d)),
                  _const(g4.shape)],
        out_specs=(_rows(tb, d), _rows(tb, d), _rows(tb, ff2), _rows(tb, D_FF), _rows(tb, d)),
        out_shape=(jax.ShapeDtypeStruct((t, d), f32), jax.ShapeDtypeStruct((t, d), bf16), jax.ShapeDtypeStruct((t, ff2), f32),
                   jax.ShapeDtypeStruct((t, D_FF), bf16), jax.ShapeDtypeStruct((t, d), f32)),
        scratch_shapes=[pltpu.VMEM((tb + FFN_HALO, ff2), f32)],
        compiler_params=_cparams(1),
    )(x1, g3, w_up, dw_w, dw_b, w_down, g4)


def _loss_fwd_bwd(y, target):
    t, d = y.shape
    tb = _tb(t)

    def body(y_ref, t_ref, dy_ref, loss_ref):
        @pl.when(pl.program_id(0) == 0)
        def _():
            loss_ref[...] = jnp.zeros((1, LANES), f32)

        e = y_ref[...] - t_ref[...]
        dy_ref[...] = e * (1.0 / d)
        loss_ref[...] += 0.5 * jnp.sum(_mean(e * e))

    dy, loss = pl.pallas_call(
        body, name="loss", grid=(t // tb,),
        in_specs=[_rows(tb, d), _rows(tb, d)],
        out_specs=(_rows(tb, d), _acc((1, LANES))),
        out_shape=(jax.ShapeDtypeStruct((t, d), f32), jax.ShapeDtypeStruct((1, LANES), f32)),
        compiler_params=_cparams(1),
    )(y, target)
    return dy, loss[0, 0]


def _ffn_bwd(l, dx2, x1, f, z, w_up, dw_w, dw_b, w_down, g3, g4, deps=()):
    t, d = x1.shape
    tb = min(128, t)
    nb = t // tb
    ff2 = 2 * D_FF
    per8 = tb // FFN_HALO

    def body(*refs):
        (dx2_ref, x1_ref, f_ref, z_ref, zp_ref, wup_ref, dww_ref, dwb_ref, wdn_ref, g3_ref, g4_ref,
         dx1_ref, df_ref, dz_ref, dg4_ref, dg3_ref, ddwb_ref, ddww_ref, zbuf, dzc) = refs[len(deps):]
        i = pl.program_id(0)
        b = nb - 1 - i

        @pl.when(i == 0)
        def _():
            dg4_ref[...] = jnp.zeros_like(dg4_ref)
            dg3_ref[...] = jnp.zeros_like(dg3_ref)
            ddwb_ref[...] = jnp.zeros_like(ddwb_ref)
            ddww_ref[...] = jnp.zeros_like(ddww_ref)
            dzc[tb:tb + FFN_HALO, :] = jnp.zeros((FFN_HALO, ff2), f32)

        dout = dx2_ref[...]
        df, dg4 = _rms_bwd(f_ref[...], g4_ref[l:l + 1, :], dout)
        dg4_ref[...] += dg4
        dfb = df.astype(bf16)
        df_ref[...] = dfb
        zbuf[0:FFN_HALO, :] = jnp.where(b > 0, zp_ref[...], 0.0)
        zbuf[FFN_HALO:FFN_HALO + tb, :] = z_ref[...]
        for c0 in range(0, D_FF, FF_CHUNK):
            dact = _dot_nt(dfb, wdn_ref[c0:c0 + FF_CHUNK, :])
            gg = _ffn_conv_cols(zbuf, dww_ref, dwb_ref, l, tb, c0, c0 + FF_CHUNK)
            vv = _ffn_conv_cols(zbuf, dww_ref, dwb_ref, l, tb, D_FF + c0, D_FF + c0 + FF_CHUNK)
            gel, dgel = _gelu_and_grad(gg)
            dzc[0:tb, c0:c0 + FF_CHUNK] = dact * vv * dgel
            dzc[0:tb, D_FF + c0:D_FF + c0 + FF_CHUNK] = dact * gel
        dh2 = jnp.zeros((tb, d), f32)
        for c0 in range(0, ff2, FF_CHUNK):
            cs = slice(c0, c0 + FF_CHUNK)
            d0 = dzc[0:tb, cs]
            ddwb_ref[:, cs] += _colsum(d0)
            for j in range(FFN_K):
                ddww_ref[j:j + 1, cs] += _colsum(d0 * zbuf[pl.ds(FFN_HALO - 2 + j, tb), cs])
            dzv = dww_ref[2:3, cs] * d0 + dww_ref[1:2, cs] * dzc[pl.ds(1, tb), cs] + dww_ref[0:1, cs] * dzc[pl.ds(2, tb), cs]
            dzb = dzv.astype(bf16)
            dz_ref[:, cs] = dzb
            dh2 = dh2 + _dot_nt(dzb, wup_ref[:, cs])
        dzc[tb:tb + FFN_HALO, :] = dzc[0:FFN_HALO, :]
        dxn, dg3 = _rms_bwd(x1_ref[...], g3_ref[l:l + 1, :], dh2)
        dg3_ref[...] += dg3
        dx1_ref[...] = dout + dxn

    return pl.pallas_call(
        body, name="ffn_bwd", grid=(nb,),
        in_specs=[_ANY] * len(deps) + [_rows_rev(tb, d, nb), _rows_rev(tb, d, nb), _rows_rev(tb, d, nb), _rows_rev(tb, ff2, nb),
                  pl.BlockSpec((FFN_HALO, ff2), lambda i: (jnp.maximum((nb - 1 - i) * per8 - 1, 0), 0)),
                  _const((d, ff2)), _const((FFN_HALO, ff2)), _const(dw_b.shape), _const((D_FF, d)), _const(g3.shape), _const(g4.shape)],
        out_specs=(_rows_rev(tb, d, nb), _rows_rev(tb, d, nb), _rows_rev(tb, ff2, nb), _acc((1, d)), _acc((1, d)), _acc((1, ff2)),
                   _acc((FFN_HALO, ff2))),
        out_shape=(jax.ShapeDtypeStruct((t, d), f32), jax.ShapeDtypeStruct((t, d), bf16), jax.ShapeDtypeStruct((t, ff2), bf16),
                   jax.ShapeDtypeStruct((1, d), f32), jax.ShapeDtypeStruct((1, d), f32), jax.ShapeDtypeStruct((1, ff2), f32),
                   jax.ShapeDtypeStruct((FFN_HALO, ff2), f32)),
        scratch_shapes=[pltpu.VMEM((tb + FFN_HALO, ff2), f32), pltpu.VMEM((tb + FFN_HALO, ff2), f32)],
        compiler_params=_cparams(1),
    )(*deps, dx2, x1, f, z, z, w_up, dw_w, dw_b, w_down, g3, g4)


def _merge_out_bwd(l, dx1, om, s, um, o, gates, conv_out_w, sg_out_w, wo_p, w_out, g2, deps=()):
    t, d = dx1.shape
    tb = _tb(t)
    hw = N_HEADS * HEAD_PAD

    def body(*refs):
        (dx_ref, om_ref, s_ref, um_ref, o_ref, gt_ref, wa_ref, wb_ref, wc_ref, wout_ref, g_ref,
         dom_ref, dgt_ref, dya_ref, dyb_ref, dyc_ref, ds_ref, dum_ref, do_ref, dg2_ref) = refs[len(deps):]

        @pl.when(pl.program_id(0) == 0)
        def _():
            dg2_ref[...] = jnp.zeros_like(dg2_ref)

        dom, dg2 = _rms_bwd(om_ref[...], g_ref[l:l + 1, :], dx_ref[...])
        dg2_ref[...] += dg2
        domb = dom.astype(bf16)
        dom_ref[...] = domb
        dmerged = _dot_nt(domb, wout_ref[...])
        branches = ((s_ref, wa_ref, dya_ref, ds_ref), (um_ref, wb_ref, dyb_ref, dum_ref), (o_ref, wc_ref, dyc_ref, do_ref))
        for br, (in_ref, w_ref, dy_ref, din_ref) in enumerate(branches):
            yv = _dot(in_ref[...], w_ref[...])
            sg = jax.nn.sigmoid(gt_ref[:, br * d:(br + 1) * d])
            dyb = (dmerged * sg).astype(bf16)
            dy_ref[...] = dyb
            dgt_ref[:, br * d:(br + 1) * d] = (dmerged * yv * sg * (1.0 - sg)).astype(bf16)
            din_ref[...] = _dot_nt(dyb, w_ref[...]).astype(din_ref.dtype)

    return pl.pallas_call(
        body, name="merge_out_bwd", grid=(t // tb,),
        in_specs=[_ANY] * len(deps) + [_rows(tb, d), _rows(tb, d), _rows(tb, D_CONV), _rows(tb, D_SG), _rows(tb, hw), _rows(tb, 3 * d),
                  _const((D_CONV, d)), _const((D_SG, d)), _const((hw, d)), _const((d, d)), _const(g2.shape)],
        out_specs=(_rows(tb, d), _rows(tb, 3 * d), _rows(tb, d), _rows(tb, d), _rows(tb, d), _rows(tb, D_CONV), _rows(tb, D_SG),
                   _rows(tb, hw), _acc((1, d))),
        out_shape=(jax.ShapeDtypeStruct((t, d), bf16), jax.ShapeDtypeStruct((t, 3 * d), bf16), jax.ShapeDtypeStruct((t, d), bf16),
                   jax.ShapeDtypeStruct((t, d), bf16), jax.ShapeDtypeStruct((t, d), bf16), jax.ShapeDtypeStruct((t, D_CONV), f32),
                   jax.ShapeDtypeStruct((t, D_SG), f32), jax.ShapeDtypeStruct((t, hw), bf16), jax.ShapeDtypeStruct((1, d), f32)),
        compiler_params=_cparams(1),
    )(*deps, dx1, om, s, um, o, gates, conv_out_w, sg_out_w, wo_p, w_out, g2)


def _attn_bwd(q, k, v, o, lse, do):
    t = q.shape[0]
    tq = _tb(t)
    hw = N_HEADS * HEAD_PAD

    def body(q_ref, k_ref, v_ref, o_ref, lse_ref, do_ref, dq_ref, dk_ref, dv_ref):
        qi = pl.program_id(1)

        @pl.when(qi == 0)
        def _():
            dk_ref[...] = jnp.zeros_like(dk_ref)
            dv_ref[...] = jnp.zeros_like(dv_ref)

        qv, dov, lse = q_ref[...], do_ref[...], lse_ref[...]
        delta = jnp.sum(dov.astype(f32) * o_ref[...].astype(f32), axis=-1, keepdims=True)

        def key_block(j, dq, diagonal):
            rows = pl.ds(pl.multiple_of(j * tq, tq), tq)
            kj, vj = k_ref[rows, :], v_ref[rows, :]
            p = jnp.exp(_dot_nt(qv, kj) * ATT_SCALE - lse)
            if diagonal:
                p = jnp.where(_diag_mask(tq), p, 0.0)
            ds = (p * (_dot_nt(dov, vj) - delta) * ATT_SCALE).astype(bf16)
            dk_ref[rows, :] += _dot_tn(ds, qv)
            dv_ref[rows, :] += _dot_tn(p.astype(bf16), dov)
            return dq + _dot(ds, kj)

        dq = lax.fori_loop(0, qi, lambda j, dq: key_block(j, dq, False), jnp.zeros((tq, HEAD_PAD), f32))
        dq_ref[...] = key_block(qi, dq, True)

    blk_q = pl.BlockSpec((tq, HEAD_PAD), lambda h, i: (i, h))
    blk_kv = pl.BlockSpec((t, HEAD_PAD), lambda h, i: (0, h))
    return pl.pallas_call(
        body, name="attn_bwd", grid=(N_HEADS, t // tq),
        in_specs=[blk_q, blk_kv, blk_kv, blk_q, pl.BlockSpec((None, tq, 1), lambda h, i: (h, i, 0)), blk_q],
        out_specs=(blk_q, blk_kv, blk_kv),
        out_shape=(jax.ShapeDtypeStruct((t, hw), f32), jax.ShapeDtypeStruct((t, hw), f32), jax.ShapeDtypeStruct((t, hw), f32)),
        compiler_params=_cparams(2),
    )(q, k, v, o, lse, do)


def _mla_proj_bwd(l, dq, dk, dv, qlat, kvlat, cos_t, sin_t, gq, wuq_p, gkv, wk_p, wv_p):
    t = qlat.shape[0]
    tb = _tb(t)
    hw = N_HEADS * HEAD_PAD

    def body(dq_ref, dk_ref, dv_ref, ql_ref, kvl_ref, c_ref, s_ref, gq_ref, wq_ref, gkv_ref, wk_ref, wv_ref,
             dqb_ref, dkb_ref, dvb_ref, dql_ref, dkvl_ref, dkr_ref, dgq_ref, dgkv_ref):
        @pl.when(pl.program_id(0) == 0)
        def _():
            dgq_ref[...] = jnp.zeros_like(dgq_ref)
            dgkv_ref[...] = jnp.zeros_like(dgkv_ref)

        cos_b, sin_b = c_ref[...], s_ref[...]
        for h in range(N_HEADS):
            hs = slice(h * HEAD_PAD, (h + 1) * HEAD_PAD)
            dqh = dq_ref[:, hs]
            dqb_ref[:, hs] = (dqh * cos_b + _swap_rope_halves(dqh * sin_b)).astype(bf16)
        dqn = _dot_nt(dqb_ref[...], wq_ref[...])
        dql, dgq = _rms_bwd(ql_ref[...], gq_ref[l:l + 1, :], dqn)
        dgq_ref[...] += dgq
        dql_ref[...] = dql.astype(bf16)
        dkv_full = dk_ref[...]
        dkb = dkv_full.astype(bf16)
        dkb_ref[...] = dkb
        dkpe = dkv_full[:, 0:HEAD_PAD]
        for h in range(1, N_HEADS):
            dkpe = dkpe + dkv_full[:, h * HEAD_PAD:(h + 1) * HEAD_PAD]
        dkr_ref[...] = (dkpe * cos_b + _swap_rope_halves(dkpe * sin_b)).astype(bf16)
        dvb = dv_ref[...].astype(bf16)
        dvb_ref[...] = dvb
        dkvn = _dot_nt(dkb, wk_ref[...]) + _dot_nt(dvb, wv_ref[...])
        dkvl, dgkv = _rms_bwd(kvl_ref[...], gkv_ref[l:l + 1, :], dkvn)
        dgkv_ref[...] += dgkv
        dkvl_ref[...] = dkvl.astype(bf16)

    return pl.pallas_call(
        body, name="mla_proj_bwd", grid=(t // tb,),
        in_specs=[_rows(tb, hw), _rows(tb, hw), _rows(tb, hw), _rows(tb, Q_LORA), _rows(tb, KV_LORA), _rows(tb, HEAD_PAD),
                  _rows(tb, HEAD_PAD), _const(gq.shape), _const((Q_LORA, hw)), _const(gkv.shape), _const((KV_LORA, hw)),
                  _const((KV_LORA, hw))],
        out_specs=(_rows(tb, hw), _rows(tb, hw), _rows(tb, hw), _rows(tb, Q_LORA), _rows(tb, KV_LORA), _rows(tb, HEAD_PAD),
                   _acc((1, Q_LORA)), _acc((1, KV_LORA))),
        out_shape=(jax.ShapeDtypeStruct((t, hw), bf16), jax.ShapeDtypeStruct((t, hw), bf16), jax.ShapeDtypeStruct((t, hw), bf16),
                   jax.ShapeDtypeStruct((t, Q_LORA), bf16), jax.ShapeDtypeStruct((t, KV_LORA), bf16),
                   jax.ShapeDtypeStruct((t, HEAD_PAD), bf16), jax.ShapeDtypeStruct((1, Q_LORA), f32),
                   jax.ShapeDtypeStruct((1, KV_LORA), f32)),
        compiler_params=_cparams(1),
    )(dq, dk, dv, qlat, kvlat, cos_t, sin_t, gq, wuq_p, gkv, wk_p, wv_p)


def _sgu_bwd(l, b_in, dum, ln_g, ln_b, sg_w, bexp):
    t = b_in.shape[0]
    tb = _tb(t)
    gw = D_SG // SG_GROUPS

    def body(b_ref, dum_ref, g_ref, be_ref, w_ref, bexp_ref, db_ref, dw_ref, dsgb_ref, dlg_ref, dlb_ref, dvn_s):
        @pl.when(pl.program_id(0) == 0)
        def _():
            dw_ref[...] = jnp.zeros_like(dw_ref)
            dsgb_ref[...] = jnp.zeros_like(dsgb_ref)
            dlg_ref[...] = jnp.zeros_like(dlg_ref)
            dlb_ref[...] = jnp.zeros_like(dlb_ref)

        gl, dgl = _gelu_and_grad(b_ref[...])
        u = gl[:, :D_SG]
        vh, rstd = _ln_stats(gl[:, D_SG:])
        ln_gain = g_ref[l:l + 1, :]
        vn = (vh * ln_gain + be_ref[l:l + 1, :]).astype(bf16)
        dumv = dum_ref[...]
        tri = _tril_mask()
        ones = jnp.ones((FFN_HALO, gw), f32)
        for g in range(SG_GROUPS):
            wg = jnp.where(tri, w_ref[l, g], 0.0).astype(bf16)
            cs = slice(g * gw, (g + 1) * gw)
            for r0 in range(0, tb, SG_CHUNK):
                rs = slice(r0, r0 + SG_CHUNK)
                vblk = vn[rs, cs]
                mixed = _dot(wg, vblk) + bexp_ref[:, cs]
                db_ref[rs, cs] = (dumv[rs, cs] * mixed * dgl[rs, cs]).astype(bf16)
                dmix = dumv[rs, cs] * u[rs, cs]
                dmb = dmix.astype(bf16)
                dw_ref[g] += jnp.where(tri, _dot_nt(dmb, vblk), 0.0)
                rowsum = lax.dot_general(ones, dmix, (((1,), (1,)), ((), ())), preferred_element_type=f32,
                                         precision=lax.Precision.HIGHEST)
                dsgb_ref[g:g + 1, :] += rowsum[0:1, :]
                dvn_s[rs, cs] = _dot_tn(wg, dmb)
        dvn = dvn_s[...]
        dlg_ref[...] += _colsum(dvn * vh)
        dlb_ref[...] += _colsum(dvn)
        db_ref[:, D_SG:] = (_ln_bwd(vh, rstd, ln_gain, dvn) * dgl[:, D_SG:]).astype(bf16)

    return pl.pallas_call(
        body, name="sgu_bwd", grid=(t // tb,),
        in_specs=[_rows(tb, 2 * D_SG), _rows(tb, D_SG), _const(ln_g.shape), _const(ln_b.shape), _const(sg_w.shape),
                  _const((SG_CHUNK, D_SG))],
        out_specs=(_rows(tb, 2 * D_SG), _acc((SG_GROUPS, SG_CHUNK, SG_CHUNK)), _acc((FFN_HALO, SG_CHUNK)), _acc((1, D_SG)),
                   _acc((1, D_SG))),
        out_shape=(jax.ShapeDtypeStruct((t, 2 * D_SG), bf16), jax.ShapeDtypeStruct((SG_GROUPS, SG_CHUNK, SG_CHUNK), f32),
                   jax.ShapeDtypeStruct((FFN_HALO, SG_CHUNK), f32), jax.ShapeDtypeStruct((1, D_SG), f32),
                   jax.ShapeDtypeStruct((1, D_SG), f32)),
        scratch_shapes=[pltpu.VMEM((tb, D_SG), f32)],
        compiler_params=_cparams(1),
    )(b_in, dum, ln_g, ln_b, sg_w, bexp)


def _conv_bwd(l, a_in, z1, ds, dw_w, ln_g, ln_b):
    t = a_in.shape[0]
    tb = _tb(t)
    nb = t // tb
    per_halo = tb // CONV_HALO

    def body(a_ref, ap_ref, z1_ref, ds_ref, w_ref, g_ref, be_ref, da_ref, ddww_ref, ddwb_ref, dlg_ref, dlb_ref, win, dzb):
        i = pl.program_id(0)
        b = nb - 1 - i

        @pl.when(i == 0)
        def _():
            ddww_ref[...] = jnp.zeros_like(ddww_ref)
            ddwb_ref[...] = jnp.zeros_like(ddwb_ref)
            dlg_ref[...] = jnp.zeros_like(dlg_ref)
            dlb_ref[...] = jnp.zeros_like(dlb_ref)
            dzb[tb:tb + CONV_HALO, :] = jnp.zeros((CONV_HALO, D_CONV), f32)

        a = a_ref[...]
        val = a[:, :D_CONV]
        sg = jax.nn.sigmoid(a[:, D_CONV:])
        ap = ap_ref[...]
        win[0:CONV_HALO, :] = jnp.where(b > 0, ap[:, :D_CONV] * jax.nn.sigmoid(ap[:, D_CONV:]), 0.0)
        win[CONV_HALO:CONV_HALO + tb, :] = val * sg
        zh, rstd = _ln_stats(z1_ref[...])
        ln_gain = g_ref[l:l + 1, :]
        zl = zh * ln_gain + be_ref[l:l + 1, :]
        sgl = jax.nn.sigmoid(zl)
        dzl = ds_ref[...] * (sgl * (1.0 + zl * (1.0 - sgl)))
        dlg_ref[...] += _colsum(dzl * zh)
        dlb_ref[...] += _colsum(dzl)
        dz1 = _ln_bwd(zh, rstd, ln_gain, dzl)
        dzb[0:tb, :] = dz1
        ddwb_ref[...] += _colsum(dz1)
        dgate_f = val * sg * (1.0 - sg)
        for c0 in range(0, D_CONV, LANES):
            cs = slice(c0, c0 + LANES)
            for r0 in range(0, tb, LANES):
                d1 = dzb[r0:r0 + LANES, cs]
                acc = jnp.zeros((LANES, LANES), f32)
                for j in range(CONV_K):
                    ddww_ref[j:j + 1, cs] += _colsum(d1 * win[pl.ds(r0 + CONV_HALO - CONV_K + 1 + j, LANES), cs])
                    acc = acc + w_ref[j:j + 1, cs] * dzb[pl.ds(r0 + CONV_K - 1 - j, LANES), cs]
                da_ref[r0:r0 + LANES, cs] = (acc * sg[r0:r0 + LANES, cs]).astype(bf16)
                da_ref[r0:r0 + LANES, c0 + D_CONV:c0 + D_CONV + LANES] = (acc * dgate_f[r0:r0 + LANES, cs]).astype(bf16)
        dzb[tb:tb + CONV_HALO, :] = dzb[0:CONV_HALO, :]

    return pl.pallas_call(
        body, name="conv_bwd", grid=(nb,),
        in_specs=[_rows_rev(tb, 2 * D_CONV, nb),
                  pl.BlockSpec((CONV_HALO, 2 * D_CONV), lambda i: (jnp.maximum((nb - 1 - i) * per_halo - 1, 0), 0)),
                  _rows_rev(tb, D_CONV, nb), _rows_rev(tb, D_CONV, nb), _const((CONV_HALO, D_CONV)), _const(ln_g.shape),
                  _const(ln_b.shape)],
        out_specs=(_rows_rev(tb, 2 * D_CONV, nb), _acc((CONV_HALO, D_CONV)), _acc((1, D_CONV)), _acc((1, D_CONV)), _acc((1, D_CONV))),
        out_shape=(jax.ShapeDtypeStruct((t, 2 * D_CONV), bf16), jax.ShapeDtypeStruct((CONV_HALO, D_CONV), f32),
                   jax.ShapeDtypeStruct((1, D_CONV), f32), jax.ShapeDtypeStruct((1, D_CONV), f32), jax.ShapeDtypeStruct((1, D_CONV), f32)),
        scratch_shapes=[pltpu.VMEM((tb + CONV_HALO, D_CONV), f32), pltpu.VMEM((tb + CONV_HALO, D_CONV), f32)],
        compiler_params=_cparams(1),
    )(a_in, a_in, z1, ds, dw_w, ln_g, ln_b)


def _mix_in_bwd(l, x, g1, dxres, dsegs, w_in_p):
    t, d = x.shape
    tb = _tb(t)

    def body(x_ref, g_ref, dr_ref, *rest):
        dseg_refs, w_ref, dx_ref, dg_ref = rest[:len(SEGS)], rest[len(SEGS)], rest[len(SEGS) + 1], rest[len(SEGS) + 2]

        @pl.when(pl.program_id(0) == 0)
        def _():
            dg_ref[...] = jnp.zeros_like(dg_ref)

        dh = jnp.zeros((tb, d), f32)
        for (off, wd), ds_ref in zip(SEGS, dseg_refs):
            dh = dh + _dot_nt(ds_ref[...], w_ref[:, off:off + wd])
        dxn, dg = _rms_bwd(x_ref[...], g_ref[l:l + 1, :], dh)
        dg_ref[...] += dg
        dx_ref[...] = dr_ref[...] + dxn

    return pl.pallas_call(
        body, name="mix_in_bwd", grid=(t // tb,),
        in_specs=[_rows(tb, d), _const(g1.shape), _rows(tb, d)] + [_rows(tb, wd) for _, wd in SEGS] + [_const((d, D_IN_PAD))],
        out_specs=(_rows(tb, d), _acc((1, d))),
        out_shape=(jax.ShapeDtypeStruct((t, d), f32), jax.ShapeDtypeStruct((1, d), f32)),
        compiler_params=_cparams(1),
    )(x, g1, dxres, *dsegs, w_in_p)


def _pick_block(n, cap=512):
    for b in (cap, 384, 256, 128):
        if b <= cap and n % b == 0:
            return b
    return n


def _wgrad(a, b, name, out_dtype=f32):
    t, kdim = a.shape
    n = b.shape[1]
    bk, bn = _pick_block(kdim), _pick_block(n)

    def body(a_ref, b_ref, o_ref):
        o_ref[...] = _dot_tn(a_ref[...], b_ref[...]).astype(out_dtype)

    return pl.pallas_call(
        body, name=name, grid=(kdim // bk, n // bn),
        in_specs=[pl.BlockSpec((t, bk), lambda i, j: (0, i)), pl.BlockSpec((t, bn), lambda i, j: (0, j))],
        out_specs=pl.BlockSpec((bk, bn), lambda i, j: (i, j)),
        out_shape=jax.ShapeDtypeStruct((kdim, n), out_dtype),
        compiler_params=_cparams(2),
    )(a, b)


_BC1 = 1.0 - ADAM_B1 ** ADAM_STEP
_BC2 = 1.0 - ADAM_B2 ** ADAM_STEP


def _adam_math(g, w, m, v):
    nm = ADAM_B1 * m + (1.0 - ADAM_B1) * g
    nv = ADAM_B2 * v + (1.0 - ADAM_B2) * (g * g)
    delta = -ADAM_LR * ((nm / _BC1) / (jnp.sqrt(nv / _BC2) + ADAM_EPS) + ADAM_WD * w)
    return delta, nm, nv


def _slot_sum(r_ref, index=()):
    g = r_ref[(0,) + index].astype(f32)
    for s in range(1, N_DEV):
        g = g + r_ref[(s,) + index].astype(f32)
    return g


def _adamw_shard(l, recv, w, m, v, prev, name, deps=()):
    _, k, ns = recv.shape
    rb = 256 if k % 256 == 0 else k
    blk = pl.BlockSpec((None, rb, ns), lambda i: (l, i, 0))

    def body(r_ref, w_ref, m_ref, v_ref, *rest):
        g_ref, d_ref, nm_ref, nv_ref = rest[-4:]
        g = _slot_sum(r_ref)
        g_ref[...] = g
        d_ref[...], nm_ref[...], nv_ref[...] = _adam_math(g, w_ref[...], m_ref[...], v_ref[...])

    out = jax.ShapeDtypeStruct(w.shape, f32)
    n_prev = 0 if prev is None else 4
    return pl.pallas_call(
        body, name=name, grid=(k // rb,),
        in_specs=[pl.BlockSpec((N_DEV, rb, ns), lambda i: (0, i, 0)), blk, blk, blk] + [_ANY] * (n_prev + len(deps)),
        out_specs=(blk, blk, blk, blk), out_shape=(out, out, out, out),
        input_output_aliases={4 + j: j for j in range(n_prev)},
        compiler_params=_cparams(1),
    )(recv, w, m, v, *(prev or ()), *deps)


def _adamw_rep(recvs, ws, ms, vs):
    depth = len(recvs)
    nt = len(REP_VECS)

    def body(*refs):
        r_refs = refs[:depth]
        w_refs, m_refs, v_refs = (refs[depth + i * nt:depth + (i + 1) * nt] for i in range(3))
        outs = refs[depth + 3 * nt:]
        for ti, (_, row, width, nrows) in enumerate(REP_VECS):
            for l in range(depth):
                g = r_refs[l][0, row:row + nrows, 0:width]
                for s in range(1, N_DEV):
                    g = g + r_refs[l][s, row:row + nrows, 0:width]
                pick = (lambda ref: ref[l]) if nrows > 1 else (lambda ref: ref[l:l + 1, :])
                delta, nm, nv = _adam_math(g, pick(w_refs[ti]), pick(m_refs[ti]), pick(v_refs[ti]))
                for o_ref, val in zip(outs[4 * ti:4 * ti + 4], (g, delta, nm, nv)):
                    if nrows > 1:
                        o_ref[l] = val
                    else:
                        o_ref[l:l + 1, :] = val

    ins = tuple(recvs) + tuple(ws) + tuple(ms) + tuple(vs)
    out_shape = tuple(jax.ShapeDtypeStruct(w.shape, f32) for w in ws for _ in range(4))
    return pl.pallas_call(
        body, name="adamw_rep", grid=(1,),
        in_specs=[_whole(a) for a in ins], out_specs=tuple(_whole(o) for o in out_shape), out_shape=out_shape,
        compiler_params=_cparams(1),
    )(*ins)


def _adamw_sg_w(recvs, w, m, v):
    depth = len(recvs)

    def body(*refs):
        r_refs = refs[:depth]
        w_ref, m_ref, v_ref = refs[depth:depth + 3]
        outs = refs[depth + 3:]
        for l in range(depth):
            for gi in range(SG_GROUPS):
                g = _slot_sum(r_refs[l], (gi,))
                delta, nm, nv = _adam_math(g, w_ref[l, gi], m_ref[l, gi], v_ref[l, gi])
                for o_ref, val in zip(outs, (g, delta, nm, nv)):
                    o_ref[l, gi] = val

    ins = tuple(recvs) + (w, m, v)
    out = jax.ShapeDtypeStruct(w.shape, f32)
    return pl.pallas_call(
        body, name="adamw_sg_w", grid=(1,),
        in_specs=[_whole(a) for a in ins], out_specs=tuple(_whole(out) for _ in range(4)), out_shape=(out,) * 4,
        compiler_params=_cparams(1),
    )(*ins)


def _rope_tables(positions):
    t = positions.shape[0]
    inv = 10000.0 ** (-jnp.arange(0, QK_ROPE, 2, dtype=f32) / QK_ROPE)
    ang = positions.astype(f32)[:, None] * inv
    cos, sin = jnp.cos(ang), jnp.sin(ang)
    tail = jnp.zeros((t, HEAD_PAD - KR_LO - QK_ROPE), f32)
    cos_t = jnp.concatenate([jnp.ones((t, KR_LO), f32), cos, cos, tail], axis=1)
    sin_t = jnp.concatenate([jnp.zeros((t, KR_LO), f32), -sin, sin, tail], axis=1)
    return cos_t, sin_t


def _bias_over_channels(sg_b_l):
    return jnp.broadcast_to(sg_b_l.T[:, :, None], (SG_CHUNK, SG_GROUPS, D_SG // SG_GROUPS)).reshape(SG_CHUNK, D_SG)


def _mixer_weights(gathered, rep, l):
    conv_out_w, sg_out_w, wuq_p, wk_p, wv_p, wo_p, conv_dw_w, ffn_dw_w = _asm_small(
        gathered["conv_out_w"], gathered["sg_out_w"], gathered["mla_w_uq"], gathered["mla_w_ukv"], gathered["mla_w_o"],
        gathered["conv_dw_w"], gathered["ffn_dw_w"])
    g_out = gathered["w_out"]
    w = dict(rep)
    w.update(
        l=l, w_in_p=_asm_w_in(gathered["w_in"]),
        conv_out_w=conv_out_w, sg_out_w=sg_out_w, wuq_p=wuq_p, wk_p=wk_p, wv_p=wv_p, wo_p=wo_p, conv_dw_w_p=conv_dw_w, ffn_dw_w_p=ffn_dw_w,
        w_out=g_out.reshape(g_out.shape[0] * g_out.shape[1], g_out.shape[2]),
        bexp=_bias_over_channels(rep["sg_b"][l]))
    return w


def _ffn_weights(gathered):
    g_down = gathered["ffn_w_down"]
    return dict(w_up=_asm_cols(gathered["ffn_w_up"], "asm_w_up"),
                w_down=g_down.reshape(g_down.shape[0] * g_down.shape[1], g_down.shape[2]))


def _mixer_fwd(x, w, cos_t, sin_t):
    l = w["l"]
    h, a_in, b_in, qlat, kvlat, krope, gates = _mix_in_fwd(l, x, w["mix_pre_g"], w["w_in_p"])
    s, z1 = _conv_fwd(l, a_in, w["conv_dw_w_p"], w["conv_dw_b"], w["conv_ln_g"], w["conv_ln_b"])
    um = _sgu_fwd(l, b_in, w["sg_ln_g"], w["sg_ln_b"], w["sg_w"], w["bexp"])
    q, k, v, qn, kvn = _mla_proj_fwd(l, qlat, kvlat, krope, cos_t, sin_t, w["mla_q_norm_g"], w["wuq_p"], w["mla_kv_norm_g"],
                                     w["wk_p"], w["wv_p"])
    o, lse = _attn_fwd(q, k, v)
    x1, merged, om = _merge_out_fwd(l, x, s, um, o, gates, w["conv_out_w"], w["sg_out_w"], w["wo_p"], w["w_out"], w["mix_post_g"])
    saved = dict(x=x, h=h, a_in=a_in, b_in=b_in, qlat=qlat, kvlat=kvlat, gates=gates, s=s, z1=z1, um=um, q=q, k=k, v=v, qn=qn,
                 kvn=kvn, o=o, lse=lse, x1=x1, merged=merged, om=om)
    return x1, saved


def _ffn_layer_fwd(x1, w):
    x2, h2, z, act, f = _ffn_fwd(w["l"], x1, w["ffn_pre_g"], w["w_up"], w["ffn_dw_w_p"], w["ffn_dw_b"], w["w_down"], w["ffn_post_g"])
    return x2, dict(h2=h2, z=z, act=act, f=f)


def _ffn_layer_bwd(dx2, sv, w, deps=()):
    l = w["l"]
    vec = {}
    dx1, df, dz, vec["ffn_post_g"], vec["ffn_pre_g"], vec["ffn_dw_b"], vec["d_fdw"] = _ffn_bwd(
        l, dx2, sv["x1"], sv["f"], sv["z"], w["w_up"], w["ffn_dw_w_p"], w["ffn_dw_b"], w["w_down"], w["ffn_pre_g"], w["ffn_post_g"],
        deps=deps)
    d_down = _wgrad(sv["act"], df, "wgrad_ffn_down", bf16)
    d_up = _wgrad(sv["h2"], dz, "wgrad_ffn_up")
    send = dict(ffn_w_up=_dis_cols(d_up, "dis_w_up"),
                ffn_w_down=d_down.reshape(N_DEV, d_down.shape[0] // N_DEV, d_down.shape[1]))
    return dx1, send, vec


def _mixer_bwd(dx1, sv, w, cos_t, sin_t, shard_cols, vec, deps=()):
    l = w["l"]
    d_fdw = vec.pop("d_fdw")
    dom, dgates, dya, dyb, dyc, ds, dum, do, vec["mix_post_g"] = _merge_out_bwd(
        l, dx1, sv["om"], sv["s"], sv["um"], sv["o"], sv["gates"], w["conv_out_w"], w["sg_out_w"], w["wo_p"], w["w_out"], w["mix_post_g"],
        deps=deps)
    d_out = _wgrad(sv["merged"], dom, "wgrad_w_out", bf16)
    d_co = _wgrad(sv["s"], dya, "wgrad_conv_out")
    d_so = _wgrad(sv["um"], dyb, "wgrad_sg_out")
    d_wo = _wgrad(sv["o"], dyc, "wgrad_w_o")

    dq, dk, dv = _attn_bwd(sv["q"], sv["k"], sv["v"], sv["o"], sv["lse"], do)
    dqb, dkb, dvb, dqlat, dkvlat, dkrope, vec["mla_q_norm_g"], vec["mla_kv_norm_g"] = _mla_proj_bwd(
        l, dq, dk, dv, sv["qlat"], sv["kvlat"], cos_t, sin_t, w["mla_q_norm_g"], w["wuq_p"], w["mla_kv_norm_g"], w["wk_p"], w["wv_p"])
    d_uq = _wgrad(sv["qn"], dqb, "wgrad_w_uq")
    d_uk = _wgrad(sv["kvn"], dkb, "wgrad_w_uk")
    d_uv = _wgrad(sv["kvn"], dvb, "wgrad_w_uv")

    db_in, dsg_w, dsgb, vec["sg_ln_g"], vec["sg_ln_b"] = _sgu_bwd(l, sv["b_in"], dum, w["sg_ln_g"], w["sg_ln_b"], w["sg_w"], w["bexp"])
    da_in, d_cdw, vec["conv_dw_b"], vec["conv_ln_g"], vec["conv_ln_b"] = _conv_bwd(
        l, sv["a_in"], sv["z1"], ds, w["conv_dw_w_p"], w["conv_ln_g"], w["conv_ln_b"])

    dsegs = (da_in, db_in, dqlat, dkvlat, dkrope, dgates)
    dx, vec["mix_pre_g"] = _mix_in_bwd(l, sv["x"], w["mix_pre_g"], dx1, dsegs, w["w_in_p"])
    names = ("a", "b", "q", "kv", "kr", "g")
    d_in_segs = [_wgrad(sv["h"], dseg, "wgrad_w_in_" + nm) for nm, dseg in zip(names, dsegs)]

    send = {}
    send["w_in"] = _dis_w_in(d_in_segs, shard_cols["w_in"])
    (send["conv_out_w"], send["sg_out_w"], send["mla_w_uq"], send["mla_w_ukv"], send["mla_w_o"], send["conv_dw_w"],
     send["ffn_dw_w"]) = _dis_small(d_co, d_so, d_uq, d_uk, d_uv, d_wo, d_cdw, d_fdw)
    send["w_out"] = d_out.reshape(N_DEV, d_out.shape[0] // N_DEV, d_out.shape[1])
    rep_pack = _pack_rep([vec[n] for n, _, _, _ in REP_VECS[:-1]], dsgb)
    return dx, send, rep_pack, dsg_w


def kernel(x, positions, mix_pre_g, mix_post_g, ffn_pre_g, ffn_post_g, w_in, conv_dw_w, conv_dw_b, conv_ln_g, conv_ln_b, conv_out_w, sg_ln_g, sg_ln_b, sg_w, sg_b, sg_out_w, mla_q_norm_g, mla_w_uq, mla_kv_norm_g, mla_w_ukv, mla_w_o, w_out, ffn_w_up, ffn_dw_w, ffn_dw_b, ffn_w_down, loss_target, m_mix_pre_g, m_mix_post_g, m_ffn_pre_g, m_ffn_post_g, m_w_in, m_conv_dw_w, m_conv_dw_b, m_conv_ln_g, m_conv_ln_b, m_conv_out_w, m_sg_ln_g, m_sg_ln_b, m_sg_w, m_sg_b, m_sg_out_w, m_mla_q_norm_g, m_mla_w_uq, m_mla_kv_norm_g, m_mla_w_ukv, m_mla_w_o, m_w_out, m_ffn_w_up, m_ffn_dw_w, m_ffn_dw_b, m_ffn_w_down, v_mix_pre_g, v_mix_post_g, v_ffn_pre_g, v_ffn_post_g, v_w_in, v_conv_dw_w, v_conv_dw_b, v_conv_ln_g, v_conv_ln_b, v_conv_out_w, v_sg_ln_g, v_sg_ln_b, v_sg_w, v_sg_b, v_sg_out_w, v_mla_q_norm_g, v_mla_w_uq, v_mla_kv_norm_g, v_mla_w_ukv, v_mla_w_o, v_w_out, v_ffn_w_up, v_ffn_dw_w, v_ffn_dw_b, v_ffn_w_down):
    args = (x, positions, mix_pre_g, mix_post_g, ffn_pre_g, ffn_post_g, w_in, conv_dw_w, conv_dw_b, conv_ln_g, conv_ln_b, conv_out_w, sg_ln_g, sg_ln_b, sg_w, sg_b, sg_out_w, mla_q_norm_g, mla_w_uq, mla_kv_norm_g, mla_w_ukv, mla_w_o, w_out, ffn_w_up, ffn_dw_w, ffn_dw_b, ffn_w_down, loss_target, m_mix_pre_g, m_mix_post_g, m_ffn_pre_g, m_ffn_post_g, m_w_in, m_conv_dw_w, m_conv_dw_b, m_conv_ln_g, m_conv_ln_b, m_conv_out_w, m_sg_ln_g, m_sg_ln_b, m_sg_w, m_sg_b, m_sg_out_w, m_mla_q_norm_g, m_mla_w_uq, m_mla_kv_norm_g, m_mla_w_ukv, m_mla_w_o, m_w_out, m_ffn_w_up, m_ffn_dw_w, m_ffn_dw_b, m_ffn_w_down, v_mix_pre_g, v_mix_post_g, v_ffn_pre_g, v_ffn_post_g, v_w_in, v_conv_dw_w, v_conv_dw_b, v_conv_ln_g, v_conv_ln_b, v_conv_out_w, v_sg_ln_g, v_sg_ln_b, v_sg_w, v_sg_b, v_sg_out_w, v_mla_q_norm_g, v_mla_w_uq, v_mla_kv_norm_g, v_mla_w_ukv, v_mla_w_o, v_w_out, v_ffn_w_up, v_ffn_dw_w, v_ffn_dw_b, v_ffn_w_down)
    n_in = len(IN_NAMES)
    a = dict(zip(IN_NAMES, args[:n_in]))
    target = args[n_in]
    n_w = len(WEIGHTS)
    m_in = dict(zip(WEIGHTS, args[n_in + 1:n_in + 1 + n_w]))
    v_in = dict(zip(WEIGHTS, args[n_in + 1 + n_w:n_in + 1 + 2 * n_w]))
    depth = a["mix_pre_g"].shape[0]
    rep = {n: a[n] for n in WEIGHTS if n not in SHARDED}
    shard_cols = {n: a[n].shape[2] for n in SHARDED}

    starts, token = [], ()
    for l in range(depth):
        for group, tag in ((MIX_GROUP, "mix"), (FFN_BIG, "ffn")):
            wire = [(a[n][l] if n in WIRE_F32 else a[n][l].astype(bf16), "gather") for n in group]
            starts.append(_exchange_start(wire, "gather_%s_weights_%d" % (tag, l), deps=token))
            token = (starts[-1]["token"],)

    cos_t, sin_t = _rope_tables(a["positions"][0])
    xl = a["x"][0]
    ws, saved = [], []
    after = token
    for l in range(depth):
        w = _mixer_weights(dict(zip(MIX_GROUP, _exchange_wait(starts[2 * l], after=after))), rep, l)
        x1, sv = _mixer_fwd(xl, w, cos_t, sin_t)
        w.update(_ffn_weights(dict(zip(FFN_BIG, _exchange_wait(starts[2 * l + 1], after=(x1,))))))
        xl, sv_ffn = _ffn_layer_fwd(x1, w)
        sv.update(sv_ffn)
        ws.append(w)
        saved.append(sv)
        after = (xl,)
    dx, loss_part = _loss_fwd_bwd(xl, target[0])
    loss = lax.psum(loss_part, AXES)

    outs = {}
    rep_recvs, sgw_recvs = [None] * depth, [None] * depth

    def finish(l, h_ffn, h_mix, after):
        for names, handle in ((FFN_BIG, h_ffn), (MIX_GROUP, h_mix)):
            got = _exchange_wait(handle, after=after)
            for n, recv in zip(names, got):
                outs[n] = _adamw_shard(l, recv, a[n], m_in[n], v_in[n], outs.get(n), "adamw_" + n)
            after = tuple(outs[n][0] for n in names)
        rep_recvs[l], sgw_recvs[l] = got[len(MIX_GROUP)], got[len(MIX_GROUP) + 1]
        return after

    token, prev, after = (), None, ()
    for l in reversed(range(depth)):
        dx1, send_ffn, vec = _ffn_layer_bwd(dx, saved[l], ws[l], deps=token)
        h_ffn = _exchange_start([(send_ffn[n], "scatter") for n in FFN_BIG], "exchange_ffn_grads_%d" % l)
        dx, send_mix, rep_pack, dsg_w = _mixer_bwd(dx1, saved[l], ws[l], cos_t, sin_t, shard_cols, vec, deps=(h_ffn["token"],))
        h_mix = _exchange_start([(send_mix[n], "scatter") for n in MIX_GROUP] + [(rep_pack, "gather"), (dsg_w, "gather")],
                                "exchange_mix_grads_%d" % l)
        token = (h_mix["token"],)
        if prev is not None:
            after = finish(*prev, after=(dx,) + token)
        prev = (l, h_ffn, h_mix)
    finish(*prev, after=after)
    vec_names = [n for n, _, _, _ in REP_VECS]
    rep_outs = _adamw_rep(rep_recvs, [a[n] for n in vec_names], [m_in[n] for n in vec_names], [v_in[n] for n in vec_names])
    for i, n in enumerate(vec_names):
        outs[n] = rep_outs[4 * i:4 * i + 4]
    outs["sg_w"] = _adamw_sg_w(sgw_recvs, a["sg_w"], m_in["sg_w"], v_in["sg_w"])

    grad_w, delta_w, new_m, new_v = ([outs[n][j] for n in WEIGHTS] for j in range(4))
    return (loss, dx[None], *grad_w, *delta_w, *new_m, *new_v)
```

```python
import math

import jax
import jax.numpy as jnp
from jax import lax
from jax.experimental import pallas as pl
from jax.experimental.pallas import tpu as pltpu

f32 = jnp.float32
bf16 = jnp.bfloat16

N_DEV = 8
AXES = ("x", "y", "c")
EPS = 1e-6
D_CONV = 512
CONV_K = 31
CONV_HALO = 32
D_SG = 512
SG_GROUPS = 4
SG_CHUNK = 128
N_HEADS = 8
QK_NOPE = 64
QK_ROPE = 32
V_HEAD = 64
HEAD_PAD = 128
Q_LORA = 384
KV_LORA = 256
D_FF = 2816
FFN_K = 3
FFN_HALO = 8
ATT_SCALE = (QK_NOPE + QK_ROPE) ** -0.5
NEG = float(jnp.finfo(jnp.float32).min)

ADAM_LR = 0.001
ADAM_B1 = 0.9
ADAM_B2 = 0.999
ADAM_EPS = 1e-08
ADAM_WD = 0.01
ADAM_STEP = 10

LANES = 128
VMEM_MB = 56

REF_CUTS = (0, 1024, 2048, 2432, 2688, 2720, 5792)
SEGS = ((0, 1024), (1024, 1024), (2048, 384), (2432, 256), (2688, 128), (2816, 3072))
D_IN = 5792
D_IN_PAD = 5888
KR_LO = 64
SEG_INNER = (0, 0, 0, 0, KR_LO, 0)

IN_NAMES = ['x', 'positions', 'mix_pre_g', 'mix_post_g', 'ffn_pre_g', 'ffn_post_g', 'w_in', 'conv_dw_w', 'conv_dw_b', 'conv_ln_g', 'conv_ln_b', 'conv_out_w', 'sg_ln_g', 'sg_ln_b', 'sg_w', 'sg_b', 'sg_out_w', 'mla_q_norm_g', 'mla_w_uq', 'mla_kv_norm_g', 'mla_w_ukv', 'mla_w_o', 'w_out', 'ffn_w_up', 'ffn_dw_w', 'ffn_dw_b', 'ffn_w_down']
WEIGHTS = IN_NAMES[2:]
SHARDED = ("w_in", "conv_dw_w", "conv_out_w", "sg_out_w", "mla_w_uq", "mla_w_ukv", "mla_w_o", "w_out", "ffn_w_up", "ffn_dw_w",
           "ffn_w_down")
FFN_BIG = ("ffn_w_up", "ffn_w_down")
MIX_GROUP = tuple(n for n in SHARDED if n not in FFN_BIG)
WIRE_F32 = ("conv_dw_w", "ffn_dw_w")
REP_VECS = (("mix_pre_g", 0, 1024, 1), ("mix_post_g", 1, 1024, 1), ("ffn_pre_g", 2, 1024, 1), ("ffn_post_g", 3, 1024, 1),
            ("conv_dw_b", 4, 512, 1), ("conv_ln_g", 5, 512, 1), ("conv_ln_b", 6, 512, 1), ("sg_ln_g", 7, 512, 1),
            ("sg_ln_b", 8, 512, 1), ("mla_q_norm_g", 9, 384, 1), ("mla_kv_norm_g", 10, 256, 1), ("ffn_dw_b", 11, 5632, 1),
            ("sg_b", 12, 128, 4))
REP_ROWS = 16
REP_W = 5632


def _cparams(n_axes):
    return pltpu.CompilerParams(dimension_semantics=("arbitrary",) * n_axes, vmem_limit_bytes=VMEM_MB * 2 ** 20)


def _rows(tb, n):
    return pl.BlockSpec((tb, n), lambda i: (i, 0))


def _rows_rev(tb, n, nb):
    return pl.BlockSpec((tb, n), lambda i: (nb - 1 - i, 0))


def _const(shape):
    nd = len(shape)
    return pl.BlockSpec(shape, lambda *_: (0,) * nd, pipeline_mode=pl.Buffered(1))


def _acc(shape):
    nd = len(shape)
    return pl.BlockSpec(shape, lambda *_: (0,) * nd)


def _whole(arr):
    return pl.BlockSpec(arr.shape, lambda *_: (0,) * arr.ndim)


def _dot(a, b):
    return jnp.dot(a, b, preferred_element_type=f32)


def _dot_nt(a, b):
    return lax.dot_general(a, b, (((1,), (1,)), ((), ())), preferred_element_type=f32)


def _dot_tn(a, b):
    return lax.dot_general(a, b, (((0,), (0,)), ((), ())), preferred_element_type=f32)


def _mean(x):
    return jnp.mean(x, axis=-1, keepdims=True)


def _colsum(x):
    return jnp.sum(x, axis=0, keepdims=True)


def _rms_fwd(x, g):
    return x * lax.rsqrt(_mean(x * x) + EPS) * g


def _rms_bwd(x, g, dy):
    r = lax.rsqrt(_mean(x * x) + EPS)
    n = x * r
    dn = dy * g
    return r * (dn - n * _mean(dn * n)), _colsum(dy * n)


def _ln_stats(x):
    mu = _mean(x)
    d = x - mu
    rstd = lax.rsqrt(_mean(d * d) + EPS)
    return d * rstd, rstd


def _ln_bwd(xhat, rstd, g, dy):
    dxh = dy * g
    return rstd * (dxh - _mean(dxh) - xhat * _mean(dxh * xhat))


_GELU_C0 = math.sqrt(2.0 / math.pi)
_GELU_C1 = 0.044715


def _gelu(x):
    t = jnp.tanh(_GELU_C0 * (x + _GELU_C1 * (x * x * x)))
    return 0.5 * x * (1.0 + t)


def _gelu_and_grad(x):
    x2 = x * x
    t = jnp.tanh(_GELU_C0 * (x + _GELU_C1 * (x2 * x)))
    g = 0.5 * x * (1.0 + t)
    dg = 0.5 * (1.0 + t) + 0.5 * x * (1.0 - t * t) * (_GELU_C0 * (1.0 + 3.0 * _GELU_C1 * x2))
    return g, dg


def _swap_rope_halves(x):
    n = x.shape[1]
    half = QK_ROPE // 2
    lane = lax.broadcasted_iota(jnp.int32, x.shape, 1) % HEAD_PAD
    first = (lane >= KR_LO) & (lane < KR_LO + half)
    second = (lane >= KR_LO + half) & (lane < KR_LO + QK_ROPE)
    return jnp.where(first, pltpu.roll(x, n - half, 1), jnp.where(second, pltpu.roll(x, half, 1), 0.0))


def _tb(t):
    return min(256, t)


_HBM = pl.BlockSpec(memory_space=pltpu.HBM)
_SEM = pl.BlockSpec(memory_space=pltpu.SEMAPHORE)
_ANY = pl.BlockSpec(memory_space=pl.ANY)
_EFFECT = pltpu.SideEffectType.DATAFLOW_SIDE_EFFECTING


def _exchange_copies(modes, ins, lands, send_sems, recv_sems, loc_sems):
    x, y, c = lax.axis_index("x"), lax.axis_index("y"), lax.axis_index("c")
    me = 4 * x + 2 * y + c
    copies = []
    for a, mode in enumerate(modes):
        def src(dst_index, a=a, mode=mode):
            return ins[a].at[dst_index] if mode == "scatter" else ins[a]
        copies.append(pltpu.make_async_copy(src(me), lands[a].at[me], loc_sems.at[a]))
        for k in range(1, N_DEV):
            px = 1 - x if (k >> 2) & 1 else x
            py = 1 - y if (k >> 1) & 1 else y
            pc = 1 - c if k & 1 else c
            copies.append(pltpu.make_async_remote_copy(
                src_ref=src(4 * px + 2 * py + pc), dst_ref=lands[a].at[me],
                send_sem=send_sems.at[a * (N_DEV - 1) + k - 1], recv_sem=recv_sems.at[a * (N_DEV - 1) + k - 1],
                device_id=(px, py, pc), device_id_type=pl.DeviceIdType.MESH))
    return copies


def _exchange_start(ops, name, deps=()):
    n = len(ops)
    arrs = [arr for arr, _ in ops]
    modes = [mode for _, mode in ops]
    lands = [lax.empty((N_DEV,) + arr.shape if mode == "gather" else arr.shape, arr.dtype) for arr, mode in ops]

    def body(*refs):
        ins, land_refs = refs[:n], refs[n:2 * n]
        send_sems, recv_sems, loc_sems = refs[2 * n + len(deps):2 * n + len(deps) + 3]
        for cp in _exchange_copies(modes, ins, land_refs, send_sems, recv_sems, loc_sems):
            cp.start()
        refs[-1][...] = jnp.zeros((8, LANES), f32)

    n_rem = n * (N_DEV - 1)
    res = pl.pallas_call(
        body, name=name,
        out_shape=(pltpu.SemaphoreType.DMA((n_rem,)), pltpu.SemaphoreType.DMA((n_rem,)), pltpu.SemaphoreType.DMA((n,)),
                   *[pltpu.HBM(x.shape, x.dtype) for x in arrs + lands], jax.ShapeDtypeStruct((8, LANES), f32)),
        in_specs=[_HBM] * (2 * n) + [_ANY] * len(deps),
        out_specs=(_SEM, _SEM, _SEM, *[_HBM] * (2 * n), pl.BlockSpec(memory_space=pltpu.VMEM)),
        input_output_aliases={i: 3 + i for i in range(2 * n)},
        compiler_params=pltpu.CompilerParams(has_side_effects=_EFFECT),
    )(*[pltpu.with_memory_space_constraint(x, pltpu.HBM) for x in arrs + lands], *deps)
    return dict(modes=modes, sems=res[:3], thru=res[3:3 + 2 * n], token=res[-1], name=name)


def _exchange_wait(handle, after=()):
    modes, thru = handle["modes"], handle["thru"]
    n = len(modes)

    def body(*refs):
        ins, land_refs = refs[:n], refs[n:2 * n]
        send_sems, recv_sems, loc_sems = refs[2 * n:2 * n + 3]
        for cp in _exchange_copies(modes, ins, land_refs, send_sems, recv_sems, loc_sems):
            cp.wait()

    res = pl.pallas_call(
        body, name=handle["name"] + "_wait",
        out_shape=tuple(pltpu.HBM(x.shape, x.dtype) for x in thru),
        in_specs=[_HBM] * (2 * n) + [_SEM] * 3 + [_ANY] * len(after),
        out_specs=tuple([_HBM] * (2 * n)),
        input_output_aliases={i: i for i in range(2 * n)},
        compiler_params=pltpu.CompilerParams(has_side_effects=_EFFECT),
    )(*thru, *handle["sems"], *after)
    return res[n:]


def _w_in_pieces(ns):
    out = []
    for e in range(N_DEV):
        lo, hi = ns * e, ns * (e + 1)
        for s in range(len(SEGS)):
            a, b = max(lo, REF_CUTS[s]), min(hi, REF_CUTS[s + 1])
            if a < b:
                inner = SEG_INNER[s] + a - REF_CUTS[s]
                out.append((e, a - lo, b - lo, s, inner, inner + b - a))
    return out


def _asm_w_in(g):
    _, d, ns = g.shape
    rb = 256
    pieces = _w_in_pieces(ns)

    def body(g_ref, o_ref):
        kr = SEGS[4][0]
        o_ref[:, kr:kr + KR_LO] = jnp.zeros((rb, KR_LO), g.dtype)
        o_ref[:, kr + KR_LO + QK_ROPE:kr + HEAD_PAD] = jnp.zeros((rb, HEAD_PAD - KR_LO - QK_ROPE), g.dtype)
        for e, s0, s1, seg, d0, d1 in pieces:
            off = SEGS[seg][0]
            o_ref[:, off + d0:off + d1] = g_ref[e, :, s0:s1]

    return pl.pallas_call(
        body, name="asm_w_in", grid=(d // rb,),
        in_specs=[pl.BlockSpec((N_DEV, rb, ns), lambda i: (0, i, 0))],
        out_specs=_rows(rb, D_IN_PAD), out_shape=jax.ShapeDtypeStruct((d, D_IN_PAD), g.dtype),
        compiler_params=_cparams(1),
    )(g)


def _dis_w_in(dsegs, ns):
    d = dsegs[0].shape[0]
    rb = 256
    pieces = _w_in_pieces(ns)

    def body(*refs):
        seg_refs, o_ref = refs[:len(SEGS)], refs[len(SEGS)]
        for e, s0, s1, seg, d0, d1 in pieces:
            o_ref[e, :, s0:s1] = seg_refs[seg][:, d0:d1].astype(bf16)

    return pl.pallas_call(
        body, name="dis_w_in", grid=(d // rb,),
        in_specs=[_rows(rb, wd) for _, wd in SEGS],
        out_specs=pl.BlockSpec((N_DEV, rb, ns), lambda i: (0, i, 0)),
        out_shape=jax.ShapeDtypeStruct((N_DEV, d, ns), bf16),
        compiler_params=_cparams(1),
    )(*dsegs)


def _asm_cols(g, name):
    _, k, ns = g.shape
    rb = 256

    def body(g_ref, o_ref):
        for e in range(N_DEV):
            o_ref[:, e * ns:(e + 1) * ns] = g_ref[e]

    return pl.pallas_call(
        body, name=name, grid=(k // rb,),
        in_specs=[pl.BlockSpec((N_DEV, rb, ns), lambda i: (0, i, 0))],
        out_specs=_rows(rb, N_DEV * ns), out_shape=jax.ShapeDtypeStruct((k, N_DEV * ns), g.dtype),
        compiler_params=_cparams(1),
    )(g)


def _dis_cols(w, name):
    k, n = w.shape
    ns = n // N_DEV
    rb = 256

    def body(w_ref, o_ref):
        for e in range(N_DEV):
            o_ref[e] = w_ref[:, e * ns:(e + 1) * ns].astype(bf16)

    return pl.pallas_call(
        body, name=name, grid=(k // rb,),
        in_specs=[_rows(rb, n)],
        out_specs=pl.BlockSpec((N_DEV, rb, ns), lambda i: (0, i, 0)),
        out_shape=jax.ShapeDtypeStruct((N_DEV, k, ns), bf16),
        compiler_params=_cparams(1),
    )(w)


def _asm_small(g_conv_out, g_sg_out, g_uq, g_ukv, g_wo, g_cdw, g_fdw):
    d = g_conv_out.shape[2] * N_DEV
    hw = N_HEADS * HEAD_PAD
    hq = QK_NOPE + QK_ROPE
    ff2 = g_fdw.shape[2] * N_DEV
    cw, fw = g_cdw.shape[2], g_fdw.shape[2]

    def body(co_ref, so_ref, uq_ref, ukv_ref, wo_ref, cdw_ref, fdw_ref, o_co, o_so, o_uq, o_k, o_v, o_wo, o_cdw, o_fdw):
        o_cdw[CONV_K:CONV_HALO, :] = jnp.zeros((CONV_HALO - CONV_K, D_CONV), f32)
        o_fdw[FFN_K:FFN_HALO, :] = jnp.zeros((FFN_HALO - FFN_K, ff2), f32)
        for e in range(N_DEV):
            cs = e * HEAD_PAD
            o_co[:, cs:cs + HEAD_PAD] = co_ref[e]
            o_so[:, cs:cs + HEAD_PAD] = so_ref[e]
            o_uq[:, cs:cs + hq] = uq_ref[e]
            o_uq[:, cs + hq:cs + HEAD_PAD] = jnp.zeros((Q_LORA, HEAD_PAD - hq), bf16)
            o_k[:, cs:cs + QK_NOPE] = ukv_ref[e, :, 0:QK_NOPE]
            o_k[:, cs + QK_NOPE:cs + HEAD_PAD] = jnp.zeros((KV_LORA, HEAD_PAD - QK_NOPE), bf16)
            o_v[:, cs:cs + V_HEAD] = ukv_ref[e, :, QK_NOPE:QK_NOPE + V_HEAD]
            o_v[:, cs + V_HEAD:cs + HEAD_PAD] = jnp.zeros((KV_LORA, HEAD_PAD - V_HEAD), bf16)
            for h in range(N_HEADS):
                o_wo[h * HEAD_PAD:h * HEAD_PAD + V_HEAD, cs:cs + HEAD_PAD] = wo_ref[e, h * V_HEAD:(h + 1) * V_HEAD, :]
                o_wo[h * HEAD_PAD + V_HEAD:(h + 1) * HEAD_PAD, cs:cs + HEAD_PAD] = jnp.zeros((HEAD_PAD - V_HEAD, HEAD_PAD), bf16)
            o_cdw[0:CONV_K, e * cw:(e + 1) * cw] = cdw_ref[e]
            o_fdw[0:FFN_K, e * fw:(e + 1) * fw] = fdw_ref[e]

    ins = (g_conv_out, g_sg_out, g_uq, g_ukv, g_wo, g_cdw, g_fdw)
    out_shape = (jax.ShapeDtypeStruct((D_CONV, d), bf16), jax.ShapeDtypeStruct((D_SG, d), bf16), jax.ShapeDtypeStruct((Q_LORA, hw), bf16),
                 jax.ShapeDtypeStruct((KV_LORA, hw), bf16), jax.ShapeDtypeStruct((KV_LORA, hw), bf16), jax.ShapeDtypeStruct((hw, d), bf16),
                 jax.ShapeDtypeStruct((CONV_HALO, D_CONV), f32), jax.ShapeDtypeStruct((FFN_HALO, ff2), f32))
    return pl.pallas_call(
        body, name="asm_small", grid=(1,),
        in_specs=[_whole(a) for a in ins], out_specs=tuple(_whole(o) for o in out_shape), out_shape=out_shape,
        compiler_params=_cparams(1),
    )(*ins)


def _dis_small(d_co, d_so, d_uq, d_k, d_v, d_wo, d_cdw, d_fdw):
    d = d_co.shape[1]
    hq = QK_NOPE + QK_ROPE
    cw, fw = D_CONV // N_DEV, d_fdw.shape[1] // N_DEV

    def body(co_ref, so_ref, uq_ref, k_ref, v_ref, wo_ref, cdw_ref, fdw_ref, o_co, o_so, o_uq, o_ukv, o_wo, o_cdw, o_fdw):
        for e in range(N_DEV):
            cs = e * HEAD_PAD
            o_co[e] = co_ref[:, cs:cs + HEAD_PAD].astype(bf16)
            o_so[e] = so_ref[:, cs:cs + HEAD_PAD].astype(bf16)
            o_uq[e] = uq_ref[:, cs:cs + hq].astype(bf16)
            o_ukv[e, :, 0:QK_NOPE] = k_ref[:, cs:cs + QK_NOPE].astype(bf16)
            o_ukv[e, :, QK_NOPE:QK_NOPE + V_HEAD] = v_ref[:, cs:cs + V_HEAD].astype(bf16)
            for h in range(N_HEADS):
                o_wo[e, h * V_HEAD:(h + 1) * V_HEAD, :] = wo_ref[h * HEAD_PAD:h * HEAD_PAD + V_HEAD, cs:cs + HEAD_PAD].astype(bf16)
            o_cdw[e] = cdw_ref[0:CONV_K, e * cw:(e + 1) * cw]
            o_fdw[e] = fdw_ref[0:FFN_K, e * fw:(e + 1) * fw]

    ins = (d_co, d_so, d_uq, d_k, d_v, d_wo, d_cdw, d_fdw)
    out_shape = (jax.ShapeDtypeStruct((N_DEV, D_CONV, d // N_DEV), bf16), jax.ShapeDtypeStruct((N_DEV, D_SG, d // N_DEV), bf16),
                 jax.ShapeDtypeStruct((N_DEV, Q_LORA, hq), bf16), jax.ShapeDtypeStruct((N_DEV, KV_LORA, QK_NOPE + V_HEAD), bf16),
                 jax.ShapeDtypeStruct((N_DEV, N_HEADS * V_HEAD, d // N_DEV), bf16), jax.ShapeDtypeStruct((N_DEV, CONV_K, cw), f32),
                 jax.ShapeDtypeStruct((N_DEV, FFN_K, fw), f32))
    return pl.pallas_call(
        body, name="dis_small", grid=(1,),
        in_specs=[_whole(a) for a in ins], out_specs=tuple(_whole(o) for o in out_shape), out_shape=out_shape,
        compiler_params=_cparams(1),
    )(*ins)


def _pack_rep(vec_grads, dsgb):
    def body(*refs):
        o_ref = refs[-1]
        o_ref[...] = jnp.zeros((REP_ROWS, REP_W), f32)
        for (_, row, width, nrows), ref in zip(REP_VECS, refs[:-1]):
            o_ref[row:row + nrows, 0:width] = ref[0:nrows, :]

    ins = tuple(vec_grads) + (dsgb,)
    return pl.pallas_call(
        body, name="pack_rep", grid=(1,),
        in_specs=[_whole(a) for a in ins], out_specs=pl.BlockSpec((REP_ROWS, REP_W), lambda i: (0, 0)),
        out_shape=jax.ShapeDtypeStruct((REP_ROWS, REP_W), f32), compiler_params=_cparams(1),
    )(*ins)


def _mix_in_fwd(l, x, g1, w_in_p):
    t, d = x.shape
    tb = _tb(t)

    def body(x_ref, g_ref, w_ref, h_ref, *outs):
        h = _rms_fwd(x_ref[...], g_ref[l:l + 1, :]).astype(bf16)
        h_ref[...] = h
        for (off, wd), o_ref in zip(SEGS, outs):
            o_ref[...] = _dot(h, w_ref[:, off:off + wd])

    return pl.pallas_call(
        body, name="mix_in_fwd", grid=(t // tb,),
        in_specs=[_rows(tb, d), _const(g1.shape), _const((d, D_IN_PAD))],
        out_specs=tuple([_rows(tb, d)] + [_rows(tb, wd) for _, wd in SEGS]),
        out_shape=tuple([jax.ShapeDtypeStruct((t, d), bf16)] + [jax.ShapeDtypeStruct((t, wd), f32) for _, wd in SEGS]),
        compiler_params=_cparams(1),
    )(x, g1, w_in_p)


def _conv_fwd(l, a_in, dw_w, dw_b, ln_g, ln_b):
    t = a_in.shape[0]
    tb = _tb(t)

    def body(a_ref, w_ref, b_ref, g_ref, be_ref, s_ref, z1_ref, win):
        @pl.when(pl.program_id(0) == 0)
        def _():
            win[0:CONV_HALO, :] = jnp.zeros((CONV_HALO, D_CONV), f32)

        a = a_ref[...]
        win[CONV_HALO:CONV_HALO + tb, :] = a[:, :D_CONV] * jax.nn.sigmoid(a[:, D_CONV:])
        for r0 in range(0, tb, LANES):
            for c0 in range(0, D_CONV, LANES):
                acc = jnp.broadcast_to(b_ref[l:l + 1, c0:c0 + LANES], (LANES, LANES))
                for j in range(CONV_K):
                    acc = acc + w_ref[j:j + 1, c0:c0 + LANES] * win[pl.ds(r0 + CONV_HALO - CONV_K + 1 + j, LANES), c0:c0 + LANES]
                z1_ref[r0:r0 + LANES, c0:c0 + LANES] = acc
        zh, _ = _ln_stats(z1_ref[...])
        zl = zh * g_ref[l:l + 1, :] + be_ref[l:l + 1, :]
        s_ref[...] = (zl * jax.nn.sigmoid(zl)).astype(bf16)
        win[0:CONV_HALO, :] = win[tb:tb + CONV_HALO, :]

    return pl.pallas_call(
        body, name="conv_fwd", grid=(t // tb,),
        in_specs=[_rows(tb, 2 * D_CONV), _const((CONV_HALO, D_CONV)), _const(dw_b.shape), _const(ln_g.shape), _const(ln_b.shape)],
        out_specs=(_rows(tb, D_CONV), _rows(tb, D_CONV)),
        out_shape=(jax.ShapeDtypeStruct((t, D_CONV), bf16), jax.ShapeDtypeStruct((t, D_CONV), f32)),
        scratch_shapes=[pltpu.VMEM((tb + CONV_HALO, D_CONV), f32)],
        compiler_params=_cparams(1),
    )(a_in, dw_w, dw_b, ln_g, ln_b)


def _tril_mask():
    r = lax.broadcasted_iota(jnp.int32, (SG_CHUNK, SG_CHUNK), 0)
    c = lax.broadcasted_iota(jnp.int32, (SG_CHUNK, SG_CHUNK), 1)
    return r >= c


def _sgu_fwd(l, b_in, ln_g, ln_b, sg_w, bexp):
    t = b_in.shape[0]
    tb = _tb(t)
    gw = D_SG // SG_GROUPS

    def body(b_ref, g_ref, be_ref, w_ref, bexp_ref, um_ref):
        gl = _gelu(b_ref[...])
        u = gl[:, :D_SG]
        vh, _ = _ln_stats(gl[:, D_SG:])
        vn = (vh * g_ref[l:l + 1, :] + be_ref[l:l + 1, :]).astype(bf16)
        tri = _tril_mask()
        for g in range(SG_GROUPS):
            wg = jnp.where(tri, w_ref[l, g], 0.0).astype(bf16)
            cs = slice(g * gw, (g + 1) * gw)
            for r0 in range(0, tb, SG_CHUNK):
                rs = slice(r0, r0 + SG_CHUNK)
                mixed = _dot(wg, vn[rs, cs]) + bexp_ref[:, cs]
                um_ref[rs, cs] = (u[rs, cs] * mixed).astype(bf16)

    return pl.pallas_call(
        body, name="sgu_fwd", grid=(t // tb,),
        in_specs=[_rows(tb, 2 * D_SG), _const(ln_g.shape), _const(ln_b.shape), _const(sg_w.shape), _const((SG_CHUNK, D_SG))],
        out_specs=_rows(tb, D_SG),
        out_shape=jax.ShapeDtypeStruct((t, D_SG), bf16),
        compiler_params=_cparams(1),
    )(b_in, ln_g, ln_b, sg_w, bexp)


def _mla_proj_fwd(l, qlat, kvlat, krope, cos_t, sin_t, gq, wuq_p, gkv, wk_p, wv_p):
    t = qlat.shape[0]
    tb = _tb(t)
    hw = N_HEADS * HEAD_PAD

    def body(ql_ref, kvl_ref, kr_ref, c_ref, s_ref, gq_ref, wq_ref, gkv_ref, wk_ref, wv_ref, q_ref, k_ref, v_ref, qn_ref, kvn_ref):
        cos_b, sin_b = c_ref[...], s_ref[...]
        qn = _rms_fwd(ql_ref[...], gq_ref[l:l + 1, :]).astype(bf16)
        qn_ref[...] = qn
        q = _dot(qn, wq_ref[...])
        sw = _swap_rope_halves(q)
        for h in range(N_HEADS):
            hs = slice(h * HEAD_PAD, (h + 1) * HEAD_PAD)
            q_ref[:, hs] = (q[:, hs] * cos_b + sw[:, hs] * sin_b).astype(bf16)
        kvn = _rms_fwd(kvl_ref[...], gkv_ref[l:l + 1, :]).astype(bf16)
        kvn_ref[...] = kvn
        kr = kr_ref[...]
        kpe = kr * cos_b + _swap_rope_halves(kr) * sin_b
        k = _dot(kvn, wk_ref[...])
        for h in range(N_HEADS):
            hs = slice(h * HEAD_PAD, (h + 1) * HEAD_PAD)
            k_ref[:, hs] = (k[:, hs] + kpe).astype(bf16)
        v_ref[...] = _dot(kvn, wv_ref[...]).astype(bf16)

    return pl.pallas_call(
        body, name="mla_proj_fwd", grid=(t // tb,),
        in_specs=[_rows(tb, Q_LORA), _rows(tb, KV_LORA), _rows(tb, HEAD_PAD), _rows(tb, HEAD_PAD), _rows(tb, HEAD_PAD),
                  _const(gq.shape), _const((Q_LORA, hw)), _const(gkv.shape), _const((KV_LORA, hw)), _const((KV_LORA, hw))],
        out_specs=(_rows(tb, hw), _rows(tb, hw), _rows(tb, hw), _rows(tb, Q_LORA), _rows(tb, KV_LORA)),
        out_shape=(jax.ShapeDtypeStruct((t, hw), bf16), jax.ShapeDtypeStruct((t, hw), bf16), jax.ShapeDtypeStruct((t, hw), bf16),
                   jax.ShapeDtypeStruct((t, Q_LORA), bf16), jax.ShapeDtypeStruct((t, KV_LORA), bf16)),
        compiler_params=_cparams(1),
    )(qlat, kvlat, krope, cos_t, sin_t, gq, wuq_p, gkv, wk_p, wv_p)


def _diag_mask(tq):
    return lax.broadcasted_iota(jnp.int32, (tq, tq), 0) >= lax.broadcasted_iota(jnp.int32, (tq, tq), 1)


def _attn_fwd(q, k, v):
    t = q.shape[0]
    tq = _tb(t)

    def body(q_ref, k_ref, v_ref, o_ref, lse_ref):
        qi = pl.program_id(1)
        for i in range(t // tq):
            @pl.when(qi == i)
            def _(i=i):
                qv = q_ref[...]
                lo, hi = i * tq, (i + 1) * tq
                s_d = jnp.where(_diag_mask(tq), _dot_nt(qv, k_ref[lo:hi, :]) * ATT_SCALE, NEG)
                m = jnp.max(s_d, axis=-1, keepdims=True)
                if i > 0:
                    s_o = _dot_nt(qv, k_ref[0:lo, :]) * ATT_SCALE
                    m = jnp.maximum(m, jnp.max(s_o, axis=-1, keepdims=True))
                p_d = jnp.exp(s_d - m)
                lsum = jnp.sum(p_d, axis=-1, keepdims=True)
                acc = _dot(p_d.astype(bf16), v_ref[lo:hi, :])
                if i > 0:
                    p_o = jnp.exp(s_o - m)
                    lsum = lsum + jnp.sum(p_o, axis=-1, keepdims=True)
                    acc = acc + _dot(p_o.astype(bf16), v_ref[0:lo, :])
                o_ref[...] = (acc / lsum).astype(bf16)
                lse_ref[...] = m + jnp.log(lsum)

    return pl.pallas_call(
        body, name="attn_fwd", grid=(N_HEADS, t // tq),
        in_specs=[pl.BlockSpec((tq, HEAD_PAD), lambda h, i: (i, h)), pl.BlockSpec((t, HEAD_PAD), lambda h, i: (0, h)),
                  pl.BlockSpec((t, HEAD_PAD), lambda h, i: (0, h))],
        out_specs=(pl.BlockSpec((tq, HEAD_PAD), lambda h, i: (i, h)), pl.BlockSpec((None, tq, 1), lambda h, i: (h, i, 0))),
        out_shape=(jax.ShapeDtypeStruct((t, N_HEADS * HEAD_PAD), bf16), jax.ShapeDtypeStruct((N_HEADS, t, 1), f32)),
        compiler_params=_cparams(2),
    )(q, k, v)


def _merge_out_fwd(l, x, s, um, o, gates, conv_out_w, sg_out_w, wo_p, w_out, g2):
    t, d = x.shape
    tb = _tb(t)

    def body(x_ref, s_ref, um_ref, o_ref, gt_ref, wa_ref, wb_ref, wc_ref, wout_ref, g_ref, x1_ref, mg_ref, om_ref):
        merged = (jax.nn.sigmoid(gt_ref[:, 0:d]) * _dot(s_ref[...], wa_ref[...])
                  + jax.nn.sigmoid(gt_ref[:, d:2 * d]) * _dot(um_ref[...], wb_ref[...])
                  + jax.nn.sigmoid(gt_ref[:, 2 * d:3 * d]) * _dot(o_ref[...], wc_ref[...]))
        mb = merged.astype(bf16)
        mg_ref[...] = mb
        om = _dot(mb, wout_ref[...])
        om_ref[...] = om
        x1_ref[...] = x_ref[...] + _rms_fwd(om, g_ref[l:l + 1, :])

    hw = N_HEADS * HEAD_PAD
    return pl.pallas_call(
        body, name="merge_out_fwd", grid=(t // tb,),
        in_specs=[_rows(tb, d), _rows(tb, D_CONV), _rows(tb, D_SG), _rows(tb, hw), _rows(tb, 3 * d),
                  _const((D_CONV, d)), _const((D_SG, d)), _const((hw, d)), _const((d, d)), _const(g2.shape)],
        out_specs=(_rows(tb, d), _rows(tb, d), _rows(tb, d)),
        out_shape=(jax.ShapeDtypeStruct((t, d), f32), jax.ShapeDtypeStruct((t, d), bf16), jax.ShapeDtypeStruct((t, d), f32)),
        compiler_params=_cparams(1),
    )(x, s, um, o, gates, conv_out_w, sg_out_w, wo_p, w_out, g2)


FF_CHUNK = 1408


def _ffn_conv_cols(zbuf, w_ref, b_ref, l, nrows, c0, c1):
    acc = b_ref[l:l + 1, c0:c1] + w_ref[0:1, c0:c1] * zbuf[pl.ds(FFN_HALO - 2, nrows), c0:c1]
    acc = acc + w_ref[1:2, c0:c1] * zbuf[pl.ds(FFN_HALO - 1, nrows), c0:c1]
    return acc + w_ref[2:3, c0:c1] * zbuf[pl.ds(FFN_HALO, nrows), c0:c1]


def _ffn_fwd(l, x1, g3, w_up, dw_w, dw_b, w_down, g4):
    t, d = x1.shape
    tb = _tb(t)
    ff2 = 2 * D_FF

    def body(x_ref, g3_ref, wup_ref, dww_ref, dwb_ref, wdn_ref, g4_ref, x2_ref, h2_ref, z_ref, zc_ref, act_ref, f_ref, zbuf):
        @pl.when(pl.program_id(0) == 0)
        def _():
            zbuf[0:FFN_HALO, :] = jnp.zeros((FFN_HALO, ff2), f32)

        xv = x_ref[...]
        h2 = _rms_fwd(xv, g3_ref[l:l + 1, :]).astype(bf16)
        h2_ref[...] = h2
        for c0 in range(0, ff2, FF_CHUNK):
            zv = _dot(h2, wup_ref[:, c0:c0 + FF_CHUNK])
            z_ref[:, c0:c0 + FF_CHUNK] = zv.astype(bf16)
            zbuf[FFN_HALO:FFN_HALO + tb, c0:c0 + FF_CHUNK] = zv
        facc = jnp.zeros((tb, d), f32)
        for c0 in range(0, D_FF, FF_CHUNK):
            gg = _ffn_conv_cols(zbuf, dww_ref, dwb_ref, l, tb, c0, c0 + FF_CHUNK)
            vv = _ffn_conv_cols(zbuf, dww_ref, dwb_ref, l, tb, D_FF + c0, D_FF + c0 + FF_CHUNK)
            zc_ref[:, c0:c0 + FF_CHUNK] = gg.astype(bf16)
            zc_ref[:, D_FF + c0:D_FF + c0 + FF_CHUNK] = vv.astype(bf16)
            a = (_gelu(gg) * vv).astype(bf16)
            act_ref[:, c0:c0 + FF_CHUNK] = a
            facc = facc + _dot(a, wdn_ref[c0:c0 + FF_CHUNK, :])
        f_ref[...] = facc
        x2_ref[...] = xv + _rms_fwd(facc, g4_ref[l:l + 1, :])
        zbuf[0:FFN_HALO, :] = zbuf[tb:tb + FFN_HALO, :]

    return pl.pallas_call(
        body, name="ffn_fwd", grid=(t // tb,),
        in_specs=[_rows(tb, d), _const(g3.shape), _const((d, ff2)), _const((FFN_HALO, ff2)), _const(dw_b.shape), _const((D_FF, d)),
                  _const(g4.shape)],
        out_specs=(_rows(tb, d), _rows(tb, d), _rows(tb, ff2), _rows(tb, ff2), _rows(tb, D_FF), _rows(tb, d)),
        out_shape=(jax.ShapeDtypeStruct((t, d), f32), jax.ShapeDtypeStruct((t, d), bf16), jax.ShapeDtypeStruct((t, ff2), bf16),
                   jax.ShapeDtypeStruct((t, ff2), bf16), jax.ShapeDtypeStruct((t, D_FF), bf16), jax.ShapeDtypeStruct((t, d), f32)),
        scratch_shapes=[pltpu.VMEM((tb + FFN_HALO, ff2), f32)],
        compiler_params=_cparams(1),
    )(x1, g3, w_up, dw_w, dw_b, w_down, g4)


def _loss_fwd_bwd(y, target):
    t, d = y.shape
    tb = _tb(t)

    def body(y_ref, t_ref, dy_ref, loss_ref):
        @pl.when(pl.program_id(0) == 0)
        def _():
            loss_ref[...] = jnp.zeros((1, LANES), f32)

        e = y_ref[...] - t_ref[...]
        dy_ref[...] = e * (1.0 / d)
        loss_ref[...] += 0.5 * jnp.sum(_mean(e * e))

    dy, loss = pl.pallas_call(
        body, name="loss", grid=(t // tb,),
        in_specs=[_rows(tb, d), _rows(tb, d)],
        out_specs=(_rows(tb, d), _acc((1, LANES))),
        out_shape=(jax.ShapeDtypeStruct((t, d), f32), jax.ShapeDtypeStruct((1, LANES), f32)),
        compiler_params=_cparams(1),
    )(y, target)
    return dy, loss[0, 0]


def _ffn_bwd(l, dx2, x1, f, z, zc, w_up, dw_w, w_down, g3, g4, deps=()):
    t, d = x1.shape
    tb = min(128, t)
    nb = t // tb
    ff2 = 2 * D_FF
    hrows = 16
    per_h = tb // hrows

    def body(*refs):
        (dx2_ref, x1_ref, f_ref, z_ref, zp_ref, zc_ref, wup_ref, dww_ref, wdn_ref, g3_ref, g4_ref,
         dx1_ref, df_ref, dz_ref, dg4_ref, dg3_ref, ddwb_ref, ddww_ref, zbuf, dzc) = refs[len(deps):]
        i = pl.program_id(0)
        b = nb - 1 - i

        @pl.when(i == 0)
        def _():
            dg4_ref[...] = jnp.zeros_like(dg4_ref)
            dg3_ref[...] = jnp.zeros_like(dg3_ref)
            ddwb_ref[...] = jnp.zeros_like(ddwb_ref)
            ddww_ref[...] = jnp.zeros_like(ddww_ref)
            dzc[tb:tb + FFN_HALO, :] = jnp.zeros((FFN_HALO, ff2), f32)

        dout = dx2_ref[...]
        df, dg4 = _rms_bwd(f_ref[...], g4_ref[l:l + 1, :], dout)
        dg4_ref[...] += dg4
        dfb = df.astype(bf16)
        df_ref[...] = dfb
        zbuf[0:FFN_HALO, :] = jnp.where(b > 0, zp_ref[hrows - FFN_HALO:hrows, :].astype(f32), 0.0)
        zbuf[FFN_HALO:FFN_HALO + tb, :] = z_ref[...].astype(f32)
        for c0 in range(0, D_FF, FF_CHUNK):
            dact = _dot_nt(dfb, wdn_ref[c0:c0 + FF_CHUNK, :])
            gel, dgel = _gelu_and_grad(zc_ref[:, c0:c0 + FF_CHUNK].astype(f32))
            dzc[0:tb, c0:c0 + FF_CHUNK] = dact * zc_ref[:, D_FF + c0:D_FF + c0 + FF_CHUNK].astype(f32) * dgel
            dzc[0:tb, D_FF + c0:D_FF + c0 + FF_CHUNK] = dact * gel
        dh2 = jnp.zeros((tb, d), f32)
        for c0 in range(0, ff2, FF_CHUNK):
            cs = slice(c0, c0 + FF_CHUNK)
            d0 = dzc[0:tb, cs]
            ddwb_ref[:, cs] += _colsum(d0)
            for j in range(FFN_K):
                ddww_ref[j:j + 1, cs] += _colsum(d0 * zbuf[pl.ds(FFN_HALO - 2 + j, tb), cs])
            dzv = dww_ref[2:3, cs] * d0 + dww_ref[1:2, cs] * dzc[pl.ds(1, tb), cs] + dww_ref[0:1, cs] * dzc[pl.ds(2, tb), cs]
            dzb = dzv.astype(bf16)
            dz_ref[:, cs] = dzb
            dh2 = dh2 + _dot_nt(dzb, wup_ref[:, cs])
        dzc[tb:tb + FFN_HALO, :] = dzc[0:FFN_HALO, :]
        dxn, dg3 = _rms_bwd(x1_ref[...], g3_ref[l:l + 1, :], dh2)
        dg3_ref[...] += dg3
        dx1_ref[...] = dout + dxn

    return pl.pallas_call(
        body, name="ffn_bwd", grid=(nb,),
        in_specs=[_ANY] * len(deps) + [_rows_rev(tb, d, nb), _rows_rev(tb, d, nb), _rows_rev(tb, d, nb), _rows_rev(tb, ff2, nb),
                  pl.BlockSpec((hrows, ff2), lambda i: (jnp.maximum((nb - 1 - i) * per_h - 1, 0), 0)), _rows_rev(tb, ff2, nb),
                  _const((d, ff2)), _const((FFN_HALO, ff2)), _const((D_FF, d)), _const(g3.shape), _const(g4.shape)],
        out_specs=(_rows_rev(tb, d, nb), _rows_rev(tb, d, nb), _rows_rev(tb, ff2, nb), _acc((1, d)), _acc((1, d)), _acc((1, ff2)),
                   _acc((FFN_HALO, ff2))),
        out_shape=(jax.ShapeDtypeStruct((t, d), f32), jax.ShapeDtypeStruct((t, d), bf16), jax.ShapeDtypeStruct((t, ff2), bf16),
                   jax.ShapeDtypeStruct((1, d), f32), jax.ShapeDtypeStruct((1, d), f32), jax.ShapeDtypeStruct((1, ff2), f32),
                   jax.ShapeDtypeStruct((FFN_HALO, ff2), f32)),
        scratch_shapes=[pltpu.VMEM((tb + FFN_HALO, ff2), f32), pltpu.VMEM((tb + FFN_HALO, ff2), f32)],
        compiler_params=_cparams(1),
    )(*deps, dx2, x1, f, z, z, zc, w_up, dw_w, w_down, g3, g4)


def _merge_out_bwd(l, dx1, om, s, um, o, gates, conv_out_w, sg_out_w, wo_p, w_out, g2, deps=()):
    t, d = dx1.shape
    tb = _tb(t)
    hw = N_HEADS * HEAD_PAD

    def body(*refs):
        (dx_ref, om_ref, s_ref, um_ref, o_ref, gt_ref, wa_ref, wb_ref, wc_ref, wout_ref, g_ref,
         dom_ref, dgt_ref, dya_ref, dyb_ref, dyc_ref, ds_ref, dum_ref, do_ref, dg2_ref) = refs[len(deps):]

        @pl.when(pl.program_id(0) == 0)
        def _():
            dg2_ref[...] = jnp.zeros_like(dg2_ref)

        dom, dg2 = _rms_bwd(om_ref[...], g_ref[l:l + 1, :], dx_ref[...])
        dg2_ref[...] += dg2
        domb = dom.astype(bf16)
        dom_ref[...] = domb
        dmerged = _dot_nt(domb, wout_ref[...])
        branches = ((s_ref, wa_ref, dya_ref, ds_ref), (um_ref, wb_ref, dyb_ref, dum_ref), (o_ref, wc_ref, dyc_ref, do_ref))
        for br, (in_ref, w_ref, dy_ref, din_ref) in enumerate(branches):
            yv = _dot(in_ref[...], w_ref[...])
            sg = jax.nn.sigmoid(gt_ref[:, br * d:(br + 1) * d])
            dyb = (dmerged * sg).astype(bf16)
            dy_ref[...] = dyb
            dgt_ref[:, br * d:(br + 1) * d] = (dmerged * yv * sg * (1.0 - sg)).astype(bf16)
            din_ref[...] = _dot_nt(dyb, w_ref[...]).astype(din_ref.dtype)

    return pl.pallas_call(
        body, name="merge_out_bwd", grid=(t // tb,),
        in_specs=[_ANY] * len(deps) + [_rows(tb, d), _rows(tb, d), _rows(tb, D_CONV), _rows(tb, D_SG), _rows(tb, hw), _rows(tb, 3 * d),
                  _const((D_CONV, d)), _const((D_SG, d)), _const((hw, d)), _const((d, d)), _const(g2.shape)],
        out_specs=(_rows(tb, d), _rows(tb, 3 * d), _rows(tb, d), _rows(tb, d), _rows(tb, d), _rows(tb, D_CONV), _rows(tb, D_SG),
                   _rows(tb, hw), _acc((1, d))),
        out_shape=(jax.ShapeDtypeStruct((t, d), bf16), jax.ShapeDtypeStruct((t, 3 * d), bf16), jax.ShapeDtypeStruct((t, d), bf16),
                   jax.ShapeDtypeStruct((t, d), bf16), jax.ShapeDtypeStruct((t, d), bf16), jax.ShapeDtypeStruct((t, D_CONV), f32),
                   jax.ShapeDtypeStruct((t, D_SG), f32), jax.ShapeDtypeStruct((t, hw), bf16), jax.ShapeDtypeStruct((1, d), f32)),
        compiler_params=_cparams(1),
    )(*deps, dx1, om, s, um, o, gates, conv_out_w, sg_out_w, wo_p, w_out, g2)


def _attn_bwd(q, k, v, o, lse, do):
    t = q.shape[0]
    tq = _tb(t)
    hw = N_HEADS * HEAD_PAD

    def body(q_ref, k_ref, v_ref, o_ref, lse_ref, do_ref, dq_ref, dk_ref, dv_ref):
        qi = pl.program_id(1)

        @pl.when(qi == 0)
        def _():
            dk_ref[...] = jnp.zeros_like(dk_ref)
            dv_ref[...] = jnp.zeros_like(dv_ref)

        def keys(lo, hi, qv, dov, lse, delta, diagonal):
            kj, vj = k_ref[lo:hi, :], v_ref[lo:hi, :]
            p = jnp.exp(_dot_nt(qv, kj) * ATT_SCALE - lse)
            if diagonal:
                p = jnp.where(_diag_mask(tq), p, 0.0)
            ds = (p * (_dot_nt(dov, vj) - delta) * ATT_SCALE).astype(bf16)
            dk_ref[lo:hi, :] += _dot_tn(ds, qv)
            dv_ref[lo:hi, :] += _dot_tn(p.astype(bf16), dov)
            return _dot(ds, kj)

        for i in range(t // tq):
            @pl.when(qi == i)
            def _(i=i):
                qv, dov, lse = q_ref[...], do_ref[...], lse_ref[...]
                delta = jnp.sum(dov.astype(f32) * o_ref[...].astype(f32), axis=-1, keepdims=True)
                dq = keys(i * tq, (i + 1) * tq, qv, dov, lse, delta, True)
                if i > 0:
                    dq = dq + keys(0, i * tq, qv, dov, lse, delta, False)
                dq_ref[...] = dq

    blk_q = pl.BlockSpec((tq, HEAD_PAD), lambda h, i: (i, h))
    blk_kv = pl.BlockSpec((t, HEAD_PAD), lambda h, i: (0, h))
    return pl.pallas_call(
        body, name="attn_bwd", grid=(N_HEADS, t // tq),
        in_specs=[blk_q, blk_kv, blk_kv, blk_q, pl.BlockSpec((None, tq, 1), lambda h, i: (h, i, 0)), blk_q],
        out_specs=(blk_q, blk_kv, blk_kv),
        out_shape=(jax.ShapeDtypeStruct((t, hw), f32), jax.ShapeDtypeStruct((t, hw), f32), jax.ShapeDtypeStruct((t, hw), f32)),
        compiler_params=_cparams(2),
    )(q, k, v, o, lse, do)


def _mla_proj_bwd(l, dq, dk, dv, qlat, kvlat, cos_t, sin_t, gq, wuq_p, gkv, wk_p, wv_p):
    t = qlat.shape[0]
    tb = _tb(t)
    hw = N_HEADS * HEAD_PAD

    def body(dq_ref, dk_ref, dv_ref, ql_ref, kvl_ref, c_ref, s_ref, gq_ref, wq_ref, gkv_ref, wk_ref, wv_ref,
             dqb_ref, dkb_ref, dvb_ref, dql_ref, dkvl_ref, dkr_ref, dgq_ref, dgkv_ref):
        @pl.when(pl.program_id(0) == 0)
        def _():
            dgq_ref[...] = jnp.zeros_like(dgq_ref)
            dgkv_ref[...] = jnp.zeros_like(dgkv_ref)

        cos_b, sin_b = c_ref[...], s_ref[...]
        for h in range(N_HEADS):
            hs = slice(h * HEAD_PAD, (h + 1) * HEAD_PAD)
            dqh = dq_ref[:, hs]
            dqb_ref[:, hs] = (dqh * cos_b + _swap_rope_halves(dqh * sin_b)).astype(bf16)
        dqn = _dot_nt(dqb_ref[...], wq_ref[...])
        dql, dgq = _rms_bwd(ql_ref[...], gq_ref[l:l + 1, :], dqn)
        dgq_ref[...] += dgq
        dql_ref[...] = dql.astype(bf16)
        dkv_full = dk_ref[...]
        dkb = dkv_full.astype(bf16)
        dkb_ref[...] = dkb
        dkpe = dkv_full[:, 0:HEAD_PAD]
        for h in range(1, N_HEADS):
            dkpe = dkpe + dkv_full[:, h * HEAD_PAD:(h + 1) * HEAD_PAD]
        dkr_ref[...] = (dkpe * cos_b + _swap_rope_halves(dkpe * sin_b)).astype(bf16)
        dvb = dv_ref[...].astype(bf16)
        dvb_ref[...] = dvb
        dkvn = _dot_nt(dkb, wk_ref[...]) + _dot_nt(dvb, wv_ref[...])
        dkvl, dgkv = _rms_bwd(kvl_ref[...], gkv_ref[l:l + 1, :], dkvn)
        dgkv_ref[...] += dgkv
        dkvl_ref[...] = dkvl.astype(bf16)

    return pl.pallas_call(
        body, name="mla_proj_bwd", grid=(t // tb,),
        in_specs=[_rows(tb, hw), _rows(tb, hw), _rows(tb, hw), _rows(tb, Q_LORA), _rows(tb, KV_LORA), _rows(tb, HEAD_PAD),
                  _rows(tb, HEAD_PAD), _const(gq.shape), _const((Q_LORA, hw)), _const(gkv.shape), _const((KV_LORA, hw)),
                  _const((KV_LORA, hw))],
        out_specs=(_rows(tb, hw), _rows(tb, hw), _rows(tb, hw), _rows(tb, Q_LORA), _rows(tb, KV_LORA), _rows(tb, HEAD_PAD),
                   _acc((1, Q_LORA)), _acc((1, KV_LORA))),
        out_shape=(jax.ShapeDtypeStruct((t, hw), bf16), jax.ShapeDtypeStruct((t, hw), bf16), jax.ShapeDtypeStruct((t, hw), bf16),
                   jax.ShapeDtypeStruct((t, Q_LORA), bf16), jax.ShapeDtypeStruct((t, KV_LORA), bf16),
                   jax.ShapeDtypeStruct((t, HEAD_PAD), bf16), jax.ShapeDtypeStruct((1, Q_LORA), f32),
                   jax.ShapeDtypeStruct((1, KV_LORA), f32)),
        compiler_params=_cparams(1),
    )(dq, dk, dv, qlat, kvlat, cos_t, sin_t, gq, wuq_p, gkv, wk_p, wv_p)


def _sgu_bwd(l, b_in, dum, ln_g, ln_b, sg_w, bexp):
    t = b_in.shape[0]
    tb = _tb(t)
    gw = D_SG // SG_GROUPS

    def body(b_ref, dum_ref, g_ref, be_ref, w_ref, bexp_ref, db_ref, dw_ref, dsgb_ref, dlg_ref, dlb_ref, dvn_s):
        @pl.when(pl.program_id(0) == 0)
        def _():
            dw_ref[...] = jnp.zeros_like(dw_ref)
            dsgb_ref[...] = jnp.zeros_like(dsgb_ref)
            dlg_ref[...] = jnp.zeros_like(dlg_ref)
            dlb_ref[...] = jnp.zeros_like(dlb_ref)

        gl, dgl = _gelu_and_grad(b_ref[...])
        u = gl[:, :D_SG]
        vh, rstd = _ln_stats(gl[:, D_SG:])
        ln_gain = g_ref[l:l + 1, :]
        vn = (vh * ln_gain + be_ref[l:l + 1, :]).astype(bf16)
        dumv = dum_ref[...]
        tri = _tril_mask()
        ones = jnp.ones((FFN_HALO, gw), f32)
        for g in range(SG_GROUPS):
            wg = jnp.where(tri, w_ref[l, g], 0.0).astype(bf16)
            cs = slice(g * gw, (g + 1) * gw)
            for r0 in range(0, tb, SG_CHUNK):
                rs = slice(r0, r0 + SG_CHUNK)
                vblk = vn[rs, cs]
                mixed = _dot(wg, vblk) + bexp_ref[:, cs]
                db_ref[rs, cs] = (dumv[rs, cs] * mixed * dgl[rs, cs]).astype(bf16)
                dmix = dumv[rs, cs] * u[rs, cs]
                dmb = dmix.astype(bf16)
                dw_ref[g] += jnp.where(tri, _dot_nt(dmb, vblk), 0.0)
                rowsum = lax.dot_general(ones, dmix, (((1,), (1,)), ((), ())), preferred_element_type=f32,
                                         precision=lax.Precision.HIGHEST)
                dsgb_ref[g:g + 1, :] += rowsum[0:1, :]
                dvn_s[rs, cs] = _dot_tn(wg, dmb)
        dvn = dvn_s[...]
        dlg_ref[...] += _colsum(dvn * vh)
        dlb_ref[...] += _colsum(dvn)
        db_ref[:, D_SG:] = (_ln_bwd(vh, rstd, ln_gain, dvn) * dgl[:, D_SG:]).astype(bf16)

    return pl.pallas_call(
        body, name="sgu_bwd", grid=(t // tb,),
        in_specs=[_rows(tb, 2 * D_SG), _rows(tb, D_SG), _const(ln_g.shape), _const(ln_b.shape), _const(sg_w.shape),
                  _const((SG_CHUNK, D_SG))],
        out_specs=(_rows(tb, 2 * D_SG), _acc((SG_GROUPS, SG_CHUNK, SG_CHUNK)), _acc((FFN_HALO, SG_CHUNK)), _acc((1, D_SG)),
                   _acc((1, D_SG))),
        out_shape=(jax.ShapeDtypeStruct((t, 2 * D_SG), bf16), jax.ShapeDtypeStruct((SG_GROUPS, SG_CHUNK, SG_CHUNK), f32),
                   jax.ShapeDtypeStruct((FFN_HALO, SG_CHUNK), f32), jax.ShapeDtypeStruct((1, D_SG), f32),
                   jax.ShapeDtypeStruct((1, D_SG), f32)),
        scratch_shapes=[pltpu.VMEM((tb, D_SG), f32)],
        compiler_params=_cparams(1),
    )(b_in, dum, ln_g, ln_b, sg_w, bexp)


def _conv_bwd(l, a_in, z1, ds, dw_w, ln_g, ln_b):
    t = a_in.shape[0]
    tb = _tb(t)
    nb = t // tb
    per_halo = tb // CONV_HALO

    def body(a_ref, ap_ref, z1_ref, ds_ref, w_ref, g_ref, be_ref, da_ref, ddww_ref, ddwb_ref, dlg_ref, dlb_ref, win, dzb):
        i = pl.program_id(0)
        b = nb - 1 - i

        @pl.when(i == 0)
        def _():
            ddww_ref[...] = jnp.zeros_like(ddww_ref)
            ddwb_ref[...] = jnp.zeros_like(ddwb_ref)
            dlg_ref[...] = jnp.zeros_like(dlg_ref)
            dlb_ref[...] = jnp.zeros_like(dlb_ref)
            dzb[tb:tb + CONV_HALO, :] = jnp.zeros((CONV_HALO, D_CONV), f32)

        a = a_ref[...]
        val = a[:, :D_CONV]
        sg = jax.nn.sigmoid(a[:, D_CONV:])
        ap = ap_ref[...]
        win[0:CONV_HALO, :] = jnp.where(b > 0, ap[:, :D_CONV] * jax.nn.sigmoid(ap[:, D_CONV:]), 0.0)
        win[CONV_HALO:CONV_HALO + tb, :] = val * sg
        zh, rstd = _ln_stats(z1_ref[...])
        ln_gain = g_ref[l:l + 1, :]
        zl = zh * ln_gain + be_ref[l:l + 1, :]
        sgl = jax.nn.sigmoid(zl)
        dzl = ds_ref[...] * (sgl * (1.0 + zl * (1.0 - sgl)))
        dlg_ref[...] += _colsum(dzl * zh)
        dlb_ref[...] += _colsum(dzl)
        dz1 = _ln_bwd(zh, rstd, ln_gain, dzl)
        dzb[0:tb, :] = dz1
        ddwb_ref[...] += _colsum(dz1)
        dgate_f = val * sg * (1.0 - sg)
        for c0 in range(0, D_CONV, LANES):
            cs = slice(c0, c0 + LANES)
            for r0 in range(0, tb, LANES):
                d1 = dzb[r0:r0 + LANES, cs]
                acc = jnp.zeros((LANES, LANES), f32)
                for j in range(CONV_K):
                    ddww_ref[j:j + 1, cs] += _colsum(d1 * win[pl.ds(r0 + CONV_HALO - CONV_K + 1 + j, LANES), cs])
                    acc = acc + w_ref[j:j + 1, cs] * dzb[pl.ds(r0 + CONV_K - 1 - j, LANES), cs]
                da_ref[r0:r0 + LANES, cs] = (acc * sg[r0:r0 + LANES, cs]).astype(bf16)
                da_ref[r0:r0 + LANES, c0 + D_CONV:c0 + D_CONV + LANES] = (acc * dgate_f[r0:r0 + LANES, cs]).astype(bf16)
        dzb[tb:tb + CONV_HALO, :] = dzb[0:CONV_HALO, :]

    return pl.pallas_call(
        body, name="conv_bwd", grid=(nb,),
        in_specs=[_rows_rev(tb, 2 * D_CONV, nb),
                  pl.BlockSpec((CONV_HALO, 2 * D_CONV), lambda i: (jnp.maximum((nb - 1 - i) * per_halo - 1, 0), 0)),
                  _rows_rev(tb, D_CONV, nb), _rows_rev(tb, D_CONV, nb), _const((CONV_HALO, D_CONV)), _const(ln_g.shape),
                  _const(ln_b.shape)],
        out_specs=(_rows_rev(tb, 2 * D_CONV, nb), _acc((CONV_HALO, D_CONV)), _acc((1, D_CONV)), _acc((1, D_CONV)), _acc((1, D_CONV))),
        out_shape=(jax.ShapeDtypeStruct((t, 2 * D_CONV), bf16), jax.ShapeDtypeStruct((CONV_HALO, D_CONV), f32),
                   jax.ShapeDtypeStruct((1, D_CONV), f32), jax.ShapeDtypeStruct((1, D_CONV), f32), jax.ShapeDtypeStruct((1, D_CONV), f32)),
        scratch_shapes=[pltpu.VMEM((tb + CONV_HALO, D_CONV), f32), pltpu.VMEM((tb + CONV_HALO, D_CONV), f32)],
        compiler_params=_cparams(1),
    )(a_in, a_in, z1, ds, dw_w, ln_g, ln_b)


def _mix_in_bwd(l, x, g1, dxres, dsegs, w_in_p):
    t, d = x.shape
    tb = _tb(t)

    def body(x_ref, g_ref, dr_ref, *rest):
        dseg_refs, w_ref, dx_ref, dg_ref = rest[:len(SEGS)], rest[len(SEGS)], rest[len(SEGS) + 1], rest[len(SEGS) + 2]

        @pl.when(pl.program_id(0) == 0)
        def _():
            dg_ref[...] = jnp.zeros_like(dg_ref)

        dh = jnp.zeros((tb, d), f32)
        for (off, wd), ds_ref in zip(SEGS, dseg_refs):
            dh = dh + _dot_nt(ds_ref[...], w_ref[:, off:off + wd])
        dxn, dg = _rms_bwd(x_ref[...], g_ref[l:l + 1, :], dh)
        dg_ref[...] += dg
        dx_ref[...] = dr_ref[...] + dxn

    return pl.pallas_call(
        body, name="mix_in_bwd", grid=(t // tb,),
        in_specs=[_rows(tb, d), _const(g1.shape), _rows(tb, d)] + [_rows(tb, wd) for _, wd in SEGS] + [_const((d, D_IN_PAD))],
        out_specs=(_rows(tb, d), _acc((1, d))),
        out_shape=(jax.ShapeDtypeStruct((t, d), f32), jax.ShapeDtypeStruct((1, d), f32)),
        compiler_params=_cparams(1),
    )(x, g1, dxres, *dsegs, w_in_p)


def _pick_block(n, cap=512):
    for b in (cap, 384, 256, 128):
        if b <= cap and n % b == 0:
            return b
    return n


def _wgrad(a, b, name, out_dtype=f32):
    t, kdim = a.shape
    n = b.shape[1]
    bk, bn = _pick_block(kdim), _pick_block(n)

    def body(a_ref, b_ref, o_ref):
        o_ref[...] = _dot_tn(a_ref[...], b_ref[...]).astype(out_dtype)

    return pl.pallas_call(
        body, name=name, grid=(kdim // bk, n // bn),
        in_specs=[pl.BlockSpec((t, bk), lambda i, j: (0, i)), pl.BlockSpec((t, bn), lambda i, j: (0, j))],
        out_specs=pl.BlockSpec((bk, bn), lambda i, j: (i, j)),
        out_shape=jax.ShapeDtypeStruct((kdim, n), out_dtype),
        compiler_params=_cparams(2),
    )(a, b)


_BC1 = 1.0 - ADAM_B1 ** ADAM_STEP
_BC2 = 1.0 - ADAM_B2 ** ADAM_STEP


def _adam_math(g, w, m, v):
    nm = ADAM_B1 * m + (1.0 - ADAM_B1) * g
    nv = ADAM_B2 * v + (1.0 - ADAM_B2) * (g * g)
    delta = -ADAM_LR * ((nm / _BC1) / (jnp.sqrt(nv / _BC2) + ADAM_EPS) + ADAM_WD * w)
    return delta, nm, nv


def _slot_sum(r_ref, index=()):
    g = r_ref[(0,) + index].astype(f32)
    for s in range(1, N_DEV):
        g = g + r_ref[(s,) + index].astype(f32)
    return g


def _adamw_shard(l, recv, w, m, v, prev, name, deps=()):
    _, k, ns = recv.shape
    rb = 256 if k % 256 == 0 else k
    blk = pl.BlockSpec((None, rb, ns), lambda i: (l, i, 0))

    def body(r_ref, w_ref, m_ref, v_ref, *rest):
        g_ref, d_ref, nm_ref, nv_ref = rest[-4:]
        g = _slot_sum(r_ref)
        g_ref[...] = g
        d_ref[...], nm_ref[...], nv_ref[...] = _adam_math(g, w_ref[...], m_ref[...], v_ref[...])

    out = jax.ShapeDtypeStruct(w.shape, f32)
    n_prev = 0 if prev is None else 4
    return pl.pallas_call(
        body, name=name, grid=(k // rb,),
        in_specs=[pl.BlockSpec((N_DEV, rb, ns), lambda i: (0, i, 0)), blk, blk, blk] + [_ANY] * (n_prev + len(deps)),
        out_specs=(blk, blk, blk, blk), out_shape=(out, out, out, out),
        input_output_aliases={4 + j: j for j in range(n_prev)},
        compiler_params=_cparams(1),
    )(recv, w, m, v, *(prev or ()), *deps)


def _adamw_rep(recvs, ws, ms, vs):
    depth = len(recvs)
    nt = len(REP_VECS)

    def body(*refs):
        r_refs = refs[:depth]
        w_refs, m_refs, v_refs = (refs[depth + i * nt:depth + (i + 1) * nt] for i in range(3))
        outs = refs[depth + 3 * nt:]
        for ti, (_, row, width, nrows) in enumerate(REP_VECS):
            for l in range(depth):
                g = r_refs[l][0, row:row + nrows, 0:width]
                for s in range(1, N_DEV):
                    g = g + r_refs[l][s, row:row + nrows, 0:width]
                pick = (lambda ref: ref[l]) if nrows > 1 else (lambda ref: ref[l:l + 1, :])
                delta, nm, nv = _adam_math(g, pick(w_refs[ti]), pick(m_refs[ti]), pick(v_refs[ti]))
                for o_ref, val in zip(outs[4 * ti:4 * ti + 4], (g, delta, nm, nv)):
                    if nrows > 1:
                        o_ref[l] = val
                    else:
                        o_ref[l:l + 1, :] = val

    ins = tuple(recvs) + tuple(ws) + tuple(ms) + tuple(vs)
    out_shape = tuple(jax.ShapeDtypeStruct(w.shape, f32) for w in ws for _ in range(4))
    return pl.pallas_call(
        body, name="adamw_rep", grid=(1,),
        in_specs=[_whole(a) for a in ins], out_specs=tuple(_whole(o) for o in out_shape), out_shape=out_shape,
        compiler_params=_cparams(1),
    )(*ins)


def _adamw_sg_w(recvs, w, m, v):
    depth = len(recvs)

    def body(*refs):
        r_refs = refs[:depth]
        w_ref, m_ref, v_ref = refs[depth:depth + 3]
        outs = refs[depth + 3:]
        for l in range(depth):
            for gi in range(SG_GROUPS):
                g = _slot_sum(r_refs[l], (gi,))
                delta, nm, nv = _adam_math(g, w_ref[l, gi], m_ref[l, gi], v_ref[l, gi])
                for o_ref, val in zip(outs, (g, delta, nm, nv)):
                    o_ref[l, gi] = val

    ins = tuple(recvs) + (w, m, v)
    out = jax.ShapeDtypeStruct(w.shape, f32)
    return pl.pallas_call(
        body, name="adamw_sg_w", grid=(1,),
        in_specs=[_whole(a) for a in ins], out_specs=tuple(_whole(out) for _ in range(4)), out_shape=(out,) * 4,
        compiler_params=_cparams(1),
    )(*ins)


def _rope_tables(positions):
    t = positions.shape[0]
    inv = 10000.0 ** (-jnp.arange(0, QK_ROPE, 2, dtype=f32) / QK_ROPE)
    ang = positions.astype(f32)[:, None] * inv
    cos, sin = jnp.cos(ang), jnp.sin(ang)
    tail = jnp.zeros((t, HEAD_PAD - KR_LO - QK_ROPE), f32)
    cos_t = jnp.concatenate([jnp.ones((t, KR_LO), f32), cos, cos, tail], axis=1)
    sin_t = jnp.concatenate([jnp.zeros((t, KR_LO), f32), -sin, sin, tail], axis=1)
    return cos_t, sin_t


def _bias_over_channels(sg_b_l):
    return jnp.broadcast_to(sg_b_l.T[:, :, None], (SG_CHUNK, SG_GROUPS, D_SG // SG_GROUPS)).reshape(SG_CHUNK, D_SG)


def _mixer_weights(gathered, rep, l):
    conv_out_w, sg_out_w, wuq_p, wk_p, wv_p, wo_p, conv_dw_w, ffn_dw_w = _asm_small(
        gathered["conv_out_w"], gathered["sg_out_w"], gathered["mla_w_uq"], gathered["mla_w_ukv"], gathered["mla_w_o"],
        gathered["conv_dw_w"], gathered["ffn_dw_w"])
    g_out = gathered["w_out"]
    w = dict(rep)
    w.update(
        l=l, w_in_p=_asm_w_in(gathered["w_in"]),
        conv_out_w=conv_out_w, sg_out_w=sg_out_w, wuq_p=wuq_p, wk_p=wk_p, wv_p=wv_p, wo_p=wo_p, conv_dw_w_p=conv_dw_w, ffn_dw_w_p=ffn_dw_w,
        w_out=g_out.reshape(g_out.shape[0] * g_out.shape[1], g_out.shape[2]),
        bexp=_bias_over_channels(rep["sg_b"][l]))
    return w


def _ffn_weights(gathered):
    g_down = gathered["ffn_w_down"]
    return dict(w_up=_asm_cols(gathered["ffn_w_up"], "asm_w_up"),
                w_down=g_down.reshape(g_down.shape[0] * g_down.shape[1], g_down.shape[2]))


def _mixer_fwd(x, w, cos_t, sin_t):
    l = w["l"]
    h, a_in, b_in, qlat, kvlat, krope, gates = _mix_in_fwd(l, x, w["mix_pre_g"], w["w_in_p"])
    s, z1 = _conv_fwd(l, a_in, w["conv_dw_w_p"], w["conv_dw_b"], w["conv_ln_g"], w["conv_ln_b"])
    um = _sgu_fwd(l, b_in, w["sg_ln_g"], w["sg_ln_b"], w["sg_w"], w["bexp"])
    q, k, v, qn, kvn = _mla_proj_fwd(l, qlat, kvlat, krope, cos_t, sin_t, w["mla_q_norm_g"], w["wuq_p"], w["mla_kv_norm_g"],
                                     w["wk_p"], w["wv_p"])
    o, lse = _attn_fwd(q, k, v)
    x1, merged, om = _merge_out_fwd(l, x, s, um, o, gates, w["conv_out_w"], w["sg_out_w"], w["wo_p"], w["w_out"], w["mix_post_g"])
    saved = dict(x=x, h=h, a_in=a_in, b_in=b_in, qlat=qlat, kvlat=kvlat, gates=gates, s=s, z1=z1, um=um, q=q, k=k, v=v, qn=qn,
                 kvn=kvn, o=o, lse=lse, x1=x1, merged=merged, om=om)
    return x1, saved


def _ffn_layer_fwd(x1, w):
    x2, h2, z, zc, act, f = _ffn_fwd(w["l"], x1, w["ffn_pre_g"], w["w_up"], w["ffn_dw_w_p"], w["ffn_dw_b"], w["w_down"], w["ffn_post_g"])
    return x2, dict(h2=h2, z=z, zc=zc, act=act, f=f)


def _ffn_layer_bwd(dx2, sv, w, deps=()):
    l = w["l"]
    vec = {}
    dx1, df, dz, vec["ffn_post_g"], vec["ffn_pre_g"], vec["ffn_dw_b"], vec["d_fdw"] = _ffn_bwd(
        l, dx2, sv["x1"], sv["f"], sv["z"], sv["zc"], w["w_up"], w["ffn_dw_w_p"], w["w_down"], w["ffn_pre_g"], w["ffn_post_g"],
        deps=deps)
    d_down = _wgrad(sv["act"], df, "wgrad_ffn_down", bf16)
    d_up = _wgrad(sv["h2"], dz, "wgrad_ffn_up")
    send = dict(ffn_w_up=_dis_cols(d_up, "dis_w_up"),
                ffn_w_down=d_down.reshape(N_DEV, d_down.shape[0] // N_DEV, d_down.shape[1]))
    return dx1, send, vec


def _mixer_bwd(dx1, sv, w, cos_t, sin_t, shard_cols, vec, deps=()):
    l = w["l"]
    d_fdw = vec.pop("d_fdw")
    dom, dgates, dya, dyb, dyc, ds, dum, do, vec["mix_post_g"] = _merge_out_bwd(
        l, dx1, sv["om"], sv["s"], sv["um"], sv["o"], sv["gates"], w["conv_out_w"], w["sg_out_w"], w["wo_p"], w["w_out"], w["mix_post_g"],
        deps=deps)
    d_out = _wgrad(sv["merged"], dom, "wgrad_w_out", bf16)
    d_co = _wgrad(sv["s"], dya, "wgrad_conv_out")
    d_so = _wgrad(sv["um"], dyb, "wgrad_sg_out")
    d_wo = _wgrad(sv["o"], dyc, "wgrad_w_o")

    dq, dk, dv = _attn_bwd(sv["q"], sv["k"], sv["v"], sv["o"], sv["lse"], do)
    dqb, dkb, dvb, dqlat, dkvlat, dkrope, vec["mla_q_norm_g"], vec["mla_kv_norm_g"] = _mla_proj_bwd(
        l, dq, dk, dv, sv["qlat"], sv["kvlat"], cos_t, sin_t, w["mla_q_norm_g"], w["wuq_p"], w["mla_kv_norm_g"], w["wk_p"], w["wv_p"])
    d_uq = _wgrad(sv["qn"], dqb, "wgrad_w_uq")
    d_uk = _wgrad(sv["kvn"], dkb, "wgrad_w_uk")
    d_uv = _wgrad(sv["kvn"], dvb, "wgrad_w_uv")

    db_in, dsg_w, dsgb, vec["sg_ln_g"], vec["sg_ln_b"] = _sgu_bwd(l, sv["b_in"], dum, w["sg_ln_g"], w["sg_ln_b"], w["sg_w"], w["bexp"])
    da_in, d_cdw, vec["conv_dw_b"], vec["conv_ln_g"], vec["conv_ln_b"] = _conv_bwd(
        l, sv["a_in"], sv["z1"], ds, w["conv_dw_w_p"], w["conv_ln_g"], w["conv_ln_b"])

    dsegs = (da_in, db_in, dqlat, dkvlat, dkrope, dgates)
    dx, vec["mix_pre_g"] = _mix_in_bwd(l, sv["x"], w["mix_pre_g"], dx1, dsegs, w["w_in_p"])
    names = ("a", "b", "q", "kv", "kr", "g")
    d_in_segs = [_wgrad(sv["h"], dseg, "wgrad_w_in_" + nm) for nm, dseg in zip(names, dsegs)]

    send = {}
    send["w_in"] = _dis_w_in(d_in_segs, shard_cols["w_in"])
    (send["conv_out_w"], send["sg_out_w"], send["mla_w_uq"], send["mla_w_ukv"], send["mla_w_o"], send["conv_dw_w"],
     send["ffn_dw_w"]) = _dis_small(d_co, d_so, d_uq, d_uk, d_uv, d_wo, d_cdw, d_fdw)
    send["w_out"] = d_out.reshape(N_DEV, d_out.shape[0] // N_DEV, d_out.shape[1])
    rep_pack = _pack_rep([vec[n] for n, _, _, _ in REP_VECS[:-1]], dsgb)
    return dx, send, rep_pack, dsg_w


def kernel(x, positions, mix_pre_g, mix_post_g, ffn_pre_g, ffn_post_g, w_in, conv_dw_w, conv_dw_b, conv_ln_g, conv_ln_b, conv_out_w, sg_ln_g, sg_ln_b, sg_w, sg_b, sg_out_w, mla_q_norm_g, mla_w_uq, mla_kv_norm_g, mla_w_ukv, mla_w_o, w_out, ffn_w_up, ffn_dw_w, ffn_dw_b, ffn_w_down, loss_target, m_mix_pre_g, m_mix_post_g, m_ffn_pre_g, m_ffn_post_g, m_w_in, m_conv_dw_w, m_conv_dw_b, m_conv_ln_g, m_conv_ln_b, m_conv_out_w, m_sg_ln_g, m_sg_ln_b, m_sg_w, m_sg_b, m_sg_out_w, m_mla_q_norm_g, m_mla_w_uq, m_mla_kv_norm_g, m_mla_w_ukv, m_mla_w_o, m_w_out, m_ffn_w_up, m_ffn_dw_w, m_ffn_dw_b, m_ffn_w_down, v_mix_pre_g, v_mix_post_g, v_ffn_pre_g, v_ffn_post_g, v_w_in, v_conv_dw_w, v_conv_dw_b, v_conv_ln_g, v_conv_ln_b, v_conv_out_w, v_sg_ln_g, v_sg_ln_b, v_sg_w, v_sg_b, v_sg_out_w, v_mla_q_norm_g, v_mla_w_uq, v_mla_kv_norm_g, v_mla_w_ukv, v_mla_w_o, v_w_out, v_ffn_w_up, v_ffn_dw_w, v_ffn_dw_b, v_ffn_w_down):
    args = (x, positions, mix_pre_g, mix_post_g, ffn_pre_g, ffn_post_g, w_in, conv_dw_w, conv_dw_b, conv_ln_g, conv_ln_b, conv_out_w, sg_ln_g, sg_ln_b, sg_w, sg_b, sg_out_w, mla_q_norm_g, mla_w_uq, mla_kv_norm_g, mla_w_ukv, mla_w_o, w_out, ffn_w_up, ffn_dw_w, ffn_dw_b, ffn_w_down, loss_target, m_mix_pre_g, m_mix_post_g, m_ffn_pre_g, m_ffn_post_g, m_w_in, m_conv_dw_w, m_conv_dw_b, m_conv_ln_g, m_conv_ln_b, m_conv_out_w, m_sg_ln_g, m_sg_ln_b, m_sg_w, m_sg_b, m_sg_out_w, m_mla_q_norm_g, m_mla_w_uq, m_mla_kv_norm_g, m_mla_w_ukv, m_mla_w_o, m_w_out, m_ffn_w_up, m_ffn_dw_w, m_ffn_dw_b, m_ffn_w_down, v_mix_pre_g, v_mix_post_g, v_ffn_pre_g, v_ffn_post_g, v_w_in, v_conv_dw_w, v_conv_dw_b, v_conv_ln_g, v_conv_ln_b, v_conv_out_w, v_sg_ln_g, v_sg_ln_b, v_sg_w, v_sg_b, v_sg_out_w, v_mla_q_norm_g, v_mla_w_uq, v_mla_kv_norm_g, v_mla_w_ukv, v_mla_w_o, v_w_out, v_ffn_w_up, v_ffn_dw_w, v_ffn_dw_b, v_ffn_w_down)
    n_in = len(IN_NAMES)
    a = dict(zip(IN_NAMES, args[:n_in]))
    target = args[n_in]
    n_w = len(WEIGHTS)
    m_in = dict(zip(WEIGHTS, args[n_in + 1:n_in + 1 + n_w]))
    v_in = dict(zip(WEIGHTS, args[n_in + 1 + n_w:n_in + 1 + 2 * n_w]))
    depth = a["mix_pre_g"].shape[0]
    rep = {n: a[n] for n in WEIGHTS if n not in SHARDED}
    shard_cols = {n: a[n].shape[2] for n in SHARDED}

    starts, token = [], ()
    for l in range(depth):
        for group, tag in ((MIX_GROUP, "mix"), (FFN_BIG, "ffn")):
            wire = [(a[n][l] if n in WIRE_F32 else a[n][l].astype(bf16), "gather") for n in group]
            starts.append(_exchange_start(wire, "gather_%s_weights_%d" % (tag, l), deps=token))
            token = (starts[-1]["token"],)

    cos_t, sin_t = _rope_tables(a["positions"][0])
    xl = a["x"][0]
    ws, saved = [], []
    after = token
    for l in range(depth):
        w = _mixer_weights(dict(zip(MIX_GROUP, _exchange_wait(starts[2 * l], after=after))), rep, l)
        x1, sv = _mixer_fwd(xl, w, cos_t, sin_t)
        w.update(_ffn_weights(dict(zip(FFN_BIG, _exchange_wait(starts[2 * l + 1], after=(x1,))))))
        xl, sv_ffn = _ffn_layer_fwd(x1, w)
        sv.update(sv_ffn)
        ws.append(w)
        saved.append(sv)
        after = (xl,)
    dx, loss_part = _loss_fwd_bwd(xl, target[0])
    loss = lax.psum(loss_part, AXES)

    outs = {}
    rep_recvs, sgw_recvs = [None] * depth, [None] * depth

    def finish(l, h_ffn, h_mix, after):
        for names, handle in ((FFN_BIG, h_ffn), (MIX_GROUP, h_mix)):
            got = _exchange_wait(handle, after=after)
            for n, recv in zip(names, got):
                outs[n] = _adamw_shard(l, recv, a[n], m_in[n], v_in[n], outs.get(n), "adamw_" + n)
            after = tuple(outs[n][0] for n in names)
        rep_recvs[l], sgw_recvs[l] = got[len(MIX_GROUP)], got[len(MIX_GROUP) + 1]
        return after

    token, prev, after = (), None, ()
    for l in reversed(range(depth)):
        dx1, send_ffn, vec = _ffn_layer_bwd(dx, saved[l], ws[l], deps=token)
        h_ffn = _exchange_start([(send_ffn[n], "scatter") for n in FFN_BIG], "exchange_ffn_grads_%d" % l)
        dx, send_mix, rep_pack, dsg_w = _mixer_bwd(dx1, saved[l], ws[l], cos_t, sin_t, shard_cols, vec, deps=(h_ffn["token"],))
        h_mix = _exchange_start([(send_mix[n], "scatter") for n in MIX_GROUP] + [(rep_pack, "gather"), (dsg_w, "gather")],
                                "exchange_mix_grads_%d" % l)
        token = (h_mix["token"],)
        if prev is not None:
            after = finish(*prev, after=(dx,) + token)
        prev = (l, h_ffn, h_mix)
    finish(*prev, after=after)
    vec_names = [n for n, _, _, _ in REP_VECS]
    rep_outs = _adamw_rep(rep_recvs, [a[n] for n in vec_names], [m_in[n] for n in vec_names], [v_in[n] for n in vec_names])
    for i, n in enumerate(vec_names):
        outs[n] = rep_outs[4 * i:4 * i + 4]
    outs["sg_w"] = _adamw_sg_w(sgw_recvs, a["sg_w"], m_in["sg_w"], v_in["sg_w"])

    grad_w, delta_w, new_m, new_v = ([outs[n][j] for n in WEIGHTS] for j in range(4))
    return (loss, dx[None], *grad_w, *delta_w, *new_m, *new_v)
```

```python
import math

import jax
import jax.numpy as jnp
from jax import lax
from jax.experimental import pallas as pl
from jax.experimental.pallas import tpu as pltpu

f32 = jnp.float32
bf16 = jnp.bfloat16

N_DEV = 8
AXES = ("x", "y", "c")
EPS = 1e-6
D_CONV = 512
CONV_K = 31
CONV_HALO = 32
D_SG = 512
SG_GROUPS = 4
SG_CHUNK = 128
N_HEADS = 8
QK_NOPE = 64
QK_ROPE = 32
V_HEAD = 64
HEAD_PAD = 128
Q_LORA = 384
KV_LORA = 256
D_FF = 2816
FFN_K = 3
FFN_HALO = 8
ATT_SCALE = (QK_NOPE + QK_ROPE) ** -0.5
NEG = float(jnp.finfo(jnp.float32).min)

ADAM_LR = 0.001
ADAM_B1 = 0.9
ADAM_B2 = 0.999
ADAM_EPS = 1e-08
ADAM_WD = 0.01
ADAM_STEP = 10

LANES = 128
VMEM_MB = 56

REF_CUTS = (0, 1024, 2048, 2432, 2688, 2720, 5792)
SEGS = ((0, 1024), (1024, 1024), (2048, 384), (2432, 256), (2688, 128), (2816, 3072))
D_IN = 5792
D_IN_PAD = 5888
KR_LO = 64
SEG_INNER = (0, 0, 0, 0, KR_LO, 0)

IN_NAMES = ['x', 'positions', 'mix_pre_g', 'mix_post_g', 'ffn_pre_g', 'ffn_post_g', 'w_in', 'conv_dw_w', 'conv_dw_b', 'conv_ln_g', 'conv_ln_b', 'conv_out_w', 'sg_ln_g', 'sg_ln_b', 'sg_w', 'sg_b', 'sg_out_w', 'mla_q_norm_g', 'mla_w_uq', 'mla_kv_norm_g', 'mla_w_ukv', 'mla_w_o', 'w_out', 'ffn_w_up', 'ffn_dw_w', 'ffn_dw_b', 'ffn_w_down']
WEIGHTS = IN_NAMES[2:]
SHARDED = ("w_in", "conv_dw_w", "conv_out_w", "sg_out_w", "mla_w_uq", "mla_w_ukv", "mla_w_o", "w_out", "ffn_w_up", "ffn_dw_w",
           "ffn_w_down")
FFN_BIG = ("ffn_w_up", "ffn_w_down")
MIX_GROUP = tuple(n for n in SHARDED if n not in FFN_BIG)
WIRE_F32 = ("conv_dw_w", "ffn_dw_w")
REP_VECS = (("mix_pre_g", 0, 1024, 1), ("mix_post_g", 1, 1024, 1), ("ffn_pre_g", 2, 1024, 1), ("ffn_post_g", 3, 1024, 1),
            ("conv_dw_b", 4, 512, 1), ("conv_ln_g", 5, 512, 1), ("conv_ln_b", 6, 512, 1), ("sg_ln_g", 7, 512, 1),
            ("sg_ln_b", 8, 512, 1), ("mla_q_norm_g", 9, 384, 1), ("mla_kv_norm_g", 10, 256, 1), ("ffn_dw_b", 11, 5632, 1),
            ("sg_b", 12, 128, 4))
REP_ROWS = 16
REP_W = 5632


def _cparams(n_axes):
    return pltpu.CompilerParams(dimension_semantics=("arbitrary",) * n_axes, vmem_limit_bytes=VMEM_MB * 2 ** 20)


def _rows(tb, n):
    return pl.BlockSpec((tb, n), lambda i: (i, 0))


def _rows_rev(tb, n, nb):
    return pl.BlockSpec((tb, n), lambda i: (nb - 1 - i, 0))


def _const(shape):
    nd = len(shape)
    return pl.BlockSpec(shape, lambda *_: (0,) * nd, pipeline_mode=pl.Buffered(1))


def _acc(shape):
    nd = len(shape)
    return pl.BlockSpec(shape, lambda *_: (0,) * nd)


def _whole(arr):
    return pl.BlockSpec(arr.shape, lambda *_: (0,) * arr.ndim)


def _dot(a, b):
    return jnp.dot(a, b, preferred_element_type=f32)


def _dot_nt(a, b):
    return lax.dot_general(a, b, (((1,), (1,)), ((), ())), preferred_element_type=f32)


def _dot_tn(a, b):
    return lax.dot_general(a, b, (((0,), (0,)), ((), ())), preferred_element_type=f32)


def _mean(x):
    return jnp.mean(x, axis=-1, keepdims=True)


def _colsum(x):
    return jnp.sum(x, axis=0, keepdims=True)


def _rms_fwd(x, g):
    return x * lax.rsqrt(_mean(x * x) + EPS) * g


def _rms_bwd(x, g, dy):
    r = lax.rsqrt(_mean(x * x) + EPS)
    n = x * r
    dn = dy * g
    return r * (dn - n * _mean(dn * n)), _colsum(dy * n)


def _ln_stats(x):
    mu = _mean(x)
    d = x - mu
    rstd = lax.rsqrt(_mean(d * d) + EPS)
    return d * rstd, rstd


def _ln_bwd(xhat, rstd, g, dy):
    dxh = dy * g
    return rstd * (dxh - _mean(dxh) - xhat * _mean(dxh * xhat))


_GELU_C0 = math.sqrt(2.0 / math.pi)
_GELU_C1 = 0.044715


def _gelu(x):
    t = jnp.tanh(_GELU_C0 * (x + _GELU_C1 * (x * x * x)))
    return 0.5 * x * (1.0 + t)


def _gelu_and_grad(x):
    x2 = x * x
    t = jnp.tanh(_GELU_C0 * (x + _GELU_C1 * (x2 * x)))
    g = 0.5 * x * (1.0 + t)
    dg = 0.5 * (1.0 + t) + 0.5 * x * (1.0 - t * t) * (_GELU_C0 * (1.0 + 3.0 * _GELU_C1 * x2))
    return g, dg


def _swap_rope_halves(x):
    n = x.shape[1]
    half = QK_ROPE // 2
    lane = lax.broadcasted_iota(jnp.int32, x.shape, 1) % HEAD_PAD
    first = (lane >= KR_LO) & (lane < KR_LO + half)
    second = (lane >= KR_LO + half) & (lane < KR_LO + QK_ROPE)
    return jnp.where(first, pltpu.roll(x, n - half, 1), jnp.where(second, pltpu.roll(x, half, 1), 0.0))


def _tb(t):
    return min(256, t)


_HBM = pl.BlockSpec(memory_space=pltpu.HBM)
_SEM = pl.BlockSpec(memory_space=pltpu.SEMAPHORE)
_ANY = pl.BlockSpec(memory_space=pl.ANY)
_EFFECT = pltpu.SideEffectType.DATAFLOW_SIDE_EFFECTING


def _exchange_copies(modes, ins, lands, send_sems, recv_sems, loc_sems):
    x, y, c = lax.axis_index("x"), lax.axis_index("y"), lax.axis_index("c")
    me = 4 * x + 2 * y + c
    copies = []
    for a, mode in enumerate(modes):
        def src(dst_index, a=a, mode=mode):
            return ins[a].at[dst_index] if mode == "scatter" else ins[a]
        copies.append(pltpu.make_async_copy(src(me), lands[a].at[me], loc_sems.at[a]))
        for k in range(1, N_DEV):
            px = 1 - x if (k >> 2) & 1 else x
            py = 1 - y if (k >> 1) & 1 else y
            pc = 1 - c if k & 1 else c
            copies.append(pltpu.make_async_remote_copy(
                src_ref=src(4 * px + 2 * py + pc), dst_ref=lands[a].at[me],
                send_sem=send_sems.at[a * (N_DEV - 1) + k - 1], recv_sem=recv_sems.at[a * (N_DEV - 1) + k - 1],
                device_id=(px, py, pc), device_id_type=pl.DeviceIdType.MESH))
    return copies


def _exchange_start(ops, name, deps=()):
    n = len(ops)
    arrs = [arr for arr, _ in ops]
    modes = [mode for _, mode in ops]
    lands = [lax.empty((N_DEV,) + arr.shape if mode == "gather" else arr.shape, arr.dtype) for arr, mode in ops]

    def body(*refs):
        ins, land_refs = refs[:n], refs[n:2 * n]
        send_sems, recv_sems, loc_sems = refs[2 * n + len(deps):2 * n + len(deps) + 3]
        for cp in _exchange_copies(modes, ins, land_refs, send_sems, recv_sems, loc_sems):
            cp.start()
        refs[-1][...] = jnp.zeros((8, LANES), f32)

    n_rem = n * (N_DEV - 1)
    res = pl.pallas_call(
        body, name=name,
        out_shape=(pltpu.SemaphoreType.DMA((n_rem,)), pltpu.SemaphoreType.DMA((n_rem,)), pltpu.SemaphoreType.DMA((n,)),
                   *[pltpu.HBM(x.shape, x.dtype) for x in arrs + lands], jax.ShapeDtypeStruct((8, LANES), f32)),
        in_specs=[_HBM] * (2 * n) + [_ANY] * len(deps),
        out_specs=(_SEM, _SEM, _SEM, *[_HBM] * (2 * n), pl.BlockSpec(memory_space=pltpu.VMEM)),
        input_output_aliases={i: 3 + i for i in range(2 * n)},
        compiler_params=pltpu.CompilerParams(has_side_effects=_EFFECT),
    )(*[pltpu.with_memory_space_constraint(x, pltpu.HBM) for x in arrs + lands], *deps)
    return dict(modes=modes, sems=res[:3], thru=res[3:3 + 2 * n], token=res[-1], name=name)


def _exchange_wait(handle, after=()):
    modes, thru = handle["modes"], handle["thru"]
    n = len(modes)

    def body(*refs):
        ins, land_refs = refs[:n], refs[n:2 * n]
        send_sems, recv_sems, loc_sems = refs[2 * n:2 * n + 3]
        for cp in _exchange_copies(modes, ins, land_refs, send_sems, recv_sems, loc_sems):
            cp.wait()

    res = pl.pallas_call(
        body, name=handle["name"] + "_wait",
        out_shape=tuple(pltpu.HBM(x.shape, x.dtype) for x in thru),
        in_specs=[_HBM] * (2 * n) + [_SEM] * 3 + [_ANY] * len(after),
        out_specs=tuple([_HBM] * (2 * n)),
        input_output_aliases={i: i for i in range(2 * n)},
        compiler_params=pltpu.CompilerParams(has_side_effects=_EFFECT),
    )(*thru, *handle["sems"], *after)
    return res[n:]


_SAME_CORE_PEERS = ((1, 0), (0, 1), (1, 1))


def _gather_copies_one(ins, lands, send_sems, recv_sems, loc_sems):
    x, y, c = lax.axis_index("x"), lax.axis_index("y"), lax.axis_index("c")
    me = 4 * x + 2 * y + c
    targets = [(x, y, 1 - c)] + [(1 - x if fx else x, 1 - y if fy else y, c) for fx, fy in _SAME_CORE_PEERS]
    copies = []
    for a in range(len(ins)):
        copies.append(pltpu.make_async_copy(ins[a], lands[a].at[me], loc_sems.at[a]))
        for j, target in enumerate(targets):
            copies.append(pltpu.make_async_remote_copy(
                src_ref=ins[a], dst_ref=lands[a].at[me], send_sem=send_sems.at[4 * a + j], recv_sem=recv_sems.at[4 * a + j],
                device_id=target, device_id_type=pl.DeviceIdType.MESH))
    return copies


def _gather_copies_two(lands, send_sems, recv_sems):
    x, y, c = lax.axis_index("x"), lax.axis_index("y"), lax.axis_index("c")
    copies = []
    for a in range(len(lands)):
        for j, (fx, fy) in enumerate(_SAME_CORE_PEERS):
            slot = 4 * (1 - x if fx else x) + 2 * (1 - y if fy else y) + c
            copies.append(pltpu.make_async_remote_copy(
                src_ref=lands[a].at[slot], dst_ref=lands[a].at[slot], send_sem=send_sems.at[3 * a + j],
                recv_sem=recv_sems.at[3 * a + j], device_id=(x, y, 1 - c), device_id_type=pl.DeviceIdType.MESH))
    return copies


def _gather_start(arrs, name, deps=()):
    n = len(arrs)
    lands = [lax.empty((N_DEV,) + arr.shape, arr.dtype) for arr in arrs]

    def body(*refs):
        ins, land_refs = refs[:n], refs[n:2 * n]
        send_sems, recv_sems, loc_sems = refs[2 * n + len(deps):2 * n + len(deps) + 3]
        for cp in _gather_copies_one(ins, land_refs, send_sems, recv_sems, loc_sems):
            cp.start()
        refs[-1][...] = jnp.zeros((8, LANES), f32)

    res = pl.pallas_call(
        body, name=name,
        out_shape=(pltpu.SemaphoreType.DMA((4 * n,)), pltpu.SemaphoreType.DMA((4 * n,)), pltpu.SemaphoreType.DMA((n,)),
                   *[pltpu.HBM(x.shape, x.dtype) for x in arrs + lands], jax.ShapeDtypeStruct((8, LANES), f32)),
        in_specs=[_HBM] * (2 * n) + [_ANY] * len(deps),
        out_specs=(_SEM, _SEM, _SEM, *[_HBM] * (2 * n), pl.BlockSpec(memory_space=pltpu.VMEM)),
        input_output_aliases={i: 3 + i for i in range(2 * n)},
        compiler_params=pltpu.CompilerParams(has_side_effects=_EFFECT),
    )(*[pltpu.with_memory_space_constraint(x, pltpu.HBM) for x in arrs + lands], *deps)
    return dict(n=n, sems=res[:3], thru=res[3:3 + 2 * n], token=res[-1], name=name)


def _gather_forward(handle, after=()):
    n, thru = handle["n"], handle["thru"]

    def body(*refs):
        ins, land_refs = refs[:n], refs[n:2 * n]
        send_one, recv_one, loc_sems = refs[2 * n:2 * n + 3]
        send_two, recv_two = refs[2 * n + 3 + len(after):2 * n + 5 + len(after)]
        for cp in _gather_copies_one(ins, land_refs, send_one, recv_one, loc_sems):
            cp.wait()
        for cp in _gather_copies_two(land_refs, send_two, recv_two):
            cp.start()
        refs[-1][...] = jnp.zeros((8, LANES), f32)

    res = pl.pallas_call(
        body, name=handle["name"] + "_forward",
        out_shape=(pltpu.SemaphoreType.DMA((3 * n,)), pltpu.SemaphoreType.DMA((3 * n,)),
                   *[pltpu.HBM(x.shape, x.dtype) for x in thru], jax.ShapeDtypeStruct((8, LANES), f32)),
        in_specs=[_HBM] * (2 * n) + [_SEM] * 3 + [_ANY] * len(after),
        out_specs=(_SEM, _SEM, *[_HBM] * (2 * n), pl.BlockSpec(memory_space=pltpu.VMEM)),
        input_output_aliases={i: 2 + i for i in range(2 * n)},
        compiler_params=pltpu.CompilerParams(has_side_effects=_EFFECT),
    )(*thru, *handle["sems"], *after)
    return dict(n=n, sems=res[:2], lands=res[2 + n:2 + 2 * n], token=res[-1], name=handle["name"])


def _gather_wait(handle, after=()):
    n, lands = handle["n"], handle["lands"]

    def body(*refs):
        land_refs = refs[:n]
        send_two, recv_two = refs[n:n + 2]
        for cp in _gather_copies_two(land_refs, send_two, recv_two):
            cp.wait()

    return pl.pallas_call(
        body, name=handle["name"] + "_wait",
        out_shape=tuple(pltpu.HBM(x.shape, x.dtype) for x in lands),
        in_specs=[_HBM] * n + [_SEM] * 2 + [_ANY] * len(after),
        out_specs=tuple([_HBM] * n),
        input_output_aliases={i: i for i in range(n)},
        compiler_params=pltpu.CompilerParams(has_side_effects=_EFFECT),
    )(*lands, *handle["sems"], *after)


def _w_in_pieces(ns):
    out = []
    for e in range(N_DEV):
        lo, hi = ns * e, ns * (e + 1)
        for s in range(len(SEGS)):
            a, b = max(lo, REF_CUTS[s]), min(hi, REF_CUTS[s + 1])
            if a < b:
                inner = SEG_INNER[s] + a - REF_CUTS[s]
                out.append((e, a - lo, b - lo, s, inner, inner + b - a))
    return out


def _asm_w_in(g):
    _, d, ns = g.shape
    rb = 256
    pieces = _w_in_pieces(ns)

    def body(g_ref, o_ref):
        kr = SEGS[4][0]
        o_ref[:, kr:kr + KR_LO] = jnp.zeros((rb, KR_LO), g.dtype)
        o_ref[:, kr + KR_LO + QK_ROPE:kr + HEAD_PAD] = jnp.zeros((rb, HEAD_PAD - KR_LO - QK_ROPE), g.dtype)
        for e, s0, s1, seg, d0, d1 in pieces:
            off = SEGS[seg][0]
            o_ref[:, off + d0:off + d1] = g_ref[e, :, s0:s1]

    return pl.pallas_call(
        body, name="asm_w_in", grid=(d // rb,),
        in_specs=[pl.BlockSpec((N_DEV, rb, ns), lambda i: (0, i, 0))],
        out_specs=_rows(rb, D_IN_PAD), out_shape=jax.ShapeDtypeStruct((d, D_IN_PAD), g.dtype),
        compiler_params=_cparams(1),
    )(g)


def _dis_w_in(dsegs, ns):
    d = dsegs[0].shape[0]
    rb = 256
    pieces = _w_in_pieces(ns)

    def body(*refs):
        seg_refs, o_ref = refs[:len(SEGS)], refs[len(SEGS)]
        for e, s0, s1, seg, d0, d1 in pieces:
            o_ref[e, :, s0:s1] = seg_refs[seg][:, d0:d1].astype(bf16)

    return pl.pallas_call(
        body, name="dis_w_in", grid=(d // rb,),
        in_specs=[_rows(rb, wd) for _, wd in SEGS],
        out_specs=pl.BlockSpec((N_DEV, rb, ns), lambda i: (0, i, 0)),
        out_shape=jax.ShapeDtypeStruct((N_DEV, d, ns), bf16),
        compiler_params=_cparams(1),
    )(*dsegs)


def _asm_cols(g, name):
    _, k, ns = g.shape
    rb = 256

    def body(g_ref, o_ref):
        for e in range(N_DEV):
            o_ref[:, e * ns:(e + 1) * ns] = g_ref[e]

    return pl.pallas_call(
        body, name=name, grid=(k // rb,),
        in_specs=[pl.BlockSpec((N_DEV, rb, ns), lambda i: (0, i, 0))],
        out_specs=_rows(rb, N_DEV * ns), out_shape=jax.ShapeDtypeStruct((k, N_DEV * ns), g.dtype),
        compiler_params=_cparams(1),
    )(g)


def _dis_cols(w, name):
    k, n = w.shape
    ns = n // N_DEV
    rb = 256

    def body(w_ref, o_ref):
        for e in range(N_DEV):
            o_ref[e] = w_ref[:, e * ns:(e + 1) * ns].astype(bf16)

    return pl.pallas_call(
        body, name=name, grid=(k // rb,),
        in_specs=[_rows(rb, n)],
        out_specs=pl.BlockSpec((N_DEV, rb, ns), lambda i: (0, i, 0)),
        out_shape=jax.ShapeDtypeStruct((N_DEV, k, ns), bf16),
        compiler_params=_cparams(1),
    )(w)


def _asm_small(g_conv_out, g_sg_out, g_uq, g_ukv, g_wo, g_cdw, g_fdw):
    d = g_conv_out.shape[2] * N_DEV
    hw = N_HEADS * HEAD_PAD
    hq = QK_NOPE + QK_ROPE
    ff2 = g_fdw.shape[2] * N_DEV
    cw, fw = g_cdw.shape[2], g_fdw.shape[2]

    def body(co_ref, so_ref, uq_ref, ukv_ref, wo_ref, cdw_ref, fdw_ref, o_co, o_so, o_uq, o_k, o_v, o_wo, o_cdw, o_fdw):
        o_cdw[CONV_K:CONV_HALO, :] = jnp.zeros((CONV_HALO - CONV_K, D_CONV), f32)
        o_fdw[FFN_K:FFN_HALO, :] = jnp.zeros((FFN_HALO - FFN_K, ff2), f32)
        for e in range(N_DEV):
            cs = e * HEAD_PAD
            o_co[:, cs:cs + HEAD_PAD] = co_ref[e]
            o_so[:, cs:cs + HEAD_PAD] = so_ref[e]
            o_uq[:, cs:cs + hq] = uq_ref[e]
            o_uq[:, cs + hq:cs + HEAD_PAD] = jnp.zeros((Q_LORA, HEAD_PAD - hq), bf16)
            o_k[:, cs:cs + QK_NOPE] = ukv_ref[e, :, 0:QK_NOPE]
            o_k[:, cs + QK_NOPE:cs + HEAD_PAD] = jnp.zeros((KV_LORA, HEAD_PAD - QK_NOPE), bf16)
            o_v[:, cs:cs + V_HEAD] = ukv_ref[e, :, QK_NOPE:QK_NOPE + V_HEAD]
            o_v[:, cs + V_HEAD:cs + HEAD_PAD] = jnp.zeros((KV_LORA, HEAD_PAD - V_HEAD), bf16)
            for h in range(N_HEADS):
                o_wo[h * HEAD_PAD:h * HEAD_PAD + V_HEAD, cs:cs + HEAD_PAD] = wo_ref[e, h * V_HEAD:(h + 1) * V_HEAD, :]
                o_wo[h * HEAD_PAD + V_HEAD:(h + 1) * HEAD_PAD, cs:cs + HEAD_PAD] = jnp.zeros((HEAD_PAD - V_HEAD, HEAD_PAD), bf16)
            o_cdw[0:CONV_K, e * cw:(e + 1) * cw] = cdw_ref[e]
            o_fdw[0:FFN_K, e * fw:(e + 1) * fw] = fdw_ref[e]

    ins = (g_conv_out, g_sg_out, g_uq, g_ukv, g_wo, g_cdw, g_fdw)
    out_shape = (jax.ShapeDtypeStruct((D_CONV, d), bf16), jax.ShapeDtypeStruct((D_SG, d), bf16), jax.ShapeDtypeStruct((Q_LORA, hw), bf16),
                 jax.ShapeDtypeStruct((KV_LORA, hw), bf16), jax.ShapeDtypeStruct((KV_LORA, hw), bf16), jax.ShapeDtypeStruct((hw, d), bf16),
                 jax.ShapeDtypeStruct((CONV_HALO, D_CONV), f32), jax.ShapeDtypeStruct((FFN_HALO, ff2), f32))
    return pl.pallas_call(
        body, name="asm_small", grid=(1,),
        in_specs=[_whole(a) for a in ins], out_specs=tuple(_whole(o) for o in out_shape), out_shape=out_shape,
        compiler_params=_cparams(1),
    )(*ins)


def _dis_small(d_co, d_so, d_uq, d_k, d_v, d_wo, d_cdw, d_fdw):
    d = d_co.shape[1]
    hq = QK_NOPE + QK_ROPE
    cw, fw = D_CONV // N_DEV, d_fdw.shape[1] // N_DEV

    def body(co_ref, so_ref, uq_ref, k_ref, v_ref, wo_ref, cdw_ref, fdw_ref, o_co, o_so, o_uq, o_ukv, o_wo, o_cdw, o_fdw):
        for e in range(N_DEV):
            cs = e * HEAD_PAD
            o_co[e] = co_ref[:, cs:cs + HEAD_PAD].astype(bf16)
            o_so[e] = so_ref[:, cs:cs + HEAD_PAD].astype(bf16)
            o_uq[e] = uq_ref[:, cs:cs + hq].astype(bf16)
            o_ukv[e, :, 0:QK_NOPE] = k_ref[:, cs:cs + QK_NOPE].astype(bf16)
            o_ukv[e, :, QK_NOPE:QK_NOPE + V_HEAD] = v_ref[:, cs:cs + V_HEAD].astype(bf16)
            for h in range(N_HEADS):
                o_wo[e, h * V_HEAD:(h + 1) * V_HEAD, :] = wo_ref[h * HEAD_PAD:h * HEAD_PAD + V_HEAD, cs:cs + HEAD_PAD].astype(bf16)
            o_cdw[e] = cdw_ref[0:CONV_K, e * cw:(e + 1) * cw]
            o_fdw[e] = fdw_ref[0:FFN_K, e * fw:(e + 1) * fw]

    ins = (d_co, d_so, d_uq, d_k, d_v, d_wo, d_cdw, d_fdw)
    out_shape = (jax.ShapeDtypeStruct((N_DEV, D_CONV, d // N_DEV), bf16), jax.ShapeDtypeStruct((N_DEV, D_SG, d // N_DEV), bf16),
                 jax.ShapeDtypeStruct((N_DEV, Q_LORA, hq), bf16), jax.ShapeDtypeStruct((N_DEV, KV_LORA, QK_NOPE + V_HEAD), bf16),
                 jax.ShapeDtypeStruct((N_DEV, N_HEADS * V_HEAD, d // N_DEV), bf16), jax.ShapeDtypeStruct((N_DEV, CONV_K, cw), f32),
                 jax.ShapeDtypeStruct((N_DEV, FFN_K, fw), f32))
    return pl.pallas_call(
        body, name="dis_small", grid=(1,),
        in_specs=[_whole(a) for a in ins], out_specs=tuple(_whole(o) for o in out_shape), out_shape=out_shape,
        compiler_params=_cparams(1),
    )(*ins)


def _pack_rep(vec_grads, dsgb):
    def body(*refs):
        o_ref = refs[-1]
        o_ref[...] = jnp.zeros((REP_ROWS, REP_W), f32)
        for (_, row, width, nrows), ref in zip(REP_VECS, refs[:-1]):
            o_ref[row:row + nrows, 0:width] = ref[0:nrows, :]

    ins = tuple(vec_grads) + (dsgb,)
    return pl.pallas_call(
        body, name="pack_rep", grid=(1,),
        in_specs=[_whole(a) for a in ins], out_specs=pl.BlockSpec((REP_ROWS, REP_W), lambda i: (0, 0)),
        out_shape=jax.ShapeDtypeStruct((REP_ROWS, REP_W), f32), compiler_params=_cparams(1),
    )(*ins)


def _mix_in_fwd(l, x, g1, w_in_p):
    t, d = x.shape
    tb = _tb(t)

    def body(x_ref, g_ref, w_ref, h_ref, *outs):
        h = _rms_fwd(x_ref[...], g_ref[l:l + 1, :]).astype(bf16)
        h_ref[...] = h
        for (off, wd), o_ref in zip(SEGS, outs):
            o_ref[...] = _dot(h, w_ref[:, off:off + wd])

    return pl.pallas_call(
        body, name="mix_in_fwd", grid=(t // tb,),
        in_specs=[_rows(tb, d), _const(g1.shape), _const((d, D_IN_PAD))],
        out_specs=tuple([_rows(tb, d)] + [_rows(tb, wd) for _, wd in SEGS]),
        out_shape=tuple([jax.ShapeDtypeStruct((t, d), bf16)] + [jax.ShapeDtypeStruct((t, wd), f32) for _, wd in SEGS]),
        compiler_params=_cparams(1),
    )(x, g1, w_in_p)


def _conv_fwd(l, a_in, dw_w, dw_b, ln_g, ln_b):
    t = a_in.shape[0]
    tb = _tb(t)

    def body(a_ref, w_ref, b_ref, g_ref, be_ref, s_ref, z1_ref, win):
        @pl.when(pl.program_id(0) == 0)
        def _():
            win[0:CONV_HALO, :] = jnp.zeros((CONV_HALO, D_CONV), f32)

        a = a_ref[...]
        win[CONV_HALO:CONV_HALO + tb, :] = a[:, :D_CONV] * jax.nn.sigmoid(a[:, D_CONV:])
        for r0 in range(0, tb, LANES):
            for c0 in range(0, D_CONV, LANES):
                acc = jnp.broadcast_to(b_ref[l:l + 1, c0:c0 + LANES], (LANES, LANES))
                for j in range(CONV_K):
                    acc = acc + w_ref[j:j + 1, c0:c0 + LANES] * win[pl.ds(r0 + CONV_HALO - CONV_K + 1 + j, LANES), c0:c0 + LANES]
                z1_ref[r0:r0 + LANES, c0:c0 + LANES] = acc
        zh, _ = _ln_stats(z1_ref[...])
        zl = zh * g_ref[l:l + 1, :] + be_ref[l:l + 1, :]
        s_ref[...] = (zl * jax.nn.sigmoid(zl)).astype(bf16)
        win[0:CONV_HALO, :] = win[tb:tb + CONV_HALO, :]

    return pl.pallas_call(
        body, name="conv_fwd", grid=(t // tb,),
        in_specs=[_rows(tb, 2 * D_CONV), _const((CONV_HALO, D_CONV)), _const(dw_b.shape), _const(ln_g.shape), _const(ln_b.shape)],
        out_specs=(_rows(tb, D_CONV), _rows(tb, D_CONV)),
        out_shape=(jax.ShapeDtypeStruct((t, D_CONV), bf16), jax.ShapeDtypeStruct((t, D_CONV), f32)),
        scratch_shapes=[pltpu.VMEM((tb + CONV_HALO, D_CONV), f32)],
        compiler_params=_cparams(1),
    )(a_in, dw_w, dw_b, ln_g, ln_b)


def _tril_mask():
    r = lax.broadcasted_iota(jnp.int32, (SG_CHUNK, SG_CHUNK), 0)
    c = lax.broadcasted_iota(jnp.int32, (SG_CHUNK, SG_CHUNK), 1)
    return r >= c


def _sgu_fwd(l, b_in, ln_g, ln_b, sg_w, bexp):
    t = b_in.shape[0]
    tb = _tb(t)
    gw = D_SG // SG_GROUPS

    def body(b_ref, g_ref, be_ref, w_ref, bexp_ref, um_ref):
        gl = _gelu(b_ref[...])
        u = gl[:, :D_SG]
        vh, _ = _ln_stats(gl[:, D_SG:])
        vn = (vh * g_ref[l:l + 1, :] + be_ref[l:l + 1, :]).astype(bf16)
        tri = _tril_mask()
        for g in range(SG_GROUPS):
            wg = jnp.where(tri, w_ref[l, g], 0.0).astype(bf16)
            cs = slice(g * gw, (g + 1) * gw)
            for r0 in range(0, tb, SG_CHUNK):
                rs = slice(r0, r0 + SG_CHUNK)
                mixed = _dot(wg, vn[rs, cs]) + bexp_ref[:, cs]
                um_ref[rs, cs] = (u[rs, cs] * mixed).astype(bf16)

    return pl.pallas_call(
        body, name="sgu_fwd", grid=(t // tb,),
        in_specs=[_rows(tb, 2 * D_SG), _const(ln_g.shape), _const(ln_b.shape), _const(sg_w.shape), _const((SG_CHUNK, D_SG))],
        out_specs=_rows(tb, D_SG),
        out_shape=jax.ShapeDtypeStruct((t, D_SG), bf16),
        compiler_params=_cparams(1),
    )(b_in, ln_g, ln_b, sg_w, bexp)


def _mla_proj_fwd(l, qlat, kvlat, krope, cos_t, sin_t, gq, wuq_p, gkv, wk_p, wv_p):
    t = qlat.shape[0]
    tb = _tb(t)
    hw = N_HEADS * HEAD_PAD

    def body(ql_ref, kvl_ref, kr_ref, c_ref, s_ref, gq_ref, wq_ref, gkv_ref, wk_ref, wv_ref, q_ref, k_ref, v_ref, qn_ref, kvn_ref):
        cos_b, sin_b = c_ref[...], s_ref[...]
        qn = _rms_fwd(ql_ref[...], gq_ref[l:l + 1, :]).astype(bf16)
        qn_ref[...] = qn
        q = _dot(qn, wq_ref[...])
        sw = _swap_rope_halves(q)
        for h in range(N_HEADS):
            hs = slice(h * HEAD_PAD, (h + 1) * HEAD_PAD)
            q_ref[:, hs] = (q[:, hs] * cos_b + sw[:, hs] * sin_b).astype(bf16)
        kvn = _rms_fwd(kvl_ref[...], gkv_ref[l:l + 1, :]).astype(bf16)
        kvn_ref[...] = kvn
        kr = kr_ref[...]
        kpe = kr * cos_b + _swap_rope_halves(kr) * sin_b
        k = _dot(kvn, wk_ref[...])
        for h in range(N_HEADS):
            hs = slice(h * HEAD_PAD, (h + 1) * HEAD_PAD)
            k_ref[:, hs] = (k[:, hs] + kpe).astype(bf16)
        v_ref[...] = _dot(kvn, wv_ref[...]).astype(bf16)

    return pl.pallas_call(
        body, name="mla_proj_fwd", grid=(t // tb,),
        in_specs=[_rows(tb, Q_LORA), _rows(tb, KV_LORA), _rows(tb, HEAD_PAD), _rows(tb, HEAD_PAD), _rows(tb, HEAD_PAD),
                  _const(gq.shape), _const((Q_LORA, hw)), _const(gkv.shape), _const((KV_LORA, hw)), _const((KV_LORA, hw))],
        out_specs=(_rows(tb, hw), _rows(tb, hw), _rows(tb, hw), _rows(tb, Q_LORA), _rows(tb, KV_LORA)),
        out_shape=(jax.ShapeDtypeStruct((t, hw), bf16), jax.ShapeDtypeStruct((t, hw), bf16), jax.ShapeDtypeStruct((t, hw), bf16),
                   jax.ShapeDtypeStruct((t, Q_LORA), bf16), jax.ShapeDtypeStruct((t, KV_LORA), bf16)),
        compiler_params=_cparams(1),
    )(qlat, kvlat, krope, cos_t, sin_t, gq, wuq_p, gkv, wk_p, wv_p)


def _diag_mask(tq):
    return lax.broadcasted_iota(jnp.int32, (tq, tq), 0) >= lax.broadcasted_iota(jnp.int32, (tq, tq), 1)


def _attn_fwd(q, k, v):
    t = q.shape[0]
    tq = _tb(t)

    def body(q_ref, k_ref, v_ref, o_ref, lse_ref):
        qi = pl.program_id(1)
        for i in range(t // tq):
            @pl.when(qi == i)
            def _(i=i):
                qv = q_ref[...]
                lo, hi = i * tq, (i + 1) * tq
                s_d = jnp.where(_diag_mask(tq), _dot_nt(qv, k_ref[lo:hi, :]) * ATT_SCALE, NEG)
                m = jnp.max(s_d, axis=-1, keepdims=True)
                if i > 0:
                    s_o = _dot_nt(qv, k_ref[0:lo, :]) * ATT_SCALE
                    m = jnp.maximum(m, jnp.max(s_o, axis=-1, keepdims=True))
                p_d = jnp.exp(s_d - m)
                lsum = jnp.sum(p_d, axis=-1, keepdims=True)
                acc = _dot(p_d.astype(bf16), v_ref[lo:hi, :])
                if i > 0:
                    p_o = jnp.exp(s_o - m)
                    lsum = lsum + jnp.sum(p_o, axis=-1, keepdims=True)
                    acc = acc + _dot(p_o.astype(bf16), v_ref[0:lo, :])
                o_ref[...] = (acc / lsum).astype(bf16)
                lse_ref[...] = m + jnp.log(lsum)

    return pl.pallas_call(
        body, name="attn_fwd", grid=(N_HEADS, t // tq),
        in_specs=[pl.BlockSpec((tq, HEAD_PAD), lambda h, i: (i, h)), pl.BlockSpec((t, HEAD_PAD), lambda h, i: (0, h)),
                  pl.BlockSpec((t, HEAD_PAD), lambda h, i: (0, h))],
        out_specs=(pl.BlockSpec((tq, HEAD_PAD), lambda h, i: (i, h)), pl.BlockSpec((None, tq, 1), lambda h, i: (h, i, 0))),
        out_shape=(jax.ShapeDtypeStruct((t, N_HEADS * HEAD_PAD), bf16), jax.ShapeDtypeStruct((N_HEADS, t, 1), f32)),
        compiler_params=_cparams(2),
    )(q, k, v)


def _merge_out_fwd(l, x, s, um, o, gates, conv_out_w, sg_out_w, wo_p, w_out, g2):
    t, d = x.shape
    tb = _tb(t)

    def body(x_ref, s_ref, um_ref, o_ref, gt_ref, wa_ref, wb_ref, wc_ref, wout_ref, g_ref, x1_ref, mg_ref, om_ref):
        merged = (jax.nn.sigmoid(gt_ref[:, 0:d]) * _dot(s_ref[...], wa_ref[...])
                  + jax.nn.sigmoid(gt_ref[:, d:2 * d]) * _dot(um_ref[...], wb_ref[...])
                  + jax.nn.sigmoid(gt_ref[:, 2 * d:3 * d]) * _dot(o_ref[...], wc_ref[...]))
        mb = merged.astype(bf16)
        mg_ref[...] = mb
        om = _dot(mb, wout_ref[...])
        om_ref[...] = om
        x1_ref[...] = x_ref[...] + _rms_fwd(om, g_ref[l:l + 1, :])

    hw = N_HEADS * HEAD_PAD
    return pl.pallas_call(
        body, name="merge_out_fwd", grid=(t // tb,),
        in_specs=[_rows(tb, d), _rows(tb, D_CONV), _rows(tb, D_SG), _rows(tb, hw), _rows(tb, 3 * d),
                  _const((D_CONV, d)), _const((D_SG, d)), _const((hw, d)), _const((d, d)), _const(g2.shape)],
        out_specs=(_rows(tb, d), _rows(tb, d), _rows(tb, d)),
        out_shape=(jax.ShapeDtypeStruct((t, d), f32), jax.ShapeDtypeStruct((t, d), bf16), jax.ShapeDtypeStruct((t, d), f32)),
        compiler_params=_cparams(1),
    )(x, s, um, o, gates, conv_out_w, sg_out_w, wo_p, w_out, g2)


FF_CHUNK = 1408


def _ffn_conv_cols(zbuf, w_ref, b_ref, l, nrows, c0, c1):
    acc = b_ref[l:l + 1, c0:c1] + w_ref[0:1, c0:c1] * zbuf[pl.ds(FFN_HALO - 2, nrows), c0:c1]
    acc = acc + w_ref[1:2, c0:c1] * zbuf[pl.ds(FFN_HALO - 1, nrows), c0:c1]
    return acc + w_ref[2:3, c0:c1] * zbuf[pl.ds(FFN_HALO, nrows), c0:c1]


def _ffn_fwd(l, x1, g3, w_up, dw_w, dw_b, w_down, g4):
    t, d = x1.shape
    tb = _tb(t)
    ff2 = 2 * D_FF

    def body(x_ref, g3_ref, wup_ref, dww_ref, dwb_ref, wdn_ref, g4_ref, x2_ref, h2_ref, z_ref, zc_ref, act_ref, f_ref, zbuf):
        @pl.when(pl.program_id(0) == 0)
        def _():
            zbuf[0:FFN_HALO, :] = jnp.zeros((FFN_HALO, ff2), f32)

        xv = x_ref[...]
        h2 = _rms_fwd(xv, g3_ref[l:l + 1, :]).astype(bf16)
        h2_ref[...] = h2
        for c0 in range(0, ff2, FF_CHUNK):
            zv = _dot(h2, wup_ref[:, c0:c0 + FF_CHUNK])
            z_ref[:, c0:c0 + FF_CHUNK] = zv.astype(bf16)
            zbuf[FFN_HALO:FFN_HALO + tb, c0:c0 + FF_CHUNK] = zv
        facc = jnp.zeros((tb, d), f32)
        for c0 in range(0, D_FF, FF_CHUNK):
            gg = _ffn_conv_cols(zbuf, dww_ref, dwb_ref, l, tb, c0, c0 + FF_CHUNK)
            vv = _ffn_conv_cols(zbuf, dww_ref, dwb_ref, l, tb, D_FF + c0, D_FF + c0 + FF_CHUNK)
            zc_ref[:, c0:c0 + FF_CHUNK] = gg.astype(bf16)
            zc_ref[:, D_FF + c0:D_FF + c0 + FF_CHUNK] = vv.astype(bf16)
            a = (_gelu(gg) * vv).astype(bf16)
            act_ref[:, c0:c0 + FF_CHUNK] = a
            facc = facc + _dot(a, wdn_ref[c0:c0 + FF_CHUNK, :])
        f_ref[...] = facc
        x2_ref[...] = xv + _rms_fwd(facc, g4_ref[l:l + 1, :])
        zbuf[0:FFN_HALO, :] = zbuf[tb:tb + FFN_HALO, :]

    return pl.pallas_call(
        body, name="ffn_fwd", grid=(t // tb,),
        in_specs=[_rows(tb, d), _const(g3.shape), _const((d, ff2)), _const((FFN_HALO, ff2)), _const(dw_b.shape), _const((D_FF, d)),
                  _const(g4.shape)],
        out_specs=(_rows(tb, d), _rows(tb, d), _rows(tb, ff2), _rows(tb, ff2), _rows(tb, D_FF), _rows(tb, d)),
        out_shape=(jax.ShapeDtypeStruct((t, d), f32), jax.ShapeDtypeStruct((t, d), bf16), jax.ShapeDtypeStruct((t, ff2), bf16),
                   jax.ShapeDtypeStruct((t, ff2), bf16), jax.ShapeDtypeStruct((t, D_FF), bf16), jax.ShapeDtypeStruct((t, d), f32)),
        scratch_shapes=[pltpu.VMEM((tb + FFN_HALO, ff2), f32)],
        compiler_params=_cparams(1),
    )(x1, g3, w_up, dw_w, dw_b, w_down, g4)


def _loss_fwd_bwd(y, target):
    t, d = y.shape
    tb = _tb(t)

    def body(y_ref, t_ref, dy_ref, loss_ref):
        @pl.when(pl.program_id(0) == 0)
        def _():
            loss_ref[...] = jnp.zeros((1, LANES), f32)

        e = y_ref[...] - t_ref[...]
        dy_ref[...] = e * (1.0 / d)
        loss_ref[...] += 0.5 * jnp.sum(_mean(e * e))

    dy, loss = pl.pallas_call(
        body, name="loss", grid=(t // tb,),
        in_specs=[_rows(tb, d), _rows(tb, d)],
        out_specs=(_rows(tb, d), _acc((1, LANES))),
        out_shape=(jax.ShapeDtypeStruct((t, d), f32), jax.ShapeDtypeStruct((1, LANES), f32)),
        compiler_params=_cparams(1),
    )(y, target)
    return dy, loss[0, 0]


def _ffn_bwd(l, dx2, x1, f, z, zc, w_up, dw_w, w_down, g3, g4, deps=()):
    t, d = x1.shape
    tb = min(128, t)
    nb = t // tb
    ff2 = 2 * D_FF
    hrows = 16
    per_h = tb // hrows

    def body(*refs):
        (dx2_ref, x1_ref, f_ref, z_ref, zp_ref, zc_ref, wup_ref, dww_ref, wdn_ref, g3_ref, g4_ref,
         dx1_ref, df_ref, dz_ref, dg4_ref, dg3_ref, ddwb_ref, ddww_ref, zbuf, dzc) = refs[len(deps):]
        i = pl.program_id(0)
        b = nb - 1 - i

        @pl.when(i == 0)
        def _():
            dg4_ref[...] = jnp.zeros_like(dg4_ref)
            dg3_ref[...] = jnp.zeros_like(dg3_ref)
            ddwb_ref[...] = jnp.zeros_like(ddwb_ref)
            ddww_ref[...] = jnp.zeros_like(ddww_ref)
            dzc[tb:tb + FFN_HALO, :] = jnp.zeros((FFN_HALO, ff2), f32)

        dout = dx2_ref[...]
        df, dg4 = _rms_bwd(f_ref[...], g4_ref[l:l + 1, :], dout)
        dg4_ref[...] += dg4
        dfb = df.astype(bf16)
        df_ref[...] = dfb
        zbuf[0:FFN_HALO, :] = jnp.where(b > 0, zp_ref[hrows - FFN_HALO:hrows, :].astype(f32), 0.0)
        zbuf[FFN_HALO:FFN_HALO + tb, :] = z_ref[...].astype(f32)
        for c0 in range(0, D_FF, FF_CHUNK):
            dact = _dot_nt(dfb, wdn_ref[c0:c0 + FF_CHUNK, :])
            gel, dgel = _gelu_and_grad(zc_ref[:, c0:c0 + FF_CHUNK].astype(f32))
            dzc[0:tb, c0:c0 + FF_CHUNK] = dact * zc_ref[:, D_FF + c0:D_FF + c0 + FF_CHUNK].astype(f32) * dgel
            dzc[0:tb, D_FF + c0:D_FF + c0 + FF_CHUNK] = dact * gel
        dh2 = jnp.zeros((tb, d), f32)
        for c0 in range(0, ff2, FF_CHUNK):
            cs = slice(c0, c0 + FF_CHUNK)
            d0 = dzc[0:tb, cs]
            ddwb_ref[:, cs] += _colsum(d0)
            for j in range(FFN_K):
                ddww_ref[j:j + 1, cs] += _colsum(d0 * zbuf[pl.ds(FFN_HALO - 2 + j, tb), cs])
            dzv = dww_ref[2:3, cs] * d0 + dww_ref[1:2, cs] * dzc[pl.ds(1, tb), cs] + dww_ref[0:1, cs] * dzc[pl.ds(2, tb), cs]
            dzb = dzv.astype(bf16)
            dz_ref[:, cs] = dzb
            dh2 = dh2 + _dot_nt(dzb, wup_ref[:, cs])
        dzc[tb:tb + FFN_HALO, :] = dzc[0:FFN_HALO, :]
        dxn, dg3 = _rms_bwd(x1_ref[...], g3_ref[l:l + 1, :], dh2)
        dg3_ref[...] += dg3
        dx1_ref[...] = dout + dxn

    return pl.pallas_call(
        body, name="ffn_bwd", grid=(nb,),
        in_specs=[_ANY] * len(deps) + [_rows_rev(tb, d, nb), _rows_rev(tb, d, nb), _rows_rev(tb, d, nb), _rows_rev(tb, ff2, nb),
                  pl.BlockSpec((hrows, ff2), lambda i: (jnp.maximum((nb - 1 - i) * per_h - 1, 0), 0)), _rows_rev(tb, ff2, nb),
                  _const((d, ff2)), _const((FFN_HALO, ff2)), _const((D_FF, d)), _const(g3.shape), _const(g4.shape)],
        out_specs=(_rows_rev(tb, d, nb), _rows_rev(tb, d, nb), _rows_rev(tb, ff2, nb), _acc((1, d)), _acc((1, d)), _acc((1, ff2)),
                   _acc((FFN_HALO, ff2))),
        out_shape=(jax.ShapeDtypeStruct((t, d), f32), jax.ShapeDtypeStruct((t, d), bf16), jax.ShapeDtypeStruct((t, ff2), bf16),
                   jax.ShapeDtypeStruct((1, d), f32), jax.ShapeDtypeStruct((1, d), f32), jax.ShapeDtypeStruct((1, ff2), f32),
                   jax.ShapeDtypeStruct((FFN_HALO, ff2), f32)),
        scratch_shapes=[pltpu.VMEM((tb + FFN_HALO, ff2), f32), pltpu.VMEM((tb + FFN_HALO, ff2), f32)],
        compiler_params=_cparams(1),
    )(*deps, dx2, x1, f, z, z, zc, w_up, dw_w, w_down, g3, g4)


def _merge_out_bwd(l, dx1, om, s, um, o, gates, conv_out_w, sg_out_w, wo_p, w_out, g2, deps=()):
    t, d = dx1.shape
    tb = _tb(t)
    hw = N_HEADS * HEAD_PAD

    def body(*refs):
        (dx_ref, om_ref, s_ref, um_ref, o_ref, gt_ref, wa_ref, wb_ref, wc_ref, wout_ref, g_ref,
         dom_ref, dgt_ref, dya_ref, dyb_ref, dyc_ref, ds_ref, dum_ref, do_ref, dg2_ref) = refs[len(deps):]

        @pl.when(pl.program_id(0) == 0)
        def _():
            dg2_ref[...] = jnp.zeros_like(dg2_ref)

        dom, dg2 = _rms_bwd(om_ref[...], g_ref[l:l + 1, :], dx_ref[...])
        dg2_ref[...] += dg2
        domb = dom.astype(bf16)
        dom_ref[...] = domb
        dmerged = _dot_nt(domb, wout_ref[...])
        branches = ((s_ref, wa_ref, dya_ref, ds_ref), (um_ref, wb_ref, dyb_ref, dum_ref), (o_ref, wc_ref, dyc_ref, do_ref))
        for br, (in_ref, w_ref, dy_ref, din_ref) in enumerate(branches):
            yv = _dot(in_ref[...], w_ref[...])
            sg = jax.nn.sigmoid(gt_ref[:, br * d:(br + 1) * d])
            dyb = (dmerged * sg).astype(bf16)
            dy_ref[...] = dyb
            dgt_ref[:, br * d:(br + 1) * d] = (dmerged * yv * sg * (1.0 - sg)).astype(bf16)
            din_ref[...] = _dot_nt(dyb, w_ref[...]).astype(din_ref.dtype)

    return pl.pallas_call(
        body, name="merge_out_bwd", grid=(t // tb,),
        in_specs=[_ANY] * len(deps) + [_rows(tb, d), _rows(tb, d), _rows(tb, D_CONV), _rows(tb, D_SG), _rows(tb, hw), _rows(tb, 3 * d),
                  _const((D_CONV, d)), _const((D_SG, d)), _const((hw, d)), _const((d, d)), _const(g2.shape)],
        out_specs=(_rows(tb, d), _rows(tb, 3 * d), _rows(tb, d), _rows(tb, d), _rows(tb, d), _rows(tb, D_CONV), _rows(tb, D_SG),
                   _rows(tb, hw), _acc((1, d))),
        out_shape=(jax.ShapeDtypeStruct((t, d), bf16), jax.ShapeDtypeStruct((t, 3 * d), bf16), jax.ShapeDtypeStruct((t, d), bf16),
                   jax.ShapeDtypeStruct((t, d), bf16), jax.ShapeDtypeStruct((t, d), bf16), jax.ShapeDtypeStruct((t, D_CONV), f32),
                   jax.ShapeDtypeStruct((t, D_SG), f32), jax.ShapeDtypeStruct((t, hw), bf16), jax.ShapeDtypeStruct((1, d), f32)),
        compiler_params=_cparams(1),
    )(*deps, dx1, om, s, um, o, gates, conv_out_w, sg_out_w, wo_p, w_out, g2)


def _attn_bwd(q, k, v, o, lse, do):
    t = q.shape[0]
    tq = _tb(t)
    hw = N_HEADS * HEAD_PAD

    def body(q_ref, k_ref, v_ref, o_ref, lse_ref, do_ref, dq_ref, dk_ref, dv_ref):
        qi = pl.program_id(1)

        @pl.when(qi == 0)
        def _():
            dk_ref[...] = jnp.zeros_like(dk_ref)
            dv_ref[...] = jnp.zeros_like(dv_ref)

        def keys(lo, hi, qv, dov, lse, delta, diagonal):
            kj, vj = k_ref[lo:hi, :], v_ref[lo:hi, :]
            p = jnp.exp(_dot_nt(qv, kj) * ATT_SCALE - lse)
            if diagonal:
                p = jnp.where(_diag_mask(tq), p, 0.0)
            ds = (p * (_dot_nt(dov, vj) - delta) * ATT_SCALE).astype(bf16)
            dk_ref[lo:hi, :] += _dot_tn(ds, qv)
            dv_ref[lo:hi, :] += _dot_tn(p.astype(bf16), dov)
            return _dot(ds, kj)

        for i in range(t // tq):
            @pl.when(qi == i)
            def _(i=i):
                qv, dov, lse = q_ref[...], do_ref[...], lse_ref[...]
                delta = jnp.sum(dov.astype(f32) * o_ref[...].astype(f32), axis=-1, keepdims=True)
                dq = keys(i * tq, (i + 1) * tq, qv, dov, lse, delta, True)
                if i > 0:
                    dq = dq + keys(0, i * tq, qv, dov, lse, delta, False)
                dq_ref[...] = dq

    blk_q = pl.BlockSpec((tq, HEAD_PAD), lambda h, i: (i, h))
    blk_kv = pl.BlockSpec((t, HEAD_PAD), lambda h, i: (0, h))
    return pl.pallas_call(
        body, name="attn_bwd", grid=(N_HEADS, t // tq),
        in_specs=[blk_q, blk_kv, blk_kv, blk_q, pl.BlockSpec((None, tq, 1), lambda h, i: (h, i, 0)), blk_q],
        out_specs=(blk_q, blk_kv, blk_kv),
        out_shape=(jax.ShapeDtypeStruct((t, hw), f32), jax.ShapeDtypeStruct((t, hw), f32), jax.ShapeDtypeStruct((t, hw), f32)),
        compiler_params=_cparams(2),
    )(q, k, v, o, lse, do)


def _mla_proj_bwd(l, dq, dk, dv, qlat, kvlat, cos_t, sin_t, gq, wuq_p, gkv, wk_p, wv_p):
    t = qlat.shape[0]
    tb = _tb(t)
    hw = N_HEADS * HEAD_PAD

    def body(dq_ref, dk_ref, dv_ref, ql_ref, kvl_ref, c_ref, s_ref, gq_ref, wq_ref, gkv_ref, wk_ref, wv_ref,
             dqb_ref, dkb_ref, dvb_ref, dql_ref, dkvl_ref, dkr_ref, dgq_ref, dgkv_ref):
        @pl.when(pl.program_id(0) == 0)
        def _():
            dgq_ref[...] = jnp.zeros_like(dgq_ref)
            dgkv_ref[...] = jnp.zeros_like(dgkv_ref)

        cos_b, sin_b = c_ref[...], s_ref[...]
        for h in range(N_HEADS):
            hs = slice(h * HEAD_PAD, (h + 1) * HEAD_PAD)
            dqh = dq_ref[:, hs]
            dqb_ref[:, hs] = (dqh * cos_b + _swap_rope_halves(dqh * sin_b)).astype(bf16)
        dqn = _dot_nt(dqb_ref[...], wq_ref[...])
        dql, dgq = _rms_bwd(ql_ref[...], gq_ref[l:l + 1, :], dqn)
        dgq_ref[...] += dgq
        dql_ref[...] = dql.astype(bf16)
        dkv_full = dk_ref[...]
        dkb = dkv_full.astype(bf16)
        dkb_ref[...] = dkb
        dkpe = dkv_full[:, 0:HEAD_PAD]
        for h in range(1, N_HEADS):
            dkpe = dkpe + dkv_full[:, h * HEAD_PAD:(h + 1) * HEAD_PAD]
        dkr_ref[...] = (dkpe * cos_b + _swap_rope_halves(dkpe * sin_b)).astype(bf16)
        dvb = dv_ref[...].astype(bf16)
        dvb_ref[...] = dvb
        dkvn = _dot_nt(dkb, wk_ref[...]) + _dot_nt(dvb, wv_ref[...])
        dkvl, dgkv = _rms_bwd(kvl_ref[...], gkv_ref[l:l + 1, :], dkvn)
        dgkv_ref[...] += dgkv
        dkvl_ref[...] = dkvl.astype(bf16)

    return pl.pallas_call(
        body, name="mla_proj_bwd", grid=(t // tb,),
        in_specs=[_rows(tb, hw), _rows(tb, hw), _rows(tb, hw), _rows(tb, Q_LORA), _rows(tb, KV_LORA), _rows(tb, HEAD_PAD),
                  _rows(tb, HEAD_PAD), _const(gq.shape), _const((Q_LORA, hw)), _const(gkv.shape), _const((KV_LORA, hw)),
                  _const((KV_LORA, hw))],
        out_specs=(_rows(tb, hw), _rows(tb, hw), _rows(tb, hw), _rows(tb, Q_LORA), _rows(tb, KV_LORA), _rows(tb, HEAD_PAD),
                   _acc((1, Q_LORA)), _acc((1, KV_LORA))),
        out_shape=(jax.ShapeDtypeStruct((t, hw), bf16), jax.ShapeDtypeStruct((t, hw), bf16), jax.ShapeDtypeStruct((t, hw), bf16),
                   jax.ShapeDtypeStruct((t, Q_LORA), bf16), jax.ShapeDtypeStruct((t, KV_LORA), bf16),
                   jax.ShapeDtypeStruct((t, HEAD_PAD), bf16), jax.ShapeDtypeStruct((1, Q_LORA), f32),
                   jax.ShapeDtypeStruct((1, KV_LORA), f32)),
        compiler_params=_cparams(1),
    )(dq, dk, dv, qlat, kvlat, cos_t, sin_t, gq, wuq_p, gkv, wk_p, wv_p)


def _sgu_bwd(l, b_in, dum, ln_g, ln_b, sg_w, bexp):
    t = b_in.shape[0]
    tb = _tb(t)
    gw = D_SG // SG_GROUPS

    def body(b_ref, dum_ref, g_ref, be_ref, w_ref, bexp_ref, db_ref, dw_ref, dsgb_ref, dlg_ref, dlb_ref, dvn_s):
        @pl.when(pl.program_id(0) == 0)
        def _():
            dw_ref[...] = jnp.zeros_like(dw_ref)
            dsgb_ref[...] = jnp.zeros_like(dsgb_ref)
            dlg_ref[...] = jnp.zeros_like(dlg_ref)
            dlb_ref[...] = jnp.zeros_like(dlb_ref)

        gl, dgl = _gelu_and_grad(b_ref[...])
        u = gl[:, :D_SG]
        vh, rstd = _ln_stats(gl[:, D_SG:])
        ln_gain = g_ref[l:l + 1, :]
        vn = (vh * ln_gain + be_ref[l:l + 1, :]).astype(bf16)
        dumv = dum_ref[...]
        tri = _tril_mask()
        ones = jnp.ones((FFN_HALO, gw), f32)
        for g in range(SG_GROUPS):
            wg = jnp.where(tri, w_ref[l, g], 0.0).astype(bf16)
            cs = slice(g * gw, (g + 1) * gw)
            for r0 in range(0, tb, SG_CHUNK):
                rs = slice(r0, r0 + SG_CHUNK)
                vblk = vn[rs, cs]
                mixed = _dot(wg, vblk) + bexp_ref[:, cs]
                db_ref[rs, cs] = (dumv[rs, cs] * mixed * dgl[rs, cs]).astype(bf16)
                dmix = dumv[rs, cs] * u[rs, cs]
                dmb = dmix.astype(bf16)
                dw_ref[g] += jnp.where(tri, _dot_nt(dmb, vblk), 0.0)
                rowsum = lax.dot_general(ones, dmix, (((1,), (1,)), ((), ())), preferred_element_type=f32,
                                         precision=lax.Precision.HIGHEST)
                dsgb_ref[g:g + 1, :] += rowsum[0:1, :]
                dvn_s[rs, cs] = _dot_tn(wg, dmb)
        dvn = dvn_s[...]
        dlg_ref[...] += _colsum(dvn * vh)
        dlb_ref[...] += _colsum(dvn)
        db_ref[:, D_SG:] = (_ln_bwd(vh, rstd, ln_gain, dvn) * dgl[:, D_SG:]).astype(bf16)

    return pl.pallas_call(
        body, name="sgu_bwd", grid=(t // tb,),
        in_specs=[_rows(tb, 2 * D_SG), _rows(tb, D_SG), _const(ln_g.shape), _const(ln_b.shape), _const(sg_w.shape),
                  _const((SG_CHUNK, D_SG))],
        out_specs=(_rows(tb, 2 * D_SG), _acc((SG_GROUPS, SG_CHUNK, SG_CHUNK)), _acc((FFN_HALO, SG_CHUNK)), _acc((1, D_SG)),
                   _acc((1, D_SG))),
        out_shape=(jax.ShapeDtypeStruct((t, 2 * D_SG), bf16), jax.ShapeDtypeStruct((SG_GROUPS, SG_CHUNK, SG_CHUNK), f32),
                   jax.ShapeDtypeStruct((FFN_HALO, SG_CHUNK), f32), jax.ShapeDtypeStruct((1, D_SG), f32),
                   jax.ShapeDtypeStruct((1, D_SG), f32)),
        scratch_shapes=[pltpu.VMEM((tb, D_SG), f32)],
        compiler_params=_cparams(1),
    )(b_in, dum, ln_g, ln_b, sg_w, bexp)


def _conv_bwd(l, a_in, z1, ds, dw_w, ln_g, ln_b):
    t = a_in.shape[0]
    tb = _tb(t)
    nb = t // tb
    per_halo = tb // CONV_HALO

    def body(a_ref, ap_ref, z1_ref, ds_ref, w_ref, g_ref, be_ref, da_ref, ddww_ref, ddwb_ref, dlg_ref, dlb_ref, win, dzb):
        i = pl.program_id(0)
        b = nb - 1 - i

        @pl.when(i == 0)
        def _():
            ddww_ref[...] = jnp.zeros_like(ddww_ref)
            ddwb_ref[...] = jnp.zeros_like(ddwb_ref)
            dlg_ref[...] = jnp.zeros_like(dlg_ref)
            dlb_ref[...] = jnp.zeros_like(dlb_ref)
            dzb[tb:tb + CONV_HALO, :] = jnp.zeros((CONV_HALO, D_CONV), f32)

        a = a_ref[...]
        val = a[:, :D_CONV]
        sg = jax.nn.sigmoid(a[:, D_CONV:])
        ap = ap_ref[...]
        win[0:CONV_HALO, :] = jnp.where(b > 0, ap[:, :D_CONV] * jax.nn.sigmoid(ap[:, D_CONV:]), 0.0)
        win[CONV_HALO:CONV_HALO + tb, :] = val * sg
        zh, rstd = _ln_stats(z1_ref[...])
        ln_gain = g_ref[l:l + 1, :]
        zl = zh * ln_gain + be_ref[l:l + 1, :]
        sgl = jax.nn.sigmoid(zl)
        dzl = ds_ref[...] * (sgl * (1.0 + zl * (1.0 - sgl)))
        dlg_ref[...] += _colsum(dzl * zh)
        dlb_ref[...] += _colsum(dzl)
        dz1 = _ln_bwd(zh, rstd, ln_gain, dzl)
        dzb[0:tb, :] = dz1
        ddwb_ref[...] += _colsum(dz1)
        dgate_f = val * sg * (1.0 - sg)
        for c0 in range(0, D_CONV, LANES):
            cs = slice(c0, c0 + LANES)
            for r0 in range(0, tb, LANES):
                d1 = dzb[r0:r0 + LANES, cs]
                acc = jnp.zeros((LANES, LANES), f32)
                for j in range(CONV_K):
                    ddww_ref[j:j + 1, cs] += _colsum(d1 * win[pl.ds(r0 + CONV_HALO - CONV_K + 1 + j, LANES), cs])
                    acc = acc + w_ref[j:j + 1, cs] * dzb[pl.ds(r0 + CONV_K - 1 - j, LANES), cs]
                da_ref[r0:r0 + LANES, cs] = (acc * sg[r0:r0 + LANES, cs]).astype(bf16)
                da_ref[r0:r0 + LANES, c0 + D_CONV:c0 + D_CONV + LANES] = (acc * dgate_f[r0:r0 + LANES, cs]).astype(bf16)
        dzb[tb:tb + CONV_HALO, :] = dzb[0:CONV_HALO, :]

    return pl.pallas_call(
        body, name="conv_bwd", grid=(nb,),
        in_specs=[_rows_rev(tb, 2 * D_CONV, nb),
                  pl.BlockSpec((CONV_HALO, 2 * D_CONV), lambda i: (jnp.maximum((nb - 1 - i) * per_halo - 1, 0), 0)),
                  _rows_rev(tb, D_CONV, nb), _rows_rev(tb, D_CONV, nb), _const((CONV_HALO, D_CONV)), _const(ln_g.shape),
                  _const(ln_b.shape)],
        out_specs=(_rows_rev(tb, 2 * D_CONV, nb), _acc((CONV_HALO, D_CONV)), _acc((1, D_CONV)), _acc((1, D_CONV)), _acc((1, D_CONV))),
        out_shape=(jax.ShapeDtypeStruct((t, 2 * D_CONV), bf16), jax.ShapeDtypeStruct((CONV_HALO, D_CONV), f32),
                   jax.ShapeDtypeStruct((1, D_CONV), f32), jax.ShapeDtypeStruct((1, D_CONV), f32), jax.ShapeDtypeStruct((1, D_CONV), f32)),
        scratch_shapes=[pltpu.VMEM((tb + CONV_HALO, D_CONV), f32), pltpu.VMEM((tb + CONV_HALO, D_CONV), f32)],
        compiler_params=_cparams(1),
    )(a_in, a_in, z1, ds, dw_w, ln_g, ln_b)


def _mix_in_bwd(l, x, g1, dxres, dsegs, w_in_p):
    t, d = x.shape
    tb = _tb(t)

    def body(x_ref, g_ref, dr_ref, *rest):
        dseg_refs, w_ref, dx_ref, dg_ref = rest[:len(SEGS)], rest[len(SEGS)], rest[len(SEGS) + 1], rest[len(SEGS) + 2]

        @pl.when(pl.program_id(0) == 0)
        def _():
            dg_ref[...] = jnp.zeros_like(dg_ref)

        dh = jnp.zeros((tb, d), f32)
        for (off, wd), ds_ref in zip(SEGS, dseg_refs):
            dh = dh + _dot_nt(ds_ref[...], w_ref[:, off:off + wd])
        dxn, dg = _rms_bwd(x_ref[...], g_ref[l:l + 1, :], dh)
        dg_ref[...] += dg
        dx_ref[...] = dr_ref[...] + dxn

    return pl.pallas_call(
        body, name="mix_in_bwd", grid=(t // tb,),
        in_specs=[_rows(tb, d), _const(g1.shape), _rows(tb, d)] + [_rows(tb, wd) for _, wd in SEGS] + [_const((d, D_IN_PAD))],
        out_specs=(_rows(tb, d), _acc((1, d))),
        out_shape=(jax.ShapeDtypeStruct((t, d), f32), jax.ShapeDtypeStruct((1, d), f32)),
        compiler_params=_cparams(1),
    )(x, g1, dxres, *dsegs, w_in_p)


def _pick_block(n, cap=512):
    for b in (cap, 384, 256, 128):
        if b <= cap and n % b == 0:
            return b
    return n


def _wgrad(a, b, name, out_dtype=f32):
    t, kdim = a.shape
    n = b.shape[1]
    bk, bn = _pick_block(kdim), _pick_block(n)

    def body(a_ref, b_ref, o_ref):
        o_ref[...] = _dot_tn(a_ref[...], b_ref[...]).astype(out_dtype)

    return pl.pallas_call(
        body, name=name, grid=(kdim // bk, n // bn),
        in_specs=[pl.BlockSpec((t, bk), lambda i, j: (0, i)), pl.BlockSpec((t, bn), lambda i, j: (0, j))],
        out_specs=pl.BlockSpec((bk, bn), lambda i, j: (i, j)),
        out_shape=jax.ShapeDtypeStruct((kdim, n), out_dtype),
        compiler_params=_cparams(2),
    )(a, b)


_BC1 = 1.0 - ADAM_B1 ** ADAM_STEP
_BC2 = 1.0 - ADAM_B2 ** ADAM_STEP


def _adam_math(g, w, m, v):
    nm = ADAM_B1 * m + (1.0 - ADAM_B1) * g
    nv = ADAM_B2 * v + (1.0 - ADAM_B2) * (g * g)
    delta = -ADAM_LR * ((nm / _BC1) / (jnp.sqrt(nv / _BC2) + ADAM_EPS) + ADAM_WD * w)
    return delta, nm, nv


def _slot_sum(r_ref, index=()):
    g = r_ref[(0,) + index].astype(f32)
    for s in range(1, N_DEV):
        g = g + r_ref[(s,) + index].astype(f32)
    return g


def _adamw_shard(l, recv, w, m, v, prev, name, deps=()):
    _, k, ns = recv.shape
    rb = 256 if k % 256 == 0 else k
    blk = pl.BlockSpec((None, rb, ns), lambda i: (l, i, 0))

    def body(r_ref, w_ref, m_ref, v_ref, *rest):
        g_ref, d_ref, nm_ref, nv_ref = rest[-4:]
        g = _slot_sum(r_ref)
        g_ref[...] = g
        d_ref[...], nm_ref[...], nv_ref[...] = _adam_math(g, w_ref[...], m_ref[...], v_ref[...])

    out = jax.ShapeDtypeStruct(w.shape, f32)
    n_prev = 0 if prev is None else 4
    return pl.pallas_call(
        body, name=name, grid=(k // rb,),
        in_specs=[pl.BlockSpec((N_DEV, rb, ns), lambda i: (0, i, 0)), blk, blk, blk] + [_ANY] * (n_prev + len(deps)),
        out_specs=(blk, blk, blk, blk), out_shape=(out, out, out, out),
        input_output_aliases={4 + j: j for j in range(n_prev)},
        compiler_params=_cparams(1),
    )(recv, w, m, v, *(prev or ()), *deps)


def _adamw_rep(recvs, ws, ms, vs):
    depth = len(recvs)
    nt = len(REP_VECS)

    def body(*refs):
        r_refs = refs[:depth]
        w_refs, m_refs, v_refs = (refs[depth + i * nt:depth + (i + 1) * nt] for i in range(3))
        outs = refs[depth + 3 * nt:]
        for ti, (_, row, width, nrows) in enumerate(REP_VECS):
            for l in range(depth):
                g = r_refs[l][0, row:row + nrows, 0:width]
                for s in range(1, N_DEV):
                    g = g + r_refs[l][s, row:row + nrows, 0:width]
                pick = (lambda ref: ref[l]) if nrows > 1 else (lambda ref: ref[l:l + 1, :])
                delta, nm, nv = _adam_math(g, pick(w_refs[ti]), pick(m_refs[ti]), pick(v_refs[ti]))
                for o_ref, val in zip(outs[4 * ti:4 * ti + 4], (g, delta, nm, nv)):
                    if nrows > 1:
                        o_ref[l] = val
                    else:
                        o_ref[l:l + 1, :] = val

    ins = tuple(recvs) + tuple(ws) + tuple(ms) + tuple(vs)
    out_shape = tuple(jax.ShapeDtypeStruct(w.shape, f32) for w in ws for _ in range(4))
    return pl.pallas_call(
        body, name="adamw_rep", grid=(1,),
        in_specs=[_whole(a) for a in ins], out_specs=tuple(_whole(o) for o in out_shape), out_shape=out_shape,
        compiler_params=_cparams(1),
    )(*ins)


def _adamw_sg_w(recvs, w, m, v):
    depth = len(recvs)

    def body(*refs):
        r_refs = refs[:depth]
        w_ref, m_ref, v_ref = refs[depth:depth + 3]
        outs = refs[depth + 3:]
        for l in range(depth):
            for gi in range(SG_GROUPS):
                g = _slot_sum(r_refs[l], (gi,))
                delta, nm, nv = _adam_math(g, w_ref[l, gi], m_ref[l, gi], v_ref[l, gi])
                for o_ref, val in zip(outs, (g, delta, nm, nv)):
                    o_ref[l, gi] = val

    ins = tuple(recvs) + (w, m, v)
    out = jax.ShapeDtypeStruct(w.shape, f32)
    return pl.pallas_call(
        body, name="adamw_sg_w", grid=(1,),
        in_specs=[_whole(a) for a in ins], out_specs=tuple(_whole(out) for _ in range(4)), out_shape=(out,) * 4,
        compiler_params=_cparams(1),
    )(*ins)


def _rope_tables(positions):
    t = positions.shape[0]
    inv = 10000.0 ** (-jnp.arange(0, QK_ROPE, 2, dtype=f32) / QK_ROPE)
    ang = positions.astype(f32)[:, None] * inv
    cos, sin = jnp.cos(ang), jnp.sin(ang)
    tail = jnp.zeros((t, HEAD_PAD - KR_LO - QK_ROPE), f32)
    cos_t = jnp.concatenate([jnp.ones((t, KR_LO), f32), cos, cos, tail], axis=1)
    sin_t = jnp.concatenate([jnp.zeros((t, KR_LO), f32), -sin, sin, tail], axis=1)
    return cos_t, sin_t


def _bias_over_channels(sg_b_l):
    return jnp.broadcast_to(sg_b_l.T[:, :, None], (SG_CHUNK, SG_GROUPS, D_SG // SG_GROUPS)).reshape(SG_CHUNK, D_SG)


def _mixer_weights(gathered, rep, l):
    conv_out_w, sg_out_w, wuq_p, wk_p, wv_p, wo_p, conv_dw_w, ffn_dw_w = _asm_small(
        gathered["conv_out_w"], gathered["sg_out_w"], gathered["mla_w_uq"], gathered["mla_w_ukv"], gathered["mla_w_o"],
        gathered["conv_dw_w"], gathered["ffn_dw_w"])
    g_out = gathered["w_out"]
    w = dict(rep)
    w.update(
        l=l, w_in_p=_asm_w_in(gathered["w_in"]),
        conv_out_w=conv_out_w, sg_out_w=sg_out_w, wuq_p=wuq_p, wk_p=wk_p, wv_p=wv_p, wo_p=wo_p, conv_dw_w_p=conv_dw_w, ffn_dw_w_p=ffn_dw_w,
        w_out=g_out.reshape(g_out.shape[0] * g_out.shape[1], g_out.shape[2]),
        bexp=_bias_over_channels(rep["sg_b"][l]))
    return w


def _ffn_weights(gathered):
    g_down = gathered["ffn_w_down"]
    return dict(w_up=_asm_cols(gathered["ffn_w_up"], "asm_w_up"),
                w_down=g_down.reshape(g_down.shape[0] * g_down.shape[1], g_down.shape[2]))


def _mixer_fwd(x, w, cos_t, sin_t):
    l = w["l"]
    h, a_in, b_in, qlat, kvlat, krope, gates = _mix_in_fwd(l, x, w["mix_pre_g"], w["w_in_p"])
    s, z1 = _conv_fwd(l, a_in, w["conv_dw_w_p"], w["conv_dw_b"], w["conv_ln_g"], w["conv_ln_b"])
    um = _sgu_fwd(l, b_in, w["sg_ln_g"], w["sg_ln_b"], w["sg_w"], w["bexp"])
    q, k, v, qn, kvn = _mla_proj_fwd(l, qlat, kvlat, krope, cos_t, sin_t, w["mla_q_norm_g"], w["wuq_p"], w["mla_kv_norm_g"],
                                     w["wk_p"], w["wv_p"])
    o, lse = _attn_fwd(q, k, v)
    x1, merged, om = _merge_out_fwd(l, x, s, um, o, gates, w["conv_out_w"], w["sg_out_w"], w["wo_p"], w["w_out"], w["mix_post_g"])
    saved = dict(x=x, h=h, a_in=a_in, b_in=b_in, qlat=qlat, kvlat=kvlat, gates=gates, s=s, z1=z1, um=um, q=q, k=k, v=v, qn=qn,
                 kvn=kvn, o=o, lse=lse, x1=x1, merged=merged, om=om)
    return x1, saved


def _ffn_layer_fwd(x1, w):
    x2, h2, z, zc, act, f = _ffn_fwd(w["l"], x1, w["ffn_pre_g"], w["w_up"], w["ffn_dw_w_p"], w["ffn_dw_b"], w["w_down"], w["ffn_post_g"])
    return x2, dict(h2=h2, z=z, zc=zc, act=act, f=f)


def _ffn_layer_bwd(dx2, sv, w, deps=()):
    l = w["l"]
    vec = {}
    dx1, df, dz, vec["ffn_post_g"], vec["ffn_pre_g"], vec["ffn_dw_b"], vec["d_fdw"] = _ffn_bwd(
        l, dx2, sv["x1"], sv["f"], sv["z"], sv["zc"], w["w_up"], w["ffn_dw_w_p"], w["w_down"], w["ffn_pre_g"], w["ffn_post_g"],
        deps=deps)
    d_down = _wgrad(sv["act"], df, "wgrad_ffn_down", bf16)
    d_up = _wgrad(sv["h2"], dz, "wgrad_ffn_up")
    send = dict(ffn_w_up=_dis_cols(d_up, "dis_w_up"),
                ffn_w_down=d_down.reshape(N_DEV, d_down.shape[0] // N_DEV, d_down.shape[1]))
    return dx1, send, vec


def _mixer_bwd(dx1, sv, w, cos_t, sin_t, shard_cols, vec, deps=()):
    l = w["l"]
    d_fdw = vec.pop("d_fdw")
    dom, dgates, dya, dyb, dyc, ds, dum, do, vec["mix_post_g"] = _merge_out_bwd(
        l, dx1, sv["om"], sv["s"], sv["um"], sv["o"], sv["gates"], w["conv_out_w"], w["sg_out_w"], w["wo_p"], w["w_out"], w["mix_post_g"],
        deps=deps)
    d_out = _wgrad(sv["merged"], dom, "wgrad_w_out", bf16)
    d_co = _wgrad(sv["s"], dya, "wgrad_conv_out")
    d_so = _wgrad(sv["um"], dyb, "wgrad_sg_out")
    d_wo = _wgrad(sv["o"], dyc, "wgrad_w_o")

    dq, dk, dv = _attn_bwd(sv["q"], sv["k"], sv["v"], sv["o"], sv["lse"], do)
    dqb, dkb, dvb, dqlat, dkvlat, dkrope, vec["mla_q_norm_g"], vec["mla_kv_norm_g"] = _mla_proj_bwd(
        l, dq, dk, dv, sv["qlat"], sv["kvlat"], cos_t, sin_t, w["mla_q_norm_g"], w["wuq_p"], w["mla_kv_norm_g"], w["wk_p"], w["wv_p"])
    d_uq = _wgrad(sv["qn"], dqb, "wgrad_w_uq")
    d_uk = _wgrad(sv["kvn"], dkb, "wgrad_w_uk")
    d_uv = _wgrad(sv["kvn"], dvb, "wgrad_w_uv")

    db_in, dsg_w, dsgb, vec["sg_ln_g"], vec["sg_ln_b"] = _sgu_bwd(l, sv["b_in"], dum, w["sg_ln_g"], w["sg_ln_b"], w["sg_w"], w["bexp"])
    da_in, d_cdw, vec["conv_dw_b"], vec["conv_ln_g"], vec["conv_ln_b"] = _conv_bwd(
        l, sv["a_in"], sv["z1"], ds, w["conv_dw_w_p"], w["conv_ln_g"], w["conv_ln_b"])

    dsegs = (da_in, db_in, dqlat, dkvlat, dkrope, dgates)
    dx, vec["mix_pre_g"] = _mix_in_bwd(l, sv["x"], w["mix_pre_g"], dx1, dsegs, w["w_in_p"])
    names = ("a", "b", "q", "kv", "kr", "g")
    d_in_segs = [_wgrad(sv["h"], dseg, "wgrad_w_in_" + nm) for nm, dseg in zip(names, dsegs)]

    send = {}
    send["w_in"] = _dis_w_in(d_in_segs, shard_cols["w_in"])
    (send["conv_out_w"], send["sg_out_w"], send["mla_w_uq"], send["mla_w_ukv"], send["mla_w_o"], send["conv_dw_w"],
     send["ffn_dw_w"]) = _dis_small(d_co, d_so, d_uq, d_uk, d_uv, d_wo, d_cdw, d_fdw)
    send["w_out"] = d_out.reshape(N_DEV, d_out.shape[0] // N_DEV, d_out.shape[1])
    rep_pack = _pack_rep([vec[n] for n, _, _, _ in REP_VECS[:-1]], dsgb)
    return dx, send, rep_pack, dsg_w


def kernel(x, positions, mix_pre_g, mix_post_g, ffn_pre_g, ffn_post_g, w_in, conv_dw_w, conv_dw_b, conv_ln_g, conv_ln_b, conv_out_w, sg_ln_g, sg_ln_b, sg_w, sg_b, sg_out_w, mla_q_norm_g, mla_w_uq, mla_kv_norm_g, mla_w_ukv, mla_w_o, w_out, ffn_w_up, ffn_dw_w, ffn_dw_b, ffn_w_down, loss_target, m_mix_pre_g, m_mix_post_g, m_ffn_pre_g, m_ffn_post_g, m_w_in, m_conv_dw_w, m_conv_dw_b, m_conv_ln_g, m_conv_ln_b, m_conv_out_w, m_sg_ln_g, m_sg_ln_b, m_sg_w, m_sg_b, m_sg_out_w, m_mla_q_norm_g, m_mla_w_uq, m_mla_kv_norm_g, m_mla_w_ukv, m_mla_w_o, m_w_out, m_ffn_w_up, m_ffn_dw_w, m_ffn_dw_b, m_ffn_w_down, v_mix_pre_g, v_mix_post_g, v_ffn_pre_g, v_ffn_post_g, v_w_in, v_conv_dw_w, v_conv_dw_b, v_conv_ln_g, v_conv_ln_b, v_conv_out_w, v_sg_ln_g, v_sg_ln_b, v_sg_w, v_sg_b, v_sg_out_w, v_mla_q_norm_g, v_mla_w_uq, v_mla_kv_norm_g, v_mla_w_ukv, v_mla_w_o, v_w_out, v_ffn_w_up, v_ffn_dw_w, v_ffn_dw_b, v_ffn_w_down):
    args = (x, positions, mix_pre_g, mix_post_g, ffn_pre_g, ffn_post_g, w_in, conv_dw_w, conv_dw_b, conv_ln_g, conv_ln_b, conv_out_w, sg_ln_g, sg_ln_b, sg_w, sg_b, sg_out_w, mla_q_norm_g, mla_w_uq, mla_kv_norm_g, mla_w_ukv, mla_w_o, w_out, ffn_w_up, ffn_dw_w, ffn_dw_b, ffn_w_down, loss_target, m_mix_pre_g, m_mix_post_g, m_ffn_pre_g, m_ffn_post_g, m_w_in, m_conv_dw_w, m_conv_dw_b, m_conv_ln_g, m_conv_ln_b, m_conv_out_w, m_sg_ln_g, m_sg_ln_b, m_sg_w, m_sg_b, m_sg_out_w, m_mla_q_norm_g, m_mla_w_uq, m_mla_kv_norm_g, m_mla_w_ukv, m_mla_w_o, m_w_out, m_ffn_w_up, m_ffn_dw_w, m_ffn_dw_b, m_ffn_w_down, v_mix_pre_g, v_mix_post_g, v_ffn_pre_g, v_ffn_post_g, v_w_in, v_conv_dw_w, v_conv_dw_b, v_conv_ln_g, v_conv_ln_b, v_conv_out_w, v_sg_ln_g, v_sg_ln_b, v_sg_w, v_sg_b, v_sg_out_w, v_mla_q_norm_g, v_mla_w_uq, v_mla_kv_norm_g, v_mla_w_ukv, v_mla_w_o, v_w_out, v_ffn_w_up, v_ffn_dw_w, v_ffn_dw_b, v_ffn_w_down)
    n_in = len(IN_NAMES)
    a = dict(zip(IN_NAMES, args[:n_in]))
    target = args[n_in]
    n_w = len(WEIGHTS)
    m_in = dict(zip(WEIGHTS, args[n_in + 1:n_in + 1 + n_w]))
    v_in = dict(zip(WEIGHTS, args[n_in + 1 + n_w:n_in + 1 + 2 * n_w]))
    depth = a["mix_pre_g"].shape[0]
    rep = {n: a[n] for n in WEIGHTS if n not in SHARDED}
    shard_cols = {n: a[n].shape[2] for n in SHARDED}

    starts, token = {}, ()
    for l in range(depth):
        for group, tag in ((MIX_GROUP, "mix"), (FFN_BIG, "ffn")):
            wire = [a[n][l] if n in WIRE_F32 else a[n][l].astype(bf16) for n in group]
            starts[l, tag] = _gather_start(wire, "gather_%s_weights_%d" % (tag, l), deps=token)
            token = (starts[l, tag]["token"],)

    cos_t, sin_t = _rope_tables(a["positions"][0])
    xl = a["x"][0]
    ws, saved = [], []
    passing = {(0, "mix"): _gather_forward(starts[0, "mix"], after=token)}
    after = (passing[0, "mix"]["token"],)
    for l in range(depth):
        w = _mixer_weights(dict(zip(MIX_GROUP, _gather_wait(passing[l, "mix"], after=after))), rep, l)
        x1, sv = _mixer_fwd(xl, w, cos_t, sin_t)
        passing[l, "ffn"] = _gather_forward(starts[l, "ffn"], after=(x1,))
        after = (passing[l, "ffn"]["token"],)
        if l + 1 < depth:
            passing[l + 1, "mix"] = _gather_forward(starts[l + 1, "mix"], after=after)
            after = (passing[l + 1, "mix"]["token"],)
        w.update(_ffn_weights(dict(zip(FFN_BIG, _gather_wait(passing[l, "ffn"], after=after)))))
        xl, sv_ffn = _ffn_layer_fwd(x1, w)
        sv.update(sv_ffn)
        ws.append(w)
        saved.append(sv)
        after = (xl,)
    dx, loss_part = _loss_fwd_bwd(xl, target[0])
    loss = lax.psum(loss_part, AXES)

    outs = {}
    rep_recvs, sgw_recvs = [None] * depth, [None] * depth

    def finish(l, h_ffn, h_mix, after):
        for names, handle in ((FFN_BIG, h_ffn), (MIX_GROUP, h_mix)):
            got = _exchange_wait(handle, after=after)
            for n, recv in zip(names, got):
                outs[n] = _adamw_shard(l, recv, a[n], m_in[n], v_in[n], outs.get(n), "adamw_" + n)
            after = tuple(outs[n][0] for n in names)
        rep_recvs[l], sgw_recvs[l] = got[len(MIX_GROUP)], got[len(MIX_GROUP) + 1]
        return after

    token, prev, after = (), None, ()
    for l in reversed(range(depth)):
        dx1, send_ffn, vec = _ffn_layer_bwd(dx, saved[l], ws[l], deps=token)
        h_ffn = _exchange_start([(send_ffn[n], "scatter") for n in FFN_BIG], "exchange_ffn_grads_%d" % l)
        dx, send_mix, rep_pack, dsg_w = _mixer_bwd(dx1, saved[l], ws[l], cos_t, sin_t, shard_cols, vec, deps=(h_ffn["token"],))
        h_mix = _exchange_start([(send_mix[n], "scatter") for n in MIX_GROUP] + [(rep_pack, "gather"), (dsg_w, "gather")],
                                "exchange_mix_grads_%d" % l)
        token = (h_mix["token"],)
        if prev is not None:
            after = finish(*prev, after=(dx,) + token)
        prev = (l, h_ffn, h_mix)
    finish(*prev, after=after)
    vec_names = [n for n, _, _, _ in REP_VECS]
    rep_outs = _adamw_rep(rep_recvs, [a[n] for n in vec_names], [m_in[n] for n in vec_names], [v_in[n] for n in vec_names])
    for i, n in enumerate(vec_names):
        outs[n] = rep_outs[4 * i:4 * i + 4]
    outs["sg_w"] = _adamw_sg_w(sgw_recvs, a["sg_w"], m_in["sg_w"], v_in["sg_w"])

    grad_w, delta_w, new_m, new_v = ([outs[n][j] for n in WEIGHTS] for j in range(4))
    return (loss, dx[None], *grad_w, *delta_w, *new_m, *new_v)
```

```python
import math

import jax
import jax.numpy as jnp
from jax import lax
from jax.experimental import pallas as pl
from jax.experimental.pallas import tpu as pltpu

f32 = jnp.float32
bf16 = jnp.bfloat16

N_DEV = 8
AXES = ("x", "y", "c")
EPS = 1e-6
D_CONV = 512
CONV_K = 31
CONV_HALO = 32
D_SG = 512
SG_GROUPS = 4
SG_CHUNK = 128
N_HEADS = 8
QK_NOPE = 64
QK_ROPE = 32
V_HEAD = 64
HEAD_PAD = 128
Q_LORA = 384
KV_LORA = 256
D_FF = 2816
FFN_K = 3
FFN_HALO = 8
ATT_SCALE = (QK_NOPE + QK_ROPE) ** -0.5
NEG = float(jnp.finfo(jnp.float32).min)

ADAM_LR = 0.001
ADAM_B1 = 0.9
ADAM_B2 = 0.999
ADAM_EPS = 1e-08
ADAM_WD = 0.01
ADAM_STEP = 10

LANES = 128
VMEM_MB = 56

REF_CUTS = (0, 1024, 2048, 2432, 2688, 2720, 5792)
SEGS = ((0, 1024), (1024, 1024), (2048, 384), (2432, 256), (2688, 128), (2816, 3072))
D_IN = 5792
D_IN_PAD = 5888
KR_LO = 64
SEG_INNER = (0, 0, 0, 0, KR_LO, 0)

IN_NAMES = ['x', 'positions', 'mix_pre_g', 'mix_post_g', 'ffn_pre_g', 'ffn_post_g', 'w_in', 'conv_dw_w', 'conv_dw_b', 'conv_ln_g', 'conv_ln_b', 'conv_out_w', 'sg_ln_g', 'sg_ln_b', 'sg_w', 'sg_b', 'sg_out_w', 'mla_q_norm_g', 'mla_w_uq', 'mla_kv_norm_g', 'mla_w_ukv', 'mla_w_o', 'w_out', 'ffn_w_up', 'ffn_dw_w', 'ffn_dw_b', 'ffn_w_down']
WEIGHTS = IN_NAMES[2:]
SHARDED = ("w_in", "conv_dw_w", "conv_out_w", "sg_out_w", "mla_w_uq", "mla_w_ukv", "mla_w_o", "w_out", "ffn_w_up", "ffn_dw_w",
           "ffn_w_down")
FFN_BIG = ("ffn_w_up", "ffn_w_down")
MIX_GROUP = tuple(n for n in SHARDED if n not in FFN_BIG)
MIX_SMALL = tuple(n for n in MIX_GROUP if n != "w_in")
TRANSPOSED = ("ffn_w_up",)
WIRE_F32 = ("conv_dw_w", "ffn_dw_w")
REP_VECS = (("mix_pre_g", 0, 1024, 1), ("mix_post_g", 1, 1024, 1), ("ffn_pre_g", 2, 1024, 1), ("ffn_post_g", 3, 1024, 1),
            ("conv_dw_b", 4, 512, 1), ("conv_ln_g", 5, 512, 1), ("conv_ln_b", 6, 512, 1), ("sg_ln_g", 7, 512, 1),
            ("sg_ln_b", 8, 512, 1), ("mla_q_norm_g", 9, 384, 1), ("mla_kv_norm_g", 10, 256, 1), ("ffn_dw_b", 11, 5632, 1),
            ("sg_b", 12, 128, 4))
REP_ROWS = 16
REP_W = 5632


def _cparams(n_axes):
    return pltpu.CompilerParams(dimension_semantics=("arbitrary",) * n_axes, vmem_limit_bytes=VMEM_MB * 2 ** 20)


def _rows(tb, n):
    return pl.BlockSpec((tb, n), lambda i: (i, 0))


def _rows_rev(tb, n, nb):
    return pl.BlockSpec((tb, n), lambda i: (nb - 1 - i, 0))


def _const(shape):
    nd = len(shape)
    return pl.BlockSpec(shape, lambda *_: (0,) * nd, pipeline_mode=pl.Buffered(1))


def _acc(shape):
    nd = len(shape)
    return pl.BlockSpec(shape, lambda *_: (0,) * nd)


def _whole(arr):
    return pl.BlockSpec(arr.shape, lambda *_: (0,) * arr.ndim)


def _dot(a, b):
    return jnp.dot(a, b, preferred_element_type=f32)


def _dot_nt(a, b):
    return lax.dot_general(a, b, (((1,), (1,)), ((), ())), preferred_element_type=f32)


def _dot_tn(a, b):
    return lax.dot_general(a, b, (((0,), (0,)), ((), ())), preferred_element_type=f32)


def _mean(x):
    return jnp.mean(x, axis=-1, keepdims=True)


def _colsum(x):
    return jnp.sum(x, axis=0, keepdims=True)


def _rms_fwd(x, g):
    return x * lax.rsqrt(_mean(x * x) + EPS) * g


def _rms_bwd(x, g, dy):
    r = lax.rsqrt(_mean(x * x) + EPS)
    n = x * r
    dn = dy * g
    return r * (dn - n * _mean(dn * n)), _colsum(dy * n)


def _ln_stats(x):
    mu = _mean(x)
    d = x - mu
    rstd = lax.rsqrt(_mean(d * d) + EPS)
    return d * rstd, rstd


def _ln_bwd(xhat, rstd, g, dy):
    dxh = dy * g
    return rstd * (dxh - _mean(dxh) - xhat * _mean(dxh * xhat))


_GELU_C0 = math.sqrt(2.0 / math.pi)
_GELU_C1 = 0.044715


def _gelu(x):
    t = jnp.tanh(_GELU_C0 * (x + _GELU_C1 * (x * x * x)))
    return 0.5 * x * (1.0 + t)


def _gelu_and_grad(x):
    x2 = x * x
    t = jnp.tanh(_GELU_C0 * (x + _GELU_C1 * (x2 * x)))
    g = 0.5 * x * (1.0 + t)
    dg = 0.5 * (1.0 + t) + 0.5 * x * (1.0 - t * t) * (_GELU_C0 * (1.0 + 3.0 * _GELU_C1 * x2))
    return g, dg


def _swap_rope_halves(x):
    n = x.shape[1]
    half = QK_ROPE // 2
    lane = lax.broadcasted_iota(jnp.int32, x.shape, 1) % HEAD_PAD
    first = (lane >= KR_LO) & (lane < KR_LO + half)
    second = (lane >= KR_LO + half) & (lane < KR_LO + QK_ROPE)
    return jnp.where(first, pltpu.roll(x, n - half, 1), jnp.where(second, pltpu.roll(x, half, 1), 0.0))


def _tb(t):
    return min(256, t)


_HBM = pl.BlockSpec(memory_space=pltpu.HBM)
_SEM = pl.BlockSpec(memory_space=pltpu.SEMAPHORE)
_ANY = pl.BlockSpec(memory_space=pl.ANY)
_EFFECT = pltpu.SideEffectType.DATAFLOW_SIDE_EFFECTING


def _exchange_copies(modes, ins, lands, send_sems, recv_sems, loc_sems):
    x, y, c = lax.axis_index("x"), lax.axis_index("y"), lax.axis_index("c")
    me = 4 * x + 2 * y + c
    copies = []
    for a, mode in enumerate(modes):
        def src(dst_index, a=a, mode=mode):
            return ins[a].at[dst_index] if mode == "scatter" else ins[a]
        copies.append(pltpu.make_async_copy(src(me), lands[a].at[me], loc_sems.at[a]))
        for k in range(1, N_DEV):
            px = 1 - x if (k >> 2) & 1 else x
            py = 1 - y if (k >> 1) & 1 else y
            pc = 1 - c if k & 1 else c
            copies.append(pltpu.make_async_remote_copy(
                src_ref=src(4 * px + 2 * py + pc), dst_ref=lands[a].at[me],
                send_sem=send_sems.at[a * (N_DEV - 1) + k - 1], recv_sem=recv_sems.at[a * (N_DEV - 1) + k - 1],
                device_id=(px, py, pc), device_id_type=pl.DeviceIdType.MESH))
    return copies


def _exchange_start(ops, name, deps=()):
    n = len(ops)
    arrs = [arr for arr, _ in ops]
    modes = [mode for _, mode in ops]
    lands = [lax.empty((N_DEV,) + arr.shape if mode == "gather" else arr.shape, arr.dtype) for arr, mode in ops]

    def body(*refs):
        ins, land_refs = refs[:n], refs[n:2 * n]
        send_sems, recv_sems, loc_sems = refs[2 * n + len(deps):2 * n + len(deps) + 3]
        for cp in _exchange_copies(modes, ins, land_refs, send_sems, recv_sems, loc_sems):
            cp.start()
        refs[-1][...] = jnp.zeros((8, LANES), f32)

    n_rem = n * (N_DEV - 1)
    res = pl.pallas_call(
        body, name=name,
        out_shape=(pltpu.SemaphoreType.DMA((n_rem,)), pltpu.SemaphoreType.DMA((n_rem,)), pltpu.SemaphoreType.DMA((n,)),
                   *[pltpu.HBM(x.shape, x.dtype) for x in arrs + lands], jax.ShapeDtypeStruct((8, LANES), f32)),
        in_specs=[_HBM] * (2 * n) + [_ANY] * len(deps),
        out_specs=(_SEM, _SEM, _SEM, *[_HBM] * (2 * n), pl.BlockSpec(memory_space=pltpu.VMEM)),
        input_output_aliases={i: 3 + i for i in range(2 * n)},
        compiler_params=pltpu.CompilerParams(has_side_effects=_EFFECT),
    )(*[pltpu.with_memory_space_constraint(x, pltpu.HBM) for x in arrs + lands], *deps)
    return dict(modes=modes, sems=res[:3], thru=res[3:3 + 2 * n], token=res[-1], name=name)


def _exchange_wait(handle, after=()):
    modes, thru = handle["modes"], handle["thru"]
    n = len(modes)

    def body(*refs):
        ins, land_refs = refs[:n], refs[n:2 * n]
        send_sems, recv_sems, loc_sems = refs[2 * n:2 * n + 3]
        for cp in _exchange_copies(modes, ins, land_refs, send_sems, recv_sems, loc_sems):
            cp.wait()

    res = pl.pallas_call(
        body, name=handle["name"] + "_wait",
        out_shape=tuple(pltpu.HBM(x.shape, x.dtype) for x in thru),
        in_specs=[_HBM] * (2 * n) + [_SEM] * 3 + [_ANY] * len(after),
        out_specs=tuple([_HBM] * (2 * n)),
        input_output_aliases={i: i for i in range(2 * n)},
        compiler_params=pltpu.CompilerParams(has_side_effects=_EFFECT),
    )(*thru, *handle["sems"], *after)
    return res[n:]


_SAME_CORE_PEERS = ((1, 0), (0, 1), (1, 1))


def _gather_copies_one(ins, lands, send_sems, recv_sems, loc_sems):
    x, y, c = lax.axis_index("x"), lax.axis_index("y"), lax.axis_index("c")
    me = 4 * x + 2 * y + c
    targets = [(x, y, 1 - c)] + [(1 - x if fx else x, 1 - y if fy else y, c) for fx, fy in _SAME_CORE_PEERS]
    copies = []
    for a in range(len(ins)):
        copies.append(pltpu.make_async_copy(ins[a], lands[a].at[me], loc_sems.at[a]))
        for j, target in enumerate(targets):
            copies.append(pltpu.make_async_remote_copy(
                src_ref=ins[a], dst_ref=lands[a].at[me], send_sem=send_sems.at[4 * a + j], recv_sem=recv_sems.at[4 * a + j],
                device_id=target, device_id_type=pl.DeviceIdType.MESH))
    return copies


def _gather_copies_two(lands, send_sems, recv_sems):
    x, y, c = lax.axis_index("x"), lax.axis_index("y"), lax.axis_index("c")
    copies = []
    for a in range(len(lands)):
        for j, (fx, fy) in enumerate(_SAME_CORE_PEERS):
            slot = 4 * (1 - x if fx else x) + 2 * (1 - y if fy else y) + c
            copies.append(pltpu.make_async_remote_copy(
                src_ref=lands[a].at[slot], dst_ref=lands[a].at[slot], send_sem=send_sems.at[3 * a + j],
                recv_sem=recv_sems.at[3 * a + j], device_id=(x, y, 1 - c), device_id_type=pl.DeviceIdType.MESH))
    return copies


def _gather_start(arrs, name, deps=()):
    n = len(arrs)
    lands = [lax.empty((N_DEV,) + arr.shape, arr.dtype) for arr in arrs]

    def body(*refs):
        ins, land_refs = refs[:n], refs[n:2 * n]
        send_sems, recv_sems, loc_sems = refs[2 * n + len(deps):2 * n + len(deps) + 3]
        for cp in _gather_copies_one(ins, land_refs, send_sems, recv_sems, loc_sems):
            cp.start()
        refs[-1][...] = jnp.zeros((8, LANES), f32)

    res = pl.pallas_call(
        body, name=name,
        out_shape=(pltpu.SemaphoreType.DMA((4 * n,)), pltpu.SemaphoreType.DMA((4 * n,)), pltpu.SemaphoreType.DMA((n,)),
                   *[pltpu.HBM(x.shape, x.dtype) for x in arrs + lands], jax.ShapeDtypeStruct((8, LANES), f32)),
        in_specs=[_HBM] * (2 * n) + [_ANY] * len(deps),
        out_specs=(_SEM, _SEM, _SEM, *[_HBM] * (2 * n), pl.BlockSpec(memory_space=pltpu.VMEM)),
        input_output_aliases={i: 3 + i for i in range(2 * n)},
        compiler_params=pltpu.CompilerParams(has_side_effects=_EFFECT),
    )(*[pltpu.with_memory_space_constraint(x, pltpu.HBM) for x in arrs + lands], *deps)
    return dict(n=n, sems=res[:3], thru=res[3:3 + 2 * n], token=res[-1], name=name)


def _gather_forward(handle, after=()):
    n, thru = handle["n"], handle["thru"]

    def body(*refs):
        ins, land_refs = refs[:n], refs[n:2 * n]
        send_one, recv_one, loc_sems = refs[2 * n:2 * n + 3]
        send_two, recv_two = refs[2 * n + 3 + len(after):2 * n + 5 + len(after)]
        for cp in _gather_copies_one(ins, land_refs, send_one, recv_one, loc_sems):
            cp.wait()
        for cp in _gather_copies_two(land_refs, send_two, recv_two):
            cp.start()
        refs[-1][...] = jnp.zeros((8, LANES), f32)

    res = pl.pallas_call(
        body, name=handle["name"] + "_forward",
        out_shape=(pltpu.SemaphoreType.DMA((3 * n,)), pltpu.SemaphoreType.DMA((3 * n,)),
                   *[pltpu.HBM(x.shape, x.dtype) for x in thru], jax.ShapeDtypeStruct((8, LANES), f32)),
        in_specs=[_HBM] * (2 * n) + [_SEM] * 3 + [_ANY] * len(after),
        out_specs=(_SEM, _SEM, *[_HBM] * (2 * n), pl.BlockSpec(memory_space=pltpu.VMEM)),
        input_output_aliases={i: 2 + i for i in range(2 * n)},
        compiler_params=pltpu.CompilerParams(has_side_effects=_EFFECT),
    )(*thru, *handle["sems"], *after)
    return dict(n=n, sems=res[:2], lands=res[2 + n:2 + 2 * n], token=res[-1], name=handle["name"])


def _gather_wait(handle, after=()):
    n, lands = handle["n"], handle["lands"]

    def body(*refs):
        land_refs = refs[:n]
        send_two, recv_two = refs[n:n + 2]
        for cp in _gather_copies_two(land_refs, send_two, recv_two):
            cp.wait()

    return pl.pallas_call(
        body, name=handle["name"] + "_wait",
        out_shape=tuple(pltpu.HBM(x.shape, x.dtype) for x in lands),
        in_specs=[_HBM] * n + [_SEM] * 2 + [_ANY] * len(after),
        out_specs=tuple([_HBM] * n),
        input_output_aliases={i: i for i in range(n)},
        compiler_params=pltpu.CompilerParams(has_side_effects=_EFFECT),
    )(*lands, *handle["sems"], *after)


def _w_in_pieces(ns):
    out = []
    for e in range(N_DEV):
        lo, hi = ns * e, ns * (e + 1)
        for s in range(len(SEGS)):
            a, b = max(lo, REF_CUTS[s]), min(hi, REF_CUTS[s + 1])
            if a < b:
                inner = SEG_INNER[s] + a - REF_CUTS[s]
                out.append((e, a - lo, b - lo, s, inner, inner + b - a))
    return out


def _asm_w_in(g):
    _, d, ns = g.shape
    rb = 256
    pieces = _w_in_pieces(ns)

    def body(g_ref, o_ref):
        kr = SEGS[4][0]
        o_ref[:, kr:kr + KR_LO] = jnp.zeros((rb, KR_LO), g.dtype)
        o_ref[:, kr + KR_LO + QK_ROPE:kr + HEAD_PAD] = jnp.zeros((rb, HEAD_PAD - KR_LO - QK_ROPE), g.dtype)
        for e, s0, s1, seg, d0, d1 in pieces:
            off = SEGS[seg][0]
            o_ref[:, off + d0:off + d1] = g_ref[e, :, s0:s1]

    return pl.pallas_call(
        body, name="asm_w_in", grid=(d // rb,),
        in_specs=[pl.BlockSpec((N_DEV, rb, ns), lambda i: (0, i, 0))],
        out_specs=_rows(rb, D_IN_PAD), out_shape=jax.ShapeDtypeStruct((d, D_IN_PAD), g.dtype),
        compiler_params=_cparams(1),
    )(g)


def _dis_w_in(dsegs, ns):
    d = dsegs[0].shape[0]
    rb = 256
    pieces = _w_in_pieces(ns)

    def body(*refs):
        seg_refs, o_ref = refs[:len(SEGS)], refs[len(SEGS)]
        for e, s0, s1, seg, d0, d1 in pieces:
            o_ref[e, :, s0:s1] = seg_refs[seg][:, d0:d1].astype(bf16)

    return pl.pallas_call(
        body, name="dis_w_in", grid=(d // rb,),
        in_specs=[_rows(rb, wd) for _, wd in SEGS],
        out_specs=pl.BlockSpec((N_DEV, rb, ns), lambda i: (0, i, 0)),
        out_shape=jax.ShapeDtypeStruct((N_DEV, d, ns), bf16),
        compiler_params=_cparams(1),
    )(*dsegs)


def _asm_small(g_conv_out, g_sg_out, g_uq, g_ukv, g_wo, g_cdw, g_fdw):
    d = g_conv_out.shape[2] * N_DEV
    hw = N_HEADS * HEAD_PAD
    hq = QK_NOPE + QK_ROPE
    ff2 = g_fdw.shape[2] * N_DEV
    cw, fw = g_cdw.shape[2], g_fdw.shape[2]

    def body(co_ref, so_ref, uq_ref, ukv_ref, wo_ref, cdw_ref, fdw_ref, o_co, o_so, o_uq, o_k, o_v, o_wo, o_cdw, o_fdw):
        o_cdw[CONV_K:CONV_HALO, :] = jnp.zeros((CONV_HALO - CONV_K, D_CONV), f32)
        o_fdw[FFN_K:FFN_HALO, :] = jnp.zeros((FFN_HALO - FFN_K, ff2), f32)
        for e in range(N_DEV):
            cs = e * HEAD_PAD
            o_co[:, cs:cs + HEAD_PAD] = co_ref[e]
            o_so[:, cs:cs + HEAD_PAD] = so_ref[e]
            o_uq[:, cs:cs + hq] = uq_ref[e]
            o_uq[:, cs + hq:cs + HEAD_PAD] = jnp.zeros((Q_LORA, HEAD_PAD - hq), bf16)
            o_k[:, cs:cs + QK_NOPE] = ukv_ref[e, :, 0:QK_NOPE]
            o_k[:, cs + QK_NOPE:cs + HEAD_PAD] = jnp.zeros((KV_LORA, HEAD_PAD - QK_NOPE), bf16)
            o_v[:, cs:cs + V_HEAD] = ukv_ref[e, :, QK_NOPE:QK_NOPE + V_HEAD]
            o_v[:, cs + V_HEAD:cs + HEAD_PAD] = jnp.zeros((KV_LORA, HEAD_PAD - V_HEAD), bf16)
            for h in range(N_HEADS):
                o_wo[h * HEAD_PAD:h * HEAD_PAD + V_HEAD, cs:cs + HEAD_PAD] = wo_ref[e, h * V_HEAD:(h + 1) * V_HEAD, :]
                o_wo[h * HEAD_PAD + V_HEAD:(h + 1) * HEAD_PAD, cs:cs + HEAD_PAD] = jnp.zeros((HEAD_PAD - V_HEAD, HEAD_PAD), bf16)
            o_cdw[0:CONV_K, e * cw:(e + 1) * cw] = cdw_ref[e]
            o_fdw[0:FFN_K, e * fw:(e + 1) * fw] = fdw_ref[e]

    ins = (g_conv_out, g_sg_out, g_uq, g_ukv, g_wo, g_cdw, g_fdw)
    out_shape = (jax.ShapeDtypeStruct((D_CONV, d), bf16), jax.ShapeDtypeStruct((D_SG, d), bf16), jax.ShapeDtypeStruct((Q_LORA, hw), bf16),
                 jax.ShapeDtypeStruct((KV_LORA, hw), bf16), jax.ShapeDtypeStruct((KV_LORA, hw), bf16), jax.ShapeDtypeStruct((hw, d), bf16),
                 jax.ShapeDtypeStruct((CONV_HALO, D_CONV), f32), jax.ShapeDtypeStruct((FFN_HALO, ff2), f32))
    return pl.pallas_call(
        body, name="asm_small", grid=(1,),
        in_specs=[_whole(a) for a in ins], out_specs=tuple(_whole(o) for o in out_shape), out_shape=out_shape,
        compiler_params=_cparams(1),
    )(*ins)


def _dis_small(d_co, d_so, d_uq, d_k, d_v, d_wo, d_cdw, d_fdw):
    d = d_co.shape[1]
    hq = QK_NOPE + QK_ROPE
    cw, fw = D_CONV // N_DEV, d_fdw.shape[1] // N_DEV

    def body(co_ref, so_ref, uq_ref, k_ref, v_ref, wo_ref, cdw_ref, fdw_ref, o_co, o_so, o_uq, o_ukv, o_wo, o_cdw, o_fdw):
        for e in range(N_DEV):
            cs = e * HEAD_PAD
            o_co[e] = co_ref[:, cs:cs + HEAD_PAD].astype(bf16)
            o_so[e] = so_ref[:, cs:cs + HEAD_PAD].astype(bf16)
            o_uq[e] = uq_ref[:, cs:cs + hq].astype(bf16)
            o_ukv[e, :, 0:QK_NOPE] = k_ref[:, cs:cs + QK_NOPE].astype(bf16)
            o_ukv[e, :, QK_NOPE:QK_NOPE + V_HEAD] = v_ref[:, cs:cs + V_HEAD].astype(bf16)
            for h in range(N_HEADS):
                o_wo[e, h * V_HEAD:(h + 1) * V_HEAD, :] = wo_ref[h * HEAD_PAD:h * HEAD_PAD + V_HEAD, cs:cs + HEAD_PAD].astype(bf16)
            o_cdw[e] = cdw_ref[0:CONV_K, e * cw:(e + 1) * cw]
            o_fdw[e] = fdw_ref[0:FFN_K, e * fw:(e + 1) * fw]

    ins = (d_co, d_so, d_uq, d_k, d_v, d_wo, d_cdw, d_fdw)
    out_shape = (jax.ShapeDtypeStruct((N_DEV, D_CONV, d // N_DEV), bf16), jax.ShapeDtypeStruct((N_DEV, D_SG, d // N_DEV), bf16),
                 jax.ShapeDtypeStruct((N_DEV, Q_LORA, hq), bf16), jax.ShapeDtypeStruct((N_DEV, KV_LORA, QK_NOPE + V_HEAD), bf16),
                 jax.ShapeDtypeStruct((N_DEV, N_HEADS * V_HEAD, d // N_DEV), bf16), jax.ShapeDtypeStruct((N_DEV, CONV_K, cw), f32),
                 jax.ShapeDtypeStruct((N_DEV, FFN_K, fw), f32))
    return pl.pallas_call(
        body, name="dis_small", grid=(1,),
        in_specs=[_whole(a) for a in ins], out_specs=tuple(_whole(o) for o in out_shape), out_shape=out_shape,
        compiler_params=_cparams(1),
    )(*ins)


def _pack_rep(vec_grads, dsgb):
    def body(*refs):
        o_ref = refs[-1]
        o_ref[...] = jnp.zeros((REP_ROWS, REP_W), f32)
        for (_, row, width, nrows), ref in zip(REP_VECS, refs[:-1]):
            o_ref[row:row + nrows, 0:width] = ref[0:nrows, :]

    ins = tuple(vec_grads) + (dsgb,)
    return pl.pallas_call(
        body, name="pack_rep", grid=(1,),
        in_specs=[_whole(a) for a in ins], out_specs=pl.BlockSpec((REP_ROWS, REP_W), lambda i: (0, 0)),
        out_shape=jax.ShapeDtypeStruct((REP_ROWS, REP_W), f32), compiler_params=_cparams(1),
    )(*ins)


def _mix_in_fwd(l, x, g1, w_in_p):
    t, d = x.shape
    tb = _tb(t)

    def body(x_ref, g_ref, w_ref, h_ref, *outs):
        h = _rms_fwd(x_ref[...], g_ref[l:l + 1, :]).astype(bf16)
        h_ref[...] = h
        for (off, wd), o_ref in zip(SEGS, outs):
            o_ref[...] = _dot(h, w_ref[:, off:off + wd])

    return pl.pallas_call(
        body, name="mix_in_fwd", grid=(t // tb,),
        in_specs=[_rows(tb, d), _const(g1.shape), _const((d, D_IN_PAD))],
        out_specs=tuple([_rows(tb, d)] + [_rows(tb, wd) for _, wd in SEGS]),
        out_shape=tuple([jax.ShapeDtypeStruct((t, d), bf16)] + [jax.ShapeDtypeStruct((t, wd), f32) for _, wd in SEGS]),
        compiler_params=_cparams(1),
    )(x, g1, w_in_p)


def _conv_fwd(l, a_in, dw_w, dw_b, ln_g, ln_b):
    t = a_in.shape[0]
    tb = _tb(t)

    def body(a_ref, w_ref, b_ref, g_ref, be_ref, s_ref, z1_ref, win):
        @pl.when(pl.program_id(0) == 0)
        def _():
            win[0:CONV_HALO, :] = jnp.zeros((CONV_HALO, D_CONV), f32)

        a = a_ref[...]
        win[CONV_HALO:CONV_HALO + tb, :] = a[:, :D_CONV] * jax.nn.sigmoid(a[:, D_CONV:])
        for r0 in range(0, tb, LANES):
            for c0 in range(0, D_CONV, LANES):
                acc = jnp.broadcast_to(b_ref[l:l + 1, c0:c0 + LANES], (LANES, LANES))
                for j in range(CONV_K):
                    acc = acc + w_ref[j:j + 1, c0:c0 + LANES] * win[pl.ds(r0 + CONV_HALO - CONV_K + 1 + j, LANES), c0:c0 + LANES]
                z1_ref[r0:r0 + LANES, c0:c0 + LANES] = acc
        zh, _ = _ln_stats(z1_ref[...])
        zl = zh * g_ref[l:l + 1, :] + be_ref[l:l + 1, :]
        s_ref[...] = (zl * jax.nn.sigmoid(zl)).astype(bf16)
        win[0:CONV_HALO, :] = win[tb:tb + CONV_HALO, :]

    return pl.pallas_call(
        body, name="conv_fwd", grid=(t // tb,),
        in_specs=[_rows(tb, 2 * D_CONV), _const((CONV_HALO, D_CONV)), _const(dw_b.shape), _const(ln_g.shape), _const(ln_b.shape)],
        out_specs=(_rows(tb, D_CONV), _rows(tb, D_CONV)),
        out_shape=(jax.ShapeDtypeStruct((t, D_CONV), bf16), jax.ShapeDtypeStruct((t, D_CONV), f32)),
        scratch_shapes=[pltpu.VMEM((tb + CONV_HALO, D_CONV), f32)],
        compiler_params=_cparams(1),
    )(a_in, dw_w, dw_b, ln_g, ln_b)


def _tril_mask():
    r = lax.broadcasted_iota(jnp.int32, (SG_CHUNK, SG_CHUNK), 0)
    c = lax.broadcasted_iota(jnp.int32, (SG_CHUNK, SG_CHUNK), 1)
    return r >= c


def _sgu_fwd(l, b_in, ln_g, ln_b, sg_w, bexp):
    t = b_in.shape[0]
    tb = _tb(t)
    gw = D_SG // SG_GROUPS

    def body(b_ref, g_ref, be_ref, w_ref, bexp_ref, um_ref):
        gl = _gelu(b_ref[...])
        u = gl[:, :D_SG]
        vh, _ = _ln_stats(gl[:, D_SG:])
        vn = (vh * g_ref[l:l + 1, :] + be_ref[l:l + 1, :]).astype(bf16)
        tri = _tril_mask()
        for g in range(SG_GROUPS):
            wg = jnp.where(tri, w_ref[l, g], 0.0).astype(bf16)
            cs = slice(g * gw, (g + 1) * gw)
            for r0 in range(0, tb, SG_CHUNK):
                rs = slice(r0, r0 + SG_CHUNK)
                mixed = _dot(wg, vn[rs, cs]) + bexp_ref[:, cs]
                um_ref[rs, cs] = (u[rs, cs] * mixed).astype(bf16)

    return pl.pallas_call(
        body, name="sgu_fwd", grid=(t // tb,),
        in_specs=[_rows(tb, 2 * D_SG), _const(ln_g.shape), _const(ln_b.shape), _const(sg_w.shape), _const((SG_CHUNK, D_SG))],
        out_specs=_rows(tb, D_SG),
        out_shape=jax.ShapeDtypeStruct((t, D_SG), bf16),
        compiler_params=_cparams(1),
    )(b_in, ln_g, ln_b, sg_w, bexp)


def _mla_proj_fwd(l, qlat, kvlat, krope, cos_t, sin_t, gq, wuq_p, gkv, wk_p, wv_p):
    t = qlat.shape[0]
    tb = _tb(t)
    hw = N_HEADS * HEAD_PAD

    def body(ql_ref, kvl_ref, kr_ref, c_ref, s_ref, gq_ref, wq_ref, gkv_ref, wk_ref, wv_ref, q_ref, k_ref, v_ref, qn_ref, kvn_ref):
        cos_b, sin_b = c_ref[...], s_ref[...]
        qn = _rms_fwd(ql_ref[...], gq_ref[l:l + 1, :]).astype(bf16)
        qn_ref[...] = qn
        q = _dot(qn, wq_ref[...])
        sw = _swap_rope_halves(q)
        for h in range(N_HEADS):
            hs = slice(h * HEAD_PAD, (h + 1) * HEAD_PAD)
            q_ref[:, hs] = (q[:, hs] * cos_b + sw[:, hs] * sin_b).astype(bf16)
        kvn = _rms_fwd(kvl_ref[...], gkv_ref[l:l + 1, :]).astype(bf16)
        kvn_ref[...] = kvn
        kr = kr_ref[...]
        kpe = kr * cos_b + _swap_rope_halves(kr) * sin_b
        k = _dot(kvn, wk_ref[...])
        for h in range(N_HEADS):
            hs = slice(h * HEAD_PAD, (h + 1) * HEAD_PAD)
            k_ref[:, hs] = (k[:, hs] + kpe).astype(bf16)
        v_ref[...] = _dot(kvn, wv_ref[...]).astype(bf16)

    return pl.pallas_call(
        body, name="mla_proj_fwd", grid=(t // tb,),
        in_specs=[_rows(tb, Q_LORA), _rows(tb, KV_LORA), _rows(tb, HEAD_PAD), _rows(tb, HEAD_PAD), _rows(tb, HEAD_PAD),
                  _const(gq.shape), _const((Q_LORA, hw)), _const(gkv.shape), _const((KV_LORA, hw)), _const((KV_LORA, hw))],
        out_specs=(_rows(tb, hw), _rows(tb, hw), _rows(tb, hw), _rows(tb, Q_LORA), _rows(tb, KV_LORA)),
        out_shape=(jax.ShapeDtypeStruct((t, hw), bf16), jax.ShapeDtypeStruct((t, hw), bf16), jax.ShapeDtypeStruct((t, hw), bf16),
                   jax.ShapeDtypeStruct((t, Q_LORA), bf16), jax.ShapeDtypeStruct((t, KV_LORA), bf16)),
        compiler_params=_cparams(1),
    )(qlat, kvlat, krope, cos_t, sin_t, gq, wuq_p, gkv, wk_p, wv_p)


def _diag_mask(tq):
    return lax.broadcasted_iota(jnp.int32, (tq, tq), 0) >= lax.broadcasted_iota(jnp.int32, (tq, tq), 1)


def _attn_fwd(q, k, v):
    t = q.shape[0]
    tq = _tb(t)

    def body(q_ref, k_ref, v_ref, o_ref, lse_ref):
        qi = pl.program_id(1)
        for i in range(t // tq):
            @pl.when(qi == i)
            def _(i=i):
                qv = q_ref[...]
                lo, hi = i * tq, (i + 1) * tq
                s_d = jnp.where(_diag_mask(tq), _dot_nt(qv, k_ref[lo:hi, :]) * ATT_SCALE, NEG)
                m = jnp.max(s_d, axis=-1, keepdims=True)
                if i > 0:
                    s_o = _dot_nt(qv, k_ref[0:lo, :]) * ATT_SCALE
                    m = jnp.maximum(m, jnp.max(s_o, axis=-1, keepdims=True))
                p_d = jnp.exp(s_d - m)
                lsum = jnp.sum(p_d, axis=-1, keepdims=True)
                acc = _dot(p_d.astype(bf16), v_ref[lo:hi, :])
                if i > 0:
                    p_o = jnp.exp(s_o - m)
                    lsum = lsum + jnp.sum(p_o, axis=-1, keepdims=True)
                    acc = acc + _dot(p_o.astype(bf16), v_ref[0:lo, :])
                o_ref[...] = (acc / lsum).astype(bf16)
                lse_ref[...] = m + jnp.log(lsum)

    return pl.pallas_call(
        body, name="attn_fwd", grid=(N_HEADS, t // tq),
        in_specs=[pl.BlockSpec((tq, HEAD_PAD), lambda h, i: (i, h)), pl.BlockSpec((t, HEAD_PAD), lambda h, i: (0, h)),
                  pl.BlockSpec((t, HEAD_PAD), lambda h, i: (0, h))],
        out_specs=(pl.BlockSpec((tq, HEAD_PAD), lambda h, i: (i, h)), pl.BlockSpec((None, tq, 1), lambda h, i: (h, i, 0))),
        out_shape=(jax.ShapeDtypeStruct((t, N_HEADS * HEAD_PAD), bf16), jax.ShapeDtypeStruct((N_HEADS, t, 1), f32)),
        compiler_params=_cparams(2),
    )(q, k, v)


def _merge_out_fwd(l, x, s, um, o, gates, conv_out_w, sg_out_w, wo_p, w_out, g2):
    t, d = x.shape
    tb = _tb(t)

    def body(x_ref, s_ref, um_ref, o_ref, gt_ref, wa_ref, wb_ref, wc_ref, wout_ref, g_ref, x1_ref, mg_ref, om_ref):
        merged = (jax.nn.sigmoid(gt_ref[:, 0:d]) * _dot(s_ref[...], wa_ref[...])
                  + jax.nn.sigmoid(gt_ref[:, d:2 * d]) * _dot(um_ref[...], wb_ref[...])
                  + jax.nn.sigmoid(gt_ref[:, 2 * d:3 * d]) * _dot(o_ref[...], wc_ref[...]))
        mb = merged.astype(bf16)
        mg_ref[...] = mb
        om = _dot(mb, wout_ref[...])
        om_ref[...] = om
        x1_ref[...] = x_ref[...] + _rms_fwd(om, g_ref[l:l + 1, :])

    hw = N_HEADS * HEAD_PAD
    return pl.pallas_call(
        body, name="merge_out_fwd", grid=(t // tb,),
        in_specs=[_rows(tb, d), _rows(tb, D_CONV), _rows(tb, D_SG), _rows(tb, hw), _rows(tb, 3 * d),
                  _const((D_CONV, d)), _const((D_SG, d)), _const((hw, d)), _const((d, d)), _const(g2.shape)],
        out_specs=(_rows(tb, d), _rows(tb, d), _rows(tb, d)),
        out_shape=(jax.ShapeDtypeStruct((t, d), f32), jax.ShapeDtypeStruct((t, d), bf16), jax.ShapeDtypeStruct((t, d), f32)),
        compiler_params=_cparams(1),
    )(x, s, um, o, gates, conv_out_w, sg_out_w, wo_p, w_out, g2)


FF_CHUNK = 1408


def _ffn_conv_cols(zbuf, w_ref, b_ref, l, nrows, c0, c1):
    acc = b_ref[l:l + 1, c0:c1] + w_ref[0:1, c0:c1] * zbuf[pl.ds(FFN_HALO - 2, nrows), c0:c1]
    acc = acc + w_ref[1:2, c0:c1] * zbuf[pl.ds(FFN_HALO - 1, nrows), c0:c1]
    return acc + w_ref[2:3, c0:c1] * zbuf[pl.ds(FFN_HALO, nrows), c0:c1]


def _ffn_fwd(l, x1, g3, w_up, dw_w, dw_b, w_down, g4):
    t, d = x1.shape
    tb = _tb(t)
    ff2 = 2 * D_FF

    def body(x_ref, g3_ref, wup_ref, dww_ref, dwb_ref, wdn_ref, g4_ref, x2_ref, h2_ref, z_ref, zc_ref, act_ref, f_ref, zbuf):
        @pl.when(pl.program_id(0) == 0)
        def _():
            zbuf[0:FFN_HALO, :] = jnp.zeros((FFN_HALO, ff2), f32)

        xv = x_ref[...]
        h2 = _rms_fwd(xv, g3_ref[l:l + 1, :]).astype(bf16)
        h2_ref[...] = h2
        for c0 in range(0, ff2, FF_CHUNK):
            zv = _dot_nt(h2, wup_ref[c0:c0 + FF_CHUNK, :])
            z_ref[:, c0:c0 + FF_CHUNK] = zv.astype(bf16)
            zbuf[FFN_HALO:FFN_HALO + tb, c0:c0 + FF_CHUNK] = zv
        facc = jnp.zeros((tb, d), f32)
        for c0 in range(0, D_FF, FF_CHUNK):
            gg = _ffn_conv_cols(zbuf, dww_ref, dwb_ref, l, tb, c0, c0 + FF_CHUNK)
            vv = _ffn_conv_cols(zbuf, dww_ref, dwb_ref, l, tb, D_FF + c0, D_FF + c0 + FF_CHUNK)
            zc_ref[:, c0:c0 + FF_CHUNK] = gg.astype(bf16)
            zc_ref[:, D_FF + c0:D_FF + c0 + FF_CHUNK] = vv.astype(bf16)
            a = (_gelu(gg) * vv).astype(bf16)
            act_ref[:, c0:c0 + FF_CHUNK] = a
            facc = facc + _dot(a, wdn_ref[c0:c0 + FF_CHUNK, :])
        f_ref[...] = facc
        x2_ref[...] = xv + _rms_fwd(facc, g4_ref[l:l + 1, :])
        zbuf[0:FFN_HALO, :] = zbuf[tb:tb + FFN_HALO, :]

    return pl.pallas_call(
        body, name="ffn_fwd", grid=(t // tb,),
        in_specs=[_rows(tb, d), _const(g3.shape), _const((ff2, d)), _const((FFN_HALO, ff2)), _const(dw_b.shape), _const((D_FF, d)),
                  _const(g4.shape)],
        out_specs=(_rows(tb, d), _rows(tb, d), _rows(tb, ff2), _rows(tb, ff2), _rows(tb, D_FF), _rows(tb, d)),
        out_shape=(jax.ShapeDtypeStruct((t, d), f32), jax.ShapeDtypeStruct((t, d), bf16), jax.ShapeDtypeStruct((t, ff2), bf16),
                   jax.ShapeDtypeStruct((t, ff2), bf16), jax.ShapeDtypeStruct((t, D_FF), bf16), jax.ShapeDtypeStruct((t, d), f32)),
        scratch_shapes=[pltpu.VMEM((tb + FFN_HALO, ff2), f32)],
        compiler_params=_cparams(1),
    )(x1, g3, w_up, dw_w, dw_b, w_down, g4)


def _loss_fwd_bwd(y, target):
    t, d = y.shape
    tb = _tb(t)

    def body(y_ref, t_ref, dy_ref, loss_ref):
        @pl.when(pl.program_id(0) == 0)
        def _():
            loss_ref[...] = jnp.zeros((1, LANES), f32)

        e = y_ref[...] - t_ref[...]
        dy_ref[...] = e * (1.0 / d)
        loss_ref[...] += 0.5 * jnp.sum(_mean(e * e))

    dy, loss = pl.pallas_call(
        body, name="loss", grid=(t // tb,),
        in_specs=[_rows(tb, d), _rows(tb, d)],
        out_specs=(_rows(tb, d), _acc((1, LANES))),
        out_shape=(jax.ShapeDtypeStruct((t, d), f32), jax.ShapeDtypeStruct((1, LANES), f32)),
        compiler_params=_cparams(1),
    )(y, target)
    return dy, loss[0, 0]


def _ffn_bwd(l, dx2, x1, f, z, zc, w_up, dw_w, w_down, g3, g4, deps=()):
    t, d = x1.shape
    tb = min(128, t)
    nb = t // tb
    ff2 = 2 * D_FF
    hrows = 16
    per_h = tb // hrows

    def body(*refs):
        (dx2_ref, x1_ref, f_ref, z_ref, zp_ref, zc_ref, wup_ref, dww_ref, wdn_ref, g3_ref, g4_ref,
         dx1_ref, df_ref, dz_ref, dg4_ref, dg3_ref, ddwb_ref, ddww_ref, zbuf, dzc) = refs[len(deps):]
        i = pl.program_id(0)
        b = nb - 1 - i

        @pl.when(i == 0)
        def _():
            dg4_ref[...] = jnp.zeros_like(dg4_ref)
            dg3_ref[...] = jnp.zeros_like(dg3_ref)
            ddwb_ref[...] = jnp.zeros_like(ddwb_ref)
            ddww_ref[...] = jnp.zeros_like(ddww_ref)
            dzc[tb:tb + FFN_HALO, :] = jnp.zeros((FFN_HALO, ff2), f32)

        dout = dx2_ref[...]
        df, dg4 = _rms_bwd(f_ref[...], g4_ref[l:l + 1, :], dout)
        dg4_ref[...] += dg4
        dfb = df.astype(bf16)
        df_ref[...] = dfb
        zbuf[0:FFN_HALO, :] = jnp.where(b > 0, zp_ref[hrows - FFN_HALO:hrows, :].astype(f32), 0.0)
        zbuf[FFN_HALO:FFN_HALO + tb, :] = z_ref[...].astype(f32)
        for c0 in range(0, D_FF, FF_CHUNK):
            dact = _dot_nt(dfb, wdn_ref[c0:c0 + FF_CHUNK, :])
            gel, dgel = _gelu_and_grad(zc_ref[:, c0:c0 + FF_CHUNK].astype(f32))
            dzc[0:tb, c0:c0 + FF_CHUNK] = dact * zc_ref[:, D_FF + c0:D_FF + c0 + FF_CHUNK].astype(f32) * dgel
            dzc[0:tb, D_FF + c0:D_FF + c0 + FF_CHUNK] = dact * gel
        dh2 = jnp.zeros((tb, d), f32)
        for c0 in range(0, ff2, FF_CHUNK):
            cs = slice(c0, c0 + FF_CHUNK)
            d0 = dzc[0:tb, cs]
            ddwb_ref[:, cs] += _colsum(d0)
            for j in range(FFN_K):
                ddww_ref[j:j + 1, cs] += _colsum(d0 * zbuf[pl.ds(FFN_HALO - 2 + j, tb), cs])
            dzv = dww_ref[2:3, cs] * d0 + dww_ref[1:2, cs] * dzc[pl.ds(1, tb), cs] + dww_ref[0:1, cs] * dzc[pl.ds(2, tb), cs]
            dzb = dzv.astype(bf16)
            dz_ref[:, cs] = dzb
            dh2 = dh2 + _dot(dzb, wup_ref[cs, :])
        dzc[tb:tb + FFN_HALO, :] = dzc[0:FFN_HALO, :]
        dxn, dg3 = _rms_bwd(x1_ref[...], g3_ref[l:l + 1, :], dh2)
        dg3_ref[...] += dg3
        dx1_ref[...] = dout + dxn

    return pl.pallas_call(
        body, name="ffn_bwd", grid=(nb,),
        in_specs=[_ANY] * len(deps) + [_rows_rev(tb, d, nb), _rows_rev(tb, d, nb), _rows_rev(tb, d, nb), _rows_rev(tb, ff2, nb),
                  pl.BlockSpec((hrows, ff2), lambda i: (jnp.maximum((nb - 1 - i) * per_h - 1, 0), 0)), _rows_rev(tb, ff2, nb),
                  _const((ff2, d)), _const((FFN_HALO, ff2)), _const((D_FF, d)), _const(g3.shape), _const(g4.shape)],
        out_specs=(_rows_rev(tb, d, nb), _rows_rev(tb, d, nb), _rows_rev(tb, ff2, nb), _acc((1, d)), _acc((1, d)), _acc((1, ff2)),
                   _acc((FFN_HALO, ff2))),
        out_shape=(jax.ShapeDtypeStruct((t, d), f32), jax.ShapeDtypeStruct((t, d), bf16), jax.ShapeDtypeStruct((t, ff2), bf16),
                   jax.ShapeDtypeStruct((1, d), f32), jax.ShapeDtypeStruct((1, d), f32), jax.ShapeDtypeStruct((1, ff2), f32),
                   jax.ShapeDtypeStruct((FFN_HALO, ff2), f32)),
        scratch_shapes=[pltpu.VMEM((tb + FFN_HALO, ff2), f32), pltpu.VMEM((tb + FFN_HALO, ff2), f32)],
        compiler_params=_cparams(1),
    )(*deps, dx2, x1, f, z, z, zc, w_up, dw_w, w_down, g3, g4)


def _merge_out_bwd(l, dx1, om, s, um, o, gates, conv_out_w, sg_out_w, wo_p, w_out, g2, deps=()):
    t, d = dx1.shape
    tb = _tb(t)
    hw = N_HEADS * HEAD_PAD

    def body(*refs):
        (dx_ref, om_ref, s_ref, um_ref, o_ref, gt_ref, wa_ref, wb_ref, wc_ref, wout_ref, g_ref,
         dom_ref, dgt_ref, dya_ref, dyb_ref, dyc_ref, ds_ref, dum_ref, do_ref, dg2_ref) = refs[len(deps):]

        @pl.when(pl.program_id(0) == 0)
        def _():
            dg2_ref[...] = jnp.zeros_like(dg2_ref)

        dom, dg2 = _rms_bwd(om_ref[...], g_ref[l:l + 1, :], dx_ref[...])
        dg2_ref[...] += dg2
        domb = dom.astype(bf16)
        dom_ref[...] = domb
        dmerged = _dot_nt(domb, wout_ref[...])
        branches = ((s_ref, wa_ref, dya_ref, ds_ref), (um_ref, wb_ref, dyb_ref, dum_ref), (o_ref, wc_ref, dyc_ref, do_ref))
        for br, (in_ref, w_ref, dy_ref, din_ref) in enumerate(branches):
            yv = _dot(in_ref[...], w_ref[...])
            sg = jax.nn.sigmoid(gt_ref[:, br * d:(br + 1) * d])
            dyb = (dmerged * sg).astype(bf16)
            dy_ref[...] = dyb
            dgt_ref[:, br * d:(br + 1) * d] = (dmerged * yv * sg * (1.0 - sg)).astype(bf16)
            din_ref[...] = _dot_nt(dyb, w_ref[...]).astype(din_ref.dtype)

    return pl.pallas_call(
        body, name="merge_out_bwd", grid=(t // tb,),
        in_specs=[_ANY] * len(deps) + [_rows(tb, d), _rows(tb, d), _rows(tb, D_CONV), _rows(tb, D_SG), _rows(tb, hw), _rows(tb, 3 * d),
                  _const((D_CONV, d)), _const((D_SG, d)), _const((hw, d)), _const((d, d)), _const(g2.shape)],
        out_specs=(_rows(tb, d), _rows(tb, 3 * d), _rows(tb, d), _rows(tb, d), _rows(tb, d), _rows(tb, D_CONV), _rows(tb, D_SG),
                   _rows(tb, hw), _acc((1, d))),
        out_shape=(jax.ShapeDtypeStruct((t, d), bf16), jax.ShapeDtypeStruct((t, 3 * d), bf16), jax.ShapeDtypeStruct((t, d), bf16),
                   jax.ShapeDtypeStruct((t, d), bf16), jax.ShapeDtypeStruct((t, d), bf16), jax.ShapeDtypeStruct((t, D_CONV), f32),
                   jax.ShapeDtypeStruct((t, D_SG), f32), jax.ShapeDtypeStruct((t, hw), bf16), jax.ShapeDtypeStruct((1, d), f32)),
        compiler_params=_cparams(1),
    )(*deps, dx1, om, s, um, o, gates, conv_out_w, sg_out_w, wo_p, w_out, g2)


def _attn_bwd(q, k, v, o, lse, do):
    t = q.shape[0]
    tq = _tb(t)
    hw = N_HEADS * HEAD_PAD

    def body(q_ref, k_ref, v_ref, o_ref, lse_ref, do_ref, dq_ref, dk_ref, dv_ref):
        qi = pl.program_id(1)

        @pl.when(qi == 0)
        def _():
            dk_ref[...] = jnp.zeros_like(dk_ref)
            dv_ref[...] = jnp.zeros_like(dv_ref)

        def keys(lo, hi, qv, dov, lse, delta, diagonal):
            kj, vj = k_ref[lo:hi, :], v_ref[lo:hi, :]
            p = jnp.exp(_dot_nt(qv, kj) * ATT_SCALE - lse)
            if diagonal:
                p = jnp.where(_diag_mask(tq), p, 0.0)
            ds = (p * (_dot_nt(dov, vj) - delta) * ATT_SCALE).astype(bf16)
            dk_ref[lo:hi, :] += _dot_tn(ds, qv)
            dv_ref[lo:hi, :] += _dot_tn(p.astype(bf16), dov)
            return _dot(ds, kj)

        for i in range(t // tq):
            @pl.when(qi == i)
            def _(i=i):
                qv, dov, lse = q_ref[...], do_ref[...], lse_ref[...]
                delta = jnp.sum(dov.astype(f32) * o_ref[...].astype(f32), axis=-1, keepdims=True)
                dq = keys(i * tq, (i + 1) * tq, qv, dov, lse, delta, True)
                if i > 0:
                    dq = dq + keys(0, i * tq, qv, dov, lse, delta, False)
                dq_ref[...] = dq

    blk_q = pl.BlockSpec((tq, HEAD_PAD), lambda h, i: (i, h))
    blk_kv = pl.BlockSpec((t, HEAD_PAD), lambda h, i: (0, h))
    return pl.pallas_call(
        body, name="attn_bwd", grid=(N_HEADS, t // tq),
        in_specs=[blk_q, blk_kv, blk_kv, blk_q, pl.BlockSpec((None, tq, 1), lambda h, i: (h, i, 0)), blk_q],
        out_specs=(blk_q, blk_kv, blk_kv),
        out_shape=(jax.ShapeDtypeStruct((t, hw), f32), jax.ShapeDtypeStruct((t, hw), f32), jax.ShapeDtypeStruct((t, hw), f32)),
        compiler_params=_cparams(2),
    )(q, k, v, o, lse, do)


def _mla_proj_bwd(l, dq, dk, dv, qlat, kvlat, cos_t, sin_t, gq, wuq_p, gkv, wk_p, wv_p):
    t = qlat.shape[0]
    tb = _tb(t)
    hw = N_HEADS * HEAD_PAD

    def body(dq_ref, dk_ref, dv_ref, ql_ref, kvl_ref, c_ref, s_ref, gq_ref, wq_ref, gkv_ref, wk_ref, wv_ref,
             dqb_ref, dkb_ref, dvb_ref, dql_ref, dkvl_ref, dkr_ref, dgq_ref, dgkv_ref):
        @pl.when(pl.program_id(0) == 0)
        def _():
            dgq_ref[...] = jnp.zeros_like(dgq_ref)
            dgkv_ref[...] = jnp.zeros_like(dgkv_ref)

        cos_b, sin_b = c_ref[...], s_ref[...]
        for h in range(N_HEADS):
            hs = slice(h * HEAD_PAD, (h + 1) * HEAD_PAD)
            dqh = dq_ref[:, hs]
            dqb_ref[:, hs] = (dqh * cos_b + _swap_rope_halves(dqh * sin_b)).astype(bf16)
        dqn = _dot_nt(dqb_ref[...], wq_ref[...])
        dql, dgq = _rms_bwd(ql_ref[...], gq_ref[l:l + 1, :], dqn)
        dgq_ref[...] += dgq
        dql_ref[...] = dql.astype(bf16)
        dkv_full = dk_ref[...]
        dkb = dkv_full.astype(bf16)
        dkb_ref[...] = dkb
        dkpe = dkv_full[:, 0:HEAD_PAD]
        for h in range(1, N_HEADS):
            dkpe = dkpe + dkv_full[:, h * HEAD_PAD:(h + 1) * HEAD_PAD]
        dkr_ref[...] = (dkpe * cos_b + _swap_rope_halves(dkpe * sin_b)).astype(bf16)
        dvb = dv_ref[...].astype(bf16)
        dvb_ref[...] = dvb
        dkvn = _dot_nt(dkb, wk_ref[...]) + _dot_nt(dvb, wv_ref[...])
        dkvl, dgkv = _rms_bwd(kvl_ref[...], gkv_ref[l:l + 1, :], dkvn)
        dgkv_ref[...] += dgkv
        dkvl_ref[...] = dkvl.astype(bf16)

    return pl.pallas_call(
        body, name="mla_proj_bwd", grid=(t // tb,),
        in_specs=[_rows(tb, hw), _rows(tb, hw), _rows(tb, hw), _rows(tb, Q_LORA), _rows(tb, KV_LORA), _rows(tb, HEAD_PAD),
                  _rows(tb, HEAD_PAD), _const(gq.shape), _const((Q_LORA, hw)), _const(gkv.shape), _const((KV_LORA, hw)),
                  _const((KV_LORA, hw))],
        out_specs=(_rows(tb, hw), _rows(tb, hw), _rows(tb, hw), _rows(tb, Q_LORA), _rows(tb, KV_LORA), _rows(tb, HEAD_PAD),
                   _acc((1, Q_LORA)), _acc((1, KV_LORA))),
        out_shape=(jax.ShapeDtypeStruct((t, hw), bf16), jax.ShapeDtypeStruct((t, hw), bf16), jax.ShapeDtypeStruct((t, hw), bf16),
                   jax.ShapeDtypeStruct((t, Q_LORA), bf16), jax.ShapeDtypeStruct((t, KV_LORA), bf16),
                   jax.ShapeDtypeStruct((t, HEAD_PAD), bf16), jax.ShapeDtypeStruct((1, Q_LORA), f32),
                   jax.ShapeDtypeStruct((1, KV_LORA), f32)),
        compiler_params=_cparams(1),
    )(dq, dk, dv, qlat, kvlat, cos_t, sin_t, gq, wuq_p, gkv, wk_p, wv_p)


def _sgu_bwd(l, b_in, dum, ln_g, ln_b, sg_w, bexp):
    t = b_in.shape[0]
    tb = _tb(t)
    gw = D_SG // SG_GROUPS

    def body(b_ref, dum_ref, g_ref, be_ref, w_ref, bexp_ref, db_ref, dw_ref, dsgb_ref, dlg_ref, dlb_ref, dvn_s):
        @pl.when(pl.program_id(0) == 0)
        def _():
            dw_ref[...] = jnp.zeros_like(dw_ref)
            dsgb_ref[...] = jnp.zeros_like(dsgb_ref)
            dlg_ref[...] = jnp.zeros_like(dlg_ref)
            dlb_ref[...] = jnp.zeros_like(dlb_ref)

        gl, dgl = _gelu_and_grad(b_ref[...])
        u = gl[:, :D_SG]
        vh, rstd = _ln_stats(gl[:, D_SG:])
        ln_gain = g_ref[l:l + 1, :]
        vn = (vh * ln_gain + be_ref[l:l + 1, :]).astype(bf16)
        dumv = dum_ref[...]
        tri = _tril_mask()
        ones = jnp.ones((FFN_HALO, gw), f32)
        for g in range(SG_GROUPS):
            wg = jnp.where(tri, w_ref[l, g], 0.0).astype(bf16)
            cs = slice(g * gw, (g + 1) * gw)
            for r0 in range(0, tb, SG_CHUNK):
                rs = slice(r0, r0 + SG_CHUNK)
                vblk = vn[rs, cs]
                mixed = _dot(wg, vblk) + bexp_ref[:, cs]
                db_ref[rs, cs] = (dumv[rs, cs] * mixed * dgl[rs, cs]).astype(bf16)
                dmix = dumv[rs, cs] * u[rs, cs]
                dmb = dmix.astype(bf16)
                dw_ref[g] += jnp.where(tri, _dot_nt(dmb, vblk), 0.0)
                rowsum = lax.dot_general(ones, dmix, (((1,), (1,)), ((), ())), preferred_element_type=f32,
                                         precision=lax.Precision.HIGHEST)
                dsgb_ref[g:g + 1, :] += rowsum[0:1, :]
                dvn_s[rs, cs] = _dot_tn(wg, dmb)
        dvn = dvn_s[...]
        dlg_ref[...] += _colsum(dvn * vh)
        dlb_ref[...] += _colsum(dvn)
        db_ref[:, D_SG:] = (_ln_bwd(vh, rstd, ln_gain, dvn) * dgl[:, D_SG:]).astype(bf16)

    return pl.pallas_call(
        body, name="sgu_bwd", grid=(t // tb,),
        in_specs=[_rows(tb, 2 * D_SG), _rows(tb, D_SG), _const(ln_g.shape), _const(ln_b.shape), _const(sg_w.shape),
                  _const((SG_CHUNK, D_SG))],
        out_specs=(_rows(tb, 2 * D_SG), _acc((SG_GROUPS, SG_CHUNK, SG_CHUNK)), _acc((FFN_HALO, SG_CHUNK)), _acc((1, D_SG)),
                   _acc((1, D_SG))),
        out_shape=(jax.ShapeDtypeStruct((t, 2 * D_SG), bf16), jax.ShapeDtypeStruct((SG_GROUPS, SG_CHUNK, SG_CHUNK), f32),
                   jax.ShapeDtypeStruct((FFN_HALO, SG_CHUNK), f32), jax.ShapeDtypeStruct((1, D_SG), f32),
                   jax.ShapeDtypeStruct((1, D_SG), f32)),
        scratch_shapes=[pltpu.VMEM((tb, D_SG), f32)],
        compiler_params=_cparams(1),
    )(b_in, dum, ln_g, ln_b, sg_w, bexp)


def _conv_bwd(l, a_in, z1, ds, dw_w, ln_g, ln_b):
    t = a_in.shape[0]
    tb = _tb(t)
    nb = t // tb
    per_halo = tb // CONV_HALO

    def body(a_ref, ap_ref, z1_ref, ds_ref, w_ref, g_ref, be_ref, da_ref, ddww_ref, ddwb_ref, dlg_ref, dlb_ref, win, dzb):
        i = pl.program_id(0)
        b = nb - 1 - i

        @pl.when(i == 0)
        def _():
            ddww_ref[...] = jnp.zeros_like(ddww_ref)
            ddwb_ref[...] = jnp.zeros_like(ddwb_ref)
            dlg_ref[...] = jnp.zeros_like(dlg_ref)
            dlb_ref[...] = jnp.zeros_like(dlb_ref)
            dzb[tb:tb + CONV_HALO, :] = jnp.zeros((CONV_HALO, D_CONV), f32)

        a = a_ref[...]
        val = a[:, :D_CONV]
        sg = jax.nn.sigmoid(a[:, D_CONV:])
        ap = ap_ref[...]
        win[0:CONV_HALO, :] = jnp.where(b > 0, ap[:, :D_CONV] * jax.nn.sigmoid(ap[:, D_CONV:]), 0.0)
        win[CONV_HALO:CONV_HALO + tb, :] = val * sg
        zh, rstd = _ln_stats(z1_ref[...])
        ln_gain = g_ref[l:l + 1, :]
        zl = zh * ln_gain + be_ref[l:l + 1, :]
        sgl = jax.nn.sigmoid(zl)
        dzl = ds_ref[...] * (sgl * (1.0 + zl * (1.0 - sgl)))
        dlg_ref[...] += _colsum(dzl * zh)
        dlb_ref[...] += _colsum(dzl)
        dz1 = _ln_bwd(zh, rstd, ln_gain, dzl)
        dzb[0:tb, :] = dz1
        ddwb_ref[...] += _colsum(dz1)
        dgate_f = val * sg * (1.0 - sg)
        for c0 in range(0, D_CONV, LANES):
            cs = slice(c0, c0 + LANES)
            for r0 in range(0, tb, LANES):
                d1 = dzb[r0:r0 + LANES, cs]
                acc = jnp.zeros((LANES, LANES), f32)
                for j in range(CONV_K):
                    ddww_ref[j:j + 1, cs] += _colsum(d1 * win[pl.ds(r0 + CONV_HALO - CONV_K + 1 + j, LANES), cs])
                    acc = acc + w_ref[j:j + 1, cs] * dzb[pl.ds(r0 + CONV_K - 1 - j, LANES), cs]
                da_ref[r0:r0 + LANES, cs] = (acc * sg[r0:r0 + LANES, cs]).astype(bf16)
                da_ref[r0:r0 + LANES, c0 + D_CONV:c0 + D_CONV + LANES] = (acc * dgate_f[r0:r0 + LANES, cs]).astype(bf16)
        dzb[tb:tb + CONV_HALO, :] = dzb[0:CONV_HALO, :]

    return pl.pallas_call(
        body, name="conv_bwd", grid=(nb,),
        in_specs=[_rows_rev(tb, 2 * D_CONV, nb),
                  pl.BlockSpec((CONV_HALO, 2 * D_CONV), lambda i: (jnp.maximum((nb - 1 - i) * per_halo - 1, 0), 0)),
                  _rows_rev(tb, D_CONV, nb), _rows_rev(tb, D_CONV, nb), _const((CONV_HALO, D_CONV)), _const(ln_g.shape),
                  _const(ln_b.shape)],
        out_specs=(_rows_rev(tb, 2 * D_CONV, nb), _acc((CONV_HALO, D_CONV)), _acc((1, D_CONV)), _acc((1, D_CONV)), _acc((1, D_CONV))),
        out_shape=(jax.ShapeDtypeStruct((t, 2 * D_CONV), bf16), jax.ShapeDtypeStruct((CONV_HALO, D_CONV), f32),
                   jax.ShapeDtypeStruct((1, D_CONV), f32), jax.ShapeDtypeStruct((1, D_CONV), f32), jax.ShapeDtypeStruct((1, D_CONV), f32)),
        scratch_shapes=[pltpu.VMEM((tb + CONV_HALO, D_CONV), f32), pltpu.VMEM((tb + CONV_HALO, D_CONV), f32)],
        compiler_params=_cparams(1),
    )(a_in, a_in, z1, ds, dw_w, ln_g, ln_b)


def _mix_in_bwd(l, x, g1, dxres, dsegs, w_in_p, deps=()):
    t, d = x.shape
    tb = _tb(t)

    def body(*refs):
        x_ref, g_ref, dr_ref = refs[len(deps):len(deps) + 3]
        rest = refs[len(deps) + 3:]
        dseg_refs, w_ref, dx_ref, dg_ref = rest[:len(SEGS)], rest[len(SEGS)], rest[len(SEGS) + 1], rest[len(SEGS) + 2]

        @pl.when(pl.program_id(0) == 0)
        def _():
            dg_ref[...] = jnp.zeros_like(dg_ref)

        dh = jnp.zeros((tb, d), f32)
        for (off, wd), ds_ref in zip(SEGS, dseg_refs):
            dh = dh + _dot_nt(ds_ref[...], w_ref[:, off:off + wd])
        dxn, dg = _rms_bwd(x_ref[...], g_ref[l:l + 1, :], dh)
        dg_ref[...] += dg
        dx_ref[...] = dr_ref[...] + dxn

    return pl.pallas_call(
        body, name="mix_in_bwd", grid=(t // tb,),
        in_specs=([_ANY] * len(deps) + [_rows(tb, d), _const(g1.shape), _rows(tb, d)] + [_rows(tb, wd) for _, wd in SEGS]
                  + [_const((d, D_IN_PAD))]),
        out_specs=(_rows(tb, d), _acc((1, d))),
        out_shape=(jax.ShapeDtypeStruct((t, d), f32), jax.ShapeDtypeStruct((1, d), f32)),
        compiler_params=_cparams(1),
    )(*deps, x, g1, dxres, *dsegs, w_in_p)


def _pick_block(n, cap=512):
    for b in (cap, 384, 256, 128):
        if b <= cap and n % b == 0:
            return b
    return n


def _wgrad(a, b, name, out_dtype=f32):
    t, kdim = a.shape
    n = b.shape[1]
    bk, bn = _pick_block(kdim), _pick_block(n)

    def body(a_ref, b_ref, o_ref):
        o_ref[...] = _dot_tn(a_ref[...], b_ref[...]).astype(out_dtype)

    return pl.pallas_call(
        body, name=name, grid=(kdim // bk, n // bn),
        in_specs=[pl.BlockSpec((t, bk), lambda i, j: (0, i)), pl.BlockSpec((t, bn), lambda i, j: (0, j))],
        out_specs=pl.BlockSpec((bk, bn), lambda i, j: (i, j)),
        out_shape=jax.ShapeDtypeStruct((kdim, n), out_dtype),
        compiler_params=_cparams(2),
    )(a, b)


_BC1 = 1.0 - ADAM_B1 ** ADAM_STEP
_BC2 = 1.0 - ADAM_B2 ** ADAM_STEP


def _adam_math(g, w, m, v):
    nm = ADAM_B1 * m + (1.0 - ADAM_B1) * g
    nv = ADAM_B2 * v + (1.0 - ADAM_B2) * (g * g)
    delta = -ADAM_LR * ((nm / _BC1) / (jnp.sqrt(nv / _BC2) + ADAM_EPS) + ADAM_WD * w)
    return delta, nm, nv


def _slot_sum(r_ref, index=()):
    g = r_ref[(0,) + index].astype(f32)
    for s in range(1, N_DEV):
        g = g + r_ref[(s,) + index].astype(f32)
    return g


def _adamw_shard(l, recv, w, m, v, prev, name, deps=()):
    _, k, ns = recv.shape
    rb = next((c for c in (256, 192, 176, 128) if k % c == 0), k)
    blk = pl.BlockSpec((None, rb, ns), lambda i: (l, i, 0))

    def body(r_ref, w_ref, m_ref, v_ref, *rest):
        g_ref, d_ref, nm_ref, nv_ref = rest[-4:]
        g = _slot_sum(r_ref)
        g_ref[...] = g
        d_ref[...], nm_ref[...], nv_ref[...] = _adam_math(g, w_ref[...], m_ref[...], v_ref[...])

    out = jax.ShapeDtypeStruct(w.shape, f32)
    n_prev = 0 if prev is None else 4
    return pl.pallas_call(
        body, name=name, grid=(k // rb,),
        in_specs=[pl.BlockSpec((N_DEV, rb, ns), lambda i: (0, i, 0)), blk, blk, blk] + [_ANY] * (n_prev + len(deps)),
        out_specs=(blk, blk, blk, blk), out_shape=(out, out, out, out),
        input_output_aliases={4 + j: j for j in range(n_prev)},
        compiler_params=_cparams(1),
    )(recv, w, m, v, *(prev or ()), *deps)


def _adamw_rep(recvs, ws, ms, vs):
    depth = len(recvs)
    nt = len(REP_VECS)

    def body(*refs):
        r_refs = refs[:depth]
        w_refs, m_refs, v_refs = (refs[depth + i * nt:depth + (i + 1) * nt] for i in range(3))
        outs = refs[depth + 3 * nt:]
        for ti, (_, row, width, nrows) in enumerate(REP_VECS):
            for l in range(depth):
                g = r_refs[l][0, row:row + nrows, 0:width]
                for s in range(1, N_DEV):
                    g = g + r_refs[l][s, row:row + nrows, 0:width]
                pick = (lambda ref: ref[l]) if nrows > 1 else (lambda ref: ref[l:l + 1, :])
                delta, nm, nv = _adam_math(g, pick(w_refs[ti]), pick(m_refs[ti]), pick(v_refs[ti]))
                for o_ref, val in zip(outs[4 * ti:4 * ti + 4], (g, delta, nm, nv)):
                    if nrows > 1:
                        o_ref[l] = val
                    else:
                        o_ref[l:l + 1, :] = val

    ins = tuple(recvs) + tuple(ws) + tuple(ms) + tuple(vs)
    out_shape = tuple(jax.ShapeDtypeStruct(w.shape, f32) for w in ws for _ in range(4))
    return pl.pallas_call(
        body, name="adamw_rep", grid=(1,),
        in_specs=[_whole(a) for a in ins], out_specs=tuple(_whole(o) for o in out_shape), out_shape=out_shape,
        compiler_params=_cparams(1),
    )(*ins)


def _adamw_sg_w(recvs, w, m, v):
    depth = len(recvs)

    def body(*refs):
        r_refs = refs[:depth]
        w_ref, m_ref, v_ref = refs[depth:depth + 3]
        outs = refs[depth + 3:]
        for l in range(depth):
            for gi in range(SG_GROUPS):
                g = _slot_sum(r_refs[l], (gi,))
                delta, nm, nv = _adam_math(g, w_ref[l, gi], m_ref[l, gi], v_ref[l, gi])
                for o_ref, val in zip(outs, (g, delta, nm, nv)):
                    o_ref[l, gi] = val

    ins = tuple(recvs) + (w, m, v)
    out = jax.ShapeDtypeStruct(w.shape, f32)
    return pl.pallas_call(
        body, name="adamw_sg_w", grid=(1,),
        in_specs=[_whole(a) for a in ins], out_specs=tuple(_whole(out) for _ in range(4)), out_shape=(out,) * 4,
        compiler_params=_cparams(1),
    )(*ins)


def _rope_tables(positions):
    t = positions.shape[0]
    inv = 10000.0 ** (-jnp.arange(0, QK_ROPE, 2, dtype=f32) / QK_ROPE)
    ang = positions.astype(f32)[:, None] * inv
    cos, sin = jnp.cos(ang), jnp.sin(ang)
    tail = jnp.zeros((t, HEAD_PAD - KR_LO - QK_ROPE), f32)
    cos_t = jnp.concatenate([jnp.ones((t, KR_LO), f32), cos, cos, tail], axis=1)
    sin_t = jnp.concatenate([jnp.zeros((t, KR_LO), f32), -sin, sin, tail], axis=1)
    return cos_t, sin_t


def _bias_over_channels(sg_b_l):
    return jnp.broadcast_to(sg_b_l.T[:, :, None], (SG_CHUNK, SG_GROUPS, D_SG // SG_GROUPS)).reshape(SG_CHUNK, D_SG)


def _mixer_weights(gathered, rep, l):
    conv_out_w, sg_out_w, wuq_p, wk_p, wv_p, wo_p, conv_dw_w, ffn_dw_w = _asm_small(
        gathered["conv_out_w"], gathered["sg_out_w"], gathered["mla_w_uq"], gathered["mla_w_ukv"], gathered["mla_w_o"],
        gathered["conv_dw_w"], gathered["ffn_dw_w"])
    g_out = gathered["w_out"]
    w = dict(rep)
    w.update(
        l=l, w_in_p=_asm_w_in(gathered["w_in"]),
        conv_out_w=conv_out_w, sg_out_w=sg_out_w, wuq_p=wuq_p, wk_p=wk_p, wv_p=wv_p, wo_p=wo_p, conv_dw_w_p=conv_dw_w, ffn_dw_w_p=ffn_dw_w,
        w_out=g_out.reshape(g_out.shape[0] * g_out.shape[1], g_out.shape[2]),
        bexp=_bias_over_channels(rep["sg_b"][l]))
    return w


def _ffn_weights(gathered):
    stack = lambda g: g.reshape(g.shape[0] * g.shape[1], g.shape[2])
    return dict(w_up=stack(gathered["ffn_w_up"]), w_down=stack(gathered["ffn_w_down"]))


def _mixer_fwd(x, w, cos_t, sin_t):
    l = w["l"]
    h, a_in, b_in, qlat, kvlat, krope, gates = _mix_in_fwd(l, x, w["mix_pre_g"], w["w_in_p"])
    s, z1 = _conv_fwd(l, a_in, w["conv_dw_w_p"], w["conv_dw_b"], w["conv_ln_g"], w["conv_ln_b"])
    um = _sgu_fwd(l, b_in, w["sg_ln_g"], w["sg_ln_b"], w["sg_w"], w["bexp"])
    q, k, v, qn, kvn = _mla_proj_fwd(l, qlat, kvlat, krope, cos_t, sin_t, w["mla_q_norm_g"], w["wuq_p"], w["mla_kv_norm_g"],
                                     w["wk_p"], w["wv_p"])
    o, lse = _attn_fwd(q, k, v)
    x1, merged, om = _merge_out_fwd(l, x, s, um, o, gates, w["conv_out_w"], w["sg_out_w"], w["wo_p"], w["w_out"], w["mix_post_g"])
    saved = dict(x=x, h=h, a_in=a_in, b_in=b_in, qlat=qlat, kvlat=kvlat, gates=gates, s=s, z1=z1, um=um, q=q, k=k, v=v, qn=qn,
                 kvn=kvn, o=o, lse=lse, x1=x1, merged=merged, om=om)
    return x1, saved


def _ffn_layer_fwd(x1, w):
    x2, h2, z, zc, act, f = _ffn_fwd(w["l"], x1, w["ffn_pre_g"], w["w_up"], w["ffn_dw_w_p"], w["ffn_dw_b"], w["w_down"], w["ffn_post_g"])
    return x2, dict(h2=h2, z=z, zc=zc, act=act, f=f)


def _ffn_layer_bwd(dx2, sv, w, deps=()):
    l = w["l"]
    vec = {}
    dx1, df, dz, vec["ffn_post_g"], vec["ffn_pre_g"], vec["ffn_dw_b"], vec["d_fdw"] = _ffn_bwd(
        l, dx2, sv["x1"], sv["f"], sv["z"], sv["zc"], w["w_up"], w["ffn_dw_w_p"], w["w_down"], w["ffn_pre_g"], w["ffn_post_g"],
        deps=deps)
    d_down = _wgrad(sv["act"], df, "wgrad_ffn_down", bf16)
    d_up_t = _wgrad(dz, sv["h2"], "wgrad_ffn_up", bf16)
    unstack = lambda g: g.reshape(N_DEV, g.shape[0] // N_DEV, g.shape[1])
    send = dict(ffn_w_up=unstack(d_up_t), ffn_w_down=unstack(d_down))
    return dx1, send, vec


def _mixer_bwd(dx1, sv, w, cos_t, sin_t, shard_cols, vec, start_small, deps=()):
    l = w["l"]
    d_fdw = vec.pop("d_fdw")
    dom, dgates, dya, dyb, dyc, ds, dum, do, vec["mix_post_g"] = _merge_out_bwd(
        l, dx1, sv["om"], sv["s"], sv["um"], sv["o"], sv["gates"], w["conv_out_w"], w["sg_out_w"], w["wo_p"], w["w_out"], w["mix_post_g"],
        deps=deps)
    d_out = _wgrad(sv["merged"], dom, "wgrad_w_out", bf16)
    d_co = _wgrad(sv["s"], dya, "wgrad_conv_out")
    d_so = _wgrad(sv["um"], dyb, "wgrad_sg_out")
    d_wo = _wgrad(sv["o"], dyc, "wgrad_w_o")

    dq, dk, dv = _attn_bwd(sv["q"], sv["k"], sv["v"], sv["o"], sv["lse"], do)
    dqb, dkb, dvb, dqlat, dkvlat, dkrope, vec["mla_q_norm_g"], vec["mla_kv_norm_g"] = _mla_proj_bwd(
        l, dq, dk, dv, sv["qlat"], sv["kvlat"], cos_t, sin_t, w["mla_q_norm_g"], w["wuq_p"], w["mla_kv_norm_g"], w["wk_p"], w["wv_p"])
    d_uq = _wgrad(sv["qn"], dqb, "wgrad_w_uq")
    d_uk = _wgrad(sv["kvn"], dkb, "wgrad_w_uk")
    d_uv = _wgrad(sv["kvn"], dvb, "wgrad_w_uv")

    db_in, dsg_w, dsgb, vec["sg_ln_g"], vec["sg_ln_b"] = _sgu_bwd(l, sv["b_in"], dum, w["sg_ln_g"], w["sg_ln_b"], w["sg_w"], w["bexp"])
    da_in, d_cdw, vec["conv_dw_b"], vec["conv_ln_g"], vec["conv_ln_b"] = _conv_bwd(
        l, sv["a_in"], sv["z1"], ds, w["conv_dw_w_p"], w["conv_ln_g"], w["conv_ln_b"])

    send = {}
    (send["conv_out_w"], send["sg_out_w"], send["mla_w_uq"], send["mla_w_ukv"], send["mla_w_o"], send["conv_dw_w"],
     send["ffn_dw_w"]) = _dis_small(d_co, d_so, d_uq, d_uk, d_uv, d_wo, d_cdw, d_fdw)
    send["w_out"] = d_out.reshape(N_DEV, d_out.shape[0] // N_DEV, d_out.shape[1])
    token = start_small(send)

    dsegs = (da_in, db_in, dqlat, dkvlat, dkrope, dgates)
    dx, vec["mix_pre_g"] = _mix_in_bwd(l, sv["x"], w["mix_pre_g"], dx1, dsegs, w["w_in_p"], deps=token)
    names = ("a", "b", "q", "kv", "kr", "g")
    d_in_segs = [_wgrad(sv["h"], dseg, "wgrad_w_in_" + nm) for nm, dseg in zip(names, dsegs)]
    rep_pack = _pack_rep([vec[n] for n, _, _, _ in REP_VECS[:-1]], dsgb)
    return dx, _dis_w_in(d_in_segs, shard_cols["w_in"]), rep_pack, dsg_w


def kernel(x, positions, mix_pre_g, mix_post_g, ffn_pre_g, ffn_post_g, w_in, conv_dw_w, conv_dw_b, conv_ln_g, conv_ln_b, conv_out_w, sg_ln_g, sg_ln_b, sg_w, sg_b, sg_out_w, mla_q_norm_g, mla_w_uq, mla_kv_norm_g, mla_w_ukv, mla_w_o, w_out, ffn_w_up, ffn_dw_w, ffn_dw_b, ffn_w_down, loss_target, m_mix_pre_g, m_mix_post_g, m_ffn_pre_g, m_ffn_post_g, m_w_in, m_conv_dw_w, m_conv_dw_b, m_conv_ln_g, m_conv_ln_b, m_conv_out_w, m_sg_ln_g, m_sg_ln_b, m_sg_w, m_sg_b, m_sg_out_w, m_mla_q_norm_g, m_mla_w_uq, m_mla_kv_norm_g, m_mla_w_ukv, m_mla_w_o, m_w_out, m_ffn_w_up, m_ffn_dw_w, m_ffn_dw_b, m_ffn_w_down, v_mix_pre_g, v_mix_post_g, v_ffn_pre_g, v_ffn_post_g, v_w_in, v_conv_dw_w, v_conv_dw_b, v_conv_ln_g, v_conv_ln_b, v_conv_out_w, v_sg_ln_g, v_sg_ln_b, v_sg_w, v_sg_b, v_sg_out_w, v_mla_q_norm_g, v_mla_w_uq, v_mla_kv_norm_g, v_mla_w_ukv, v_mla_w_o, v_w_out, v_ffn_w_up, v_ffn_dw_w, v_ffn_dw_b, v_ffn_w_down):
    args = (x, positions, mix_pre_g, mix_post_g, ffn_pre_g, ffn_post_g, w_in, conv_dw_w, conv_dw_b, conv_ln_g, conv_ln_b, conv_out_w, sg_ln_g, sg_ln_b, sg_w, sg_b, sg_out_w, mla_q_norm_g, mla_w_uq, mla_kv_norm_g, mla_w_ukv, mla_w_o, w_out, ffn_w_up, ffn_dw_w, ffn_dw_b, ffn_w_down, loss_target, m_mix_pre_g, m_mix_post_g, m_ffn_pre_g, m_ffn_post_g, m_w_in, m_conv_dw_w, m_conv_dw_b, m_conv_ln_g, m_conv_ln_b, m_conv_out_w, m_sg_ln_g, m_sg_ln_b, m_sg_w, m_sg_b, m_sg_out_w, m_mla_q_norm_g, m_mla_w_uq, m_mla_kv_norm_g, m_mla_w_ukv, m_mla_w_o, m_w_out, m_ffn_w_up, m_ffn_dw_w, m_ffn_dw_b, m_ffn_w_down, v_mix_pre_g, v_mix_post_g, v_ffn_pre_g, v_ffn_post_g, v_w_in, v_conv_dw_w, v_conv_dw_b, v_conv_ln_g, v_conv_ln_b, v_conv_out_w, v_sg_ln_g, v_sg_ln_b, v_sg_w, v_sg_b, v_sg_out_w, v_mla_q_norm_g, v_mla_w_uq, v_mla_kv_norm_g, v_mla_w_ukv, v_mla_w_o, v_w_out, v_ffn_w_up, v_ffn_dw_w, v_ffn_dw_b, v_ffn_w_down)
    n_in = len(IN_NAMES)
    a = dict(zip(IN_NAMES, args[:n_in]))
    target = args[n_in]
    n_w = len(WEIGHTS)
    m_in = dict(zip(WEIGHTS, args[n_in + 1:n_in + 1 + n_w]))
    v_in = dict(zip(WEIGHTS, args[n_in + 1 + n_w:n_in + 1 + 2 * n_w]))
    depth = a["mix_pre_g"].shape[0]
    rep = {n: a[n] for n in WEIGHTS if n not in SHARDED}
    shard_cols = {n: a[n].shape[2] for n in SHARDED}

    def view(arr, n):
        return jnp.swapaxes(arr, 1, 2) if n in TRANSPOSED else arr

    w_sh = {n: view(a[n], n) for n in SHARDED}
    m_sh = {n: view(m_in[n], n) for n in SHARDED}
    v_sh = {n: view(v_in[n], n) for n in SHARDED}

    starts, token = {}, ()
    for l in range(depth):
        for group, tag in ((MIX_GROUP, "mix"), (FFN_BIG, "ffn")):
            wire = [w_sh[n][l] if n in WIRE_F32 else w_sh[n][l].astype(bf16) for n in group]
            starts[l, tag] = _gather_start(wire, "gather_%s_weights_%d" % (tag, l), deps=token)
            token = (starts[l, tag]["token"],)

    cos_t, sin_t = _rope_tables(a["positions"][0])
    xl = a["x"][0]
    ws, saved = [], []
    passing = {(0, "mix"): _gather_forward(starts[0, "mix"], after=token)}
    after = (passing[0, "mix"]["token"],)
    for l in range(depth):
        w = _mixer_weights(dict(zip(MIX_GROUP, _gather_wait(passing[l, "mix"], after=after))), rep, l)
        x1, sv = _mixer_fwd(xl, w, cos_t, sin_t)
        passing[l, "ffn"] = _gather_forward(starts[l, "ffn"], after=(x1,))
        after = (passing[l, "ffn"]["token"],)
        if l + 1 < depth:
            passing[l + 1, "mix"] = _gather_forward(starts[l + 1, "mix"], after=after)
            after = (passing[l + 1, "mix"]["token"],)
        w.update(_ffn_weights(dict(zip(FFN_BIG, _gather_wait(passing[l, "ffn"], after=after)))))
        xl, sv_ffn = _ffn_layer_fwd(x1, w)
        sv.update(sv_ffn)
        ws.append(w)
        saved.append(sv)
        after = (xl,)
    dx, loss_part = _loss_fwd_bwd(xl, target[0])
    loss = lax.psum(loss_part, AXES)

    outs = {}
    rep_recvs, sgw_recvs = [None] * depth, [None] * depth

    def finish(l, handles, after):
        for names, handle in zip((FFN_BIG, MIX_SMALL, ("w_in",)), handles):
            got = _exchange_wait(handle, after=after)
            for n, recv in zip(names, got):
                outs[n] = _adamw_shard(l, recv, w_sh[n], m_sh[n], v_sh[n], outs.get(n), "adamw_" + n)
            after = tuple(outs[n][0] for n in names)
        rep_recvs[l], sgw_recvs[l] = got[1], got[2]
        return after

    token, prev, after = (), None, ()
    for l in reversed(range(depth)):
        dx1, send_ffn, vec = _ffn_layer_bwd(dx, saved[l], ws[l], deps=token)
        handles = [_exchange_start([(send_ffn[n], "scatter") for n in FFN_BIG], "exchange_ffn_grads_%d" % l)]

        def start_small(send, l=l, handles=handles):
            handles.append(_exchange_start([(send[n], "scatter") for n in MIX_SMALL], "exchange_mix_grads_%d" % l))
            return (handles[-1]["token"],)

        dx, send_w_in, rep_pack, dsg_w = _mixer_bwd(dx1, saved[l], ws[l], cos_t, sin_t, shard_cols, vec, start_small,
                                                    deps=(handles[0]["token"],))
        handles.append(_exchange_start([(send_w_in, "scatter"), (rep_pack, "gather"), (dsg_w, "gather")],
                                       "exchange_w_in_grads_%d" % l))
        token = (handles[-1]["token"],)
        if prev is not None:
            after = finish(*prev, after=(dx,) + token)
        prev = (l, handles)
    finish(*prev, after=after)
    vec_names = [n for n, _, _, _ in REP_VECS]
    rep_outs = _adamw_rep(rep_recvs, [a[n] for n in vec_names], [m_in[n] for n in vec_names], [v_in[n] for n in vec_names])
    for i, n in enumerate(vec_names):
        outs[n] = rep_outs[4 * i:4 * i + 4]
    outs["sg_w"] = _adamw_sg_w(sgw_recvs, a["sg_w"], m_in["sg_w"], v_in["sg_w"])
    for n in TRANSPOSED:
        outs[n] = tuple(view(o, n) for o in outs[n])

    grad_w, delta_w, new_m, new_v = ([outs[n][j] for n in WEIGHTS] for j in range(4))
    return (loss, dx[None], *grad_w, *delta_w, *new_m, *new_v)
```

```python
import math

import jax
import jax.numpy as jnp
from jax import lax
from jax.experimental import pallas as pl
from jax.experimental.pallas import tpu as pltpu

f32 = jnp.float32
bf16 = jnp.bfloat16

N_DEV = 8
AXES = ("x", "y", "c")
EPS = 1e-6
D_CONV = 512
CONV_K = 31
CONV_HALO = 32
D_SG = 512
SG_GROUPS = 4
SG_CHUNK = 128
N_HEADS = 8
QK_NOPE = 64
QK_ROPE = 32
V_HEAD = 64
HEAD_PAD = 128
Q_LORA = 384
KV_LORA = 256
D_FF = 2816
FFN_K = 3
FFN_HALO = 8
ATT_SCALE = (QK_NOPE + QK_ROPE) ** -0.5
NEG = float(jnp.finfo(jnp.float32).min)

ADAM_LR = 0.001
ADAM_B1 = 0.9
ADAM_B2 = 0.999
ADAM_EPS = 1e-08
ADAM_WD = 0.01
ADAM_STEP = 10

LANES = 128
VMEM_MB = 56

REF_CUTS = (0, 1024, 2048, 2432, 2688, 2720, 5792)
SEGS = ((0, 1024), (1024, 1024), (2048, 384), (2432, 256), (2688, 128), (2816, 3072))
D_IN = 5792
D_IN_PAD = 5888
KR_LO = 64
SEG_INNER = (0, 0, 0, 0, KR_LO, 0)

IN_NAMES = ['x', 'positions', 'mix_pre_g', 'mix_post_g', 'ffn_pre_g', 'ffn_post_g', 'w_in', 'conv_dw_w', 'conv_dw_b', 'conv_ln_g', 'conv_ln_b', 'conv_out_w', 'sg_ln_g', 'sg_ln_b', 'sg_w', 'sg_b', 'sg_out_w', 'mla_q_norm_g', 'mla_w_uq', 'mla_kv_norm_g', 'mla_w_ukv', 'mla_w_o', 'w_out', 'ffn_w_up', 'ffn_dw_w', 'ffn_dw_b', 'ffn_w_down']
WEIGHTS = IN_NAMES[2:]
SHARDED = ("w_in", "conv_dw_w", "conv_out_w", "sg_out_w", "mla_w_uq", "mla_w_ukv", "mla_w_o", "w_out", "ffn_w_up", "ffn_dw_w",
           "ffn_w_down")
FFN_BIG = ("ffn_w_up", "ffn_w_down")
MIX_GROUP = tuple(n for n in SHARDED if n not in FFN_BIG)
MIX_SMALL = tuple(n for n in MIX_GROUP if n != "w_in")
TRANSPOSED = ("ffn_w_up",)
WIRE_F32 = ("conv_dw_w", "ffn_dw_w")
REP_VECS = (("mix_pre_g", 0, 1024, 1), ("mix_post_g", 1, 1024, 1), ("ffn_pre_g", 2, 1024, 1), ("ffn_post_g", 3, 1024, 1),
            ("conv_dw_b", 4, 512, 1), ("conv_ln_g", 5, 512, 1), ("conv_ln_b", 6, 512, 1), ("sg_ln_g", 7, 512, 1),
            ("sg_ln_b", 8, 512, 1), ("mla_q_norm_g", 9, 384, 1), ("mla_kv_norm_g", 10, 256, 1), ("ffn_dw_b", 11, 5632, 1),
            ("sg_b", 12, 128, 4))
REP_ROWS = 16
REP_W = 5632


def _cparams(n_axes):
    return pltpu.CompilerParams(dimension_semantics=("arbitrary",) * n_axes, vmem_limit_bytes=VMEM_MB * 2 ** 20)


def _rows(tb, n):
    return pl.BlockSpec((tb, n), lambda i: (i, 0))


def _rows_rev(tb, n, nb):
    return pl.BlockSpec((tb, n), lambda i: (nb - 1 - i, 0))


def _const(shape):
    nd = len(shape)
    return pl.BlockSpec(shape, lambda *_: (0,) * nd, pipeline_mode=pl.Buffered(1))


def _acc(shape):
    nd = len(shape)
    return pl.BlockSpec(shape, lambda *_: (0,) * nd)


def _whole(arr):
    return pl.BlockSpec(arr.shape, lambda *_: (0,) * arr.ndim)


def _dot(a, b):
    return jnp.dot(a, b, preferred_element_type=f32)


def _dot_nt(a, b):
    return lax.dot_general(a, b, (((1,), (1,)), ((), ())), preferred_element_type=f32)


def _dot_tn(a, b):
    return lax.dot_general(a, b, (((0,), (0,)), ((), ())), preferred_element_type=f32)


def _mean(x):
    return jnp.mean(x, axis=-1, keepdims=True)


def _colsum(x):
    return jnp.sum(x, axis=0, keepdims=True)


def _rms_fwd(x, g):
    return x * lax.rsqrt(_mean(x * x) + EPS) * g


def _rms_bwd(x, g, dy):
    r = lax.rsqrt(_mean(x * x) + EPS)
    n = x * r
    dn = dy * g
    return r * (dn - n * _mean(dn * n)), _colsum(dy * n)


def _ln_stats(x):
    mu = _mean(x)
    d = x - mu
    rstd = lax.rsqrt(_mean(d * d) + EPS)
    return d * rstd, rstd


def _ln_bwd(xhat, rstd, g, dy):
    dxh = dy * g
    return rstd * (dxh - _mean(dxh) - xhat * _mean(dxh * xhat))


_GELU_C0 = math.sqrt(2.0 / math.pi)
_GELU_C1 = 0.044715


def _gelu(x):
    t = jnp.tanh(_GELU_C0 * (x + _GELU_C1 * (x * x * x)))
    return 0.5 * x * (1.0 + t)


def _gelu_and_grad(x):
    x2 = x * x
    t = jnp.tanh(_GELU_C0 * (x + _GELU_C1 * (x2 * x)))
    g = 0.5 * x * (1.0 + t)
    dg = 0.5 * (1.0 + t) + 0.5 * x * (1.0 - t * t) * (_GELU_C0 * (1.0 + 3.0 * _GELU_C1 * x2))
    return g, dg


def _swap_rope_halves(x):
    n = x.shape[1]
    half = QK_ROPE // 2
    lane = lax.broadcasted_iota(jnp.int32, x.shape, 1) % HEAD_PAD
    first = (lane >= KR_LO) & (lane < KR_LO + half)
    second = (lane >= KR_LO + half) & (lane < KR_LO + QK_ROPE)
    return jnp.where(first, pltpu.roll(x, n - half, 1), jnp.where(second, pltpu.roll(x, half, 1), 0.0))


def _tb(t):
    return min(256, t)


_HBM = pl.BlockSpec(memory_space=pltpu.HBM)
_SEM = pl.BlockSpec(memory_space=pltpu.SEMAPHORE)
_ANY = pl.BlockSpec(memory_space=pl.ANY)
_EFFECT = pltpu.SideEffectType.DATAFLOW_SIDE_EFFECTING


def _exchange_copies(modes, ins, lands, send_sems, recv_sems, loc_sems):
    x, y, c = lax.axis_index("x"), lax.axis_index("y"), lax.axis_index("c")
    me = 4 * x + 2 * y + c
    copies = []
    for a, mode in enumerate(modes):
        def src(dst_index, a=a, mode=mode):
            return ins[a].at[dst_index] if mode == "scatter" else ins[a]
        copies.append(pltpu.make_async_copy(src(me), lands[a].at[me], loc_sems.at[a]))
        for k in range(1, N_DEV):
            px = 1 - x if (k >> 2) & 1 else x
            py = 1 - y if (k >> 1) & 1 else y
            pc = 1 - c if k & 1 else c
            copies.append(pltpu.make_async_remote_copy(
                src_ref=src(4 * px + 2 * py + pc), dst_ref=lands[a].at[me],
                send_sem=send_sems.at[a * (N_DEV - 1) + k - 1], recv_sem=recv_sems.at[a * (N_DEV - 1) + k - 1],
                device_id=(px, py, pc), device_id_type=pl.DeviceIdType.MESH))
    return copies


def _exchange_start(ops, name, deps=()):
    n = len(ops)
    arrs = [arr for arr, _ in ops]
    modes = [mode for _, mode in ops]
    lands = [lax.empty((N_DEV,) + arr.shape if mode == "gather" else arr.shape, arr.dtype) for arr, mode in ops]

    def body(*refs):
        ins, land_refs = refs[:n], refs[n:2 * n]
        send_sems, recv_sems, loc_sems = refs[2 * n + len(deps):2 * n + len(deps) + 3]
        for cp in _exchange_copies(modes, ins, land_refs, send_sems, recv_sems, loc_sems):
            cp.start()
        refs[-1][...] = jnp.zeros((8, LANES), f32)

    n_rem = n * (N_DEV - 1)
    res = pl.pallas_call(
        body, name=name,
        out_shape=(pltpu.SemaphoreType.DMA((n_rem,)), pltpu.SemaphoreType.DMA((n_rem,)), pltpu.SemaphoreType.DMA((n,)),
                   *[pltpu.HBM(x.shape, x.dtype) for x in arrs + lands], jax.ShapeDtypeStruct((8, LANES), f32)),
        in_specs=[_HBM] * (2 * n) + [_ANY] * len(deps),
        out_specs=(_SEM, _SEM, _SEM, *[_HBM] * (2 * n), pl.BlockSpec(memory_space=pltpu.VMEM)),
        input_output_aliases={i: 3 + i for i in range(2 * n)},
        compiler_params=pltpu.CompilerParams(has_side_effects=_EFFECT),
    )(*[pltpu.with_memory_space_constraint(x, pltpu.HBM) for x in arrs + lands], *deps)
    return dict(modes=modes, sems=res[:3], thru=res[3:3 + 2 * n], token=res[-1], name=name)


def _exchange_wait(handle, after=()):
    modes, thru = handle["modes"], handle["thru"]
    n = len(modes)

    def body(*refs):
        ins, land_refs = refs[:n], refs[n:2 * n]
        send_sems, recv_sems, loc_sems = refs[2 * n:2 * n + 3]
        for cp in _exchange_copies(modes, ins, land_refs, send_sems, recv_sems, loc_sems):
            cp.wait()

    res = pl.pallas_call(
        body, name=handle["name"] + "_wait",
        out_shape=tuple(pltpu.HBM(x.shape, x.dtype) for x in thru),
        in_specs=[_HBM] * (2 * n) + [_SEM] * 3 + [_ANY] * len(after),
        out_specs=tuple([_HBM] * (2 * n)),
        input_output_aliases={i: i for i in range(2 * n)},
        compiler_params=pltpu.CompilerParams(has_side_effects=_EFFECT),
    )(*thru, *handle["sems"], *after)
    return res[n:]


_SAME_CORE_PEERS = ((1, 0), (0, 1), (1, 1))


def _gather_copies_one(ins, lands, send_sems, recv_sems, loc_sems):
    x, y, c = lax.axis_index("x"), lax.axis_index("y"), lax.axis_index("c")
    me = 4 * x + 2 * y + c
    targets = [(x, y, 1 - c)] + [(1 - x if fx else x, 1 - y if fy else y, c) for fx, fy in _SAME_CORE_PEERS]
    copies = []
    for a in range(len(ins)):
        copies.append(pltpu.make_async_copy(ins[a], lands[a].at[me], loc_sems.at[a]))
        for j, target in enumerate(targets):
            copies.append(pltpu.make_async_remote_copy(
                src_ref=ins[a], dst_ref=lands[a].at[me], send_sem=send_sems.at[4 * a + j], recv_sem=recv_sems.at[4 * a + j],
                device_id=target, device_id_type=pl.DeviceIdType.MESH))
    return copies


def _gather_copies_two(lands, send_sems, recv_sems):
    x, y, c = lax.axis_index("x"), lax.axis_index("y"), lax.axis_index("c")
    copies = []
    for a in range(len(lands)):
        for j, (fx, fy) in enumerate(_SAME_CORE_PEERS):
            slot = 4 * (1 - x if fx else x) + 2 * (1 - y if fy else y) + c
            copies.append(pltpu.make_async_remote_copy(
                src_ref=lands[a].at[slot], dst_ref=lands[a].at[slot], send_sem=send_sems.at[3 * a + j],
                recv_sem=recv_sems.at[3 * a + j], device_id=(x, y, 1 - c), device_id_type=pl.DeviceIdType.MESH))
    return copies


def _gather_start(arrs, name, deps=()):
    n = len(arrs)
    lands = [lax.empty((N_DEV,) + arr.shape, arr.dtype) for arr in arrs]

    def body(*refs):
        ins, land_refs = refs[:n], refs[n:2 * n]
        send_sems, recv_sems, loc_sems = refs[2 * n + len(deps):2 * n + len(deps) + 3]
        for cp in _gather_copies_one(ins, land_refs, send_sems, recv_sems, loc_sems):
            cp.start()
        refs[-1][...] = jnp.zeros((8, LANES), f32)

    res = pl.pallas_call(
        body, name=name,
        out_shape=(pltpu.SemaphoreType.DMA((4 * n,)), pltpu.SemaphoreType.DMA((4 * n,)), pltpu.SemaphoreType.DMA((n,)),
                   *[pltpu.HBM(x.shape, x.dtype) for x in arrs + lands], jax.ShapeDtypeStruct((8, LANES), f32)),
        in_specs=[_HBM] * (2 * n) + [_ANY] * len(deps),
        out_specs=(_SEM, _SEM, _SEM, *[_HBM] * (2 * n), pl.BlockSpec(memory_space=pltpu.VMEM)),
        input_output_aliases={i: 3 + i for i in range(2 * n)},
        compiler_params=pltpu.CompilerParams(has_side_effects=_EFFECT),
    )(*[pltpu.with_memory_space_constraint(x, pltpu.HBM) for x in arrs + lands], *deps)
    return dict(n=n, sems=res[:3], thru=res[3:3 + 2 * n], token=res[-1], name=name)


def _gather_forward(handle, after=()):
    n, thru = handle["n"], handle["thru"]

    def body(*refs):
        ins, land_refs = refs[:n], refs[n:2 * n]
        send_one, recv_one, loc_sems = refs[2 * n:2 * n + 3]
        send_two, recv_two = refs[2 * n + 3 + len(after):2 * n + 5 + len(after)]
        for cp in _gather_copies_one(ins, land_refs, send_one, recv_one, loc_sems):
            cp.wait()
        for cp in _gather_copies_two(land_refs, send_two, recv_two):
            cp.start()
        refs[-1][...] = jnp.zeros((8, LANES), f32)

    res = pl.pallas_call(
        body, name=handle["name"] + "_forward",
        out_shape=(pltpu.SemaphoreType.DMA((3 * n,)), pltpu.SemaphoreType.DMA((3 * n,)),
                   *[pltpu.HBM(x.shape, x.dtype) for x in thru], jax.ShapeDtypeStruct((8, LANES), f32)),
        in_specs=[_HBM] * (2 * n) + [_SEM] * 3 + [_ANY] * len(after),
        out_specs=(_SEM, _SEM, *[_HBM] * (2 * n), pl.BlockSpec(memory_space=pltpu.VMEM)),
        input_output_aliases={i: 2 + i for i in range(2 * n)},
        compiler_params=pltpu.CompilerParams(has_side_effects=_EFFECT),
    )(*thru, *handle["sems"], *after)
    return dict(n=n, sems=res[:2], lands=res[2 + n:2 + 2 * n], token=res[-1], name=handle["name"])


def _gather_wait(handle, after=()):
    n, lands = handle["n"], handle["lands"]

    def body(*refs):
        land_refs = refs[:n]
        send_two, recv_two = refs[n:n + 2]
        for cp in _gather_copies_two(land_refs, send_two, recv_two):
            cp.wait()

    return pl.pallas_call(
        body, name=handle["name"] + "_wait",
        out_shape=tuple(pltpu.HBM(x.shape, x.dtype) for x in lands),
        in_specs=[_HBM] * n + [_SEM] * 2 + [_ANY] * len(after),
        out_specs=tuple([_HBM] * n),
        input_output_aliases={i: i for i in range(n)},
        compiler_params=pltpu.CompilerParams(has_side_effects=_EFFECT),
    )(*lands, *handle["sems"], *after)


def _w_in_pieces(ns):
    out = []
    for e in range(N_DEV):
        lo, hi = ns * e, ns * (e + 1)
        for s in range(len(SEGS)):
            a, b = max(lo, REF_CUTS[s]), min(hi, REF_CUTS[s + 1])
            if a < b:
                inner = SEG_INNER[s] + a - REF_CUTS[s]
                out.append((e, a - lo, b - lo, s, inner, inner + b - a))
    return out


def _asm_w_in(g):
    _, d, ns = g.shape
    rb = 256
    pieces = _w_in_pieces(ns)

    def body(g_ref, o_ref):
        kr = SEGS[4][0]
        o_ref[:, kr:kr + KR_LO] = jnp.zeros((rb, KR_LO), g.dtype)
        o_ref[:, kr + KR_LO + QK_ROPE:kr + HEAD_PAD] = jnp.zeros((rb, HEAD_PAD - KR_LO - QK_ROPE), g.dtype)
        for e, s0, s1, seg, d0, d1 in pieces:
            off = SEGS[seg][0]
            o_ref[:, off + d0:off + d1] = g_ref[e, :, s0:s1]

    return pl.pallas_call(
        body, name="asm_w_in", grid=(d // rb,),
        in_specs=[pl.BlockSpec((N_DEV, rb, ns), lambda i: (0, i, 0))],
        out_specs=_rows(rb, D_IN_PAD), out_shape=jax.ShapeDtypeStruct((d, D_IN_PAD), g.dtype),
        compiler_params=_cparams(1),
    )(g)


def _dis_w_in(dsegs, ns):
    d = dsegs[0].shape[0]
    rb = 256
    pieces = _w_in_pieces(ns)

    def body(*refs):
        seg_refs, o_ref = refs[:len(SEGS)], refs[len(SEGS)]
        for e, s0, s1, seg, d0, d1 in pieces:
            o_ref[e, :, s0:s1] = seg_refs[seg][:, d0:d1].astype(bf16)

    return pl.pallas_call(
        body, name="dis_w_in", grid=(d // rb,),
        in_specs=[_rows(rb, wd) for _, wd in SEGS],
        out_specs=pl.BlockSpec((N_DEV, rb, ns), lambda i: (0, i, 0)),
        out_shape=jax.ShapeDtypeStruct((N_DEV, d, ns), bf16),
        compiler_params=_cparams(1),
    )(*dsegs)


def _asm_small(g_conv_out, g_sg_out, g_uq, g_ukv, g_wo, g_cdw, g_fdw):
    d = g_conv_out.shape[2] * N_DEV
    hw = N_HEADS * HEAD_PAD
    hq = QK_NOPE + QK_ROPE
    ff2 = g_fdw.shape[2] * N_DEV
    cw, fw = g_cdw.shape[2], g_fdw.shape[2]

    def body(co_ref, so_ref, uq_ref, ukv_ref, wo_ref, cdw_ref, fdw_ref, o_co, o_so, o_uq, o_k, o_v, o_wo, o_cdw, o_fdw):
        o_cdw[CONV_K:CONV_HALO, :] = jnp.zeros((CONV_HALO - CONV_K, D_CONV), f32)
        o_fdw[FFN_K:FFN_HALO, :] = jnp.zeros((FFN_HALO - FFN_K, ff2), f32)
        for e in range(N_DEV):
            cs = e * HEAD_PAD
            o_co[:, cs:cs + HEAD_PAD] = co_ref[e]
            o_so[:, cs:cs + HEAD_PAD] = so_ref[e]
            o_uq[:, cs:cs + hq] = uq_ref[e]
            o_uq[:, cs + hq:cs + HEAD_PAD] = jnp.zeros((Q_LORA, HEAD_PAD - hq), bf16)
            o_k[:, cs:cs + QK_NOPE] = ukv_ref[e, :, 0:QK_NOPE]
            o_k[:, cs + QK_NOPE:cs + HEAD_PAD] = jnp.zeros((KV_LORA, HEAD_PAD - QK_NOPE), bf16)
            o_v[:, cs:cs + V_HEAD] = ukv_ref[e, :, QK_NOPE:QK_NOPE + V_HEAD]
            o_v[:, cs + V_HEAD:cs + HEAD_PAD] = jnp.zeros((KV_LORA, HEAD_PAD - V_HEAD), bf16)
            for h in range(N_HEADS):
                o_wo[h * HEAD_PAD:h * HEAD_PAD + V_HEAD, cs:cs + HEAD_PAD] = wo_ref[e, h * V_HEAD:(h + 1) * V_HEAD, :]
                o_wo[h * HEAD_PAD + V_HEAD:(h + 1) * HEAD_PAD, cs:cs + HEAD_PAD] = jnp.zeros((HEAD_PAD - V_HEAD, HEAD_PAD), bf16)
            o_cdw[0:CONV_K, e * cw:(e + 1) * cw] = cdw_ref[e]
            o_fdw[0:FFN_K, e * fw:(e + 1) * fw] = fdw_ref[e]

    ins = (g_conv_out, g_sg_out, g_uq, g_ukv, g_wo, g_cdw, g_fdw)
    out_shape = (jax.ShapeDtypeStruct((D_CONV, d), bf16), jax.ShapeDtypeStruct((D_SG, d), bf16), jax.ShapeDtypeStruct((Q_LORA, hw), bf16),
                 jax.ShapeDtypeStruct((KV_LORA, hw), bf16), jax.ShapeDtypeStruct((KV_LORA, hw), bf16), jax.ShapeDtypeStruct((hw, d), bf16),
                 jax.ShapeDtypeStruct((CONV_HALO, D_CONV), f32), jax.ShapeDtypeStruct((FFN_HALO, ff2), f32))
    return pl.pallas_call(
        body, name="asm_small", grid=(1,),
        in_specs=[_whole(a) for a in ins], out_specs=tuple(_whole(o) for o in out_shape), out_shape=out_shape,
        compiler_params=_cparams(1),
    )(*ins)


def _dis_small(d_co, d_so, d_uq, d_k, d_v, d_wo, d_cdw, d_fdw):
    d = d_co.shape[1]
    hq = QK_NOPE + QK_ROPE
    cw, fw = D_CONV // N_DEV, d_fdw.shape[1] // N_DEV

    def body(co_ref, so_ref, uq_ref, k_ref, v_ref, wo_ref, cdw_ref, fdw_ref, o_co, o_so, o_uq, o_ukv, o_wo, o_cdw, o_fdw):
        for e in range(N_DEV):
            cs = e * HEAD_PAD
            o_co[e] = co_ref[:, cs:cs + HEAD_PAD].astype(bf16)
            o_so[e] = so_ref[:, cs:cs + HEAD_PAD].astype(bf16)
            o_uq[e] = uq_ref[:, cs:cs + hq].astype(bf16)
            o_ukv[e, :, 0:QK_NOPE] = k_ref[:, cs:cs + QK_NOPE].astype(bf16)
            o_ukv[e, :, QK_NOPE:QK_NOPE + V_HEAD] = v_ref[:, cs:cs + V_HEAD].astype(bf16)
            for h in range(N_HEADS):
                o_wo[e, h * V_HEAD:(h + 1) * V_HEAD, :] = wo_ref[h * HEAD_PAD:h * HEAD_PAD + V_HEAD, cs:cs + HEAD_PAD].astype(bf16)
            o_cdw[e] = cdw_ref[0:CONV_K, e * cw:(e + 1) * cw]
            o_fdw[e] = fdw_ref[0:FFN_K, e * fw:(e + 1) * fw]

    ins = (d_co, d_so, d_uq, d_k, d_v, d_wo, d_cdw, d_fdw)
    out_shape = (jax.ShapeDtypeStruct((N_DEV, D_CONV, d // N_DEV), bf16), jax.ShapeDtypeStruct((N_DEV, D_SG, d // N_DEV), bf16),
                 jax.ShapeDtypeStruct((N_DEV, Q_LORA, hq), bf16), jax.ShapeDtypeStruct((N_DEV, KV_LORA, QK_NOPE + V_HEAD), bf16),
                 jax.ShapeDtypeStruct((N_DEV, N_HEADS * V_HEAD, d // N_DEV), bf16), jax.ShapeDtypeStruct((N_DEV, CONV_K, cw), f32),
                 jax.ShapeDtypeStruct((N_DEV, FFN_K, fw), f32))
    return pl.pallas_call(
        body, name="dis_small", grid=(1,),
        in_specs=[_whole(a) for a in ins], out_specs=tuple(_whole(o) for o in out_shape), out_shape=out_shape,
        compiler_params=_cparams(1),
    )(*ins)


def _pack_rep(vec_grads, dsgb):
    def body(*refs):
        o_ref = refs[-1]
        o_ref[...] = jnp.zeros((REP_ROWS, REP_W), f32)
        for (_, row, width, nrows), ref in zip(REP_VECS, refs[:-1]):
            o_ref[row:row + nrows, 0:width] = ref[0:nrows, :]

    ins = tuple(vec_grads) + (dsgb,)
    return pl.pallas_call(
        body, name="pack_rep", grid=(1,),
        in_specs=[_whole(a) for a in ins], out_specs=pl.BlockSpec((REP_ROWS, REP_W), lambda i: (0, 0)),
        out_shape=jax.ShapeDtypeStruct((REP_ROWS, REP_W), f32), compiler_params=_cparams(1),
    )(*ins)


def _mix_in_fwd(l, x, g1, w_in_p):
    t, d = x.shape
    tb = min(512, t)

    def body(x_ref, g_ref, w_ref, h_ref, *outs):
        h = _rms_fwd(x_ref[...], g_ref[l:l + 1, :]).astype(bf16)
        h_ref[...] = h
        for (off, wd), o_ref in zip(SEGS, outs):
            o_ref[...] = _dot(h, w_ref[:, off:off + wd])

    return pl.pallas_call(
        body, name="mix_in_fwd", grid=(t // tb,),
        in_specs=[_rows(tb, d), _const(g1.shape), _const((d, D_IN_PAD))],
        out_specs=tuple([_rows(tb, d)] + [_rows(tb, wd) for _, wd in SEGS]),
        out_shape=tuple([jax.ShapeDtypeStruct((t, d), bf16)] + [jax.ShapeDtypeStruct((t, wd), f32) for _, wd in SEGS]),
        compiler_params=_cparams(1),
    )(x, g1, w_in_p)


def _conv_fwd(l, a_in, dw_w, dw_b, ln_g, ln_b):
    t = a_in.shape[0]
    tb = _tb(t)

    def body(a_ref, w_ref, b_ref, g_ref, be_ref, s_ref, z1_ref, win):
        @pl.when(pl.program_id(0) == 0)
        def _():
            win[0:CONV_HALO, :] = jnp.zeros((CONV_HALO, D_CONV), f32)

        a = a_ref[...]
        win[CONV_HALO:CONV_HALO + tb, :] = a[:, :D_CONV] * jax.nn.sigmoid(a[:, D_CONV:])
        for r0 in range(0, tb, LANES):
            for c0 in range(0, D_CONV, LANES):
                cs = slice(c0, c0 + LANES)
                acc = jnp.broadcast_to(b_ref[l:l + 1, cs], (LANES, LANES))
                for j in range(CONV_K):
                    acc = acc + w_ref[j:j + 1, cs] * win[pl.ds(r0 + CONV_HALO - CONV_K + 1 + j, LANES), cs]
                z1_ref[r0:r0 + LANES, cs] = acc
        zh, _ = _ln_stats(z1_ref[...])
        zl = zh * g_ref[l:l + 1, :] + be_ref[l:l + 1, :]
        s_ref[...] = (zl * jax.nn.sigmoid(zl)).astype(bf16)
        win[0:CONV_HALO, :] = win[tb:tb + CONV_HALO, :]

    return pl.pallas_call(
        body, name="conv_fwd", grid=(t // tb,),
        in_specs=[_rows(tb, 2 * D_CONV), _const((CONV_HALO, D_CONV)), _const(dw_b.shape), _const(ln_g.shape), _const(ln_b.shape)],
        out_specs=(_rows(tb, D_CONV), _rows(tb, D_CONV)),
        out_shape=(jax.ShapeDtypeStruct((t, D_CONV), bf16), jax.ShapeDtypeStruct((t, D_CONV), f32)),
        scratch_shapes=[pltpu.VMEM((tb + CONV_HALO, D_CONV), f32)],
        compiler_params=_cparams(1),
    )(a_in, dw_w, dw_b, ln_g, ln_b)


def _tril_mask():
    r = lax.broadcasted_iota(jnp.int32, (SG_CHUNK, SG_CHUNK), 0)
    c = lax.broadcasted_iota(jnp.int32, (SG_CHUNK, SG_CHUNK), 1)
    return r >= c


def _sgu_fwd(l, b_in, ln_g, ln_b, sg_w, bexp):
    t = b_in.shape[0]
    tb = _tb(t)
    gw = D_SG // SG_GROUPS

    def body(b_ref, g_ref, be_ref, w_ref, bexp_ref, um_ref):
        gl = _gelu(b_ref[...])
        u = gl[:, :D_SG]
        vh, _ = _ln_stats(gl[:, D_SG:])
        vn = (vh * g_ref[l:l + 1, :] + be_ref[l:l + 1, :]).astype(bf16)
        tri = _tril_mask()
        for g in range(SG_GROUPS):
            wg = jnp.where(tri, w_ref[l, g], 0.0).astype(bf16)
            cs = slice(g * gw, (g + 1) * gw)
            for r0 in range(0, tb, SG_CHUNK):
                rs = slice(r0, r0 + SG_CHUNK)
                mixed = _dot(wg, vn[rs, cs]) + bexp_ref[:, cs]
                um_ref[rs, cs] = (u[rs, cs] * mixed).astype(bf16)

    return pl.pallas_call(
        body, name="sgu_fwd", grid=(t // tb,),
        in_specs=[_rows(tb, 2 * D_SG), _const(ln_g.shape), _const(ln_b.shape), _const(sg_w.shape), _const((SG_CHUNK, D_SG))],
        out_specs=_rows(tb, D_SG),
        out_shape=jax.ShapeDtypeStruct((t, D_SG), bf16),
        compiler_params=_cparams(1),
    )(b_in, ln_g, ln_b, sg_w, bexp)


def _mla_proj_fwd(l, qlat, kvlat, krope, cos_t, sin_t, gq, wuq_p, gkv, wk_p, wv_p):
    t = qlat.shape[0]
    tb = _tb(t)
    hw = N_HEADS * HEAD_PAD

    def body(ql_ref, kvl_ref, kr_ref, c_ref, s_ref, gq_ref, wq_ref, gkv_ref, wk_ref, wv_ref, q_ref, k_ref, v_ref, qn_ref, kvn_ref):
        cos_b, sin_b = c_ref[...], s_ref[...]
        qn = _rms_fwd(ql_ref[...], gq_ref[l:l + 1, :]).astype(bf16)
        qn_ref[...] = qn
        q = _dot(qn, wq_ref[...])
        sw = _swap_rope_halves(q)
        for h in range(N_HEADS):
            hs = slice(h * HEAD_PAD, (h + 1) * HEAD_PAD)
            q_ref[:, hs] = (q[:, hs] * cos_b + sw[:, hs] * sin_b).astype(bf16)
        kvn = _rms_fwd(kvl_ref[...], gkv_ref[l:l + 1, :]).astype(bf16)
        kvn_ref[...] = kvn
        kr = kr_ref[...]
        kpe = kr * cos_b + _swap_rope_halves(kr) * sin_b
        k = _dot(kvn, wk_ref[...])
        for h in range(N_HEADS):
            hs = slice(h * HEAD_PAD, (h + 1) * HEAD_PAD)
            k_ref[:, hs] = (k[:, hs] + kpe).astype(bf16)
        v_ref[...] = _dot(kvn, wv_ref[...]).astype(bf16)

    return pl.pallas_call(
        body, name="mla_proj_fwd", grid=(t // tb,),
        in_specs=[_rows(tb, Q_LORA), _rows(tb, KV_LORA), _rows(tb, HEAD_PAD), _rows(tb, HEAD_PAD), _rows(tb, HEAD_PAD),
                  _const(gq.shape), _const((Q_LORA, hw)), _const(gkv.shape), _const((KV_LORA, hw)), _const((KV_LORA, hw))],
        out_specs=(_rows(tb, hw), _rows(tb, hw), _rows(tb, hw), _rows(tb, Q_LORA), _rows(tb, KV_LORA)),
        out_shape=(jax.ShapeDtypeStruct((t, hw), bf16), jax.ShapeDtypeStruct((t, hw), bf16), jax.ShapeDtypeStruct((t, hw), bf16),
                   jax.ShapeDtypeStruct((t, Q_LORA), bf16), jax.ShapeDtypeStruct((t, KV_LORA), bf16)),
        compiler_params=_cparams(1),
    )(qlat, kvlat, krope, cos_t, sin_t, gq, wuq_p, gkv, wk_p, wv_p)


def _diag_mask(tq):
    return lax.broadcasted_iota(jnp.int32, (tq, tq), 0) >= lax.broadcasted_iota(jnp.int32, (tq, tq), 1)


def _attn_fwd(q, k, v):
    t = q.shape[0]
    tq = _tb(t)

    def body(q_ref, k_ref, v_ref, o_ref, lse_ref):
        qi = pl.program_id(1)
        for i in range(t // tq):
            @pl.when(qi == i)
            def _(i=i):
                qv = q_ref[...]
                lo, hi = i * tq, (i + 1) * tq
                s_d = jnp.where(_diag_mask(tq), _dot_nt(qv, k_ref[lo:hi, :]) * ATT_SCALE, NEG)
                m = jnp.max(s_d, axis=-1, keepdims=True)
                if i > 0:
                    s_o = _dot_nt(qv, k_ref[0:lo, :]) * ATT_SCALE
                    m = jnp.maximum(m, jnp.max(s_o, axis=-1, keepdims=True))
                p_d = jnp.exp(s_d - m)
                lsum = jnp.sum(p_d, axis=-1, keepdims=True)
                acc = _dot(p_d.astype(bf16), v_ref[lo:hi, :])
                if i > 0:
                    p_o = jnp.exp(s_o - m)
                    lsum = lsum + jnp.sum(p_o, axis=-1, keepdims=True)
                    acc = acc + _dot(p_o.astype(bf16), v_ref[0:lo, :])
                o_ref[...] = (acc / lsum).astype(bf16)
                lse_ref[...] = m + jnp.log(lsum)

    return pl.pallas_call(
        body, name="attn_fwd", grid=(N_HEADS, t // tq),
        in_specs=[pl.BlockSpec((tq, HEAD_PAD), lambda h, i: (i, h)), pl.BlockSpec((t, HEAD_PAD), lambda h, i: (0, h)),
                  pl.BlockSpec((t, HEAD_PAD), lambda h, i: (0, h))],
        out_specs=(pl.BlockSpec((tq, HEAD_PAD), lambda h, i: (i, h)), pl.BlockSpec((None, tq, 1), lambda h, i: (h, i, 0))),
        out_shape=(jax.ShapeDtypeStruct((t, N_HEADS * HEAD_PAD), bf16), jax.ShapeDtypeStruct((N_HEADS, t, 1), f32)),
        compiler_params=_cparams(2),
    )(q, k, v)


def _merge_out_fwd(l, x, s, um, o, gates, conv_out_w, sg_out_w, wo_p, w_out, g2):
    t, d = x.shape
    tb = _tb(t)

    def body(x_ref, s_ref, um_ref, o_ref, gt_ref, wa_ref, wb_ref, wc_ref, wout_ref, g_ref, x1_ref, mg_ref, om_ref):
        merged = (jax.nn.sigmoid(gt_ref[:, 0:d]) * _dot(s_ref[...], wa_ref[...])
                  + jax.nn.sigmoid(gt_ref[:, d:2 * d]) * _dot(um_ref[...], wb_ref[...])
                  + jax.nn.sigmoid(gt_ref[:, 2 * d:3 * d]) * _dot(o_ref[...], wc_ref[...]))
        mb = merged.astype(bf16)
        mg_ref[...] = mb
        om = _dot(mb, wout_ref[...])
        om_ref[...] = om
        x1_ref[...] = x_ref[...] + _rms_fwd(om, g_ref[l:l + 1, :])

    hw = N_HEADS * HEAD_PAD
    return pl.pallas_call(
        body, name="merge_out_fwd", grid=(t // tb,),
        in_specs=[_rows(tb, d), _rows(tb, D_CONV), _rows(tb, D_SG), _rows(tb, hw), _rows(tb, 3 * d),
                  _const((D_CONV, d)), _const((D_SG, d)), _const((hw, d)), _const((d, d)), _const(g2.shape)],
        out_specs=(_rows(tb, d), _rows(tb, d), _rows(tb, d)),
        out_shape=(jax.ShapeDtypeStruct((t, d), f32), jax.ShapeDtypeStruct((t, d), bf16), jax.ShapeDtypeStruct((t, d), f32)),
        compiler_params=_cparams(1),
    )(x, s, um, o, gates, conv_out_w, sg_out_w, wo_p, w_out, g2)


FF_CHUNK = 1408


def _ffn_conv_cols(zbuf, w_ref, b_ref, l, nrows, c0, c1):
    acc = b_ref[l:l + 1, c0:c1] + w_ref[0:1, c0:c1] * zbuf[pl.ds(FFN_HALO - 2, nrows), c0:c1]
    acc = acc + w_ref[1:2, c0:c1] * zbuf[pl.ds(FFN_HALO - 1, nrows), c0:c1]
    return acc + w_ref[2:3, c0:c1] * zbuf[pl.ds(FFN_HALO, nrows), c0:c1]


def _ffn_fwd(l, x1, g3, w_up, dw_w, dw_b, w_down, g4):
    t, d = x1.shape
    tb = _tb(t)
    ff2 = 2 * D_FF

    def body(x_ref, g3_ref, wup_ref, dww_ref, dwb_ref, wdn_ref, g4_ref, x2_ref, h2_ref, z_ref, zc_ref, act_ref, f_ref, zbuf):
        @pl.when(pl.program_id(0) == 0)
        def _():
            zbuf[0:FFN_HALO, :] = jnp.zeros((FFN_HALO, ff2), f32)

        xv = x_ref[...]
        h2 = _rms_fwd(xv, g3_ref[l:l + 1, :]).astype(bf16)
        h2_ref[...] = h2
        for c0 in range(0, ff2, FF_CHUNK):
            zv = _dot_nt(h2, wup_ref[c0:c0 + FF_CHUNK, :])
            z_ref[:, c0:c0 + FF_CHUNK] = zv.astype(bf16)
            zbuf[FFN_HALO:FFN_HALO + tb, c0:c0 + FF_CHUNK] = zv
        facc = jnp.zeros((tb, d), f32)
        for c0 in range(0, D_FF, FF_CHUNK):
            gg = _ffn_conv_cols(zbuf, dww_ref, dwb_ref, l, tb, c0, c0 + FF_CHUNK)
            vv = _ffn_conv_cols(zbuf, dww_ref, dwb_ref, l, tb, D_FF + c0, D_FF + c0 + FF_CHUNK)
            zc_ref[:, c0:c0 + FF_CHUNK] = gg.astype(bf16)
            zc_ref[:, D_FF + c0:D_FF + c0 + FF_CHUNK] = vv.astype(bf16)
            a = (_gelu(gg) * vv).astype(bf16)
            act_ref[:, c0:c0 + FF_CHUNK] = a
            facc = facc + _dot(a, wdn_ref[c0:c0 + FF_CHUNK, :])
        f_ref[...] = facc
        x2_ref[...] = xv + _rms_fwd(facc, g4_ref[l:l + 1, :])
        zbuf[0:FFN_HALO, :] = zbuf[tb:tb + FFN_HALO, :]

    return pl.pallas_call(
        body, name="ffn_fwd", grid=(t // tb,),
        in_specs=[_rows(tb, d), _const(g3.shape), _const((ff2, d)), _const((FFN_HALO, ff2)), _const(dw_b.shape), _const((D_FF, d)),
                  _const(g4.shape)],
        out_specs=(_rows(tb, d), _rows(tb, d), _rows(tb, ff2), _rows(tb, ff2), _rows(tb, D_FF), _rows(tb, d)),
        out_shape=(jax.ShapeDtypeStruct((t, d), f32), jax.ShapeDtypeStruct((t, d), bf16), jax.ShapeDtypeStruct((t, ff2), bf16),
                   jax.ShapeDtypeStruct((t, ff2), bf16), jax.ShapeDtypeStruct((t, D_FF), bf16), jax.ShapeDtypeStruct((t, d), f32)),
        scratch_shapes=[pltpu.VMEM((tb + FFN_HALO, ff2), f32)],
        compiler_params=_cparams(1),
    )(x1, g3, w_up, dw_w, dw_b, w_down, g4)


def _loss_fwd_bwd(y, target):
    t, d = y.shape
    tb = _tb(t)

    def body(y_ref, t_ref, dy_ref, loss_ref):
        @pl.when(pl.program_id(0) == 0)
        def _():
            loss_ref[...] = jnp.zeros((1, LANES), f32)

        e = y_ref[...] - t_ref[...]
        dy_ref[...] = e * (1.0 / d)
        loss_ref[...] += 0.5 * jnp.sum(_mean(e * e))

    dy, loss = pl.pallas_call(
        body, name="loss", grid=(t // tb,),
        in_specs=[_rows(tb, d), _rows(tb, d)],
        out_specs=(_rows(tb, d), _acc((1, LANES))),
        out_shape=(jax.ShapeDtypeStruct((t, d), f32), jax.ShapeDtypeStruct((1, LANES), f32)),
        compiler_params=_cparams(1),
    )(y, target)
    return dy, loss[0, 0]


def _ffn_bwd(l, dx2, x1, f, z, zc, w_up, dw_w, w_down, g3, g4, deps=()):
    t, d = x1.shape
    tb = min(128, t)
    nb = t // tb
    ff2 = 2 * D_FF
    hrows = 16
    per_h = tb // hrows

    def body(*refs):
        (dx2_ref, x1_ref, f_ref, z_ref, zp_ref, zc_ref, wup_ref, dww_ref, wdn_ref, g3_ref, g4_ref,
         dx1_ref, df_ref, dz_ref, dg4_ref, dg3_ref, ddwb_ref, ddww_ref, zbuf, dzc) = refs[len(deps):]
        i = pl.program_id(0)
        b = nb - 1 - i

        @pl.when(i == 0)
        def _():
            dg4_ref[...] = jnp.zeros_like(dg4_ref)
            dg3_ref[...] = jnp.zeros_like(dg3_ref)
            ddwb_ref[...] = jnp.zeros_like(ddwb_ref)
            ddww_ref[...] = jnp.zeros_like(ddww_ref)
            dzc[tb:tb + FFN_HALO, :] = jnp.zeros((FFN_HALO, ff2), f32)

        dout = dx2_ref[...]
        df, dg4 = _rms_bwd(f_ref[...], g4_ref[l:l + 1, :], dout)
        dg4_ref[...] += dg4
        dfb = df.astype(bf16)
        df_ref[...] = dfb
        zbuf[0:FFN_HALO, :] = jnp.where(b > 0, zp_ref[hrows - FFN_HALO:hrows, :].astype(f32), 0.0)
        zbuf[FFN_HALO:FFN_HALO + tb, :] = z_ref[...].astype(f32)
        for c0 in range(0, D_FF, FF_CHUNK):
            dact = _dot_nt(dfb, wdn_ref[c0:c0 + FF_CHUNK, :])
            gel, dgel = _gelu_and_grad(zc_ref[:, c0:c0 + FF_CHUNK].astype(f32))
            dzc[0:tb, c0:c0 + FF_CHUNK] = dact * zc_ref[:, D_FF + c0:D_FF + c0 + FF_CHUNK].astype(f32) * dgel
            dzc[0:tb, D_FF + c0:D_FF + c0 + FF_CHUNK] = dact * gel
        dh2 = jnp.zeros((tb, d), f32)
        for c0 in range(0, ff2, FF_CHUNK):
            cs = slice(c0, c0 + FF_CHUNK)
            d0 = dzc[0:tb, cs]
            ddwb_ref[:, cs] += _colsum(d0)
            for j in range(FFN_K):
                ddww_ref[j:j + 1, cs] += _colsum(d0 * zbuf[pl.ds(FFN_HALO - 2 + j, tb), cs])
            dzv = dww_ref[2:3, cs] * d0 + dww_ref[1:2, cs] * dzc[pl.ds(1, tb), cs] + dww_ref[0:1, cs] * dzc[pl.ds(2, tb), cs]
            dzb = dzv.astype(bf16)
            dz_ref[:, cs] = dzb
            dh2 = dh2 + _dot(dzb, wup_ref[cs, :])
        dzc[tb:tb + FFN_HALO, :] = dzc[0:FFN_HALO, :]
        dxn, dg3 = _rms_bwd(x1_ref[...], g3_ref[l:l + 1, :], dh2)
        dg3_ref[...] += dg3
        dx1_ref[...] = dout + dxn

    return pl.pallas_call(
        body, name="ffn_bwd", grid=(nb,),
        in_specs=[_ANY] * len(deps) + [_rows_rev(tb, d, nb), _rows_rev(tb, d, nb), _rows_rev(tb, d, nb), _rows_rev(tb, ff2, nb),
                  pl.BlockSpec((hrows, ff2), lambda i: (jnp.maximum((nb - 1 - i) * per_h - 1, 0), 0)), _rows_rev(tb, ff2, nb),
                  _const((ff2, d)), _const((FFN_HALO, ff2)), _const((D_FF, d)), _const(g3.shape), _const(g4.shape)],
        out_specs=(_rows_rev(tb, d, nb), _rows_rev(tb, d, nb), _rows_rev(tb, ff2, nb), _acc((1, d)), _acc((1, d)), _acc((1, ff2)),
                   _acc((FFN_HALO, ff2))),
        out_shape=(jax.ShapeDtypeStruct((t, d), f32), jax.ShapeDtypeStruct((t, d), bf16), jax.ShapeDtypeStruct((t, ff2), bf16),
                   jax.ShapeDtypeStruct((1, d), f32), jax.ShapeDtypeStruct((1, d), f32), jax.ShapeDtypeStruct((1, ff2), f32),
                   jax.ShapeDtypeStruct((FFN_HALO, ff2), f32)),
        scratch_shapes=[pltpu.VMEM((tb + FFN_HALO, ff2), f32), pltpu.VMEM((tb + FFN_HALO, ff2), f32)],
        compiler_params=_cparams(1),
    )(*deps, dx2, x1, f, z, z, zc, w_up, dw_w, w_down, g3, g4)


def _merge_out_bwd(l, dx1, om, s, um, o, gates, conv_out_w, sg_out_w, wo_p, w_out, g2, deps=()):
    t, d = dx1.shape
    tb = _tb(t)
    hw = N_HEADS * HEAD_PAD

    def body(*refs):
        (dx_ref, om_ref, s_ref, um_ref, o_ref, gt_ref, wa_ref, wb_ref, wc_ref, wout_ref, g_ref,
         dom_ref, dgt_ref, dya_ref, dyb_ref, dyc_ref, ds_ref, dum_ref, do_ref, dg2_ref) = refs[len(deps):]

        @pl.when(pl.program_id(0) == 0)
        def _():
            dg2_ref[...] = jnp.zeros_like(dg2_ref)

        dom, dg2 = _rms_bwd(om_ref[...], g_ref[l:l + 1, :], dx_ref[...])
        dg2_ref[...] += dg2
        domb = dom.astype(bf16)
        dom_ref[...] = domb
        dmerged = _dot_nt(domb, wout_ref[...])
        branches = ((s_ref, wa_ref, dya_ref, ds_ref), (um_ref, wb_ref, dyb_ref, dum_ref), (o_ref, wc_ref, dyc_ref, do_ref))
        for br, (in_ref, w_ref, dy_ref, din_ref) in enumerate(branches):
            yv = _dot(in_ref[...], w_ref[...])
            sg = jax.nn.sigmoid(gt_ref[:, br * d:(br + 1) * d])
            dyb = (dmerged * sg).astype(bf16)
            dy_ref[...] = dyb
            dgt_ref[:, br * d:(br + 1) * d] = (dmerged * yv * sg * (1.0 - sg)).astype(bf16)
            din_ref[...] = _dot_nt(dyb, w_ref[...]).astype(din_ref.dtype)

    return pl.pallas_call(
        body, name="merge_out_bwd", grid=(t // tb,),
        in_specs=[_ANY] * len(deps) + [_rows(tb, d), _rows(tb, d), _rows(tb, D_CONV), _rows(tb, D_SG), _rows(tb, hw), _rows(tb, 3 * d),
                  _const((D_CONV, d)), _const((D_SG, d)), _const((hw, d)), _const((d, d)), _const(g2.shape)],
        out_specs=(_rows(tb, d), _rows(tb, 3 * d), _rows(tb, d), _rows(tb, d), _rows(tb, d), _rows(tb, D_CONV), _rows(tb, D_SG),
                   _rows(tb, hw), _acc((1, d))),
        out_shape=(jax.ShapeDtypeStruct((t, d), bf16), jax.ShapeDtypeStruct((t, 3 * d), bf16), jax.ShapeDtypeStruct((t, d), bf16),
                   jax.ShapeDtypeStruct((t, d), bf16), jax.ShapeDtypeStruct((t, d), bf16), jax.ShapeDtypeStruct((t, D_CONV), f32),
                   jax.ShapeDtypeStruct((t, D_SG), f32), jax.ShapeDtypeStruct((t, hw), bf16), jax.ShapeDtypeStruct((1, d), f32)),
        compiler_params=_cparams(1),
    )(*deps, dx1, om, s, um, o, gates, conv_out_w, sg_out_w, wo_p, w_out, g2)


def _attn_bwd(q, k, v, o, lse, do):
    t = q.shape[0]
    tq = _tb(t)
    hw = N_HEADS * HEAD_PAD

    def body(q_ref, k_ref, v_ref, o_ref, lse_ref, do_ref, dq_ref, dk_ref, dv_ref):
        qi = pl.program_id(1)

        @pl.when(qi == 0)
        def _():
            dk_ref[...] = jnp.zeros_like(dk_ref)
            dv_ref[...] = jnp.zeros_like(dv_ref)

        def keys(lo, hi, qv, dov, lse, delta, diagonal):
            kj, vj = k_ref[lo:hi, :], v_ref[lo:hi, :]
            p = jnp.exp(_dot_nt(qv, kj) * ATT_SCALE - lse)
            if diagonal:
                p = jnp.where(_diag_mask(tq), p, 0.0)
            ds = (p * (_dot_nt(dov, vj) - delta) * ATT_SCALE).astype(bf16)
            dk_ref[lo:hi, :] += _dot_tn(ds, qv)
            dv_ref[lo:hi, :] += _dot_tn(p.astype(bf16), dov)
            return _dot(ds, kj)

        for i in range(t // tq):
            @pl.when(qi == i)
            def _(i=i):
                qv, dov, lse = q_ref[...], do_ref[...], lse_ref[...]
                delta = jnp.sum(dov.astype(f32) * o_ref[...].astype(f32), axis=-1, keepdims=True)
                dq = keys(i * tq, (i + 1) * tq, qv, dov, lse, delta, True)
                if i > 0:
                    dq = dq + keys(0, i * tq, qv, dov, lse, delta, False)
                dq_ref[...] = dq

    blk_q = pl.BlockSpec((tq, HEAD_PAD), lambda h, i: (i, h))
    blk_kv = pl.BlockSpec((t, HEAD_PAD), lambda h, i: (0, h))
    return pl.pallas_call(
        body, name="attn_bwd", grid=(N_HEADS, t // tq),
        in_specs=[blk_q, blk_kv, blk_kv, blk_q, pl.BlockSpec((None, tq, 1), lambda h, i: (h, i, 0)), blk_q],
        out_specs=(blk_q, blk_kv, blk_kv),
        out_shape=(jax.ShapeDtypeStruct((t, hw), f32), jax.ShapeDtypeStruct((t, hw), f32), jax.ShapeDtypeStruct((t, hw), f32)),
        compiler_params=_cparams(2),
    )(q, k, v, o, lse, do)


def _mla_proj_bwd(l, dq, dk, dv, qlat, kvlat, cos_t, sin_t, gq, wuq_p, gkv, wk_p, wv_p):
    t = qlat.shape[0]
    tb = _tb(t)
    hw = N_HEADS * HEAD_PAD

    def body(dq_ref, dk_ref, dv_ref, ql_ref, kvl_ref, c_ref, s_ref, gq_ref, wq_ref, gkv_ref, wk_ref, wv_ref,
             dqb_ref, dkb_ref, dvb_ref, dql_ref, dkvl_ref, dkr_ref, dgq_ref, dgkv_ref):
        @pl.when(pl.program_id(0) == 0)
        def _():
            dgq_ref[...] = jnp.zeros_like(dgq_ref)
            dgkv_ref[...] = jnp.zeros_like(dgkv_ref)

        cos_b, sin_b = c_ref[...], s_ref[...]
        for h in range(N_HEADS):
            hs = slice(h * HEAD_PAD, (h + 1) * HEAD_PAD)
            dqh = dq_ref[:, hs]
            dqb_ref[:, hs] = (dqh * cos_b + _swap_rope_halves(dqh * sin_b)).astype(bf16)
        dqn = _dot_nt(dqb_ref[...], wq_ref[...])
        dql, dgq = _rms_bwd(ql_ref[...], gq_ref[l:l + 1, :], dqn)
        dgq_ref[...] += dgq
        dql_ref[...] = dql.astype(bf16)
        dkv_full = dk_ref[...]
        dkb = dkv_full.astype(bf16)
        dkb_ref[...] = dkb
        dkpe = dkv_full[:, 0:HEAD_PAD]
        for h in range(1, N_HEADS):
            dkpe = dkpe + dkv_full[:, h * HEAD_PAD:(h + 1) * HEAD_PAD]
        dkr_ref[...] = (dkpe * cos_b + _swap_rope_halves(dkpe * sin_b)).astype(bf16)
        dvb = dv_ref[...].astype(bf16)
        dvb_ref[...] = dvb
        dkvn = _dot_nt(dkb, wk_ref[...]) + _dot_nt(dvb, wv_ref[...])
        dkvl, dgkv = _rms_bwd(kvl_ref[...], gkv_ref[l:l + 1, :], dkvn)
        dgkv_ref[...] += dgkv
        dkvl_ref[...] = dkvl.astype(bf16)

    return pl.pallas_call(
        body, name="mla_proj_bwd", grid=(t // tb,),
        in_specs=[_rows(tb, hw), _rows(tb, hw), _rows(tb, hw), _rows(tb, Q_LORA), _rows(tb, KV_LORA), _rows(tb, HEAD_PAD),
                  _rows(tb, HEAD_PAD), _const(gq.shape), _const((Q_LORA, hw)), _const(gkv.shape), _const((KV_LORA, hw)),
                  _const((KV_LORA, hw))],
        out_specs=(_rows(tb, hw), _rows(tb, hw), _rows(tb, hw), _rows(tb, Q_LORA), _rows(tb, KV_LORA), _rows(tb, HEAD_PAD),
                   _acc((1, Q_LORA)), _acc((1, KV_LORA))),
        out_shape=(jax.ShapeDtypeStruct((t, hw), bf16), jax.ShapeDtypeStruct((t, hw), bf16), jax.ShapeDtypeStruct((t, hw), bf16),
                   jax.ShapeDtypeStruct((t, Q_LORA), bf16), jax.ShapeDtypeStruct((t, KV_LORA), bf16),
                   jax.ShapeDtypeStruct((t, HEAD_PAD), bf16), jax.ShapeDtypeStruct((1, Q_LORA), f32),
                   jax.ShapeDtypeStruct((1, KV_LORA), f32)),
        compiler_params=_cparams(1),
    )(dq, dk, dv, qlat, kvlat, cos_t, sin_t, gq, wuq_p, gkv, wk_p, wv_p)


def _sgu_bwd(l, b_in, dum, ln_g, ln_b, sg_w, bexp):
    t = b_in.shape[0]
    tb = _tb(t)
    gw = D_SG // SG_GROUPS

    def body(b_ref, dum_ref, g_ref, be_ref, w_ref, bexp_ref, db_ref, dw_ref, dsgb_ref, dlg_ref, dlb_ref, dvn_s):
        @pl.when(pl.program_id(0) == 0)
        def _():
            dw_ref[...] = jnp.zeros_like(dw_ref)
            dsgb_ref[...] = jnp.zeros_like(dsgb_ref)
            dlg_ref[...] = jnp.zeros_like(dlg_ref)
            dlb_ref[...] = jnp.zeros_like(dlb_ref)

        gl, dgl = _gelu_and_grad(b_ref[...])
        u = gl[:, :D_SG]
        vh, rstd = _ln_stats(gl[:, D_SG:])
        ln_gain = g_ref[l:l + 1, :]
        vn = (vh * ln_gain + be_ref[l:l + 1, :]).astype(bf16)
        dumv = dum_ref[...]
        tri = _tril_mask()
        ones = jnp.ones((FFN_HALO, gw), f32)
        for g in range(SG_GROUPS):
            wg = jnp.where(tri, w_ref[l, g], 0.0).astype(bf16)
            cs = slice(g * gw, (g + 1) * gw)
            for r0 in range(0, tb, SG_CHUNK):
                rs = slice(r0, r0 + SG_CHUNK)
                vblk = vn[rs, cs]
                mixed = _dot(wg, vblk) + bexp_ref[:, cs]
                db_ref[rs, cs] = (dumv[rs, cs] * mixed * dgl[rs, cs]).astype(bf16)
                dmix = dumv[rs, cs] * u[rs, cs]
                dmb = dmix.astype(bf16)
                dw_ref[g] += jnp.where(tri, _dot_nt(dmb, vblk), 0.0)
                rowsum = lax.dot_general(ones, dmix, (((1,), (1,)), ((), ())), preferred_element_type=f32,
                                         precision=lax.Precision.HIGHEST)
                dsgb_ref[g:g + 1, :] += rowsum[0:1, :]
                dvn_s[rs, cs] = _dot_tn(wg, dmb)
        dvn = dvn_s[...]
        dlg_ref[...] += _colsum(dvn * vh)
        dlb_ref[...] += _colsum(dvn)
        db_ref[:, D_SG:] = (_ln_bwd(vh, rstd, ln_gain, dvn) * dgl[:, D_SG:]).astype(bf16)

    return pl.pallas_call(
        body, name="sgu_bwd", grid=(t // tb,),
        in_specs=[_rows(tb, 2 * D_SG), _rows(tb, D_SG), _const(ln_g.shape), _const(ln_b.shape), _const(sg_w.shape),
                  _const((SG_CHUNK, D_SG))],
        out_specs=(_rows(tb, 2 * D_SG), _acc((SG_GROUPS, SG_CHUNK, SG_CHUNK)), _acc((FFN_HALO, SG_CHUNK)), _acc((1, D_SG)),
                   _acc((1, D_SG))),
        out_shape=(jax.ShapeDtypeStruct((t, 2 * D_SG), bf16), jax.ShapeDtypeStruct((SG_GROUPS, SG_CHUNK, SG_CHUNK), f32),
                   jax.ShapeDtypeStruct((FFN_HALO, SG_CHUNK), f32), jax.ShapeDtypeStruct((1, D_SG), f32),
                   jax.ShapeDtypeStruct((1, D_SG), f32)),
        scratch_shapes=[pltpu.VMEM((tb, D_SG), f32)],
        compiler_params=_cparams(1),
    )(b_in, dum, ln_g, ln_b, sg_w, bexp)


def _conv_bwd(l, a_in, z1, ds, dw_w, ln_g, ln_b):
    t = a_in.shape[0]
    tb = _tb(t)
    nb = t // tb
    per_halo = tb // CONV_HALO

    def body(a_ref, ap_ref, z1_ref, ds_ref, w_ref, g_ref, be_ref, da_ref, ddww_ref, ddwb_ref, dlg_ref, dlb_ref, win, dzb):
        i = pl.program_id(0)
        b = nb - 1 - i

        @pl.when(i == 0)
        def _():
            ddww_ref[...] = jnp.zeros_like(ddww_ref)
            ddwb_ref[...] = jnp.zeros_like(ddwb_ref)
            dlg_ref[...] = jnp.zeros_like(dlg_ref)
            dlb_ref[...] = jnp.zeros_like(dlb_ref)
            dzb[tb:tb + CONV_HALO, :] = jnp.zeros((CONV_HALO, D_CONV), f32)

        a = a_ref[...]
        val = a[:, :D_CONV]
        sg = jax.nn.sigmoid(a[:, D_CONV:])
        ap = ap_ref[...]
        win[0:CONV_HALO, :] = jnp.where(b > 0, ap[:, :D_CONV] * jax.nn.sigmoid(ap[:, D_CONV:]), 0.0)
        win[CONV_HALO:CONV_HALO + tb, :] = val * sg
        zh, rstd = _ln_stats(z1_ref[...])
        ln_gain = g_ref[l:l + 1, :]
        zl = zh * ln_gain + be_ref[l:l + 1, :]
        sgl = jax.nn.sigmoid(zl)
        dzl = ds_ref[...] * (sgl * (1.0 + zl * (1.0 - sgl)))
        dlg_ref[...] += _colsum(dzl * zh)
        dlb_ref[...] += _colsum(dzl)
        dz1 = _ln_bwd(zh, rstd, ln_gain, dzl)
        dzb[0:tb, :] = dz1
        ddwb_ref[...] += _colsum(dz1)
        dgate_f = val * sg * (1.0 - sg)
        for c0 in range(0, D_CONV, LANES):
            cs = slice(c0, c0 + LANES)
            for r0 in range(0, tb, LANES):
                d1 = dzb[r0:r0 + LANES, cs]
                acc = jnp.zeros((LANES, LANES), f32)
                for j in range(CONV_K):
                    ddww_ref[j:j + 1, cs] += _colsum(d1 * win[pl.ds(r0 + CONV_HALO - CONV_K + 1 + j, LANES), cs])
                    acc = acc + w_ref[j:j + 1, cs] * dzb[pl.ds(r0 + CONV_K - 1 - j, LANES), cs]
                da_ref[r0:r0 + LANES, cs] = (acc * sg[r0:r0 + LANES, cs]).astype(bf16)
                da_ref[r0:r0 + LANES, c0 + D_CONV:c0 + D_CONV + LANES] = (acc * dgate_f[r0:r0 + LANES, cs]).astype(bf16)
        dzb[tb:tb + CONV_HALO, :] = dzb[0:CONV_HALO, :]

    return pl.pallas_call(
        body, name="conv_bwd", grid=(nb,),
        in_specs=[_rows_rev(tb, 2 * D_CONV, nb),
                  pl.BlockSpec((CONV_HALO, 2 * D_CONV), lambda i: (jnp.maximum((nb - 1 - i) * per_halo - 1, 0), 0)),
                  _rows_rev(tb, D_CONV, nb), _rows_rev(tb, D_CONV, nb), _const((CONV_HALO, D_CONV)), _const(ln_g.shape),
                  _const(ln_b.shape)],
        out_specs=(_rows_rev(tb, 2 * D_CONV, nb), _acc((CONV_HALO, D_CONV)), _acc((1, D_CONV)), _acc((1, D_CONV)), _acc((1, D_CONV))),
        out_shape=(jax.ShapeDtypeStruct((t, 2 * D_CONV), bf16), jax.ShapeDtypeStruct((CONV_HALO, D_CONV), f32),
                   jax.ShapeDtypeStruct((1, D_CONV), f32), jax.ShapeDtypeStruct((1, D_CONV), f32), jax.ShapeDtypeStruct((1, D_CONV), f32)),
        scratch_shapes=[pltpu.VMEM((tb + CONV_HALO, D_CONV), f32), pltpu.VMEM((tb + CONV_HALO, D_CONV), f32)],
        compiler_params=_cparams(1),
    )(a_in, a_in, z1, ds, dw_w, ln_g, ln_b)


def _mix_in_bwd(l, x, g1, dxres, dsegs, w_in_p, deps=()):
    t, d = x.shape
    tb = min(512, t)

    def body(*refs):
        x_ref, g_ref, dr_ref = refs[len(deps):len(deps) + 3]
        rest = refs[len(deps) + 3:]
        dseg_refs, w_ref, dx_ref, dg_ref = rest[:len(SEGS)], rest[len(SEGS)], rest[len(SEGS) + 1], rest[len(SEGS) + 2]

        @pl.when(pl.program_id(0) == 0)
        def _():
            dg_ref[...] = jnp.zeros_like(dg_ref)

        dh = jnp.zeros((tb, d), f32)
        for (off, wd), ds_ref in zip(SEGS, dseg_refs):
            dh = dh + _dot_nt(ds_ref[...], w_ref[:, off:off + wd])
        dxn, dg = _rms_bwd(x_ref[...], g_ref[l:l + 1, :], dh)
        dg_ref[...] += dg
        dx_ref[...] = dr_ref[...] + dxn

    return pl.pallas_call(
        body, name="mix_in_bwd", grid=(t // tb,),
        in_specs=([_ANY] * len(deps) + [_rows(tb, d), _const(g1.shape), _rows(tb, d)] + [_rows(tb, wd) for _, wd in SEGS]
                  + [_const((d, D_IN_PAD))]),
        out_specs=(_rows(tb, d), _acc((1, d))),
        out_shape=(jax.ShapeDtypeStruct((t, d), f32), jax.ShapeDtypeStruct((1, d), f32)),
        compiler_params=_cparams(1),
    )(*deps, x, g1, dxres, *dsegs, w_in_p)


def _pick_block(n, cap=512):
    for b in (cap, 384, 256, 128):
        if b <= cap and n % b == 0:
            return b
    return n


def _wgrad(a, b, name, out_dtype=f32):
    t, kdim = a.shape
    n = b.shape[1]
    bk, bn = _pick_block(kdim), _pick_block(n, cap=1024)

    def body(a_ref, b_ref, o_ref):
        o_ref[...] = _dot_tn(a_ref[...], b_ref[...]).astype(out_dtype)

    return pl.pallas_call(
        body, name=name, grid=(kdim // bk, n // bn),
        in_specs=[pl.BlockSpec((t, bk), lambda i, j: (0, i)), pl.BlockSpec((t, bn), lambda i, j: (0, j))],
        out_specs=pl.BlockSpec((bk, bn), lambda i, j: (i, j)),
        out_shape=jax.ShapeDtypeStruct((kdim, n), out_dtype),
        compiler_params=_cparams(2),
    )(a, b)


def _wgrad_multi(a, bs, name):
    t, kdim = a.shape
    bk = _pick_block(kdim)
    nb = len(bs)

    def body(a_ref, *refs):
        at = a_ref[...].T
        for b_ref, o_ref in zip(refs[:nb], refs[nb:]):
            o_ref[...] = _dot(at, b_ref[...])

    return pl.pallas_call(
        body, name=name, grid=(kdim // bk,),
        in_specs=[pl.BlockSpec((t, bk), lambda i: (0, i))] + [_const(b.shape) for b in bs],
        out_specs=tuple(pl.BlockSpec((bk, b.shape[1]), lambda i: (i, 0)) for b in bs),
        out_shape=tuple(jax.ShapeDtypeStruct((kdim, b.shape[1]), f32) for b in bs),
        compiler_params=_cparams(1),
    )(a, *bs)


_BC1 = 1.0 - ADAM_B1 ** ADAM_STEP
_BC2 = 1.0 - ADAM_B2 ** ADAM_STEP


def _adam_math(g, w, m, v):
    nm = ADAM_B1 * m + (1.0 - ADAM_B1) * g
    nv = ADAM_B2 * v + (1.0 - ADAM_B2) * (g * g)
    delta = -ADAM_LR * ((nm / _BC1) / (jnp.sqrt(nv / _BC2) + ADAM_EPS) + ADAM_WD * w)
    return delta, nm, nv


def _slot_sum(r_ref, index=()):
    g = r_ref[(0,) + index].astype(f32)
    for s in range(1, N_DEV):
        g = g + r_ref[(s,) + index].astype(f32)
    return g


def _adamw_shard(l, recv, w, m, v, prev, name, deps=()):
    _, k, ns = recv.shape
    rb = next((c for c in (256, 192, 176, 128) if k % c == 0), k)
    blk = pl.BlockSpec((None, rb, ns), lambda i: (l, i, 0))

    def body(r_ref, w_ref, m_ref, v_ref, *rest):
        g_ref, d_ref, nm_ref, nv_ref = rest[-4:]
        g = _slot_sum(r_ref)
        g_ref[...] = g
        d_ref[...], nm_ref[...], nv_ref[...] = _adam_math(g, w_ref[...], m_ref[...], v_ref[...])

    out = jax.ShapeDtypeStruct(w.shape, f32)
    n_prev = 0 if prev is None else 4
    return pl.pallas_call(
        body, name=name, grid=(k // rb,),
        in_specs=[pl.BlockSpec((N_DEV, rb, ns), lambda i: (0, i, 0)), blk, blk, blk] + [_ANY] * (n_prev + len(deps)),
        out_specs=(blk, blk, blk, blk), out_shape=(out, out, out, out),
        input_output_aliases={4 + j: j for j in range(n_prev)},
        compiler_params=_cparams(1),
    )(recv, w, m, v, *(prev or ()), *deps)


def _adamw_rep(recvs, ws, ms, vs):
    depth = len(recvs)
    nt = len(REP_VECS)

    def body(*refs):
        r_refs = refs[:depth]
        w_refs, m_refs, v_refs = (refs[depth + i * nt:depth + (i + 1) * nt] for i in range(3))
        outs = refs[depth + 3 * nt:]
        for ti, (_, row, width, nrows) in enumerate(REP_VECS):
            for l in range(depth):
                g = r_refs[l][0, row:row + nrows, 0:width]
                for s in range(1, N_DEV):
                    g = g + r_refs[l][s, row:row + nrows, 0:width]
                pick = (lambda ref: ref[l]) if nrows > 1 else (lambda ref: ref[l:l + 1, :])
                delta, nm, nv = _adam_math(g, pick(w_refs[ti]), pick(m_refs[ti]), pick(v_refs[ti]))
                for o_ref, val in zip(outs[4 * ti:4 * ti + 4], (g, delta, nm, nv)):
                    if nrows > 1:
                        o_ref[l] = val
                    else:
                        o_ref[l:l + 1, :] = val

    ins = tuple(recvs) + tuple(ws) + tuple(ms) + tuple(vs)
    out_shape = tuple(jax.ShapeDtypeStruct(w.shape, f32) for w in ws for _ in range(4))
    return pl.pallas_call(
        body, name="adamw_rep", grid=(1,),
        in_specs=[_whole(a) for a in ins], out_specs=tuple(_whole(o) for o in out_shape), out_shape=out_shape,
        compiler_params=_cparams(1),
    )(*ins)


def _adamw_sg_w(recvs, w, m, v):
    depth = len(recvs)

    def body(*refs):
        r_refs = refs[:depth]
        w_ref, m_ref, v_ref = refs[depth:depth + 3]
        outs = refs[depth + 3:]
        for l in range(depth):
            for gi in range(SG_GROUPS):
                g = _slot_sum(r_refs[l], (gi,))
                delta, nm, nv = _adam_math(g, w_ref[l, gi], m_ref[l, gi], v_ref[l, gi])
                for o_ref, val in zip(outs, (g, delta, nm, nv)):
                    o_ref[l, gi] = val

    ins = tuple(recvs) + (w, m, v)
    out = jax.ShapeDtypeStruct(w.shape, f32)
    return pl.pallas_call(
        body, name="adamw_sg_w", grid=(1,),
        in_specs=[_whole(a) for a in ins], out_specs=tuple(_whole(out) for _ in range(4)), out_shape=(out,) * 4,
        compiler_params=_cparams(1),
    )(*ins)


def _rope_tables(positions):
    t = positions.shape[0]
    inv = 10000.0 ** (-jnp.arange(0, QK_ROPE, 2, dtype=f32) / QK_ROPE)
    ang = positions.astype(f32)[:, None] * inv
    cos, sin = jnp.cos(ang), jnp.sin(ang)
    tail = jnp.zeros((t, HEAD_PAD - KR_LO - QK_ROPE), f32)
    cos_t = jnp.concatenate([jnp.ones((t, KR_LO), f32), cos, cos, tail], axis=1)
    sin_t = jnp.concatenate([jnp.zeros((t, KR_LO), f32), -sin, sin, tail], axis=1)
    return cos_t, sin_t


def _bias_over_channels(sg_b_l):
    return jnp.broadcast_to(sg_b_l.T[:, :, None], (SG_CHUNK, SG_GROUPS, D_SG // SG_GROUPS)).reshape(SG_CHUNK, D_SG)


def _mixer_weights(gathered, rep, l):
    conv_out_w, sg_out_w, wuq_p, wk_p, wv_p, wo_p, conv_dw_w, ffn_dw_w = _asm_small(
        gathered["conv_out_w"], gathered["sg_out_w"], gathered["mla_w_uq"], gathered["mla_w_ukv"], gathered["mla_w_o"],
        gathered["conv_dw_w"], gathered["ffn_dw_w"])
    g_out = gathered["w_out"]
    w = dict(rep)
    w.update(
        l=l, w_in_p=_asm_w_in(gathered["w_in"]),
        conv_out_w=conv_out_w, sg_out_w=sg_out_w, wuq_p=wuq_p, wk_p=wk_p, wv_p=wv_p, wo_p=wo_p, conv_dw_w_p=conv_dw_w, ffn_dw_w_p=ffn_dw_w,
        w_out=g_out.reshape(g_out.shape[0] * g_out.shape[1], g_out.shape[2]),
        bexp=_bias_over_channels(rep["sg_b"][l]))
    return w


def _ffn_weights(gathered):
    stack = lambda g: g.reshape(g.shape[0] * g.shape[1], g.shape[2])
    return dict(w_up=stack(gathered["ffn_w_up"]), w_down=stack(gathered["ffn_w_down"]))


def _mixer_fwd(x, w, cos_t, sin_t):
    l = w["l"]
    h, a_in, b_in, qlat, kvlat, krope, gates = _mix_in_fwd(l, x, w["mix_pre_g"], w["w_in_p"])
    s, z1 = _conv_fwd(l, a_in, w["conv_dw_w_p"], w["conv_dw_b"], w["conv_ln_g"], w["conv_ln_b"])
    um = _sgu_fwd(l, b_in, w["sg_ln_g"], w["sg_ln_b"], w["sg_w"], w["bexp"])
    q, k, v, qn, kvn = _mla_proj_fwd(l, qlat, kvlat, krope, cos_t, sin_t, w["mla_q_norm_g"], w["wuq_p"], w["mla_kv_norm_g"],
                                     w["wk_p"], w["wv_p"])
    o, lse = _attn_fwd(q, k, v)
    x1, merged, om = _merge_out_fwd(l, x, s, um, o, gates, w["conv_out_w"], w["sg_out_w"], w["wo_p"], w["w_out"], w["mix_post_g"])
    saved = dict(x=x, h=h, a_in=a_in, b_in=b_in, qlat=qlat, kvlat=kvlat, gates=gates, s=s, z1=z1, um=um, q=q, k=k, v=v, qn=qn,
                 kvn=kvn, o=o, lse=lse, x1=x1, merged=merged, om=om)
    return x1, saved


def _ffn_layer_fwd(x1, w):
    x2, h2, z, zc, act, f = _ffn_fwd(w["l"], x1, w["ffn_pre_g"], w["w_up"], w["ffn_dw_w_p"], w["ffn_dw_b"], w["w_down"], w["ffn_post_g"])
    return x2, dict(h2=h2, z=z, zc=zc, act=act, f=f)


def _ffn_layer_bwd(dx2, sv, w, deps=()):
    l = w["l"]
    vec = {}
    dx1, df, dz, vec["ffn_post_g"], vec["ffn_pre_g"], vec["ffn_dw_b"], vec["d_fdw"] = _ffn_bwd(
        l, dx2, sv["x1"], sv["f"], sv["z"], sv["zc"], w["w_up"], w["ffn_dw_w_p"], w["w_down"], w["ffn_pre_g"], w["ffn_post_g"],
        deps=deps)
    d_down = _wgrad(sv["act"], df, "wgrad_ffn_down", bf16)
    d_up_t = _wgrad(dz, sv["h2"], "wgrad_ffn_up", bf16)
    unstack = lambda g: g.reshape(N_DEV, g.shape[0] // N_DEV, g.shape[1])
    send = dict(ffn_w_up=unstack(d_up_t), ffn_w_down=unstack(d_down))
    return dx1, send, vec


def _mixer_bwd(dx1, sv, w, cos_t, sin_t, shard_cols, vec, start_small, deps=()):
    l = w["l"]
    d_fdw = vec.pop("d_fdw")
    dom, dgates, dya, dyb, dyc, ds, dum, do, vec["mix_post_g"] = _merge_out_bwd(
        l, dx1, sv["om"], sv["s"], sv["um"], sv["o"], sv["gates"], w["conv_out_w"], w["sg_out_w"], w["wo_p"], w["w_out"], w["mix_post_g"],
        deps=deps)
    d_out = _wgrad(sv["merged"], dom, "wgrad_w_out", bf16)
    d_co = _wgrad(sv["s"], dya, "wgrad_conv_out")
    d_so = _wgrad(sv["um"], dyb, "wgrad_sg_out")
    d_wo = _wgrad(sv["o"], dyc, "wgrad_w_o")

    dq, dk, dv = _attn_bwd(sv["q"], sv["k"], sv["v"], sv["o"], sv["lse"], do)
    dqb, dkb, dvb, dqlat, dkvlat, dkrope, vec["mla_q_norm_g"], vec["mla_kv_norm_g"] = _mla_proj_bwd(
        l, dq, dk, dv, sv["qlat"], sv["kvlat"], cos_t, sin_t, w["mla_q_norm_g"], w["wuq_p"], w["mla_kv_norm_g"], w["wk_p"], w["wv_p"])
    d_uq = _wgrad(sv["qn"], dqb, "wgrad_w_uq")
    d_uk, d_uv = _wgrad_multi(sv["kvn"], [dkb, dvb], "wgrad_w_ukv")

    db_in, dsg_w, dsgb, vec["sg_ln_g"], vec["sg_ln_b"] = _sgu_bwd(l, sv["b_in"], dum, w["sg_ln_g"], w["sg_ln_b"], w["sg_w"], w["bexp"])
    da_in, d_cdw, vec["conv_dw_b"], vec["conv_ln_g"], vec["conv_ln_b"] = _conv_bwd(
        l, sv["a_in"], sv["z1"], ds, w["conv_dw_w_p"], w["conv_ln_g"], w["conv_ln_b"])

    send = {}
    (send["conv_out_w"], send["sg_out_w"], send["mla_w_uq"], send["mla_w_ukv"], send["mla_w_o"], send["conv_dw_w"],
     send["ffn_dw_w"]) = _dis_small(d_co, d_so, d_uq, d_uk, d_uv, d_wo, d_cdw, d_fdw)
    send["w_out"] = d_out.reshape(N_DEV, d_out.shape[0] // N_DEV, d_out.shape[1])
    token = start_small(send)

    dsegs = (da_in, db_in, dqlat, dkvlat, dkrope, dgates)
    dx, vec["mix_pre_g"] = _mix_in_bwd(l, sv["x"], w["mix_pre_g"], dx1, dsegs, w["w_in_p"], deps=token)
    d_in_segs = list(_wgrad_multi(sv["h"], dsegs[:5], "wgrad_w_in_abqkr")) + [_wgrad(sv["h"], dgates, "wgrad_w_in_g")]
    rep_pack = _pack_rep([vec[n] for n, _, _, _ in REP_VECS[:-1]], dsgb)
    return dx, _dis_w_in(d_in_segs, shard_cols["w_in"]), rep_pack, dsg_w


def kernel(x, positions, mix_pre_g, mix_post_g, ffn_pre_g, ffn_post_g, w_in, conv_dw_w, conv_dw_b, conv_ln_g, conv_ln_b, conv_out_w, sg_ln_g, sg_ln_b, sg_w, sg_b, sg_out_w, mla_q_norm_g, mla_w_uq, mla_kv_norm_g, mla_w_ukv, mla_w_o, w_out, ffn_w_up, ffn_dw_w, ffn_dw_b, ffn_w_down, loss_target, m_mix_pre_g, m_mix_post_g, m_ffn_pre_g, m_ffn_post_g, m_w_in, m_conv_dw_w, m_conv_dw_b, m_conv_ln_g, m_conv_ln_b, m_conv_out_w, m_sg_ln_g, m_sg_ln_b, m_sg_w, m_sg_b, m_sg_out_w, m_mla_q_norm_g, m_mla_w_uq, m_mla_kv_norm_g, m_mla_w_ukv, m_mla_w_o, m_w_out, m_ffn_w_up, m_ffn_dw_w, m_ffn_dw_b, m_ffn_w_down, v_mix_pre_g, v_mix_post_g, v_ffn_pre_g, v_ffn_post_g, v_w_in, v_conv_dw_w, v_conv_dw_b, v_conv_ln_g, v_conv_ln_b, v_conv_out_w, v_sg_ln_g, v_sg_ln_b, v_sg_w, v_sg_b, v_sg_out_w, v_mla_q_norm_g, v_mla_w_uq, v_mla_kv_norm_g, v_mla_w_ukv, v_mla_w_o, v_w_out, v_ffn_w_up, v_ffn_dw_w, v_ffn_dw_b, v_ffn_w_down):
    args = (x, positions, mix_pre_g, mix_post_g, ffn_pre_g, ffn_post_g, w_in, conv_dw_w, conv_dw_b, conv_ln_g, conv_ln_b, conv_out_w, sg_ln_g, sg_ln_b, sg_w, sg_b, sg_out_w, mla_q_norm_g, mla_w_uq, mla_kv_norm_g, mla_w_ukv, mla_w_o, w_out, ffn_w_up, ffn_dw_w, ffn_dw_b, ffn_w_down, loss_target, m_mix_pre_g, m_mix_post_g, m_ffn_pre_g, m_ffn_post_g, m_w_in, m_conv_dw_w, m_conv_dw_b, m_conv_ln_g, m_conv_ln_b, m_conv_out_w, m_sg_ln_g, m_sg_ln_b, m_sg_w, m_sg_b, m_sg_out_w, m_mla_q_norm_g, m_mla_w_uq, m_mla_kv_norm_g, m_mla_w_ukv, m_mla_w_o, m_w_out, m_ffn_w_up, m_ffn_dw_w, m_ffn_dw_b, m_ffn_w_down, v_mix_pre_g, v_mix_post_g, v_ffn_pre_g, v_ffn_post_g, v_w_in, v_conv_dw_w, v_conv_dw_b, v_conv_ln_g, v_conv_ln_b, v_conv_out_w, v_sg_ln_g, v_sg_ln_b, v_sg_w, v_sg_b, v_sg_out_w, v_mla_q_norm_g, v_mla_w_uq, v_mla_kv_norm_g, v_mla_w_ukv, v_mla_w_o, v_w_out, v_ffn_w_up, v_ffn_dw_w, v_ffn_dw_b, v_ffn_w_down)
    n_in = len(IN_NAMES)
    a = dict(zip(IN_NAMES, args[:n_in]))
    target = args[n_in]
    n_w = len(WEIGHTS)
    m_in = dict(zip(WEIGHTS, args[n_in + 1:n_in + 1 + n_w]))
    v_in = dict(zip(WEIGHTS, args[n_in + 1 + n_w:n_in + 1 + 2 * n_w]))
    depth = a["mix_pre_g"].shape[0]
    rep = {n: a[n] for n in WEIGHTS if n not in SHARDED}
    shard_cols = {n: a[n].shape[2] for n in SHARDED}

    def view(arr, n):
        return jnp.swapaxes(arr, 1, 2) if n in TRANSPOSED else arr

    w_sh = {n: view(a[n], n) for n in SHARDED}
    m_sh = {n: view(m_in[n], n) for n in SHARDED}
    v_sh = {n: view(v_in[n], n) for n in SHARDED}

    def gather_groups(l):
        return (("mix", MIX_GROUP), ("ffn", FFN_BIG)) if l == 0 else (("all", SHARDED),)

    starts, token = {}, ()
    for l in range(depth):
        for tag, group in gather_groups(l):
            wire = [w_sh[n][l] if n in WIRE_F32 else w_sh[n][l].astype(bf16) for n in group]
            starts[l, tag] = _gather_start(wire, "gather_%s_weights_%d" % (tag, l), deps=token)
            token = (starts[l, tag]["token"],)

    cos_t, sin_t = _rope_tables(a["positions"][0])
    xl = a["x"][0]
    ws, saved = [], []
    passing = {}

    def pass_on(l, tag, after):
        passing[l, tag] = _gather_forward(starts[l, tag], after=after)
        return (passing[l, tag]["token"],)

    after = pass_on(0, gather_groups(0)[0][0], token)
    for l in range(depth):
        groups = gather_groups(l)
        got = dict(zip(groups[0][1], _gather_wait(passing[l, groups[0][0]], after=after)))
        w = _mixer_weights(got, rep, l)
        x1, sv = _mixer_fwd(xl, w, cos_t, sin_t)
        after = (x1,)
        if len(groups) > 1:
            after = pass_on(l, groups[1][0], after)
        if l + 1 < depth:
            after = pass_on(l + 1, gather_groups(l + 1)[0][0], after)
        if len(groups) > 1:
            got = dict(zip(groups[1][1], _gather_wait(passing[l, groups[1][0]], after=after)))
        w.update(_ffn_weights(got))
        xl, sv_ffn = _ffn_layer_fwd(x1, w)
        sv.update(sv_ffn)
        ws.append(w)
        saved.append(sv)
        after = (xl,)
    dx, loss_part = _loss_fwd_bwd(xl, target[0])
    loss = lax.psum(loss_part, AXES)

    outs = {}
    rep_recvs, sgw_recvs = [None] * depth, [None] * depth

    def finish(l, parts, after):
        for names, handle in parts:
            got = _exchange_wait(handle, after=after)
            for n, recv in zip(names, got):
                outs[n] = _adamw_shard(l, recv, w_sh[n], m_sh[n], v_sh[n], outs.get(n), "adamw_" + n)
            after = tuple(outs[n][0] for n in names)
        rep_recvs[l], sgw_recvs[l] = got[len(names)], got[len(names) + 1]
        return after

    token, prev, after = (), None, ()
    for l in reversed(range(depth)):
        staged = l == 0
        dx1, send, vec = _ffn_layer_bwd(dx, saved[l], ws[l], deps=token)
        parts = []

        def start(names, extra=(), l=l, parts=parts, send=send):
            tag = "all" if len(names) == len(SHARDED) else names[0]
            handle = _exchange_start([(send[n], "scatter") for n in names] + list(extra), "exchange_%s_grads_%d" % (tag, l))
            parts.append((names, handle))
            return (handle["token"],)

        def start_small(send_small, send=send):
            send.update(send_small)
            return start(MIX_SMALL) if staged else ()

        token = start(FFN_BIG) if staged else ()
        dx, send["w_in"], rep_pack, dsg_w = _mixer_bwd(dx1, saved[l], ws[l], cos_t, sin_t, shard_cols, vec, start_small, deps=token)
        token = start(("w_in",) if staged else SHARDED, extra=[(rep_pack, "gather"), (dsg_w, "gather")])
        if prev is not None:
            after = finish(*prev, after=(dx,) + token)
        prev = (l, parts)
    finish(*prev, after=after)
    vec_names = [n for n, _, _, _ in REP_VECS]
    rep_outs = _adamw_rep(rep_recvs, [a[n] for n in vec_names], [m_in[n] for n in vec_names], [v_in[n] for n in vec_names])
    for i, n in enumerate(vec_names):
        outs[n] = rep_outs[4 * i:4 * i + 4]
    outs["sg_w"] = _adamw_sg_w(sgw_recvs, a["sg_w"], m_in["sg_w"], v_in["sg_w"])
    for n in TRANSPOSED:
        outs[n] = tuple(view(o, n) for o in outs[n])

    grad_w, delta_w, new_m, new_v = ([outs[n][j] for n in WEIGHTS] for j in range(4))
    return (loss, dx[None], *grad_w, *delta_w, *new_m, *new_v)
```

```python
import math

import jax
import jax.numpy as jnp
from jax import lax
from jax.experimental import pallas as pl
from jax.experimental.pallas import tpu as pltpu

f32 = jnp.float32
bf16 = jnp.bfloat16

N_DEV = 8
AXES = ("x", "y", "c")
EPS = 1e-6
D_CONV = 512
CONV_K = 31
CONV_HALO = 32
D_SG = 512
SG_GROUPS = 4
SG_CHUNK = 128
N_HEADS = 8
QK_NOPE = 64
QK_ROPE = 32
V_HEAD = 64
HEAD_PAD = 128
Q_LORA = 384
KV_LORA = 256
D_FF = 2816
FFN_K = 3
FFN_HALO = 8
ATT_SCALE = (QK_NOPE + QK_ROPE) ** -0.5
NEG = float(jnp.finfo(jnp.float32).min)

ADAM_LR = 0.001
ADAM_B1 = 0.9
ADAM_B2 = 0.999
ADAM_EPS = 1e-08
ADAM_WD = 0.01
ADAM_STEP = 10

LANES = 128
VMEM_MB = 56

REF_CUTS = (0, 1024, 2048, 2432, 2688, 2720, 5792)
SEGS = ((0, 1024), (1024, 1024), (2048, 384), (2432, 256), (2688, 128), (2816, 3072))
D_IN = 5792
D_IN_PAD = 5888
KR_LO = 64
SEG_INNER = (0, 0, 0, 0, KR_LO, 0)

IN_NAMES = ['x', 'positions', 'mix_pre_g', 'mix_post_g', 'ffn_pre_g', 'ffn_post_g', 'w_in', 'conv_dw_w', 'conv_dw_b', 'conv_ln_g', 'conv_ln_b', 'conv_out_w', 'sg_ln_g', 'sg_ln_b', 'sg_w', 'sg_b', 'sg_out_w', 'mla_q_norm_g', 'mla_w_uq', 'mla_kv_norm_g', 'mla_w_ukv', 'mla_w_o', 'w_out', 'ffn_w_up', 'ffn_dw_w', 'ffn_dw_b', 'ffn_w_down']
WEIGHTS = IN_NAMES[2:]
SHARDED = ("w_in", "conv_dw_w", "conv_out_w", "sg_out_w", "mla_w_uq", "mla_w_ukv", "mla_w_o", "w_out", "ffn_w_up", "ffn_dw_w",
           "ffn_w_down")
FFN_BIG = ("ffn_w_up", "ffn_w_down")
MIX_GROUP = tuple(n for n in SHARDED if n not in FFN_BIG)
MIX_SMALL = tuple(n for n in MIX_GROUP if n != "w_in")
TRANSPOSED = ("ffn_w_up",)
WIRE_F32 = ("conv_dw_w", "ffn_dw_w")
REP_VECS = (("mix_pre_g", 0, 1024, 1), ("mix_post_g", 1, 1024, 1), ("ffn_pre_g", 2, 1024, 1), ("ffn_post_g", 3, 1024, 1),
            ("conv_dw_b", 4, 512, 1), ("conv_ln_g", 5, 512, 1), ("conv_ln_b", 6, 512, 1), ("sg_ln_g", 7, 512, 1),
            ("sg_ln_b", 8, 512, 1), ("mla_q_norm_g", 9, 384, 1), ("mla_kv_norm_g", 10, 256, 1), ("ffn_dw_b", 11, 5632, 1),
            ("sg_b", 12, 128, 4))
REP_ROWS = 16
REP_W = 5632


def _cparams(n_axes):
    return pltpu.CompilerParams(dimension_semantics=("arbitrary",) * n_axes, vmem_limit_bytes=VMEM_MB * 2 ** 20)


def _rows(tb, n):
    return pl.BlockSpec((tb, n), lambda i: (i, 0))


def _rows_rev(tb, n, nb):
    return pl.BlockSpec((tb, n), lambda i: (nb - 1 - i, 0))


def _const(shape):
    nd = len(shape)
    return pl.BlockSpec(shape, lambda *_: (0,) * nd, pipeline_mode=pl.Buffered(1))


def _acc(shape):
    nd = len(shape)
    return pl.BlockSpec(shape, lambda *_: (0,) * nd)


def _whole(arr):
    return pl.BlockSpec(arr.shape, lambda *_: (0,) * arr.ndim)


def _dot(a, b):
    return jnp.dot(a, b, preferred_element_type=f32)


def _dot_nt(a, b):
    return lax.dot_general(a, b, (((1,), (1,)), ((), ())), preferred_element_type=f32)


def _dot_tn(a, b):
    return lax.dot_general(a, b, (((0,), (0,)), ((), ())), preferred_element_type=f32)


def _mean(x):
    return jnp.mean(x, axis=-1, keepdims=True)


def _colsum(x):
    return jnp.sum(x, axis=0, keepdims=True)


def _rms_fwd(x, g):
    return x * lax.rsqrt(_mean(x * x) + EPS) * g


def _rms_bwd(x, g, dy):
    r = lax.rsqrt(_mean(x * x) + EPS)
    n = x * r
    dn = dy * g
    return r * (dn - n * _mean(dn * n)), _colsum(dy * n)


def _ln_stats(x):
    mu = _mean(x)
    d = x - mu
    rstd = lax.rsqrt(_mean(d * d) + EPS)
    return d * rstd, rstd


def _ln_bwd(xhat, rstd, g, dy):
    dxh = dy * g
    return rstd * (dxh - _mean(dxh) - xhat * _mean(dxh * xhat))


_GELU_C0 = math.sqrt(2.0 / math.pi)
_GELU_C1 = 0.044715


def _gelu(x):
    t = jnp.tanh(_GELU_C0 * (x + _GELU_C1 * (x * x * x)))
    return 0.5 * x * (1.0 + t)


def _gelu_and_grad(x):
    x2 = x * x
    t = jnp.tanh(_GELU_C0 * (x + _GELU_C1 * (x2 * x)))
    g = 0.5 * x * (1.0 + t)
    dg = 0.5 * (1.0 + t) + 0.5 * x * (1.0 - t * t) * (_GELU_C0 * (1.0 + 3.0 * _GELU_C1 * x2))
    return g, dg


def _swap_rope_halves(x):
    n = x.shape[1]
    half = QK_ROPE // 2
    lane = lax.broadcasted_iota(jnp.int32, x.shape, 1) % HEAD_PAD
    first = (lane >= KR_LO) & (lane < KR_LO + half)
    second = (lane >= KR_LO + half) & (lane < KR_LO + QK_ROPE)
    return jnp.where(first, pltpu.roll(x, n - half, 1), jnp.where(second, pltpu.roll(x, half, 1), 0.0))


def _tb(t):
    return min(256, t)


_HBM = pl.BlockSpec(memory_space=pltpu.HBM)
_SEM = pl.BlockSpec(memory_space=pltpu.SEMAPHORE)
_ANY = pl.BlockSpec(memory_space=pl.ANY)
_EFFECT = pltpu.SideEffectType.DATAFLOW_SIDE_EFFECTING


def _exchange_copies(modes, ins, lands, send_sems, recv_sems, loc_sems):
    x, y, c = lax.axis_index("x"), lax.axis_index("y"), lax.axis_index("c")
    me = 4 * x + 2 * y + c
    copies = []
    for a, mode in enumerate(modes):
        def src(dst_index, a=a, mode=mode):
            return ins[a].at[dst_index] if mode == "scatter" else ins[a]
        copies.append(pltpu.make_async_copy(src(me), lands[a].at[me], loc_sems.at[a]))
        for k in range(1, N_DEV):
            px = 1 - x if (k >> 2) & 1 else x
            py = 1 - y if (k >> 1) & 1 else y
            pc = 1 - c if k & 1 else c
            copies.append(pltpu.make_async_remote_copy(
                src_ref=src(4 * px + 2 * py + pc), dst_ref=lands[a].at[me],
                send_sem=send_sems.at[a * (N_DEV - 1) + k - 1], recv_sem=recv_sems.at[a * (N_DEV - 1) + k - 1],
                device_id=(px, py, pc), device_id_type=pl.DeviceIdType.MESH))
    return copies


def _exchange_start(ops, name, deps=()):
    n = len(ops)
    arrs = [arr for arr, _ in ops]
    modes = [mode for _, mode in ops]
    lands = [lax.empty((N_DEV,) + arr.shape if mode == "gather" else arr.shape, arr.dtype) for arr, mode in ops]

    def body(*refs):
        ins, land_refs = refs[:n], refs[n:2 * n]
        send_sems, recv_sems, loc_sems = refs[2 * n + len(deps):2 * n + len(deps) + 3]
        for cp in _exchange_copies(modes, ins, land_refs, send_sems, recv_sems, loc_sems):
            cp.start()
        refs[-1][...] = jnp.zeros((8, LANES), f32)

    n_rem = n * (N_DEV - 1)
    res = pl.pallas_call(
        body, name=name,
        out_shape=(pltpu.SemaphoreType.DMA((n_rem,)), pltpu.SemaphoreType.DMA((n_rem,)), pltpu.SemaphoreType.DMA((n,)),
                   *[pltpu.HBM(x.shape, x.dtype) for x in arrs + lands], jax.ShapeDtypeStruct((8, LANES), f32)),
        in_specs=[_HBM] * (2 * n) + [_ANY] * len(deps),
        out_specs=(_SEM, _SEM, _SEM, *[_HBM] * (2 * n), pl.BlockSpec(memory_space=pltpu.VMEM)),
        input_output_aliases={i: 3 + i for i in range(2 * n)},
        compiler_params=pltpu.CompilerParams(has_side_effects=_EFFECT),
    )(*[pltpu.with_memory_space_constraint(x, pltpu.HBM) for x in arrs + lands], *deps)
    return dict(modes=modes, sems=res[:3], thru=res[3:3 + 2 * n], token=res[-1], name=name)


def _exchange_wait(handle, after=()):
    modes, thru = handle["modes"], handle["thru"]
    n = len(modes)

    def body(*refs):
        ins, land_refs = refs[:n], refs[n:2 * n]
        send_sems, recv_sems, loc_sems = refs[2 * n:2 * n + 3]
        for cp in _exchange_copies(modes, ins, land_refs, send_sems, recv_sems, loc_sems):
            cp.wait()

    res = pl.pallas_call(
        body, name=handle["name"] + "_wait",
        out_shape=tuple(pltpu.HBM(x.shape, x.dtype) for x in thru),
        in_specs=[_HBM] * (2 * n) + [_SEM] * 3 + [_ANY] * len(after),
        out_specs=tuple([_HBM] * (2 * n)),
        input_output_aliases={i: i for i in range(2 * n)},
        compiler_params=pltpu.CompilerParams(has_side_effects=_EFFECT),
    )(*thru, *handle["sems"], *after)
    return res[n:]


_SAME_CORE_PEERS = ((1, 0), (0, 1), (1, 1))


def _gather_copies_one(ins, lands, send_sems, recv_sems, loc_sems):
    x, y, c = lax.axis_index("x"), lax.axis_index("y"), lax.axis_index("c")
    me = 4 * x + 2 * y + c
    targets = [(x, y, 1 - c)] + [(1 - x if fx else x, 1 - y if fy else y, c) for fx, fy in _SAME_CORE_PEERS]
    copies = []
    for a in range(len(ins)):
        copies.append(pltpu.make_async_copy(ins[a], lands[a].at[me], loc_sems.at[a]))
        for j, target in enumerate(targets):
            copies.append(pltpu.make_async_remote_copy(
                src_ref=ins[a], dst_ref=lands[a].at[me], send_sem=send_sems.at[4 * a + j], recv_sem=recv_sems.at[4 * a + j],
                device_id=target, device_id_type=pl.DeviceIdType.MESH))
    return copies


def _gather_copies_two(lands, send_sems, recv_sems):
    x, y, c = lax.axis_index("x"), lax.axis_index("y"), lax.axis_index("c")
    copies = []
    for a in range(len(lands)):
        for j, (fx, fy) in enumerate(_SAME_CORE_PEERS):
            slot = 4 * (1 - x if fx else x) + 2 * (1 - y if fy else y) + c
            copies.append(pltpu.make_async_remote_copy(
                src_ref=lands[a].at[slot], dst_ref=lands[a].at[slot], send_sem=send_sems.at[3 * a + j],
                recv_sem=recv_sems.at[3 * a + j], device_id=(x, y, 1 - c), device_id_type=pl.DeviceIdType.MESH))
    return copies


def _gather_start(arrs, name, deps=()):
    n = len(arrs)
    lands = [lax.empty((N_DEV,) + arr.shape, arr.dtype) for arr in arrs]

    def body(*refs):
        ins, land_refs = refs[:n], refs[n:2 * n]
        send_sems, recv_sems, loc_sems = refs[2 * n + len(deps):2 * n + len(deps) + 3]
        for cp in _gather_copies_one(ins, land_refs, send_sems, recv_sems, loc_sems):
            cp.start()
        refs[-1][...] = jnp.zeros((8, LANES), f32)

    res = pl.pallas_call(
        body, name=name,
        out_shape=(pltpu.SemaphoreType.DMA((4 * n,)), pltpu.SemaphoreType.DMA((4 * n,)), pltpu.SemaphoreType.DMA((n,)),
                   *[pltpu.HBM(x.shape, x.dtype) for x in arrs + lands], jax.ShapeDtypeStruct((8, LANES), f32)),
        in_specs=[_HBM] * (2 * n) + [_ANY] * len(deps),
        out_specs=(_SEM, _SEM, _SEM, *[_HBM] * (2 * n), pl.BlockSpec(memory_space=pltpu.VMEM)),
        input_output_aliases={i: 3 + i for i in range(2 * n)},
        compiler_params=pltpu.CompilerParams(has_side_effects=_EFFECT),
    )(*[pltpu.with_memory_space_constraint(x, pltpu.HBM) for x in arrs + lands], *deps)
    return dict(n=n, sems=res[:3], thru=res[3:3 + 2 * n], token=res[-1], name=name)


def _gather_forward(handle, after=()):
    n, thru = handle["n"], handle["thru"]

    def body(*refs):
        ins, land_refs = refs[:n], refs[n:2 * n]
        send_one, recv_one, loc_sems = refs[2 * n:2 * n + 3]
        send_two, recv_two = refs[2 * n + 3 + len(after):2 * n + 5 + len(after)]
        for cp in _gather_copies_one(ins, land_refs, send_one, recv_one, loc_sems):
            cp.wait()
        for cp in _gather_copies_two(land_refs, send_two, recv_two):
            cp.start()
        refs[-1][...] = jnp.zeros((8, LANES), f32)

    res = pl.pallas_call(
        body, name=handle["name"] + "_forward",
        out_shape=(pltpu.SemaphoreType.DMA((3 * n,)), pltpu.SemaphoreType.DMA((3 * n,)),
                   *[pltpu.HBM(x.shape, x.dtype) for x in thru], jax.ShapeDtypeStruct((8, LANES), f32)),
        in_specs=[_HBM] * (2 * n) + [_SEM] * 3 + [_ANY] * len(after),
        out_specs=(_SEM, _SEM, *[_HBM] * (2 * n), pl.BlockSpec(memory_space=pltpu.VMEM)),
        input_output_aliases={i: 2 + i for i in range(2 * n)},
        compiler_params=pltpu.CompilerParams(has_side_effects=_EFFECT),
    )(*thru, *handle["sems"], *after)
    return dict(n=n, sems=res[:2], lands=res[2 + n:2 + 2 * n], token=res[-1], name=handle["name"])


def _gather_wait(handle, after=()):
    n, lands = handle["n"], handle["lands"]

    def body(*refs):
        land_refs = refs[:n]
        send_two, recv_two = refs[n:n + 2]
        for cp in _gather_copies_two(land_refs, send_two, recv_two):
            cp.wait()

    return pl.pallas_call(
        body, name=handle["name"] + "_wait",
        out_shape=tuple(pltpu.HBM(x.shape, x.dtype) for x in lands),
        in_specs=[_HBM] * n + [_SEM] * 2 + [_ANY] * len(after),
        out_specs=tuple([_HBM] * n),
        input_output_aliases={i: i for i in range(n)},
        compiler_params=pltpu.CompilerParams(has_side_effects=_EFFECT),
    )(*lands, *handle["sems"], *after)


def _w_in_pieces(ns):
    out = []
    for e in range(N_DEV):
        lo, hi = ns * e, ns * (e + 1)
        for s in range(len(SEGS)):
            a, b = max(lo, REF_CUTS[s]), min(hi, REF_CUTS[s + 1])
            if a < b:
                inner = SEG_INNER[s] + a - REF_CUTS[s]
                out.append((e, a - lo, b - lo, s, inner, inner + b - a))
    return out


def _asm_w_in(g):
    _, d, ns = g.shape
    rb = 256
    pieces = _w_in_pieces(ns)

    def body(g_ref, o_ref):
        kr = SEGS[4][0]
        o_ref[:, kr:kr + KR_LO] = jnp.zeros((rb, KR_LO), g.dtype)
        o_ref[:, kr + KR_LO + QK_ROPE:kr + HEAD_PAD] = jnp.zeros((rb, HEAD_PAD - KR_LO - QK_ROPE), g.dtype)
        for e, s0, s1, seg, d0, d1 in pieces:
            off = SEGS[seg][0]
            o_ref[:, off + d0:off + d1] = g_ref[e, :, s0:s1]

    return pl.pallas_call(
        body, name="asm_w_in", grid=(d // rb,),
        in_specs=[pl.BlockSpec((N_DEV, rb, ns), lambda i: (0, i, 0))],
        out_specs=_rows(rb, D_IN_PAD), out_shape=jax.ShapeDtypeStruct((d, D_IN_PAD), g.dtype),
        compiler_params=_cparams(1),
    )(g)


def _dis_w_in(dsegs, ns):
    d = dsegs[0].shape[0]
    rb = 256
    pieces = _w_in_pieces(ns)

    def body(*refs):
        seg_refs, o_ref = refs[:len(SEGS)], refs[len(SEGS)]
        for e, s0, s1, seg, d0, d1 in pieces:
            o_ref[e, :, s0:s1] = seg_refs[seg][:, d0:d1].astype(bf16)

    return pl.pallas_call(
        body, name="dis_w_in", grid=(d // rb,),
        in_specs=[_rows(rb, wd) for _, wd in SEGS],
        out_specs=pl.BlockSpec((N_DEV, rb, ns), lambda i: (0, i, 0)),
        out_shape=jax.ShapeDtypeStruct((N_DEV, d, ns), bf16),
        compiler_params=_cparams(1),
    )(*dsegs)


def _asm_small(g_conv_out, g_sg_out, g_uq, g_ukv, g_wo, g_cdw, g_fdw):
    d = g_conv_out.shape[2] * N_DEV
    hw = N_HEADS * HEAD_PAD
    hq = QK_NOPE + QK_ROPE
    ff2 = g_fdw.shape[2] * N_DEV
    cw, fw = g_cdw.shape[2], g_fdw.shape[2]

    def body(co_ref, so_ref, uq_ref, ukv_ref, wo_ref, cdw_ref, fdw_ref, o_co, o_so, o_uq, o_k, o_v, o_wo, o_cdw, o_fdw):
        o_cdw[CONV_K:CONV_HALO, :] = jnp.zeros((CONV_HALO - CONV_K, D_CONV), f32)
        o_fdw[FFN_K:FFN_HALO, :] = jnp.zeros((FFN_HALO - FFN_K, ff2), f32)
        for e in range(N_DEV):
            cs = e * HEAD_PAD
            o_co[:, cs:cs + HEAD_PAD] = co_ref[e]
            o_so[:, cs:cs + HEAD_PAD] = so_ref[e]
            o_uq[:, cs:cs + hq] = uq_ref[e]
            o_uq[:, cs + hq:cs + HEAD_PAD] = jnp.zeros((Q_LORA, HEAD_PAD - hq), bf16)
            o_k[:, cs:cs + QK_NOPE] = ukv_ref[e, :, 0:QK_NOPE]
            o_k[:, cs + QK_NOPE:cs + HEAD_PAD] = jnp.zeros((KV_LORA, HEAD_PAD - QK_NOPE), bf16)
            o_v[:, cs:cs + V_HEAD] = ukv_ref[e, :, QK_NOPE:QK_NOPE + V_HEAD]
            o_v[:, cs + V_HEAD:cs + HEAD_PAD] = jnp.zeros((KV_LORA, HEAD_PAD - V_HEAD), bf16)
            for h in range(N_HEADS):
                o_wo[h * HEAD_PAD:h * HEAD_PAD + V_HEAD, cs:cs + HEAD_PAD] = wo_ref[e, h * V_HEAD:(h + 1) * V_HEAD, :]
                o_wo[h * HEAD_PAD + V_HEAD:(h + 1) * HEAD_PAD, cs:cs + HEAD_PAD] = jnp.zeros((HEAD_PAD - V_HEAD, HEAD_PAD), bf16)
            o_cdw[0:CONV_K, e * cw:(e + 1) * cw] = cdw_ref[e]
            o_fdw[0:FFN_K, e * fw:(e + 1) * fw] = fdw_ref[e]

    ins = (g_conv_out, g_sg_out, g_uq, g_ukv, g_wo, g_cdw, g_fdw)
    out_shape = (jax.ShapeDtypeStruct((D_CONV, d), bf16), jax.ShapeDtypeStruct((D_SG, d), bf16), jax.ShapeDtypeStruct((Q_LORA, hw), bf16),
                 jax.ShapeDtypeStruct((KV_LORA, hw), bf16), jax.ShapeDtypeStruct((KV_LORA, hw), bf16), jax.ShapeDtypeStruct((hw, d), bf16),
                 jax.ShapeDtypeStruct((CONV_HALO, D_CONV), f32), jax.ShapeDtypeStruct((FFN_HALO, ff2), f32))
    return pl.pallas_call(
        body, name="asm_small", grid=(1,),
        in_specs=[_whole(a) for a in ins], out_specs=tuple(_whole(o) for o in out_shape), out_shape=out_shape,
        compiler_params=_cparams(1),
    )(*ins)


def _dis_small(d_co, d_so, d_uq, d_k, d_v, d_wo, d_cdw, d_fdw):
    d = d_co.shape[1]
    hq = QK_NOPE + QK_ROPE
    cw, fw = D_CONV // N_DEV, d_fdw.shape[1] // N_DEV

    def body(co_ref, so_ref, uq_ref, k_ref, v_ref, wo_ref, cdw_ref, fdw_ref, o_co, o_so, o_uq, o_ukv, o_wo, o_cdw, o_fdw):
        for e in range(N_DEV):
            cs = e * HEAD_PAD
            o_co[e] = co_ref[:, cs:cs + HEAD_PAD].astype(bf16)
            o_so[e] = so_ref[:, cs:cs + HEAD_PAD].astype(bf16)
            o_uq[e] = uq_ref[:, cs:cs + hq].astype(bf16)
            o_ukv[e, :, 0:QK_NOPE] = k_ref[:, cs:cs + QK_NOPE].astype(bf16)
            o_ukv[e, :, QK_NOPE:QK_NOPE + V_HEAD] = v_ref[:, cs:cs + V_HEAD].astype(bf16)
            for h in range(N_HEADS):
                o_wo[e, h * V_HEAD:(h + 1) * V_HEAD, :] = wo_ref[h * HEAD_PAD:h * HEAD_PAD + V_HEAD, cs:cs + HEAD_PAD].astype(bf16)
            o_cdw[e] = cdw_ref[0:CONV_K, e * cw:(e + 1) * cw]
            o_fdw[e] = fdw_ref[0:FFN_K, e * fw:(e + 1) * fw]

    ins = (d_co, d_so, d_uq, d_k, d_v, d_wo, d_cdw, d_fdw)
    out_shape = (jax.ShapeDtypeStruct((N_DEV, D_CONV, d // N_DEV), bf16), jax.ShapeDtypeStruct((N_DEV, D_SG, d // N_DEV), bf16),
                 jax.ShapeDtypeStruct((N_DEV, Q_LORA, hq), bf16), jax.ShapeDtypeStruct((N_DEV, KV_LORA, QK_NOPE + V_HEAD), bf16),
                 jax.ShapeDtypeStruct((N_DEV, N_HEADS * V_HEAD, d // N_DEV), bf16), jax.ShapeDtypeStruct((N_DEV, CONV_K, cw), f32),
                 jax.ShapeDtypeStruct((N_DEV, FFN_K, fw), f32))
    return pl.pallas_call(
        body, name="dis_small", grid=(1,),
        in_specs=[_whole(a) for a in ins], out_specs=tuple(_whole(o) for o in out_shape), out_shape=out_shape,
        compiler_params=_cparams(1),
    )(*ins)


def _pack_rep(vec_grads, dsgb):
    def body(*refs):
        o_ref = refs[-1]
        o_ref[...] = jnp.zeros((REP_ROWS, REP_W), f32)
        for (_, row, width, nrows), ref in zip(REP_VECS, refs[:-1]):
            o_ref[row:row + nrows, 0:width] = ref[0:nrows, :]

    ins = tuple(vec_grads) + (dsgb,)
    return pl.pallas_call(
        body, name="pack_rep", grid=(1,),
        in_specs=[_whole(a) for a in ins], out_specs=pl.BlockSpec((REP_ROWS, REP_W), lambda i: (0, 0)),
        out_shape=jax.ShapeDtypeStruct((REP_ROWS, REP_W), f32), compiler_params=_cparams(1),
    )(*ins)


def _mix_in_fwd(l, x, g1, w_in_p):
    t, d = x.shape
    tb = min(512, t)

    def body(x_ref, g_ref, w_ref, h_ref, *outs):
        h = _rms_fwd(x_ref[...], g_ref[l:l + 1, :]).astype(bf16)
        h_ref[...] = h
        for (off, wd), o_ref in zip(SEGS, outs):
            o_ref[...] = _dot(h, w_ref[:, off:off + wd])

    return pl.pallas_call(
        body, name="mix_in_fwd", grid=(t // tb,),
        in_specs=[_rows(tb, d), _const(g1.shape), _const((d, D_IN_PAD))],
        out_specs=tuple([_rows(tb, d)] + [_rows(tb, wd) for _, wd in SEGS]),
        out_shape=tuple([jax.ShapeDtypeStruct((t, d), bf16)] + [jax.ShapeDtypeStruct((t, wd), f32) for _, wd in SEGS]),
        compiler_params=_cparams(1),
    )(x, g1, w_in_p)


def _conv_fwd(l, a_in, dw_w, dw_b, ln_g, ln_b):
    t = a_in.shape[0]
    tb = _tb(t)

    def body(a_ref, w_ref, b_ref, g_ref, be_ref, s_ref, z1_ref, win):
        @pl.when(pl.program_id(0) == 0)
        def _():
            win[0:CONV_HALO, :] = jnp.zeros((CONV_HALO, D_CONV), f32)

        a = a_ref[...]
        win[CONV_HALO:CONV_HALO + tb, :] = a[:, :D_CONV] * jax.nn.sigmoid(a[:, D_CONV:])
        for r0 in range(0, tb, LANES):
            for c0 in range(0, D_CONV, LANES):
                cs = slice(c0, c0 + LANES)
                acc = jnp.broadcast_to(b_ref[l:l + 1, cs], (LANES, LANES))
                for j in range(CONV_K):
                    acc = acc + w_ref[j:j + 1, cs] * win[pl.ds(r0 + CONV_HALO - CONV_K + 1 + j, LANES), cs]
                z1_ref[r0:r0 + LANES, cs] = acc
        zh, _ = _ln_stats(z1_ref[...])
        zl = zh * g_ref[l:l + 1, :] + be_ref[l:l + 1, :]
        s_ref[...] = (zl * jax.nn.sigmoid(zl)).astype(bf16)
        win[0:CONV_HALO, :] = win[tb:tb + CONV_HALO, :]

    return pl.pallas_call(
        body, name="conv_fwd", grid=(t // tb,),
        in_specs=[_rows(tb, 2 * D_CONV), _const((CONV_HALO, D_CONV)), _const(dw_b.shape), _const(ln_g.shape), _const(ln_b.shape)],
        out_specs=(_rows(tb, D_CONV), _rows(tb, D_CONV)),
        out_shape=(jax.ShapeDtypeStruct((t, D_CONV), bf16), jax.ShapeDtypeStruct((t, D_CONV), f32)),
        scratch_shapes=[pltpu.VMEM((tb + CONV_HALO, D_CONV), f32)],
        compiler_params=_cparams(1),
    )(a_in, dw_w, dw_b, ln_g, ln_b)


def _tril_mask():
    r = lax.broadcasted_iota(jnp.int32, (SG_CHUNK, SG_CHUNK), 0)
    c = lax.broadcasted_iota(jnp.int32, (SG_CHUNK, SG_CHUNK), 1)
    return r >= c


def _sgu_fwd(l, b_in, ln_g, ln_b, sg_w, bexp):
    t = b_in.shape[0]
    tb = _tb(t)
    gw = D_SG // SG_GROUPS

    def body(b_ref, g_ref, be_ref, w_ref, bexp_ref, um_ref):
        gl = _gelu(b_ref[...])
        u = gl[:, :D_SG]
        vh, _ = _ln_stats(gl[:, D_SG:])
        vn = (vh * g_ref[l:l + 1, :] + be_ref[l:l + 1, :]).astype(bf16)
        tri = _tril_mask()
        for g in range(SG_GROUPS):
            wg = jnp.where(tri, w_ref[l, g], 0.0).astype(bf16)
            cs = slice(g * gw, (g + 1) * gw)
            for r0 in range(0, tb, SG_CHUNK):
                rs = slice(r0, r0 + SG_CHUNK)
                mixed = _dot(wg, vn[rs, cs]) + bexp_ref[:, cs]
                um_ref[rs, cs] = (u[rs, cs] * mixed).astype(bf16)

    return pl.pallas_call(
        body, name="sgu_fwd", grid=(t // tb,),
        in_specs=[_rows(tb, 2 * D_SG), _const(ln_g.shape), _const(ln_b.shape), _const(sg_w.shape), _const((SG_CHUNK, D_SG))],
        out_specs=_rows(tb, D_SG),
        out_shape=jax.ShapeDtypeStruct((t, D_SG), bf16),
        compiler_params=_cparams(1),
    )(b_in, ln_g, ln_b, sg_w, bexp)


def _mla_proj_fwd(l, qlat, kvlat, krope, cos_t, sin_t, gq, wuq_p, gkv, wk_p, wv_p):
    t = qlat.shape[0]
    tb = _tb(t)
    hw = N_HEADS * HEAD_PAD

    def body(ql_ref, kvl_ref, kr_ref, c_ref, s_ref, gq_ref, wq_ref, gkv_ref, wk_ref, wv_ref, q_ref, k_ref, v_ref, qn_ref, kvn_ref):
        cos_b, sin_b = c_ref[...], s_ref[...]
        qn = _rms_fwd(ql_ref[...], gq_ref[l:l + 1, :]).astype(bf16)
        qn_ref[...] = qn
        q = _dot(qn, wq_ref[...])
        sw = _swap_rope_halves(q)
        for h in range(N_HEADS):
            hs = slice(h * HEAD_PAD, (h + 1) * HEAD_PAD)
            q_ref[:, hs] = (q[:, hs] * cos_b + sw[:, hs] * sin_b).astype(bf16)
        kvn = _rms_fwd(kvl_ref[...], gkv_ref[l:l + 1, :]).astype(bf16)
        kvn_ref[...] = kvn
        kr = kr_ref[...]
        kpe = kr * cos_b + _swap_rope_halves(kr) * sin_b
        k = _dot(kvn, wk_ref[...])
        for h in range(N_HEADS):
            hs = slice(h * HEAD_PAD, (h + 1) * HEAD_PAD)
            k_ref[:, hs] = (k[:, hs] + kpe).astype(bf16)
        v_ref[...] = _dot(kvn, wv_ref[...]).astype(bf16)

    return pl.pallas_call(
        body, name="mla_proj_fwd", grid=(t // tb,),
        in_specs=[_rows(tb, Q_LORA), _rows(tb, KV_LORA), _rows(tb, HEAD_PAD), _rows(tb, HEAD_PAD), _rows(tb, HEAD_PAD),
                  _const(gq.shape), _const((Q_LORA, hw)), _const(gkv.shape), _const((KV_LORA, hw)), _const((KV_LORA, hw))],
        out_specs=(_rows(tb, hw), _rows(tb, hw), _rows(tb, hw), _rows(tb, Q_LORA), _rows(tb, KV_LORA)),
        out_shape=(jax.ShapeDtypeStruct((t, hw), bf16), jax.ShapeDtypeStruct((t, hw), bf16), jax.ShapeDtypeStruct((t, hw), bf16),
                   jax.ShapeDtypeStruct((t, Q_LORA), bf16), jax.ShapeDtypeStruct((t, KV_LORA), bf16)),
        compiler_params=_cparams(1),
    )(qlat, kvlat, krope, cos_t, sin_t, gq, wuq_p, gkv, wk_p, wv_p)


def _diag_mask(tq):
    return lax.broadcasted_iota(jnp.int32, (tq, tq), 0) >= lax.broadcasted_iota(jnp.int32, (tq, tq), 1)


def _attn_fwd(q, k, v):
    t = q.shape[0]
    tq = _tb(t)

    def body(q_ref, k_ref, v_ref, o_ref, lse_ref):
        qi = pl.program_id(1)
        for i in range(t // tq):
            @pl.when(qi == i)
            def _(i=i):
                qv = q_ref[...]
                lo, hi = i * tq, (i + 1) * tq
                s_d = jnp.where(_diag_mask(tq), _dot_nt(qv, k_ref[lo:hi, :]) * ATT_SCALE, NEG)
                m = jnp.max(s_d, axis=-1, keepdims=True)
                if i > 0:
                    s_o = _dot_nt(qv, k_ref[0:lo, :]) * ATT_SCALE
                    m = jnp.maximum(m, jnp.max(s_o, axis=-1, keepdims=True))
                p_d = jnp.exp(s_d - m)
                lsum = jnp.sum(p_d, axis=-1, keepdims=True)
                acc = _dot(p_d.astype(bf16), v_ref[lo:hi, :])
                if i > 0:
                    p_o = jnp.exp(s_o - m)
                    lsum = lsum + jnp.sum(p_o, axis=-1, keepdims=True)
                    acc = acc + _dot(p_o.astype(bf16), v_ref[0:lo, :])
                o_ref[...] = (acc / lsum).astype(bf16)
                lse_ref[...] = m + jnp.log(lsum)

    return pl.pallas_call(
        body, name="attn_fwd", grid=(N_HEADS, t // tq),
        in_specs=[pl.BlockSpec((tq, HEAD_PAD), lambda h, i: (i, h)), pl.BlockSpec((t, HEAD_PAD), lambda h, i: (0, h)),
                  pl.BlockSpec((t, HEAD_PAD), lambda h, i: (0, h))],
        out_specs=(pl.BlockSpec((tq, HEAD_PAD), lambda h, i: (i, h)), pl.BlockSpec((None, tq, 1), lambda h, i: (h, i, 0))),
        out_shape=(jax.ShapeDtypeStruct((t, N_HEADS * HEAD_PAD), bf16), jax.ShapeDtypeStruct((N_HEADS, t, 1), f32)),
        compiler_params=_cparams(2),
    )(q, k, v)


def _merge_out_fwd(l, x, s, um, o, gates, conv_out_w, sg_out_w, wo_p, w_out, g2):
    t, d = x.shape
    tb = _tb(t)

    def body(x_ref, s_ref, um_ref, o_ref, gt_ref, wa_ref, wb_ref, wc_ref, wout_ref, g_ref, x1_ref, mg_ref, om_ref):
        merged = (jax.nn.sigmoid(gt_ref[:, 0:d]) * _dot(s_ref[...], wa_ref[...])
                  + jax.nn.sigmoid(gt_ref[:, d:2 * d]) * _dot(um_ref[...], wb_ref[...])
                  + jax.nn.sigmoid(gt_ref[:, 2 * d:3 * d]) * _dot(o_ref[...], wc_ref[...]))
        mb = merged.astype(bf16)
        mg_ref[...] = mb
        om = _dot(mb, wout_ref[...])
        om_ref[...] = om
        x1_ref[...] = x_ref[...] + _rms_fwd(om, g_ref[l:l + 1, :])

    hw = N_HEADS * HEAD_PAD
    return pl.pallas_call(
        body, name="merge_out_fwd", grid=(t // tb,),
        in_specs=[_rows(tb, d), _rows(tb, D_CONV), _rows(tb, D_SG), _rows(tb, hw), _rows(tb, 3 * d),
                  _const((D_CONV, d)), _const((D_SG, d)), _const((hw, d)), _const((d, d)), _const(g2.shape)],
        out_specs=(_rows(tb, d), _rows(tb, d), _rows(tb, d)),
        out_shape=(jax.ShapeDtypeStruct((t, d), f32), jax.ShapeDtypeStruct((t, d), bf16), jax.ShapeDtypeStruct((t, d), f32)),
        compiler_params=_cparams(1),
    )(x, s, um, o, gates, conv_out_w, sg_out_w, wo_p, w_out, g2)


FF_CHUNK = 1408


def _ffn_conv_cols(zbuf, w_ref, b_ref, l, nrows, c0, c1):
    acc = b_ref[l:l + 1, c0:c1] + w_ref[0:1, c0:c1] * zbuf[pl.ds(FFN_HALO - 2, nrows), c0:c1]
    acc = acc + w_ref[1:2, c0:c1] * zbuf[pl.ds(FFN_HALO - 1, nrows), c0:c1]
    return acc + w_ref[2:3, c0:c1] * zbuf[pl.ds(FFN_HALO, nrows), c0:c1]


def _ffn_fwd(l, x1, g3, w_up, dw_w, dw_b, w_down, g4):
    t, d = x1.shape
    tb = _tb(t)
    ff2 = 2 * D_FF

    def body(x_ref, g3_ref, wup_ref, dww_ref, dwb_ref, wdn_ref, g4_ref, x2_ref, h2_ref, z_ref, zc_ref, act_ref, f_ref, zbuf):
        @pl.when(pl.program_id(0) == 0)
        def _():
            zbuf[0:FFN_HALO, :] = jnp.zeros((FFN_HALO, ff2), f32)

        xv = x_ref[...]
        h2 = _rms_fwd(xv, g3_ref[l:l + 1, :]).astype(bf16)
        h2_ref[...] = h2
        for c0 in range(0, ff2, FF_CHUNK):
            zv = _dot_nt(h2, wup_ref[c0:c0 + FF_CHUNK, :])
            z_ref[:, c0:c0 + FF_CHUNK] = zv.astype(bf16)
            zbuf[FFN_HALO:FFN_HALO + tb, c0:c0 + FF_CHUNK] = zv
        facc = jnp.zeros((tb, d), f32)
        for c0 in range(0, D_FF, FF_CHUNK):
            gg = _ffn_conv_cols(zbuf, dww_ref, dwb_ref, l, tb, c0, c0 + FF_CHUNK)
            vv = _ffn_conv_cols(zbuf, dww_ref, dwb_ref, l, tb, D_FF + c0, D_FF + c0 + FF_CHUNK)
            zc_ref[:, c0:c0 + FF_CHUNK] = gg.astype(bf16)
            zc_ref[:, D_FF + c0:D_FF + c0 + FF_CHUNK] = vv.astype(bf16)
            a = (_gelu(gg) * vv).astype(bf16)
            act_ref[:, c0:c0 + FF_CHUNK] = a
            facc = facc + _dot(a, wdn_ref[c0:c0 + FF_CHUNK, :])
        f_ref[...] = facc
        x2_ref[...] = xv + _rms_fwd(facc, g4_ref[l:l + 1, :])
        zbuf[0:FFN_HALO, :] = zbuf[tb:tb + FFN_HALO, :]

    return pl.pallas_call(
        body, name="ffn_fwd", grid=(t // tb,),
        in_specs=[_rows(tb, d), _const(g3.shape), _const((ff2, d)), _const((FFN_HALO, ff2)), _const(dw_b.shape), _const((D_FF, d)),
                  _const(g4.shape)],
        out_specs=(_rows(tb, d), _rows(tb, d), _rows(tb, ff2), _rows(tb, ff2), _rows(tb, D_FF), _rows(tb, d)),
        out_shape=(jax.ShapeDtypeStruct((t, d), f32), jax.ShapeDtypeStruct((t, d), bf16), jax.ShapeDtypeStruct((t, ff2), bf16),
                   jax.ShapeDtypeStruct((t, ff2), bf16), jax.ShapeDtypeStruct((t, D_FF), bf16), jax.ShapeDtypeStruct((t, d), f32)),
        scratch_shapes=[pltpu.VMEM((tb + FFN_HALO, ff2), f32)],
        compiler_params=_cparams(1),
    )(x1, g3, w_up, dw_w, dw_b, w_down, g4)


def _loss_fwd_bwd(y, target):
    t, d = y.shape
    tb = _tb(t)

    def body(y_ref, t_ref, dy_ref, loss_ref):
        @pl.when(pl.program_id(0) == 0)
        def _():
            loss_ref[...] = jnp.zeros((1, LANES), f32)

        e = y_ref[...] - t_ref[...]
        dy_ref[...] = e * (1.0 / d)
        loss_ref[...] += 0.5 * jnp.sum(_mean(e * e))

    dy, loss = pl.pallas_call(
        body, name="loss", grid=(t // tb,),
        in_specs=[_rows(tb, d), _rows(tb, d)],
        out_specs=(_rows(tb, d), _acc((1, LANES))),
        out_shape=(jax.ShapeDtypeStruct((t, d), f32), jax.ShapeDtypeStruct((1, LANES), f32)),
        compiler_params=_cparams(1),
    )(y, target)
    return dy, loss[0, 0]


def _ffn_bwd(l, dx2, x1, f, z, zc, w_up, dw_w, w_down, g3, g4, deps=()):
    t, d = x1.shape
    tb = min(128, t)
    nb = t // tb
    ff2 = 2 * D_FF
    hrows = 16
    per_h = tb // hrows

    def body(*refs):
        (dx2_ref, x1_ref, f_ref, z_ref, zp_ref, zc_ref, wup_ref, dww_ref, wdn_ref, g3_ref, g4_ref,
         dx1_ref, df_ref, dz_ref, dg4_ref, dg3_ref, ddwb_ref, ddww_ref, zbuf, dzc) = refs[len(deps):]
        i = pl.program_id(0)
        b = nb - 1 - i

        @pl.when(i == 0)
        def _():
            dg4_ref[...] = jnp.zeros_like(dg4_ref)
            dg3_ref[...] = jnp.zeros_like(dg3_ref)
            ddwb_ref[...] = jnp.zeros_like(ddwb_ref)
            ddww_ref[...] = jnp.zeros_like(ddww_ref)
            dzc[tb:tb + FFN_HALO, :] = jnp.zeros((FFN_HALO, ff2), f32)

        dout = dx2_ref[...]
        df, dg4 = _rms_bwd(f_ref[...], g4_ref[l:l + 1, :], dout)
        dg4_ref[...] += dg4
        dfb = df.astype(bf16)
        df_ref[...] = dfb
        zbuf[0:FFN_HALO, :] = jnp.where(b > 0, zp_ref[hrows - FFN_HALO:hrows, :].astype(f32), 0.0)
        zbuf[FFN_HALO:FFN_HALO + tb, :] = z_ref[...].astype(f32)
        for c0 in range(0, D_FF, FF_CHUNK):
            dact = _dot_nt(dfb, wdn_ref[c0:c0 + FF_CHUNK, :])
            gel, dgel = _gelu_and_grad(zc_ref[:, c0:c0 + FF_CHUNK].astype(f32))
            dzc[0:tb, c0:c0 + FF_CHUNK] = dact * zc_ref[:, D_FF + c0:D_FF + c0 + FF_CHUNK].astype(f32) * dgel
            dzc[0:tb, D_FF + c0:D_FF + c0 + FF_CHUNK] = dact * gel
        dh2 = jnp.zeros((tb, d), f32)
        for c0 in range(0, ff2, FF_CHUNK):
            cs = slice(c0, c0 + FF_CHUNK)
            d0 = dzc[0:tb, cs]
            ddwb_ref[:, cs] += _colsum(d0)
            for j in range(FFN_K):
                ddww_ref[j:j + 1, cs] += _colsum(d0 * zbuf[pl.ds(FFN_HALO - 2 + j, tb), cs])
            dzv = dww_ref[2:3, cs] * d0 + dww_ref[1:2, cs] * dzc[pl.ds(1, tb), cs] + dww_ref[0:1, cs] * dzc[pl.ds(2, tb), cs]
            dzb = dzv.astype(bf16)
            dz_ref[:, cs] = dzb
            dh2 = dh2 + _dot(dzb, wup_ref[cs, :])
        dzc[tb:tb + FFN_HALO, :] = dzc[0:FFN_HALO, :]
        dxn, dg3 = _rms_bwd(x1_ref[...], g3_ref[l:l + 1, :], dh2)
        dg3_ref[...] += dg3
        dx1_ref[...] = dout + dxn

    return pl.pallas_call(
        body, name="ffn_bwd", grid=(nb,),
        in_specs=[_ANY] * len(deps) + [_rows_rev(tb, d, nb), _rows_rev(tb, d, nb), _rows_rev(tb, d, nb), _rows_rev(tb, ff2, nb),
                  pl.BlockSpec((hrows, ff2), lambda i: (jnp.maximum((nb - 1 - i) * per_h - 1, 0), 0)), _rows_rev(tb, ff2, nb),
                  _const((ff2, d)), _const((FFN_HALO, ff2)), _const((D_FF, d)), _const(g3.shape), _const(g4.shape)],
        out_specs=(_rows_rev(tb, d, nb), _rows_rev(tb, d, nb), _rows_rev(tb, ff2, nb), _acc((1, d)), _acc((1, d)), _acc((1, ff2)),
                   _acc((FFN_HALO, ff2))),
        out_shape=(jax.ShapeDtypeStruct((t, d), f32), jax.ShapeDtypeStruct((t, d), bf16), jax.ShapeDtypeStruct((t, ff2), bf16),
                   jax.ShapeDtypeStruct((1, d), f32), jax.ShapeDtypeStruct((1, d), f32), jax.ShapeDtypeStruct((1, ff2), f32),
                   jax.ShapeDtypeStruct((FFN_HALO, ff2), f32)),
        scratch_shapes=[pltpu.VMEM((tb + FFN_HALO, ff2), f32), pltpu.VMEM((tb + FFN_HALO, ff2), f32)],
        compiler_params=_cparams(1),
    )(*deps, dx2, x1, f, z, z, zc, w_up, dw_w, w_down, g3, g4)


def _merge_out_bwd(l, dx1, om, s, um, o, gates, conv_out_w, sg_out_w, wo_p, w_out, g2, deps=()):
    t, d = dx1.shape
    tb = _tb(t)
    hw = N_HEADS * HEAD_PAD

    def body(*refs):
        (dx_ref, om_ref, s_ref, um_ref, o_ref, gt_ref, wa_ref, wb_ref, wc_ref, wout_ref, g_ref,
         dom_ref, dgt_ref, dya_ref, dyb_ref, dyc_ref, ds_ref, dum_ref, do_ref, dg2_ref) = refs[len(deps):]

        @pl.when(pl.program_id(0) == 0)
        def _():
            dg2_ref[...] = jnp.zeros_like(dg2_ref)

        dom, dg2 = _rms_bwd(om_ref[...], g_ref[l:l + 1, :], dx_ref[...])
        dg2_ref[...] += dg2
        domb = dom.astype(bf16)
        dom_ref[...] = domb
        dmerged = _dot_nt(domb, wout_ref[...])
        branches = ((s_ref, wa_ref, dya_ref, ds_ref), (um_ref, wb_ref, dyb_ref, dum_ref), (o_ref, wc_ref, dyc_ref, do_ref))
        for br, (in_ref, w_ref, dy_ref, din_ref) in enumerate(branches):
            yv = _dot(in_ref[...], w_ref[...])
            sg = jax.nn.sigmoid(gt_ref[:, br * d:(br + 1) * d])
            dyb = (dmerged * sg).astype(bf16)
            dy_ref[...] = dyb
            dgt_ref[:, br * d:(br + 1) * d] = (dmerged * yv * sg * (1.0 - sg)).astype(bf16)
            din_ref[...] = _dot_nt(dyb, w_ref[...]).astype(din_ref.dtype)

    return pl.pallas_call(
        body, name="merge_out_bwd", grid=(t // tb,),
        in_specs=[_ANY] * len(deps) + [_rows(tb, d), _rows(tb, d), _rows(tb, D_CONV), _rows(tb, D_SG), _rows(tb, hw), _rows(tb, 3 * d),
                  _const((D_CONV, d)), _const((D_SG, d)), _const((hw, d)), _const((d, d)), _const(g2.shape)],
        out_specs=(_rows(tb, d), _rows(tb, 3 * d), _rows(tb, d), _rows(tb, d), _rows(tb, d), _rows(tb, D_CONV), _rows(tb, D_SG),
                   _rows(tb, hw), _acc((1, d))),
        out_shape=(jax.ShapeDtypeStruct((t, d), bf16), jax.ShapeDtypeStruct((t, 3 * d), bf16), jax.ShapeDtypeStruct((t, d), bf16),
                   jax.ShapeDtypeStruct((t, d), bf16), jax.ShapeDtypeStruct((t, d), bf16), jax.ShapeDtypeStruct((t, D_CONV), f32),
                   jax.ShapeDtypeStruct((t, D_SG), f32), jax.ShapeDtypeStruct((t, hw), bf16), jax.ShapeDtypeStruct((1, d), f32)),
        compiler_params=_cparams(1),
    )(*deps, dx1, om, s, um, o, gates, conv_out_w, sg_out_w, wo_p, w_out, g2)


def _attn_bwd(q, k, v, o, lse, do):
    t = q.shape[0]
    tq = _tb(t)
    hw = N_HEADS * HEAD_PAD

    def body(q_ref, k_ref, v_ref, o_ref, lse_ref, do_ref, dq_ref, dk_ref, dv_ref):
        qi = pl.program_id(1)

        @pl.when(qi == 0)
        def _():
            dk_ref[...] = jnp.zeros_like(dk_ref)
            dv_ref[...] = jnp.zeros_like(dv_ref)

        def keys(lo, hi, qv, dov, lse, delta, diagonal):
            kj, vj = k_ref[lo:hi, :], v_ref[lo:hi, :]
            p = jnp.exp(_dot_nt(qv, kj) * ATT_SCALE - lse)
            if diagonal:
                p = jnp.where(_diag_mask(tq), p, 0.0)
            ds = (p * (_dot_nt(dov, vj) - delta) * ATT_SCALE).astype(bf16)
            dk_ref[lo:hi, :] += _dot_tn(ds, qv)
            dv_ref[lo:hi, :] += _dot_tn(p.astype(bf16), dov)
            return _dot(ds, kj)

        for i in range(t // tq):
            @pl.when(qi == i)
            def _(i=i):
                qv, dov, lse = q_ref[...], do_ref[...], lse_ref[...]
                delta = jnp.sum(dov.astype(f32) * o_ref[...].astype(f32), axis=-1, keepdims=True)
                dq = keys(i * tq, (i + 1) * tq, qv, dov, lse, delta, True)
                if i > 0:
                    dq = dq + keys(0, i * tq, qv, dov, lse, delta, False)
                dq_ref[...] = dq

    blk_q = pl.BlockSpec((tq, HEAD_PAD), lambda h, i: (i, h))
    blk_kv = pl.BlockSpec((t, HEAD_PAD), lambda h, i: (0, h))
    return pl.pallas_call(
        body, name="attn_bwd", grid=(N_HEADS, t // tq),
        in_specs=[blk_q, blk_kv, blk_kv, blk_q, pl.BlockSpec((None, tq, 1), lambda h, i: (h, i, 0)), blk_q],
        out_specs=(blk_q, blk_kv, blk_kv),
        out_shape=(jax.ShapeDtypeStruct((t, hw), f32), jax.ShapeDtypeStruct((t, hw), f32), jax.ShapeDtypeStruct((t, hw), f32)),
        compiler_params=_cparams(2),
    )(q, k, v, o, lse, do)


def _mla_proj_bwd(l, dq, dk, dv, qlat, kvlat, cos_t, sin_t, gq, wuq_p, gkv, wk_p, wv_p):
    t = qlat.shape[0]
    tb = _tb(t)
    hw = N_HEADS * HEAD_PAD

    def body(dq_ref, dk_ref, dv_ref, ql_ref, kvl_ref, c_ref, s_ref, gq_ref, wq_ref, gkv_ref, wk_ref, wv_ref,
             dqb_ref, dkb_ref, dvb_ref, dql_ref, dkvl_ref, dkr_ref, dgq_ref, dgkv_ref):
        @pl.when(pl.program_id(0) == 0)
        def _():
            dgq_ref[...] = jnp.zeros_like(dgq_ref)
            dgkv_ref[...] = jnp.zeros_like(dgkv_ref)

        cos_b, sin_b = c_ref[...], s_ref[...]
        for h in range(N_HEADS):
            hs = slice(h * HEAD_PAD, (h + 1) * HEAD_PAD)
            dqh = dq_ref[:, hs]
            dqb_ref[:, hs] = (dqh * cos_b + _swap_rope_halves(dqh * sin_b)).astype(bf16)
        dqn = _dot_nt(dqb_ref[...], wq_ref[...])
        dql, dgq = _rms_bwd(ql_ref[...], gq_ref[l:l + 1, :], dqn)
        dgq_ref[...] += dgq
        dql_ref[...] = dql.astype(bf16)
        dkv_full = dk_ref[...]
        dkb = dkv_full.astype(bf16)
        dkb_ref[...] = dkb
        dkpe = dkv_full[:, 0:HEAD_PAD]
        for h in range(1, N_HEADS):
            dkpe = dkpe + dkv_full[:, h * HEAD_PAD:(h + 1) * HEAD_PAD]
        dkr_ref[...] = (dkpe * cos_b + _swap_rope_halves(dkpe * sin_b)).astype(bf16)
        dvb = dv_ref[...].astype(bf16)
        dvb_ref[...] = dvb
        dkvn = _dot_nt(dkb, wk_ref[...]) + _dot_nt(dvb, wv_ref[...])
        dkvl, dgkv = _rms_bwd(kvl_ref[...], gkv_ref[l:l + 1, :], dkvn)
        dgkv_ref[...] += dgkv
        dkvl_ref[...] = dkvl.astype(bf16)

    return pl.pallas_call(
        body, name="mla_proj_bwd", grid=(t // tb,),
        in_specs=[_rows(tb, hw), _rows(tb, hw), _rows(tb, hw), _rows(tb, Q_LORA), _rows(tb, KV_LORA), _rows(tb, HEAD_PAD),
                  _rows(tb, HEAD_PAD), _const(gq.shape), _const((Q_LORA, hw)), _const(gkv.shape), _const((KV_LORA, hw)),
                  _const((KV_LORA, hw))],
        out_specs=(_rows(tb, hw), _rows(tb, hw), _rows(tb, hw), _rows(tb, Q_LORA), _rows(tb, KV_LORA), _rows(tb, HEAD_PAD),
                   _acc((1, Q_LORA)), _acc((1, KV_LORA))),
        out_shape=(jax.ShapeDtypeStruct((t, hw), bf16), jax.ShapeDtypeStruct((t, hw), bf16), jax.ShapeDtypeStruct((t, hw), bf16),
                   jax.ShapeDtypeStruct((t, Q_LORA), bf16), jax.ShapeDtypeStruct((t, KV_LORA), bf16),
                   jax.ShapeDtypeStruct((t, HEAD_PAD), bf16), jax.ShapeDtypeStruct((1, Q_LORA), f32),
                   jax.ShapeDtypeStruct((1, KV_LORA), f32)),
        compiler_params=_cparams(1),
    )(dq, dk, dv, qlat, kvlat, cos_t, sin_t, gq, wuq_p, gkv, wk_p, wv_p)


def _sgu_bwd(l, b_in, dum, ln_g, ln_b, sg_w, bexp):
    t = b_in.shape[0]
    tb = _tb(t)
    gw = D_SG // SG_GROUPS

    def body(b_ref, dum_ref, g_ref, be_ref, w_ref, bexp_ref, db_ref, dw_ref, dsgb_ref, dlg_ref, dlb_ref, dvn_s):
        @pl.when(pl.program_id(0) == 0)
        def _():
            dw_ref[...] = jnp.zeros_like(dw_ref)
            dsgb_ref[...] = jnp.zeros_like(dsgb_ref)
            dlg_ref[...] = jnp.zeros_like(dlg_ref)
            dlb_ref[...] = jnp.zeros_like(dlb_ref)

        gl, dgl = _gelu_and_grad(b_ref[...])
        u = gl[:, :D_SG]
        vh, rstd = _ln_stats(gl[:, D_SG:])
        ln_gain = g_ref[l:l + 1, :]
        vn = (vh * ln_gain + be_ref[l:l + 1, :]).astype(bf16)
        dumv = dum_ref[...]
        tri = _tril_mask()
        ones = jnp.ones((FFN_HALO, gw), f32)
        for g in range(SG_GROUPS):
            wg = jnp.where(tri, w_ref[l, g], 0.0).astype(bf16)
            cs = slice(g * gw, (g + 1) * gw)
            for r0 in range(0, tb, SG_CHUNK):
                rs = slice(r0, r0 + SG_CHUNK)
                vblk = vn[rs, cs]
                mixed = _dot(wg, vblk) + bexp_ref[:, cs]
                db_ref[rs, cs] = (dumv[rs, cs] * mixed * dgl[rs, cs]).astype(bf16)
                dmix = dumv[rs, cs] * u[rs, cs]
                dmb = dmix.astype(bf16)
                dw_ref[g] += jnp.where(tri, _dot_nt(dmb, vblk), 0.0)
                rowsum = lax.dot_general(ones, dmix, (((1,), (1,)), ((), ())), preferred_element_type=f32,
                                         precision=lax.Precision.HIGHEST)
                dsgb_ref[g:g + 1, :] += rowsum[0:1, :]
                dvn_s[rs, cs] = _dot_tn(wg, dmb)
        dvn = dvn_s[...]
        dlg_ref[...] += _colsum(dvn * vh)
        dlb_ref[...] += _colsum(dvn)
        db_ref[:, D_SG:] = (_ln_bwd(vh, rstd, ln_gain, dvn) * dgl[:, D_SG:]).astype(bf16)

    return pl.pallas_call(
        body, name="sgu_bwd", grid=(t // tb,),
        in_specs=[_rows(tb, 2 * D_SG), _rows(tb, D_SG), _const(ln_g.shape), _const(ln_b.shape), _const(sg_w.shape),
                  _const((SG_CHUNK, D_SG))],
        out_specs=(_rows(tb, 2 * D_SG), _acc((SG_GROUPS, SG_CHUNK, SG_CHUNK)), _acc((FFN_HALO, SG_CHUNK)), _acc((1, D_SG)),
                   _acc((1, D_SG))),
        out_shape=(jax.ShapeDtypeStruct((t, 2 * D_SG), bf16), jax.ShapeDtypeStruct((SG_GROUPS, SG_CHUNK, SG_CHUNK), f32),
                   jax.ShapeDtypeStruct((FFN_HALO, SG_CHUNK), f32), jax.ShapeDtypeStruct((1, D_SG), f32),
                   jax.ShapeDtypeStruct((1, D_SG), f32)),
        scratch_shapes=[pltpu.VMEM((tb, D_SG), f32)],
        compiler_params=_cparams(1),
    )(b_in, dum, ln_g, ln_b, sg_w, bexp)


def _conv_bwd(l, a_in, z1, ds, dw_w, ln_g, ln_b):
    t = a_in.shape[0]
    tb = _tb(t)
    nb = t // tb
    per_halo = tb // CONV_HALO

    def body(a_ref, ap_ref, z1_ref, ds_ref, w_ref, g_ref, be_ref, da_ref, ddww_ref, ddwb_ref, dlg_ref, dlb_ref, win, dzb):
        i = pl.program_id(0)
        b = nb - 1 - i

        @pl.when(i == 0)
        def _():
            ddww_ref[...] = jnp.zeros_like(ddww_ref)
            ddwb_ref[...] = jnp.zeros_like(ddwb_ref)
            dlg_ref[...] = jnp.zeros_like(dlg_ref)
            dlb_ref[...] = jnp.zeros_like(dlb_ref)
            dzb[tb:tb + CONV_HALO, :] = jnp.zeros((CONV_HALO, D_CONV), f32)

        a = a_ref[...]
        val = a[:, :D_CONV]
        sg = jax.nn.sigmoid(a[:, D_CONV:])
        ap = ap_ref[...]
        win[0:CONV_HALO, :] = jnp.where(b > 0, ap[:, :D_CONV] * jax.nn.sigmoid(ap[:, D_CONV:]), 0.0)
        win[CONV_HALO:CONV_HALO + tb, :] = val * sg
        zh, rstd = _ln_stats(z1_ref[...])
        ln_gain = g_ref[l:l + 1, :]
        zl = zh * ln_gain + be_ref[l:l + 1, :]
        sgl = jax.nn.sigmoid(zl)
        dzl = ds_ref[...] * (sgl * (1.0 + zl * (1.0 - sgl)))
        dlg_ref[...] += _colsum(dzl * zh)
        dlb_ref[...] += _colsum(dzl)
        dz1 = _ln_bwd(zh, rstd, ln_gain, dzl)
        dzb[0:tb, :] = dz1
        ddwb_ref[...] += _colsum(dz1)
        dgate_f = val * sg * (1.0 - sg)
        for c0 in range(0, D_CONV, LANES):
            cs = slice(c0, c0 + LANES)
            for r0 in range(0, tb, LANES):
                d1 = dzb[r0:r0 + LANES, cs]
                acc = jnp.zeros((LANES, LANES), f32)
                for j in range(CONV_K):
                    ddww_ref[j:j + 1, cs] += _colsum(d1 * win[pl.ds(r0 + CONV_HALO - CONV_K + 1 + j, LANES), cs])
                    acc = acc + w_ref[j:j + 1, cs] * dzb[pl.ds(r0 + CONV_K - 1 - j, LANES), cs]
                da_ref[r0:r0 + LANES, cs] = (acc * sg[r0:r0 + LANES, cs]).astype(bf16)
                da_ref[r0:r0 + LANES, c0 + D_CONV:c0 + D_CONV + LANES] = (acc * dgate_f[r0:r0 + LANES, cs]).astype(bf16)
        dzb[tb:tb + CONV_HALO, :] = dzb[0:CONV_HALO, :]

    return pl.pallas_call(
        body, name="conv_bwd", grid=(nb,),
        in_specs=[_rows_rev(tb, 2 * D_CONV, nb),
                  pl.BlockSpec((CONV_HALO, 2 * D_CONV), lambda i: (jnp.maximum((nb - 1 - i) * per_halo - 1, 0), 0)),
                  _rows_rev(tb, D_CONV, nb), _rows_rev(tb, D_CONV, nb), _const((CONV_HALO, D_CONV)), _const(ln_g.shape),
                  _const(ln_b.shape)],
        out_specs=(_rows_rev(tb, 2 * D_CONV, nb), _acc((CONV_HALO, D_CONV)), _acc((1, D_CONV)), _acc((1, D_CONV)), _acc((1, D_CONV))),
        out_shape=(jax.ShapeDtypeStruct((t, 2 * D_CONV), bf16), jax.ShapeDtypeStruct((CONV_HALO, D_CONV), f32),
                   jax.ShapeDtypeStruct((1, D_CONV), f32), jax.ShapeDtypeStruct((1, D_CONV), f32), jax.ShapeDtypeStruct((1, D_CONV), f32)),
        scratch_shapes=[pltpu.VMEM((tb + CONV_HALO, D_CONV), f32), pltpu.VMEM((tb + CONV_HALO, D_CONV), f32)],
        compiler_params=_cparams(1),
    )(a_in, a_in, z1, ds, dw_w, ln_g, ln_b)


def _mix_in_bwd(l, x, g1, dxres, dsegs, w_in_p, deps=()):
    t, d = x.shape
    tb = min(512, t)

    def body(*refs):
        x_ref, g_ref, dr_ref = refs[len(deps):len(deps) + 3]
        rest = refs[len(deps) + 3:]
        dseg_refs, w_ref, dx_ref, dg_ref = rest[:len(SEGS)], rest[len(SEGS)], rest[len(SEGS) + 1], rest[len(SEGS) + 2]

        @pl.when(pl.program_id(0) == 0)
        def _():
            dg_ref[...] = jnp.zeros_like(dg_ref)

        dh = jnp.zeros((tb, d), f32)
        for (off, wd), ds_ref in zip(SEGS, dseg_refs):
            dh = dh + _dot_nt(ds_ref[...], w_ref[:, off:off + wd])
        dxn, dg = _rms_bwd(x_ref[...], g_ref[l:l + 1, :], dh)
        dg_ref[...] += dg
        dx_ref[...] = dr_ref[...] + dxn

    return pl.pallas_call(
        body, name="mix_in_bwd", grid=(t // tb,),
        in_specs=([_ANY] * len(deps) + [_rows(tb, d), _const(g1.shape), _rows(tb, d)] + [_rows(tb, wd) for _, wd in SEGS]
                  + [_const((d, D_IN_PAD))]),
        out_specs=(_rows(tb, d), _acc((1, d))),
        out_shape=(jax.ShapeDtypeStruct((t, d), f32), jax.ShapeDtypeStruct((1, d), f32)),
        compiler_params=_cparams(1),
    )(*deps, x, g1, dxres, *dsegs, w_in_p)


def _pick_block(n, cap=512):
    for b in (cap, 384, 256, 128):
        if b <= cap and n % b == 0:
            return b
    return n


def _wgrad(a, b, name, out_dtype=f32):
    t, kdim = a.shape
    n = b.shape[1]
    bk, bn = _pick_block(kdim), _pick_block(n, cap=1024)

    def body(a_ref, b_ref, o_ref):
        o_ref[...] = _dot_tn(a_ref[...], b_ref[...]).astype(out_dtype)

    return pl.pallas_call(
        body, name=name, grid=(kdim // bk, n // bn),
        in_specs=[pl.BlockSpec((t, bk), lambda i, j: (0, i)), pl.BlockSpec((t, bn), lambda i, j: (0, j))],
        out_specs=pl.BlockSpec((bk, bn), lambda i, j: (i, j)),
        out_shape=jax.ShapeDtypeStruct((kdim, n), out_dtype),
        compiler_params=_cparams(2),
    )(a, b)


def _wgrad_multi(a, bs, name, deps=()):
    t, kdim = a.shape
    bk = _pick_block(kdim)
    nb = len(bs)

    def body(*refs):
        a_ref, refs = refs[len(deps)], refs[len(deps) + 1:]
        at = a_ref[...].T
        for b_ref, o_ref in zip(refs[:nb], refs[nb:]):
            o_ref[...] = _dot(at, b_ref[...])

    return pl.pallas_call(
        body, name=name, grid=(kdim // bk,),
        in_specs=[_ANY] * len(deps) + [pl.BlockSpec((t, bk), lambda i: (0, i))] + [_const(b.shape) for b in bs],
        out_specs=tuple(pl.BlockSpec((bk, b.shape[1]), lambda i: (i, 0)) for b in bs),
        out_shape=tuple(jax.ShapeDtypeStruct((kdim, b.shape[1]), f32) for b in bs),
        compiler_params=_cparams(1),
    )(*deps, a, *bs)


_BC1 = 1.0 - ADAM_B1 ** ADAM_STEP
_BC2 = 1.0 - ADAM_B2 ** ADAM_STEP


def _adam_math(g, w, m, v):
    nm = ADAM_B1 * m + (1.0 - ADAM_B1) * g
    nv = ADAM_B2 * v + (1.0 - ADAM_B2) * (g * g)
    delta = -ADAM_LR * ((nm / _BC1) / (jnp.sqrt(nv / _BC2) + ADAM_EPS) + ADAM_WD * w)
    return delta, nm, nv


def _slot_sum(r_ref, index=()):
    g = r_ref[(0,) + index].astype(f32)
    for s in range(1, N_DEV):
        g = g + r_ref[(s,) + index].astype(f32)
    return g


def _adamw_shard(l, recv, w, m, v, prev, name, deps=()):
    _, k, ns = recv.shape
    rb = next((c for c in (256, 192, 176, 128) if k % c == 0), k)
    blk = pl.BlockSpec((None, rb, ns), lambda i: (l, i, 0))

    def body(r_ref, w_ref, m_ref, v_ref, *rest):
        g_ref, d_ref, nm_ref, nv_ref = rest[-4:]
        g = _slot_sum(r_ref)
        g_ref[...] = g
        d_ref[...], nm_ref[...], nv_ref[...] = _adam_math(g, w_ref[...], m_ref[...], v_ref[...])

    out = jax.ShapeDtypeStruct(w.shape, f32)
    n_prev = 0 if prev is None else 4
    return pl.pallas_call(
        body, name=name, grid=(k // rb,),
        in_specs=[pl.BlockSpec((N_DEV, rb, ns), lambda i: (0, i, 0)), blk, blk, blk] + [_ANY] * (n_prev + len(deps)),
        out_specs=(blk, blk, blk, blk), out_shape=(out, out, out, out),
        input_output_aliases={4 + j: j for j in range(n_prev)},
        compiler_params=_cparams(1),
    )(recv, w, m, v, *(prev or ()), *deps)


def _adamw_rep(recvs, ws, ms, vs):
    depth = len(recvs)
    nt = len(REP_VECS)

    def body(*refs):
        r_refs = refs[:depth]
        w_refs, m_refs, v_refs = (refs[depth + i * nt:depth + (i + 1) * nt] for i in range(3))
        outs = refs[depth + 3 * nt:]
        for ti, (_, row, width, nrows) in enumerate(REP_VECS):
            for l in range(depth):
                g = r_refs[l][0, row:row + nrows, 0:width]
                for s in range(1, N_DEV):
                    g = g + r_refs[l][s, row:row + nrows, 0:width]
                pick = (lambda ref: ref[l]) if nrows > 1 else (lambda ref: ref[l:l + 1, :])
                delta, nm, nv = _adam_math(g, pick(w_refs[ti]), pick(m_refs[ti]), pick(v_refs[ti]))
                for o_ref, val in zip(outs[4 * ti:4 * ti + 4], (g, delta, nm, nv)):
                    if nrows > 1:
                        o_ref[l] = val
                    else:
                        o_ref[l:l + 1, :] = val

    ins = tuple(recvs) + tuple(ws) + tuple(ms) + tuple(vs)
    out_shape = tuple(jax.ShapeDtypeStruct(w.shape, f32) for w in ws for _ in range(4))
    return pl.pallas_call(
        body, name="adamw_rep", grid=(1,),
        in_specs=[_whole(a) for a in ins], out_specs=tuple(_whole(o) for o in out_shape), out_shape=out_shape,
        compiler_params=_cparams(1),
    )(*ins)


def _adamw_sg_w(recvs, w, m, v):
    depth = len(recvs)

    def body(*refs):
        r_refs = refs[:depth]
        w_ref, m_ref, v_ref = refs[depth:depth + 3]
        outs = refs[depth + 3:]
        for l in range(depth):
            for gi in range(SG_GROUPS):
                g = _slot_sum(r_refs[l], (gi,))
                delta, nm, nv = _adam_math(g, w_ref[l, gi], m_ref[l, gi], v_ref[l, gi])
                for o_ref, val in zip(outs, (g, delta, nm, nv)):
                    o_ref[l, gi] = val

    ins = tuple(recvs) + (w, m, v)
    out = jax.ShapeDtypeStruct(w.shape, f32)
    return pl.pallas_call(
        body, name="adamw_sg_w", grid=(1,),
        in_specs=[_whole(a) for a in ins], out_specs=tuple(_whole(out) for _ in range(4)), out_shape=(out,) * 4,
        compiler_params=_cparams(1),
    )(*ins)


def _rope_tables(positions):
    t = positions.shape[0]
    inv = 10000.0 ** (-jnp.arange(0, QK_ROPE, 2, dtype=f32) / QK_ROPE)
    ang = positions.astype(f32)[:, None] * inv
    cos, sin = jnp.cos(ang), jnp.sin(ang)
    tail = jnp.zeros((t, HEAD_PAD - KR_LO - QK_ROPE), f32)
    cos_t = jnp.concatenate([jnp.ones((t, KR_LO), f32), cos, cos, tail], axis=1)
    sin_t = jnp.concatenate([jnp.zeros((t, KR_LO), f32), -sin, sin, tail], axis=1)
    return cos_t, sin_t


def _bias_over_channels(sg_b_l):
    return jnp.broadcast_to(sg_b_l.T[:, :, None], (SG_CHUNK, SG_GROUPS, D_SG // SG_GROUPS)).reshape(SG_CHUNK, D_SG)


def _mixer_weights(gathered, rep, l):
    conv_out_w, sg_out_w, wuq_p, wk_p, wv_p, wo_p, conv_dw_w, ffn_dw_w = _asm_small(
        gathered["conv_out_w"], gathered["sg_out_w"], gathered["mla_w_uq"], gathered["mla_w_ukv"], gathered["mla_w_o"],
        gathered["conv_dw_w"], gathered["ffn_dw_w"])
    g_out = gathered["w_out"]
    w = dict(rep)
    w.update(
        l=l, w_in_p=_asm_w_in(gathered["w_in"]),
        conv_out_w=conv_out_w, sg_out_w=sg_out_w, wuq_p=wuq_p, wk_p=wk_p, wv_p=wv_p, wo_p=wo_p, conv_dw_w_p=conv_dw_w, ffn_dw_w_p=ffn_dw_w,
        w_out=g_out.reshape(g_out.shape[0] * g_out.shape[1], g_out.shape[2]),
        bexp=_bias_over_channels(rep["sg_b"][l]))
    return w


def _ffn_weights(gathered):
    stack = lambda g: g.reshape(g.shape[0] * g.shape[1], g.shape[2])
    return dict(w_up=stack(gathered["ffn_w_up"]), w_down=stack(gathered["ffn_w_down"]))


def _mixer_fwd(x, w, cos_t, sin_t):
    l = w["l"]
    h, a_in, b_in, qlat, kvlat, krope, gates = _mix_in_fwd(l, x, w["mix_pre_g"], w["w_in_p"])
    s, z1 = _conv_fwd(l, a_in, w["conv_dw_w_p"], w["conv_dw_b"], w["conv_ln_g"], w["conv_ln_b"])
    um = _sgu_fwd(l, b_in, w["sg_ln_g"], w["sg_ln_b"], w["sg_w"], w["bexp"])
    q, k, v, qn, kvn = _mla_proj_fwd(l, qlat, kvlat, krope, cos_t, sin_t, w["mla_q_norm_g"], w["wuq_p"], w["mla_kv_norm_g"],
                                     w["wk_p"], w["wv_p"])
    o, lse = _attn_fwd(q, k, v)
    x1, merged, om = _merge_out_fwd(l, x, s, um, o, gates, w["conv_out_w"], w["sg_out_w"], w["wo_p"], w["w_out"], w["mix_post_g"])
    saved = dict(x=x, h=h, a_in=a_in, b_in=b_in, qlat=qlat, kvlat=kvlat, gates=gates, s=s, z1=z1, um=um, q=q, k=k, v=v, qn=qn,
                 kvn=kvn, o=o, lse=lse, x1=x1, merged=merged, om=om)
    return x1, saved


def _ffn_layer_fwd(x1, w):
    x2, h2, z, zc, act, f = _ffn_fwd(w["l"], x1, w["ffn_pre_g"], w["w_up"], w["ffn_dw_w_p"], w["ffn_dw_b"], w["w_down"], w["ffn_post_g"])
    return x2, dict(h2=h2, z=z, zc=zc, act=act, f=f)


def _ffn_layer_bwd(dx2, sv, w, deps=()):
    l = w["l"]
    vec = {}
    dx1, df, dz, vec["ffn_post_g"], vec["ffn_pre_g"], vec["ffn_dw_b"], vec["d_fdw"] = _ffn_bwd(
        l, dx2, sv["x1"], sv["f"], sv["z"], sv["zc"], w["w_up"], w["ffn_dw_w_p"], w["w_down"], w["ffn_pre_g"], w["ffn_post_g"],
        deps=deps)
    d_down = _wgrad(sv["act"], df, "wgrad_ffn_down", bf16)
    d_up_t = _wgrad(dz, sv["h2"], "wgrad_ffn_up", bf16)
    unstack = lambda g: g.reshape(N_DEV, g.shape[0] // N_DEV, g.shape[1])
    send = dict(ffn_w_up=unstack(d_up_t), ffn_w_down=unstack(d_down))
    return dx1, send, vec


def _mixer_bwd(dx1, sv, w, cos_t, sin_t, shard_cols, vec, start, deps=()):
    l = w["l"]
    d_fdw = vec.pop("d_fdw")
    dom, dgates, dya, dyb, dyc, ds, dum, do, vec["mix_post_g"] = _merge_out_bwd(
        l, dx1, sv["om"], sv["s"], sv["um"], sv["o"], sv["gates"], w["conv_out_w"], w["sg_out_w"], w["wo_p"], w["w_out"], w["mix_post_g"],
        deps=deps)
    d_out = _wgrad(sv["merged"], dom, "wgrad_w_out", bf16)
    d_co = _wgrad(sv["s"], dya, "wgrad_conv_out")
    d_so = _wgrad(sv["um"], dyb, "wgrad_sg_out")
    d_wo = _wgrad(sv["o"], dyc, "wgrad_w_o")

    dq, dk, dv = _attn_bwd(sv["q"], sv["k"], sv["v"], sv["o"], sv["lse"], do)
    dqb, dkb, dvb, dqlat, dkvlat, dkrope, vec["mla_q_norm_g"], vec["mla_kv_norm_g"] = _mla_proj_bwd(
        l, dq, dk, dv, sv["qlat"], sv["kvlat"], cos_t, sin_t, w["mla_q_norm_g"], w["wuq_p"], w["mla_kv_norm_g"], w["wk_p"], w["wv_p"])
    d_uq = _wgrad(sv["qn"], dqb, "wgrad_w_uq")
    d_uk, d_uv = _wgrad_multi(sv["kvn"], [dkb, dvb], "wgrad_w_ukv")

    db_in, dsg_w, dsgb, vec["sg_ln_g"], vec["sg_ln_b"] = _sgu_bwd(l, sv["b_in"], dum, w["sg_ln_g"], w["sg_ln_b"], w["sg_w"], w["bexp"])
    da_in, d_cdw, vec["conv_dw_b"], vec["conv_ln_g"], vec["conv_ln_b"] = _conv_bwd(
        l, sv["a_in"], sv["z1"], ds, w["conv_dw_w_p"], w["conv_ln_g"], w["conv_ln_b"])

    send = {}
    (send["conv_out_w"], send["sg_out_w"], send["mla_w_uq"], send["mla_w_ukv"], send["mla_w_o"], send["conv_dw_w"],
     send["ffn_dw_w"]) = _dis_small(d_co, d_so, d_uq, d_uk, d_uv, d_wo, d_cdw, d_fdw)
    send["w_out"] = d_out.reshape(N_DEV, d_out.shape[0] // N_DEV, d_out.shape[1])
    token = start(MIX_SMALL, send)

    dsegs = (da_in, db_in, dqlat, dkvlat, dkrope, dgates)
    d_in_segs = list(_wgrad_multi(sv["h"], dsegs[:5], "wgrad_w_in_abqkr", deps=token)) + [_wgrad(sv["h"], dgates, "wgrad_w_in_g")]
    token = start(("w_in",), dict(w_in=_dis_w_in(d_in_segs, shard_cols["w_in"])))
    dx, vec["mix_pre_g"] = _mix_in_bwd(l, sv["x"], w["mix_pre_g"], dx1, dsegs, w["w_in_p"], deps=token)
    rep_pack = _pack_rep([vec[n] for n, _, _, _ in REP_VECS[:-1]], dsgb)
    return dx, rep_pack, dsg_w


def kernel(x, positions, mix_pre_g, mix_post_g, ffn_pre_g, ffn_post_g, w_in, conv_dw_w, conv_dw_b, conv_ln_g, conv_ln_b, conv_out_w, sg_ln_g, sg_ln_b, sg_w, sg_b, sg_out_w, mla_q_norm_g, mla_w_uq, mla_kv_norm_g, mla_w_ukv, mla_w_o, w_out, ffn_w_up, ffn_dw_w, ffn_dw_b, ffn_w_down, loss_target, m_mix_pre_g, m_mix_post_g, m_ffn_pre_g, m_ffn_post_g, m_w_in, m_conv_dw_w, m_conv_dw_b, m_conv_ln_g, m_conv_ln_b, m_conv_out_w, m_sg_ln_g, m_sg_ln_b, m_sg_w, m_sg_b, m_sg_out_w, m_mla_q_norm_g, m_mla_w_uq, m_mla_kv_norm_g, m_mla_w_ukv, m_mla_w_o, m_w_out, m_ffn_w_up, m_ffn_dw_w, m_ffn_dw_b, m_ffn_w_down, v_mix_pre_g, v_mix_post_g, v_ffn_pre_g, v_ffn_post_g, v_w_in, v_conv_dw_w, v_conv_dw_b, v_conv_ln_g, v_conv_ln_b, v_conv_out_w, v_sg_ln_g, v_sg_ln_b, v_sg_w, v_sg_b, v_sg_out_w, v_mla_q_norm_g, v_mla_w_uq, v_mla_kv_norm_g, v_mla_w_ukv, v_mla_w_o, v_w_out, v_ffn_w_up, v_ffn_dw_w, v_ffn_dw_b, v_ffn_w_down):
    args = (x, positions, mix_pre_g, mix_post_g, ffn_pre_g, ffn_post_g, w_in, conv_dw_w, conv_dw_b, conv_ln_g, conv_ln_b, conv_out_w, sg_ln_g, sg_ln_b, sg_w, sg_b, sg_out_w, mla_q_norm_g, mla_w_uq, mla_kv_norm_g, mla_w_ukv, mla_w_o, w_out, ffn_w_up, ffn_dw_w, ffn_dw_b, ffn_w_down, loss_target, m_mix_pre_g, m_mix_post_g, m_ffn_pre_g, m_ffn_post_g, m_w_in, m_conv_dw_w, m_conv_dw_b, m_conv_ln_g, m_conv_ln_b, m_conv_out_w, m_sg_ln_g, m_sg_ln_b, m_sg_w, m_sg_b, m_sg_out_w, m_mla_q_norm_g, m_mla_w_uq, m_mla_kv_norm_g, m_mla_w_ukv, m_mla_w_o, m_w_out, m_ffn_w_up, m_ffn_dw_w, m_ffn_dw_b, m_ffn_w_down, v_mix_pre_g, v_mix_post_g, v_ffn_pre_g, v_ffn_post_g, v_w_in, v_conv_dw_w, v_conv_dw_b, v_conv_ln_g, v_conv_ln_b, v_conv_out_w, v_sg_ln_g, v_sg_ln_b, v_sg_w, v_sg_b, v_sg_out_w, v_mla_q_norm_g, v_mla_w_uq, v_mla_kv_norm_g, v_mla_w_ukv, v_mla_w_o, v_w_out, v_ffn_w_up, v_ffn_dw_w, v_ffn_dw_b, v_ffn_w_down)
    n_in = len(IN_NAMES)
    a = dict(zip(IN_NAMES, args[:n_in]))
    target = args[n_in]
    n_w = len(WEIGHTS)
    m_in = dict(zip(WEIGHTS, args[n_in + 1:n_in + 1 + n_w]))
    v_in = dict(zip(WEIGHTS, args[n_in + 1 + n_w:n_in + 1 + 2 * n_w]))
    depth = a["mix_pre_g"].shape[0]
    rep = {n: a[n] for n in WEIGHTS if n not in SHARDED}
    shard_cols = {n: a[n].shape[2] for n in SHARDED}

    def view(arr, n):
        return jnp.swapaxes(arr, 1, 2) if n in TRANSPOSED else arr

    w_sh = {n: view(a[n], n) for n in SHARDED}
    m_sh = {n: view(m_in[n], n) for n in SHARDED}
    v_sh = {n: view(v_in[n], n) for n in SHARDED}

    def gather_groups(l):
        return (("mix", MIX_GROUP), ("ffn", FFN_BIG))

    starts, token = {}, ()
    for l in range(depth):
        for tag, group in gather_groups(l):
            wire = [w_sh[n][l] if n in WIRE_F32 else w_sh[n][l].astype(bf16) for n in group]
            starts[l, tag] = _gather_start(wire, "gather_%s_weights_%d" % (tag, l), deps=token)
            token = (starts[l, tag]["token"],)

    cos_t, sin_t = _rope_tables(a["positions"][0])
    xl = a["x"][0]
    ws, saved = [], []
    passing = {}

    def pass_on(l, tag, after):
        passing[l, tag] = _gather_forward(starts[l, tag], after=after)
        return (passing[l, tag]["token"],)

    after = pass_on(0, gather_groups(0)[0][0], token)
    for l in range(depth):
        groups = gather_groups(l)
        got = dict(zip(groups[0][1], _gather_wait(passing[l, groups[0][0]], after=after)))
        w = _mixer_weights(got, rep, l)
        x1, sv = _mixer_fwd(xl, w, cos_t, sin_t)
        after = (x1,)
        if len(groups) > 1:
            after = pass_on(l, groups[1][0], after)
        if l + 1 < depth:
            after = pass_on(l + 1, gather_groups(l + 1)[0][0], after)
        if len(groups) > 1:
            got = dict(zip(groups[1][1], _gather_wait(passing[l, groups[1][0]], after=after)))
        w.update(_ffn_weights(got))
        xl, sv_ffn = _ffn_layer_fwd(x1, w)
        sv.update(sv_ffn)
        ws.append(w)
        saved.append(sv)
        after = (xl,)
    dx, loss_part = _loss_fwd_bwd(xl, target[0])
    loss = lax.psum(loss_part, AXES)

    outs = {}
    rep_recvs, sgw_recvs = [None] * depth, [None] * depth

    def finish(l, parts, after):
        for names, handle in parts:
            got = _exchange_wait(handle, after=after)
            for n, recv in zip(names, got):
                outs[n] = _adamw_shard(l, recv, w_sh[n], m_sh[n], v_sh[n], outs.get(n), "adamw_" + n)
            after = tuple(outs[n][0] for n in names) or after
        rep_recvs[l], sgw_recvs[l] = got[0], got[1]
        return after

    token, prev, after = (), None, ()
    for l in reversed(range(depth)):
        dx1, send_ffn, vec = _ffn_layer_bwd(dx, saved[l], ws[l], deps=token)
        parts = []

        def start(names, send, extra=(), l=l, parts=parts):
            tag = names[0] if names else "rep"
            handle = _exchange_start([(send[n], "scatter") for n in names] + list(extra), "exchange_%s_grads_%d" % (tag, l))
            parts.append((names, handle))
            return (handle["token"],)

        token = start(FFN_BIG, send_ffn)
        dx, rep_pack, dsg_w = _mixer_bwd(dx1, saved[l], ws[l], cos_t, sin_t, shard_cols, vec, start, deps=token)
        token = start((), {}, extra=[(rep_pack, "gather"), (dsg_w, "gather")])
        if prev is not None:
            after = finish(*prev, after=(dx,) + token)
        prev = (l, parts)
    finish(*prev, after=after)
    vec_names = [n for n, _, _, _ in REP_VECS]
    rep_outs = _adamw_rep(rep_recvs, [a[n] for n in vec_names], [m_in[n] for n in vec_names], [v_in[n] for n in vec_names])
    for i, n in enumerate(vec_names):
        outs[n] = rep_outs[4 * i:4 * i + 4]
    outs["sg_w"] = _adamw_sg_w(sgw_recvs, a["sg_w"], m_in["sg_w"], v_in["sg_w"])
    for n in TRANSPOSED:
        outs[n] = tuple(view(o, n) for o in outs[n])

    grad_w, delta_w, new_m, new_v = ([outs[n][j] for n in WEIGHTS] for j in range(4))
    return (loss, dx[None], *grad_w, *delta_w, *new_m, *new_v)
```

```python
import math

import jax
import jax.numpy as jnp
from jax import lax
from jax.experimental import pallas as pl
from jax.experimental.pallas import tpu as pltpu

f32 = jnp.float32
bf16 = jnp.bfloat16

N_DEV = 8
AXES = ("x", "y", "c")
EPS = 1e-6
D_CONV = 512
CONV_K = 31
CONV_HALO = 32
D_SG = 512
SG_GROUPS = 4
SG_CHUNK = 128
N_HEADS = 8
QK_NOPE = 64
QK_ROPE = 32
V_HEAD = 64
HEAD_PAD = 128
Q_LORA = 384
KV_LORA = 256
D_FF = 2816
FFN_K = 3
FFN_HALO = 8
ATT_SCALE = (QK_NOPE + QK_ROPE) ** -0.5
ATT_BLOCK = 256
NEG = float(jnp.finfo(jnp.float32).min)

ADAM_LR = 0.001
ADAM_B1 = 0.9
ADAM_B2 = 0.999
ADAM_EPS = 1e-08
ADAM_WD = 0.01
ADAM_STEP = 10

LANES = 128
VMEM_MB = 56

REF_CUTS = (0, 1024, 2048, 2432, 2688, 2720, 5792)
SEGS = ((0, 1024), (1024, 1024), (2048, 384), (2432, 256), (2688, 128), (2816, 3072))
D_IN = 5792
D_IN_PAD = 5888
KR_LO = 64
SEG_INNER = (0, 0, 0, 0, KR_LO, 0)

IN_NAMES = ['x', 'positions', 'mix_pre_g', 'mix_post_g', 'ffn_pre_g', 'ffn_post_g', 'w_in', 'conv_dw_w', 'conv_dw_b', 'conv_ln_g', 'conv_ln_b', 'conv_out_w', 'sg_ln_g', 'sg_ln_b', 'sg_w', 'sg_b', 'sg_out_w', 'mla_q_norm_g', 'mla_w_uq', 'mla_kv_norm_g', 'mla_w_ukv', 'mla_w_o', 'w_out', 'ffn_w_up', 'ffn_dw_w', 'ffn_dw_b', 'ffn_w_down']
WEIGHTS = IN_NAMES[2:]
SHARDED = ("w_in", "conv_dw_w", "conv_out_w", "sg_out_w", "mla_w_uq", "mla_w_ukv", "mla_w_o", "w_out", "ffn_w_up", "ffn_dw_w",
           "ffn_w_down")
FFN_BIG = ("ffn_w_up", "ffn_w_down")
MIX_GROUP = tuple(n for n in SHARDED if n not in FFN_BIG)
MIX_SMALL = tuple(n for n in MIX_GROUP if n != "w_in")
TRANSPOSED = ("ffn_w_up",)
WIRE_F32 = ("conv_dw_w", "ffn_dw_w")
REP_VECS = (("mix_pre_g", 0, 1024, 1), ("mix_post_g", 1, 1024, 1), ("ffn_pre_g", 2, 1024, 1), ("ffn_post_g", 3, 1024, 1),
            ("conv_dw_b", 4, 512, 1), ("conv_ln_g", 5, 512, 1), ("conv_ln_b", 6, 512, 1), ("sg_ln_g", 7, 512, 1),
            ("sg_ln_b", 8, 512, 1), ("mla_q_norm_g", 9, 384, 1), ("mla_kv_norm_g", 10, 256, 1), ("ffn_dw_b", 11, 5632, 1),
            ("sg_b", 12, 128, 4))
REP_ROWS = 16
REP_W = 5632


def _cparams(n_axes):
    return pltpu.CompilerParams(dimension_semantics=("arbitrary",) * n_axes, vmem_limit_bytes=VMEM_MB * 2 ** 20)


def _rows(tb, n):
    return pl.BlockSpec((tb, n), lambda i: (i, 0))


def _rows_rev(tb, n, nb):
    return pl.BlockSpec((tb, n), lambda i: (nb - 1 - i, 0))


def _const(shape):
    nd = len(shape)
    return pl.BlockSpec(shape, lambda *_: (0,) * nd, pipeline_mode=pl.Buffered(1))


def _acc(shape):
    nd = len(shape)
    return pl.BlockSpec(shape, lambda *_: (0,) * nd)


def _whole(arr):
    return pl.BlockSpec(arr.shape, lambda *_: (0,) * arr.ndim)


def _dot(a, b):
    return jnp.dot(a, b, preferred_element_type=f32)


def _dot_nt(a, b):
    return lax.dot_general(a, b, (((1,), (1,)), ((), ())), preferred_element_type=f32)


def _dot_tn(a, b):
    return lax.dot_general(a, b, (((0,), (0,)), ((), ())), preferred_element_type=f32)


def _mean(x):
    return jnp.mean(x, axis=-1, keepdims=True)


def _colsum(x):
    return jnp.sum(x, axis=0, keepdims=True)


def _rms_fwd(x, g):
    return x * lax.rsqrt(_mean(x * x) + EPS) * g


def _rms_bwd(x, g, dy):
    r = lax.rsqrt(_mean(x * x) + EPS)
    n = x * r
    dn = dy * g
    return r * (dn - n * _mean(dn * n)), _colsum(dy * n)


def _ln_stats(x):
    mu = _mean(x)
    d = x - mu
    rstd = lax.rsqrt(_mean(d * d) + EPS)
    return d * rstd, rstd


def _ln_bwd(xhat, rstd, g, dy):
    dxh = dy * g
    return rstd * (dxh - _mean(dxh) - xhat * _mean(dxh * xhat))


_GELU_C0 = math.sqrt(2.0 / math.pi)
_GELU_C1 = 0.044715


def _gelu(x):
    t = jnp.tanh(_GELU_C0 * (x + _GELU_C1 * (x * x * x)))
    return 0.5 * x * (1.0 + t)


def _gelu_and_grad(x):
    x2 = x * x
    t = jnp.tanh(_GELU_C0 * (x + _GELU_C1 * (x2 * x)))
    g = 0.5 * x * (1.0 + t)
    dg = 0.5 * (1.0 + t) + 0.5 * x * (1.0 - t * t) * (_GELU_C0 * (1.0 + 3.0 * _GELU_C1 * x2))
    return g, dg


def _swap_rope_halves(x):
    n = x.shape[1]
    half = QK_ROPE // 2
    lane = lax.broadcasted_iota(jnp.int32, x.shape, 1) % HEAD_PAD
    first = (lane >= KR_LO) & (lane < KR_LO + half)
    second = (lane >= KR_LO + half) & (lane < KR_LO + QK_ROPE)
    return jnp.where(first, pltpu.roll(x, n - half, 1), jnp.where(second, pltpu.roll(x, half, 1), 0.0))


def _tb(t):
    return min(256, t)


_HBM = pl.BlockSpec(memory_space=pltpu.HBM)
_SEM = pl.BlockSpec(memory_space=pltpu.SEMAPHORE)
_ANY = pl.BlockSpec(memory_space=pl.ANY)
_EFFECT = pltpu.SideEffectType.DATAFLOW_SIDE_EFFECTING


def _exchange_copies(modes, ins, lands, send_sems, recv_sems, loc_sems):
    x, y, c = lax.axis_index("x"), lax.axis_index("y"), lax.axis_index("c")
    me = 4 * x + 2 * y + c
    copies = []
    for a, mode in enumerate(modes):
        def src(dst_index, a=a, mode=mode):
            return ins[a].at[dst_index] if mode == "scatter" else ins[a]
        copies.append(pltpu.make_async_copy(src(me), lands[a].at[me], loc_sems.at[a]))
        for k in range(1, N_DEV):
            px = 1 - x if (k >> 2) & 1 else x
            py = 1 - y if (k >> 1) & 1 else y
            pc = 1 - c if k & 1 else c
            copies.append(pltpu.make_async_remote_copy(
                src_ref=src(4 * px + 2 * py + pc), dst_ref=lands[a].at[me],
                send_sem=send_sems.at[a * (N_DEV - 1) + k - 1], recv_sem=recv_sems.at[a * (N_DEV - 1) + k - 1],
                device_id=(px, py, pc), device_id_type=pl.DeviceIdType.MESH))
    return copies


def _exchange_start(ops, name, deps=()):
    n = len(ops)
    arrs = [arr for arr, _ in ops]
    modes = [mode for _, mode in ops]
    lands = [lax.empty((N_DEV,) + arr.shape if mode == "gather" else arr.shape, arr.dtype) for arr, mode in ops]

    def body(*refs):
        ins, land_refs = refs[:n], refs[n:2 * n]
        send_sems, recv_sems, loc_sems = refs[2 * n + len(deps):2 * n + len(deps) + 3]
        for cp in _exchange_copies(modes, ins, land_refs, send_sems, recv_sems, loc_sems):
            cp.start()
        refs[-1][...] = jnp.zeros((8, LANES), f32)

    n_rem = n * (N_DEV - 1)
    res = pl.pallas_call(
        body, name=name,
        out_shape=(pltpu.SemaphoreType.DMA((n_rem,)), pltpu.SemaphoreType.DMA((n_rem,)), pltpu.SemaphoreType.DMA((n,)),
                   *[pltpu.HBM(x.shape, x.dtype) for x in arrs + lands], jax.ShapeDtypeStruct((8, LANES), f32)),
        in_specs=[_HBM] * (2 * n) + [_ANY] * len(deps),
        out_specs=(_SEM, _SEM, _SEM, *[_HBM] * (2 * n), pl.BlockSpec(memory_space=pltpu.VMEM)),
        input_output_aliases={i: 3 + i for i in range(2 * n)},
        compiler_params=pltpu.CompilerParams(has_side_effects=_EFFECT),
    )(*[pltpu.with_memory_space_constraint(x, pltpu.HBM) for x in arrs + lands], *deps)
    return dict(modes=modes, sems=res[:3], thru=res[3:3 + 2 * n], token=res[-1], name=name)


def _exchange_wait(handle, after=()):
    modes, thru = handle["modes"], handle["thru"]
    n = len(modes)

    def body(*refs):
        ins, land_refs = refs[:n], refs[n:2 * n]
        send_sems, recv_sems, loc_sems = refs[2 * n:2 * n + 3]
        for cp in _exchange_copies(modes, ins, land_refs, send_sems, recv_sems, loc_sems):
            cp.wait()

    res = pl.pallas_call(
        body, name=handle["name"] + "_wait",
        out_shape=tuple(pltpu.HBM(x.shape, x.dtype) for x in thru),
        in_specs=[_HBM] * (2 * n) + [_SEM] * 3 + [_ANY] * len(after),
        out_specs=tuple([_HBM] * (2 * n)),
        input_output_aliases={i: i for i in range(2 * n)},
        compiler_params=pltpu.CompilerParams(has_side_effects=_EFFECT),
    )(*thru, *handle["sems"], *after)
    return res[n:]


_SAME_CORE_PEERS = ((1, 0), (0, 1), (1, 1))


def _gather_copies_one(ins, lands, send_sems, recv_sems, loc_sems):
    x, y, c = lax.axis_index("x"), lax.axis_index("y"), lax.axis_index("c")
    me = 4 * x + 2 * y + c
    targets = [(x, y, 1 - c)] + [(1 - x if fx else x, 1 - y if fy else y, c) for fx, fy in _SAME_CORE_PEERS]
    copies = []
    for a in range(len(ins)):
        copies.append(pltpu.make_async_copy(ins[a], lands[a].at[me], loc_sems.at[a]))
        for j, target in enumerate(targets):
            copies.append(pltpu.make_async_remote_copy(
                src_ref=ins[a], dst_ref=lands[a].at[me], send_sem=send_sems.at[4 * a + j], recv_sem=recv_sems.at[4 * a + j],
                device_id=target, device_id_type=pl.DeviceIdType.MESH))
    return copies


def _gather_copies_two(lands, send_sems, recv_sems):
    x, y, c = lax.axis_index("x"), lax.axis_index("y"), lax.axis_index("c")
    copies = []
    for a in range(len(lands)):
        for j, (fx, fy) in enumerate(_SAME_CORE_PEERS):
            slot = 4 * (1 - x if fx else x) + 2 * (1 - y if fy else y) + c
            copies.append(pltpu.make_async_remote_copy(
                src_ref=lands[a].at[slot], dst_ref=lands[a].at[slot], send_sem=send_sems.at[3 * a + j],
                recv_sem=recv_sems.at[3 * a + j], device_id=(x, y, 1 - c), device_id_type=pl.DeviceIdType.MESH))
    return copies


def _gather_start(arrs, name, deps=()):
    n = len(arrs)
    lands = [lax.empty((N_DEV,) + arr.shape, arr.dtype) for arr in arrs]

    def body(*refs):
        ins, land_refs = refs[:n], refs[n:2 * n]
        send_sems, recv_sems, loc_sems = refs[2 * n + len(deps):2 * n + len(deps) + 3]
        for cp in _gather_copies_one(ins, land_refs, send_sems, recv_sems, loc_sems):
            cp.start()
        refs[-1][...] = jnp.zeros((8, LANES), f32)

    res = pl.pallas_call(
        body, name=name,
        out_shape=(pltpu.SemaphoreType.DMA((4 * n,)), pltpu.SemaphoreType.DMA((4 * n,)), pltpu.SemaphoreType.DMA((n,)),
                   *[pltpu.HBM(x.shape, x.dtype) for x in arrs + lands], jax.ShapeDtypeStruct((8, LANES), f32)),
        in_specs=[_HBM] * (2 * n) + [_ANY] * len(deps),
        out_specs=(_SEM, _SEM, _SEM, *[_HBM] * (2 * n), pl.BlockSpec(memory_space=pltpu.VMEM)),
        input_output_aliases={i: 3 + i for i in range(2 * n)},
        compiler_params=pltpu.CompilerParams(has_side_effects=_EFFECT),
    )(*[pltpu.with_memory_space_constraint(x, pltpu.HBM) for x in arrs + lands], *deps)
    return dict(n=n, sems=res[:3], thru=res[3:3 + 2 * n], token=res[-1], name=name)


def _gather_forward(handle, after=()):
    n, thru = handle["n"], handle["thru"]

    def body(*refs):
        ins, land_refs = refs[:n], refs[n:2 * n]
        send_one, recv_one, loc_sems = refs[2 * n:2 * n + 3]
        send_two, recv_two = refs[2 * n + 3 + len(after):2 * n + 5 + len(after)]
        for cp in _gather_copies_one(ins, land_refs, send_one, recv_one, loc_sems):
            cp.wait()
        for cp in _gather_copies_two(land_refs, send_two, recv_two):
            cp.start()
        refs[-1][...] = jnp.zeros((8, LANES), f32)

    res = pl.pallas_call(
        body, name=handle["name"] + "_forward",
        out_shape=(pltpu.SemaphoreType.DMA((3 * n,)), pltpu.SemaphoreType.DMA((3 * n,)),
                   *[pltpu.HBM(x.shape, x.dtype) for x in thru], jax.ShapeDtypeStruct((8, LANES), f32)),
        in_specs=[_HBM] * (2 * n) + [_SEM] * 3 + [_ANY] * len(after),
        out_specs=(_SEM, _SEM, *[_HBM] * (2 * n), pl.BlockSpec(memory_space=pltpu.VMEM)),
        input_output_aliases={i: 2 + i for i in range(2 * n)},
        compiler_params=pltpu.CompilerParams(has_side_effects=_EFFECT),
    )(*thru, *handle["sems"], *after)
    return dict(n=n, sems=res[:2], lands=res[2 + n:2 + 2 * n], token=res[-1], name=handle["name"])


def _gather_wait(handle, after=()):
    n, lands = handle["n"], handle["lands"]

    def body(*refs):
        land_refs = refs[:n]
        send_two, recv_two = refs[n:n + 2]
        for cp in _gather_copies_two(land_refs, send_two, recv_two):
            cp.wait()

    return pl.pallas_call(
        body, name=handle["name"] + "_wait",
        out_shape=tuple(pltpu.HBM(x.shape, x.dtype) for x in lands),
        in_specs=[_HBM] * n + [_SEM] * 2 + [_ANY] * len(after),
        out_specs=tuple([_HBM] * n),
        input_output_aliases={i: i for i in range(n)},
        compiler_params=pltpu.CompilerParams(has_side_effects=_EFFECT),
    )(*lands, *handle["sems"], *after)


def _w_in_pieces(ns):
    out = []
    for e in range(N_DEV):
        lo, hi = ns * e, ns * (e + 1)
        for s in range(len(SEGS)):
            a, b = max(lo, REF_CUTS[s]), min(hi, REF_CUTS[s + 1])
            if a < b:
                inner = SEG_INNER[s] + a - REF_CUTS[s]
                out.append((e, a - lo, b - lo, s, inner, inner + b - a))
    return out


def _asm_w_in(g):
    _, d, ns = g.shape
    rb = 256
    pieces = _w_in_pieces(ns)

    def body(g_ref, o_ref):
        kr = SEGS[4][0]
        o_ref[:, kr:kr + KR_LO] = jnp.zeros((rb, KR_LO), g.dtype)
        o_ref[:, kr + KR_LO + QK_ROPE:kr + HEAD_PAD] = jnp.zeros((rb, HEAD_PAD - KR_LO - QK_ROPE), g.dtype)
        for e, s0, s1, seg, d0, d1 in pieces:
            off = SEGS[seg][0]
            o_ref[:, off + d0:off + d1] = g_ref[e, :, s0:s1]

    return pl.pallas_call(
        body, name="asm_w_in", grid=(d // rb,),
        in_specs=[pl.BlockSpec((N_DEV, rb, ns), lambda i: (0, i, 0))],
        out_specs=_rows(rb, D_IN_PAD), out_shape=jax.ShapeDtypeStruct((d, D_IN_PAD), g.dtype),
        compiler_params=_cparams(1),
    )(g)


def _dis_w_in(dsegs, ns):
    d = dsegs[0].shape[0]
    rb = 256
    pieces = _w_in_pieces(ns)

    def body(*refs):
        seg_refs, o_ref = refs[:len(SEGS)], refs[len(SEGS)]
        for e, s0, s1, seg, d0, d1 in pieces:
            o_ref[e, :, s0:s1] = seg_refs[seg][:, d0:d1].astype(bf16)

    return pl.pallas_call(
        body, name="dis_w_in", grid=(d // rb,),
        in_specs=[_rows(rb, wd) for _, wd in SEGS],
        out_specs=pl.BlockSpec((N_DEV, rb, ns), lambda i: (0, i, 0)),
        out_shape=jax.ShapeDtypeStruct((N_DEV, d, ns), bf16),
        compiler_params=_cparams(1),
    )(*dsegs)


def _asm_small(g_conv_out, g_sg_out, g_uq, g_ukv, g_wo, g_cdw, g_fdw):
    d = g_conv_out.shape[2] * N_DEV
    hw = N_HEADS * HEAD_PAD
    hq = QK_NOPE + QK_ROPE
    ff2 = g_fdw.shape[2] * N_DEV
    cw, fw = g_cdw.shape[2], g_fdw.shape[2]

    def body(co_ref, so_ref, uq_ref, ukv_ref, wo_ref, cdw_ref, fdw_ref, o_co, o_so, o_uq, o_k, o_v, o_wo, o_cdw, o_fdw):
        o_cdw[CONV_K:CONV_HALO, :] = jnp.zeros((CONV_HALO - CONV_K, D_CONV), f32)
        o_fdw[FFN_K:FFN_HALO, :] = jnp.zeros((FFN_HALO - FFN_K, ff2), f32)
        for e in range(N_DEV):
            cs = e * HEAD_PAD
            o_co[:, cs:cs + HEAD_PAD] = co_ref[e]
            o_so[:, cs:cs + HEAD_PAD] = so_ref[e]
            o_uq[:, cs:cs + hq] = uq_ref[e]
            o_uq[:, cs + hq:cs + HEAD_PAD] = jnp.zeros((Q_LORA, HEAD_PAD - hq), bf16)
            o_k[:, cs:cs + QK_NOPE] = ukv_ref[e, :, 0:QK_NOPE]
            o_k[:, cs + QK_NOPE:cs + HEAD_PAD] = jnp.zeros((KV_LORA, HEAD_PAD - QK_NOPE), bf16)
            o_v[:, cs:cs + V_HEAD] = ukv_ref[e, :, QK_NOPE:QK_NOPE + V_HEAD]
            o_v[:, cs + V_HEAD:cs + HEAD_PAD] = jnp.zeros((KV_LORA, HEAD_PAD - V_HEAD), bf16)
            for h in range(N_HEADS):
                o_wo[h * HEAD_PAD:h * HEAD_PAD + V_HEAD, cs:cs + HEAD_PAD] = wo_ref[e, h * V_HEAD:(h + 1) * V_HEAD, :]
                o_wo[h * HEAD_PAD + V_HEAD:(h + 1) * HEAD_PAD, cs:cs + HEAD_PAD] = jnp.zeros((HEAD_PAD - V_HEAD, HEAD_PAD), bf16)
            o_cdw[0:CONV_K, e * cw:(e + 1) * cw] = cdw_ref[e]
            o_fdw[0:FFN_K, e * fw:(e + 1) * fw] = fdw_ref[e]

    ins = (g_conv_out, g_sg_out, g_uq, g_ukv, g_wo, g_cdw, g_fdw)
    out_shape = (jax.ShapeDtypeStruct((D_CONV, d), bf16), jax.ShapeDtypeStruct((D_SG, d), bf16), jax.ShapeDtypeStruct((Q_LORA, hw), bf16),
                 jax.ShapeDtypeStruct((KV_LORA, hw), bf16), jax.ShapeDtypeStruct((KV_LORA, hw), bf16), jax.ShapeDtypeStruct((hw, d), bf16),
                 jax.ShapeDtypeStruct((CONV_HALO, D_CONV), f32), jax.ShapeDtypeStruct((FFN_HALO, ff2), f32))
    return pl.pallas_call(
        body, name="asm_small", grid=(1,),
        in_specs=[_whole(a) for a in ins], out_specs=tuple(_whole(o) for o in out_shape), out_shape=out_shape,
        compiler_params=_cparams(1),
    )(*ins)


def _dis_small(d_co, d_so, d_uq, d_k, d_v, d_wo, d_cdw, d_fdw):
    d = d_co.shape[1]
    hq = QK_NOPE + QK_ROPE
    cw, fw = D_CONV // N_DEV, d_fdw.shape[1] // N_DEV

    def body(co_ref, so_ref, uq_ref, k_ref, v_ref, wo_ref, cdw_ref, fdw_ref, o_co, o_so, o_uq, o_ukv, o_wo, o_cdw, o_fdw):
        for e in range(N_DEV):
            cs = e * HEAD_PAD
            o_co[e] = co_ref[:, cs:cs + HEAD_PAD].astype(bf16)
            o_so[e] = so_ref[:, cs:cs + HEAD_PAD].astype(bf16)
            o_uq[e] = uq_ref[:, cs:cs + hq].astype(bf16)
            o_ukv[e, :, 0:QK_NOPE] = k_ref[:, cs:cs + QK_NOPE].astype(bf16)
            o_ukv[e, :, QK_NOPE:QK_NOPE + V_HEAD] = v_ref[:, cs:cs + V_HEAD].astype(bf16)
            for h in range(N_HEADS):
                o_wo[e, h * V_HEAD:(h + 1) * V_HEAD, :] = wo_ref[h * HEAD_PAD:h * HEAD_PAD + V_HEAD, cs:cs + HEAD_PAD].astype(bf16)
            o_cdw[e] = cdw_ref[0:CONV_K, e * cw:(e + 1) * cw]
            o_fdw[e] = fdw_ref[0:FFN_K, e * fw:(e + 1) * fw]

    ins = (d_co, d_so, d_uq, d_k, d_v, d_wo, d_cdw, d_fdw)
    out_shape = (jax.ShapeDtypeStruct((N_DEV, D_CONV, d // N_DEV), bf16), jax.ShapeDtypeStruct((N_DEV, D_SG, d // N_DEV), bf16),
                 jax.ShapeDtypeStruct((N_DEV, Q_LORA, hq), bf16), jax.ShapeDtypeStruct((N_DEV, KV_LORA, QK_NOPE + V_HEAD), bf16),
                 jax.ShapeDtypeStruct((N_DEV, N_HEADS * V_HEAD, d // N_DEV), bf16), jax.ShapeDtypeStruct((N_DEV, CONV_K, cw), f32),
                 jax.ShapeDtypeStruct((N_DEV, FFN_K, fw), f32))
    return pl.pallas_call(
        body, name="dis_small", grid=(1,),
        in_specs=[_whole(a) for a in ins], out_specs=tuple(_whole(o) for o in out_shape), out_shape=out_shape,
        compiler_params=_cparams(1),
    )(*ins)


def _pack_rep(vec_grads, dsgb):
    def body(*refs):
        o_ref = refs[-1]
        o_ref[...] = jnp.zeros((REP_ROWS, REP_W), f32)
        for (_, row, width, nrows), ref in zip(REP_VECS, refs[:-1]):
            o_ref[row:row + nrows, 0:width] = ref[0:nrows, :]

    ins = tuple(vec_grads) + (dsgb,)
    return pl.pallas_call(
        body, name="pack_rep", grid=(1,),
        in_specs=[_whole(a) for a in ins], out_specs=pl.BlockSpec((REP_ROWS, REP_W), lambda i: (0, 0)),
        out_shape=jax.ShapeDtypeStruct((REP_ROWS, REP_W), f32), compiler_params=_cparams(1),
    )(*ins)


def _mix_in_fwd(l, x, g1, w_in_p):
    t, d = x.shape
    tb = _tb(t)

    def body(x_ref, g_ref, w_ref, h_ref, *outs):
        h = _rms_fwd(x_ref[...], g_ref[l:l + 1, :]).astype(bf16)
        h_ref[...] = h
        for (off, wd), o_ref in zip(SEGS, outs):
            o_ref[...] = _dot(h, w_ref[:, off:off + wd])

    return pl.pallas_call(
        body, name="mix_in_fwd", grid=(t // tb,),
        in_specs=[_rows(tb, d), _const(g1.shape), _const((d, D_IN_PAD))],
        out_specs=tuple([_rows(tb, d)] + [_rows(tb, wd) for _, wd in SEGS]),
        out_shape=tuple([jax.ShapeDtypeStruct((t, d), bf16)] + [jax.ShapeDtypeStruct((t, wd), f32) for _, wd in SEGS]),
        compiler_params=_cparams(1),
    )(x, g1, w_in_p)


def _conv_fwd(l, a_in, dw_w, dw_b, ln_g, ln_b):
    t = a_in.shape[0]
    tb = _tb(t)

    def body(a_ref, w_ref, b_ref, g_ref, be_ref, s_ref, z1_ref, win):
        @pl.when(pl.program_id(0) == 0)
        def _():
            win[0:CONV_HALO, :] = jnp.zeros((CONV_HALO, D_CONV), f32)

        a = a_ref[...]
        win[CONV_HALO:CONV_HALO + tb, :] = a[:, :D_CONV] * jax.nn.sigmoid(a[:, D_CONV:])
        for r0 in range(0, tb, LANES):
            for c0 in range(0, D_CONV, LANES):
                cs = slice(c0, c0 + LANES)
                acc = jnp.broadcast_to(b_ref[l:l + 1, cs], (LANES, LANES))
                for j in range(CONV_K):
                    acc = acc + w_ref[j:j + 1, cs] * win[pl.ds(r0 + CONV_HALO - CONV_K + 1 + j, LANES), cs]
                z1_ref[r0:r0 + LANES, cs] = acc
        zh, _ = _ln_stats(z1_ref[...])
        zl = zh * g_ref[l:l + 1, :] + be_ref[l:l + 1, :]
        s_ref[...] = (zl * jax.nn.sigmoid(zl)).astype(bf16)
        win[0:CONV_HALO, :] = win[tb:tb + CONV_HALO, :]

    return pl.pallas_call(
        body, name="conv_fwd", grid=(t // tb,),
        in_specs=[_rows(tb, 2 * D_CONV), _const((CONV_HALO, D_CONV)), _const(dw_b.shape), _const(ln_g.shape), _const(ln_b.shape)],
        out_specs=(_rows(tb, D_CONV), _rows(tb, D_CONV)),
        out_shape=(jax.ShapeDtypeStruct((t, D_CONV), bf16), jax.ShapeDtypeStruct((t, D_CONV), f32)),
        scratch_shapes=[pltpu.VMEM((tb + CONV_HALO, D_CONV), f32)],
        compiler_params=_cparams(1),
    )(a_in, dw_w, dw_b, ln_g, ln_b)


def _tril_mask():
    r = lax.broadcasted_iota(jnp.int32, (SG_CHUNK, SG_CHUNK), 0)
    c = lax.broadcasted_iota(jnp.int32, (SG_CHUNK, SG_CHUNK), 1)
    return r >= c


def _sgu_fwd(l, b_in, ln_g, ln_b, sg_w, bexp):
    t = b_in.shape[0]
    tb = _tb(t)
    gw = D_SG // SG_GROUPS

    def body(b_ref, g_ref, be_ref, w_ref, bexp_ref, um_ref):
        gl = _gelu(b_ref[...])
        u = gl[:, :D_SG]
        vh, _ = _ln_stats(gl[:, D_SG:])
        vn = (vh * g_ref[l:l + 1, :] + be_ref[l:l + 1, :]).astype(bf16)
        tri = _tril_mask()
        for g in range(SG_GROUPS):
            wg = jnp.where(tri, w_ref[l, g], 0.0).astype(bf16)
            cs = slice(g * gw, (g + 1) * gw)
            for r0 in range(0, tb, SG_CHUNK):
                rs = slice(r0, r0 + SG_CHUNK)
                mixed = _dot(wg, vn[rs, cs]) + bexp_ref[:, cs]
                um_ref[rs, cs] = (u[rs, cs] * mixed).astype(bf16)

    return pl.pallas_call(
        body, name="sgu_fwd", grid=(t // tb,),
        in_specs=[_rows(tb, 2 * D_SG), _const(ln_g.shape), _const(ln_b.shape), _const(sg_w.shape), _const((SG_CHUNK, D_SG))],
        out_specs=_rows(tb, D_SG),
        out_shape=jax.ShapeDtypeStruct((t, D_SG), bf16),
        compiler_params=_cparams(1),
    )(b_in, ln_g, ln_b, sg_w, bexp)


def _mla_proj_fwd(l, qlat, kvlat, krope, cos_t, sin_t, gq, wuq_p, gkv, wk_p, wv_p):
    t = qlat.shape[0]
    tb = _tb(t)
    hw = N_HEADS * HEAD_PAD

    def body(ql_ref, kvl_ref, kr_ref, c_ref, s_ref, gq_ref, wq_ref, gkv_ref, wk_ref, wv_ref, q_ref, k_ref, v_ref, qn_ref, kvn_ref):
        cos_b, sin_b = c_ref[...], s_ref[...]
        qn = _rms_fwd(ql_ref[...], gq_ref[l:l + 1, :]).astype(bf16)
        qn_ref[...] = qn
        q = _dot(qn, wq_ref[...])
        sw = _swap_rope_halves(q)
        for h in range(N_HEADS):
            hs = slice(h * HEAD_PAD, (h + 1) * HEAD_PAD)
            q_ref[:, hs] = (q[:, hs] * cos_b + sw[:, hs] * sin_b).astype(bf16)
        kvn = _rms_fwd(kvl_ref[...], gkv_ref[l:l + 1, :]).astype(bf16)
        kvn_ref[...] = kvn
        kr = kr_ref[...]
        kpe = kr * cos_b + _swap_rope_halves(kr) * sin_b
        k = _dot(kvn, wk_ref[...])
        for h in range(N_HEADS):
            hs = slice(h * HEAD_PAD, (h + 1) * HEAD_PAD)
            k_ref[:, hs] = (k[:, hs] + kpe).astype(bf16)
        v_ref[...] = _dot(kvn, wv_ref[...]).astype(bf16)

    return pl.pallas_call(
        body, name="mla_proj_fwd", grid=(t // tb,),
        in_specs=[_rows(tb, Q_LORA), _rows(tb, KV_LORA), _rows(tb, HEAD_PAD), _rows(tb, HEAD_PAD), _rows(tb, HEAD_PAD),
                  _const(gq.shape), _const((Q_LORA, hw)), _const(gkv.shape), _const((KV_LORA, hw)), _const((KV_LORA, hw))],
        out_specs=(_rows(tb, hw), _rows(tb, hw), _rows(tb, hw), _rows(tb, Q_LORA), _rows(tb, KV_LORA)),
        out_shape=(jax.ShapeDtypeStruct((t, hw), bf16), jax.ShapeDtypeStruct((t, hw), bf16), jax.ShapeDtypeStruct((t, hw), bf16),
                   jax.ShapeDtypeStruct((t, Q_LORA), bf16), jax.ShapeDtypeStruct((t, KV_LORA), bf16)),
        compiler_params=_cparams(1),
    )(qlat, kvlat, krope, cos_t, sin_t, gq, wuq_p, gkv, wk_p, wv_p)


def _diag_mask(tq):
    return lax.broadcasted_iota(jnp.int32, (tq, tq), 0) >= lax.broadcasted_iota(jnp.int32, (tq, tq), 1)


def _attn_fwd(q, k, v):
    t = q.shape[0]
    tq = min(ATT_BLOCK, t)

    def body(q_ref, k_ref, v_ref, o_ref, lse_ref):
        qi = pl.program_id(1)
        for i in range(t // tq):
            @pl.when(qi == i)
            def _(i=i):
                qv = q_ref[...]
                lo, hi = i * tq, (i + 1) * tq
                s_d = jnp.where(_diag_mask(tq), _dot_nt(qv, k_ref[lo:hi, :]) * ATT_SCALE, NEG)
                m = jnp.max(s_d, axis=-1, keepdims=True)
                if i > 0:
                    s_o = _dot_nt(qv, k_ref[0:lo, :]) * ATT_SCALE
                    m = jnp.maximum(m, jnp.max(s_o, axis=-1, keepdims=True))
                p_d = jnp.exp(s_d - m)
                lsum = jnp.sum(p_d, axis=-1, keepdims=True)
                acc = _dot(p_d.astype(bf16), v_ref[lo:hi, :])
                if i > 0:
                    p_o = jnp.exp(s_o - m)
                    lsum = lsum + jnp.sum(p_o, axis=-1, keepdims=True)
                    acc = acc + _dot(p_o.astype(bf16), v_ref[0:lo, :])
                o_ref[...] = (acc / lsum).astype(bf16)
                lse_ref[...] = m + jnp.log(lsum)

    return pl.pallas_call(
        body, name="attn_fwd", grid=(N_HEADS, t // tq),
        in_specs=[pl.BlockSpec((tq, HEAD_PAD), lambda h, i: (i, h)), pl.BlockSpec((t, HEAD_PAD), lambda h, i: (0, h)),
                  pl.BlockSpec((t, HEAD_PAD), lambda h, i: (0, h))],
        out_specs=(pl.BlockSpec((tq, HEAD_PAD), lambda h, i: (i, h)), pl.BlockSpec((None, tq, 1), lambda h, i: (h, i, 0))),
        out_shape=(jax.ShapeDtypeStruct((t, N_HEADS * HEAD_PAD), bf16), jax.ShapeDtypeStruct((N_HEADS, t, 1), f32)),
        compiler_params=_cparams(2),
    )(q, k, v)


def _merge_out_fwd(l, x, s, um, o, gates, conv_out_w, sg_out_w, wo_p, w_out, g2):
    t, d = x.shape
    tb = _tb(t)

    def body(x_ref, s_ref, um_ref, o_ref, gt_ref, wa_ref, wb_ref, wc_ref, wout_ref, g_ref, x1_ref, mg_ref, om_ref):
        merged = (jax.nn.sigmoid(gt_ref[:, 0:d]) * _dot(s_ref[...], wa_ref[...])
                  + jax.nn.sigmoid(gt_ref[:, d:2 * d]) * _dot(um_ref[...], wb_ref[...])
                  + jax.nn.sigmoid(gt_ref[:, 2 * d:3 * d]) * _dot(o_ref[...], wc_ref[...]))
        mb = merged.astype(bf16)
        mg_ref[...] = mb
        om = _dot(mb, wout_ref[...])
        om_ref[...] = om
        x1_ref[...] = x_ref[...] + _rms_fwd(om, g_ref[l:l + 1, :])

    hw = N_HEADS * HEAD_PAD
    return pl.pallas_call(
        body, name="merge_out_fwd", grid=(t // tb,),
        in_specs=[_rows(tb, d), _rows(tb, D_CONV), _rows(tb, D_SG), _rows(tb, hw), _rows(tb, 3 * d),
                  _const((D_CONV, d)), _const((D_SG, d)), _const((hw, d)), _const((d, d)), _const(g2.shape)],
        out_specs=(_rows(tb, d), _rows(tb, d), _rows(tb, d)),
        out_shape=(jax.ShapeDtypeStruct((t, d), f32), jax.ShapeDtypeStruct((t, d), bf16), jax.ShapeDtypeStruct((t, d), f32)),
        compiler_params=_cparams(1),
    )(x, s, um, o, gates, conv_out_w, sg_out_w, wo_p, w_out, g2)


FF_CHUNK = 1408


def _ffn_conv_cols(zbuf, w_ref, b_ref, l, nrows, c0, c1):
    acc = b_ref[l:l + 1, c0:c1] + w_ref[0:1, c0:c1] * zbuf[pl.ds(FFN_HALO - 2, nrows), c0:c1]
    acc = acc + w_ref[1:2, c0:c1] * zbuf[pl.ds(FFN_HALO - 1, nrows), c0:c1]
    return acc + w_ref[2:3, c0:c1] * zbuf[pl.ds(FFN_HALO, nrows), c0:c1]


def _ffn_fwd(l, x1, g3, w_up, dw_w, dw_b, w_down, g4):
    t, d = x1.shape
    tb = _tb(t)
    ff2 = 2 * D_FF

    def body(x_ref, g3_ref, wup_ref, dww_ref, dwb_ref, wdn_ref, g4_ref, x2_ref, h2_ref, z_ref, zc_ref, act_ref, f_ref, zbuf):
        @pl.when(pl.program_id(0) == 0)
        def _():
            zbuf[0:FFN_HALO, :] = jnp.zeros((FFN_HALO, ff2), f32)

        xv = x_ref[...]
        h2 = _rms_fwd(xv, g3_ref[l:l + 1, :]).astype(bf16)
        h2_ref[...] = h2
        for c0 in range(0, ff2, FF_CHUNK):
            zv = _dot_nt(h2, wup_ref[c0:c0 + FF_CHUNK, :])
            z_ref[:, c0:c0 + FF_CHUNK] = zv.astype(bf16)
            zbuf[FFN_HALO:FFN_HALO + tb, c0:c0 + FF_CHUNK] = zv
        facc = jnp.zeros((tb, d), f32)
        for c0 in range(0, D_FF, FF_CHUNK):
            gg = _ffn_conv_cols(zbuf, dww_ref, dwb_ref, l, tb, c0, c0 + FF_CHUNK)
            vv = _ffn_conv_cols(zbuf, dww_ref, dwb_ref, l, tb, D_FF + c0, D_FF + c0 + FF_CHUNK)
            zc_ref[:, c0:c0 + FF_CHUNK] = gg.astype(bf16)
            zc_ref[:, D_FF + c0:D_FF + c0 + FF_CHUNK] = vv.astype(bf16)
            a = (_gelu(gg) * vv).astype(bf16)
            act_ref[:, c0:c0 + FF_CHUNK] = a
            facc = facc + _dot(a, wdn_ref[c0:c0 + FF_CHUNK, :])
        f_ref[...] = facc
        x2_ref[...] = xv + _rms_fwd(facc, g4_ref[l:l + 1, :])
        zbuf[0:FFN_HALO, :] = zbuf[tb:tb + FFN_HALO, :]

    return pl.pallas_call(
        body, name="ffn_fwd", grid=(t // tb,),
        in_specs=[_rows(tb, d), _const(g3.shape), _const((ff2, d)), _const((FFN_HALO, ff2)), _const(dw_b.shape), _const((D_FF, d)),
                  _const(g4.shape)],
        out_specs=(_rows(tb, d), _rows(tb, d), _rows(tb, ff2), _rows(tb, ff2), _rows(tb, D_FF), _rows(tb, d)),
        out_shape=(jax.ShapeDtypeStruct((t, d), f32), jax.ShapeDtypeStruct((t, d), bf16), jax.ShapeDtypeStruct((t, ff2), bf16),
                   jax.ShapeDtypeStruct((t, ff2), bf16), jax.ShapeDtypeStruct((t, D_FF), bf16), jax.ShapeDtypeStruct((t, d), f32)),
        scratch_shapes=[pltpu.VMEM((tb + FFN_HALO, ff2), f32)],
        compiler_params=_cparams(1),
    )(x1, g3, w_up, dw_w, dw_b, w_down, g4)


def _loss_fwd_bwd(y, target):
    t, d = y.shape
    tb = _tb(t)

    def body(y_ref, t_ref, dy_ref, loss_ref):
        @pl.when(pl.program_id(0) == 0)
        def _():
            loss_ref[...] = jnp.zeros((1, LANES), f32)

        e = y_ref[...] - t_ref[...]
        dy_ref[...] = e * (1.0 / d)
        loss_ref[...] += 0.5 * jnp.sum(_mean(e * e))

    dy, loss = pl.pallas_call(
        body, name="loss", grid=(t // tb,),
        in_specs=[_rows(tb, d), _rows(tb, d)],
        out_specs=(_rows(tb, d), _acc((1, LANES))),
        out_shape=(jax.ShapeDtypeStruct((t, d), f32), jax.ShapeDtypeStruct((1, LANES), f32)),
        compiler_params=_cparams(1),
    )(y, target)
    return dy, loss[0, 0]


def _ffn_bwd(l, dx2, x1, f, z, zc, w_up, dw_w, w_down, g3, g4, deps=()):
    t, d = x1.shape
    tb = min(128, t)
    nb = t // tb
    ff2 = 2 * D_FF
    hrows = 16
    per_h = tb // hrows

    def body(*refs):
        (dx2_ref, x1_ref, f_ref, z_ref, zp_ref, zc_ref, wup_ref, dww_ref, wdn_ref, g3_ref, g4_ref,
         dx1_ref, df_ref, dz_ref, dg4_ref, dg3_ref, ddwb_ref, ddww_ref, zbuf, dzc) = refs[len(deps):]
        i = pl.program_id(0)
        b = nb - 1 - i

        @pl.when(i == 0)
        def _():
            dg4_ref[...] = jnp.zeros_like(dg4_ref)
            dg3_ref[...] = jnp.zeros_like(dg3_ref)
            ddwb_ref[...] = jnp.zeros_like(ddwb_ref)
            ddww_ref[...] = jnp.zeros_like(ddww_ref)
            dzc[tb:tb + FFN_HALO, :] = jnp.zeros((FFN_HALO, ff2), f32)

        dout = dx2_ref[...]
        df, dg4 = _rms_bwd(f_ref[...], g4_ref[l:l + 1, :], dout)
        dg4_ref[...] += dg4
        dfb = df.astype(bf16)
        df_ref[...] = dfb
        zbuf[0:FFN_HALO, :] = jnp.where(b > 0, zp_ref[hrows - FFN_HALO:hrows, :].astype(f32), 0.0)
        zbuf[FFN_HALO:FFN_HALO + tb, :] = z_ref[...].astype(f32)
        for c0 in range(0, D_FF, FF_CHUNK):
            dact = _dot_nt(dfb, wdn_ref[c0:c0 + FF_CHUNK, :])
            gel, dgel = _gelu_and_grad(zc_ref[:, c0:c0 + FF_CHUNK].astype(f32))
            dzc[0:tb, c0:c0 + FF_CHUNK] = dact * zc_ref[:, D_FF + c0:D_FF + c0 + FF_CHUNK].astype(f32) * dgel
            dzc[0:tb, D_FF + c0:D_FF + c0 + FF_CHUNK] = dact * gel
        dh2 = jnp.zeros((tb, d), f32)
        for c0 in range(0, ff2, FF_CHUNK):
            cs = slice(c0, c0 + FF_CHUNK)
            d0 = dzc[0:tb, cs]
            ddwb_ref[:, cs] += _colsum(d0)
            for j in range(FFN_K):
                ddww_ref[j:j + 1, cs] += _colsum(d0 * zbuf[pl.ds(FFN_HALO - 2 + j, tb), cs])
            dzv = dww_ref[2:3, cs] * d0 + dww_ref[1:2, cs] * dzc[pl.ds(1, tb), cs] + dww_ref[0:1, cs] * dzc[pl.ds(2, tb), cs]
            dzb = dzv.astype(bf16)
            dz_ref[:, cs] = dzb
            dh2 = dh2 + _dot(dzb, wup_ref[cs, :])
        dzc[tb:tb + FFN_HALO, :] = dzc[0:FFN_HALO, :]
        dxn, dg3 = _rms_bwd(x1_ref[...], g3_ref[l:l + 1, :], dh2)
        dg3_ref[...] += dg3
        dx1_ref[...] = dout + dxn

    return pl.pallas_call(
        body, name="ffn_bwd", grid=(nb,),
        in_specs=[_ANY] * len(deps) + [_rows_rev(tb, d, nb), _rows_rev(tb, d, nb), _rows_rev(tb, d, nb), _rows_rev(tb, ff2, nb),
                  pl.BlockSpec((hrows, ff2), lambda i: (jnp.maximum((nb - 1 - i) * per_h - 1, 0), 0)), _rows_rev(tb, ff2, nb),
                  _const((ff2, d)), _const((FFN_HALO, ff2)), _const((D_FF, d)), _const(g3.shape), _const(g4.shape)],
        out_specs=(_rows_rev(tb, d, nb), _rows_rev(tb, d, nb), _rows_rev(tb, ff2, nb), _acc((1, d)), _acc((1, d)), _acc((1, ff2)),
                   _acc((FFN_HALO, ff2))),
        out_shape=(jax.ShapeDtypeStruct((t, d), f32), jax.ShapeDtypeStruct((t, d), bf16), jax.ShapeDtypeStruct((t, ff2), bf16),
                   jax.ShapeDtypeStruct((1, d), f32), jax.ShapeDtypeStruct((1, d), f32), jax.ShapeDtypeStruct((1, ff2), f32),
                   jax.ShapeDtypeStruct((FFN_HALO, ff2), f32)),
        scratch_shapes=[pltpu.VMEM((tb + FFN_HALO, ff2), f32), pltpu.VMEM((tb + FFN_HALO, ff2), f32)],
        compiler_params=_cparams(1),
    )(*deps, dx2, x1, f, z, z, zc, w_up, dw_w, w_down, g3, g4)


def _merge_out_bwd(l, dx1, om, s, um, o, gates, conv_out_w, sg_out_w, wo_p, w_out, g2, deps=()):
    t, d = dx1.shape
    tb = _tb(t)
    hw = N_HEADS * HEAD_PAD

    def body(*refs):
        (dx_ref, om_ref, s_ref, um_ref, o_ref, gt_ref, wa_ref, wb_ref, wc_ref, wout_ref, g_ref,
         dom_ref, dgt_ref, dya_ref, dyb_ref, dyc_ref, ds_ref, dum_ref, do_ref, dg2_ref) = refs[len(deps):]

        @pl.when(pl.program_id(0) == 0)
        def _():
            dg2_ref[...] = jnp.zeros_like(dg2_ref)

        dom, dg2 = _rms_bwd(om_ref[...], g_ref[l:l + 1, :], dx_ref[...])
        dg2_ref[...] += dg2
        domb = dom.astype(bf16)
        dom_ref[...] = domb
        dmerged = _dot_nt(domb, wout_ref[...])
        branches = ((s_ref, wa_ref, dya_ref, ds_ref), (um_ref, wb_ref, dyb_ref, dum_ref), (o_ref, wc_ref, dyc_ref, do_ref))
        for br, (in_ref, w_ref, dy_ref, din_ref) in enumerate(branches):
            yv = _dot(in_ref[...], w_ref[...])
            sg = jax.nn.sigmoid(gt_ref[:, br * d:(br + 1) * d])
            dyb = (dmerged * sg).astype(bf16)
            dy_ref[...] = dyb
            dgt_ref[:, br * d:(br + 1) * d] = (dmerged * yv * sg * (1.0 - sg)).astype(bf16)
            din_ref[...] = _dot_nt(dyb, w_ref[...]).astype(din_ref.dtype)

    return pl.pallas_call(
        body, name="merge_out_bwd", grid=(t // tb,),
        in_specs=[_ANY] * len(deps) + [_rows(tb, d), _rows(tb, d), _rows(tb, D_CONV), _rows(tb, D_SG), _rows(tb, hw), _rows(tb, 3 * d),
                  _const((D_CONV, d)), _const((D_SG, d)), _const((hw, d)), _const((d, d)), _const(g2.shape)],
        out_specs=(_rows(tb, d), _rows(tb, 3 * d), _rows(tb, d), _rows(tb, d), _rows(tb, d), _rows(tb, D_CONV), _rows(tb, D_SG),
                   _rows(tb, hw), _acc((1, d))),
        out_shape=(jax.ShapeDtypeStruct((t, d), bf16), jax.ShapeDtypeStruct((t, 3 * d), bf16), jax.ShapeDtypeStruct((t, d), bf16),
                   jax.ShapeDtypeStruct((t, d), bf16), jax.ShapeDtypeStruct((t, d), bf16), jax.ShapeDtypeStruct((t, D_CONV), f32),
                   jax.ShapeDtypeStruct((t, D_SG), f32), jax.ShapeDtypeStruct((t, hw), bf16), jax.ShapeDtypeStruct((1, d), f32)),
        compiler_params=_cparams(1),
    )(*deps, dx1, om, s, um, o, gates, conv_out_w, sg_out_w, wo_p, w_out, g2)


def _attn_bwd(q, k, v, o, lse, do):
    t = q.shape[0]
    tq = min(ATT_BLOCK, t)
    hw = N_HEADS * HEAD_PAD

    def body(q_ref, k_ref, v_ref, o_ref, lse_ref, do_ref, dq_ref, dk_ref, dv_ref):
        qi = pl.program_id(1)

        @pl.when(qi == 0)
        def _():
            dk_ref[...] = jnp.zeros_like(dk_ref)
            dv_ref[...] = jnp.zeros_like(dv_ref)

        def keys(lo, hi, qv, dov, lse, delta, diagonal):
            kj, vj = k_ref[lo:hi, :], v_ref[lo:hi, :]
            p = jnp.exp(_dot_nt(qv, kj) * ATT_SCALE - lse)
            if diagonal:
                p = jnp.where(_diag_mask(tq), p, 0.0)
            ds = (p * (_dot_nt(dov, vj) - delta) * ATT_SCALE).astype(bf16)
            dk_ref[lo:hi, :] += _dot_tn(ds, qv)
            dv_ref[lo:hi, :] += _dot_tn(p.astype(bf16), dov)
            return _dot(ds, kj)

        for i in range(t // tq):
            @pl.when(qi == i)
            def _(i=i):
                qv, dov, lse = q_ref[...], do_ref[...], lse_ref[...]
                delta = jnp.sum(dov.astype(f32) * o_ref[...].astype(f32), axis=-1, keepdims=True)
                dq = keys(i * tq, (i + 1) * tq, qv, dov, lse, delta, True)
                if i > 0:
                    dq = dq + keys(0, i * tq, qv, dov, lse, delta, False)
                dq_ref[...] = dq

    blk_q = pl.BlockSpec((tq, HEAD_PAD), lambda h, i: (i, h))
    blk_kv = pl.BlockSpec((t, HEAD_PAD), lambda h, i: (0, h))
    return pl.pallas_call(
        body, name="attn_bwd", grid=(N_HEADS, t // tq),
        in_specs=[blk_q, blk_kv, blk_kv, blk_q, pl.BlockSpec((None, tq, 1), lambda h, i: (h, i, 0)), blk_q],
        out_specs=(blk_q, blk_kv, blk_kv),
        out_shape=(jax.ShapeDtypeStruct((t, hw), f32), jax.ShapeDtypeStruct((t, hw), f32), jax.ShapeDtypeStruct((t, hw), f32)),
        compiler_params=_cparams(2),
    )(q, k, v, o, lse, do)


def _mla_proj_bwd(l, dq, dk, dv, qlat, kvlat, cos_t, sin_t, gq, wuq_p, gkv, wk_p, wv_p):
    t = qlat.shape[0]
    tb = _tb(t)
    hw = N_HEADS * HEAD_PAD

    def body(dq_ref, dk_ref, dv_ref, ql_ref, kvl_ref, c_ref, s_ref, gq_ref, wq_ref, gkv_ref, wk_ref, wv_ref,
             dqb_ref, dkb_ref, dvb_ref, dql_ref, dkvl_ref, dkr_ref, dgq_ref, dgkv_ref):
        @pl.when(pl.program_id(0) == 0)
        def _():
            dgq_ref[...] = jnp.zeros_like(dgq_ref)
            dgkv_ref[...] = jnp.zeros_like(dgkv_ref)

        cos_b, sin_b = c_ref[...], s_ref[...]
        for h in range(N_HEADS):
            hs = slice(h * HEAD_PAD, (h + 1) * HEAD_PAD)
            dqh = dq_ref[:, hs]
            dqb_ref[:, hs] = (dqh * cos_b + _swap_rope_halves(dqh * sin_b)).astype(bf16)
        dqn = _dot_nt(dqb_ref[...], wq_ref[...])
        dql, dgq = _rms_bwd(ql_ref[...], gq_ref[l:l + 1, :], dqn)
        dgq_ref[...] += dgq
        dql_ref[...] = dql.astype(bf16)
        dkv_full = dk_ref[...]
        dkb = dkv_full.astype(bf16)
        dkb_ref[...] = dkb
        dkpe = dkv_full[:, 0:HEAD_PAD]
        for h in range(1, N_HEADS):
            dkpe = dkpe + dkv_full[:, h * HEAD_PAD:(h + 1) * HEAD_PAD]
        dkr_ref[...] = (dkpe * cos_b + _swap_rope_halves(dkpe * sin_b)).astype(bf16)
        dvb = dv_ref[...].astype(bf16)
        dvb_ref[...] = dvb
        dkvn = _dot_nt(dkb, wk_ref[...]) + _dot_nt(dvb, wv_ref[...])
        dkvl, dgkv = _rms_bwd(kvl_ref[...], gkv_ref[l:l + 1, :], dkvn)
        dgkv_ref[...] += dgkv
        dkvl_ref[...] = dkvl.astype(bf16)

    return pl.pallas_call(
        body, name="mla_proj_bwd", grid=(t // tb,),
        in_specs=[_rows(tb, hw), _rows(tb, hw), _rows(tb, hw), _rows(tb, Q_LORA), _rows(tb, KV_LORA), _rows(tb, HEAD_PAD),
                  _rows(tb, HEAD_PAD), _const(gq.shape), _const((Q_LORA, hw)), _const(gkv.shape), _const((KV_LORA, hw)),
                  _const((KV_LORA, hw))],
        out_specs=(_rows(tb, hw), _rows(tb, hw), _rows(tb, hw), _rows(tb, Q_LORA), _rows(tb, KV_LORA), _rows(tb, HEAD_PAD),
                   _acc((1, Q_LORA)), _acc((1, KV_LORA))),
        out_shape=(jax.ShapeDtypeStruct((t, hw), bf16), jax.ShapeDtypeStruct((t, hw), bf16), jax.ShapeDtypeStruct((t, hw), bf16),
                   jax.ShapeDtypeStruct((t, Q_LORA), bf16), jax.ShapeDtypeStruct((t, KV_LORA), bf16),
                   jax.ShapeDtypeStruct((t, HEAD_PAD), bf16), jax.ShapeDtypeStruct((1, Q_LORA), f32),
                   jax.ShapeDtypeStruct((1, KV_LORA), f32)),
        compiler_params=_cparams(1),
    )(dq, dk, dv, qlat, kvlat, cos_t, sin_t, gq, wuq_p, gkv, wk_p, wv_p)


def _sgu_bwd(l, b_in, dum, ln_g, ln_b, sg_w, bexp):
    t = b_in.shape[0]
    tb = _tb(t)
    gw = D_SG // SG_GROUPS

    def body(b_ref, dum_ref, g_ref, be_ref, w_ref, bexp_ref, db_ref, dw_ref, dsgb_ref, dlg_ref, dlb_ref, dvn_s):
        @pl.when(pl.program_id(0) == 0)
        def _():
            dw_ref[...] = jnp.zeros_like(dw_ref)
            dsgb_ref[...] = jnp.zeros_like(dsgb_ref)
            dlg_ref[...] = jnp.zeros_like(dlg_ref)
            dlb_ref[...] = jnp.zeros_like(dlb_ref)

        gl, dgl = _gelu_and_grad(b_ref[...])
        u = gl[:, :D_SG]
        vh, rstd = _ln_stats(gl[:, D_SG:])
        ln_gain = g_ref[l:l + 1, :]
        vn = (vh * ln_gain + be_ref[l:l + 1, :]).astype(bf16)
        dumv = dum_ref[...]
        tri = _tril_mask()
        ones = jnp.ones((FFN_HALO, gw), f32)
        for g in range(SG_GROUPS):
            wg = jnp.where(tri, w_ref[l, g], 0.0).astype(bf16)
            cs = slice(g * gw, (g + 1) * gw)
            for r0 in range(0, tb, SG_CHUNK):
                rs = slice(r0, r0 + SG_CHUNK)
                vblk = vn[rs, cs]
                mixed = _dot(wg, vblk) + bexp_ref[:, cs]
                db_ref[rs, cs] = (dumv[rs, cs] * mixed * dgl[rs, cs]).astype(bf16)
                dmix = dumv[rs, cs] * u[rs, cs]
                dmb = dmix.astype(bf16)
                dw_ref[g] += jnp.where(tri, _dot_nt(dmb, vblk), 0.0)
                rowsum = lax.dot_general(ones, dmix, (((1,), (1,)), ((), ())), preferred_element_type=f32,
                                         precision=lax.Precision.HIGHEST)
                dsgb_ref[g:g + 1, :] += rowsum[0:1, :]
                dvn_s[rs, cs] = _dot_tn(wg, dmb)
        dvn = dvn_s[...]
        dlg_ref[...] += _colsum(dvn * vh)
        dlb_ref[...] += _colsum(dvn)
        db_ref[:, D_SG:] = (_ln_bwd(vh, rstd, ln_gain, dvn) * dgl[:, D_SG:]).astype(bf16)

    return pl.pallas_call(
        body, name="sgu_bwd", grid=(t // tb,),
        in_specs=[_rows(tb, 2 * D_SG), _rows(tb, D_SG), _const(ln_g.shape), _const(ln_b.shape), _const(sg_w.shape),
                  _const((SG_CHUNK, D_SG))],
        out_specs=(_rows(tb, 2 * D_SG), _acc((SG_GROUPS, SG_CHUNK, SG_CHUNK)), _acc((FFN_HALO, SG_CHUNK)), _acc((1, D_SG)),
                   _acc((1, D_SG))),
        out_shape=(jax.ShapeDtypeStruct((t, 2 * D_SG), bf16), jax.ShapeDtypeStruct((SG_GROUPS, SG_CHUNK, SG_CHUNK), f32),
                   jax.ShapeDtypeStruct((FFN_HALO, SG_CHUNK), f32), jax.ShapeDtypeStruct((1, D_SG), f32),
                   jax.ShapeDtypeStruct((1, D_SG), f32)),
        scratch_shapes=[pltpu.VMEM((tb, D_SG), f32)],
        compiler_params=_cparams(1),
    )(b_in, dum, ln_g, ln_b, sg_w, bexp)


def _conv_bwd(l, a_in, z1, ds, dw_w, ln_g, ln_b):
    t = a_in.shape[0]
    tb = _tb(t)
    nb = t // tb
    per_halo = tb // CONV_HALO

    def body(a_ref, ap_ref, z1_ref, ds_ref, w_ref, g_ref, be_ref, da_ref, ddww_ref, ddwb_ref, dlg_ref, dlb_ref, win, dzb):
        i = pl.program_id(0)
        b = nb - 1 - i

        @pl.when(i == 0)
        def _():
            ddww_ref[...] = jnp.zeros_like(ddww_ref)
            ddwb_ref[...] = jnp.zeros_like(ddwb_ref)
            dlg_ref[...] = jnp.zeros_like(dlg_ref)
            dlb_ref[...] = jnp.zeros_like(dlb_ref)
            dzb[tb:tb + CONV_HALO, :] = jnp.zeros((CONV_HALO, D_CONV), f32)

        a = a_ref[...]
        val = a[:, :D_CONV]
        sg = jax.nn.sigmoid(a[:, D_CONV:])
        ap = ap_ref[...]
        win[0:CONV_HALO, :] = jnp.where(b > 0, ap[:, :D_CONV] * jax.nn.sigmoid(ap[:, D_CONV:]), 0.0)
        win[CONV_HALO:CONV_HALO + tb, :] = val * sg
        zh, rstd = _ln_stats(z1_ref[...])
        ln_gain = g_ref[l:l + 1, :]
        zl = zh * ln_gain + be_ref[l:l + 1, :]
        sgl = jax.nn.sigmoid(zl)
        dzl = ds_ref[...] * (sgl * (1.0 + zl * (1.0 - sgl)))
        dlg_ref[...] += _colsum(dzl * zh)
        dlb_ref[...] += _colsum(dzl)
        dz1 = _ln_bwd(zh, rstd, ln_gain, dzl)
        dzb[0:tb, :] = dz1
        ddwb_ref[...] += _colsum(dz1)
        dgate_f = val * sg * (1.0 - sg)
        for c0 in range(0, D_CONV, LANES):
            cs = slice(c0, c0 + LANES)
            for r0 in range(0, tb, LANES):
                d1 = dzb[r0:r0 + LANES, cs]
                acc = jnp.zeros((LANES, LANES), f32)
                for j in range(CONV_K):
                    ddww_ref[j:j + 1, cs] += _colsum(d1 * win[pl.ds(r0 + CONV_HALO - CONV_K + 1 + j, LANES), cs])
                    acc = acc + w_ref[j:j + 1, cs] * dzb[pl.ds(r0 + CONV_K - 1 - j, LANES), cs]
                da_ref[r0:r0 + LANES, cs] = (acc * sg[r0:r0 + LANES, cs]).astype(bf16)
                da_ref[r0:r0 + LANES, c0 + D_CONV:c0 + D_CONV + LANES] = (acc * dgate_f[r0:r0 + LANES, cs]).astype(bf16)
        dzb[tb:tb + CONV_HALO, :] = dzb[0:CONV_HALO, :]

    return pl.pallas_call(
        body, name="conv_bwd", grid=(nb,),
        in_specs=[_rows_rev(tb, 2 * D_CONV, nb),
                  pl.BlockSpec((CONV_HALO, 2 * D_CONV), lambda i: (jnp.maximum((nb - 1 - i) * per_halo - 1, 0), 0)),
                  _rows_rev(tb, D_CONV, nb), _rows_rev(tb, D_CONV, nb), _const((CONV_HALO, D_CONV)), _const(ln_g.shape),
                  _const(ln_b.shape)],
        out_specs=(_rows_rev(tb, 2 * D_CONV, nb), _acc((CONV_HALO, D_CONV)), _acc((1, D_CONV)), _acc((1, D_CONV)), _acc((1, D_CONV))),
        out_shape=(jax.ShapeDtypeStruct((t, 2 * D_CONV), bf16), jax.ShapeDtypeStruct((CONV_HALO, D_CONV), f32),
                   jax.ShapeDtypeStruct((1, D_CONV), f32), jax.ShapeDtypeStruct((1, D_CONV), f32), jax.ShapeDtypeStruct((1, D_CONV), f32)),
        scratch_shapes=[pltpu.VMEM((tb + CONV_HALO, D_CONV), f32), pltpu.VMEM((tb + CONV_HALO, D_CONV), f32)],
        compiler_params=_cparams(1),
    )(a_in, a_in, z1, ds, dw_w, ln_g, ln_b)


def _mix_in_bwd(l, x, g1, dxres, dsegs, w_in_p, deps=()):
    t, d = x.shape
    tb = _tb(t)

    def body(*refs):
        x_ref, g_ref, dr_ref = refs[len(deps):len(deps) + 3]
        rest = refs[len(deps) + 3:]
        dseg_refs, w_ref, dx_ref, dg_ref = rest[:len(SEGS)], rest[len(SEGS)], rest[len(SEGS) + 1], rest[len(SEGS) + 2]

        @pl.when(pl.program_id(0) == 0)
        def _():
            dg_ref[...] = jnp.zeros_like(dg_ref)

        dh = jnp.zeros((tb, d), f32)
        for (off, wd), ds_ref in zip(SEGS, dseg_refs):
            dh = dh + _dot_nt(ds_ref[...], w_ref[:, off:off + wd])
        dxn, dg = _rms_bwd(x_ref[...], g_ref[l:l + 1, :], dh)
        dg_ref[...] += dg
        dx_ref[...] = dr_ref[...] + dxn

    return pl.pallas_call(
        body, name="mix_in_bwd", grid=(t // tb,),
        in_specs=([_ANY] * len(deps) + [_rows(tb, d), _const(g1.shape), _rows(tb, d)] + [_rows(tb, wd) for _, wd in SEGS]
                  + [_const((d, D_IN_PAD))]),
        out_specs=(_rows(tb, d), _acc((1, d))),
        out_shape=(jax.ShapeDtypeStruct((t, d), f32), jax.ShapeDtypeStruct((1, d), f32)),
        compiler_params=_cparams(1),
    )(*deps, x, g1, dxres, *dsegs, w_in_p)


def _pick_block(n, cap=512):
    for b in (cap, 384, 256, 128):
        if b <= cap and n % b == 0:
            return b
    return n


def _wgrad(a, b, name, out_dtype=f32):
    t, kdim = a.shape
    n = b.shape[1]
    bk, bn = _pick_block(kdim), _pick_block(n, cap=1024)

    def body(a_ref, b_ref, o_ref):
        o_ref[...] = _dot_tn(a_ref[...], b_ref[...]).astype(out_dtype)

    return pl.pallas_call(
        body, name=name, grid=(kdim // bk, n // bn),
        in_specs=[pl.BlockSpec((t, bk), lambda i, j: (0, i)), pl.BlockSpec((t, bn), lambda i, j: (0, j))],
        out_specs=pl.BlockSpec((bk, bn), lambda i, j: (i, j)),
        out_shape=jax.ShapeDtypeStruct((kdim, n), out_dtype),
        compiler_params=_cparams(2),
    )(a, b)


def _wgrad_multi(a, bs, name, deps=()):
    t, kdim = a.shape
    bk = _pick_block(kdim)
    nb = len(bs)

    def body(*refs):
        a_ref, refs = refs[len(deps)], refs[len(deps) + 1:]
        at = a_ref[...].T
        for b_ref, o_ref in zip(refs[:nb], refs[nb:]):
            o_ref[...] = _dot(at, b_ref[...])

    return pl.pallas_call(
        body, name=name, grid=(kdim // bk,),
        in_specs=[_ANY] * len(deps) + [pl.BlockSpec((t, bk), lambda i: (0, i))] + [_const(b.shape) for b in bs],
        out_specs=tuple(pl.BlockSpec((bk, b.shape[1]), lambda i: (i, 0)) for b in bs),
        out_shape=tuple(jax.ShapeDtypeStruct((kdim, b.shape[1]), f32) for b in bs),
        compiler_params=_cparams(1),
    )(*deps, a, *bs)


_BC1 = 1.0 - ADAM_B1 ** ADAM_STEP
_BC2 = 1.0 - ADAM_B2 ** ADAM_STEP


def _adam_math(g, w, m, v):
    nm = ADAM_B1 * m + (1.0 - ADAM_B1) * g
    nv = ADAM_B2 * v + (1.0 - ADAM_B2) * (g * g)
    delta = -ADAM_LR * ((nm / _BC1) / (jnp.sqrt(nv / _BC2) + ADAM_EPS) + ADAM_WD * w)
    return delta, nm, nv


def _slot_sum(r_ref, index=()):
    g = r_ref[(0,) + index].astype(f32)
    for s in range(1, N_DEV):
        g = g + r_ref[(s,) + index].astype(f32)
    return g


def _adamw_shard(l, recv, w, m, v, prev, name, deps=()):
    _, k, ns = recv.shape
    rb = next((c for c in (256, 192, 176, 128) if k % c == 0), k)
    blk = pl.BlockSpec((None, rb, ns), lambda i: (l, i, 0))

    def body(r_ref, w_ref, m_ref, v_ref, *rest):
        g_ref, d_ref, nm_ref, nv_ref = rest[-4:]
        g = _slot_sum(r_ref)
        g_ref[...] = g
        d_ref[...], nm_ref[...], nv_ref[...] = _adam_math(g, w_ref[...], m_ref[...], v_ref[...])

    out = jax.ShapeDtypeStruct(w.shape, f32)
    n_prev = 0 if prev is None else 4
    return pl.pallas_call(
        body, name=name, grid=(k // rb,),
        in_specs=[pl.BlockSpec((N_DEV, rb, ns), lambda i: (0, i, 0)), blk, blk, blk] + [_ANY] * (n_prev + len(deps)),
        out_specs=(blk, blk, blk, blk), out_shape=(out, out, out, out),
        input_output_aliases={4 + j: j for j in range(n_prev)},
        compiler_params=_cparams(1),
    )(recv, w, m, v, *(prev or ()), *deps)


def _adamw_rep(recvs, ws, ms, vs):
    depth = len(recvs)
    nt = len(REP_VECS)

    def body(*refs):
        r_refs = refs[:depth]
        w_refs, m_refs, v_refs = (refs[depth + i * nt:depth + (i + 1) * nt] for i in range(3))
        outs = refs[depth + 3 * nt:]
        for ti, (_, row, width, nrows) in enumerate(REP_VECS):
            for l in range(depth):
                g = r_refs[l][0, row:row + nrows, 0:width]
                for s in range(1, N_DEV):
                    g = g + r_refs[l][s, row:row + nrows, 0:width]
                pick = (lambda ref: ref[l]) if nrows > 1 else (lambda ref: ref[l:l + 1, :])
                delta, nm, nv = _adam_math(g, pick(w_refs[ti]), pick(m_refs[ti]), pick(v_refs[ti]))
                for o_ref, val in zip(outs[4 * ti:4 * ti + 4], (g, delta, nm, nv)):
                    if nrows > 1:
                        o_ref[l] = val
                    else:
                        o_ref[l:l + 1, :] = val

    ins = tuple(recvs) + tuple(ws) + tuple(ms) + tuple(vs)
    out_shape = tuple(jax.ShapeDtypeStruct(w.shape, f32) for w in ws for _ in range(4))
    return pl.pallas_call(
        body, name="adamw_rep", grid=(1,),
        in_specs=[_whole(a) for a in ins], out_specs=tuple(_whole(o) for o in out_shape), out_shape=out_shape,
        compiler_params=_cparams(1),
    )(*ins)


def _adamw_sg_w(recvs, w, m, v):
    depth = len(recvs)

    def body(*refs):
        r_refs = refs[:depth]
        w_ref, m_ref, v_ref = refs[depth:depth + 3]
        outs = refs[depth + 3:]
        for l in range(depth):
            for gi in range(SG_GROUPS):
                g = _slot_sum(r_refs[l], (gi,))
                delta, nm, nv = _adam_math(g, w_ref[l, gi], m_ref[l, gi], v_ref[l, gi])
                for o_ref, val in zip(outs, (g, delta, nm, nv)):
                    o_ref[l, gi] = val

    ins = tuple(recvs) + (w, m, v)
    out = jax.ShapeDtypeStruct(w.shape, f32)
    return pl.pallas_call(
        body, name="adamw_sg_w", grid=(1,),
        in_specs=[_whole(a) for a in ins], out_specs=tuple(_whole(out) for _ in range(4)), out_shape=(out,) * 4,
        compiler_params=_cparams(1),
    )(*ins)


def _rope_tables(positions):
    t = positions.shape[0]
    inv = 10000.0 ** (-jnp.arange(0, QK_ROPE, 2, dtype=f32) / QK_ROPE)
    ang = positions.astype(f32)[:, None] * inv
    cos, sin = jnp.cos(ang), jnp.sin(ang)
    tail = jnp.zeros((t, HEAD_PAD - KR_LO - QK_ROPE), f32)
    cos_t = jnp.concatenate([jnp.ones((t, KR_LO), f32), cos, cos, tail], axis=1)
    sin_t = jnp.concatenate([jnp.zeros((t, KR_LO), f32), -sin, sin, tail], axis=1)
    return cos_t, sin_t


def _bias_over_channels(sg_b_l):
    return jnp.broadcast_to(sg_b_l.T[:, :, None], (SG_CHUNK, SG_GROUPS, D_SG // SG_GROUPS)).reshape(SG_CHUNK, D_SG)


def _mixer_weights(gathered, rep, l):
    conv_out_w, sg_out_w, wuq_p, wk_p, wv_p, wo_p, conv_dw_w, ffn_dw_w = _asm_small(
        gathered["conv_out_w"], gathered["sg_out_w"], gathered["mla_w_uq"], gathered["mla_w_ukv"], gathered["mla_w_o"],
        gathered["conv_dw_w"], gathered["ffn_dw_w"])
    g_out = gathered["w_out"]
    w = dict(rep)
    w.update(
        l=l, w_in_p=_asm_w_in(gathered["w_in"]),
        conv_out_w=conv_out_w, sg_out_w=sg_out_w, wuq_p=wuq_p, wk_p=wk_p, wv_p=wv_p, wo_p=wo_p, conv_dw_w_p=conv_dw_w, ffn_dw_w_p=ffn_dw_w,
        w_out=g_out.reshape(g_out.shape[0] * g_out.shape[1], g_out.shape[2]),
        bexp=_bias_over_channels(rep["sg_b"][l]))
    return w


def _ffn_weights(gathered):
    stack = lambda g: g.reshape(g.shape[0] * g.shape[1], g.shape[2])
    return dict(w_up=stack(gathered["ffn_w_up"]), w_down=stack(gathered["ffn_w_down"]))


def _mixer_fwd(x, w, cos_t, sin_t):
    l = w["l"]
    h, a_in, b_in, qlat, kvlat, krope, gates = _mix_in_fwd(l, x, w["mix_pre_g"], w["w_in_p"])
    s, z1 = _conv_fwd(l, a_in, w["conv_dw_w_p"], w["conv_dw_b"], w["conv_ln_g"], w["conv_ln_b"])
    um = _sgu_fwd(l, b_in, w["sg_ln_g"], w["sg_ln_b"], w["sg_w"], w["bexp"])
    q, k, v, qn, kvn = _mla_proj_fwd(l, qlat, kvlat, krope, cos_t, sin_t, w["mla_q_norm_g"], w["wuq_p"], w["mla_kv_norm_g"],
                                     w["wk_p"], w["wv_p"])
    o, lse = _attn_fwd(q, k, v)
    x1, merged, om = _merge_out_fwd(l, x, s, um, o, gates, w["conv_out_w"], w["sg_out_w"], w["wo_p"], w["w_out"], w["mix_post_g"])
    saved = dict(x=x, h=h, a_in=a_in, b_in=b_in, qlat=qlat, kvlat=kvlat, gates=gates, s=s, z1=z1, um=um, q=q, k=k, v=v, qn=qn,
                 kvn=kvn, o=o, lse=lse, x1=x1, merged=merged, om=om)
    return x1, saved


def _ffn_layer_fwd(x1, w):
    x2, h2, z, zc, act, f = _ffn_fwd(w["l"], x1, w["ffn_pre_g"], w["w_up"], w["ffn_dw_w_p"], w["ffn_dw_b"], w["w_down"], w["ffn_post_g"])
    return x2, dict(h2=h2, z=z, zc=zc, act=act, f=f)


def _ffn_layer_bwd(dx2, sv, w, deps=()):
    l = w["l"]
    vec = {}
    dx1, df, dz, vec["ffn_post_g"], vec["ffn_pre_g"], vec["ffn_dw_b"], vec["d_fdw"] = _ffn_bwd(
        l, dx2, sv["x1"], sv["f"], sv["z"], sv["zc"], w["w_up"], w["ffn_dw_w_p"], w["w_down"], w["ffn_pre_g"], w["ffn_post_g"],
        deps=deps)
    d_down = _wgrad(sv["act"], df, "wgrad_ffn_down", bf16)
    d_up_t = _wgrad(dz, sv["h2"], "wgrad_ffn_up", bf16)
    unstack = lambda g: g.reshape(N_DEV, g.shape[0] // N_DEV, g.shape[1])
    send = dict(ffn_w_up=unstack(d_up_t), ffn_w_down=unstack(d_down))
    return dx1, send, vec


def _mixer_bwd(dx1, sv, w, cos_t, sin_t, shard_cols, vec, start, deps=()):
    l = w["l"]
    d_fdw = vec.pop("d_fdw")
    dom, dgates, dya, dyb, dyc, ds, dum, do, vec["mix_post_g"] = _merge_out_bwd(
        l, dx1, sv["om"], sv["s"], sv["um"], sv["o"], sv["gates"], w["conv_out_w"], w["sg_out_w"], w["wo_p"], w["w_out"], w["mix_post_g"],
        deps=deps)
    d_out = _wgrad(sv["merged"], dom, "wgrad_w_out", bf16)
    d_co = _wgrad(sv["s"], dya, "wgrad_conv_out")
    d_so = _wgrad(sv["um"], dyb, "wgrad_sg_out")
    d_wo = _wgrad(sv["o"], dyc, "wgrad_w_o")

    dq, dk, dv = _attn_bwd(sv["q"], sv["k"], sv["v"], sv["o"], sv["lse"], do)
    dqb, dkb, dvb, dqlat, dkvlat, dkrope, vec["mla_q_norm_g"], vec["mla_kv_norm_g"] = _mla_proj_bwd(
        l, dq, dk, dv, sv["qlat"], sv["kvlat"], cos_t, sin_t, w["mla_q_norm_g"], w["wuq_p"], w["mla_kv_norm_g"], w["wk_p"], w["wv_p"])
    d_uq = _wgrad(sv["qn"], dqb, "wgrad_w_uq")
    d_uk, d_uv = _wgrad_multi(sv["kvn"], [dkb, dvb], "wgrad_w_ukv")

    db_in, dsg_w, dsgb, vec["sg_ln_g"], vec["sg_ln_b"] = _sgu_bwd(l, sv["b_in"], dum, w["sg_ln_g"], w["sg_ln_b"], w["sg_w"], w["bexp"])
    da_in, d_cdw, vec["conv_dw_b"], vec["conv_ln_g"], vec["conv_ln_b"] = _conv_bwd(
        l, sv["a_in"], sv["z1"], ds, w["conv_dw_w_p"], w["conv_ln_g"], w["conv_ln_b"])

    send = {}
    (send["conv_out_w"], send["sg_out_w"], send["mla_w_uq"], send["mla_w_ukv"], send["mla_w_o"], send["conv_dw_w"],
     send["ffn_dw_w"]) = _dis_small(d_co, d_so, d_uq, d_uk, d_uv, d_wo, d_cdw, d_fdw)
    send["w_out"] = d_out.reshape(N_DEV, d_out.shape[0] // N_DEV, d_out.shape[1])
    token = start(MIX_SMALL, send)

    dsegs = (da_in, db_in, dqlat, dkvlat, dkrope, dgates)
    d_in_segs = list(_wgrad_multi(sv["h"], dsegs[:5], "wgrad_w_in_abqkr", deps=token)) + [_wgrad(sv["h"], dgates, "wgrad_w_in_g")]
    token = start(("w_in",), dict(w_in=_dis_w_in(d_in_segs, shard_cols["w_in"])))
    dx, vec["mix_pre_g"] = _mix_in_bwd(l, sv["x"], w["mix_pre_g"], dx1, dsegs, w["w_in_p"], deps=token)
    rep_pack = _pack_rep([vec[n] for n, _, _, _ in REP_VECS[:-1]], dsgb)
    return dx, rep_pack, dsg_w


def kernel(x, positions, mix_pre_g, mix_post_g, ffn_pre_g, ffn_post_g, w_in, conv_dw_w, conv_dw_b, conv_ln_g, conv_ln_b, conv_out_w, sg_ln_g, sg_ln_b, sg_w, sg_b, sg_out_w, mla_q_norm_g, mla_w_uq, mla_kv_norm_g, mla_w_ukv, mla_w_o, w_out, ffn_w_up, ffn_dw_w, ffn_dw_b, ffn_w_down, loss_target, m_mix_pre_g, m_mix_post_g, m_ffn_pre_g, m_ffn_post_g, m_w_in, m_conv_dw_w, m_conv_dw_b, m_conv_ln_g, m_conv_ln_b, m_conv_out_w, m_sg_ln_g, m_sg_ln_b, m_sg_w, m_sg_b, m_sg_out_w, m_mla_q_norm_g, m_mla_w_uq, m_mla_kv_norm_g, m_mla_w_ukv, m_mla_w_o, m_w_out, m_ffn_w_up, m_ffn_dw_w, m_ffn_dw_b, m_ffn_w_down, v_mix_pre_g, v_mix_post_g, v_ffn_pre_g, v_ffn_post_g, v_w_in, v_conv_dw_w, v_conv_dw_b, v_conv_ln_g, v_conv_ln_b, v_conv_out_w, v_sg_ln_g, v_sg_ln_b, v_sg_w, v_sg_b, v_sg_out_w, v_mla_q_norm_g, v_mla_w_uq, v_mla_kv_norm_g, v_mla_w_ukv, v_mla_w_o, v_w_out, v_ffn_w_up, v_ffn_dw_w, v_ffn_dw_b, v_ffn_w_down):
    args = (x, positions, mix_pre_g, mix_post_g, ffn_pre_g, ffn_post_g, w_in, conv_dw_w, conv_dw_b, conv_ln_g, conv_ln_b, conv_out_w, sg_ln_g, sg_ln_b, sg_w, sg_b, sg_out_w, mla_q_norm_g, mla_w_uq, mla_kv_norm_g, mla_w_ukv, mla_w_o, w_out, ffn_w_up, ffn_dw_w, ffn_dw_b, ffn_w_down, loss_target, m_mix_pre_g, m_mix_post_g, m_ffn_pre_g, m_ffn_post_g, m_w_in, m_conv_dw_w, m_conv_dw_b, m_conv_ln_g, m_conv_ln_b, m_conv_out_w, m_sg_ln_g, m_sg_ln_b, m_sg_w, m_sg_b, m_sg_out_w, m_mla_q_norm_g, m_mla_w_uq, m_mla_kv_norm_g, m_mla_w_ukv, m_mla_w_o, m_w_out, m_ffn_w_up, m_ffn_dw_w, m_ffn_dw_b, m_ffn_w_down, v_mix_pre_g, v_mix_post_g, v_ffn_pre_g, v_ffn_post_g, v_w_in, v_conv_dw_w, v_conv_dw_b, v_conv_ln_g, v_conv_ln_b, v_conv_out_w, v_sg_ln_g, v_sg_ln_b, v_sg_w, v_sg_b, v_sg_out_w, v_mla_q_norm_g, v_mla_w_uq, v_mla_kv_norm_g, v_mla_w_ukv, v_mla_w_o, v_w_out, v_ffn_w_up, v_ffn_dw_w, v_ffn_dw_b, v_ffn_w_down)
    n_in = len(IN_NAMES)
    a = dict(zip(IN_NAMES, args[:n_in]))
    target = args[n_in]
    n_w = len(WEIGHTS)
    m_in = dict(zip(WEIGHTS, args[n_in + 1:n_in + 1 + n_w]))
    v_in = dict(zip(WEIGHTS, args[n_in + 1 + n_w:n_in + 1 + 2 * n_w]))
    depth = a["mix_pre_g"].shape[0]
    rep = {n: a[n] for n in WEIGHTS if n not in SHARDED}
    shard_cols = {n: a[n].shape[2] for n in SHARDED}

    def view(arr, n):
        return jnp.swapaxes(arr, 1, 2) if n in TRANSPOSED else arr

    w_sh = {n: view(a[n], n) for n in SHARDED}
    m_sh = {n: view(m_in[n], n) for n in SHARDED}
    v_sh = {n: view(v_in[n], n) for n in SHARDED}

    def gather_groups(l):
        return (("mix", MIX_GROUP), ("ffn", FFN_BIG))

    starts, token = {}, ()
    for l in range(depth):
        for tag, group in gather_groups(l):
            wire = [w_sh[n][l] if n in WIRE_F32 else w_sh[n][l].astype(bf16) for n in group]
            starts[l, tag] = _gather_start(wire, "gather_%s_weights_%d" % (tag, l), deps=token)
            token = (starts[l, tag]["token"],)

    cos_t, sin_t = _rope_tables(a["positions"][0])
    xl = a["x"][0]
    ws, saved = [], []
    passing = {}

    def pass_on(l, tag, after):
        passing[l, tag] = _gather_forward(starts[l, tag], after=after)
        return (passing[l, tag]["token"],)

    relaid = tuple(t[n] for n in SHARDED for t in (m_sh, v_sh))
    after = pass_on(0, gather_groups(0)[0][0], token + relaid)
    for l in range(depth):
        groups = gather_groups(l)
        got = dict(zip(groups[0][1], _gather_wait(passing[l, groups[0][0]], after=after)))
        w = _mixer_weights(got, rep, l)
        x1, sv = _mixer_fwd(xl, w, cos_t, sin_t)
        after = (x1,)
        if len(groups) > 1:
            after = pass_on(l, groups[1][0], after)
        if l + 1 < depth:
            after = pass_on(l + 1, gather_groups(l + 1)[0][0], after)
        if len(groups) > 1:
            got = dict(zip(groups[1][1], _gather_wait(passing[l, groups[1][0]], after=after)))
        w.update(_ffn_weights(got))
        xl, sv_ffn = _ffn_layer_fwd(x1, w)
        sv.update(sv_ffn)
        ws.append(w)
        saved.append(sv)
        after = (xl,)
    dx, loss_part = _loss_fwd_bwd(xl, target[0])
    loss = lax.psum(loss_part, AXES)

    outs = {}
    rep_recvs, sgw_recvs = [None] * depth, [None] * depth

    def finish(l, parts, after):
        for names, handle in parts:
            got = _exchange_wait(handle, after=after)
            for n, recv in zip(names, got):
                outs[n] = _adamw_shard(l, recv, w_sh[n], m_sh[n], v_sh[n], outs.get(n), "adamw_" + n)
            after = tuple(outs[n][0] for n in names) or after
        rep_recvs[l], sgw_recvs[l] = got[0], got[1]
        return after

    token, prev, after = (), None, ()
    for l in reversed(range(depth)):
        dx1, send_ffn, vec = _ffn_layer_bwd(dx, saved[l], ws[l], deps=token)
        parts = []

        def start(names, send, extra=(), l=l, parts=parts):
            tag = names[0] if names else "rep"
            handle = _exchange_start([(send[n], "scatter") for n in names] + list(extra), "exchange_%s_grads_%d" % (tag, l))
            parts.append((names, handle))
            return (handle["token"],)

        token = start(FFN_BIG, send_ffn)
        dx, rep_pack, dsg_w = _mixer_bwd(dx1, saved[l], ws[l], cos_t, sin_t, shard_cols, vec, start, deps=token)
        token = start((), {}, extra=[(rep_pack, "gather"), (dsg_w, "gather")])
        if prev is not None:
            after = finish(*prev, after=(dx,) + token)
        prev = (l, parts)
    finish(*prev, after=after)
    vec_names = [n for n, _, _, _ in REP_VECS]
    rep_outs = _adamw_rep(rep_recvs, [a[n] for n in vec_names], [m_in[n] for n in vec_names], [v_in[n] for n in vec_names])
    for i, n in enumerate(vec_names):
        outs[n] = rep_outs[4 * i:4 * i + 4]
    outs["sg_w"] = _adamw_sg_w(sgw_recvs, a["sg_w"], m_in["sg_w"], v_in["sg_w"])
    for n in TRANSPOSED:
        outs[n] = tuple(view(o, n) for o in outs[n])

    grad_w, delta_w, new_m, new_v = ([outs[n][j] for n in WEIGHTS] for j in range(4))
    return (loss, dx[None], *grad_w, *delta_w, *new_m, *new_v)
```

```python
import math

import jax
import jax.numpy as jnp
from jax import lax
from jax.experimental import pallas as pl
from jax.experimental.pallas import tpu as pltpu

f32 = jnp.float32
bf16 = jnp.bfloat16

N_DEV = 8
AXES = ("x", "y", "c")
EPS = 1e-6
D_CONV = 512
CONV_K = 31
CONV_HALO = 32
D_SG = 512
SG_GROUPS = 4
SG_CHUNK = 128
N_HEADS = 8
QK_NOPE = 64
QK_ROPE = 32
V_HEAD = 64
HEAD_PAD = 128
Q_LORA = 384
KV_LORA = 256
D_FF = 2816
FFN_K = 3
FFN_HALO = 8
ATT_SCALE = (QK_NOPE + QK_ROPE) ** -0.5
ATT_BLOCK = 256
NEG = float(jnp.finfo(jnp.float32).min)

ADAM_LR = 0.001
ADAM_B1 = 0.9
ADAM_B2 = 0.999
ADAM_EPS = 1e-08
ADAM_WD = 0.01
ADAM_STEP = 10

LANES = 128
VMEM_MB = 56

REF_CUTS = (0, 1024, 2048, 2432, 2688, 2720, 5792)
SEGS = ((0, 1024), (1024, 1024), (2048, 384), (2432, 256), (2688, 128), (2816, 3072))
D_IN = 5792
D_IN_PAD = 5888
KR_LO = 64
SEG_INNER = (0, 0, 0, 0, KR_LO, 0)

IN_NAMES = ['x', 'positions', 'mix_pre_g', 'mix_post_g', 'ffn_pre_g', 'ffn_post_g', 'w_in', 'conv_dw_w', 'conv_dw_b', 'conv_ln_g', 'conv_ln_b', 'conv_out_w', 'sg_ln_g', 'sg_ln_b', 'sg_w', 'sg_b', 'sg_out_w', 'mla_q_norm_g', 'mla_w_uq', 'mla_kv_norm_g', 'mla_w_ukv', 'mla_w_o', 'w_out', 'ffn_w_up', 'ffn_dw_w', 'ffn_dw_b', 'ffn_w_down']
WEIGHTS = IN_NAMES[2:]
SHARDED = ("w_in", "conv_dw_w", "conv_out_w", "sg_out_w", "mla_w_uq", "mla_w_ukv", "mla_w_o", "w_out", "ffn_w_up", "ffn_dw_w",
           "ffn_w_down")
FFN_BIG = ("ffn_w_up", "ffn_w_down")
MIX_GROUP = tuple(n for n in SHARDED if n not in FFN_BIG)
MIX_SMALL = tuple(n for n in MIX_GROUP if n != "w_in")
TRANSPOSED = ("ffn_w_up",)
WIRE_F32 = ("conv_dw_w", "ffn_dw_w")
REP_VECS = (("mix_pre_g", 0, 1024, 1), ("mix_post_g", 1, 1024, 1), ("ffn_pre_g", 2, 1024, 1), ("ffn_post_g", 3, 1024, 1),
            ("conv_dw_b", 4, 512, 1), ("conv_ln_g", 5, 512, 1), ("conv_ln_b", 6, 512, 1), ("sg_ln_g", 7, 512, 1),
            ("sg_ln_b", 8, 512, 1), ("mla_q_norm_g", 9, 384, 1), ("mla_kv_norm_g", 10, 256, 1), ("ffn_dw_b", 11, 5632, 1),
            ("sg_b", 12, 128, 4))
REP_ROWS = 16
REP_W = 5632


def _cparams(n_axes):
    return pltpu.CompilerParams(dimension_semantics=("arbitrary",) * n_axes, vmem_limit_bytes=VMEM_MB * 2 ** 20)


def _rows(tb, n):
    return pl.BlockSpec((tb, n), lambda i: (i, 0))


def _rows_rev(tb, n, nb):
    return pl.BlockSpec((tb, n), lambda i: (nb - 1 - i, 0))


def _const(shape):
    nd = len(shape)
    return pl.BlockSpec(shape, lambda *_: (0,) * nd, pipeline_mode=pl.Buffered(1))


def _acc(shape):
    nd = len(shape)
    return pl.BlockSpec(shape, lambda *_: (0,) * nd)


def _whole(arr):
    return pl.BlockSpec(arr.shape, lambda *_: (0,) * arr.ndim)


def _dot(a, b):
    return jnp.dot(a, b, preferred_element_type=f32)


def _dot_nt(a, b):
    return lax.dot_general(a, b, (((1,), (1,)), ((), ())), preferred_element_type=f32)


def _dot_tn(a, b):
    return lax.dot_general(a, b, (((0,), (0,)), ((), ())), preferred_element_type=f32)


def _mean(x):
    return jnp.mean(x, axis=-1, keepdims=True)


def _colsum(x):
    return jnp.sum(x, axis=0, keepdims=True)


def _rms_fwd(x, g):
    return x * lax.rsqrt(_mean(x * x) + EPS) * g


def _rms_bwd(x, g, dy):
    r = lax.rsqrt(_mean(x * x) + EPS)
    n = x * r
    dn = dy * g
    return r * (dn - n * _mean(dn * n)), _colsum(dy * n)


def _ln_stats(x):
    mu = _mean(x)
    d = x - mu
    rstd = lax.rsqrt(_mean(d * d) + EPS)
    return d * rstd, rstd


def _ln_bwd(xhat, rstd, g, dy):
    dxh = dy * g
    return rstd * (dxh - _mean(dxh) - xhat * _mean(dxh * xhat))


_GELU_C0 = math.sqrt(2.0 / math.pi)
_GELU_C1 = 0.044715


def _gelu(x):
    t = jnp.tanh(_GELU_C0 * (x + _GELU_C1 * (x * x * x)))
    return 0.5 * x * (1.0 + t)


def _gelu_and_grad(x):
    x2 = x * x
    t = jnp.tanh(_GELU_C0 * (x + _GELU_C1 * (x2 * x)))
    g = 0.5 * x * (1.0 + t)
    dg = 0.5 * (1.0 + t) + 0.5 * x * (1.0 - t * t) * (_GELU_C0 * (1.0 + 3.0 * _GELU_C1 * x2))
    return g, dg


def _swap_rope_halves(x):
    n = x.shape[1]
    half = QK_ROPE // 2
    lane = lax.broadcasted_iota(jnp.int32, x.shape, 1) % HEAD_PAD
    first = (lane >= KR_LO) & (lane < KR_LO + half)
    second = (lane >= KR_LO + half) & (lane < KR_LO + QK_ROPE)
    return jnp.where(first, pltpu.roll(x, n - half, 1), jnp.where(second, pltpu.roll(x, half, 1), 0.0))


def _tb(t):
    return min(256, t)


_HBM = pl.BlockSpec(memory_space=pltpu.HBM)
_SEM = pl.BlockSpec(memory_space=pltpu.SEMAPHORE)
_ANY = pl.BlockSpec(memory_space=pl.ANY)
_EFFECT = pltpu.SideEffectType.DATAFLOW_SIDE_EFFECTING


def _exchange_copies(modes, ins, lands, send_sems, recv_sems, loc_sems):
    x, y, c = lax.axis_index("x"), lax.axis_index("y"), lax.axis_index("c")
    me = 4 * x + 2 * y + c
    copies = []
    for a, mode in enumerate(modes):
        def src(dst_index, a=a, mode=mode):
            return ins[a].at[dst_index] if mode == "scatter" else ins[a]
        copies.append(pltpu.make_async_copy(src(me), lands[a].at[me], loc_sems.at[a]))
        for k in range(1, N_DEV):
            px = 1 - x if (k >> 2) & 1 else x
            py = 1 - y if (k >> 1) & 1 else y
            pc = 1 - c if k & 1 else c
            copies.append(pltpu.make_async_remote_copy(
                src_ref=src(4 * px + 2 * py + pc), dst_ref=lands[a].at[me],
                send_sem=send_sems.at[a * (N_DEV - 1) + k - 1], recv_sem=recv_sems.at[a * (N_DEV - 1) + k - 1],
                device_id=(px, py, pc), device_id_type=pl.DeviceIdType.MESH))
    return copies


def _exchange_start(ops, name, deps=()):
    n = len(ops)
    arrs = [arr for arr, _ in ops]
    modes = [mode for _, mode in ops]
    lands = [lax.empty((N_DEV,) + arr.shape if mode == "gather" else arr.shape, arr.dtype) for arr, mode in ops]

    def body(*refs):
        ins, land_refs = refs[:n], refs[n:2 * n]
        send_sems, recv_sems, loc_sems = refs[2 * n + len(deps):2 * n + len(deps) + 3]
        for cp in _exchange_copies(modes, ins, land_refs, send_sems, recv_sems, loc_sems):
            cp.start()
        refs[-1][...] = jnp.zeros((8, LANES), f32)

    n_rem = n * (N_DEV - 1)
    res = pl.pallas_call(
        body, name=name,
        out_shape=(pltpu.SemaphoreType.DMA((n_rem,)), pltpu.SemaphoreType.DMA((n_rem,)), pltpu.SemaphoreType.DMA((n,)),
                   *[pltpu.HBM(x.shape, x.dtype) for x in arrs + lands], jax.ShapeDtypeStruct((8, LANES), f32)),
        in_specs=[_HBM] * (2 * n) + [_ANY] * len(deps),
        out_specs=(_SEM, _SEM, _SEM, *[_HBM] * (2 * n), pl.BlockSpec(memory_space=pltpu.VMEM)),
        input_output_aliases={i: 3 + i for i in range(2 * n)},
        compiler_params=pltpu.CompilerParams(has_side_effects=_EFFECT),
    )(*[pltpu.with_memory_space_constraint(x, pltpu.HBM) for x in arrs + lands], *deps)
    return dict(modes=modes, sems=res[:3], thru=res[3:3 + 2 * n], token=res[-1], name=name)


def _exchange_wait(handle, after=()):
    modes, thru = handle["modes"], handle["thru"]
    n = len(modes)

    def body(*refs):
        ins, land_refs = refs[:n], refs[n:2 * n]
        send_sems, recv_sems, loc_sems = refs[2 * n:2 * n + 3]
        for cp in _exchange_copies(modes, ins, land_refs, send_sems, recv_sems, loc_sems):
            cp.wait()

    res = pl.pallas_call(
        body, name=handle["name"] + "_wait",
        out_shape=tuple(pltpu.HBM(x.shape, x.dtype) for x in thru),
        in_specs=[_HBM] * (2 * n) + [_SEM] * 3 + [_ANY] * len(after),
        out_specs=tuple([_HBM] * (2 * n)),
        input_output_aliases={i: i for i in range(2 * n)},
        compiler_params=pltpu.CompilerParams(has_side_effects=_EFFECT),
    )(*thru, *handle["sems"], *after)
    return res[n:]


_SAME_CORE_PEERS = ((1, 0), (0, 1), (1, 1))


def _gather_copies_one(ins, lands, send_sems, recv_sems, loc_sems):
    x, y, c = lax.axis_index("x"), lax.axis_index("y"), lax.axis_index("c")
    me = 4 * x + 2 * y + c
    targets = [(x, y, 1 - c)] + [(1 - x if fx else x, 1 - y if fy else y, c) for fx, fy in _SAME_CORE_PEERS]
    copies = []
    for a in range(len(ins)):
        copies.append(pltpu.make_async_copy(ins[a], lands[a].at[me], loc_sems.at[a]))
        for j, target in enumerate(targets):
            copies.append(pltpu.make_async_remote_copy(
                src_ref=ins[a], dst_ref=lands[a].at[me], send_sem=send_sems.at[4 * a + j], recv_sem=recv_sems.at[4 * a + j],
                device_id=target, device_id_type=pl.DeviceIdType.MESH))
    return copies


def _gather_copies_two(lands, send_sems, recv_sems):
    x, y, c = lax.axis_index("x"), lax.axis_index("y"), lax.axis_index("c")
    copies = []
    for a in range(len(lands)):
        for j, (fx, fy) in enumerate(_SAME_CORE_PEERS):
            slot = 4 * (1 - x if fx else x) + 2 * (1 - y if fy else y) + c
            copies.append(pltpu.make_async_remote_copy(
                src_ref=lands[a].at[slot], dst_ref=lands[a].at[slot], send_sem=send_sems.at[3 * a + j],
                recv_sem=recv_sems.at[3 * a + j], device_id=(x, y, 1 - c), device_id_type=pl.DeviceIdType.MESH))
    return copies


def _gather_start(arrs, name, deps=()):
    n = len(arrs)
    lands = [lax.empty((N_DEV,) + arr.shape, arr.dtype) for arr in arrs]

    def body(*refs):
        ins, land_refs = refs[:n], refs[n:2 * n]
        send_sems, recv_sems, loc_sems = refs[2 * n + len(deps):2 * n + len(deps) + 3]
        for cp in _gather_copies_one(ins, land_refs, send_sems, recv_sems, loc_sems):
            cp.start()
        refs[-1][...] = jnp.zeros((8, LANES), f32)

    res = pl.pallas_call(
        body, name=name,
        out_shape=(pltpu.SemaphoreType.DMA((4 * n,)), pltpu.SemaphoreType.DMA((4 * n,)), pltpu.SemaphoreType.DMA((n,)),
                   *[pltpu.HBM(x.shape, x.dtype) for x in arrs + lands], jax.ShapeDtypeStruct((8, LANES), f32)),
        in_specs=[_HBM] * (2 * n) + [_ANY] * len(deps),
        out_specs=(_SEM, _SEM, _SEM, *[_HBM] * (2 * n), pl.BlockSpec(memory_space=pltpu.VMEM)),
        input_output_aliases={i: 3 + i for i in range(2 * n)},
        compiler_params=pltpu.CompilerParams(has_side_effects=_EFFECT),
    )(*[pltpu.with_memory_space_constraint(x, pltpu.HBM) for x in arrs + lands], *deps)
    return dict(n=n, sems=res[:3], thru=res[3:3 + 2 * n], token=res[-1], name=name)


def _gather_forward(handle, after=()):
    n, thru = handle["n"], handle["thru"]

    def body(*refs):
        ins, land_refs = refs[:n], refs[n:2 * n]
        send_one, recv_one, loc_sems = refs[2 * n:2 * n + 3]
        send_two, recv_two = refs[2 * n + 3 + len(after):2 * n + 5 + len(after)]
        for cp in _gather_copies_one(ins, land_refs, send_one, recv_one, loc_sems):
            cp.wait()
        for cp in _gather_copies_two(land_refs, send_two, recv_two):
            cp.start()
        refs[-1][...] = jnp.zeros((8, LANES), f32)

    res = pl.pallas_call(
        body, name=handle["name"] + "_forward",
        out_shape=(pltpu.SemaphoreType.DMA((3 * n,)), pltpu.SemaphoreType.DMA((3 * n,)),
                   *[pltpu.HBM(x.shape, x.dtype) for x in thru], jax.ShapeDtypeStruct((8, LANES), f32)),
        in_specs=[_HBM] * (2 * n) + [_SEM] * 3 + [_ANY] * len(after),
        out_specs=(_SEM, _SEM, *[_HBM] * (2 * n), pl.BlockSpec(memory_space=pltpu.VMEM)),
        input_output_aliases={i: 2 + i for i in range(2 * n)},
        compiler_params=pltpu.CompilerParams(has_side_effects=_EFFECT),
    )(*thru, *handle["sems"], *after)
    return dict(n=n, sems=res[:2], lands=res[2 + n:2 + 2 * n], token=res[-1], name=handle["name"])


def _gather_wait(handle, after=()):
    n, lands = handle["n"], handle["lands"]

    def body(*refs):
        land_refs = refs[:n]
        send_two, recv_two = refs[n:n + 2]
        for cp in _gather_copies_two(land_refs, send_two, recv_two):
            cp.wait()

    return pl.pallas_call(
        body, name=handle["name"] + "_wait",
        out_shape=tuple(pltpu.HBM(x.shape, x.dtype) for x in lands),
        in_specs=[_HBM] * n + [_SEM] * 2 + [_ANY] * len(after),
        out_specs=tuple([_HBM] * n),
        input_output_aliases={i: i for i in range(n)},
        compiler_params=pltpu.CompilerParams(has_side_effects=_EFFECT),
    )(*lands, *handle["sems"], *after)


def _w_in_pieces(ns):
    out = []
    for e in range(N_DEV):
        lo, hi = ns * e, ns * (e + 1)
        for s in range(len(SEGS)):
            a, b = max(lo, REF_CUTS[s]), min(hi, REF_CUTS[s + 1])
            if a < b:
                inner = SEG_INNER[s] + a - REF_CUTS[s]
                out.append((e, a - lo, b - lo, s, inner, inner + b - a))
    return out


def _asm_w_in(g):
    _, d, ns = g.shape
    rb = 256
    pieces = _w_in_pieces(ns)

    def body(g_ref, o_ref):
        kr = SEGS[4][0]
        o_ref[:, kr:kr + KR_LO] = jnp.zeros((rb, KR_LO), g.dtype)
        o_ref[:, kr + KR_LO + QK_ROPE:kr + HEAD_PAD] = jnp.zeros((rb, HEAD_PAD - KR_LO - QK_ROPE), g.dtype)
        for e, s0, s1, seg, d0, d1 in pieces:
            off = SEGS[seg][0]
            o_ref[:, off + d0:off + d1] = g_ref[e, :, s0:s1]

    return pl.pallas_call(
        body, name="asm_w_in", grid=(d // rb,),
        in_specs=[pl.BlockSpec((N_DEV, rb, ns), lambda i: (0, i, 0))],
        out_specs=_rows(rb, D_IN_PAD), out_shape=jax.ShapeDtypeStruct((d, D_IN_PAD), g.dtype),
        compiler_params=_cparams(1),
    )(g)


def _dis_w_in(dsegs, ns):
    d = dsegs[0].shape[0]
    rb = 256
    pieces = _w_in_pieces(ns)

    def body(*refs):
        seg_refs, o_ref = refs[:len(SEGS)], refs[len(SEGS)]
        for e, s0, s1, seg, d0, d1 in pieces:
            o_ref[e, :, s0:s1] = seg_refs[seg][:, d0:d1].astype(bf16)

    return pl.pallas_call(
        body, name="dis_w_in", grid=(d // rb,),
        in_specs=[_rows(rb, wd) for _, wd in SEGS],
        out_specs=pl.BlockSpec((N_DEV, rb, ns), lambda i: (0, i, 0)),
        out_shape=jax.ShapeDtypeStruct((N_DEV, d, ns), bf16),
        compiler_params=_cparams(1),
    )(*dsegs)


def _asm_small(g_conv_out, g_sg_out, g_uq, g_ukv, g_wo, g_cdw, g_fdw):
    d = g_conv_out.shape[2] * N_DEV
    hw = N_HEADS * HEAD_PAD
    hq = QK_NOPE + QK_ROPE
    ff2 = g_fdw.shape[2] * N_DEV
    cw, fw = g_cdw.shape[2], g_fdw.shape[2]

    def body(co_ref, so_ref, uq_ref, ukv_ref, wo_ref, cdw_ref, fdw_ref, o_co, o_so, o_uq, o_k, o_v, o_wo, o_cdw, o_fdw):
        o_cdw[CONV_K:CONV_HALO, :] = jnp.zeros((CONV_HALO - CONV_K, D_CONV), f32)
        o_fdw[FFN_K:FFN_HALO, :] = jnp.zeros((FFN_HALO - FFN_K, ff2), f32)
        for e in range(N_DEV):
            cs = e * HEAD_PAD
            o_co[:, cs:cs + HEAD_PAD] = co_ref[e]
            o_so[:, cs:cs + HEAD_PAD] = so_ref[e]
            o_uq[:, cs:cs + hq] = uq_ref[e]
            o_uq[:, cs + hq:cs + HEAD_PAD] = jnp.zeros((Q_LORA, HEAD_PAD - hq), bf16)
            o_k[:, cs:cs + QK_NOPE] = ukv_ref[e, :, 0:QK_NOPE]
            o_k[:, cs + QK_NOPE:cs + HEAD_PAD] = jnp.zeros((KV_LORA, HEAD_PAD - QK_NOPE), bf16)
            o_v[:, cs:cs + V_HEAD] = ukv_ref[e, :, QK_NOPE:QK_NOPE + V_HEAD]
            o_v[:, cs + V_HEAD:cs + HEAD_PAD] = jnp.zeros((KV_LORA, HEAD_PAD - V_HEAD), bf16)
            for h in range(N_HEADS):
                o_wo[h * HEAD_PAD:h * HEAD_PAD + V_HEAD, cs:cs + HEAD_PAD] = wo_ref[e, h * V_HEAD:(h + 1) * V_HEAD, :]
                o_wo[h * HEAD_PAD + V_HEAD:(h + 1) * HEAD_PAD, cs:cs + HEAD_PAD] = jnp.zeros((HEAD_PAD - V_HEAD, HEAD_PAD), bf16)
            o_cdw[0:CONV_K, e * cw:(e + 1) * cw] = cdw_ref[e]
            o_fdw[0:FFN_K, e * fw:(e + 1) * fw] = fdw_ref[e]

    ins = (g_conv_out, g_sg_out, g_uq, g_ukv, g_wo, g_cdw, g_fdw)
    out_shape = (jax.ShapeDtypeStruct((D_CONV, d), bf16), jax.ShapeDtypeStruct((D_SG, d), bf16), jax.ShapeDtypeStruct((Q_LORA, hw), bf16),
                 jax.ShapeDtypeStruct((KV_LORA, hw), bf16), jax.ShapeDtypeStruct((KV_LORA, hw), bf16), jax.ShapeDtypeStruct((hw, d), bf16),
                 jax.ShapeDtypeStruct((CONV_HALO, D_CONV), f32), jax.ShapeDtypeStruct((FFN_HALO, ff2), f32))
    return pl.pallas_call(
        body, name="asm_small", grid=(1,),
        in_specs=[_whole(a) for a in ins], out_specs=tuple(_whole(o) for o in out_shape), out_shape=out_shape,
        compiler_params=_cparams(1),
    )(*ins)


def _dis_small(d_co, d_so, d_uq, d_k, d_v, d_wo, d_cdw, d_fdw):
    d = d_co.shape[1]
    hq = QK_NOPE + QK_ROPE
    cw, fw = D_CONV // N_DEV, d_fdw.shape[1] // N_DEV

    def body(co_ref, so_ref, uq_ref, k_ref, v_ref, wo_ref, cdw_ref, fdw_ref, o_co, o_so, o_uq, o_ukv, o_wo, o_cdw, o_fdw):
        for e in range(N_DEV):
            cs = e * HEAD_PAD
            o_co[e] = co_ref[:, cs:cs + HEAD_PAD].astype(bf16)
            o_so[e] = so_ref[:, cs:cs + HEAD_PAD].astype(bf16)
            o_uq[e] = uq_ref[:, cs:cs + hq].astype(bf16)
            o_ukv[e, :, 0:QK_NOPE] = k_ref[:, cs:cs + QK_NOPE].astype(bf16)
            o_ukv[e, :, QK_NOPE:QK_NOPE + V_HEAD] = v_ref[:, cs:cs + V_HEAD].astype(bf16)
            for h in range(N_HEADS):
                o_wo[e, h * V_HEAD:(h + 1) * V_HEAD, :] = wo_ref[h * HEAD_PAD:h * HEAD_PAD + V_HEAD, cs:cs + HEAD_PAD].astype(bf16)
            o_cdw[e] = cdw_ref[0:CONV_K, e * cw:(e + 1) * cw]
            o_fdw[e] = fdw_ref[0:FFN_K, e * fw:(e + 1) * fw]

    ins = (d_co, d_so, d_uq, d_k, d_v, d_wo, d_cdw, d_fdw)
    out_shape = (jax.ShapeDtypeStruct((N_DEV, D_CONV, d // N_DEV), bf16), jax.ShapeDtypeStruct((N_DEV, D_SG, d // N_DEV), bf16),
                 jax.ShapeDtypeStruct((N_DEV, Q_LORA, hq), bf16), jax.ShapeDtypeStruct((N_DEV, KV_LORA, QK_NOPE + V_HEAD), bf16),
                 jax.ShapeDtypeStruct((N_DEV, N_HEADS * V_HEAD, d // N_DEV), bf16), jax.ShapeDtypeStruct((N_DEV, CONV_K, cw), f32),
                 jax.ShapeDtypeStruct((N_DEV, FFN_K, fw), f32))
    return pl.pallas_call(
        body, name="dis_small", grid=(1,),
        in_specs=[_whole(a) for a in ins], out_specs=tuple(_whole(o) for o in out_shape), out_shape=out_shape,
        compiler_params=_cparams(1),
    )(*ins)


def _pack_rep(vec_grads, dsgb):
    def body(*refs):
        o_ref = refs[-1]
        o_ref[...] = jnp.zeros((REP_ROWS, REP_W), f32)
        for (_, row, width, nrows), ref in zip(REP_VECS, refs[:-1]):
            o_ref[row:row + nrows, 0:width] = ref[0:nrows, :]

    ins = tuple(vec_grads) + (dsgb,)
    return pl.pallas_call(
        body, name="pack_rep", grid=(1,),
        in_specs=[_whole(a) for a in ins], out_specs=pl.BlockSpec((REP_ROWS, REP_W), lambda i: (0, 0)),
        out_shape=jax.ShapeDtypeStruct((REP_ROWS, REP_W), f32), compiler_params=_cparams(1),
    )(*ins)


def _mix_in_fwd(l, x, g1, w_in_p):
    t, d = x.shape
    tb = _tb(t)

    def body(x_ref, g_ref, w_ref, h_ref, *outs):
        h = _rms_fwd(x_ref[...], g_ref[l:l + 1, :]).astype(bf16)
        h_ref[...] = h
        for (off, wd), o_ref in zip(SEGS, outs):
            o_ref[...] = _dot(h, w_ref[:, off:off + wd])

    return pl.pallas_call(
        body, name="mix_in_fwd", grid=(t // tb,),
        in_specs=[_rows(tb, d), _const(g1.shape), _const((d, D_IN_PAD))],
        out_specs=tuple([_rows(tb, d)] + [_rows(tb, wd) for _, wd in SEGS]),
        out_shape=tuple([jax.ShapeDtypeStruct((t, d), bf16)] + [jax.ShapeDtypeStruct((t, wd), f32) for _, wd in SEGS]),
        compiler_params=_cparams(1),
    )(x, g1, w_in_p)


def _conv_fwd(l, a_in, dw_w, dw_b, ln_g, ln_b):
    t = a_in.shape[0]
    tb = _tb(t)

    def body(a_ref, w_ref, b_ref, g_ref, be_ref, s_ref, z1_ref, win):
        @pl.when(pl.program_id(0) == 0)
        def _():
            win[0:CONV_HALO, :] = jnp.zeros((CONV_HALO, D_CONV), f32)

        a = a_ref[...]
        win[CONV_HALO:CONV_HALO + tb, :] = a[:, :D_CONV] * jax.nn.sigmoid(a[:, D_CONV:])
        for r0 in range(0, tb, LANES):
            for c0 in range(0, D_CONV, LANES):
                cs = slice(c0, c0 + LANES)
                acc = jnp.broadcast_to(b_ref[l:l + 1, cs], (LANES, LANES))
                for j in range(CONV_K):
                    acc = acc + w_ref[j:j + 1, cs] * win[pl.ds(r0 + CONV_HALO - CONV_K + 1 + j, LANES), cs]
                z1_ref[r0:r0 + LANES, cs] = acc
        zh, _ = _ln_stats(z1_ref[...])
        zl = zh * g_ref[l:l + 1, :] + be_ref[l:l + 1, :]
        s_ref[...] = (zl * jax.nn.sigmoid(zl)).astype(bf16)
        win[0:CONV_HALO, :] = win[tb:tb + CONV_HALO, :]

    return pl.pallas_call(
        body, name="conv_fwd", grid=(t // tb,),
        in_specs=[_rows(tb, 2 * D_CONV), _const((CONV_HALO, D_CONV)), _const(dw_b.shape), _const(ln_g.shape), _const(ln_b.shape)],
        out_specs=(_rows(tb, D_CONV), _rows(tb, D_CONV)),
        out_shape=(jax.ShapeDtypeStruct((t, D_CONV), bf16), jax.ShapeDtypeStruct((t, D_CONV), f32)),
        scratch_shapes=[pltpu.VMEM((tb + CONV_HALO, D_CONV), f32)],
        compiler_params=_cparams(1),
    )(a_in, dw_w, dw_b, ln_g, ln_b)


def _tril_mask():
    r = lax.broadcasted_iota(jnp.int32, (SG_CHUNK, SG_CHUNK), 0)
    c = lax.broadcasted_iota(jnp.int32, (SG_CHUNK, SG_CHUNK), 1)
    return r >= c


def _sgu_fwd(l, b_in, ln_g, ln_b, sg_w, bexp):
    t = b_in.shape[0]
    tb = _tb(t)
    gw = D_SG // SG_GROUPS

    def body(b_ref, g_ref, be_ref, w_ref, bexp_ref, um_ref):
        gl = _gelu(b_ref[...])
        u = gl[:, :D_SG]
        vh, _ = _ln_stats(gl[:, D_SG:])
        vn = (vh * g_ref[l:l + 1, :] + be_ref[l:l + 1, :]).astype(bf16)
        tri = _tril_mask()
        for g in range(SG_GROUPS):
            wg = jnp.where(tri, w_ref[l, g], 0.0).astype(bf16)
            cs = slice(g * gw, (g + 1) * gw)
            for r0 in range(0, tb, SG_CHUNK):
                rs = slice(r0, r0 + SG_CHUNK)
                mixed = _dot(wg, vn[rs, cs]) + bexp_ref[:, cs]
                um_ref[rs, cs] = (u[rs, cs] * mixed).astype(bf16)

    return pl.pallas_call(
        body, name="sgu_fwd", grid=(t // tb,),
        in_specs=[_rows(tb, 2 * D_SG), _const(ln_g.shape), _const(ln_b.shape), _const(sg_w.shape), _const((SG_CHUNK, D_SG))],
        out_specs=_rows(tb, D_SG),
        out_shape=jax.ShapeDtypeStruct((t, D_SG), bf16),
        compiler_params=_cparams(1),
    )(b_in, ln_g, ln_b, sg_w, bexp)


def _mla_proj_fwd(l, qlat, kvlat, krope, cos_t, sin_t, gq, wuq_p, gkv, wk_p, wv_p):
    t = qlat.shape[0]
    tb = _tb(t)
    hw = N_HEADS * HEAD_PAD

    def body(ql_ref, kvl_ref, kr_ref, c_ref, s_ref, gq_ref, wq_ref, gkv_ref, wk_ref, wv_ref, q_ref, k_ref, v_ref, qn_ref, kvn_ref):
        cos_b, sin_b = c_ref[...], s_ref[...]
        qn = _rms_fwd(ql_ref[...], gq_ref[l:l + 1, :]).astype(bf16)
        qn_ref[...] = qn
        q = _dot(qn, wq_ref[...])
        sw = _swap_rope_halves(q)
        for h in range(N_HEADS):
            hs = slice(h * HEAD_PAD, (h + 1) * HEAD_PAD)
            q_ref[:, hs] = (q[:, hs] * cos_b + sw[:, hs] * sin_b).astype(bf16)
        kvn = _rms_fwd(kvl_ref[...], gkv_ref[l:l + 1, :]).astype(bf16)
        kvn_ref[...] = kvn
        kr = kr_ref[...]
        kpe = kr * cos_b + _swap_rope_halves(kr) * sin_b
        k = _dot(kvn, wk_ref[...])
        for h in range(N_HEADS):
            hs = slice(h * HEAD_PAD, (h + 1) * HEAD_PAD)
            k_ref[:, hs] = (k[:, hs] + kpe).astype(bf16)
        v_ref[...] = _dot(kvn, wv_ref[...]).astype(bf16)

    return pl.pallas_call(
        body, name="mla_proj_fwd", grid=(t // tb,),
        in_specs=[_rows(tb, Q_LORA), _rows(tb, KV_LORA), _rows(tb, HEAD_PAD), _rows(tb, HEAD_PAD), _rows(tb, HEAD_PAD),
                  _const(gq.shape), _const((Q_LORA, hw)), _const(gkv.shape), _const((KV_LORA, hw)), _const((KV_LORA, hw))],
        out_specs=(_rows(tb, hw), _rows(tb, hw), _rows(tb, hw), _rows(tb, Q_LORA), _rows(tb, KV_LORA)),
        out_shape=(jax.ShapeDtypeStruct((t, hw), bf16), jax.ShapeDtypeStruct((t, hw), bf16), jax.ShapeDtypeStruct((t, hw), bf16),
                   jax.ShapeDtypeStruct((t, Q_LORA), bf16), jax.ShapeDtypeStruct((t, KV_LORA), bf16)),
        compiler_params=_cparams(1),
    )(qlat, kvlat, krope, cos_t, sin_t, gq, wuq_p, gkv, wk_p, wv_p)


def _diag_mask(tq):
    return lax.broadcasted_iota(jnp.int32, (tq, tq), 0) >= lax.broadcasted_iota(jnp.int32, (tq, tq), 1)


def _attn_fwd(q, k, v):
    t = q.shape[0]
    tq = min(ATT_BLOCK, t)

    def body(q_ref, k_ref, v_ref, o_ref, lse_ref):
        qi = pl.program_id(1)
        for i in range(t // tq):
            @pl.when(qi == i)
            def _(i=i):
                qv = q_ref[...]
                lo, hi = i * tq, (i + 1) * tq
                s_d = jnp.where(_diag_mask(tq), _dot_nt(qv, k_ref[lo:hi, :]) * ATT_SCALE, NEG)
                m = jnp.max(s_d, axis=-1, keepdims=True)
                if i > 0:
                    s_o = _dot_nt(qv, k_ref[0:lo, :]) * ATT_SCALE
                    m = jnp.maximum(m, jnp.max(s_o, axis=-1, keepdims=True))
                p_d = jnp.exp(s_d - m)
                lsum = jnp.sum(p_d, axis=-1, keepdims=True)
                acc = _dot(p_d.astype(bf16), v_ref[lo:hi, :])
                if i > 0:
                    p_o = jnp.exp(s_o - m)
                    lsum = lsum + jnp.sum(p_o, axis=-1, keepdims=True)
                    acc = acc + _dot(p_o.astype(bf16), v_ref[0:lo, :])
                o_ref[...] = (acc / lsum).astype(bf16)
                lse_ref[...] = m + jnp.log(lsum)

    return pl.pallas_call(
        body, name="attn_fwd", grid=(N_HEADS, t // tq),
        in_specs=[pl.BlockSpec((tq, HEAD_PAD), lambda h, i: (i, h)), pl.BlockSpec((t, HEAD_PAD), lambda h, i: (0, h)),
                  pl.BlockSpec((t, HEAD_PAD), lambda h, i: (0, h))],
        out_specs=(pl.BlockSpec((tq, HEAD_PAD), lambda h, i: (i, h)), pl.BlockSpec((None, tq, 1), lambda h, i: (h, i, 0))),
        out_shape=(jax.ShapeDtypeStruct((t, N_HEADS * HEAD_PAD), bf16), jax.ShapeDtypeStruct((N_HEADS, t, 1), f32)),
        compiler_params=_cparams(2),
    )(q, k, v)


def _merge_out_fwd(l, x, s, um, o, gates, conv_out_w, sg_out_w, wo_p, w_out, g2):
    t, d = x.shape
    tb = _tb(t)

    def body(x_ref, s_ref, um_ref, o_ref, gt_ref, wa_ref, wb_ref, wc_ref, wout_ref, g_ref, x1_ref, mg_ref, om_ref):
        merged = (jax.nn.sigmoid(gt_ref[:, 0:d]) * _dot(s_ref[...], wa_ref[...])
                  + jax.nn.sigmoid(gt_ref[:, d:2 * d]) * _dot(um_ref[...], wb_ref[...])
                  + jax.nn.sigmoid(gt_ref[:, 2 * d:3 * d]) * _dot(o_ref[...], wc_ref[...]))
        mb = merged.astype(bf16)
        mg_ref[...] = mb
        om = _dot(mb, wout_ref[...])
        om_ref[...] = om
        x1_ref[...] = x_ref[...] + _rms_fwd(om, g_ref[l:l + 1, :])

    hw = N_HEADS * HEAD_PAD
    return pl.pallas_call(
        body, name="merge_out_fwd", grid=(t // tb,),
        in_specs=[_rows(tb, d), _rows(tb, D_CONV), _rows(tb, D_SG), _rows(tb, hw), _rows(tb, 3 * d),
                  _const((D_CONV, d)), _const((D_SG, d)), _const((hw, d)), _const((d, d)), _const(g2.shape)],
        out_specs=(_rows(tb, d), _rows(tb, d), _rows(tb, d)),
        out_shape=(jax.ShapeDtypeStruct((t, d), f32), jax.ShapeDtypeStruct((t, d), bf16), jax.ShapeDtypeStruct((t, d), f32)),
        compiler_params=_cparams(1),
    )(x, s, um, o, gates, conv_out_w, sg_out_w, wo_p, w_out, g2)


FF_CHUNK = 1408


def _ffn_conv_cols(zbuf, w_ref, b_ref, l, nrows, c0, c1):
    acc = b_ref[l:l + 1, c0:c1] + w_ref[0:1, c0:c1] * zbuf[pl.ds(FFN_HALO - 2, nrows), c0:c1]
    acc = acc + w_ref[1:2, c0:c1] * zbuf[pl.ds(FFN_HALO - 1, nrows), c0:c1]
    return acc + w_ref[2:3, c0:c1] * zbuf[pl.ds(FFN_HALO, nrows), c0:c1]


def _ffn_fwd(l, x1, g3, w_up, dw_w, dw_b, w_down, g4):
    t, d = x1.shape
    tb = _tb(t)
    ff2 = 2 * D_FF

    def body(x_ref, g3_ref, wup_ref, dww_ref, dwb_ref, wdn_ref, g4_ref, x2_ref, h2_ref, z_ref, zc_ref, act_ref, f_ref, zbuf):
        @pl.when(pl.program_id(0) == 0)
        def _():
            zbuf[0:FFN_HALO, :] = jnp.zeros((FFN_HALO, ff2), f32)

        xv = x_ref[...]
        h2 = _rms_fwd(xv, g3_ref[l:l + 1, :]).astype(bf16)
        h2_ref[...] = h2
        for c0 in range(0, ff2, FF_CHUNK):
            zv = _dot_nt(h2, wup_ref[c0:c0 + FF_CHUNK, :])
            z_ref[:, c0:c0 + FF_CHUNK] = zv.astype(bf16)
            zbuf[FFN_HALO:FFN_HALO + tb, c0:c0 + FF_CHUNK] = zv
        facc = jnp.zeros((tb, d), f32)
        for c0 in range(0, D_FF, FF_CHUNK):
            gg = _ffn_conv_cols(zbuf, dww_ref, dwb_ref, l, tb, c0, c0 + FF_CHUNK)
            vv = _ffn_conv_cols(zbuf, dww_ref, dwb_ref, l, tb, D_FF + c0, D_FF + c0 + FF_CHUNK)
            zc_ref[:, c0:c0 + FF_CHUNK] = gg.astype(bf16)
            zc_ref[:, D_FF + c0:D_FF + c0 + FF_CHUNK] = vv.astype(bf16)
            a = (_gelu(gg) * vv).astype(bf16)
            act_ref[:, c0:c0 + FF_CHUNK] = a
            facc = facc + _dot(a, wdn_ref[c0:c0 + FF_CHUNK, :])
        f_ref[...] = facc
        x2_ref[...] = xv + _rms_fwd(facc, g4_ref[l:l + 1, :])
        zbuf[0:FFN_HALO, :] = zbuf[tb:tb + FFN_HALO, :]

    return pl.pallas_call(
        body, name="ffn_fwd", grid=(t // tb,),
        in_specs=[_rows(tb, d), _const(g3.shape), _const((ff2, d)), _const((FFN_HALO, ff2)), _const(dw_b.shape), _const((D_FF, d)),
                  _const(g4.shape)],
        out_specs=(_rows(tb, d), _rows(tb, d), _rows(tb, ff2), _rows(tb, ff2), _rows(tb, D_FF), _rows(tb, d)),
        out_shape=(jax.ShapeDtypeStruct((t, d), f32), jax.ShapeDtypeStruct((t, d), bf16), jax.ShapeDtypeStruct((t, ff2), bf16),
                   jax.ShapeDtypeStruct((t, ff2), bf16), jax.ShapeDtypeStruct((t, D_FF), bf16), jax.ShapeDtypeStruct((t, d), f32)),
        scratch_shapes=[pltpu.VMEM((tb + FFN_HALO, ff2), f32)],
        compiler_params=_cparams(1),
    )(x1, g3, w_up, dw_w, dw_b, w_down, g4)


def _loss_fwd_bwd(y, target):
    t, d = y.shape
    tb = _tb(t)

    def body(y_ref, t_ref, dy_ref, loss_ref):
        @pl.when(pl.program_id(0) == 0)
        def _():
            loss_ref[...] = jnp.zeros((1, LANES), f32)

        e = y_ref[...] - t_ref[...]
        dy_ref[...] = e * (1.0 / d)
        loss_ref[...] += 0.5 * jnp.sum(_mean(e * e))

    dy, loss = pl.pallas_call(
        body, name="loss", grid=(t // tb,),
        in_specs=[_rows(tb, d), _rows(tb, d)],
        out_specs=(_rows(tb, d), _acc((1, LANES))),
        out_shape=(jax.ShapeDtypeStruct((t, d), f32), jax.ShapeDtypeStruct((1, LANES), f32)),
        compiler_params=_cparams(1),
    )(y, target)
    return dy, loss[0, 0]


def _ffn_bwd(l, dx2, x1, f, z, zc, w_up, dw_w, w_down, g3, g4, deps=()):
    t, d = x1.shape
    tb = min(128, t)
    nb = t // tb
    ff2 = 2 * D_FF
    hrows = 16
    per_h = tb // hrows

    def body(*refs):
        (dx2_ref, x1_ref, f_ref, z_ref, zp_ref, zc_ref, wup_ref, dww_ref, wdn_ref, g3_ref, g4_ref,
         dx1_ref, df_ref, dz_ref, dg4_ref, dg3_ref, ddwb_ref, ddww_ref, zbuf, dzc) = refs[len(deps):]
        i = pl.program_id(0)
        b = nb - 1 - i

        @pl.when(i == 0)
        def _():
            dg4_ref[...] = jnp.zeros_like(dg4_ref)
            dg3_ref[...] = jnp.zeros_like(dg3_ref)
            ddwb_ref[...] = jnp.zeros_like(ddwb_ref)
            ddww_ref[...] = jnp.zeros_like(ddww_ref)
            dzc[tb:tb + FFN_HALO, :] = jnp.zeros((FFN_HALO, ff2), f32)

        dout = dx2_ref[...]
        df, dg4 = _rms_bwd(f_ref[...], g4_ref[l:l + 1, :], dout)
        dg4_ref[...] += dg4
        dfb = df.astype(bf16)
        df_ref[...] = dfb
        zbuf[0:FFN_HALO, :] = jnp.where(b > 0, zp_ref[hrows - FFN_HALO:hrows, :].astype(f32), 0.0)
        zbuf[FFN_HALO:FFN_HALO + tb, :] = z_ref[...].astype(f32)
        for c0 in range(0, D_FF, FF_CHUNK):
            dact = _dot_nt(dfb, wdn_ref[c0:c0 + FF_CHUNK, :])
            gel, dgel = _gelu_and_grad(zc_ref[:, c0:c0 + FF_CHUNK].astype(f32))
            dzc[0:tb, c0:c0 + FF_CHUNK] = dact * zc_ref[:, D_FF + c0:D_FF + c0 + FF_CHUNK].astype(f32) * dgel
            dzc[0:tb, D_FF + c0:D_FF + c0 + FF_CHUNK] = dact * gel
        dh2 = jnp.zeros((tb, d), f32)
        for c0 in range(0, ff2, FF_CHUNK):
            cs = slice(c0, c0 + FF_CHUNK)
            d0 = dzc[0:tb, cs]
            ddwb_ref[:, cs] += _colsum(d0)
            for j in range(FFN_K):
                ddww_ref[j:j + 1, cs] += _colsum(d0 * zbuf[pl.ds(FFN_HALO - 2 + j, tb), cs])
            dzv = dww_ref[2:3, cs] * d0 + dww_ref[1:2, cs] * dzc[pl.ds(1, tb), cs] + dww_ref[0:1, cs] * dzc[pl.ds(2, tb), cs]
            dzb = dzv.astype(bf16)
            dz_ref[:, cs] = dzb
            dh2 = dh2 + _dot(dzb, wup_ref[cs, :])
        dzc[tb:tb + FFN_HALO, :] = dzc[0:FFN_HALO, :]
        dxn, dg3 = _rms_bwd(x1_ref[...], g3_ref[l:l + 1, :], dh2)
        dg3_ref[...] += dg3
        dx1_ref[...] = dout + dxn

    return pl.pallas_call(
        body, name="ffn_bwd", grid=(nb,),
        in_specs=[_ANY] * len(deps) + [_rows_rev(tb, d, nb), _rows_rev(tb, d, nb), _rows_rev(tb, d, nb), _rows_rev(tb, ff2, nb),
                  pl.BlockSpec((hrows, ff2), lambda i: (jnp.maximum((nb - 1 - i) * per_h - 1, 0), 0)), _rows_rev(tb, ff2, nb),
                  _const((ff2, d)), _const((FFN_HALO, ff2)), _const((D_FF, d)), _const(g3.shape), _const(g4.shape)],
        out_specs=(_rows_rev(tb, d, nb), _rows_rev(tb, d, nb), _rows_rev(tb, ff2, nb), _acc((1, d)), _acc((1, d)), _acc((1, ff2)),
                   _acc((FFN_HALO, ff2))),
        out_shape=(jax.ShapeDtypeStruct((t, d), f32), jax.ShapeDtypeStruct((t, d), bf16), jax.ShapeDtypeStruct((t, ff2), bf16),
                   jax.ShapeDtypeStruct((1, d), f32), jax.ShapeDtypeStruct((1, d), f32), jax.ShapeDtypeStruct((1, ff2), f32),
                   jax.ShapeDtypeStruct((FFN_HALO, ff2), f32)),
        scratch_shapes=[pltpu.VMEM((tb + FFN_HALO, ff2), f32), pltpu.VMEM((tb + FFN_HALO, ff2), f32)],
        compiler_params=_cparams(1),
    )(*deps, dx2, x1, f, z, z, zc, w_up, dw_w, w_down, g3, g4)


def _merge_out_bwd(l, dx1, om, s, um, o, gates, conv_out_w, sg_out_w, wo_p, w_out, g2, deps=()):
    t, d = dx1.shape
    tb = _tb(t)
    hw = N_HEADS * HEAD_PAD

    def body(*refs):
        (dx_ref, om_ref, s_ref, um_ref, o_ref, gt_ref, wa_ref, wb_ref, wc_ref, wout_ref, g_ref,
         dom_ref, dgt_ref, dya_ref, dyb_ref, dyc_ref, ds_ref, dum_ref, do_ref, dg2_ref) = refs[len(deps):]

        @pl.when(pl.program_id(0) == 0)
        def _():
            dg2_ref[...] = jnp.zeros_like(dg2_ref)

        dom, dg2 = _rms_bwd(om_ref[...], g_ref[l:l + 1, :], dx_ref[...])
        dg2_ref[...] += dg2
        domb = dom.astype(bf16)
        dom_ref[...] = domb
        dmerged = _dot_nt(domb, wout_ref[...])
        branches = ((s_ref, wa_ref, dya_ref, ds_ref), (um_ref, wb_ref, dyb_ref, dum_ref), (o_ref, wc_ref, dyc_ref, do_ref))
        for br, (in_ref, w_ref, dy_ref, din_ref) in enumerate(branches):
            yv = _dot(in_ref[...], w_ref[...])
            sg = jax.nn.sigmoid(gt_ref[:, br * d:(br + 1) * d])
            dyb = (dmerged * sg).astype(bf16)
            dy_ref[...] = dyb
            dgt_ref[:, br * d:(br + 1) * d] = (dmerged * yv * sg * (1.0 - sg)).astype(bf16)
            din_ref[...] = _dot_nt(dyb, w_ref[...]).astype(din_ref.dtype)

    return pl.pallas_call(
        body, name="merge_out_bwd", grid=(t // tb,),
        in_specs=[_ANY] * len(deps) + [_rows(tb, d), _rows(tb, d), _rows(tb, D_CONV), _rows(tb, D_SG), _rows(tb, hw), _rows(tb, 3 * d),
                  _const((D_CONV, d)), _const((D_SG, d)), _const((hw, d)), _const((d, d)), _const(g2.shape)],
        out_specs=(_rows(tb, d), _rows(tb, 3 * d), _rows(tb, d), _rows(tb, d), _rows(tb, d), _rows(tb, D_CONV), _rows(tb, D_SG),
                   _rows(tb, hw), _acc((1, d))),
        out_shape=(jax.ShapeDtypeStruct((t, d), bf16), jax.ShapeDtypeStruct((t, 3 * d), bf16), jax.ShapeDtypeStruct((t, d), bf16),
                   jax.ShapeDtypeStruct((t, d), bf16), jax.ShapeDtypeStruct((t, d), bf16), jax.ShapeDtypeStruct((t, D_CONV), f32),
                   jax.ShapeDtypeStruct((t, D_SG), f32), jax.ShapeDtypeStruct((t, hw), bf16), jax.ShapeDtypeStruct((1, d), f32)),
        compiler_params=_cparams(1),
    )(*deps, dx1, om, s, um, o, gates, conv_out_w, sg_out_w, wo_p, w_out, g2)


def _attn_bwd(q, k, v, o, lse, do):
    t = q.shape[0]
    tq = min(ATT_BLOCK, t)
    hw = N_HEADS * HEAD_PAD

    def body(q_ref, k_ref, v_ref, o_ref, lse_ref, do_ref, dq_ref, dk_ref, dv_ref):
        qi = pl.program_id(1)

        @pl.when(qi == 0)
        def _():
            dk_ref[...] = jnp.zeros_like(dk_ref)
            dv_ref[...] = jnp.zeros_like(dv_ref)

        def keys(lo, hi, qv, dov, lse, delta, diagonal):
            kj, vj = k_ref[lo:hi, :], v_ref[lo:hi, :]
            p = jnp.exp(_dot_nt(qv, kj) * ATT_SCALE - lse)
            if diagonal:
                p = jnp.where(_diag_mask(tq), p, 0.0)
            ds = (p * (_dot_nt(dov, vj) - delta) * ATT_SCALE).astype(bf16)
            dk_ref[lo:hi, :] += _dot_tn(ds, qv)
            dv_ref[lo:hi, :] += _dot_tn(p.astype(bf16), dov)
            return _dot(ds, kj)

        for i in range(t // tq):
            @pl.when(qi == i)
            def _(i=i):
                qv, dov, lse = q_ref[...], do_ref[...], lse_ref[...]
                delta = jnp.sum(dov.astype(f32) * o_ref[...].astype(f32), axis=-1, keepdims=True)
                dq = keys(i * tq, (i + 1) * tq, qv, dov, lse, delta, True)
                if i > 0:
                    dq = dq + keys(0, i * tq, qv, dov, lse, delta, False)
                dq_ref[...] = dq

    blk_q = pl.BlockSpec((tq, HEAD_PAD), lambda h, i: (i, h))
    blk_kv = pl.BlockSpec((t, HEAD_PAD), lambda h, i: (0, h))
    return pl.pallas_call(
        body, name="attn_bwd", grid=(N_HEADS, t // tq),
        in_specs=[blk_q, blk_kv, blk_kv, blk_q, pl.BlockSpec((None, tq, 1), lambda h, i: (h, i, 0)), blk_q],
        out_specs=(blk_q, blk_kv, blk_kv),
        out_shape=(jax.ShapeDtypeStruct((t, hw), f32), jax.ShapeDtypeStruct((t, hw), f32), jax.ShapeDtypeStruct((t, hw), f32)),
        compiler_params=_cparams(2),
    )(q, k, v, o, lse, do)


def _mla_proj_bwd(l, dq, dk, dv, qlat, kvlat, cos_t, sin_t, gq, wuq_p, gkv, wk_p, wv_p):
    t = qlat.shape[0]
    tb = _tb(t)
    hw = N_HEADS * HEAD_PAD

    def body(dq_ref, dk_ref, dv_ref, ql_ref, kvl_ref, c_ref, s_ref, gq_ref, wq_ref, gkv_ref, wk_ref, wv_ref,
             dqb_ref, dkb_ref, dvb_ref, dql_ref, dkvl_ref, dkr_ref, dgq_ref, dgkv_ref):
        @pl.when(pl.program_id(0) == 0)
        def _():
            dgq_ref[...] = jnp.zeros_like(dgq_ref)
            dgkv_ref[...] = jnp.zeros_like(dgkv_ref)

        cos_b, sin_b = c_ref[...], s_ref[...]
        for h in range(N_HEADS):
            hs = slice(h * HEAD_PAD, (h + 1) * HEAD_PAD)
            dqh = dq_ref[:, hs]
            dqb_ref[:, hs] = (dqh * cos_b + _swap_rope_halves(dqh * sin_b)).astype(bf16)
        dqn = _dot_nt(dqb_ref[...], wq_ref[...])
        dql, dgq = _rms_bwd(ql_ref[...], gq_ref[l:l + 1, :], dqn)
        dgq_ref[...] += dgq
        dql_ref[...] = dql.astype(bf16)
        dkv_full = dk_ref[...]
        dkb = dkv_full.astype(bf16)
        dkb_ref[...] = dkb
        dkpe = dkv_full[:, 0:HEAD_PAD]
        for h in range(1, N_HEADS):
            dkpe = dkpe + dkv_full[:, h * HEAD_PAD:(h + 1) * HEAD_PAD]
        dkr_ref[...] = (dkpe * cos_b + _swap_rope_halves(dkpe * sin_b)).astype(bf16)
        dvb = dv_ref[...].astype(bf16)
        dvb_ref[...] = dvb
        dkvn = _dot_nt(dkb, wk_ref[...]) + _dot_nt(dvb, wv_ref[...])
        dkvl, dgkv = _rms_bwd(kvl_ref[...], gkv_ref[l:l + 1, :], dkvn)
        dgkv_ref[...] += dgkv
        dkvl_ref[...] = dkvl.astype(bf16)

    return pl.pallas_call(
        body, name="mla_proj_bwd", grid=(t // tb,),
        in_specs=[_rows(tb, hw), _rows(tb, hw), _rows(tb, hw), _rows(tb, Q_LORA), _rows(tb, KV_LORA), _rows(tb, HEAD_PAD),
                  _rows(tb, HEAD_PAD), _const(gq.shape), _const((Q_LORA, hw)), _const(gkv.shape), _const((KV_LORA, hw)),
                  _const((KV_LORA, hw))],
        out_specs=(_rows(tb, hw), _rows(tb, hw), _rows(tb, hw), _rows(tb, Q_LORA), _rows(tb, KV_LORA), _rows(tb, HEAD_PAD),
                   _acc((1, Q_LORA)), _acc((1, KV_LORA))),
        out_shape=(jax.ShapeDtypeStruct((t, hw), bf16), jax.ShapeDtypeStruct((t, hw), bf16), jax.ShapeDtypeStruct((t, hw), bf16),
                   jax.ShapeDtypeStruct((t, Q_LORA), bf16), jax.ShapeDtypeStruct((t, KV_LORA), bf16),
                   jax.ShapeDtypeStruct((t, HEAD_PAD), bf16), jax.ShapeDtypeStruct((1, Q_LORA), f32),
                   jax.ShapeDtypeStruct((1, KV_LORA), f32)),
        compiler_params=_cparams(1),
    )(dq, dk, dv, qlat, kvlat, cos_t, sin_t, gq, wuq_p, gkv, wk_p, wv_p)


def _sgu_bwd(l, b_in, dum, ln_g, ln_b, sg_w, bexp):
    t = b_in.shape[0]
    tb = _tb(t)
    gw = D_SG // SG_GROUPS

    def body(b_ref, dum_ref, g_ref, be_ref, w_ref, bexp_ref, db_ref, dw_ref, dsgb_ref, dlg_ref, dlb_ref, dvn_s):
        @pl.when(pl.program_id(0) == 0)
        def _():
            dw_ref[...] = jnp.zeros_like(dw_ref)
            dsgb_ref[...] = jnp.zeros_like(dsgb_ref)
            dlg_ref[...] = jnp.zeros_like(dlg_ref)
            dlb_ref[...] = jnp.zeros_like(dlb_ref)

        gl, dgl = _gelu_and_grad(b_ref[...])
        u = gl[:, :D_SG]
        vh, rstd = _ln_stats(gl[:, D_SG:])
        ln_gain = g_ref[l:l + 1, :]
        vn = (vh * ln_gain + be_ref[l:l + 1, :]).astype(bf16)
        dumv = dum_ref[...]
        tri = _tril_mask()
        ones = jnp.ones((FFN_HALO, gw), f32)
        for g in range(SG_GROUPS):
            wg = jnp.where(tri, w_ref[l, g], 0.0).astype(bf16)
            cs = slice(g * gw, (g + 1) * gw)
            for r0 in range(0, tb, SG_CHUNK):
                rs = slice(r0, r0 + SG_CHUNK)
                vblk = vn[rs, cs]
                mixed = _dot(wg, vblk) + bexp_ref[:, cs]
                db_ref[rs, cs] = (dumv[rs, cs] * mixed * dgl[rs, cs]).astype(bf16)
                dmix = dumv[rs, cs] * u[rs, cs]
                dmb = dmix.astype(bf16)
                dw_ref[g] += jnp.where(tri, _dot_nt(dmb, vblk), 0.0)
                rowsum = lax.dot_general(ones, dmix, (((1,), (1,)), ((), ())), preferred_element_type=f32,
                                         precision=lax.Precision.HIGHEST)
                dsgb_ref[g:g + 1, :] += rowsum[0:1, :]
                dvn_s[rs, cs] = _dot_tn(wg, dmb)
        dvn = dvn_s[...]
        dlg_ref[...] += _colsum(dvn * vh)
        dlb_ref[...] += _colsum(dvn)
        db_ref[:, D_SG:] = (_ln_bwd(vh, rstd, ln_gain, dvn) * dgl[:, D_SG:]).astype(bf16)

    return pl.pallas_call(
        body, name="sgu_bwd", grid=(t // tb,),
        in_specs=[_rows(tb, 2 * D_SG), _rows(tb, D_SG), _const(ln_g.shape), _const(ln_b.shape), _const(sg_w.shape),
                  _const((SG_CHUNK, D_SG))],
        out_specs=(_rows(tb, 2 * D_SG), _acc((SG_GROUPS, SG_CHUNK, SG_CHUNK)), _acc((FFN_HALO, SG_CHUNK)), _acc((1, D_SG)),
                   _acc((1, D_SG))),
        out_shape=(jax.ShapeDtypeStruct((t, 2 * D_SG), bf16), jax.ShapeDtypeStruct((SG_GROUPS, SG_CHUNK, SG_CHUNK), f32),
                   jax.ShapeDtypeStruct((FFN_HALO, SG_CHUNK), f32), jax.ShapeDtypeStruct((1, D_SG), f32),
                   jax.ShapeDtypeStruct((1, D_SG), f32)),
        scratch_shapes=[pltpu.VMEM((tb, D_SG), f32)],
        compiler_params=_cparams(1),
    )(b_in, dum, ln_g, ln_b, sg_w, bexp)


def _conv_bwd(l, a_in, z1, ds, dw_w, ln_g, ln_b):
    t = a_in.shape[0]
    tb = _tb(t)
    nb = t // tb
    per_halo = tb // CONV_HALO

    def body(a_ref, ap_ref, z1_ref, ds_ref, w_ref, g_ref, be_ref, da_ref, ddww_ref, ddwb_ref, dlg_ref, dlb_ref, win, dzb):
        i = pl.program_id(0)
        b = nb - 1 - i

        @pl.when(i == 0)
        def _():
            ddww_ref[...] = jnp.zeros_like(ddww_ref)
            ddwb_ref[...] = jnp.zeros_like(ddwb_ref)
            dlg_ref[...] = jnp.zeros_like(dlg_ref)
            dlb_ref[...] = jnp.zeros_like(dlb_ref)
            dzb[tb:tb + CONV_HALO, :] = jnp.zeros((CONV_HALO, D_CONV), f32)

        a = a_ref[...]
        val = a[:, :D_CONV]
        sg = jax.nn.sigmoid(a[:, D_CONV:])
        ap = ap_ref[...]
        win[0:CONV_HALO, :] = jnp.where(b > 0, ap[:, :D_CONV] * jax.nn.sigmoid(ap[:, D_CONV:]), 0.0)
        win[CONV_HALO:CONV_HALO + tb, :] = val * sg
        zh, rstd = _ln_stats(z1_ref[...])
        ln_gain = g_ref[l:l + 1, :]
        zl = zh * ln_gain + be_ref[l:l + 1, :]
        sgl = jax.nn.sigmoid(zl)
        dzl = ds_ref[...] * (sgl * (1.0 + zl * (1.0 - sgl)))
        dlg_ref[...] += _colsum(dzl * zh)
        dlb_ref[...] += _colsum(dzl)
        dz1 = _ln_bwd(zh, rstd, ln_gain, dzl)
        dzb[0:tb, :] = dz1
        ddwb_ref[...] += _colsum(dz1)
        dgate_f = val * sg * (1.0 - sg)
        for c0 in range(0, D_CONV, LANES):
            cs = slice(c0, c0 + LANES)
            for r0 in range(0, tb, LANES):
                d1 = dzb[r0:r0 + LANES, cs]
                acc = jnp.zeros((LANES, LANES), f32)
                for j in range(CONV_K):
                    ddww_ref[j:j + 1, cs] += _colsum(d1 * win[pl.ds(r0 + CONV_HALO - CONV_K + 1 + j, LANES), cs])
                    acc = acc + w_ref[j:j + 1, cs] * dzb[pl.ds(r0 + CONV_K - 1 - j, LANES), cs]
                da_ref[r0:r0 + LANES, cs] = (acc * sg[r0:r0 + LANES, cs]).astype(bf16)
                da_ref[r0:r0 + LANES, c0 + D_CONV:c0 + D_CONV + LANES] = (acc * dgate_f[r0:r0 + LANES, cs]).astype(bf16)
        dzb[tb:tb + CONV_HALO, :] = dzb[0:CONV_HALO, :]

    return pl.pallas_call(
        body, name="conv_bwd", grid=(nb,),
        in_specs=[_rows_rev(tb, 2 * D_CONV, nb),
                  pl.BlockSpec((CONV_HALO, 2 * D_CONV), lambda i: (jnp.maximum((nb - 1 - i) * per_halo - 1, 0), 0)),
                  _rows_rev(tb, D_CONV, nb), _rows_rev(tb, D_CONV, nb), _const((CONV_HALO, D_CONV)), _const(ln_g.shape),
                  _const(ln_b.shape)],
        out_specs=(_rows_rev(tb, 2 * D_CONV, nb), _acc((CONV_HALO, D_CONV)), _acc((1, D_CONV)), _acc((1, D_CONV)), _acc((1, D_CONV))),
        out_shape=(jax.ShapeDtypeStruct((t, 2 * D_CONV), bf16), jax.ShapeDtypeStruct((CONV_HALO, D_CONV), f32),
                   jax.ShapeDtypeStruct((1, D_CONV), f32), jax.ShapeDtypeStruct((1, D_CONV), f32), jax.ShapeDtypeStruct((1, D_CONV), f32)),
        scratch_shapes=[pltpu.VMEM((tb + CONV_HALO, D_CONV), f32), pltpu.VMEM((tb + CONV_HALO, D_CONV), f32)],
        compiler_params=_cparams(1),
    )(a_in, a_in, z1, ds, dw_w, ln_g, ln_b)


def _mix_in_bwd(l, x, g1, dxres, dsegs, w_in_p, deps=()):
    t, d = x.shape
    tb = _tb(t)

    def body(*refs):
        x_ref, g_ref, dr_ref = refs[len(deps):len(deps) + 3]
        rest = refs[len(deps) + 3:]
        dseg_refs, w_ref, dx_ref, dg_ref = rest[:len(SEGS)], rest[len(SEGS)], rest[len(SEGS) + 1], rest[len(SEGS) + 2]

        @pl.when(pl.program_id(0) == 0)
        def _():
            dg_ref[...] = jnp.zeros_like(dg_ref)

        dh = jnp.zeros((tb, d), f32)
        for (off, wd), ds_ref in zip(SEGS, dseg_refs):
            dh = dh + _dot_nt(ds_ref[...], w_ref[:, off:off + wd])
        dxn, dg = _rms_bwd(x_ref[...], g_ref[l:l + 1, :], dh)
        dg_ref[...] += dg
        dx_ref[...] = dr_ref[...] + dxn

    return pl.pallas_call(
        body, name="mix_in_bwd", grid=(t // tb,),
        in_specs=([_ANY] * len(deps) + [_rows(tb, d), _const(g1.shape), _rows(tb, d)] + [_rows(tb, wd) for _, wd in SEGS]
                  + [_const((d, D_IN_PAD))]),
        out_specs=(_rows(tb, d), _acc((1, d))),
        out_shape=(jax.ShapeDtypeStruct((t, d), f32), jax.ShapeDtypeStruct((1, d), f32)),
        compiler_params=_cparams(1),
    )(*deps, x, g1, dxres, *dsegs, w_in_p)


def _pick_block(n, cap=512):
    for b in (cap, 384, 256, 128):
        if b <= cap and n % b == 0:
            return b
    return n


def _wgrad(a, b, name, out_dtype=f32):
    t, kdim = a.shape
    n = b.shape[1]
    bk, bn = _pick_block(kdim), _pick_block(n, cap=1024)

    def body(a_ref, b_ref, o_ref):
        o_ref[...] = _dot_tn(a_ref[...], b_ref[...]).astype(out_dtype)

    return pl.pallas_call(
        body, name=name, grid=(kdim // bk, n // bn),
        in_specs=[pl.BlockSpec((t, bk), lambda i, j: (0, i)), pl.BlockSpec((t, bn), lambda i, j: (0, j))],
        out_specs=pl.BlockSpec((bk, bn), lambda i, j: (i, j)),
        out_shape=jax.ShapeDtypeStruct((kdim, n), out_dtype),
        compiler_params=_cparams(2),
    )(a, b)


def _wgrad_multi(a, bs, name, deps=()):
    t, kdim = a.shape
    bk = _pick_block(kdim)
    nb = len(bs)

    def body(*refs):
        a_ref, refs = refs[len(deps)], refs[len(deps) + 1:]
        at = a_ref[...].T
        for b_ref, o_ref in zip(refs[:nb], refs[nb:]):
            o_ref[...] = _dot(at, b_ref[...])

    return pl.pallas_call(
        body, name=name, grid=(kdim // bk,),
        in_specs=[_ANY] * len(deps) + [pl.BlockSpec((t, bk), lambda i: (0, i))] + [_const(b.shape) for b in bs],
        out_specs=tuple(pl.BlockSpec((bk, b.shape[1]), lambda i: (i, 0)) for b in bs),
        out_shape=tuple(jax.ShapeDtypeStruct((kdim, b.shape[1]), f32) for b in bs),
        compiler_params=_cparams(1),
    )(*deps, a, *bs)


_BC1 = 1.0 - ADAM_B1 ** ADAM_STEP
_BC2 = 1.0 - ADAM_B2 ** ADAM_STEP


def _adam_math(g, w, m, v):
    nm = ADAM_B1 * m + (1.0 - ADAM_B1) * g
    nv = ADAM_B2 * v + (1.0 - ADAM_B2) * (g * g)
    delta = -ADAM_LR * ((nm / _BC1) / (jnp.sqrt(nv / _BC2) + ADAM_EPS) + ADAM_WD * w)
    return delta, nm, nv


def _slot_sum(r_ref, index=()):
    g = r_ref[(0,) + index].astype(f32)
    for s in range(1, N_DEV):
        g = g + r_ref[(s,) + index].astype(f32)
    return g


def _adamw_shard(l, recv, w, m, v, prev, name, deps=()):
    _, k, ns = recv.shape
    rb = next((c for c in (256, 192, 176, 128) if k % c == 0), k)
    blk = pl.BlockSpec((None, rb, ns), lambda i: (l, i, 0))

    def body(r_ref, w_ref, m_ref, v_ref, *rest):
        g_ref, d_ref, nm_ref, nv_ref = rest[-4:]
        g = _slot_sum(r_ref)
        g_ref[...] = g
        d_ref[...], nm_ref[...], nv_ref[...] = _adam_math(g, w_ref[...], m_ref[...], v_ref[...])

    out = jax.ShapeDtypeStruct(w.shape, f32)
    n_prev = 0 if prev is None else 4
    return pl.pallas_call(
        body, name=name, grid=(k // rb,),
        in_specs=[pl.BlockSpec((N_DEV, rb, ns), lambda i: (0, i, 0)), blk, blk, blk] + [_ANY] * (n_prev + len(deps)),
        out_specs=(blk, blk, blk, blk), out_shape=(out, out, out, out),
        input_output_aliases={4 + j: j for j in range(n_prev)},
        compiler_params=_cparams(1),
    )(recv, w, m, v, *(prev or ()), *deps)


def _adamw_small(l, recvs, ws, ms, vs, prevs, name):
    nt = len(recvs)
    n_prev = 0 if prevs is None else 4 * nt

    def body(*refs):
        r_refs, w_refs, m_refs, v_refs = (refs[i * nt:(i + 1) * nt] for i in range(4))
        outs = refs[4 * nt + n_prev:]
        for i in range(nt):
            g = _slot_sum(r_refs[i])
            g_ref, d_ref, nm_ref, nv_ref = outs[4 * i:4 * i + 4]
            g_ref[...] = g
            d_ref[...], nm_ref[...], nv_ref[...] = _adam_math(g, w_refs[i][...], m_refs[i][...], v_refs[i][...])

    layer = [pl.BlockSpec((None,) + w.shape[1:], lambda i: (l, 0, 0)) for w in ws]
    return pl.pallas_call(
        body, name=name, grid=(1,),
        in_specs=[_whole(r) for r in recvs] + layer * 3 + [_ANY] * n_prev,
        out_specs=tuple(spec for spec in layer for _ in range(4)),
        out_shape=tuple(jax.ShapeDtypeStruct(w.shape, f32) for w in ws for _ in range(4)),
        input_output_aliases={4 * nt + j: j for j in range(n_prev)},
        compiler_params=_cparams(1),
    )(*recvs, *ws, *ms, *vs, *(prevs or ()))


def _adamw_rep(recvs, ws, ms, vs):
    depth = len(recvs)
    nt = len(REP_VECS)

    def body(*refs):
        r_refs = refs[:depth]
        w_refs, m_refs, v_refs = (refs[depth + i * nt:depth + (i + 1) * nt] for i in range(3))
        outs = refs[depth + 3 * nt:]
        for ti, (_, row, width, nrows) in enumerate(REP_VECS):
            for l in range(depth):
                g = r_refs[l][0, row:row + nrows, 0:width]
                for s in range(1, N_DEV):
                    g = g + r_refs[l][s, row:row + nrows, 0:width]
                pick = (lambda ref: ref[l]) if nrows > 1 else (lambda ref: ref[l:l + 1, :])
                delta, nm, nv = _adam_math(g, pick(w_refs[ti]), pick(m_refs[ti]), pick(v_refs[ti]))
                for o_ref, val in zip(outs[4 * ti:4 * ti + 4], (g, delta, nm, nv)):
                    if nrows > 1:
                        o_ref[l] = val
                    else:
                        o_ref[l:l + 1, :] = val

    ins = tuple(recvs) + tuple(ws) + tuple(ms) + tuple(vs)
    out_shape = tuple(jax.ShapeDtypeStruct(w.shape, f32) for w in ws for _ in range(4))
    return pl.pallas_call(
        body, name="adamw_rep", grid=(1,),
        in_specs=[_whole(a) for a in ins], out_specs=tuple(_whole(o) for o in out_shape), out_shape=out_shape,
        compiler_params=_cparams(1),
    )(*ins)


def _adamw_sg_w(recvs, w, m, v):
    depth = len(recvs)

    def body(*refs):
        r_refs = refs[:depth]
        w_ref, m_ref, v_ref = refs[depth:depth + 3]
        outs = refs[depth + 3:]
        for l in range(depth):
            for gi in range(SG_GROUPS):
                g = _slot_sum(r_refs[l], (gi,))
                delta, nm, nv = _adam_math(g, w_ref[l, gi], m_ref[l, gi], v_ref[l, gi])
                for o_ref, val in zip(outs, (g, delta, nm, nv)):
                    o_ref[l, gi] = val

    ins = tuple(recvs) + (w, m, v)
    out = jax.ShapeDtypeStruct(w.shape, f32)
    return pl.pallas_call(
        body, name="adamw_sg_w", grid=(1,),
        in_specs=[_whole(a) for a in ins], out_specs=tuple(_whole(out) for _ in range(4)), out_shape=(out,) * 4,
        compiler_params=_cparams(1),
    )(*ins)


def _rope_tables(positions):
    t = positions.shape[0]
    inv = 10000.0 ** (-jnp.arange(0, QK_ROPE, 2, dtype=f32) / QK_ROPE)
    ang = positions.astype(f32)[:, None] * inv
    cos, sin = jnp.cos(ang), jnp.sin(ang)
    tail = jnp.zeros((t, HEAD_PAD - KR_LO - QK_ROPE), f32)
    cos_t = jnp.concatenate([jnp.ones((t, KR_LO), f32), cos, cos, tail], axis=1)
    sin_t = jnp.concatenate([jnp.zeros((t, KR_LO), f32), -sin, sin, tail], axis=1)
    return cos_t, sin_t


def _bias_over_channels(sg_b_l):
    return jnp.broadcast_to(sg_b_l.T[:, :, None], (SG_CHUNK, SG_GROUPS, D_SG // SG_GROUPS)).reshape(SG_CHUNK, D_SG)


def _mixer_weights(gathered, rep, l):
    conv_out_w, sg_out_w, wuq_p, wk_p, wv_p, wo_p, conv_dw_w, ffn_dw_w = _asm_small(
        gathered["conv_out_w"], gathered["sg_out_w"], gathered["mla_w_uq"], gathered["mla_w_ukv"], gathered["mla_w_o"],
        gathered["conv_dw_w"], gathered["ffn_dw_w"])
    g_out = gathered["w_out"]
    w = dict(rep)
    w.update(
        l=l, w_in_p=_asm_w_in(gathered["w_in"]),
        conv_out_w=conv_out_w, sg_out_w=sg_out_w, wuq_p=wuq_p, wk_p=wk_p, wv_p=wv_p, wo_p=wo_p, conv_dw_w_p=conv_dw_w, ffn_dw_w_p=ffn_dw_w,
        w_out=g_out.reshape(g_out.shape[0] * g_out.shape[1], g_out.shape[2]),
        bexp=_bias_over_channels(rep["sg_b"][l]))
    return w


def _ffn_weights(gathered):
    stack = lambda g: g.reshape(g.shape[0] * g.shape[1], g.shape[2])
    return dict(w_up=stack(gathered["ffn_w_up"]), w_down=stack(gathered["ffn_w_down"]))


def _mixer_fwd(x, w, cos_t, sin_t):
    l = w["l"]
    h, a_in, b_in, qlat, kvlat, krope, gates = _mix_in_fwd(l, x, w["mix_pre_g"], w["w_in_p"])
    s, z1 = _conv_fwd(l, a_in, w["conv_dw_w_p"], w["conv_dw_b"], w["conv_ln_g"], w["conv_ln_b"])
    um = _sgu_fwd(l, b_in, w["sg_ln_g"], w["sg_ln_b"], w["sg_w"], w["bexp"])
    q, k, v, qn, kvn = _mla_proj_fwd(l, qlat, kvlat, krope, cos_t, sin_t, w["mla_q_norm_g"], w["wuq_p"], w["mla_kv_norm_g"],
                                     w["wk_p"], w["wv_p"])
    o, lse = _attn_fwd(q, k, v)
    x1, merged, om = _merge_out_fwd(l, x, s, um, o, gates, w["conv_out_w"], w["sg_out_w"], w["wo_p"], w["w_out"], w["mix_post_g"])
    saved = dict(x=x, h=h, a_in=a_in, b_in=b_in, qlat=qlat, kvlat=kvlat, gates=gates, s=s, z1=z1, um=um, q=q, k=k, v=v, qn=qn,
                 kvn=kvn, o=o, lse=lse, x1=x1, merged=merged, om=om)
    return x1, saved


def _ffn_layer_fwd(x1, w):
    x2, h2, z, zc, act, f = _ffn_fwd(w["l"], x1, w["ffn_pre_g"], w["w_up"], w["ffn_dw_w_p"], w["ffn_dw_b"], w["w_down"], w["ffn_post_g"])
    return x2, dict(h2=h2, z=z, zc=zc, act=act, f=f)


def _ffn_layer_bwd(dx2, sv, w, deps=()):
    l = w["l"]
    vec = {}
    dx1, df, dz, vec["ffn_post_g"], vec["ffn_pre_g"], vec["ffn_dw_b"], vec["d_fdw"] = _ffn_bwd(
        l, dx2, sv["x1"], sv["f"], sv["z"], sv["zc"], w["w_up"], w["ffn_dw_w_p"], w["w_down"], w["ffn_pre_g"], w["ffn_post_g"],
        deps=deps)
    d_down = _wgrad(sv["act"], df, "wgrad_ffn_down", bf16)
    d_up_t = _wgrad(dz, sv["h2"], "wgrad_ffn_up", bf16)
    unstack = lambda g: g.reshape(N_DEV, g.shape[0] // N_DEV, g.shape[1])
    send = dict(ffn_w_up=unstack(d_up_t), ffn_w_down=unstack(d_down))
    return dx1, send, vec


def _mixer_bwd(dx1, sv, w, cos_t, sin_t, shard_cols, vec, start, deps=()):
    l = w["l"]
    d_fdw = vec.pop("d_fdw")
    dom, dgates, dya, dyb, dyc, ds, dum, do, vec["mix_post_g"] = _merge_out_bwd(
        l, dx1, sv["om"], sv["s"], sv["um"], sv["o"], sv["gates"], w["conv_out_w"], w["sg_out_w"], w["wo_p"], w["w_out"], w["mix_post_g"],
        deps=deps)
    d_out = _wgrad(sv["merged"], dom, "wgrad_w_out", bf16)
    d_co = _wgrad(sv["s"], dya, "wgrad_conv_out")
    d_so = _wgrad(sv["um"], dyb, "wgrad_sg_out")
    d_wo = _wgrad(sv["o"], dyc, "wgrad_w_o")

    dq, dk, dv = _attn_bwd(sv["q"], sv["k"], sv["v"], sv["o"], sv["lse"], do)
    dqb, dkb, dvb, dqlat, dkvlat, dkrope, vec["mla_q_norm_g"], vec["mla_kv_norm_g"] = _mla_proj_bwd(
        l, dq, dk, dv, sv["qlat"], sv["kvlat"], cos_t, sin_t, w["mla_q_norm_g"], w["wuq_p"], w["mla_kv_norm_g"], w["wk_p"], w["wv_p"])
    d_uq = _wgrad(sv["qn"], dqb, "wgrad_w_uq")
    d_uk, d_uv = _wgrad_multi(sv["kvn"], [dkb, dvb], "wgrad_w_ukv")

    db_in, dsg_w, dsgb, vec["sg_ln_g"], vec["sg_ln_b"] = _sgu_bwd(l, sv["b_in"], dum, w["sg_ln_g"], w["sg_ln_b"], w["sg_w"], w["bexp"])
    da_in, d_cdw, vec["conv_dw_b"], vec["conv_ln_g"], vec["conv_ln_b"] = _conv_bwd(
        l, sv["a_in"], sv["z1"], ds, w["conv_dw_w_p"], w["conv_ln_g"], w["conv_ln_b"])

    send = {}
    (send["conv_out_w"], send["sg_out_w"], send["mla_w_uq"], send["mla_w_ukv"], send["mla_w_o"], send["conv_dw_w"],
     send["ffn_dw_w"]) = _dis_small(d_co, d_so, d_uq, d_uk, d_uv, d_wo, d_cdw, d_fdw)
    send["w_out"] = d_out.reshape(N_DEV, d_out.shape[0] // N_DEV, d_out.shape[1])
    token = start(MIX_SMALL, send)

    dsegs = (da_in, db_in, dqlat, dkvlat, dkrope, dgates)
    d_in_segs = list(_wgrad_multi(sv["h"], dsegs[:5], "wgrad_w_in_abqkr", deps=token)) + [_wgrad(sv["h"], dgates, "wgrad_w_in_g")]
    token = start(("w_in",), dict(w_in=_dis_w_in(d_in_segs, shard_cols["w_in"])))
    dx, vec["mix_pre_g"] = _mix_in_bwd(l, sv["x"], w["mix_pre_g"], dx1, dsegs, w["w_in_p"], deps=token)
    rep_pack = _pack_rep([vec[n] for n, _, _, _ in REP_VECS[:-1]], dsgb)
    return dx, rep_pack, dsg_w


def kernel(x, positions, mix_pre_g, mix_post_g, ffn_pre_g, ffn_post_g, w_in, conv_dw_w, conv_dw_b, conv_ln_g, conv_ln_b, conv_out_w, sg_ln_g, sg_ln_b, sg_w, sg_b, sg_out_w, mla_q_norm_g, mla_w_uq, mla_kv_norm_g, mla_w_ukv, mla_w_o, w_out, ffn_w_up, ffn_dw_w, ffn_dw_b, ffn_w_down, loss_target, m_mix_pre_g, m_mix_post_g, m_ffn_pre_g, m_ffn_post_g, m_w_in, m_conv_dw_w, m_conv_dw_b, m_conv_ln_g, m_conv_ln_b, m_conv_out_w, m_sg_ln_g, m_sg_ln_b, m_sg_w, m_sg_b, m_sg_out_w, m_mla_q_norm_g, m_mla_w_uq, m_mla_kv_norm_g, m_mla_w_ukv, m_mla_w_o, m_w_out, m_ffn_w_up, m_ffn_dw_w, m_ffn_dw_b, m_ffn_w_down, v_mix_pre_g, v_mix_post_g, v_ffn_pre_g, v_ffn_post_g, v_w_in, v_conv_dw_w, v_conv_dw_b, v_conv_ln_g, v_conv_ln_b, v_conv_out_w, v_sg_ln_g, v_sg_ln_b, v_sg_w, v_sg_b, v_sg_out_w, v_mla_q_norm_g, v_mla_w_uq, v_mla_kv_norm_g, v_mla_w_ukv, v_mla_w_o, v_w_out, v_ffn_w_up, v_ffn_dw_w, v_ffn_dw_b, v_ffn_w_down):
    args = (x, positions, mix_pre_g, mix_post_g, ffn_pre_g, ffn_post_g, w_in, conv_dw_w, conv_dw_b, conv_ln_g, conv_ln_b, conv_out_w, sg_ln_g, sg_ln_b, sg_w, sg_b, sg_out_w, mla_q_norm_g, mla_w_uq, mla_kv_norm_g, mla_w_ukv, mla_w_o, w_out, ffn_w_up, ffn_dw_w, ffn_dw_b, ffn_w_down, loss_target, m_mix_pre_g, m_mix_post_g, m_ffn_pre_g, m_ffn_post_g, m_w_in, m_conv_dw_w, m_conv_dw_b, m_conv_ln_g, m_conv_ln_b, m_conv_out_w, m_sg_ln_g, m_sg_ln_b, m_sg_w, m_sg_b, m_sg_out_w, m_mla_q_norm_g, m_mla_w_uq, m_mla_kv_norm_g, m_mla_w_ukv, m_mla_w_o, m_w_out, m_ffn_w_up, m_ffn_dw_w, m_ffn_dw_b, m_ffn_w_down, v_mix_pre_g, v_mix_post_g, v_ffn_pre_g, v_ffn_post_g, v_w_in, v_conv_dw_w, v_conv_dw_b, v_conv_ln_g, v_conv_ln_b, v_conv_out_w, v_sg_ln_g, v_sg_ln_b, v_sg_w, v_sg_b, v_sg_out_w, v_mla_q_norm_g, v_mla_w_uq, v_mla_kv_norm_g, v_mla_w_ukv, v_mla_w_o, v_w_out, v_ffn_w_up, v_ffn_dw_w, v_ffn_dw_b, v_ffn_w_down)
    n_in = len(IN_NAMES)
    a = dict(zip(IN_NAMES, args[:n_in]))
    target = args[n_in]
    n_w = len(WEIGHTS)
    m_in = dict(zip(WEIGHTS, args[n_in + 1:n_in + 1 + n_w]))
    v_in = dict(zip(WEIGHTS, args[n_in + 1 + n_w:n_in + 1 + 2 * n_w]))
    depth = a["mix_pre_g"].shape[0]
    rep = {n: a[n] for n in WEIGHTS if n not in SHARDED}
    shard_cols = {n: a[n].shape[2] for n in SHARDED}

    def view(arr, n):
        return jnp.swapaxes(arr, 1, 2) if n in TRANSPOSED else arr

    w_sh = {n: view(a[n], n) for n in SHARDED}
    m_sh = {n: view(m_in[n], n) for n in SHARDED}
    v_sh = {n: view(v_in[n], n) for n in SHARDED}

    def gather_groups(l):
        return (("mix", MIX_GROUP), ("ffn", FFN_BIG))

    starts, token = {}, ()
    for l in range(depth):
        for tag, group in gather_groups(l):
            wire = [w_sh[n][l] if n in WIRE_F32 else w_sh[n][l].astype(bf16) for n in group]
            starts[l, tag] = _gather_start(wire, "gather_%s_weights_%d" % (tag, l), deps=token)
            token = (starts[l, tag]["token"],)

    cos_t, sin_t = _rope_tables(a["positions"][0])
    xl = a["x"][0]
    ws, saved = [], []
    passing = {}

    def pass_on(l, tag, after):
        passing[l, tag] = _gather_forward(starts[l, tag], after=after)
        return (passing[l, tag]["token"],)

    relaid = tuple(t[n] for n in SHARDED for t in (m_sh, v_sh))
    after = pass_on(0, gather_groups(0)[0][0], token + relaid)
    for l in range(depth):
        groups = gather_groups(l)
        got = dict(zip(groups[0][1], _gather_wait(passing[l, groups[0][0]], after=after)))
        w = _mixer_weights(got, rep, l)
        x1, sv = _mixer_fwd(xl, w, cos_t, sin_t)
        after = (x1,)
        if len(groups) > 1:
            after = pass_on(l, groups[1][0], after)
        if l + 1 < depth:
            after = pass_on(l + 1, gather_groups(l + 1)[0][0], after)
        if len(groups) > 1:
            got = dict(zip(groups[1][1], _gather_wait(passing[l, groups[1][0]], after=after)))
        w.update(_ffn_weights(got))
        xl, sv_ffn = _ffn_layer_fwd(x1, w)
        sv.update(sv_ffn)
        ws.append(w)
        saved.append(sv)
        after = (xl,)
    dx, loss_part = _loss_fwd_bwd(xl, target[0])
    loss = lax.psum(loss_part, AXES)

    outs = {}
    rep_recvs, sgw_recvs = [None] * depth, [None] * depth

    def finish(l, parts, after):
        for names, handle in parts:
            got = _exchange_wait(handle, after=after)
            if names == MIX_SMALL:
                prevs = [o for n in names for o in outs[n]] if names[0] in outs else None
                res = _adamw_small(l, got, [w_sh[n] for n in names], [m_sh[n] for n in names], [v_sh[n] for n in names],
                                   prevs, "adamw_small")
                for i, n in enumerate(names):
                    outs[n] = res[4 * i:4 * i + 4]
            else:
                for n, recv in zip(names, got):
                    outs[n] = _adamw_shard(l, recv, w_sh[n], m_sh[n], v_sh[n], outs.get(n), "adamw_" + n)
            after = tuple(outs[n][0] for n in names) or after
        rep_recvs[l], sgw_recvs[l] = got[0], got[1]
        return after

    token, prev, after = (), None, ()
    for l in reversed(range(depth)):
        dx1, send_ffn, vec = _ffn_layer_bwd(dx, saved[l], ws[l], deps=token)
        parts = []

        def start(names, send, extra=(), l=l, parts=parts):
            tag = names[0] if names else "rep"
            handle = _exchange_start([(send[n], "scatter") for n in names] + list(extra), "exchange_%s_grads_%d" % (tag, l))
            parts.append((names, handle))
            return (handle["token"],)

        token = start(FFN_BIG, send_ffn)
        dx, rep_pack, dsg_w = _mixer_bwd(dx1, saved[l], ws[l], cos_t, sin_t, shard_cols, vec, start, deps=token)
        token = start((), {}, extra=[(rep_pack, "gather"), (dsg_w, "gather")])
        if prev is not None:
            after = finish(*prev, after=(dx,) + token)
        prev = (l, parts)
    finish(*prev, after=after)
    vec_names = [n for n, _, _, _ in REP_VECS]
    rep_outs = _adamw_rep(rep_recvs, [a[n] for n in vec_names], [m_in[n] for n in vec_names], [v_in[n] for n in vec_names])
    for i, n in enumerate(vec_names):
        outs[n] = rep_outs[4 * i:4 * i + 4]
    outs["sg_w"] = _adamw_sg_w(sgw_recvs, a["sg_w"], m_in["sg_w"], v_in["sg_w"])
    for n in TRANSPOSED:
        outs[n] = tuple(view(o, n) for o in outs[n])

    grad_w, delta_w, new_m, new_v = ([outs[n][j] for n in WEIGHTS] for j in range(4))
    return (loss, dx[None], *grad_w, *delta_w, *new_m, *new_v)
```

```python
import math

import jax
import jax.numpy as jnp
from jax import lax
from jax.experimental import pallas as pl
from jax.experimental.pallas import tpu as pltpu

f32 = jnp.float32
bf16 = jnp.bfloat16

N_DEV = 8
AXES = ("x", "y", "c")
EPS = 1e-6
D_CONV = 512
CONV_K = 31
CONV_HALO = 32
D_SG = 512
SG_GROUPS = 4
SG_CHUNK = 128
N_HEADS = 8
QK_NOPE = 64
QK_ROPE = 32
V_HEAD = 64
HEAD_PAD = 128
Q_LORA = 384
KV_LORA = 256
D_FF = 2816
FFN_K = 3
FFN_HALO = 8
ATT_SCALE = (QK_NOPE + QK_ROPE) ** -0.5
ATT_BLOCK = 256
ATT_HEADS = 2
NEG = float(jnp.finfo(jnp.float32).min)

ADAM_LR = 0.001
ADAM_B1 = 0.9
ADAM_B2 = 0.999
ADAM_EPS = 1e-08
ADAM_WD = 0.01
ADAM_STEP = 10

LANES = 128
VMEM_MB = 56

REF_CUTS = (0, 1024, 2048, 2432, 2688, 2720, 5792)
SEGS = ((0, 1024), (1024, 1024), (2048, 384), (2432, 256), (2688, 128), (2816, 3072))
D_IN = 5792
D_IN_PAD = 5888
KR_LO = 64
SEG_INNER = (0, 0, 0, 0, KR_LO, 0)

IN_NAMES = ['x', 'positions', 'mix_pre_g', 'mix_post_g', 'ffn_pre_g', 'ffn_post_g', 'w_in', 'conv_dw_w', 'conv_dw_b', 'conv_ln_g', 'conv_ln_b', 'conv_out_w', 'sg_ln_g', 'sg_ln_b', 'sg_w', 'sg_b', 'sg_out_w', 'mla_q_norm_g', 'mla_w_uq', 'mla_kv_norm_g', 'mla_w_ukv', 'mla_w_o', 'w_out', 'ffn_w_up', 'ffn_dw_w', 'ffn_dw_b', 'ffn_w_down']
WEIGHTS = IN_NAMES[2:]
SHARDED = ("w_in", "conv_dw_w", "conv_out_w", "sg_out_w", "mla_w_uq", "mla_w_ukv", "mla_w_o", "w_out", "ffn_w_up", "ffn_dw_w",
           "ffn_w_down")
FFN_BIG = ("ffn_w_up", "ffn_w_down")
MIX_GROUP = tuple(n for n in SHARDED if n not in FFN_BIG)
MIX_SMALL = tuple(n for n in MIX_GROUP if n != "w_in")
TRANSPOSED = ("ffn_w_up",)
WIRE_F32 = ("conv_dw_w", "ffn_dw_w")
REP_VECS = (("mix_pre_g", 0, 1024, 1), ("mix_post_g", 1, 1024, 1), ("ffn_pre_g", 2, 1024, 1), ("ffn_post_g", 3, 1024, 1),
            ("conv_dw_b", 4, 512, 1), ("conv_ln_g", 5, 512, 1), ("conv_ln_b", 6, 512, 1), ("sg_ln_g", 7, 512, 1),
            ("sg_ln_b", 8, 512, 1), ("mla_q_norm_g", 9, 384, 1), ("mla_kv_norm_g", 10, 256, 1), ("ffn_dw_b", 11, 5632, 1),
            ("sg_b", 12, 128, 4))
REP_ROWS = 16
REP_W = 5632


def _cparams(n_axes):
    return pltpu.CompilerParams(dimension_semantics=("arbitrary",) * n_axes, vmem_limit_bytes=VMEM_MB * 2 ** 20)


def _rows(tb, n):
    return pl.BlockSpec((tb, n), lambda i: (i, 0))


def _rows_rev(tb, n, nb):
    return pl.BlockSpec((tb, n), lambda i: (nb - 1 - i, 0))


def _const(shape):
    nd = len(shape)
    return pl.BlockSpec(shape, lambda *_: (0,) * nd, pipeline_mode=pl.Buffered(1))


def _acc(shape):
    nd = len(shape)
    return pl.BlockSpec(shape, lambda *_: (0,) * nd)


def _whole(arr):
    return pl.BlockSpec(arr.shape, lambda *_: (0,) * arr.ndim)


def _dot(a, b):
    return jnp.dot(a, b, preferred_element_type=f32)


def _dot_nt(a, b):
    return lax.dot_general(a, b, (((1,), (1,)), ((), ())), preferred_element_type=f32)


def _dot_tn(a, b):
    return lax.dot_general(a, b, (((0,), (0,)), ((), ())), preferred_element_type=f32)


def _mean(x):
    return jnp.mean(x, axis=-1, keepdims=True)


def _colsum(x):
    return jnp.sum(x, axis=0, keepdims=True)


def _rms_fwd(x, g):
    return x * lax.rsqrt(_mean(x * x) + EPS) * g


def _rms_bwd(x, g, dy):
    r = lax.rsqrt(_mean(x * x) + EPS)
    n = x * r
    dn = dy * g
    return r * (dn - n * _mean(dn * n)), _colsum(dy * n)


def _ln_stats(x):
    mu = _mean(x)
    d = x - mu
    rstd = lax.rsqrt(_mean(d * d) + EPS)
    return d * rstd, rstd


def _ln_bwd(xhat, rstd, g, dy):
    dxh = dy * g
    return rstd * (dxh - _mean(dxh) - xhat * _mean(dxh * xhat))


_GELU_C0 = math.sqrt(2.0 / math.pi)
_GELU_C1 = 0.044715


def _gelu(x):
    t = jnp.tanh(_GELU_C0 * (x + _GELU_C1 * (x * x * x)))
    return 0.5 * x * (1.0 + t)


def _gelu_and_grad(x):
    x2 = x * x
    t = jnp.tanh(_GELU_C0 * (x + _GELU_C1 * (x2 * x)))
    g = 0.5 * x * (1.0 + t)
    dg = 0.5 * (1.0 + t) + 0.5 * x * (1.0 - t * t) * (_GELU_C0 * (1.0 + 3.0 * _GELU_C1 * x2))
    return g, dg


def _swap_rope_halves(x):
    n = x.shape[1]
    half = QK_ROPE // 2
    lane = lax.broadcasted_iota(jnp.int32, x.shape, 1) % HEAD_PAD
    first = (lane >= KR_LO) & (lane < KR_LO + half)
    second = (lane >= KR_LO + half) & (lane < KR_LO + QK_ROPE)
    return jnp.where(first, pltpu.roll(x, n - half, 1), jnp.where(second, pltpu.roll(x, half, 1), 0.0))


def _tb(t):
    return min(256, t)


_HBM = pl.BlockSpec(memory_space=pltpu.HBM)
_SEM = pl.BlockSpec(memory_space=pltpu.SEMAPHORE)
_ANY = pl.BlockSpec(memory_space=pl.ANY)
_EFFECT = pltpu.SideEffectType.DATAFLOW_SIDE_EFFECTING


def _exchange_copies(modes, ins, lands, send_sems, recv_sems, loc_sems):
    x, y, c = lax.axis_index("x"), lax.axis_index("y"), lax.axis_index("c")
    me = 4 * x + 2 * y + c
    copies = []
    for a, mode in enumerate(modes):
        def src(dst_index, a=a, mode=mode):
            return ins[a].at[dst_index] if mode == "scatter" else ins[a]
        copies.append(pltpu.make_async_copy(src(me), lands[a].at[me], loc_sems.at[a]))
        for k in range(1, N_DEV):
            px = 1 - x if (k >> 2) & 1 else x
            py = 1 - y if (k >> 1) & 1 else y
            pc = 1 - c if k & 1 else c
            copies.append(pltpu.make_async_remote_copy(
                src_ref=src(4 * px + 2 * py + pc), dst_ref=lands[a].at[me],
                send_sem=send_sems.at[a * (N_DEV - 1) + k - 1], recv_sem=recv_sems.at[a * (N_DEV - 1) + k - 1],
                device_id=(px, py, pc), device_id_type=pl.DeviceIdType.MESH))
    return copies


def _exchange_start(ops, name, deps=()):
    n = len(ops)
    arrs = [arr for arr, _ in ops]
    modes = [mode for _, mode in ops]
    lands = [lax.empty((N_DEV,) + arr.shape if mode == "gather" else arr.shape, arr.dtype) for arr, mode in ops]

    def body(*refs):
        ins, land_refs = refs[:n], refs[n:2 * n]
        send_sems, recv_sems, loc_sems = refs[2 * n + len(deps):2 * n + len(deps) + 3]
        for cp in _exchange_copies(modes, ins, land_refs, send_sems, recv_sems, loc_sems):
            cp.start()
        refs[-1][...] = jnp.zeros((8, LANES), f32)

    n_rem = n * (N_DEV - 1)
    res = pl.pallas_call(
        body, name=name,
        out_shape=(pltpu.SemaphoreType.DMA((n_rem,)), pltpu.SemaphoreType.DMA((n_rem,)), pltpu.SemaphoreType.DMA((n,)),
                   *[pltpu.HBM(x.shape, x.dtype) for x in arrs + lands], jax.ShapeDtypeStruct((8, LANES), f32)),
        in_specs=[_HBM] * (2 * n) + [_ANY] * len(deps),
        out_specs=(_SEM, _SEM, _SEM, *[_HBM] * (2 * n), pl.BlockSpec(memory_space=pltpu.VMEM)),
        input_output_aliases={i: 3 + i for i in range(2 * n)},
        compiler_params=pltpu.CompilerParams(has_side_effects=_EFFECT),
    )(*[pltpu.with_memory_space_constraint(x, pltpu.HBM) for x in arrs + lands], *deps)
    return dict(modes=modes, sems=res[:3], thru=res[3:3 + 2 * n], token=res[-1], name=name)


def _exchange_wait(handle, after=()):
    modes, thru = handle["modes"], handle["thru"]
    n = len(modes)

    def body(*refs):
        ins, land_refs = refs[:n], refs[n:2 * n]
        send_sems, recv_sems, loc_sems = refs[2 * n:2 * n + 3]
        for cp in _exchange_copies(modes, ins, land_refs, send_sems, recv_sems, loc_sems):
            cp.wait()

    res = pl.pallas_call(
        body, name=handle["name"] + "_wait",
        out_shape=tuple(pltpu.HBM(x.shape, x.dtype) for x in thru),
        in_specs=[_HBM] * (2 * n) + [_SEM] * 3 + [_ANY] * len(after),
        out_specs=tuple([_HBM] * (2 * n)),
        input_output_aliases={i: i for i in range(2 * n)},
        compiler_params=pltpu.CompilerParams(has_side_effects=_EFFECT),
    )(*thru, *handle["sems"], *after)
    return res[n:]


_SAME_CORE_PEERS = ((1, 0), (0, 1), (1, 1))


def _gather_copies_one(ins, lands, send_sems, recv_sems, loc_sems):
    x, y, c = lax.axis_index("x"), lax.axis_index("y"), lax.axis_index("c")
    me = 4 * x + 2 * y + c
    targets = [(x, y, 1 - c)] + [(1 - x if fx else x, 1 - y if fy else y, c) for fx, fy in _SAME_CORE_PEERS]
    copies = []
    for a in range(len(ins)):
        copies.append(pltpu.make_async_copy(ins[a], lands[a].at[me], loc_sems.at[a]))
        for j, target in enumerate(targets):
            copies.append(pltpu.make_async_remote_copy(
                src_ref=ins[a], dst_ref=lands[a].at[me], send_sem=send_sems.at[4 * a + j], recv_sem=recv_sems.at[4 * a + j],
                device_id=target, device_id_type=pl.DeviceIdType.MESH))
    return copies


def _gather_copies_two(lands, send_sems, recv_sems):
    x, y, c = lax.axis_index("x"), lax.axis_index("y"), lax.axis_index("c")
    copies = []
    for a in range(len(lands)):
        for j, (fx, fy) in enumerate(_SAME_CORE_PEERS):
            slot = 4 * (1 - x if fx else x) + 2 * (1 - y if fy else y) + c
            copies.append(pltpu.make_async_remote_copy(
                src_ref=lands[a].at[slot], dst_ref=lands[a].at[slot], send_sem=send_sems.at[3 * a + j],
                recv_sem=recv_sems.at[3 * a + j], device_id=(x, y, 1 - c), device_id_type=pl.DeviceIdType.MESH))
    return copies


def _gather_start(arrs, name, deps=()):
    n = len(arrs)
    lands = [lax.empty((N_DEV,) + arr.shape, arr.dtype) for arr in arrs]

    def body(*refs):
        ins, land_refs = refs[:n], refs[n:2 * n]
        send_sems, recv_sems, loc_sems = refs[2 * n + len(deps):2 * n + len(deps) + 3]
        for cp in _gather_copies_one(ins, land_refs, send_sems, recv_sems, loc_sems):
            cp.start()
        refs[-1][...] = jnp.zeros((8, LANES), f32)

    res = pl.pallas_call(
        body, name=name,
        out_shape=(pltpu.SemaphoreType.DMA((4 * n,)), pltpu.SemaphoreType.DMA((4 * n,)), pltpu.SemaphoreType.DMA((n,)),
                   *[pltpu.HBM(x.shape, x.dtype) for x in arrs + lands], jax.ShapeDtypeStruct((8, LANES), f32)),
        in_specs=[_HBM] * (2 * n) + [_ANY] * len(deps),
        out_specs=(_SEM, _SEM, _SEM, *[_HBM] * (2 * n), pl.BlockSpec(memory_space=pltpu.VMEM)),
        input_output_aliases={i: 3 + i for i in range(2 * n)},
        compiler_params=pltpu.CompilerParams(has_side_effects=_EFFECT),
    )(*[pltpu.with_memory_space_constraint(x, pltpu.HBM) for x in arrs + lands], *deps)
    return dict(n=n, sems=res[:3], thru=res[3:3 + 2 * n], token=res[-1], name=name)


def _gather_forward(handle, after=()):
    n, thru = handle["n"], handle["thru"]

    def body(*refs):
        ins, land_refs = refs[:n], refs[n:2 * n]
        send_one, recv_one, loc_sems = refs[2 * n:2 * n + 3]
        send_two, recv_two = refs[2 * n + 3 + len(after):2 * n + 5 + len(after)]
        for cp in _gather_copies_one(ins, land_refs, send_one, recv_one, loc_sems):
            cp.wait()
        for cp in _gather_copies_two(land_refs, send_two, recv_two):
            cp.start()
        refs[-1][...] = jnp.zeros((8, LANES), f32)

    res = pl.pallas_call(
        body, name=handle["name"] + "_forward",
        out_shape=(pltpu.SemaphoreType.DMA((3 * n,)), pltpu.SemaphoreType.DMA((3 * n,)),
                   *[pltpu.HBM(x.shape, x.dtype) for x in thru], jax.ShapeDtypeStruct((8, LANES), f32)),
        in_specs=[_HBM] * (2 * n) + [_SEM] * 3 + [_ANY] * len(after),
        out_specs=(_SEM, _SEM, *[_HBM] * (2 * n), pl.BlockSpec(memory_space=pltpu.VMEM)),
        input_output_aliases={i: 2 + i for i in range(2 * n)},
        compiler_params=pltpu.CompilerParams(has_side_effects=_EFFECT),
    )(*thru, *handle["sems"], *after)
    return dict(n=n, sems=res[:2], lands=res[2 + n:2 + 2 * n], token=res[-1], name=handle["name"])


def _gather_wait(handle, after=()):
    n, lands = handle["n"], handle["lands"]

    def body(*refs):
        land_refs = refs[:n]
        send_two, recv_two = refs[n:n + 2]
        for cp in _gather_copies_two(land_refs, send_two, recv_two):
            cp.wait()

    return pl.pallas_call(
        body, name=handle["name"] + "_wait",
        out_shape=tuple(pltpu.HBM(x.shape, x.dtype) for x in lands),
        in_specs=[_HBM] * n + [_SEM] * 2 + [_ANY] * len(after),
        out_specs=tuple([_HBM] * n),
        input_output_aliases={i: i for i in range(n)},
        compiler_params=pltpu.CompilerParams(has_side_effects=_EFFECT),
    )(*lands, *handle["sems"], *after)


def _w_in_pieces(ns):
    out = []
    for e in range(N_DEV):
        lo, hi = ns * e, ns * (e + 1)
        for s in range(len(SEGS)):
            a, b = max(lo, REF_CUTS[s]), min(hi, REF_CUTS[s + 1])
            if a < b:
                inner = SEG_INNER[s] + a - REF_CUTS[s]
                out.append((e, a - lo, b - lo, s, inner, inner + b - a))
    return out


def _asm_w_in(g):
    _, d, ns = g.shape
    rb = 256
    pieces = _w_in_pieces(ns)

    def body(g_ref, o_ref):
        kr = SEGS[4][0]
        o_ref[:, kr:kr + KR_LO] = jnp.zeros((rb, KR_LO), g.dtype)
        o_ref[:, kr + KR_LO + QK_ROPE:kr + HEAD_PAD] = jnp.zeros((rb, HEAD_PAD - KR_LO - QK_ROPE), g.dtype)
        for e, s0, s1, seg, d0, d1 in pieces:
            off = SEGS[seg][0]
            o_ref[:, off + d0:off + d1] = g_ref[e, :, s0:s1]

    return pl.pallas_call(
        body, name="asm_w_in", grid=(d // rb,),
        in_specs=[pl.BlockSpec((N_DEV, rb, ns), lambda i: (0, i, 0))],
        out_specs=_rows(rb, D_IN_PAD), out_shape=jax.ShapeDtypeStruct((d, D_IN_PAD), g.dtype),
        compiler_params=_cparams(1),
    )(g)


def _dis_w_in(dsegs, ns):
    d = dsegs[0].shape[0]
    rb = 256
    pieces = _w_in_pieces(ns)

    def body(*refs):
        seg_refs, o_ref = refs[:len(SEGS)], refs[len(SEGS)]
        for e, s0, s1, seg, d0, d1 in pieces:
            o_ref[e, :, s0:s1] = seg_refs[seg][:, d0:d1].astype(bf16)

    return pl.pallas_call(
        body, name="dis_w_in", grid=(d // rb,),
        in_specs=[_rows(rb, wd) for _, wd in SEGS],
        out_specs=pl.BlockSpec((N_DEV, rb, ns), lambda i: (0, i, 0)),
        out_shape=jax.ShapeDtypeStruct((N_DEV, d, ns), bf16),
        compiler_params=_cparams(1),
    )(*dsegs)


def _asm_small(g_conv_out, g_sg_out, g_uq, g_ukv, g_wo, g_cdw, g_fdw):
    d = g_conv_out.shape[2] * N_DEV
    hw = N_HEADS * HEAD_PAD
    hq = QK_NOPE + QK_ROPE
    ff2 = g_fdw.shape[2] * N_DEV
    cw, fw = g_cdw.shape[2], g_fdw.shape[2]

    def body(co_ref, so_ref, uq_ref, ukv_ref, wo_ref, cdw_ref, fdw_ref, o_co, o_so, o_uq, o_k, o_v, o_wo, o_cdw, o_fdw):
        o_cdw[CONV_K:CONV_HALO, :] = jnp.zeros((CONV_HALO - CONV_K, D_CONV), f32)
        o_fdw[FFN_K:FFN_HALO, :] = jnp.zeros((FFN_HALO - FFN_K, ff2), f32)
        for e in range(N_DEV):
            cs = e * HEAD_PAD
            o_co[:, cs:cs + HEAD_PAD] = co_ref[e]
            o_so[:, cs:cs + HEAD_PAD] = so_ref[e]
            o_uq[:, cs:cs + hq] = uq_ref[e]
            o_uq[:, cs + hq:cs + HEAD_PAD] = jnp.zeros((Q_LORA, HEAD_PAD - hq), bf16)
            o_k[:, cs:cs + QK_NOPE] = ukv_ref[e, :, 0:QK_NOPE]
            o_k[:, cs + QK_NOPE:cs + HEAD_PAD] = jnp.zeros((KV_LORA, HEAD_PAD - QK_NOPE), bf16)
            o_v[:, cs:cs + V_HEAD] = ukv_ref[e, :, QK_NOPE:QK_NOPE + V_HEAD]
            o_v[:, cs + V_HEAD:cs + HEAD_PAD] = jnp.zeros((KV_LORA, HEAD_PAD - V_HEAD), bf16)
            for h in range(N_HEADS):
                o_wo[h * HEAD_PAD:h * HEAD_PAD + V_HEAD, cs:cs + HEAD_PAD] = wo_ref[e, h * V_HEAD:(h + 1) * V_HEAD, :]
                o_wo[h * HEAD_PAD + V_HEAD:(h + 1) * HEAD_PAD, cs:cs + HEAD_PAD] = jnp.zeros((HEAD_PAD - V_HEAD, HEAD_PAD), bf16)
            o_cdw[0:CONV_K, e * cw:(e + 1) * cw] = cdw_ref[e]
            o_fdw[0:FFN_K, e * fw:(e + 1) * fw] = fdw_ref[e]

    ins = (g_conv_out, g_sg_out, g_uq, g_ukv, g_wo, g_cdw, g_fdw)
    out_shape = (jax.ShapeDtypeStruct((D_CONV, d), bf16), jax.ShapeDtypeStruct((D_SG, d), bf16), jax.ShapeDtypeStruct((Q_LORA, hw), bf16),
                 jax.ShapeDtypeStruct((KV_LORA, hw), bf16), jax.ShapeDtypeStruct((KV_LORA, hw), bf16), jax.ShapeDtypeStruct((hw, d), bf16),
                 jax.ShapeDtypeStruct((CONV_HALO, D_CONV), f32), jax.ShapeDtypeStruct((FFN_HALO, ff2), f32))
    return pl.pallas_call(
        body, name="asm_small", grid=(1,),
        in_specs=[_whole(a) for a in ins], out_specs=tuple(_whole(o) for o in out_shape), out_shape=out_shape,
        compiler_params=_cparams(1),
    )(*ins)


def _dis_small(d_co, d_so, d_uq, d_k, d_v, d_wo, d_cdw, d_fdw):
    d = d_co.shape[1]
    hq = QK_NOPE + QK_ROPE
    cw, fw = D_CONV // N_DEV, d_fdw.shape[1] // N_DEV

    def body(co_ref, so_ref, uq_ref, k_ref, v_ref, wo_ref, cdw_ref, fdw_ref, o_co, o_so, o_uq, o_ukv, o_wo, o_cdw, o_fdw):
        for e in range(N_DEV):
            cs = e * HEAD_PAD
            o_co[e] = co_ref[:, cs:cs + HEAD_PAD].astype(bf16)
            o_so[e] = so_ref[:, cs:cs + HEAD_PAD].astype(bf16)
            o_uq[e] = uq_ref[:, cs:cs + hq].astype(bf16)
            o_ukv[e, :, 0:QK_NOPE] = k_ref[:, cs:cs + QK_NOPE].astype(bf16)
            o_ukv[e, :, QK_NOPE:QK_NOPE + V_HEAD] = v_ref[:, cs:cs + V_HEAD].astype(bf16)
            for h in range(N_HEADS):
                o_wo[e, h * V_HEAD:(h + 1) * V_HEAD, :] = wo_ref[h * HEAD_PAD:h * HEAD_PAD + V_HEAD, cs:cs + HEAD_PAD].astype(bf16)
            o_cdw[e] = cdw_ref[0:CONV_K, e * cw:(e + 1) * cw]
            o_fdw[e] = fdw_ref[0:FFN_K, e * fw:(e + 1) * fw]

    ins = (d_co, d_so, d_uq, d_k, d_v, d_wo, d_cdw, d_fdw)
    out_shape = (jax.ShapeDtypeStruct((N_DEV, D_CONV, d // N_DEV), bf16), jax.ShapeDtypeStruct((N_DEV, D_SG, d // N_DEV), bf16),
                 jax.ShapeDtypeStruct((N_DEV, Q_LORA, hq), bf16), jax.ShapeDtypeStruct((N_DEV, KV_LORA, QK_NOPE + V_HEAD), bf16),
                 jax.ShapeDtypeStruct((N_DEV, N_HEADS * V_HEAD, d // N_DEV), bf16), jax.ShapeDtypeStruct((N_DEV, CONV_K, cw), f32),
                 jax.ShapeDtypeStruct((N_DEV, FFN_K, fw), f32))
    return pl.pallas_call(
        body, name="dis_small", grid=(1,),
        in_specs=[_whole(a) for a in ins], out_specs=tuple(_whole(o) for o in out_shape), out_shape=out_shape,
        compiler_params=_cparams(1),
    )(*ins)


def _pack_rep(vec_grads, dsgb):
    def body(*refs):
        o_ref = refs[-1]
        o_ref[...] = jnp.zeros((REP_ROWS, REP_W), f32)
        for (_, row, width, nrows), ref in zip(REP_VECS, refs[:-1]):
            o_ref[row:row + nrows, 0:width] = ref[0:nrows, :]

    ins = tuple(vec_grads) + (dsgb,)
    return pl.pallas_call(
        body, name="pack_rep", grid=(1,),
        in_specs=[_whole(a) for a in ins], out_specs=pl.BlockSpec((REP_ROWS, REP_W), lambda i: (0, 0)),
        out_shape=jax.ShapeDtypeStruct((REP_ROWS, REP_W), f32), compiler_params=_cparams(1),
    )(*ins)


def _mix_in_fwd(l, x, g1, w_in_p):
    t, d = x.shape
    tb = _tb(t)

    def body(x_ref, g_ref, w_ref, h_ref, *outs):
        h = _rms_fwd(x_ref[...], g_ref[l:l + 1, :]).astype(bf16)
        h_ref[...] = h
        for (off, wd), o_ref in zip(SEGS, outs):
            o_ref[...] = _dot(h, w_ref[:, off:off + wd])

    return pl.pallas_call(
        body, name="mix_in_fwd", grid=(t // tb,),
        in_specs=[_rows(tb, d), _const(g1.shape), _const((d, D_IN_PAD))],
        out_specs=tuple([_rows(tb, d)] + [_rows(tb, wd) for _, wd in SEGS]),
        out_shape=tuple([jax.ShapeDtypeStruct((t, d), bf16)] + [jax.ShapeDtypeStruct((t, wd), f32) for _, wd in SEGS]),
        compiler_params=_cparams(1),
    )(x, g1, w_in_p)


def _conv_fwd(l, a_in, dw_w, dw_b, ln_g, ln_b):
    t = a_in.shape[0]
    tb = _tb(t)

    def body(a_ref, w_ref, b_ref, g_ref, be_ref, s_ref, z1_ref, win):
        @pl.when(pl.program_id(0) == 0)
        def _():
            win[0:CONV_HALO, :] = jnp.zeros((CONV_HALO, D_CONV), f32)

        a = a_ref[...]
        win[CONV_HALO:CONV_HALO + tb, :] = a[:, :D_CONV] * jax.nn.sigmoid(a[:, D_CONV:])
        for r0 in range(0, tb, LANES):
            for c0 in range(0, D_CONV, LANES):
                cs = slice(c0, c0 + LANES)
                acc = jnp.broadcast_to(b_ref[l:l + 1, cs], (LANES, LANES))
                for j in range(CONV_K):
                    acc = acc + w_ref[j:j + 1, cs] * win[pl.ds(r0 + CONV_HALO - CONV_K + 1 + j, LANES), cs]
                z1_ref[r0:r0 + LANES, cs] = acc
        zh, _ = _ln_stats(z1_ref[...])
        zl = zh * g_ref[l:l + 1, :] + be_ref[l:l + 1, :]
        s_ref[...] = (zl * jax.nn.sigmoid(zl)).astype(bf16)
        win[0:CONV_HALO, :] = win[tb:tb + CONV_HALO, :]

    return pl.pallas_call(
        body, name="conv_fwd", grid=(t // tb,),
        in_specs=[_rows(tb, 2 * D_CONV), _const((CONV_HALO, D_CONV)), _const(dw_b.shape), _const(ln_g.shape), _const(ln_b.shape)],
        out_specs=(_rows(tb, D_CONV), _rows(tb, D_CONV)),
        out_shape=(jax.ShapeDtypeStruct((t, D_CONV), bf16), jax.ShapeDtypeStruct((t, D_CONV), f32)),
        scratch_shapes=[pltpu.VMEM((tb + CONV_HALO, D_CONV), f32)],
        compiler_params=_cparams(1),
    )(a_in, dw_w, dw_b, ln_g, ln_b)


def _tril_mask():
    r = lax.broadcasted_iota(jnp.int32, (SG_CHUNK, SG_CHUNK), 0)
    c = lax.broadcasted_iota(jnp.int32, (SG_CHUNK, SG_CHUNK), 1)
    return r >= c


def _sgu_fwd(l, b_in, ln_g, ln_b, sg_w, bexp):
    t = b_in.shape[0]
    tb = _tb(t)
    gw = D_SG // SG_GROUPS

    def body(b_ref, g_ref, be_ref, w_ref, bexp_ref, um_ref):
        gl = _gelu(b_ref[...])
        u = gl[:, :D_SG]
        vh, _ = _ln_stats(gl[:, D_SG:])
        vn = (vh * g_ref[l:l + 1, :] + be_ref[l:l + 1, :]).astype(bf16)
        tri = _tril_mask()
        for g in range(SG_GROUPS):
            wg = jnp.where(tri, w_ref[l, g], 0.0).astype(bf16)
            cs = slice(g * gw, (g + 1) * gw)
            for r0 in range(0, tb, SG_CHUNK):
                rs = slice(r0, r0 + SG_CHUNK)
                mixed = _dot(wg, vn[rs, cs]) + bexp_ref[:, cs]
                um_ref[rs, cs] = (u[rs, cs] * mixed).astype(bf16)

    return pl.pallas_call(
        body, name="sgu_fwd", grid=(t // tb,),
        in_specs=[_rows(tb, 2 * D_SG), _const(ln_g.shape), _const(ln_b.shape), _const(sg_w.shape), _const((SG_CHUNK, D_SG))],
        out_specs=_rows(tb, D_SG),
        out_shape=jax.ShapeDtypeStruct((t, D_SG), bf16),
        compiler_params=_cparams(1),
    )(b_in, ln_g, ln_b, sg_w, bexp)


def _mla_proj_fwd(l, qlat, kvlat, krope, cos_t, sin_t, gq, wuq_p, gkv, wk_p, wv_p):
    t = qlat.shape[0]
    tb = _tb(t)
    hw = N_HEADS * HEAD_PAD

    def body(ql_ref, kvl_ref, kr_ref, c_ref, s_ref, gq_ref, wq_ref, gkv_ref, wk_ref, wv_ref, q_ref, k_ref, v_ref, qn_ref, kvn_ref):
        cos_b, sin_b = c_ref[...], s_ref[...]
        qn = _rms_fwd(ql_ref[...], gq_ref[l:l + 1, :]).astype(bf16)
        qn_ref[...] = qn
        q = _dot(qn, wq_ref[...])
        sw = _swap_rope_halves(q)
        for h in range(N_HEADS):
            hs = slice(h * HEAD_PAD, (h + 1) * HEAD_PAD)
            q_ref[:, hs] = (q[:, hs] * cos_b + sw[:, hs] * sin_b).astype(bf16)
        kvn = _rms_fwd(kvl_ref[...], gkv_ref[l:l + 1, :]).astype(bf16)
        kvn_ref[...] = kvn
        kr = kr_ref[...]
        kpe = kr * cos_b + _swap_rope_halves(kr) * sin_b
        k = _dot(kvn, wk_ref[...])
        for h in range(N_HEADS):
            hs = slice(h * HEAD_PAD, (h + 1) * HEAD_PAD)
            k_ref[:, hs] = (k[:, hs] + kpe).astype(bf16)
        v_ref[...] = _dot(kvn, wv_ref[...]).astype(bf16)

    return pl.pallas_call(
        body, name="mla_proj_fwd", grid=(t // tb,),
        in_specs=[_rows(tb, Q_LORA), _rows(tb, KV_LORA), _rows(tb, HEAD_PAD), _rows(tb, HEAD_PAD), _rows(tb, HEAD_PAD),
                  _const(gq.shape), _const((Q_LORA, hw)), _const(gkv.shape), _const((KV_LORA, hw)), _const((KV_LORA, hw))],
        out_specs=(_rows(tb, hw), _rows(tb, hw), _rows(tb, hw), _rows(tb, Q_LORA), _rows(tb, KV_LORA)),
        out_shape=(jax.ShapeDtypeStruct((t, hw), bf16), jax.ShapeDtypeStruct((t, hw), bf16), jax.ShapeDtypeStruct((t, hw), bf16),
                   jax.ShapeDtypeStruct((t, Q_LORA), bf16), jax.ShapeDtypeStruct((t, KV_LORA), bf16)),
        compiler_params=_cparams(1),
    )(qlat, kvlat, krope, cos_t, sin_t, gq, wuq_p, gkv, wk_p, wv_p)


def _diag_mask(tq):
    return lax.broadcasted_iota(jnp.int32, (tq, tq), 0) >= lax.broadcasted_iota(jnp.int32, (tq, tq), 1)


def _attn_fwd(q, k, v):
    t = q.shape[0]
    tq = min(ATT_BLOCK, t)

    def body(q_ref, k_ref, v_ref, o_ref, lse_ref):
        qi = pl.program_id(1)
        for i in range(t // tq):
            @pl.when(qi == i)
            def _(i=i):
                lo, hi = i * tq, (i + 1) * tq
                for hh in range(ATT_HEADS):
                    hs = slice(hh * HEAD_PAD, (hh + 1) * HEAD_PAD)
                    qv = q_ref[:, hs]
                    s_d = jnp.where(_diag_mask(tq), _dot_nt(qv, k_ref[lo:hi, hs]) * ATT_SCALE, NEG)
                    m = jnp.max(s_d, axis=-1, keepdims=True)
                    if i > 0:
                        s_o = _dot_nt(qv, k_ref[0:lo, hs]) * ATT_SCALE
                        m = jnp.maximum(m, jnp.max(s_o, axis=-1, keepdims=True))
                    p_d = jnp.exp(s_d - m)
                    lsum = jnp.sum(p_d, axis=-1, keepdims=True)
                    acc = _dot(p_d.astype(bf16), v_ref[lo:hi, hs])
                    if i > 0:
                        p_o = jnp.exp(s_o - m)
                        lsum = lsum + jnp.sum(p_o, axis=-1, keepdims=True)
                        acc = acc + _dot(p_o.astype(bf16), v_ref[0:lo, hs])
                    o_ref[:, hs] = (acc / lsum).astype(bf16)
                    lse_ref[hh] = m + jnp.log(lsum)

    hw = ATT_HEADS * HEAD_PAD
    return pl.pallas_call(
        body, name="attn_fwd", grid=(N_HEADS // ATT_HEADS, t // tq),
        in_specs=[pl.BlockSpec((tq, hw), lambda h, i: (i, h)), pl.BlockSpec((t, hw), lambda h, i: (0, h)),
                  pl.BlockSpec((t, hw), lambda h, i: (0, h))],
        out_specs=(pl.BlockSpec((tq, hw), lambda h, i: (i, h)), pl.BlockSpec((ATT_HEADS, tq, 1), lambda h, i: (h, i, 0))),
        out_shape=(jax.ShapeDtypeStruct((t, N_HEADS * HEAD_PAD), bf16), jax.ShapeDtypeStruct((N_HEADS, t, 1), f32)),
        compiler_params=_cparams(2),
    )(q, k, v)


def _merge_out_fwd(l, x, s, um, o, gates, conv_out_w, sg_out_w, wo_p, w_out, g2):
    t, d = x.shape
    tb = _tb(t)

    def body(x_ref, s_ref, um_ref, o_ref, gt_ref, wa_ref, wb_ref, wc_ref, wout_ref, g_ref, x1_ref, mg_ref, om_ref):
        merged = (jax.nn.sigmoid(gt_ref[:, 0:d]) * _dot(s_ref[...], wa_ref[...])
                  + jax.nn.sigmoid(gt_ref[:, d:2 * d]) * _dot(um_ref[...], wb_ref[...])
                  + jax.nn.sigmoid(gt_ref[:, 2 * d:3 * d]) * _dot(o_ref[...], wc_ref[...]))
        mb = merged.astype(bf16)
        mg_ref[...] = mb
        om = _dot(mb, wout_ref[...])
        om_ref[...] = om
        x1_ref[...] = x_ref[...] + _rms_fwd(om, g_ref[l:l + 1, :])

    hw = N_HEADS * HEAD_PAD
    return pl.pallas_call(
        body, name="merge_out_fwd", grid=(t // tb,),
        in_specs=[_rows(tb, d), _rows(tb, D_CONV), _rows(tb, D_SG), _rows(tb, hw), _rows(tb, 3 * d),
                  _const((D_CONV, d)), _const((D_SG, d)), _const((hw, d)), _const((d, d)), _const(g2.shape)],
        out_specs=(_rows(tb, d), _rows(tb, d), _rows(tb, d)),
        out_shape=(jax.ShapeDtypeStruct((t, d), f32), jax.ShapeDtypeStruct((t, d), bf16), jax.ShapeDtypeStruct((t, d), f32)),
        compiler_params=_cparams(1),
    )(x, s, um, o, gates, conv_out_w, sg_out_w, wo_p, w_out, g2)


FF_CHUNK = 1408


def _ffn_conv_cols(zbuf, w_ref, b_ref, l, nrows, c0, c1):
    acc = b_ref[l:l + 1, c0:c1] + w_ref[0:1, c0:c1] * zbuf[pl.ds(FFN_HALO - 2, nrows), c0:c1]
    acc = acc + w_ref[1:2, c0:c1] * zbuf[pl.ds(FFN_HALO - 1, nrows), c0:c1]
    return acc + w_ref[2:3, c0:c1] * zbuf[pl.ds(FFN_HALO, nrows), c0:c1]


def _ffn_fwd(l, x1, g3, w_up, dw_w, dw_b, w_down, g4):
    t, d = x1.shape
    tb = _tb(t)
    ff2 = 2 * D_FF

    def body(x_ref, g3_ref, wup_ref, dww_ref, dwb_ref, wdn_ref, g4_ref, x2_ref, h2_ref, z_ref, zc_ref, act_ref, f_ref, zbuf):
        @pl.when(pl.program_id(0) == 0)
        def _():
            zbuf[0:FFN_HALO, :] = jnp.zeros((FFN_HALO, ff2), f32)

        xv = x_ref[...]
        h2 = _rms_fwd(xv, g3_ref[l:l + 1, :]).astype(bf16)
        h2_ref[...] = h2
        for c0 in range(0, ff2, FF_CHUNK):
            zv = _dot_nt(h2, wup_ref[c0:c0 + FF_CHUNK, :])
            z_ref[:, c0:c0 + FF_CHUNK] = zv.astype(bf16)
            zbuf[FFN_HALO:FFN_HALO + tb, c0:c0 + FF_CHUNK] = zv
        facc = jnp.zeros((tb, d), f32)
        for c0 in range(0, D_FF, FF_CHUNK):
            gg = _ffn_conv_cols(zbuf, dww_ref, dwb_ref, l, tb, c0, c0 + FF_CHUNK)
            vv = _ffn_conv_cols(zbuf, dww_ref, dwb_ref, l, tb, D_FF + c0, D_FF + c0 + FF_CHUNK)
            zc_ref[:, c0:c0 + FF_CHUNK] = gg.astype(bf16)
            zc_ref[:, D_FF + c0:D_FF + c0 + FF_CHUNK] = vv.astype(bf16)
            a = (_gelu(gg) * vv).astype(bf16)
            act_ref[:, c0:c0 + FF_CHUNK] = a
            facc = facc + _dot(a, wdn_ref[c0:c0 + FF_CHUNK, :])
        f_ref[...] = facc
        x2_ref[...] = xv + _rms_fwd(facc, g4_ref[l:l + 1, :])
        zbuf[0:FFN_HALO, :] = zbuf[tb:tb + FFN_HALO, :]

    return pl.pallas_call(
        body, name="ffn_fwd", grid=(t // tb,),
        in_specs=[_rows(tb, d), _const(g3.shape), _const((ff2, d)), _const((FFN_HALO, ff2)), _const(dw_b.shape), _const((D_FF, d)),
                  _const(g4.shape)],
        out_specs=(_rows(tb, d), _rows(tb, d), _rows(tb, ff2), _rows(tb, ff2), _rows(tb, D_FF), _rows(tb, d)),
        out_shape=(jax.ShapeDtypeStruct((t, d), f32), jax.ShapeDtypeStruct((t, d), bf16), jax.ShapeDtypeStruct((t, ff2), bf16),
                   jax.ShapeDtypeStruct((t, ff2), bf16), jax.ShapeDtypeStruct((t, D_FF), bf16), jax.ShapeDtypeStruct((t, d), f32)),
        scratch_shapes=[pltpu.VMEM((tb + FFN_HALO, ff2), f32)],
        compiler_params=_cparams(1),
    )(x1, g3, w_up, dw_w, dw_b, w_down, g4)


def _loss_fwd_bwd(y, target):
    t, d = y.shape
    tb = _tb(t)

    def body(y_ref, t_ref, dy_ref, loss_ref):
        @pl.when(pl.program_id(0) == 0)
        def _():
            loss_ref[...] = jnp.zeros((1, LANES), f32)

        e = y_ref[...] - t_ref[...]
        dy_ref[...] = e * (1.0 / d)
        loss_ref[...] += 0.5 * jnp.sum(_mean(e * e))

    dy, loss = pl.pallas_call(
        body, name="loss", grid=(t // tb,),
        in_specs=[_rows(tb, d), _rows(tb, d)],
        out_specs=(_rows(tb, d), _acc((1, LANES))),
        out_shape=(jax.ShapeDtypeStruct((t, d), f32), jax.ShapeDtypeStruct((1, LANES), f32)),
        compiler_params=_cparams(1),
    )(y, target)
    return dy, loss[0, 0]


def _ffn_bwd(l, dx2, x1, f, z, zc, w_up, dw_w, w_down, g3, g4, deps=()):
    t, d = x1.shape
    tb = min(128, t)
    nb = t // tb
    ff2 = 2 * D_FF
    hrows = 16
    per_h = tb // hrows

    def body(*refs):
        (dx2_ref, x1_ref, f_ref, z_ref, zp_ref, zc_ref, wup_ref, dww_ref, wdn_ref, g3_ref, g4_ref,
         dx1_ref, df_ref, dz_ref, dg4_ref, dg3_ref, ddwb_ref, ddww_ref, zbuf, dzc) = refs[len(deps):]
        i = pl.program_id(0)
        b = nb - 1 - i

        @pl.when(i == 0)
        def _():
            dg4_ref[...] = jnp.zeros_like(dg4_ref)
            dg3_ref[...] = jnp.zeros_like(dg3_ref)
            ddwb_ref[...] = jnp.zeros_like(ddwb_ref)
            ddww_ref[...] = jnp.zeros_like(ddww_ref)
            dzc[tb:tb + FFN_HALO, :] = jnp.zeros((FFN_HALO, ff2), f32)

        dout = dx2_ref[...]
        df, dg4 = _rms_bwd(f_ref[...], g4_ref[l:l + 1, :], dout)
        dg4_ref[...] += dg4
        dfb = df.astype(bf16)
        df_ref[...] = dfb
        zbuf[0:FFN_HALO, :] = jnp.where(b > 0, zp_ref[hrows - FFN_HALO:hrows, :].astype(f32), 0.0)
        zbuf[FFN_HALO:FFN_HALO + tb, :] = z_ref[...].astype(f32)
        for c0 in range(0, D_FF, FF_CHUNK):
            dact = _dot_nt(dfb, wdn_ref[c0:c0 + FF_CHUNK, :])
            gel, dgel = _gelu_and_grad(zc_ref[:, c0:c0 + FF_CHUNK].astype(f32))
            dzc[0:tb, c0:c0 + FF_CHUNK] = dact * zc_ref[:, D_FF + c0:D_FF + c0 + FF_CHUNK].astype(f32) * dgel
            dzc[0:tb, D_FF + c0:D_FF + c0 + FF_CHUNK] = dact * gel
        dh2 = jnp.zeros((tb, d), f32)
        for c0 in range(0, ff2, FF_CHUNK):
            cs = slice(c0, c0 + FF_CHUNK)
            d0 = dzc[0:tb, cs]
            ddwb_ref[:, cs] += _colsum(d0)
            for j in range(FFN_K):
                ddww_ref[j:j + 1, cs] += _colsum(d0 * zbuf[pl.ds(FFN_HALO - 2 + j, tb), cs])
            dzv = dww_ref[2:3, cs] * d0 + dww_ref[1:2, cs] * dzc[pl.ds(1, tb), cs] + dww_ref[0:1, cs] * dzc[pl.ds(2, tb), cs]
            dzb = dzv.astype(bf16)
            dz_ref[:, cs] = dzb
            dh2 = dh2 + _dot(dzb, wup_ref[cs, :])
        dzc[tb:tb + FFN_HALO, :] = dzc[0:FFN_HALO, :]
        dxn, dg3 = _rms_bwd(x1_ref[...], g3_ref[l:l + 1, :], dh2)
        dg3_ref[...] += dg3
        dx1_ref[...] = dout + dxn

    return pl.pallas_call(
        body, name="ffn_bwd", grid=(nb,),
        in_specs=[_ANY] * len(deps) + [_rows_rev(tb, d, nb), _rows_rev(tb, d, nb), _rows_rev(tb, d, nb), _rows_rev(tb, ff2, nb),
                  pl.BlockSpec((hrows, ff2), lambda i: (jnp.maximum((nb - 1 - i) * per_h - 1, 0), 0)), _rows_rev(tb, ff2, nb),
                  _const((ff2, d)), _const((FFN_HALO, ff2)), _const((D_FF, d)), _const(g3.shape), _const(g4.shape)],
        out_specs=(_rows_rev(tb, d, nb), _rows_rev(tb, d, nb), _rows_rev(tb, ff2, nb), _acc((1, d)), _acc((1, d)), _acc((1, ff2)),
                   _acc((FFN_HALO, ff2))),
        out_shape=(jax.ShapeDtypeStruct((t, d), f32), jax.ShapeDtypeStruct((t, d), bf16), jax.ShapeDtypeStruct((t, ff2), bf16),
                   jax.ShapeDtypeStruct((1, d), f32), jax.ShapeDtypeStruct((1, d), f32), jax.ShapeDtypeStruct((1, ff2), f32),
                   jax.ShapeDtypeStruct((FFN_HALO, ff2), f32)),
        scratch_shapes=[pltpu.VMEM((tb + FFN_HALO, ff2), f32), pltpu.VMEM((tb + FFN_HALO, ff2), f32)],
        compiler_params=_cparams(1),
    )(*deps, dx2, x1, f, z, z, zc, w_up, dw_w, w_down, g3, g4)


def _merge_out_bwd(l, dx1, om, s, um, o, gates, conv_out_w, sg_out_w, wo_p, w_out, g2, deps=()):
    t, d = dx1.shape
    tb = _tb(t)
    hw = N_HEADS * HEAD_PAD

    def body(*refs):
        (dx_ref, om_ref, s_ref, um_ref, o_ref, gt_ref, wa_ref, wb_ref, wc_ref, wout_ref, g_ref,
         dom_ref, dgt_ref, dya_ref, dyb_ref, dyc_ref, ds_ref, dum_ref, do_ref, dg2_ref) = refs[len(deps):]

        @pl.when(pl.program_id(0) == 0)
        def _():
            dg2_ref[...] = jnp.zeros_like(dg2_ref)

        dom, dg2 = _rms_bwd(om_ref[...], g_ref[l:l + 1, :], dx_ref[...])
        dg2_ref[...] += dg2
        domb = dom.astype(bf16)
        dom_ref[...] = domb
        dmerged = _dot_nt(domb, wout_ref[...])
        branches = ((s_ref, wa_ref, dya_ref, ds_ref), (um_ref, wb_ref, dyb_ref, dum_ref), (o_ref, wc_ref, dyc_ref, do_ref))
        for br, (in_ref, w_ref, dy_ref, din_ref) in enumerate(branches):
            yv = _dot(in_ref[...], w_ref[...])
            sg = jax.nn.sigmoid(gt_ref[:, br * d:(br + 1) * d])
            dyb = (dmerged * sg).astype(bf16)
            dy_ref[...] = dyb
            dgt_ref[:, br * d:(br + 1) * d] = (dmerged * yv * sg * (1.0 - sg)).astype(bf16)
            din_ref[...] = _dot_nt(dyb, w_ref[...]).astype(din_ref.dtype)

    return pl.pallas_call(
        body, name="merge_out_bwd", grid=(t // tb,),
        in_specs=[_ANY] * len(deps) + [_rows(tb, d), _rows(tb, d), _rows(tb, D_CONV), _rows(tb, D_SG), _rows(tb, hw), _rows(tb, 3 * d),
                  _const((D_CONV, d)), _const((D_SG, d)), _const((hw, d)), _const((d, d)), _const(g2.shape)],
        out_specs=(_rows(tb, d), _rows(tb, 3 * d), _rows(tb, d), _rows(tb, d), _rows(tb, d), _rows(tb, D_CONV), _rows(tb, D_SG),
                   _rows(tb, hw), _acc((1, d))),
        out_shape=(jax.ShapeDtypeStruct((t, d), bf16), jax.ShapeDtypeStruct((t, 3 * d), bf16), jax.ShapeDtypeStruct((t, d), bf16),
                   jax.ShapeDtypeStruct((t, d), bf16), jax.ShapeDtypeStruct((t, d), bf16), jax.ShapeDtypeStruct((t, D_CONV), f32),
                   jax.ShapeDtypeStruct((t, D_SG), f32), jax.ShapeDtypeStruct((t, hw), bf16), jax.ShapeDtypeStruct((1, d), f32)),
        compiler_params=_cparams(1),
    )(*deps, dx1, om, s, um, o, gates, conv_out_w, sg_out_w, wo_p, w_out, g2)


def _attn_bwd(q, k, v, o, lse, do):
    t = q.shape[0]
    tq = min(ATT_BLOCK, t)
    hw = N_HEADS * HEAD_PAD

    def body(q_ref, k_ref, v_ref, o_ref, lse_ref, do_ref, dq_ref, dk_ref, dv_ref):
        qi = pl.program_id(1)

        @pl.when(qi == 0)
        def _():
            dk_ref[...] = jnp.zeros_like(dk_ref)
            dv_ref[...] = jnp.zeros_like(dv_ref)

        def keys(lo, hi, hs, qv, dov, lse, delta, diagonal):
            kj, vj = k_ref[lo:hi, hs], v_ref[lo:hi, hs]
            p = jnp.exp(_dot_nt(qv, kj) * ATT_SCALE - lse)
            if diagonal:
                p = jnp.where(_diag_mask(tq), p, 0.0)
            ds = (p * (_dot_nt(dov, vj) - delta) * ATT_SCALE).astype(bf16)
            dk_ref[lo:hi, hs] += _dot_tn(ds, qv)
            dv_ref[lo:hi, hs] += _dot_tn(p.astype(bf16), dov)
            return _dot(ds, kj)

        for i in range(t // tq):
            @pl.when(qi == i)
            def _(i=i):
                for hh in range(ATT_HEADS):
                    hs = slice(hh * HEAD_PAD, (hh + 1) * HEAD_PAD)
                    qv, dov, lse = q_ref[:, hs], do_ref[:, hs], lse_ref[hh]
                    delta = jnp.sum(dov.astype(f32) * o_ref[:, hs].astype(f32), axis=-1, keepdims=True)
                    dq = keys(i * tq, (i + 1) * tq, hs, qv, dov, lse, delta, True)
                    if i > 0:
                        dq = dq + keys(0, i * tq, hs, qv, dov, lse, delta, False)
                    dq_ref[:, hs] = dq

    blk_q = pl.BlockSpec((tq, ATT_HEADS * HEAD_PAD), lambda h, i: (i, h))
    blk_kv = pl.BlockSpec((t, ATT_HEADS * HEAD_PAD), lambda h, i: (0, h))
    return pl.pallas_call(
        body, name="attn_bwd", grid=(N_HEADS // ATT_HEADS, t // tq),
        in_specs=[blk_q, blk_kv, blk_kv, blk_q, pl.BlockSpec((ATT_HEADS, tq, 1), lambda h, i: (h, i, 0)), blk_q],
        out_specs=(blk_q, blk_kv, blk_kv),
        out_shape=(jax.ShapeDtypeStruct((t, hw), f32), jax.ShapeDtypeStruct((t, hw), f32), jax.ShapeDtypeStruct((t, hw), f32)),
        compiler_params=_cparams(2),
    )(q, k, v, o, lse, do)


def _mla_proj_bwd(l, dq, dk, dv, qlat, kvlat, cos_t, sin_t, gq, wuq_p, gkv, wk_p, wv_p):
    t = qlat.shape[0]
    tb = _tb(t)
    hw = N_HEADS * HEAD_PAD

    def body(dq_ref, dk_ref, dv_ref, ql_ref, kvl_ref, c_ref, s_ref, gq_ref, wq_ref, gkv_ref, wk_ref, wv_ref,
             dqb_ref, dkb_ref, dvb_ref, dql_ref, dkvl_ref, dkr_ref, dgq_ref, dgkv_ref):
        @pl.when(pl.program_id(0) == 0)
        def _():
            dgq_ref[...] = jnp.zeros_like(dgq_ref)
            dgkv_ref[...] = jnp.zeros_like(dgkv_ref)

        cos_b, sin_b = c_ref[...], s_ref[...]
        for h in range(N_HEADS):
            hs = slice(h * HEAD_PAD, (h + 1) * HEAD_PAD)
            dqh = dq_ref[:, hs]
            dqb_ref[:, hs] = (dqh * cos_b + _swap_rope_halves(dqh * sin_b)).astype(bf16)
        dqn = _dot_nt(dqb_ref[...], wq_ref[...])
        dql, dgq = _rms_bwd(ql_ref[...], gq_ref[l:l + 1, :], dqn)
        dgq_ref[...] += dgq
        dql_ref[...] = dql.astype(bf16)
        dkv_full = dk_ref[...]
        dkb = dkv_full.astype(bf16)
        dkb_ref[...] = dkb
        dkpe = dkv_full[:, 0:HEAD_PAD]
        for h in range(1, N_HEADS):
            dkpe = dkpe + dkv_full[:, h * HEAD_PAD:(h + 1) * HEAD_PAD]
        dkr_ref[...] = (dkpe * cos_b + _swap_rope_halves(dkpe * sin_b)).astype(bf16)
        dvb = dv_ref[...].astype(bf16)
        dvb_ref[...] = dvb
        dkvn = _dot_nt(dkb, wk_ref[...]) + _dot_nt(dvb, wv_ref[...])
        dkvl, dgkv = _rms_bwd(kvl_ref[...], gkv_ref[l:l + 1, :], dkvn)
        dgkv_ref[...] += dgkv
        dkvl_ref[...] = dkvl.astype(bf16)

    return pl.pallas_call(
        body, name="mla_proj_bwd", grid=(t // tb,),
        in_specs=[_rows(tb, hw), _rows(tb, hw), _rows(tb, hw), _rows(tb, Q_LORA), _rows(tb, KV_LORA), _rows(tb, HEAD_PAD),
                  _rows(tb, HEAD_PAD), _const(gq.shape), _const((Q_LORA, hw)), _const(gkv.shape), _const((KV_LORA, hw)),
                  _const((KV_LORA, hw))],
        out_specs=(_rows(tb, hw), _rows(tb, hw), _rows(tb, hw), _rows(tb, Q_LORA), _rows(tb, KV_LORA), _rows(tb, HEAD_PAD),
                   _acc((1, Q_LORA)), _acc((1, KV_LORA))),
        out_shape=(jax.ShapeDtypeStruct((t, hw), bf16), jax.ShapeDtypeStruct((t, hw), bf16), jax.ShapeDtypeStruct((t, hw), bf16),
                   jax.ShapeDtypeStruct((t, Q_LORA), bf16), jax.ShapeDtypeStruct((t, KV_LORA), bf16),
                   jax.ShapeDtypeStruct((t, HEAD_PAD), bf16), jax.ShapeDtypeStruct((1, Q_LORA), f32),
                   jax.ShapeDtypeStruct((1, KV_LORA), f32)),
        compiler_params=_cparams(1),
    )(dq, dk, dv, qlat, kvlat, cos_t, sin_t, gq, wuq_p, gkv, wk_p, wv_p)


def _sgu_bwd(l, b_in, dum, ln_g, ln_b, sg_w, bexp):
    t = b_in.shape[0]
    tb = _tb(t)
    gw = D_SG // SG_GROUPS

    def body(b_ref, dum_ref, g_ref, be_ref, w_ref, bexp_ref, db_ref, dw_ref, dsgb_ref, dlg_ref, dlb_ref, dvn_s):
        @pl.when(pl.program_id(0) == 0)
        def _():
            dw_ref[...] = jnp.zeros_like(dw_ref)
            dsgb_ref[...] = jnp.zeros_like(dsgb_ref)
            dlg_ref[...] = jnp.zeros_like(dlg_ref)
            dlb_ref[...] = jnp.zeros_like(dlb_ref)

        gl, dgl = _gelu_and_grad(b_ref[...])
        u = gl[:, :D_SG]
        vh, rstd = _ln_stats(gl[:, D_SG:])
        ln_gain = g_ref[l:l + 1, :]
        vn = (vh * ln_gain + be_ref[l:l + 1, :]).astype(bf16)
        dumv = dum_ref[...]
        tri = _tril_mask()
        ones = jnp.ones((FFN_HALO, gw), f32)
        for g in range(SG_GROUPS):
            wg = jnp.where(tri, w_ref[l, g], 0.0).astype(bf16)
            cs = slice(g * gw, (g + 1) * gw)
            for r0 in range(0, tb, SG_CHUNK):
                rs = slice(r0, r0 + SG_CHUNK)
                vblk = vn[rs, cs]
                mixed = _dot(wg, vblk) + bexp_ref[:, cs]
                db_ref[rs, cs] = (dumv[rs, cs] * mixed * dgl[rs, cs]).astype(bf16)
                dmix = dumv[rs, cs] * u[rs, cs]
                dmb = dmix.astype(bf16)
                dw_ref[g] += jnp.where(tri, _dot_nt(dmb, vblk), 0.0)
                rowsum = lax.dot_general(ones, dmix, (((1,), (1,)), ((), ())), preferred_element_type=f32,
                                         precision=lax.Precision.HIGHEST)
                dsgb_ref[g:g + 1, :] += rowsum[0:1, :]
                dvn_s[rs, cs] = _dot_tn(wg, dmb)
        dvn = dvn_s[...]
        dlg_ref[...] += _colsum(dvn * vh)
        dlb_ref[...] += _colsum(dvn)
        db_ref[:, D_SG:] = (_ln_bwd(vh, rstd, ln_gain, dvn) * dgl[:, D_SG:]).astype(bf16)

    return pl.pallas_call(
        body, name="sgu_bwd", grid=(t // tb,),
        in_specs=[_rows(tb, 2 * D_SG), _rows(tb, D_SG), _const(ln_g.shape), _const(ln_b.shape), _const(sg_w.shape),
                  _const((SG_CHUNK, D_SG))],
        out_specs=(_rows(tb, 2 * D_SG), _acc((SG_GROUPS, SG_CHUNK, SG_CHUNK)), _acc((FFN_HALO, SG_CHUNK)), _acc((1, D_SG)),
                   _acc((1, D_SG))),
        out_shape=(jax.ShapeDtypeStruct((t, 2 * D_SG), bf16), jax.ShapeDtypeStruct((SG_GROUPS, SG_CHUNK, SG_CHUNK), f32),
                   jax.ShapeDtypeStruct((FFN_HALO, SG_CHUNK), f32), jax.ShapeDtypeStruct((1, D_SG), f32),
                   jax.ShapeDtypeStruct((1, D_SG), f32)),
        scratch_shapes=[pltpu.VMEM((tb, D_SG), f32)],
        compiler_params=_cparams(1),
    )(b_in, dum, ln_g, ln_b, sg_w, bexp)


def _conv_bwd(l, a_in, z1, ds, dw_w, ln_g, ln_b):
    t = a_in.shape[0]
    tb = _tb(t)
    nb = t // tb
    per_halo = tb // CONV_HALO

    def body(a_ref, ap_ref, z1_ref, ds_ref, w_ref, g_ref, be_ref, da_ref, ddww_ref, ddwb_ref, dlg_ref, dlb_ref, win, dzb):
        i = pl.program_id(0)
        b = nb - 1 - i

        @pl.when(i == 0)
        def _():
            ddww_ref[...] = jnp.zeros_like(ddww_ref)
            ddwb_ref[...] = jnp.zeros_like(ddwb_ref)
            dlg_ref[...] = jnp.zeros_like(dlg_ref)
            dlb_ref[...] = jnp.zeros_like(dlb_ref)
            dzb[tb:tb + CONV_HALO, :] = jnp.zeros((CONV_HALO, D_CONV), f32)

        a = a_ref[...]
        val = a[:, :D_CONV]
        sg = jax.nn.sigmoid(a[:, D_CONV:])
        ap = ap_ref[...]
        win[0:CONV_HALO, :] = jnp.where(b > 0, ap[:, :D_CONV] * jax.nn.sigmoid(ap[:, D_CONV:]), 0.0)
        win[CONV_HALO:CONV_HALO + tb, :] = val * sg
        zh, rstd = _ln_stats(z1_ref[...])
        ln_gain = g_ref[l:l + 1, :]
        zl = zh * ln_gain + be_ref[l:l + 1, :]
        sgl = jax.nn.sigmoid(zl)
        dzl = ds_ref[...] * (sgl * (1.0 + zl * (1.0 - sgl)))
        dlg_ref[...] += _colsum(dzl * zh)
        dlb_ref[...] += _colsum(dzl)
        dz1 = _ln_bwd(zh, rstd, ln_gain, dzl)
        dzb[0:tb, :] = dz1
        ddwb_ref[...] += _colsum(dz1)
        dgate_f = val * sg * (1.0 - sg)
        for c0 in range(0, D_CONV, LANES):
            cs = slice(c0, c0 + LANES)
            for r0 in range(0, tb, LANES):
                d1 = dzb[r0:r0 + LANES, cs]
                acc = jnp.zeros((LANES, LANES), f32)
                for j in range(CONV_K):
                    ddww_ref[j:j + 1, cs] += _colsum(d1 * win[pl.ds(r0 + CONV_HALO - CONV_K + 1 + j, LANES), cs])
                    acc = acc + w_ref[j:j + 1, cs] * dzb[pl.ds(r0 + CONV_K - 1 - j, LANES), cs]
                da_ref[r0:r0 + LANES, cs] = (acc * sg[r0:r0 + LANES, cs]).astype(bf16)
                da_ref[r0:r0 + LANES, c0 + D_CONV:c0 + D_CONV + LANES] = (acc * dgate_f[r0:r0 + LANES, cs]).astype(bf16)
        dzb[tb:tb + CONV_HALO, :] = dzb[0:CONV_HALO, :]

    return pl.pallas_call(
        body, name="conv_bwd", grid=(nb,),
        in_specs=[_rows_rev(tb, 2 * D_CONV, nb),
                  pl.BlockSpec((CONV_HALO, 2 * D_CONV), lambda i: (jnp.maximum((nb - 1 - i) * per_halo - 1, 0), 0)),
                  _rows_rev(tb, D_CONV, nb), _rows_rev(tb, D_CONV, nb), _const((CONV_HALO, D_CONV)), _const(ln_g.shape),
                  _const(ln_b.shape)],
        out_specs=(_rows_rev(tb, 2 * D_CONV, nb), _acc((CONV_HALO, D_CONV)), _acc((1, D_CONV)), _acc((1, D_CONV)), _acc((1, D_CONV))),
        out_shape=(jax.ShapeDtypeStruct((t, 2 * D_CONV), bf16), jax.ShapeDtypeStruct((CONV_HALO, D_CONV), f32),
                   jax.ShapeDtypeStruct((1, D_CONV), f32), jax.ShapeDtypeStruct((1, D_CONV), f32), jax.ShapeDtypeStruct((1, D_CONV), f32)),
        scratch_shapes=[pltpu.VMEM((tb + CONV_HALO, D_CONV), f32), pltpu.VMEM((tb + CONV_HALO, D_CONV), f32)],
        compiler_params=_cparams(1),
    )(a_in, a_in, z1, ds, dw_w, ln_g, ln_b)


def _mix_in_bwd(l, x, g1, dxres, dsegs, w_in_p, deps=()):
    t, d = x.shape
    tb = _tb(t)

    def body(*refs):
        x_ref, g_ref, dr_ref = refs[len(deps):len(deps) + 3]
        rest = refs[len(deps) + 3:]
        dseg_refs, w_ref, dx_ref, dg_ref = rest[:len(SEGS)], rest[len(SEGS)], rest[len(SEGS) + 1], rest[len(SEGS) + 2]

        @pl.when(pl.program_id(0) == 0)
        def _():
            dg_ref[...] = jnp.zeros_like(dg_ref)

        dh = jnp.zeros((tb, d), f32)
        for (off, wd), ds_ref in zip(SEGS, dseg_refs):
            dh = dh + _dot_nt(ds_ref[...], w_ref[:, off:off + wd])
        dxn, dg = _rms_bwd(x_ref[...], g_ref[l:l + 1, :], dh)
        dg_ref[...] += dg
        dx_ref[...] = dr_ref[...] + dxn

    return pl.pallas_call(
        body, name="mix_in_bwd", grid=(t // tb,),
        in_specs=([_ANY] * len(deps) + [_rows(tb, d), _const(g1.shape), _rows(tb, d)] + [_rows(tb, wd) for _, wd in SEGS]
                  + [_const((d, D_IN_PAD))]),
        out_specs=(_rows(tb, d), _acc((1, d))),
        out_shape=(jax.ShapeDtypeStruct((t, d), f32), jax.ShapeDtypeStruct((1, d), f32)),
        compiler_params=_cparams(1),
    )(*deps, x, g1, dxres, *dsegs, w_in_p)


def _pick_block(n, cap=512):
    for b in (cap, 384, 256, 128):
        if b <= cap and n % b == 0:
            return b
    return n


def _wgrad(a, b, name, out_dtype=f32):
    t, kdim = a.shape
    n = b.shape[1]
    bk, bn = _pick_block(kdim), _pick_block(n, cap=1024)

    def body(a_ref, b_ref, o_ref):
        o_ref[...] = _dot_tn(a_ref[...], b_ref[...]).astype(out_dtype)

    return pl.pallas_call(
        body, name=name, grid=(kdim // bk, n // bn),
        in_specs=[pl.BlockSpec((t, bk), lambda i, j: (0, i)), pl.BlockSpec((t, bn), lambda i, j: (0, j))],
        out_specs=pl.BlockSpec((bk, bn), lambda i, j: (i, j)),
        out_shape=jax.ShapeDtypeStruct((kdim, n), out_dtype),
        compiler_params=_cparams(2),
    )(a, b)


def _wgrad_multi(a, bs, name, deps=()):
    t, kdim = a.shape
    bk = _pick_block(kdim)
    nb = len(bs)

    def body(*refs):
        a_ref, refs = refs[len(deps)], refs[len(deps) + 1:]
        at = a_ref[...].T
        for b_ref, o_ref in zip(refs[:nb], refs[nb:]):
            o_ref[...] = _dot(at, b_ref[...])

    return pl.pallas_call(
        body, name=name, grid=(kdim // bk,),
        in_specs=[_ANY] * len(deps) + [pl.BlockSpec((t, bk), lambda i: (0, i))] + [_const(b.shape) for b in bs],
        out_specs=tuple(pl.BlockSpec((bk, b.shape[1]), lambda i: (i, 0)) for b in bs),
        out_shape=tuple(jax.ShapeDtypeStruct((kdim, b.shape[1]), f32) for b in bs),
        compiler_params=_cparams(1),
    )(*deps, a, *bs)


_BC1 = 1.0 - ADAM_B1 ** ADAM_STEP
_BC2 = 1.0 - ADAM_B2 ** ADAM_STEP


def _adam_math(g, w, m, v):
    nm = ADAM_B1 * m + (1.0 - ADAM_B1) * g
    nv = ADAM_B2 * v + (1.0 - ADAM_B2) * (g * g)
    delta = -ADAM_LR * ((nm / _BC1) / (jnp.sqrt(nv / _BC2) + ADAM_EPS) + ADAM_WD * w)
    return delta, nm, nv


def _slot_sum(r_ref, index=()):
    g = r_ref[(0,) + index].astype(f32)
    for s in range(1, N_DEV):
        g = g + r_ref[(s,) + index].astype(f32)
    return g


def _adamw_shard(l, recv, w, m, v, prev, name, deps=()):
    _, k, ns = recv.shape
    rb = next((c for c in (256, 192, 176, 128) if k % c == 0), k)
    blk = pl.BlockSpec((None, rb, ns), lambda i: (l, i, 0))

    def body(r_ref, w_ref, m_ref, v_ref, *rest):
        g_ref, d_ref, nm_ref, nv_ref = rest[-4:]
        g = _slot_sum(r_ref)
        g_ref[...] = g
        d_ref[...], nm_ref[...], nv_ref[...] = _adam_math(g, w_ref[...], m_ref[...], v_ref[...])

    out = jax.ShapeDtypeStruct(w.shape, f32)
    n_prev = 0 if prev is None else 4
    return pl.pallas_call(
        body, name=name, grid=(k // rb,),
        in_specs=[pl.BlockSpec((N_DEV, rb, ns), lambda i: (0, i, 0)), blk, blk, blk] + [_ANY] * (n_prev + len(deps)),
        out_specs=(blk, blk, blk, blk), out_shape=(out, out, out, out),
        input_output_aliases={4 + j: j for j in range(n_prev)},
        compiler_params=_cparams(1),
    )(recv, w, m, v, *(prev or ()), *deps)


def _adamw_rep(recvs, ws, ms, vs):
    depth = len(recvs)
    nt = len(REP_VECS)

    def body(*refs):
        r_refs = refs[:depth]
        w_refs, m_refs, v_refs = (refs[depth + i * nt:depth + (i + 1) * nt] for i in range(3))
        outs = refs[depth + 3 * nt:]
        for ti, (_, row, width, nrows) in enumerate(REP_VECS):
            for l in range(depth):
                g = r_refs[l][0, row:row + nrows, 0:width]
                for s in range(1, N_DEV):
                    g = g + r_refs[l][s, row:row + nrows, 0:width]
                pick = (lambda ref: ref[l]) if nrows > 1 else (lambda ref: ref[l:l + 1, :])
                delta, nm, nv = _adam_math(g, pick(w_refs[ti]), pick(m_refs[ti]), pick(v_refs[ti]))
                for o_ref, val in zip(outs[4 * ti:4 * ti + 4], (g, delta, nm, nv)):
                    if nrows > 1:
                        o_ref[l] = val
                    else:
                        o_ref[l:l + 1, :] = val

    ins = tuple(recvs) + tuple(ws) + tuple(ms) + tuple(vs)
    out_shape = tuple(jax.ShapeDtypeStruct(w.shape, f32) for w in ws for _ in range(4))
    return pl.pallas_call(
        body, name="adamw_rep", grid=(1,),
        in_specs=[_whole(a) for a in ins], out_specs=tuple(_whole(o) for o in out_shape), out_shape=out_shape,
        compiler_params=_cparams(1),
    )(*ins)


def _adamw_sg_w(recvs, w, m, v):
    depth = len(recvs)

    def body(*refs):
        r_refs = refs[:depth]
        w_ref, m_ref, v_ref = refs[depth:depth + 3]
        outs = refs[depth + 3:]
        for l in range(depth):
            for gi in range(SG_GROUPS):
                g = _slot_sum(r_refs[l], (gi,))
                delta, nm, nv = _adam_math(g, w_ref[l, gi], m_ref[l, gi], v_ref[l, gi])
                for o_ref, val in zip(outs, (g, delta, nm, nv)):
                    o_ref[l, gi] = val

    ins = tuple(recvs) + (w, m, v)
    out = jax.ShapeDtypeStruct(w.shape, f32)
    return pl.pallas_call(
        body, name="adamw_sg_w", grid=(1,),
        in_specs=[_whole(a) for a in ins], out_specs=tuple(_whole(out) for _ in range(4)), out_shape=(out,) * 4,
        compiler_params=_cparams(1),
    )(*ins)


def _rope_tables(positions):
    t = positions.shape[0]
    inv = 10000.0 ** (-jnp.arange(0, QK_ROPE, 2, dtype=f32) / QK_ROPE)
    ang = positions.astype(f32)[:, None] * inv
    cos, sin = jnp.cos(ang), jnp.sin(ang)
    tail = jnp.zeros((t, HEAD_PAD - KR_LO - QK_ROPE), f32)
    cos_t = jnp.concatenate([jnp.ones((t, KR_LO), f32), cos, cos, tail], axis=1)
    sin_t = jnp.concatenate([jnp.zeros((t, KR_LO), f32), -sin, sin, tail], axis=1)
    return cos_t, sin_t


def _bias_over_channels(sg_b_l):
    return jnp.broadcast_to(sg_b_l.T[:, :, None], (SG_CHUNK, SG_GROUPS, D_SG // SG_GROUPS)).reshape(SG_CHUNK, D_SG)


def _mixer_weights(gathered, rep, l):
    conv_out_w, sg_out_w, wuq_p, wk_p, wv_p, wo_p, conv_dw_w, ffn_dw_w = _asm_small(
        gathered["conv_out_w"], gathered["sg_out_w"], gathered["mla_w_uq"], gathered["mla_w_ukv"], gathered["mla_w_o"],
        gathered["conv_dw_w"], gathered["ffn_dw_w"])
    g_out = gathered["w_out"]
    w = dict(rep)
    w.update(
        l=l, w_in_p=_asm_w_in(gathered["w_in"]),
        conv_out_w=conv_out_w, sg_out_w=sg_out_w, wuq_p=wuq_p, wk_p=wk_p, wv_p=wv_p, wo_p=wo_p, conv_dw_w_p=conv_dw_w, ffn_dw_w_p=ffn_dw_w,
        w_out=g_out.reshape(g_out.shape[0] * g_out.shape[1], g_out.shape[2]),
        bexp=_bias_over_channels(rep["sg_b"][l]))
    return w


def _ffn_weights(gathered):
    stack = lambda g: g.reshape(g.shape[0] * g.shape[1], g.shape[2])
    return dict(w_up=stack(gathered["ffn_w_up"]), w_down=stack(gathered["ffn_w_down"]))


def _mixer_fwd(x, w, cos_t, sin_t):
    l = w["l"]
    h, a_in, b_in, qlat, kvlat, krope, gates = _mix_in_fwd(l, x, w["mix_pre_g"], w["w_in_p"])
    s, z1 = _conv_fwd(l, a_in, w["conv_dw_w_p"], w["conv_dw_b"], w["conv_ln_g"], w["conv_ln_b"])
    um = _sgu_fwd(l, b_in, w["sg_ln_g"], w["sg_ln_b"], w["sg_w"], w["bexp"])
    q, k, v, qn, kvn = _mla_proj_fwd(l, qlat, kvlat, krope, cos_t, sin_t, w["mla_q_norm_g"], w["wuq_p"], w["mla_kv_norm_g"],
                                     w["wk_p"], w["wv_p"])
    o, lse = _attn_fwd(q, k, v)
    x1, merged, om = _merge_out_fwd(l, x, s, um, o, gates, w["conv_out_w"], w["sg_out_w"], w["wo_p"], w["w_out"], w["mix_post_g"])
    saved = dict(x=x, h=h, a_in=a_in, b_in=b_in, qlat=qlat, kvlat=kvlat, gates=gates, s=s, z1=z1, um=um, q=q, k=k, v=v, qn=qn,
                 kvn=kvn, o=o, lse=lse, x1=x1, merged=merged, om=om)
    return x1, saved


def _ffn_layer_fwd(x1, w):
    x2, h2, z, zc, act, f = _ffn_fwd(w["l"], x1, w["ffn_pre_g"], w["w_up"], w["ffn_dw_w_p"], w["ffn_dw_b"], w["w_down"], w["ffn_post_g"])
    return x2, dict(h2=h2, z=z, zc=zc, act=act, f=f)


def _ffn_layer_bwd(dx2, sv, w, deps=()):
    l = w["l"]
    vec = {}
    dx1, df, dz, vec["ffn_post_g"], vec["ffn_pre_g"], vec["ffn_dw_b"], vec["d_fdw"] = _ffn_bwd(
        l, dx2, sv["x1"], sv["f"], sv["z"], sv["zc"], w["w_up"], w["ffn_dw_w_p"], w["w_down"], w["ffn_pre_g"], w["ffn_post_g"],
        deps=deps)
    d_down = _wgrad(sv["act"], df, "wgrad_ffn_down", bf16)
    d_up_t = _wgrad(dz, sv["h2"], "wgrad_ffn_up", bf16)
    unstack = lambda g: g.reshape(N_DEV, g.shape[0] // N_DEV, g.shape[1])
    send = dict(ffn_w_up=unstack(d_up_t), ffn_w_down=unstack(d_down))
    return dx1, send, vec


def _mixer_bwd(dx1, sv, w, cos_t, sin_t, shard_cols, vec, start, deps=()):
    l = w["l"]
    d_fdw = vec.pop("d_fdw")
    dom, dgates, dya, dyb, dyc, ds, dum, do, vec["mix_post_g"] = _merge_out_bwd(
        l, dx1, sv["om"], sv["s"], sv["um"], sv["o"], sv["gates"], w["conv_out_w"], w["sg_out_w"], w["wo_p"], w["w_out"], w["mix_post_g"],
        deps=deps)
    d_out = _wgrad(sv["merged"], dom, "wgrad_w_out", bf16)
    d_co = _wgrad(sv["s"], dya, "wgrad_conv_out")
    d_so = _wgrad(sv["um"], dyb, "wgrad_sg_out")
    d_wo = _wgrad(sv["o"], dyc, "wgrad_w_o")

    dq, dk, dv = _attn_bwd(sv["q"], sv["k"], sv["v"], sv["o"], sv["lse"], do)
    dqb, dkb, dvb, dqlat, dkvlat, dkrope, vec["mla_q_norm_g"], vec["mla_kv_norm_g"] = _mla_proj_bwd(
        l, dq, dk, dv, sv["qlat"], sv["kvlat"], cos_t, sin_t, w["mla_q_norm_g"], w["wuq_p"], w["mla_kv_norm_g"], w["wk_p"], w["wv_p"])
    d_uq = _wgrad(sv["qn"], dqb, "wgrad_w_uq")
    d_uk, d_uv = _wgrad_multi(sv["kvn"], [dkb, dvb], "wgrad_w_ukv")

    db_in, dsg_w, dsgb, vec["sg_ln_g"], vec["sg_ln_b"] = _sgu_bwd(l, sv["b_in"], dum, w["sg_ln_g"], w["sg_ln_b"], w["sg_w"], w["bexp"])
    da_in, d_cdw, vec["conv_dw_b"], vec["conv_ln_g"], vec["conv_ln_b"] = _conv_bwd(
        l, sv["a_in"], sv["z1"], ds, w["conv_dw_w_p"], w["conv_ln_g"], w["conv_ln_b"])

    send = {}
    (send["conv_out_w"], send["sg_out_w"], send["mla_w_uq"], send["mla_w_ukv"], send["mla_w_o"], send["conv_dw_w"],
     send["ffn_dw_w"]) = _dis_small(d_co, d_so, d_uq, d_uk, d_uv, d_wo, d_cdw, d_fdw)
    send["w_out"] = d_out.reshape(N_DEV, d_out.shape[0] // N_DEV, d_out.shape[1])
    token = start(MIX_SMALL, send)

    dsegs = (da_in, db_in, dqlat, dkvlat, dkrope, dgates)
    d_in_segs = list(_wgrad_multi(sv["h"], dsegs[:5], "wgrad_w_in_abqkr", deps=token)) + [_wgrad(sv["h"], dgates, "wgrad_w_in_g")]
    token = start(("w_in",), dict(w_in=_dis_w_in(d_in_segs, shard_cols["w_in"])))
    dx, vec["mix_pre_g"] = _mix_in_bwd(l, sv["x"], w["mix_pre_g"], dx1, dsegs, w["w_in_p"], deps=token)
    rep_pack = _pack_rep([vec[n] for n, _, _, _ in REP_VECS[:-1]], dsgb)
    return dx, rep_pack, dsg_w


def kernel(x, positions, mix_pre_g, mix_post_g, ffn_pre_g, ffn_post_g, w_in, conv_dw_w, conv_dw_b, conv_ln_g, conv_ln_b, conv_out_w, sg_ln_g, sg_ln_b, sg_w, sg_b, sg_out_w, mla_q_norm_g, mla_w_uq, mla_kv_norm_g, mla_w_ukv, mla_w_o, w_out, ffn_w_up, ffn_dw_w, ffn_dw_b, ffn_w_down, loss_target, m_mix_pre_g, m_mix_post_g, m_ffn_pre_g, m_ffn_post_g, m_w_in, m_conv_dw_w, m_conv_dw_b, m_conv_ln_g, m_conv_ln_b, m_conv_out_w, m_sg_ln_g, m_sg_ln_b, m_sg_w, m_sg_b, m_sg_out_w, m_mla_q_norm_g, m_mla_w_uq, m_mla_kv_norm_g, m_mla_w_ukv, m_mla_w_o, m_w_out, m_ffn_w_up, m_ffn_dw_w, m_ffn_dw_b, m_ffn_w_down, v_mix_pre_g, v_mix_post_g, v_ffn_pre_g, v_ffn_post_g, v_w_in, v_conv_dw_w, v_conv_dw_b, v_conv_ln_g, v_conv_ln_b, v_conv_out_w, v_sg_ln_g, v_sg_ln_b, v_sg_w, v_sg_b, v_sg_out_w, v_mla_q_norm_g, v_mla_w_uq, v_mla_kv_norm_g, v_mla_w_ukv, v_mla_w_o, v_w_out, v_ffn_w_up, v_ffn_dw_w, v_ffn_dw_b, v_ffn_w_down):
    args = (x, positions, mix_pre_g, mix_post_g, ffn_pre_g, ffn_post_g, w_in, conv_dw_w, conv_dw_b, conv_ln_g, conv_ln_b, conv_out_w, sg_ln_g, sg_ln_b, sg_w, sg_b, sg_out_w, mla_q_norm_g, mla_w_uq, mla_kv_norm_g, mla_w_ukv, mla_w_o, w_out, ffn_w_up, ffn_dw_w, ffn_dw_b, ffn_w_down, loss_target, m_mix_pre_g, m_mix_post_g, m_ffn_pre_g, m_ffn_post_g, m_w_in, m_conv_dw_w, m_conv_dw_b, m_conv_ln_g, m_conv_ln_b, m_conv_out_w, m_sg_ln_g, m_sg_ln_b, m_sg_w, m_sg_b, m_sg_out_w, m_mla_q_norm_g, m_mla_w_uq, m_mla_kv_norm_g, m_mla_w_ukv, m_mla_w_o, m_w_out, m_ffn_w_up, m_ffn_dw_w, m_ffn_dw_b, m_ffn_w_down, v_mix_pre_g, v_mix_post_g, v_ffn_pre_g, v_ffn_post_g, v_w_in, v_conv_dw_w, v_conv_dw_b, v_conv_ln_g, v_conv_ln_b, v_conv_out_w, v_sg_ln_g, v_sg_ln_b, v_sg_w, v_sg_b, v_sg_out_w, v_mla_q_norm_g, v_mla_w_uq, v_mla_kv_norm_g, v_mla_w_ukv, v_mla_w_o, v_w_out, v_ffn_w_up, v_ffn_dw_w, v_ffn_dw_b, v_ffn_w_down)
    n_in = len(IN_NAMES)
    a = dict(zip(IN_NAMES, args[:n_in]))
    target = args[n_in]
    n_w = len(WEIGHTS)
    m_in = dict(zip(WEIGHTS, args[n_in + 1:n_in + 1 + n_w]))
    v_in = dict(zip(WEIGHTS, args[n_in + 1 + n_w:n_in + 1 + 2 * n_w]))
    depth = a["mix_pre_g"].shape[0]
    rep = {n: a[n] for n in WEIGHTS if n not in SHARDED}
    shard_cols = {n: a[n].shape[2] for n in SHARDED}

    def view(arr, n):
        return jnp.swapaxes(arr, 1, 2) if n in TRANSPOSED else arr

    w_sh = {n: view(a[n], n) for n in SHARDED}
    m_sh = {n: view(m_in[n], n) for n in SHARDED}
    v_sh = {n: view(v_in[n], n) for n in SHARDED}

    def gather_groups(l):
        return (("mix", MIX_GROUP), ("ffn", FFN_BIG))

    starts, token = {}, ()
    for l in range(depth):
        for tag, group in gather_groups(l):
            wire = [w_sh[n][l] if n in WIRE_F32 else w_sh[n][l].astype(bf16) for n in group]
            starts[l, tag] = _gather_start(wire, "gather_%s_weights_%d" % (tag, l), deps=token)
            token = (starts[l, tag]["token"],)

    cos_t, sin_t = _rope_tables(a["positions"][0])
    xl = a["x"][0]
    ws, saved = [], []
    passing = {}

    def pass_on(l, tag, after):
        passing[l, tag] = _gather_forward(starts[l, tag], after=after)
        return (passing[l, tag]["token"],)

    relaid = tuple(t[n] for n in SHARDED for t in (m_sh, v_sh))
    after = pass_on(0, gather_groups(0)[0][0], token + relaid)
    for l in range(depth):
        groups = gather_groups(l)
        got = dict(zip(groups[0][1], _gather_wait(passing[l, groups[0][0]], after=after)))
        w = _mixer_weights(got, rep, l)
        x1, sv = _mixer_fwd(xl, w, cos_t, sin_t)
        after = (x1,)
        if len(groups) > 1:
            after = pass_on(l, groups[1][0], after)
        if l + 1 < depth:
            after = pass_on(l + 1, gather_groups(l + 1)[0][0], after)
        if len(groups) > 1:
            got = dict(zip(groups[1][1], _gather_wait(passing[l, groups[1][0]], after=after)))
        w.update(_ffn_weights(got))
        xl, sv_ffn = _ffn_layer_fwd(x1, w)
        sv.update(sv_ffn)
        ws.append(w)
        saved.append(sv)
        after = (xl,)
    dx, loss_part = _loss_fwd_bwd(xl, target[0])
    loss = lax.psum(loss_part, AXES)

    outs = {}
    rep_recvs, sgw_recvs = [None] * depth, [None] * depth

    def finish(l, parts, after):
        for names, handle in parts:
            got = _exchange_wait(handle, after=after)
            for n, recv in zip(names, got):
                outs[n] = _adamw_shard(l, recv, w_sh[n], m_sh[n], v_sh[n], outs.get(n), "adamw_" + n)
            after = tuple(outs[n][0] for n in names) or after
        rep_recvs[l], sgw_recvs[l] = got[0], got[1]
        return after

    token, prev, after = (), None, ()
    for l in reversed(range(depth)):
        dx1, send_ffn, vec = _ffn_layer_bwd(dx, saved[l], ws[l], deps=token)
        parts = []

        def start(names, send, extra=(), l=l, parts=parts):
            tag = names[0] if names else "rep"
            handle = _exchange_start([(send[n], "scatter") for n in names] + list(extra), "exchange_%s_grads_%d" % (tag, l))
            parts.append((names, handle))
            return (handle["token"],)

        token = start(FFN_BIG, send_ffn)
        dx, rep_pack, dsg_w = _mixer_bwd(dx1, saved[l], ws[l], cos_t, sin_t, shard_cols, vec, start, deps=token)
        token = start((), {}, extra=[(rep_pack, "gather"), (dsg_w, "gather")])
        if prev is not None:
            after = finish(*prev, after=(dx,) + token)
        prev = (l, parts)
    finish(*prev, after=after)
    vec_names = [n for n, _, _, _ in REP_VECS]
    rep_outs = _adamw_rep(rep_recvs, [a[n] for n in vec_names], [m_in[n] for n in vec_names], [v_in[n] for n in vec_names])
    for i, n in enumerate(vec_names):
        outs[n] = rep_outs[4 * i:4 * i + 4]
    outs["sg_w"] = _adamw_sg_w(sgw_recvs, a["sg_w"], m_in["sg_w"], v_in["sg_w"])
    for n in TRANSPOSED:
        outs[n] = tuple(view(o, n) for o in outs[n])

    grad_w, delta_w, new_m, new_v = ([outs[n][j] for n in WEIGHTS] for j in range(4))
    return (loss, dx[None], *grad_w, *delta_w, *new_m, *new_v)
```

```python
import math

import jax
import jax.numpy as jnp
from jax import lax
from jax.experimental import pallas as pl
from jax.experimental.pallas import tpu as pltpu

f32 = jnp.float32
bf16 = jnp.bfloat16

N_DEV = 8
AXES = ("x", "y", "c")
EPS = 1e-6
D_CONV = 512
CONV_K = 31
CONV_HALO = 32
D_SG = 512
SG_GROUPS = 4
SG_CHUNK = 128
N_HEADS = 8
QK_NOPE = 64
QK_ROPE = 32
V_HEAD = 64
HEAD_PAD = 128
Q_LORA = 384
KV_LORA = 256
D_FF = 2816
FFN_K = 3
FFN_HALO = 8
ATT_SCALE = (QK_NOPE + QK_ROPE) ** -0.5
ATT_BLOCK = 256
ATT_HEADS = 4
NEG = float(jnp.finfo(jnp.float32).min)

ADAM_LR = 0.001
ADAM_B1 = 0.9
ADAM_B2 = 0.999
ADAM_EPS = 1e-08
ADAM_WD = 0.01
ADAM_STEP = 10

LANES = 128
VMEM_MB = 56

REF_CUTS = (0, 1024, 2048, 2432, 2688, 2720, 5792)
SEGS = ((0, 1024), (1024, 1024), (2048, 384), (2432, 256), (2688, 128), (2816, 3072))
D_IN = 5792
D_IN_PAD = 5888
KR_LO = 64
SEG_INNER = (0, 0, 0, 0, KR_LO, 0)

IN_NAMES = ['x', 'positions', 'mix_pre_g', 'mix_post_g', 'ffn_pre_g', 'ffn_post_g', 'w_in', 'conv_dw_w', 'conv_dw_b', 'conv_ln_g', 'conv_ln_b', 'conv_out_w', 'sg_ln_g', 'sg_ln_b', 'sg_w', 'sg_b', 'sg_out_w', 'mla_q_norm_g', 'mla_w_uq', 'mla_kv_norm_g', 'mla_w_ukv', 'mla_w_o', 'w_out', 'ffn_w_up', 'ffn_dw_w', 'ffn_dw_b', 'ffn_w_down']
WEIGHTS = IN_NAMES[2:]
SHARDED = ("w_in", "conv_dw_w", "conv_out_w", "sg_out_w", "mla_w_uq", "mla_w_ukv", "mla_w_o", "w_out", "ffn_w_up", "ffn_dw_w",
           "ffn_w_down")
FFN_BIG = ("ffn_w_up", "ffn_w_down")
MIX_GROUP = tuple(n for n in SHARDED if n not in FFN_BIG)
MIX_SMALL = tuple(n for n in MIX_GROUP if n != "w_in")
TRANSPOSED = ("ffn_w_up",)
WIRE_F32 = ("conv_dw_w", "ffn_dw_w")
REP_VECS = (("mix_pre_g", 0, 1024, 1), ("mix_post_g", 1, 1024, 1), ("ffn_pre_g", 2, 1024, 1), ("ffn_post_g", 3, 1024, 1),
            ("conv_dw_b", 4, 512, 1), ("conv_ln_g", 5, 512, 1), ("conv_ln_b", 6, 512, 1), ("sg_ln_g", 7, 512, 1),
            ("sg_ln_b", 8, 512, 1), ("mla_q_norm_g", 9, 384, 1), ("mla_kv_norm_g", 10, 256, 1), ("ffn_dw_b", 11, 5632, 1),
            ("sg_b", 12, 128, 4))
REP_ROWS = 16
REP_W = 5632


def _cparams(n_axes):
    return pltpu.CompilerParams(dimension_semantics=("arbitrary",) * n_axes, vmem_limit_bytes=VMEM_MB * 2 ** 20)


def _rows(tb, n):
    return pl.BlockSpec((tb, n), lambda i: (i, 0))


def _rows_rev(tb, n, nb):
    return pl.BlockSpec((tb, n), lambda i: (nb - 1 - i, 0))


def _const(shape):
    nd = len(shape)
    return pl.BlockSpec(shape, lambda *_: (0,) * nd, pipeline_mode=pl.Buffered(1))


def _acc(shape):
    nd = len(shape)
    return pl.BlockSpec(shape, lambda *_: (0,) * nd)


def _whole(arr):
    return pl.BlockSpec(arr.shape, lambda *_: (0,) * arr.ndim)


def _dot(a, b):
    return jnp.dot(a, b, preferred_element_type=f32)


def _dot_nt(a, b):
    return lax.dot_general(a, b, (((1,), (1,)), ((), ())), preferred_element_type=f32)


def _dot_tn(a, b):
    return lax.dot_general(a, b, (((0,), (0,)), ((), ())), preferred_element_type=f32)


def _mean(x):
    return jnp.mean(x, axis=-1, keepdims=True)


def _colsum(x):
    return jnp.sum(x, axis=0, keepdims=True)


def _rms_fwd(x, g):
    return x * lax.rsqrt(_mean(x * x) + EPS) * g


def _rms_bwd(x, g, dy):
    r = lax.rsqrt(_mean(x * x) + EPS)
    n = x * r
    dn = dy * g
    return r * (dn - n * _mean(dn * n)), _colsum(dy * n)


def _ln_stats(x):
    mu = _mean(x)
    d = x - mu
    rstd = lax.rsqrt(_mean(d * d) + EPS)
    return d * rstd, rstd


def _ln_bwd(xhat, rstd, g, dy):
    dxh = dy * g
    return rstd * (dxh - _mean(dxh) - xhat * _mean(dxh * xhat))


_GELU_C0 = math.sqrt(2.0 / math.pi)
_GELU_C1 = 0.044715


def _gelu(x):
    t = jnp.tanh(_GELU_C0 * (x + _GELU_C1 * (x * x * x)))
    return 0.5 * x * (1.0 + t)


def _gelu_and_grad(x):
    x2 = x * x
    t = jnp.tanh(_GELU_C0 * (x + _GELU_C1 * (x2 * x)))
    g = 0.5 * x * (1.0 + t)
    dg = 0.5 * (1.0 + t) + 0.5 * x * (1.0 - t * t) * (_GELU_C0 * (1.0 + 3.0 * _GELU_C1 * x2))
    return g, dg


def _swap_rope_halves(x):
    n = x.shape[1]
    half = QK_ROPE // 2
    lane = lax.broadcasted_iota(jnp.int32, x.shape, 1) % HEAD_PAD
    first = (lane >= KR_LO) & (lane < KR_LO + half)
    second = (lane >= KR_LO + half) & (lane < KR_LO + QK_ROPE)
    return jnp.where(first, pltpu.roll(x, n - half, 1), jnp.where(second, pltpu.roll(x, half, 1), 0.0))


def _tb(t):
    return min(256, t)


_HBM = pl.BlockSpec(memory_space=pltpu.HBM)
_SEM = pl.BlockSpec(memory_space=pltpu.SEMAPHORE)
_ANY = pl.BlockSpec(memory_space=pl.ANY)
_EFFECT = pltpu.SideEffectType.DATAFLOW_SIDE_EFFECTING


def _exchange_copies(modes, ins, lands, send_sems, recv_sems, loc_sems):
    x, y, c = lax.axis_index("x"), lax.axis_index("y"), lax.axis_index("c")
    me = 4 * x + 2 * y + c
    copies = []
    for a, mode in enumerate(modes):
        def src(dst_index, a=a, mode=mode):
            return ins[a].at[dst_index] if mode == "scatter" else ins[a]
        copies.append(pltpu.make_async_copy(src(me), lands[a].at[me], loc_sems.at[a]))
        for k in range(1, N_DEV):
            px = 1 - x if (k >> 2) & 1 else x
            py = 1 - y if (k >> 1) & 1 else y
            pc = 1 - c if k & 1 else c
            copies.append(pltpu.make_async_remote_copy(
                src_ref=src(4 * px + 2 * py + pc), dst_ref=lands[a].at[me],
                send_sem=send_sems.at[a * (N_DEV - 1) + k - 1], recv_sem=recv_sems.at[a * (N_DEV - 1) + k - 1],
                device_id=(px, py, pc), device_id_type=pl.DeviceIdType.MESH))
    return copies


def _exchange_start(ops, name, deps=()):
    n = len(ops)
    arrs = [arr for arr, _ in ops]
    modes = [mode for _, mode in ops]
    lands = [lax.empty((N_DEV,) + arr.shape if mode == "gather" else arr.shape, arr.dtype) for arr, mode in ops]

    def body(*refs):
        ins, land_refs = refs[:n], refs[n:2 * n]
        send_sems, recv_sems, loc_sems = refs[2 * n + len(deps):2 * n + len(deps) + 3]
        for cp in _exchange_copies(modes, ins, land_refs, send_sems, recv_sems, loc_sems):
            cp.start()
        refs[-1][...] = jnp.zeros((8, LANES), f32)

    n_rem = n * (N_DEV - 1)
    res = pl.pallas_call(
        body, name=name,
        out_shape=(pltpu.SemaphoreType.DMA((n_rem,)), pltpu.SemaphoreType.DMA((n_rem,)), pltpu.SemaphoreType.DMA((n,)),
                   *[pltpu.HBM(x.shape, x.dtype) for x in arrs + lands], jax.ShapeDtypeStruct((8, LANES), f32)),
        in_specs=[_HBM] * (2 * n) + [_ANY] * len(deps),
        out_specs=(_SEM, _SEM, _SEM, *[_HBM] * (2 * n), pl.BlockSpec(memory_space=pltpu.VMEM)),
        input_output_aliases={i: 3 + i for i in range(2 * n)},
        compiler_params=pltpu.CompilerParams(has_side_effects=_EFFECT),
    )(*[pltpu.with_memory_space_constraint(x, pltpu.HBM) for x in arrs + lands], *deps)
    return dict(modes=modes, sems=res[:3], thru=res[3:3 + 2 * n], token=res[-1], name=name)


def _exchange_wait(handle, after=()):
    modes, thru = handle["modes"], handle["thru"]
    n = len(modes)

    def body(*refs):
        ins, land_refs = refs[:n], refs[n:2 * n]
        send_sems, recv_sems, loc_sems = refs[2 * n:2 * n + 3]
        for cp in _exchange_copies(modes, ins, land_refs, send_sems, recv_sems, loc_sems):
            cp.wait()

    res = pl.pallas_call(
        body, name=handle["name"] + "_wait",
        out_shape=tuple(pltpu.HBM(x.shape, x.dtype) for x in thru),
        in_specs=[_HBM] * (2 * n) + [_SEM] * 3 + [_ANY] * len(after),
        out_specs=tuple([_HBM] * (2 * n)),
        input_output_aliases={i: i for i in range(2 * n)},
        compiler_params=pltpu.CompilerParams(has_side_effects=_EFFECT),
    )(*thru, *handle["sems"], *after)
    return res[n:]


_SAME_CORE_PEERS = ((1, 0), (0, 1), (1, 1))


def _gather_copies_one(ins, lands, send_sems, recv_sems, loc_sems):
    x, y, c = lax.axis_index("x"), lax.axis_index("y"), lax.axis_index("c")
    me = 4 * x + 2 * y + c
    targets = [(x, y, 1 - c)] + [(1 - x if fx else x, 1 - y if fy else y, c) for fx, fy in _SAME_CORE_PEERS]
    copies = []
    for a in range(len(ins)):
        copies.append(pltpu.make_async_copy(ins[a], lands[a].at[me], loc_sems.at[a]))
        for j, target in enumerate(targets):
            copies.append(pltpu.make_async_remote_copy(
                src_ref=ins[a], dst_ref=lands[a].at[me], send_sem=send_sems.at[4 * a + j], recv_sem=recv_sems.at[4 * a + j],
                device_id=target, device_id_type=pl.DeviceIdType.MESH))
    return copies


def _gather_copies_two(lands, send_sems, recv_sems):
    x, y, c = lax.axis_index("x"), lax.axis_index("y"), lax.axis_index("c")
    copies = []
    for a in range(len(lands)):
        for j, (fx, fy) in enumerate(_SAME_CORE_PEERS):
            slot = 4 * (1 - x if fx else x) + 2 * (1 - y if fy else y) + c
            copies.append(pltpu.make_async_remote_copy(
                src_ref=lands[a].at[slot], dst_ref=lands[a].at[slot], send_sem=send_sems.at[3 * a + j],
                recv_sem=recv_sems.at[3 * a + j], device_id=(x, y, 1 - c), device_id_type=pl.DeviceIdType.MESH))
    return copies


def _gather_start(arrs, name, deps=()):
    n = len(arrs)
    lands = [lax.empty((N_DEV,) + arr.shape, arr.dtype) for arr in arrs]

    def body(*refs):
        ins, land_refs = refs[:n], refs[n:2 * n]
        send_sems, recv_sems, loc_sems = refs[2 * n + len(deps):2 * n + len(deps) + 3]
        for cp in _gather_copies_one(ins, land_refs, send_sems, recv_sems, loc_sems):
            cp.start()
        refs[-1][...] = jnp.zeros((8, LANES), f32)

    res = pl.pallas_call(
        body, name=name,
        out_shape=(pltpu.SemaphoreType.DMA((4 * n,)), pltpu.SemaphoreType.DMA((4 * n,)), pltpu.SemaphoreType.DMA((n,)),
                   *[pltpu.HBM(x.shape, x.dtype) for x in arrs + lands], jax.ShapeDtypeStruct((8, LANES), f32)),
        in_specs=[_HBM] * (2 * n) + [_ANY] * len(deps),
        out_specs=(_SEM, _SEM, _SEM, *[_HBM] * (2 * n), pl.BlockSpec(memory_space=pltpu.VMEM)),
        input_output_aliases={i: 3 + i for i in range(2 * n)},
        compiler_params=pltpu.CompilerParams(has_side_effects=_EFFECT),
    )(*[pltpu.with_memory_space_constraint(x, pltpu.HBM) for x in arrs + lands], *deps)
    return dict(n=n, sems=res[:3], thru=res[3:3 + 2 * n], token=res[-1], name=name)


def _gather_forward(handle, after=()):
    n, thru = handle["n"], handle["thru"]

    def body(*refs):
        ins, land_refs = refs[:n], refs[n:2 * n]
        send_one, recv_one, loc_sems = refs[2 * n:2 * n + 3]
        send_two, recv_two = refs[2 * n + 3 + len(after):2 * n + 5 + len(after)]
        for cp in _gather_copies_one(ins, land_refs, send_one, recv_one, loc_sems):
            cp.wait()
        for cp in _gather_copies_two(land_refs, send_two, recv_two):
            cp.start()
        refs[-1][...] = jnp.zeros((8, LANES), f32)

    res = pl.pallas_call(
        body, name=handle["name"] + "_forward",
        out_shape=(pltpu.SemaphoreType.DMA((3 * n,)), pltpu.SemaphoreType.DMA((3 * n,)),
                   *[pltpu.HBM(x.shape, x.dtype) for x in thru], jax.ShapeDtypeStruct((8, LANES), f32)),
        in_specs=[_HBM] * (2 * n) + [_SEM] * 3 + [_ANY] * len(after),
        out_specs=(_SEM, _SEM, *[_HBM] * (2 * n), pl.BlockSpec(memory_space=pltpu.VMEM)),
        input_output_aliases={i: 2 + i for i in range(2 * n)},
        compiler_params=pltpu.CompilerParams(has_side_effects=_EFFECT),
    )(*thru, *handle["sems"], *after)
    return dict(n=n, sems=res[:2], lands=res[2 + n:2 + 2 * n], token=res[-1], name=handle["name"])


def _gather_wait(handle, after=()):
    n, lands = handle["n"], handle["lands"]

    def body(*refs):
        land_refs = refs[:n]
        send_two, recv_two = refs[n:n + 2]
        for cp in _gather_copies_two(land_refs, send_two, recv_two):
            cp.wait()

    return pl.pallas_call(
        body, name=handle["name"] + "_wait",
        out_shape=tuple(pltpu.HBM(x.shape, x.dtype) for x in lands),
        in_specs=[_HBM] * n + [_SEM] * 2 + [_ANY] * len(after),
        out_specs=tuple([_HBM] * n),
        input_output_aliases={i: i for i in range(n)},
        compiler_params=pltpu.CompilerParams(has_side_effects=_EFFECT),
    )(*lands, *handle["sems"], *after)


def _w_in_pieces(ns):
    out = []
    for e in range(N_DEV):
        lo, hi = ns * e, ns * (e + 1)
        for s in range(len(SEGS)):
            a, b = max(lo, REF_CUTS[s]), min(hi, REF_CUTS[s + 1])
            if a < b:
                inner = SEG_INNER[s] + a - REF_CUTS[s]
                out.append((e, a - lo, b - lo, s, inner, inner + b - a))
    return out


def _asm_w_in(g):
    _, d, ns = g.shape
    rb = 256
    pieces = _w_in_pieces(ns)

    def body(g_ref, o_ref):
        kr = SEGS[4][0]
        o_ref[:, kr:kr + KR_LO] = jnp.zeros((rb, KR_LO), g.dtype)
        o_ref[:, kr + KR_LO + QK_ROPE:kr + HEAD_PAD] = jnp.zeros((rb, HEAD_PAD - KR_LO - QK_ROPE), g.dtype)
        for e, s0, s1, seg, d0, d1 in pieces:
            off = SEGS[seg][0]
            o_ref[:, off + d0:off + d1] = g_ref[e, :, s0:s1]

    return pl.pallas_call(
        body, name="asm_w_in", grid=(d // rb,),
        in_specs=[pl.BlockSpec((N_DEV, rb, ns), lambda i: (0, i, 0))],
        out_specs=_rows(rb, D_IN_PAD), out_shape=jax.ShapeDtypeStruct((d, D_IN_PAD), g.dtype),
        compiler_params=_cparams(1),
    )(g)


def _dis_w_in(dsegs, ns):
    d = dsegs[0].shape[0]
    rb = 256
    pieces = _w_in_pieces(ns)

    def body(*refs):
        seg_refs, o_ref = refs[:len(SEGS)], refs[len(SEGS)]
        for e, s0, s1, seg, d0, d1 in pieces:
            o_ref[e, :, s0:s1] = seg_refs[seg][:, d0:d1].astype(bf16)

    return pl.pallas_call(
        body, name="dis_w_in", grid=(d // rb,),
        in_specs=[_rows(rb, wd) for _, wd in SEGS],
        out_specs=pl.BlockSpec((N_DEV, rb, ns), lambda i: (0, i, 0)),
        out_shape=jax.ShapeDtypeStruct((N_DEV, d, ns), bf16),
        compiler_params=_cparams(1),
    )(*dsegs)


def _asm_small(g_conv_out, g_sg_out, g_uq, g_ukv, g_wo, g_cdw, g_fdw):
    d = g_conv_out.shape[2] * N_DEV
    hw = N_HEADS * HEAD_PAD
    hq = QK_NOPE + QK_ROPE
    ff2 = g_fdw.shape[2] * N_DEV
    cw, fw = g_cdw.shape[2], g_fdw.shape[2]

    def body(co_ref, so_ref, uq_ref, ukv_ref, wo_ref, cdw_ref, fdw_ref, o_co, o_so, o_uq, o_k, o_v, o_wo, o_cdw, o_fdw):
        o_cdw[CONV_K:CONV_HALO, :] = jnp.zeros((CONV_HALO - CONV_K, D_CONV), f32)
        o_fdw[FFN_K:FFN_HALO, :] = jnp.zeros((FFN_HALO - FFN_K, ff2), f32)
        for e in range(N_DEV):
            cs = e * HEAD_PAD
            o_co[:, cs:cs + HEAD_PAD] = co_ref[e]
            o_so[:, cs:cs + HEAD_PAD] = so_ref[e]
            o_uq[:, cs:cs + hq] = uq_ref[e]
            o_uq[:, cs + hq:cs + HEAD_PAD] = jnp.zeros((Q_LORA, HEAD_PAD - hq), bf16)
            o_k[:, cs:cs + QK_NOPE] = ukv_ref[e, :, 0:QK_NOPE]
            o_k[:, cs + QK_NOPE:cs + HEAD_PAD] = jnp.zeros((KV_LORA, HEAD_PAD - QK_NOPE), bf16)
            o_v[:, cs:cs + V_HEAD] = ukv_ref[e, :, QK_NOPE:QK_NOPE + V_HEAD]
            o_v[:, cs + V_HEAD:cs + HEAD_PAD] = jnp.zeros((KV_LORA, HEAD_PAD - V_HEAD), bf16)
            for h in range(N_HEADS):
                o_wo[h * HEAD_PAD:h * HEAD_PAD + V_HEAD, cs:cs + HEAD_PAD] = wo_ref[e, h * V_HEAD:(h + 1) * V_HEAD, :]
                o_wo[h * HEAD_PAD + V_HEAD:(h + 1) * HEAD_PAD, cs:cs + HEAD_PAD] = jnp.zeros((HEAD_PAD - V_HEAD, HEAD_PAD), bf16)
            o_cdw[0:CONV_K, e * cw:(e + 1) * cw] = cdw_ref[e]
            o_fdw[0:FFN_K, e * fw:(e + 1) * fw] = fdw_ref[e]

    ins = (g_conv_out, g_sg_out, g_uq, g_ukv, g_wo, g_cdw, g_fdw)
    out_shape = (jax.ShapeDtypeStruct((D_CONV, d), bf16), jax.ShapeDtypeStruct((D_SG, d), bf16), jax.ShapeDtypeStruct((Q_LORA, hw), bf16),
                 jax.ShapeDtypeStruct((KV_LORA, hw), bf16), jax.ShapeDtypeStruct((KV_LORA, hw), bf16), jax.ShapeDtypeStruct((hw, d), bf16),
                 jax.ShapeDtypeStruct((CONV_HALO, D_CONV), f32), jax.ShapeDtypeStruct((FFN_HALO, ff2), f32))
    return pl.pallas_call(
        body, name="asm_small", grid=(1,),
        in_specs=[_whole(a) for a in ins], out_specs=tuple(_whole(o) for o in out_shape), out_shape=out_shape,
        compiler_params=_cparams(1),
    )(*ins)


def _dis_small(d_co, d_so, d_uq, d_k, d_v, d_wo, d_cdw, d_fdw):
    d = d_co.shape[1]
    hq = QK_NOPE + QK_ROPE
    cw, fw = D_CONV // N_DEV, d_fdw.shape[1] // N_DEV

    def body(co_ref, so_ref, uq_ref, k_ref, v_ref, wo_ref, cdw_ref, fdw_ref, o_co, o_so, o_uq, o_ukv, o_wo, o_cdw, o_fdw):
        for e in range(N_DEV):
            cs = e * HEAD_PAD
            o_co[e] = co_ref[:, cs:cs + HEAD_PAD].astype(bf16)
            o_so[e] = so_ref[:, cs:cs + HEAD_PAD].astype(bf16)
            o_uq[e] = uq_ref[:, cs:cs + hq].astype(bf16)
            o_ukv[e, :, 0:QK_NOPE] = k_ref[:, cs:cs + QK_NOPE].astype(bf16)
            o_ukv[e, :, QK_NOPE:QK_NOPE + V_HEAD] = v_ref[:, cs:cs + V_HEAD].astype(bf16)
            for h in range(N_HEADS):
                o_wo[e, h * V_HEAD:(h + 1) * V_HEAD, :] = wo_ref[h * HEAD_PAD:h * HEAD_PAD + V_HEAD, cs:cs + HEAD_PAD].astype(bf16)
            o_cdw[e] = cdw_ref[0:CONV_K, e * cw:(e + 1) * cw]
            o_fdw[e] = fdw_ref[0:FFN_K, e * fw:(e + 1) * fw]

    ins = (d_co, d_so, d_uq, d_k, d_v, d_wo, d_cdw, d_fdw)
    out_shape = (jax.ShapeDtypeStruct((N_DEV, D_CONV, d // N_DEV), bf16), jax.ShapeDtypeStruct((N_DEV, D_SG, d // N_DEV), bf16),
                 jax.ShapeDtypeStruct((N_DEV, Q_LORA, hq), bf16), jax.ShapeDtypeStruct((N_DEV, KV_LORA, QK_NOPE + V_HEAD), bf16),
                 jax.ShapeDtypeStruct((N_DEV, N_HEADS * V_HEAD, d // N_DEV), bf16), jax.ShapeDtypeStruct((N_DEV, CONV_K, cw), f32),
                 jax.ShapeDtypeStruct((N_DEV, FFN_K, fw), f32))
    return pl.pallas_call(
        body, name="dis_small", grid=(1,),
        in_specs=[_whole(a) for a in ins], out_specs=tuple(_whole(o) for o in out_shape), out_shape=out_shape,
        compiler_params=_cparams(1),
    )(*ins)


def _pack_rep(vec_grads, dsgb):
    def body(*refs):
        o_ref = refs[-1]
        o_ref[...] = jnp.zeros((REP_ROWS, REP_W), f32)
        for (_, row, width, nrows), ref in zip(REP_VECS, refs[:-1]):
            o_ref[row:row + nrows, 0:width] = ref[0:nrows, :]

    ins = tuple(vec_grads) + (dsgb,)
    return pl.pallas_call(
        body, name="pack_rep", grid=(1,),
        in_specs=[_whole(a) for a in ins], out_specs=pl.BlockSpec((REP_ROWS, REP_W), lambda i: (0, 0)),
        out_shape=jax.ShapeDtypeStruct((REP_ROWS, REP_W), f32), compiler_params=_cparams(1),
    )(*ins)


def _mix_in_fwd(l, x, g1, w_in_p):
    t, d = x.shape
    tb = _tb(t)

    def body(x_ref, g_ref, w_ref, h_ref, *outs):
        h = _rms_fwd(x_ref[...], g_ref[l:l + 1, :]).astype(bf16)
        h_ref[...] = h
        for (off, wd), o_ref in zip(SEGS, outs):
            o_ref[...] = _dot(h, w_ref[:, off:off + wd])

    return pl.pallas_call(
        body, name="mix_in_fwd", grid=(t // tb,),
        in_specs=[_rows(tb, d), _const(g1.shape), _const((d, D_IN_PAD))],
        out_specs=tuple([_rows(tb, d)] + [_rows(tb, wd) for _, wd in SEGS]),
        out_shape=tuple([jax.ShapeDtypeStruct((t, d), bf16)] + [jax.ShapeDtypeStruct((t, wd), f32) for _, wd in SEGS]),
        compiler_params=_cparams(1),
    )(x, g1, w_in_p)


def _conv_fwd(l, a_in, dw_w, dw_b, ln_g, ln_b):
    t = a_in.shape[0]
    tb = _tb(t)

    def body(a_ref, w_ref, b_ref, g_ref, be_ref, s_ref, z1_ref, win):
        @pl.when(pl.program_id(0) == 0)
        def _():
            win[0:CONV_HALO, :] = jnp.zeros((CONV_HALO, D_CONV), f32)

        a = a_ref[...]
        win[CONV_HALO:CONV_HALO + tb, :] = a[:, :D_CONV] * jax.nn.sigmoid(a[:, D_CONV:])
        for r0 in range(0, tb, LANES):
            for c0 in range(0, D_CONV, LANES):
                cs = slice(c0, c0 + LANES)
                acc = jnp.broadcast_to(b_ref[l:l + 1, cs], (LANES, LANES))
                for j in range(CONV_K):
                    acc = acc + w_ref[j:j + 1, cs] * win[pl.ds(r0 + CONV_HALO - CONV_K + 1 + j, LANES), cs]
                z1_ref[r0:r0 + LANES, cs] = acc
        zh, _ = _ln_stats(z1_ref[...])
        zl = zh * g_ref[l:l + 1, :] + be_ref[l:l + 1, :]
        s_ref[...] = (zl * jax.nn.sigmoid(zl)).astype(bf16)
        win[0:CONV_HALO, :] = win[tb:tb + CONV_HALO, :]

    return pl.pallas_call(
        body, name="conv_fwd", grid=(t // tb,),
        in_specs=[_rows(tb, 2 * D_CONV), _const((CONV_HALO, D_CONV)), _const(dw_b.shape), _const(ln_g.shape), _const(ln_b.shape)],
        out_specs=(_rows(tb, D_CONV), _rows(tb, D_CONV)),
        out_shape=(jax.ShapeDtypeStruct((t, D_CONV), bf16), jax.ShapeDtypeStruct((t, D_CONV), f32)),
        scratch_shapes=[pltpu.VMEM((tb + CONV_HALO, D_CONV), f32)],
        compiler_params=_cparams(1),
    )(a_in, dw_w, dw_b, ln_g, ln_b)


def _tril_mask():
    r = lax.broadcasted_iota(jnp.int32, (SG_CHUNK, SG_CHUNK), 0)
    c = lax.broadcasted_iota(jnp.int32, (SG_CHUNK, SG_CHUNK), 1)
    return r >= c


def _sgu_fwd(l, b_in, ln_g, ln_b, sg_w, bexp):
    t = b_in.shape[0]
    tb = _tb(t)
    gw = D_SG // SG_GROUPS

    def body(b_ref, g_ref, be_ref, w_ref, bexp_ref, um_ref):
        gl = _gelu(b_ref[...])
        u = gl[:, :D_SG]
        vh, _ = _ln_stats(gl[:, D_SG:])
        vn = (vh * g_ref[l:l + 1, :] + be_ref[l:l + 1, :]).astype(bf16)
        tri = _tril_mask()
        for g in range(SG_GROUPS):
            wg = jnp.where(tri, w_ref[l, g], 0.0).astype(bf16)
            cs = slice(g * gw, (g + 1) * gw)
            for r0 in range(0, tb, SG_CHUNK):
                rs = slice(r0, r0 + SG_CHUNK)
                mixed = _dot(wg, vn[rs, cs]) + bexp_ref[:, cs]
                um_ref[rs, cs] = (u[rs, cs] * mixed).astype(bf16)

    return pl.pallas_call(
        body, name="sgu_fwd", grid=(t // tb,),
        in_specs=[_rows(tb, 2 * D_SG), _const(ln_g.shape), _const(ln_b.shape), _const(sg_w.shape), _const((SG_CHUNK, D_SG))],
        out_specs=_rows(tb, D_SG),
        out_shape=jax.ShapeDtypeStruct((t, D_SG), bf16),
        compiler_params=_cparams(1),
    )(b_in, ln_g, ln_b, sg_w, bexp)


def _mla_proj_fwd(l, qlat, kvlat, krope, cos_t, sin_t, gq, wuq_p, gkv, wk_p, wv_p):
    t = qlat.shape[0]
    tb = _tb(t)
    hw = N_HEADS * HEAD_PAD

    def body(ql_ref, kvl_ref, kr_ref, c_ref, s_ref, gq_ref, wq_ref, gkv_ref, wk_ref, wv_ref, q_ref, k_ref, v_ref, qn_ref, kvn_ref):
        cos_b, sin_b = c_ref[...], s_ref[...]
        qn = _rms_fwd(ql_ref[...], gq_ref[l:l + 1, :]).astype(bf16)
        qn_ref[...] = qn
        q = _dot(qn, wq_ref[...])
        sw = _swap_rope_halves(q)
        for h in range(N_HEADS):
            hs = slice(h * HEAD_PAD, (h + 1) * HEAD_PAD)
            q_ref[:, hs] = (q[:, hs] * cos_b + sw[:, hs] * sin_b).astype(bf16)
        kvn = _rms_fwd(kvl_ref[...], gkv_ref[l:l + 1, :]).astype(bf16)
        kvn_ref[...] = kvn
        kr = kr_ref[...]
        kpe = kr * cos_b + _swap_rope_halves(kr) * sin_b
        k = _dot(kvn, wk_ref[...])
        for h in range(N_HEADS):
            hs = slice(h * HEAD_PAD, (h + 1) * HEAD_PAD)
            k_ref[:, hs] = (k[:, hs] + kpe).astype(bf16)
        v_ref[...] = _dot(kvn, wv_ref[...]).astype(bf16)

    return pl.pallas_call(
        body, name="mla_proj_fwd", grid=(t // tb,),
        in_specs=[_rows(tb, Q_LORA), _rows(tb, KV_LORA), _rows(tb, HEAD_PAD), _rows(tb, HEAD_PAD), _rows(tb, HEAD_PAD),
                  _const(gq.shape), _const((Q_LORA, hw)), _const(gkv.shape), _const((KV_LORA, hw)), _const((KV_LORA, hw))],
        out_specs=(_rows(tb, hw), _rows(tb, hw), _rows(tb, hw), _rows(tb, Q_LORA), _rows(tb, KV_LORA)),
        out_shape=(jax.ShapeDtypeStruct((t, hw), bf16), jax.ShapeDtypeStruct((t, hw), bf16), jax.ShapeDtypeStruct((t, hw), bf16),
                   jax.ShapeDtypeStruct((t, Q_LORA), bf16), jax.ShapeDtypeStruct((t, KV_LORA), bf16)),
        compiler_params=_cparams(1),
    )(qlat, kvlat, krope, cos_t, sin_t, gq, wuq_p, gkv, wk_p, wv_p)


def _diag_mask(tq):
    return lax.broadcasted_iota(jnp.int32, (tq, tq), 0) >= lax.broadcasted_iota(jnp.int32, (tq, tq), 1)


def _attn_fwd(q, k, v):
    t = q.shape[0]
    tq = min(ATT_BLOCK, t)

    def body(q_ref, k_ref, v_ref, o_ref, lse_ref):
        qi = pl.program_id(1)
        for i in range(t // tq):
            @pl.when(qi == i)
            def _(i=i):
                lo, hi = i * tq, (i + 1) * tq
                for hh in range(ATT_HEADS):
                    hs = slice(hh * HEAD_PAD, (hh + 1) * HEAD_PAD)
                    qv = q_ref[:, hs]
                    s_d = jnp.where(_diag_mask(tq), _dot_nt(qv, k_ref[lo:hi, hs]) * ATT_SCALE, NEG)
                    m = jnp.max(s_d, axis=-1, keepdims=True)
                    if i > 0:
                        s_o = _dot_nt(qv, k_ref[0:lo, hs]) * ATT_SCALE
                        m = jnp.maximum(m, jnp.max(s_o, axis=-1, keepdims=True))
                    p_d = jnp.exp(s_d - m)
                    lsum = jnp.sum(p_d, axis=-1, keepdims=True)
                    acc = _dot(p_d.astype(bf16), v_ref[lo:hi, hs])
                    if i > 0:
                        p_o = jnp.exp(s_o - m)
                        lsum = lsum + jnp.sum(p_o, axis=-1, keepdims=True)
                        acc = acc + _dot(p_o.astype(bf16), v_ref[0:lo, hs])
                    o_ref[:, hs] = (acc / lsum).astype(bf16)
                    lse_ref[hh] = m + jnp.log(lsum)

    hw = ATT_HEADS * HEAD_PAD
    return pl.pallas_call(
        body, name="attn_fwd", grid=(N_HEADS // ATT_HEADS, t // tq),
        in_specs=[pl.BlockSpec((tq, hw), lambda h, i: (i, h)), pl.BlockSpec((t, hw), lambda h, i: (0, h)),
                  pl.BlockSpec((t, hw), lambda h, i: (0, h))],
        out_specs=(pl.BlockSpec((tq, hw), lambda h, i: (i, h)), pl.BlockSpec((ATT_HEADS, tq, 1), lambda h, i: (h, i, 0))),
        out_shape=(jax.ShapeDtypeStruct((t, N_HEADS * HEAD_PAD), bf16), jax.ShapeDtypeStruct((N_HEADS, t, 1), f32)),
        compiler_params=_cparams(2),
    )(q, k, v)


def _merge_out_fwd(l, x, s, um, o, gates, conv_out_w, sg_out_w, wo_p, w_out, g2):
    t, d = x.shape
    tb = _tb(t)

    def body(x_ref, s_ref, um_ref, o_ref, gt_ref, wa_ref, wb_ref, wc_ref, wout_ref, g_ref, x1_ref, mg_ref, om_ref):
        merged = (jax.nn.sigmoid(gt_ref[:, 0:d]) * _dot(s_ref[...], wa_ref[...])
                  + jax.nn.sigmoid(gt_ref[:, d:2 * d]) * _dot(um_ref[...], wb_ref[...])
                  + jax.nn.sigmoid(gt_ref[:, 2 * d:3 * d]) * _dot(o_ref[...], wc_ref[...]))
        mb = merged.astype(bf16)
        mg_ref[...] = mb
        om = _dot(mb, wout_ref[...])
        om_ref[...] = om
        x1_ref[...] = x_ref[...] + _rms_fwd(om, g_ref[l:l + 1, :])

    hw = N_HEADS * HEAD_PAD
    return pl.pallas_call(
        body, name="merge_out_fwd", grid=(t // tb,),
        in_specs=[_rows(tb, d), _rows(tb, D_CONV), _rows(tb, D_SG), _rows(tb, hw), _rows(tb, 3 * d),
                  _const((D_CONV, d)), _const((D_SG, d)), _const((hw, d)), _const((d, d)), _const(g2.shape)],
        out_specs=(_rows(tb, d), _rows(tb, d), _rows(tb, d)),
        out_shape=(jax.ShapeDtypeStruct((t, d), f32), jax.ShapeDtypeStruct((t, d), bf16), jax.ShapeDtypeStruct((t, d), f32)),
        compiler_params=_cparams(1),
    )(x, s, um, o, gates, conv_out_w, sg_out_w, wo_p, w_out, g2)


FF_CHUNK = 1408


def _ffn_conv_cols(zbuf, w_ref, b_ref, l, nrows, c0, c1):
    acc = b_ref[l:l + 1, c0:c1] + w_ref[0:1, c0:c1] * zbuf[pl.ds(FFN_HALO - 2, nrows), c0:c1]
    acc = acc + w_ref[1:2, c0:c1] * zbuf[pl.ds(FFN_HALO - 1, nrows), c0:c1]
    return acc + w_ref[2:3, c0:c1] * zbuf[pl.ds(FFN_HALO, nrows), c0:c1]


def _ffn_fwd(l, x1, g3, w_up, dw_w, dw_b, w_down, g4):
    t, d = x1.shape
    tb = _tb(t)
    ff2 = 2 * D_FF

    def body(x_ref, g3_ref, wup_ref, dww_ref, dwb_ref, wdn_ref, g4_ref, x2_ref, h2_ref, z_ref, zc_ref, act_ref, f_ref, zbuf):
        @pl.when(pl.program_id(0) == 0)
        def _():
            zbuf[0:FFN_HALO, :] = jnp.zeros((FFN_HALO, ff2), f32)

        xv = x_ref[...]
        h2 = _rms_fwd(xv, g3_ref[l:l + 1, :]).astype(bf16)
        h2_ref[...] = h2
        for c0 in range(0, ff2, FF_CHUNK):
            zv = _dot_nt(h2, wup_ref[c0:c0 + FF_CHUNK, :])
            z_ref[:, c0:c0 + FF_CHUNK] = zv.astype(bf16)
            zbuf[FFN_HALO:FFN_HALO + tb, c0:c0 + FF_CHUNK] = zv
        facc = jnp.zeros((tb, d), f32)
        for c0 in range(0, D_FF, FF_CHUNK):
            gg = _ffn_conv_cols(zbuf, dww_ref, dwb_ref, l, tb, c0, c0 + FF_CHUNK)
            vv = _ffn_conv_cols(zbuf, dww_ref, dwb_ref, l, tb, D_FF + c0, D_FF + c0 + FF_CHUNK)
            zc_ref[:, c0:c0 + FF_CHUNK] = gg.astype(bf16)
            zc_ref[:, D_FF + c0:D_FF + c0 + FF_CHUNK] = vv.astype(bf16)
            a = (_gelu(gg) * vv).astype(bf16)
            act_ref[:, c0:c0 + FF_CHUNK] = a
            facc = facc + _dot(a, wdn_ref[c0:c0 + FF_CHUNK, :])
        f_ref[...] = facc
        x2_ref[...] = xv + _rms_fwd(facc, g4_ref[l:l + 1, :])
        zbuf[0:FFN_HALO, :] = zbuf[tb:tb + FFN_HALO, :]

    return pl.pallas_call(
        body, name="ffn_fwd", grid=(t // tb,),
        in_specs=[_rows(tb, d), _const(g3.shape), _const((ff2, d)), _const((FFN_HALO, ff2)), _const(dw_b.shape), _const((D_FF, d)),
                  _const(g4.shape)],
        out_specs=(_rows(tb, d), _rows(tb, d), _rows(tb, ff2), _rows(tb, ff2), _rows(tb, D_FF), _rows(tb, d)),
        out_shape=(jax.ShapeDtypeStruct((t, d), f32), jax.ShapeDtypeStruct((t, d), bf16), jax.ShapeDtypeStruct((t, ff2), bf16),
                   jax.ShapeDtypeStruct((t, ff2), bf16), jax.ShapeDtypeStruct((t, D_FF), bf16), jax.ShapeDtypeStruct((t, d), f32)),
        scratch_shapes=[pltpu.VMEM((tb + FFN_HALO, ff2), f32)],
        compiler_params=_cparams(1),
    )(x1, g3, w_up, dw_w, dw_b, w_down, g4)


def _loss_fwd_bwd(y, target):
    t, d = y.shape
    tb = _tb(t)

    def body(y_ref, t_ref, dy_ref, loss_ref):
        @pl.when(pl.program_id(0) == 0)
        def _():
            loss_ref[...] = jnp.zeros((1, LANES), f32)

        e = y_ref[...] - t_ref[...]
        dy_ref[...] = e * (1.0 / d)
        loss_ref[...] += 0.5 * jnp.sum(_mean(e * e))

    dy, loss = pl.pallas_call(
        body, name="loss", grid=(t // tb,),
        in_specs=[_rows(tb, d), _rows(tb, d)],
        out_specs=(_rows(tb, d), _acc((1, LANES))),
        out_shape=(jax.ShapeDtypeStruct((t, d), f32), jax.ShapeDtypeStruct((1, LANES), f32)),
        compiler_params=_cparams(1),
    )(y, target)
    return dy, loss[0, 0]


def _ffn_bwd(l, dx2, x1, f, z, zc, w_up, dw_w, w_down, g3, g4, deps=()):
    t, d = x1.shape
    tb = min(128, t)
    nb = t // tb
    ff2 = 2 * D_FF
    hrows = 16
    per_h = tb // hrows

    def body(*refs):
        (dx2_ref, x1_ref, f_ref, z_ref, zp_ref, zc_ref, wup_ref, dww_ref, wdn_ref, g3_ref, g4_ref,
         dx1_ref, df_ref, dz_ref, dg4_ref, dg3_ref, ddwb_ref, ddww_ref, zbuf, dzc) = refs[len(deps):]
        i = pl.program_id(0)
        b = nb - 1 - i

        @pl.when(i == 0)
        def _():
            dg4_ref[...] = jnp.zeros_like(dg4_ref)
            dg3_ref[...] = jnp.zeros_like(dg3_ref)
            ddwb_ref[...] = jnp.zeros_like(ddwb_ref)
            ddww_ref[...] = jnp.zeros_like(ddww_ref)
            dzc[tb:tb + FFN_HALO, :] = jnp.zeros((FFN_HALO, ff2), f32)

        dout = dx2_ref[...]
        df, dg4 = _rms_bwd(f_ref[...], g4_ref[l:l + 1, :], dout)
        dg4_ref[...] += dg4
        dfb = df.astype(bf16)
        df_ref[...] = dfb
        zbuf[0:FFN_HALO, :] = jnp.where(b > 0, zp_ref[hrows - FFN_HALO:hrows, :].astype(f32), 0.0)
        zbuf[FFN_HALO:FFN_HALO + tb, :] = z_ref[...].astype(f32)
        for c0 in range(0, D_FF, FF_CHUNK):
            dact = _dot_nt(dfb, wdn_ref[c0:c0 + FF_CHUNK, :])
            gel, dgel = _gelu_and_grad(zc_ref[:, c0:c0 + FF_CHUNK].astype(f32))
            dzc[0:tb, c0:c0 + FF_CHUNK] = dact * zc_ref[:, D_FF + c0:D_FF + c0 + FF_CHUNK].astype(f32) * dgel
            dzc[0:tb, D_FF + c0:D_FF + c0 + FF_CHUNK] = dact * gel
        dh2 = jnp.zeros((tb, d), f32)
        for c0 in range(0, ff2, FF_CHUNK):
            cs = slice(c0, c0 + FF_CHUNK)
            d0 = dzc[0:tb, cs]
            ddwb_ref[:, cs] += _colsum(d0)
            for j in range(FFN_K):
                ddww_ref[j:j + 1, cs] += _colsum(d0 * zbuf[pl.ds(FFN_HALO - 2 + j, tb), cs])
            dzv = dww_ref[2:3, cs] * d0 + dww_ref[1:2, cs] * dzc[pl.ds(1, tb), cs] + dww_ref[0:1, cs] * dzc[pl.ds(2, tb), cs]
            dzb = dzv.astype(bf16)
            dz_ref[:, cs] = dzb
            dh2 = dh2 + _dot(dzb, wup_ref[cs, :])
        dzc[tb:tb + FFN_HALO, :] = dzc[0:FFN_HALO, :]
        dxn, dg3 = _rms_bwd(x1_ref[...], g3_ref[l:l + 1, :], dh2)
        dg3_ref[...] += dg3
        dx1_ref[...] = dout + dxn

    return pl.pallas_call(
        body, name="ffn_bwd", grid=(nb,),
        in_specs=[_ANY] * len(deps) + [_rows_rev(tb, d, nb), _rows_rev(tb, d, nb), _rows_rev(tb, d, nb), _rows_rev(tb, ff2, nb),
                  pl.BlockSpec((hrows, ff2), lambda i: (jnp.maximum((nb - 1 - i) * per_h - 1, 0), 0)), _rows_rev(tb, ff2, nb),
                  _const((ff2, d)), _const((FFN_HALO, ff2)), _const((D_FF, d)), _const(g3.shape), _const(g4.shape)],
        out_specs=(_rows_rev(tb, d, nb), _rows_rev(tb, d, nb), _rows_rev(tb, ff2, nb), _acc((1, d)), _acc((1, d)), _acc((1, ff2)),
                   _acc((FFN_HALO, ff2))),
        out_shape=(jax.ShapeDtypeStruct((t, d), f32), jax.ShapeDtypeStruct((t, d), bf16), jax.ShapeDtypeStruct((t, ff2), bf16),
                   jax.ShapeDtypeStruct((1, d), f32), jax.ShapeDtypeStruct((1, d), f32), jax.ShapeDtypeStruct((1, ff2), f32),
                   jax.ShapeDtypeStruct((FFN_HALO, ff2), f32)),
        scratch_shapes=[pltpu.VMEM((tb + FFN_HALO, ff2), f32), pltpu.VMEM((tb + FFN_HALO, ff2), f32)],
        compiler_params=_cparams(1),
    )(*deps, dx2, x1, f, z, z, zc, w_up, dw_w, w_down, g3, g4)


def _merge_out_bwd(l, dx1, om, s, um, o, gates, conv_out_w, sg_out_w, wo_p, w_out, g2, deps=()):
    t, d = dx1.shape
    tb = _tb(t)
    hw = N_HEADS * HEAD_PAD

    def body(*refs):
        (dx_ref, om_ref, s_ref, um_ref, o_ref, gt_ref, wa_ref, wb_ref, wc_ref, wout_ref, g_ref,
         dom_ref, dgt_ref, dya_ref, dyb_ref, dyc_ref, ds_ref, dum_ref, do_ref, dg2_ref) = refs[len(deps):]

        @pl.when(pl.program_id(0) == 0)
        def _():
            dg2_ref[...] = jnp.zeros_like(dg2_ref)

        dom, dg2 = _rms_bwd(om_ref[...], g_ref[l:l + 1, :], dx_ref[...])
        dg2_ref[...] += dg2
        domb = dom.astype(bf16)
        dom_ref[...] = domb
        dmerged = _dot_nt(domb, wout_ref[...])
        branches = ((s_ref, wa_ref, dya_ref, ds_ref), (um_ref, wb_ref, dyb_ref, dum_ref), (o_ref, wc_ref, dyc_ref, do_ref))
        for br, (in_ref, w_ref, dy_ref, din_ref) in enumerate(branches):
            yv = _dot(in_ref[...], w_ref[...])
            sg = jax.nn.sigmoid(gt_ref[:, br * d:(br + 1) * d])
            dyb = (dmerged * sg).astype(bf16)
            dy_ref[...] = dyb
            dgt_ref[:, br * d:(br + 1) * d] = (dmerged * yv * sg * (1.0 - sg)).astype(bf16)
            din_ref[...] = _dot_nt(dyb, w_ref[...]).astype(din_ref.dtype)

    return pl.pallas_call(
        body, name="merge_out_bwd", grid=(t // tb,),
        in_specs=[_ANY] * len(deps) + [_rows(tb, d), _rows(tb, d), _rows(tb, D_CONV), _rows(tb, D_SG), _rows(tb, hw), _rows(tb, 3 * d),
                  _const((D_CONV, d)), _const((D_SG, d)), _const((hw, d)), _const((d, d)), _const(g2.shape)],
        out_specs=(_rows(tb, d), _rows(tb, 3 * d), _rows(tb, d), _rows(tb, d), _rows(tb, d), _rows(tb, D_CONV), _rows(tb, D_SG),
                   _rows(tb, hw), _acc((1, d))),
        out_shape=(jax.ShapeDtypeStruct((t, d), bf16), jax.ShapeDtypeStruct((t, 3 * d), bf16), jax.ShapeDtypeStruct((t, d), bf16),
                   jax.ShapeDtypeStruct((t, d), bf16), jax.ShapeDtypeStruct((t, d), bf16), jax.ShapeDtypeStruct((t, D_CONV), f32),
                   jax.ShapeDtypeStruct((t, D_SG), f32), jax.ShapeDtypeStruct((t, hw), bf16), jax.ShapeDtypeStruct((1, d), f32)),
        compiler_params=_cparams(1),
    )(*deps, dx1, om, s, um, o, gates, conv_out_w, sg_out_w, wo_p, w_out, g2)


def _attn_bwd(q, k, v, o, lse, do):
    t = q.shape[0]
    tq = min(ATT_BLOCK, t)
    hw = N_HEADS * HEAD_PAD

    def body(q_ref, k_ref, v_ref, o_ref, lse_ref, do_ref, dq_ref, dk_ref, dv_ref):
        qi = pl.program_id(1)

        @pl.when(qi == 0)
        def _():
            dk_ref[...] = jnp.zeros_like(dk_ref)
            dv_ref[...] = jnp.zeros_like(dv_ref)

        def keys(lo, hi, hs, qv, dov, lse, delta, diagonal):
            kj, vj = k_ref[lo:hi, hs], v_ref[lo:hi, hs]
            p = jnp.exp(_dot_nt(qv, kj) * ATT_SCALE - lse)
            if diagonal:
                p = jnp.where(_diag_mask(tq), p, 0.0)
            ds = (p * (_dot_nt(dov, vj) - delta) * ATT_SCALE).astype(bf16)
            dk_ref[lo:hi, hs] += _dot_tn(ds, qv)
            dv_ref[lo:hi, hs] += _dot_tn(p.astype(bf16), dov)
            return _dot(ds, kj)

        for i in range(t // tq):
            @pl.when(qi == i)
            def _(i=i):
                for hh in range(ATT_HEADS):
                    hs = slice(hh * HEAD_PAD, (hh + 1) * HEAD_PAD)
                    qv, dov, lse = q_ref[:, hs], do_ref[:, hs], lse_ref[hh]
                    delta = jnp.sum(dov.astype(f32) * o_ref[:, hs].astype(f32), axis=-1, keepdims=True)
                    dq = keys(i * tq, (i + 1) * tq, hs, qv, dov, lse, delta, True)
                    if i > 0:
                        dq = dq + keys(0, i * tq, hs, qv, dov, lse, delta, False)
                    dq_ref[:, hs] = dq

    blk_q = pl.BlockSpec((tq, ATT_HEADS * HEAD_PAD), lambda h, i: (i, h))
    blk_kv = pl.BlockSpec((t, ATT_HEADS * HEAD_PAD), lambda h, i: (0, h))
    return pl.pallas_call(
        body, name="attn_bwd", grid=(N_HEADS // ATT_HEADS, t // tq),
        in_specs=[blk_q, blk_kv, blk_kv, blk_q, pl.BlockSpec((ATT_HEADS, tq, 1), lambda h, i: (h, i, 0)), blk_q],
        out_specs=(blk_q, blk_kv, blk_kv),
        out_shape=(jax.ShapeDtypeStruct((t, hw), f32), jax.ShapeDtypeStruct((t, hw), f32), jax.ShapeDtypeStruct((t, hw), f32)),
        compiler_params=_cparams(2),
    )(q, k, v, o, lse, do)


def _mla_proj_bwd(l, dq, dk, dv, qlat, kvlat, cos_t, sin_t, gq, wuq_p, gkv, wk_p, wv_p):
    t = qlat.shape[0]
    tb = _tb(t)
    hw = N_HEADS * HEAD_PAD

    def body(dq_ref, dk_ref, dv_ref, ql_ref, kvl_ref, c_ref, s_ref, gq_ref, wq_ref, gkv_ref, wk_ref, wv_ref,
             dqb_ref, dkb_ref, dvb_ref, dql_ref, dkvl_ref, dkr_ref, dgq_ref, dgkv_ref):
        @pl.when(pl.program_id(0) == 0)
        def _():
            dgq_ref[...] = jnp.zeros_like(dgq_ref)
            dgkv_ref[...] = jnp.zeros_like(dgkv_ref)

        cos_b, sin_b = c_ref[...], s_ref[...]
        for h in range(N_HEADS):
            hs = slice(h * HEAD_PAD, (h + 1) * HEAD_PAD)
            dqh = dq_ref[:, hs]
            dqb_ref[:, hs] = (dqh * cos_b + _swap_rope_halves(dqh * sin_b)).astype(bf16)
        dqn = _dot_nt(dqb_ref[...], wq_ref[...])
        dql, dgq = _rms_bwd(ql_ref[...], gq_ref[l:l + 1, :], dqn)
        dgq_ref[...] += dgq
        dql_ref[...] = dql.astype(bf16)
        dkv_full = dk_ref[...]
        dkb = dkv_full.astype(bf16)
        dkb_ref[...] = dkb
        dkpe = dkv_full[:, 0:HEAD_PAD]
        for h in range(1, N_HEADS):
            dkpe = dkpe + dkv_full[:, h * HEAD_PAD:(h + 1) * HEAD_PAD]
        dkr_ref[...] = (dkpe * cos_b + _swap_rope_halves(dkpe * sin_b)).astype(bf16)
        dvb = dv_ref[...].astype(bf16)
        dvb_ref[...] = dvb
        dkvn = _dot_nt(dkb, wk_ref[...]) + _dot_nt(dvb, wv_ref[...])
        dkvl, dgkv = _rms_bwd(kvl_ref[...], gkv_ref[l:l + 1, :], dkvn)
        dgkv_ref[...] += dgkv
        dkvl_ref[...] = dkvl.astype(bf16)

    return pl.pallas_call(
        body, name="mla_proj_bwd", grid=(t // tb,),
        in_specs=[_rows(tb, hw), _rows(tb, hw), _rows(tb, hw), _rows(tb, Q_LORA), _rows(tb, KV_LORA), _rows(tb, HEAD_PAD),
                  _rows(tb, HEAD_PAD), _const(gq.shape), _const((Q_LORA, hw)), _const(gkv.shape), _const((KV_LORA, hw)),
                  _const((KV_LORA, hw))],
        out_specs=(_rows(tb, hw), _rows(tb, hw), _rows(tb, hw), _rows(tb, Q_LORA), _rows(tb, KV_LORA), _rows(tb, HEAD_PAD),
                   _acc((1, Q_LORA)), _acc((1, KV_LORA))),
        out_shape=(jax.ShapeDtypeStruct((t, hw), bf16), jax.ShapeDtypeStruct((t, hw), bf16), jax.ShapeDtypeStruct((t, hw), bf16),
                   jax.ShapeDtypeStruct((t, Q_LORA), bf16), jax.ShapeDtypeStruct((t, KV_LORA), bf16),
                   jax.ShapeDtypeStruct((t, HEAD_PAD), bf16), jax.ShapeDtypeStruct((1, Q_LORA), f32),
                   jax.ShapeDtypeStruct((1, KV_LORA), f32)),
        compiler_params=_cparams(1),
    )(dq, dk, dv, qlat, kvlat, cos_t, sin_t, gq, wuq_p, gkv, wk_p, wv_p)


def _sgu_bwd(l, b_in, dum, ln_g, ln_b, sg_w, bexp):
    t = b_in.shape[0]
    tb = _tb(t)
    gw = D_SG // SG_GROUPS

    def body(b_ref, dum_ref, g_ref, be_ref, w_ref, bexp_ref, db_ref, dw_ref, dsgb_ref, dlg_ref, dlb_ref, dvn_s):
        @pl.when(pl.program_id(0) == 0)
        def _():
            dw_ref[...] = jnp.zeros_like(dw_ref)
            dsgb_ref[...] = jnp.zeros_like(dsgb_ref)
            dlg_ref[...] = jnp.zeros_like(dlg_ref)
            dlb_ref[...] = jnp.zeros_like(dlb_ref)

        gl, dgl = _gelu_and_grad(b_ref[...])
        u = gl[:, :D_SG]
        vh, rstd = _ln_stats(gl[:, D_SG:])
        ln_gain = g_ref[l:l + 1, :]
        vn = (vh * ln_gain + be_ref[l:l + 1, :]).astype(bf16)
        dumv = dum_ref[...]
        tri = _tril_mask()
        ones = jnp.ones((FFN_HALO, gw), f32)
        for g in range(SG_GROUPS):
            wg = jnp.where(tri, w_ref[l, g], 0.0).astype(bf16)
            cs = slice(g * gw, (g + 1) * gw)
            for r0 in range(0, tb, SG_CHUNK):
                rs = slice(r0, r0 + SG_CHUNK)
                vblk = vn[rs, cs]
                mixed = _dot(wg, vblk) + bexp_ref[:, cs]
                db_ref[rs, cs] = (dumv[rs, cs] * mixed * dgl[rs, cs]).astype(bf16)
                dmix = dumv[rs, cs] * u[rs, cs]
                dmb = dmix.astype(bf16)
                dw_ref[g] += jnp.where(tri, _dot_nt(dmb, vblk), 0.0)
                rowsum = lax.dot_general(ones, dmix, (((1,), (1,)), ((), ())), preferred_element_type=f32,
                                         precision=lax.Precision.HIGHEST)
                dsgb_ref[g:g + 1, :] += rowsum[0:1, :]
                dvn_s[rs, cs] = _dot_tn(wg, dmb)
        dvn = dvn_s[...]
        dlg_ref[...] += _colsum(dvn * vh)
        dlb_ref[...] += _colsum(dvn)
        db_ref[:, D_SG:] = (_ln_bwd(vh, rstd, ln_gain, dvn) * dgl[:, D_SG:]).astype(bf16)

    return pl.pallas_call(
        body, name="sgu_bwd", grid=(t // tb,),
        in_specs=[_rows(tb, 2 * D_SG), _rows(tb, D_SG), _const(ln_g.shape), _const(ln_b.shape), _const(sg_w.shape),
                  _const((SG_CHUNK, D_SG))],
        out_specs=(_rows(tb, 2 * D_SG), _acc((SG_GROUPS, SG_CHUNK, SG_CHUNK)), _acc((FFN_HALO, SG_CHUNK)), _acc((1, D_SG)),
                   _acc((1, D_SG))),
        out_shape=(jax.ShapeDtypeStruct((t, 2 * D_SG), bf16), jax.ShapeDtypeStruct((SG_GROUPS, SG_CHUNK, SG_CHUNK), f32),
                   jax.ShapeDtypeStruct((FFN_HALO, SG_CHUNK), f32), jax.ShapeDtypeStruct((1, D_SG), f32),
                   jax.ShapeDtypeStruct((1, D_SG), f32)),
        scratch_shapes=[pltpu.VMEM((tb, D_SG), f32)],
        compiler_params=_cparams(1),
    )(b_in, dum, ln_g, ln_b, sg_w, bexp)


def _conv_bwd(l, a_in, z1, ds, dw_w, ln_g, ln_b):
    t = a_in.shape[0]
    tb = _tb(t)
    nb = t // tb
    per_halo = tb // CONV_HALO

    def body(a_ref, ap_ref, z1_ref, ds_ref, w_ref, g_ref, be_ref, da_ref, ddww_ref, ddwb_ref, dlg_ref, dlb_ref, win, dzb):
        i = pl.program_id(0)
        b = nb - 1 - i

        @pl.when(i == 0)
        def _():
            ddww_ref[...] = jnp.zeros_like(ddww_ref)
            ddwb_ref[...] = jnp.zeros_like(ddwb_ref)
            dlg_ref[...] = jnp.zeros_like(dlg_ref)
            dlb_ref[...] = jnp.zeros_like(dlb_ref)
            dzb[tb:tb + CONV_HALO, :] = jnp.zeros((CONV_HALO, D_CONV), f32)

        a = a_ref[...]
        val = a[:, :D_CONV]
        sg = jax.nn.sigmoid(a[:, D_CONV:])
        ap = ap_ref[...]
        win[0:CONV_HALO, :] = jnp.where(b > 0, ap[:, :D_CONV] * jax.nn.sigmoid(ap[:, D_CONV:]), 0.0)
        win[CONV_HALO:CONV_HALO + tb, :] = val * sg
        zh, rstd = _ln_stats(z1_ref[...])
        ln_gain = g_ref[l:l + 1, :]
        zl = zh * ln_gain + be_ref[l:l + 1, :]
        sgl = jax.nn.sigmoid(zl)
        dzl = ds_ref[...] * (sgl * (1.0 + zl * (1.0 - sgl)))
        dlg_ref[...] += _colsum(dzl * zh)
        dlb_ref[...] += _colsum(dzl)
        dz1 = _ln_bwd(zh, rstd, ln_gain, dzl)
        dzb[0:tb, :] = dz1
        ddwb_ref[...] += _colsum(dz1)
        dgate_f = val * sg * (1.0 - sg)
        for c0 in range(0, D_CONV, LANES):
            cs = slice(c0, c0 + LANES)
            for r0 in range(0, tb, LANES):
                d1 = dzb[r0:r0 + LANES, cs]
                acc = jnp.zeros((LANES, LANES), f32)
                for j in range(CONV_K):
                    ddww_ref[j:j + 1, cs] += _colsum(d1 * win[pl.ds(r0 + CONV_HALO - CONV_K + 1 + j, LANES), cs])
                    acc = acc + w_ref[j:j + 1, cs] * dzb[pl.ds(r0 + CONV_K - 1 - j, LANES), cs]
                da_ref[r0:r0 + LANES, cs] = (acc * sg[r0:r0 + LANES, cs]).astype(bf16)
                da_ref[r0:r0 + LANES, c0 + D_CONV:c0 + D_CONV + LANES] = (acc * dgate_f[r0:r0 + LANES, cs]).astype(bf16)
        dzb[tb:tb + CONV_HALO, :] = dzb[0:CONV_HALO, :]

    return pl.pallas_call(
        body, name="conv_bwd", grid=(nb,),
        in_specs=[_rows_rev(tb, 2 * D_CONV, nb),
                  pl.BlockSpec((CONV_HALO, 2 * D_CONV), lambda i: (jnp.maximum((nb - 1 - i) * per_halo - 1, 0), 0)),
                  _rows_rev(tb, D_CONV, nb), _rows_rev(tb, D_CONV, nb), _const((CONV_HALO, D_CONV)), _const(ln_g.shape),
                  _const(ln_b.shape)],
        out_specs=(_rows_rev(tb, 2 * D_CONV, nb), _acc((CONV_HALO, D_CONV)), _acc((1, D_CONV)), _acc((1, D_CONV)), _acc((1, D_CONV))),
        out_shape=(jax.ShapeDtypeStruct((t, 2 * D_CONV), bf16), jax.ShapeDtypeStruct((CONV_HALO, D_CONV), f32),
                   jax.ShapeDtypeStruct((1, D_CONV), f32), jax.ShapeDtypeStruct((1, D_CONV), f32), jax.ShapeDtypeStruct((1, D_CONV), f32)),
        scratch_shapes=[pltpu.VMEM((tb + CONV_HALO, D_CONV), f32), pltpu.VMEM((tb + CONV_HALO, D_CONV), f32)],
        compiler_params=_cparams(1),
    )(a_in, a_in, z1, ds, dw_w, ln_g, ln_b)


def _mix_in_bwd(l, x, g1, dxres, dsegs, w_in_p, deps=()):
    t, d = x.shape
    tb = _tb(t)

    def body(*refs):
        x_ref, g_ref, dr_ref = refs[len(deps):len(deps) + 3]
        rest = refs[len(deps) + 3:]
        dseg_refs, w_ref, dx_ref, dg_ref = rest[:len(SEGS)], rest[len(SEGS)], rest[len(SEGS) + 1], rest[len(SEGS) + 2]

        @pl.when(pl.program_id(0) == 0)
        def _():
            dg_ref[...] = jnp.zeros_like(dg_ref)

        dh = jnp.zeros((tb, d), f32)
        for (off, wd), ds_ref in zip(SEGS, dseg_refs):
            dh = dh + _dot_nt(ds_ref[...], w_ref[:, off:off + wd])
        dxn, dg = _rms_bwd(x_ref[...], g_ref[l:l + 1, :], dh)
        dg_ref[...] += dg
        dx_ref[...] = dr_ref[...] + dxn

    return pl.pallas_call(
        body, name="mix_in_bwd", grid=(t // tb,),
        in_specs=([_ANY] * len(deps) + [_rows(tb, d), _const(g1.shape), _rows(tb, d)] + [_rows(tb, wd) for _, wd in SEGS]
                  + [_const((d, D_IN_PAD))]),
        out_specs=(_rows(tb, d), _acc((1, d))),
        out_shape=(jax.ShapeDtypeStruct((t, d), f32), jax.ShapeDtypeStruct((1, d), f32)),
        compiler_params=_cparams(1),
    )(*deps, x, g1, dxres, *dsegs, w_in_p)


def _pick_block(n, cap=512):
    for b in (cap, 384, 256, 128):
        if b <= cap and n % b == 0:
            return b
    return n


def _wgrad(a, b, name, out_dtype=f32):
    t, kdim = a.shape
    n = b.shape[1]
    bk, bn = _pick_block(kdim), _pick_block(n, cap=1024)

    def body(a_ref, b_ref, o_ref):
        o_ref[...] = _dot_tn(a_ref[...], b_ref[...]).astype(out_dtype)

    return pl.pallas_call(
        body, name=name, grid=(kdim // bk, n // bn),
        in_specs=[pl.BlockSpec((t, bk), lambda i, j: (0, i)), pl.BlockSpec((t, bn), lambda i, j: (0, j))],
        out_specs=pl.BlockSpec((bk, bn), lambda i, j: (i, j)),
        out_shape=jax.ShapeDtypeStruct((kdim, n), out_dtype),
        compiler_params=_cparams(2),
    )(a, b)


def _wgrad_multi(a, bs, name, deps=()):
    t, kdim = a.shape
    bk = _pick_block(kdim)
    nb = len(bs)

    def body(*refs):
        a_ref, refs = refs[len(deps)], refs[len(deps) + 1:]
        at = a_ref[...].T
        for b_ref, o_ref in zip(refs[:nb], refs[nb:]):
            o_ref[...] = _dot(at, b_ref[...])

    return pl.pallas_call(
        body, name=name, grid=(kdim // bk,),
        in_specs=[_ANY] * len(deps) + [pl.BlockSpec((t, bk), lambda i: (0, i))] + [_const(b.shape) for b in bs],
        out_specs=tuple(pl.BlockSpec((bk, b.shape[1]), lambda i: (i, 0)) for b in bs),
        out_shape=tuple(jax.ShapeDtypeStruct((kdim, b.shape[1]), f32) for b in bs),
        compiler_params=_cparams(1),
    )(*deps, a, *bs)


_BC1 = 1.0 - ADAM_B1 ** ADAM_STEP
_BC2 = 1.0 - ADAM_B2 ** ADAM_STEP


def _adam_math(g, w, m, v):
    nm = ADAM_B1 * m + (1.0 - ADAM_B1) * g
    nv = ADAM_B2 * v + (1.0 - ADAM_B2) * (g * g)
    delta = -ADAM_LR * ((nm / _BC1) / (jnp.sqrt(nv / _BC2) + ADAM_EPS) + ADAM_WD * w)
    return delta, nm, nv


def _slot_sum(r_ref, index=()):
    g = r_ref[(0,) + index].astype(f32)
    for s in range(1, N_DEV):
        g = g + r_ref[(s,) + index].astype(f32)
    return g


def _adamw_shard(l, recv, w, m, v, prev, name, deps=()):
    _, k, ns = recv.shape
    rb = next((c for c in (256, 192, 176, 128) if k % c == 0), k)
    blk = pl.BlockSpec((None, rb, ns), lambda i: (l, i, 0))

    def body(r_ref, w_ref, m_ref, v_ref, *rest):
        g_ref, d_ref, nm_ref, nv_ref = rest[-4:]
        g = _slot_sum(r_ref)
        g_ref[...] = g
        d_ref[...], nm_ref[...], nv_ref[...] = _adam_math(g, w_ref[...], m_ref[...], v_ref[...])

    out = jax.ShapeDtypeStruct(w.shape, f32)
    n_prev = 0 if prev is None else 4
    return pl.pallas_call(
        body, name=name, grid=(k // rb,),
        in_specs=[pl.BlockSpec((N_DEV, rb, ns), lambda i: (0, i, 0)), blk, blk, blk] + [_ANY] * (n_prev + len(deps)),
        out_specs=(blk, blk, blk, blk), out_shape=(out, out, out, out),
        input_output_aliases={4 + j: j for j in range(n_prev)},
        compiler_params=_cparams(1),
    )(recv, w, m, v, *(prev or ()), *deps)


def _adamw_rep(recvs, ws, ms, vs):
    depth = len(recvs)
    nt = len(REP_VECS)

    def body(*refs):
        r_refs = refs[:depth]
        w_refs, m_refs, v_refs = (refs[depth + i * nt:depth + (i + 1) * nt] for i in range(3))
        outs = refs[depth + 3 * nt:]
        for ti, (_, row, width, nrows) in enumerate(REP_VECS):
            for l in range(depth):
                g = r_refs[l][0, row:row + nrows, 0:width]
                for s in range(1, N_DEV):
                    g = g + r_refs[l][s, row:row + nrows, 0:width]
                pick = (lambda ref: ref[l]) if nrows > 1 else (lambda ref: ref[l:l + 1, :])
                delta, nm, nv = _adam_math(g, pick(w_refs[ti]), pick(m_refs[ti]), pick(v_refs[ti]))
                for o_ref, val in zip(outs[4 * ti:4 * ti + 4], (g, delta, nm, nv)):
                    if nrows > 1:
                        o_ref[l] = val
                    else:
                        o_ref[l:l + 1, :] = val

    ins = tuple(recvs) + tuple(ws) + tuple(ms) + tuple(vs)
    out_shape = tuple(jax.ShapeDtypeStruct(w.shape, f32) for w in ws for _ in range(4))
    return pl.pallas_call(
        body, name="adamw_rep", grid=(1,),
        in_specs=[_whole(a) for a in ins], out_specs=tuple(_whole(o) for o in out_shape), out_shape=out_shape,
        compiler_params=_cparams(1),
    )(*ins)


def _adamw_sg_w(recvs, w, m, v):
    depth = len(recvs)

    def body(*refs):
        r_refs = refs[:depth]
        w_ref, m_ref, v_ref = refs[depth:depth + 3]
        outs = refs[depth + 3:]
        for l in range(depth):
            for gi in range(SG_GROUPS):
                g = _slot_sum(r_refs[l], (gi,))
                delta, nm, nv = _adam_math(g, w_ref[l, gi], m_ref[l, gi], v_ref[l, gi])
                for o_ref, val in zip(outs, (g, delta, nm, nv)):
                    o_ref[l, gi] = val

    ins = tuple(recvs) + (w, m, v)
    out = jax.ShapeDtypeStruct(w.shape, f32)
    return pl.pallas_call(
        body, name="adamw_sg_w", grid=(1,),
        in_specs=[_whole(a) for a in ins], out_specs=tuple(_whole(out) for _ in range(4)), out_shape=(out,) * 4,
        compiler_params=_cparams(1),
    )(*ins)


def _rope_tables(positions):
    t = positions.shape[0]
    inv = 10000.0 ** (-jnp.arange(0, QK_ROPE, 2, dtype=f32) / QK_ROPE)
    ang = positions.astype(f32)[:, None] * inv
    cos, sin = jnp.cos(ang), jnp.sin(ang)
    tail = jnp.zeros((t, HEAD_PAD - KR_LO - QK_ROPE), f32)
    cos_t = jnp.concatenate([jnp.ones((t, KR_LO), f32), cos, cos, tail], axis=1)
    sin_t = jnp.concatenate([jnp.zeros((t, KR_LO), f32), -sin, sin, tail], axis=1)
    return cos_t, sin_t


def _bias_over_channels(sg_b_l):
    return jnp.broadcast_to(sg_b_l.T[:, :, None], (SG_CHUNK, SG_GROUPS, D_SG // SG_GROUPS)).reshape(SG_CHUNK, D_SG)


def _mixer_weights(gathered, rep, l):
    conv_out_w, sg_out_w, wuq_p, wk_p, wv_p, wo_p, conv_dw_w, ffn_dw_w = _asm_small(
        gathered["conv_out_w"], gathered["sg_out_w"], gathered["mla_w_uq"], gathered["mla_w_ukv"], gathered["mla_w_o"],
        gathered["conv_dw_w"], gathered["ffn_dw_w"])
    g_out = gathered["w_out"]
    w = dict(rep)
    w.update(
        l=l, w_in_p=_asm_w_in(gathered["w_in"]),
        conv_out_w=conv_out_w, sg_out_w=sg_out_w, wuq_p=wuq_p, wk_p=wk_p, wv_p=wv_p, wo_p=wo_p, conv_dw_w_p=conv_dw_w, ffn_dw_w_p=ffn_dw_w,
        w_out=g_out.reshape(g_out.shape[0] * g_out.shape[1], g_out.shape[2]),
        bexp=_bias_over_channels(rep["sg_b"][l]))
    return w


def _ffn_weights(gathered):
    stack = lambda g: g.reshape(g.shape[0] * g.shape[1], g.shape[2])
    return dict(w_up=stack(gathered["ffn_w_up"]), w_down=stack(gathered["ffn_w_down"]))


def _mixer_fwd(x, w, cos_t, sin_t):
    l = w["l"]
    h, a_in, b_in, qlat, kvlat, krope, gates = _mix_in_fwd(l, x, w["mix_pre_g"], w["w_in_p"])
    s, z1 = _conv_fwd(l, a_in, w["conv_dw_w_p"], w["conv_dw_b"], w["conv_ln_g"], w["conv_ln_b"])
    um = _sgu_fwd(l, b_in, w["sg_ln_g"], w["sg_ln_b"], w["sg_w"], w["bexp"])
    q, k, v, qn, kvn = _mla_proj_fwd(l, qlat, kvlat, krope, cos_t, sin_t, w["mla_q_norm_g"], w["wuq_p"], w["mla_kv_norm_g"],
                                     w["wk_p"], w["wv_p"])
    o, lse = _attn_fwd(q, k, v)
    x1, merged, om = _merge_out_fwd(l, x, s, um, o, gates, w["conv_out_w"], w["sg_out_w"], w["wo_p"], w["w_out"], w["mix_post_g"])
    saved = dict(x=x, h=h, a_in=a_in, b_in=b_in, qlat=qlat, kvlat=kvlat, gates=gates, s=s, z1=z1, um=um, q=q, k=k, v=v, qn=qn,
                 kvn=kvn, o=o, lse=lse, x1=x1, merged=merged, om=om)
    return x1, saved


def _ffn_layer_fwd(x1, w):
    x2, h2, z, zc, act, f = _ffn_fwd(w["l"], x1, w["ffn_pre_g"], w["w_up"], w["ffn_dw_w_p"], w["ffn_dw_b"], w["w_down"], w["ffn_post_g"])
    return x2, dict(h2=h2, z=z, zc=zc, act=act, f=f)


def _ffn_layer_bwd(dx2, sv, w, deps=()):
    l = w["l"]
    vec = {}
    dx1, df, dz, vec["ffn_post_g"], vec["ffn_pre_g"], vec["ffn_dw_b"], vec["d_fdw"] = _ffn_bwd(
        l, dx2, sv["x1"], sv["f"], sv["z"], sv["zc"], w["w_up"], w["ffn_dw_w_p"], w["w_down"], w["ffn_pre_g"], w["ffn_post_g"],
        deps=deps)
    d_down = _wgrad(sv["act"], df, "wgrad_ffn_down", bf16)
    d_up_t = _wgrad(dz, sv["h2"], "wgrad_ffn_up", bf16)
    unstack = lambda g: g.reshape(N_DEV, g.shape[0] // N_DEV, g.shape[1])
    send = dict(ffn_w_up=unstack(d_up_t), ffn_w_down=unstack(d_down))
    return dx1, send, vec


def _mixer_bwd(dx1, sv, w, cos_t, sin_t, shard_cols, vec, start, deps=()):
    l = w["l"]
    d_fdw = vec.pop("d_fdw")
    dom, dgates, dya, dyb, dyc, ds, dum, do, vec["mix_post_g"] = _merge_out_bwd(
        l, dx1, sv["om"], sv["s"], sv["um"], sv["o"], sv["gates"], w["conv_out_w"], w["sg_out_w"], w["wo_p"], w["w_out"], w["mix_post_g"],
        deps=deps)
    d_out = _wgrad(sv["merged"], dom, "wgrad_w_out", bf16)
    d_co = _wgrad(sv["s"], dya, "wgrad_conv_out")
    d_so = _wgrad(sv["um"], dyb, "wgrad_sg_out")
    d_wo = _wgrad(sv["o"], dyc, "wgrad_w_o")

    dq, dk, dv = _attn_bwd(sv["q"], sv["k"], sv["v"], sv["o"], sv["lse"], do)
    dqb, dkb, dvb, dqlat, dkvlat, dkrope, vec["mla_q_norm_g"], vec["mla_kv_norm_g"] = _mla_proj_bwd(
        l, dq, dk, dv, sv["qlat"], sv["kvlat"], cos_t, sin_t, w["mla_q_norm_g"], w["wuq_p"], w["mla_kv_norm_g"], w["wk_p"], w["wv_p"])
    d_uq = _wgrad(sv["qn"], dqb, "wgrad_w_uq")
    d_uk, d_uv = _wgrad_multi(sv["kvn"], [dkb, dvb], "wgrad_w_ukv")

    db_in, dsg_w, dsgb, vec["sg_ln_g"], vec["sg_ln_b"] = _sgu_bwd(l, sv["b_in"], dum, w["sg_ln_g"], w["sg_ln_b"], w["sg_w"], w["bexp"])
    da_in, d_cdw, vec["conv_dw_b"], vec["conv_ln_g"], vec["conv_ln_b"] = _conv_bwd(
        l, sv["a_in"], sv["z1"], ds, w["conv_dw_w_p"], w["conv_ln_g"], w["conv_ln_b"])

    send = {}
    (send["conv_out_w"], send["sg_out_w"], send["mla_w_uq"], send["mla_w_ukv"], send["mla_w_o"], send["conv_dw_w"],
     send["ffn_dw_w"]) = _dis_small(d_co, d_so, d_uq, d_uk, d_uv, d_wo, d_cdw, d_fdw)
    send["w_out"] = d_out.reshape(N_DEV, d_out.shape[0] // N_DEV, d_out.shape[1])
    token = start(MIX_SMALL, send)

    dsegs = (da_in, db_in, dqlat, dkvlat, dkrope, dgates)
    d_in_segs = list(_wgrad_multi(sv["h"], dsegs[:5], "wgrad_w_in_abqkr", deps=token)) + [_wgrad(sv["h"], dgates, "wgrad_w_in_g")]
    token = start(("w_in",), dict(w_in=_dis_w_in(d_in_segs, shard_cols["w_in"])))
    dx, vec["mix_pre_g"] = _mix_in_bwd(l, sv["x"], w["mix_pre_g"], dx1, dsegs, w["w_in_p"], deps=token)
    rep_pack = _pack_rep([vec[n] for n, _, _, _ in REP_VECS[:-1]], dsgb)
    return dx, rep_pack, dsg_w


def kernel(x, positions, mix_pre_g, mix_post_g, ffn_pre_g, ffn_post_g, w_in, conv_dw_w, conv_dw_b, conv_ln_g, conv_ln_b, conv_out_w, sg_ln_g, sg_ln_b, sg_w, sg_b, sg_out_w, mla_q_norm_g, mla_w_uq, mla_kv_norm_g, mla_w_ukv, mla_w_o, w_out, ffn_w_up, ffn_dw_w, ffn_dw_b, ffn_w_down, loss_target, m_mix_pre_g, m_mix_post_g, m_ffn_pre_g, m_ffn_post_g, m_w_in, m_conv_dw_w, m_conv_dw_b, m_conv_ln_g, m_conv_ln_b, m_conv_out_w, m_sg_ln_g, m_sg_ln_b, m_sg_w, m_sg_b, m_sg_out_w, m_mla_q_norm_g, m_mla_w_uq, m_mla_kv_norm_g, m_mla_w_ukv, m_mla_w_o, m_w_out, m_ffn_w_up, m_ffn_dw_w, m_ffn_dw_b, m_ffn_w_down, v_mix_pre_g, v_mix_post_g, v_ffn_pre_g, v_ffn_post_g, v_w_in, v_conv_dw_w, v_conv_dw_b, v_conv_ln_g, v_conv_ln_b, v_conv_out_w, v_sg_ln_g, v_sg_ln_b, v_sg_w, v_sg_b, v_sg_out_w, v_mla_q_norm_g, v_mla_w_uq, v_mla_kv_norm_g, v_mla_w_ukv, v_mla_w_o, v_w_out, v_ffn_w_up, v_ffn_dw_w, v_ffn_dw_b, v_ffn_w_down):
    args = (x, positions, mix_pre_g, mix_post_g, ffn_pre_g, ffn_post_g, w_in, conv_dw_w, conv_dw_b, conv_ln_g, conv_ln_b, conv_out_w, sg_ln_g, sg_ln_b, sg_w, sg_b, sg_out_w, mla_q_norm_g, mla_w_uq, mla_kv_norm_g, mla_w_ukv, mla_w_o, w_out, ffn_w_up, ffn_dw_w, ffn_dw_b, ffn_w_down, loss_target, m_mix_pre_g, m_mix_post_g, m_ffn_pre_g, m_ffn_post_g, m_w_in, m_conv_dw_w, m_conv_dw_b, m_conv_ln_g, m_conv_ln_b, m_conv_out_w, m_sg_ln_g, m_sg_ln_b, m_sg_w, m_sg_b, m_sg_out_w, m_mla_q_norm_g, m_mla_w_uq, m_mla_kv_norm_g, m_mla_w_ukv, m_mla_w_o, m_w_out, m_ffn_w_up, m_ffn_dw_w, m_ffn_dw_b, m_ffn_w_down, v_mix_pre_g, v_mix_post_g, v_ffn_pre_g, v_ffn_post_g, v_w_in, v_conv_dw_w, v_conv_dw_b, v_conv_ln_g, v_conv_ln_b, v_conv_out_w, v_sg_ln_g, v_sg_ln_b, v_sg_w, v_sg_b, v_sg_out_w, v_mla_q_norm_g, v_mla_w_uq, v_mla_kv_norm_g, v_mla_w_ukv, v_mla_w_o, v_w_out, v_ffn_w_up, v_ffn_dw_w, v_ffn_dw_b, v_ffn_w_down)
    n_in = len(IN_NAMES)
    a = dict(zip(IN_NAMES, args[:n_in]))
    target = args[n_in]
    n_w = len(WEIGHTS)
    m_in = dict(zip(WEIGHTS, args[n_in + 1:n_in + 1 + n_w]))
    v_in = dict(zip(WEIGHTS, args[n_in + 1 + n_w:n_in + 1 + 2 * n_w]))
    depth = a["mix_pre_g"].shape[0]
    rep = {n: a[n] for n in WEIGHTS if n not in SHARDED}
    shard_cols = {n: a[n].shape[2] for n in SHARDED}

    def view(arr, n):
        return jnp.swapaxes(arr, 1, 2) if n in TRANSPOSED else arr

    w_sh = {n: view(a[n], n) for n in SHARDED}
    m_sh = {n: view(m_in[n], n) for n in SHARDED}
    v_sh = {n: view(v_in[n], n) for n in SHARDED}

    def gather_groups(l):
        return (("mix", MIX_GROUP), ("ffn", FFN_BIG))

    starts, token = {}, ()
    for l in range(depth):
        for tag, group in gather_groups(l):
            wire = [w_sh[n][l] if n in WIRE_F32 else w_sh[n][l].astype(bf16) for n in group]
            starts[l, tag] = _gather_start(wire, "gather_%s_weights_%d" % (tag, l), deps=token)
            token = (starts[l, tag]["token"],)

    cos_t, sin_t = _rope_tables(a["positions"][0])
    xl = a["x"][0]
    ws, saved = [], []
    passing = {}

    def pass_on(l, tag, after):
        passing[l, tag] = _gather_forward(starts[l, tag], after=after)
        return (passing[l, tag]["token"],)

    relaid = tuple(t[n] for n in SHARDED for t in (m_sh, v_sh))
    after = pass_on(0, gather_groups(0)[0][0], token + relaid)
    for l in range(depth):
        groups = gather_groups(l)
        got = dict(zip(groups[0][1], _gather_wait(passing[l, groups[0][0]], after=after)))
        w = _mixer_weights(got, rep, l)
        x1, sv = _mixer_fwd(xl, w, cos_t, sin_t)
        after = (x1,)
        if len(groups) > 1:
            after = pass_on(l, groups[1][0], after)
        if l + 1 < depth:
            after = pass_on(l + 1, gather_groups(l + 1)[0][0], after)
        if len(groups) > 1:
            got = dict(zip(groups[1][1], _gather_wait(passing[l, groups[1][0]], after=after)))
        w.update(_ffn_weights(got))
        xl, sv_ffn = _ffn_layer_fwd(x1, w)
        sv.update(sv_ffn)
        ws.append(w)
        saved.append(sv)
        after = (xl,)
    dx, loss_part = _loss_fwd_bwd(xl, target[0])
    loss = lax.psum(loss_part, AXES)

    outs = {}
    rep_recvs, sgw_recvs = [None] * depth, [None] * depth

    def finish(l, parts, after):
        for names, handle in parts:
            got = _exchange_wait(handle, after=after)
            for n, recv in zip(names, got):
                outs[n] = _adamw_shard(l, recv, w_sh[n], m_sh[n], v_sh[n], outs.get(n), "adamw_" + n)
            after = tuple(outs[n][0] for n in names) or after
        rep_recvs[l], sgw_recvs[l] = got[0], got[1]
        return after

    token, prev, after = (), None, ()
    for l in reversed(range(depth)):
        dx1, send_ffn, vec = _ffn_layer_bwd(dx, saved[l], ws[l], deps=token)
        parts = []

        def start(names, send, extra=(), l=l, parts=parts):
            tag = names[0] if names else "rep"
            handle = _exchange_start([(send[n], "scatter") for n in names] + list(extra), "exchange_%s_grads_%d" % (tag, l))
            parts.append((names, handle))
            return (handle["token"],)

        token = start(FFN_BIG, send_ffn)
        dx, rep_pack, dsg_w = _mixer_bwd(dx1, saved[l], ws[l], cos_t, sin_t, shard_cols, vec, start, deps=token)
        token = start((), {}, extra=[(rep_pack, "gather"), (dsg_w, "gather")])
        if prev is not None:
            after = finish(*prev, after=(dx,) + token)
        prev = (l, parts)
    finish(*prev, after=after)
    vec_names = [n for n, _, _, _ in REP_VECS]
    rep_outs = _adamw_rep(rep_recvs, [a[n] for n in vec_names], [m_in[n] for n in vec_names], [v_in[n] for n in vec_names])
    for i, n in enumerate(vec_names):
        outs[n] = rep_outs[4 * i:4 * i + 4]
    outs["sg_w"] = _adamw_sg_w(sgw_recvs, a["sg_w"], m_in["sg_w"], v_in["sg_w"])
    for n in TRANSPOSED:
        outs[n] = tuple(view(o, n) for o in outs[n])

    grad_w, delta_w, new_m, new_v = ([outs[n][j] for n in WEIGHTS] for j in range(4))
    return (loss, dx[None], *grad_w, *delta_w, *new_m, *new_v)
```

```python
import math

import jax
import jax.numpy as jnp
from jax import lax
from jax.experimental import pallas as pl
from jax.experimental.pallas import tpu as pltpu

f32 = jnp.float32
bf16 = jnp.bfloat16

N_DEV = 8
AXES = ("x", "y", "c")
EPS = 1e-6
D_CONV = 512
CONV_K = 31
CONV_HALO = 32
D_SG = 512
SG_GROUPS = 4
SG_CHUNK = 128
N_HEADS = 8
QK_NOPE = 64
QK_ROPE = 32
V_HEAD = 64
HEAD_PAD = 128
Q_LORA = 384
KV_LORA = 256
D_FF = 2816
FFN_K = 3
FFN_HALO = 8
ATT_SCALE = (QK_NOPE + QK_ROPE) ** -0.5
ATT_BLOCK = 256
ATT_HEADS = 2
NEG = float(jnp.finfo(jnp.float32).min)

ADAM_LR = 0.001
ADAM_B1 = 0.9
ADAM_B2 = 0.999
ADAM_EPS = 1e-08
ADAM_WD = 0.01
ADAM_STEP = 10

LANES = 128
VMEM_MB = 56

REF_CUTS = (0, 1024, 2048, 2432, 2688, 2720, 5792)
SEGS = ((0, 1024), (1024, 1024), (2048, 384), (2432, 256), (2688, 128), (2816, 3072))
D_IN = 5792
D_IN_PAD = 5888
KR_LO = 64
SEG_INNER = (0, 0, 0, 0, KR_LO, 0)

IN_NAMES = ['x', 'positions', 'mix_pre_g', 'mix_post_g', 'ffn_pre_g', 'ffn_post_g', 'w_in', 'conv_dw_w', 'conv_dw_b', 'conv_ln_g', 'conv_ln_b', 'conv_out_w', 'sg_ln_g', 'sg_ln_b', 'sg_w', 'sg_b', 'sg_out_w', 'mla_q_norm_g', 'mla_w_uq', 'mla_kv_norm_g', 'mla_w_ukv', 'mla_w_o', 'w_out', 'ffn_w_up', 'ffn_dw_w', 'ffn_dw_b', 'ffn_w_down']
WEIGHTS = IN_NAMES[2:]
SHARDED = ("w_in", "conv_dw_w", "conv_out_w", "sg_out_w", "mla_w_uq", "mla_w_ukv", "mla_w_o", "w_out", "ffn_w_up", "ffn_dw_w",
           "ffn_w_down")
FFN_BIG = ("ffn_w_up", "ffn_w_down")
MIX_GROUP = tuple(n for n in SHARDED if n not in FFN_BIG)
MIX_SMALL = tuple(n for n in MIX_GROUP if n != "w_in")
TRANSPOSED = ("ffn_w_up",)
WIRE_F32 = ("conv_dw_w", "ffn_dw_w")
REP_VECS = (("mix_pre_g", 0, 1024, 1), ("mix_post_g", 1, 1024, 1), ("ffn_pre_g", 2, 1024, 1), ("ffn_post_g", 3, 1024, 1),
            ("conv_dw_b", 4, 512, 1), ("conv_ln_g", 5, 512, 1), ("conv_ln_b", 6, 512, 1), ("sg_ln_g", 7, 512, 1),
            ("sg_ln_b", 8, 512, 1), ("mla_q_norm_g", 9, 384, 1), ("mla_kv_norm_g", 10, 256, 1), ("ffn_dw_b", 11, 5632, 1),
            ("sg_b", 12, 128, 4))
REP_ROWS = 16
REP_W = 5632


def _cparams(n_axes):
    return pltpu.CompilerParams(dimension_semantics=("arbitrary",) * n_axes, vmem_limit_bytes=VMEM_MB * 2 ** 20)


def _rows(tb, n):
    return pl.BlockSpec((tb, n), lambda i: (i, 0))


def _rows_rev(tb, n, nb):
    return pl.BlockSpec((tb, n), lambda i: (nb - 1 - i, 0))


def _const(shape):
    nd = len(shape)
    return pl.BlockSpec(shape, lambda *_: (0,) * nd, pipeline_mode=pl.Buffered(1))


def _acc(shape):
    nd = len(shape)
    return pl.BlockSpec(shape, lambda *_: (0,) * nd)


def _whole(arr):
    return pl.BlockSpec(arr.shape, lambda *_: (0,) * arr.ndim)


def _dot(a, b):
    return jnp.dot(a, b, preferred_element_type=f32)


def _dot_nt(a, b):
    return lax.dot_general(a, b, (((1,), (1,)), ((), ())), preferred_element_type=f32)


def _dot_tn(a, b):
    return lax.dot_general(a, b, (((0,), (0,)), ((), ())), preferred_element_type=f32)


def _mean(x):
    return jnp.mean(x, axis=-1, keepdims=True)


def _colsum(x):
    return jnp.sum(x, axis=0, keepdims=True)


def _rms_fwd(x, g):
    return x * lax.rsqrt(_mean(x * x) + EPS) * g


def _rms_bwd(x, g, dy):
    r = lax.rsqrt(_mean(x * x) + EPS)
    n = x * r
    dn = dy * g
    return r * (dn - n * _mean(dn * n)), _colsum(dy * n)


def _ln_stats(x):
    mu = _mean(x)
    d = x - mu
    rstd = lax.rsqrt(_mean(d * d) + EPS)
    return d * rstd, rstd


def _ln_bwd(xhat, rstd, g, dy):
    dxh = dy * g
    return rstd * (dxh - _mean(dxh) - xhat * _mean(dxh * xhat))


_GELU_C0 = math.sqrt(2.0 / math.pi)
_GELU_C1 = 0.044715


def _gelu(x):
    t = jnp.tanh(_GELU_C0 * (x + _GELU_C1 * (x * x * x)))
    return 0.5 * x * (1.0 + t)


def _gelu_and_grad(x):
    x2 = x * x
    t = jnp.tanh(_GELU_C0 * (x + _GELU_C1 * (x2 * x)))
    g = 0.5 * x * (1.0 + t)
    dg = 0.5 * (1.0 + t) + 0.5 * x * (1.0 - t * t) * (_GELU_C0 * (1.0 + 3.0 * _GELU_C1 * x2))
    return g, dg


def _swap_rope_halves(x):
    n = x.shape[1]
    half = QK_ROPE // 2
    lane = lax.broadcasted_iota(jnp.int32, x.shape, 1) % HEAD_PAD
    first = (lane >= KR_LO) & (lane < KR_LO + half)
    second = (lane >= KR_LO + half) & (lane < KR_LO + QK_ROPE)
    return jnp.where(first, pltpu.roll(x, n - half, 1), jnp.where(second, pltpu.roll(x, half, 1), 0.0))


def _tb(t):
    return min(256, t)


_HBM = pl.BlockSpec(memory_space=pltpu.HBM)
_SEM = pl.BlockSpec(memory_space=pltpu.SEMAPHORE)
_ANY = pl.BlockSpec(memory_space=pl.ANY)
_EFFECT = pltpu.SideEffectType.DATAFLOW_SIDE_EFFECTING


def _exchange_copies(modes, ins, lands, send_sems, recv_sems, loc_sems):
    x, y, c = lax.axis_index("x"), lax.axis_index("y"), lax.axis_index("c")
    me = 4 * x + 2 * y + c
    copies = []
    for a, mode in enumerate(modes):
        def src(dst_index, a=a, mode=mode):
            return ins[a].at[dst_index] if mode == "scatter" else ins[a]
        copies.append(pltpu.make_async_copy(src(me), lands[a].at[me], loc_sems.at[a]))
        for k in range(1, N_DEV):
            px = 1 - x if (k >> 2) & 1 else x
            py = 1 - y if (k >> 1) & 1 else y
            pc = 1 - c if k & 1 else c
            copies.append(pltpu.make_async_remote_copy(
                src_ref=src(4 * px + 2 * py + pc), dst_ref=lands[a].at[me],
                send_sem=send_sems.at[a * (N_DEV - 1) + k - 1], recv_sem=recv_sems.at[a * (N_DEV - 1) + k - 1],
                device_id=(px, py, pc), device_id_type=pl.DeviceIdType.MESH))
    return copies


def _exchange_start(ops, name, deps=()):
    n = len(ops)
    arrs = [arr for arr, _ in ops]
    modes = [mode for _, mode in ops]
    lands = [lax.empty((N_DEV,) + arr.shape if mode == "gather" else arr.shape, arr.dtype) for arr, mode in ops]

    def body(*refs):
        ins, land_refs = refs[:n], refs[n:2 * n]
        send_sems, recv_sems, loc_sems = refs[2 * n + len(deps):2 * n + len(deps) + 3]
        for cp in _exchange_copies(modes, ins, land_refs, send_sems, recv_sems, loc_sems):
            cp.start()
        refs[-1][...] = jnp.zeros((8, LANES), f32)

    n_rem = n * (N_DEV - 1)
    res = pl.pallas_call(
        body, name=name,
        out_shape=(pltpu.SemaphoreType.DMA((n_rem,)), pltpu.SemaphoreType.DMA((n_rem,)), pltpu.SemaphoreType.DMA((n,)),
                   *[pltpu.HBM(x.shape, x.dtype) for x in arrs + lands], jax.ShapeDtypeStruct((8, LANES), f32)),
        in_specs=[_HBM] * (2 * n) + [_ANY] * len(deps),
        out_specs=(_SEM, _SEM, _SEM, *[_HBM] * (2 * n), pl.BlockSpec(memory_space=pltpu.VMEM)),
        input_output_aliases={i: 3 + i for i in range(2 * n)},
        compiler_params=pltpu.CompilerParams(has_side_effects=_EFFECT),
    )(*[pltpu.with_memory_space_constraint(x, pltpu.HBM) for x in arrs + lands], *deps)
    return dict(modes=modes, sems=res[:3], thru=res[3:3 + 2 * n], token=res[-1], name=name)


def _exchange_wait(handle, after=()):
    modes, thru = handle["modes"], handle["thru"]
    n = len(modes)

    def body(*refs):
        ins, land_refs = refs[:n], refs[n:2 * n]
        send_sems, recv_sems, loc_sems = refs[2 * n:2 * n + 3]
        for cp in _exchange_copies(modes, ins, land_refs, send_sems, recv_sems, loc_sems):
            cp.wait()

    res = pl.pallas_call(
        body, name=handle["name"] + "_wait",
        out_shape=tuple(pltpu.HBM(x.shape, x.dtype) for x in thru),
        in_specs=[_HBM] * (2 * n) + [_SEM] * 3 + [_ANY] * len(after),
        out_specs=tuple([_HBM] * (2 * n)),
        input_output_aliases={i: i for i in range(2 * n)},
        compiler_params=pltpu.CompilerParams(has_side_effects=_EFFECT),
    )(*thru, *handle["sems"], *after)
    return res[n:]


_SAME_CORE_PEERS = ((1, 0), (0, 1), (1, 1))


def _gather_copies_one(ins, lands, send_sems, recv_sems, loc_sems):
    x, y, c = lax.axis_index("x"), lax.axis_index("y"), lax.axis_index("c")
    me = 4 * x + 2 * y + c
    targets = [(x, y, 1 - c)] + [(1 - x if fx else x, 1 - y if fy else y, c) for fx, fy in _SAME_CORE_PEERS]
    copies = []
    for a in range(len(ins)):
        copies.append(pltpu.make_async_copy(ins[a], lands[a].at[me], loc_sems.at[a]))
        for j, target in enumerate(targets):
            copies.append(pltpu.make_async_remote_copy(
                src_ref=ins[a], dst_ref=lands[a].at[me], send_sem=send_sems.at[4 * a + j], recv_sem=recv_sems.at[4 * a + j],
                device_id=target, device_id_type=pl.DeviceIdType.MESH))
    return copies


def _gather_copies_two(lands, send_sems, recv_sems):
    x, y, c = lax.axis_index("x"), lax.axis_index("y"), lax.axis_index("c")
    copies = []
    for a in range(len(lands)):
        for j, (fx, fy) in enumerate(_SAME_CORE_PEERS):
            slot = 4 * (1 - x if fx else x) + 2 * (1 - y if fy else y) + c
            copies.append(pltpu.make_async_remote_copy(
                src_ref=lands[a].at[slot], dst_ref=lands[a].at[slot], send_sem=send_sems.at[3 * a + j],
                recv_sem=recv_sems.at[3 * a + j], device_id=(x, y, 1 - c), device_id_type=pl.DeviceIdType.MESH))
    return copies


def _gather_start(arrs, name, deps=()):
    n = len(arrs)
    lands = [lax.empty((N_DEV,) + arr.shape, arr.dtype) for arr in arrs]

    def body(*refs):
        ins, land_refs = refs[:n], refs[n:2 * n]
        send_sems, recv_sems, loc_sems = refs[2 * n + len(deps):2 * n + len(deps) + 3]
        for cp in _gather_copies_one(ins, land_refs, send_sems, recv_sems, loc_sems):
            cp.start()
        refs[-1][...] = jnp.zeros((8, LANES), f32)

    res = pl.pallas_call(
        body, name=name,
        out_shape=(pltpu.SemaphoreType.DMA((4 * n,)), pltpu.SemaphoreType.DMA((4 * n,)), pltpu.SemaphoreType.DMA((n,)),
                   *[pltpu.HBM(x.shape, x.dtype) for x in arrs + lands], jax.ShapeDtypeStruct((8, LANES), f32)),
        in_specs=[_HBM] * (2 * n) + [_ANY] * len(deps),
        out_specs=(_SEM, _SEM, _SEM, *[_HBM] * (2 * n), pl.BlockSpec(memory_space=pltpu.VMEM)),
        input_output_aliases={i: 3 + i for i in range(2 * n)},
        compiler_params=pltpu.CompilerParams(has_side_effects=_EFFECT),
    )(*[pltpu.with_memory_space_constraint(x, pltpu.HBM) for x in arrs + lands], *deps)
    return dict(n=n, sems=res[:3], thru=res[3:3 + 2 * n], token=res[-1], name=name)


def _gather_start_many(groups, names):
    counts = [len(g) for g in groups]
    arrs = [x for g in groups for x in g]
    n, ng = len(arrs), len(groups)
    lands = [lax.empty((N_DEV,) + x.shape, x.dtype) for x in arrs]

    def body(*refs):
        ins, land_refs, sem_refs = refs[:n], refs[n:2 * n], refs[2 * n:2 * n + 3 * ng]
        off = 0
        for gi, c in enumerate(counts):
            send_sems, recv_sems, loc_sems = sem_refs[3 * gi:3 * gi + 3]
            for cp in _gather_copies_one(ins[off:off + c], land_refs[off:off + c], send_sems, recv_sems, loc_sems):
                cp.start()
            off += c
        refs[-1][...] = jnp.zeros((8, LANES), f32)

    sems = [s for c in counts for s in (pltpu.SemaphoreType.DMA((4 * c,)), pltpu.SemaphoreType.DMA((4 * c,)),
                                        pltpu.SemaphoreType.DMA((c,)))]
    res = pl.pallas_call(
        body, name="gather_weights_start",
        out_shape=(*sems, *[pltpu.HBM(x.shape, x.dtype) for x in arrs + lands], jax.ShapeDtypeStruct((8, LANES), f32)),
        in_specs=[_HBM] * (2 * n),
        out_specs=(*[_SEM] * (3 * ng), *[_HBM] * (2 * n), pl.BlockSpec(memory_space=pltpu.VMEM)),
        input_output_aliases={i: 3 * ng + i for i in range(2 * n)},
        compiler_params=pltpu.CompilerParams(has_side_effects=_EFFECT),
    )(*[pltpu.with_memory_space_constraint(x, pltpu.HBM) for x in arrs + lands])
    handles, off = [], 0
    for gi, (c, name) in enumerate(zip(counts, names)):
        thru = list(res[3 * ng + off:3 * ng + off + c]) + list(res[3 * ng + n + off:3 * ng + n + off + c])
        handles.append(dict(n=c, sems=res[3 * gi:3 * gi + 3], thru=thru, token=res[-1], name=name))
        off += c
    return handles


def _gather_forward(handle, after=()):
    n, thru = handle["n"], handle["thru"]

    def body(*refs):
        ins, land_refs = refs[:n], refs[n:2 * n]
        send_one, recv_one, loc_sems = refs[2 * n:2 * n + 3]
        send_two, recv_two = refs[2 * n + 3 + len(after):2 * n + 5 + len(after)]
        for cp in _gather_copies_one(ins, land_refs, send_one, recv_one, loc_sems):
            cp.wait()
        for cp in _gather_copies_two(land_refs, send_two, recv_two):
            cp.start()
        refs[-1][...] = jnp.zeros((8, LANES), f32)

    res = pl.pallas_call(
        body, name=handle["name"] + "_forward",
        out_shape=(pltpu.SemaphoreType.DMA((3 * n,)), pltpu.SemaphoreType.DMA((3 * n,)),
                   *[pltpu.HBM(x.shape, x.dtype) for x in thru], jax.ShapeDtypeStruct((8, LANES), f32)),
        in_specs=[_HBM] * (2 * n) + [_SEM] * 3 + [_ANY] * len(after),
        out_specs=(_SEM, _SEM, *[_HBM] * (2 * n), pl.BlockSpec(memory_space=pltpu.VMEM)),
        input_output_aliases={i: 2 + i for i in range(2 * n)},
        compiler_params=pltpu.CompilerParams(has_side_effects=_EFFECT),
    )(*thru, *handle["sems"], *after)
    return dict(n=n, sems=res[:2], lands=res[2 + n:2 + 2 * n], token=res[-1], name=handle["name"])


def _gather_wait(handle, after=()):
    n, lands = handle["n"], handle["lands"]

    def body(*refs):
        land_refs = refs[:n]
        send_two, recv_two = refs[n:n + 2]
        for cp in _gather_copies_two(land_refs, send_two, recv_two):
            cp.wait()

    return pl.pallas_call(
        body, name=handle["name"] + "_wait",
        out_shape=tuple(pltpu.HBM(x.shape, x.dtype) for x in lands),
        in_specs=[_HBM] * n + [_SEM] * 2 + [_ANY] * len(after),
        out_specs=tuple([_HBM] * n),
        input_output_aliases={i: i for i in range(n)},
        compiler_params=pltpu.CompilerParams(has_side_effects=_EFFECT),
    )(*lands, *handle["sems"], *after)


def _w_in_pieces(ns):
    out = []
    for e in range(N_DEV):
        lo, hi = ns * e, ns * (e + 1)
        for s in range(len(SEGS)):
            a, b = max(lo, REF_CUTS[s]), min(hi, REF_CUTS[s + 1])
            if a < b:
                inner = SEG_INNER[s] + a - REF_CUTS[s]
                out.append((e, a - lo, b - lo, s, inner, inner + b - a))
    return out


def _asm_w_in(g):
    _, d, ns = g.shape
    rb = 256
    pieces = _w_in_pieces(ns)

    def body(g_ref, o_ref):
        kr = SEGS[4][0]
        o_ref[:, kr:kr + KR_LO] = jnp.zeros((rb, KR_LO), g.dtype)
        o_ref[:, kr + KR_LO + QK_ROPE:kr + HEAD_PAD] = jnp.zeros((rb, HEAD_PAD - KR_LO - QK_ROPE), g.dtype)
        for e, s0, s1, seg, d0, d1 in pieces:
            off = SEGS[seg][0]
            o_ref[:, off + d0:off + d1] = g_ref[e, :, s0:s1]

    return pl.pallas_call(
        body, name="asm_w_in", grid=(d // rb,),
        in_specs=[pl.BlockSpec((N_DEV, rb, ns), lambda i: (0, i, 0))],
        out_specs=_rows(rb, D_IN_PAD), out_shape=jax.ShapeDtypeStruct((d, D_IN_PAD), g.dtype),
        compiler_params=_cparams(1),
    )(g)


def _dis_w_in(dsegs, ns):
    d = dsegs[0].shape[0]
    rb = 256
    pieces = _w_in_pieces(ns)

    def body(*refs):
        seg_refs, o_ref = refs[:len(SEGS)], refs[len(SEGS)]
        for e, s0, s1, seg, d0, d1 in pieces:
            o_ref[e, :, s0:s1] = seg_refs[seg][:, d0:d1].astype(bf16)

    return pl.pallas_call(
        body, name="dis_w_in", grid=(d // rb,),
        in_specs=[_rows(rb, wd) for _, wd in SEGS],
        out_specs=pl.BlockSpec((N_DEV, rb, ns), lambda i: (0, i, 0)),
        out_shape=jax.ShapeDtypeStruct((N_DEV, d, ns), bf16),
        compiler_params=_cparams(1),
    )(*dsegs)


def _asm_small(g_conv_out, g_sg_out, g_uq, g_ukv, g_wo, g_cdw, g_fdw):
    d = g_conv_out.shape[2] * N_DEV
    hw = N_HEADS * HEAD_PAD
    hq = QK_NOPE + QK_ROPE
    ff2 = g_fdw.shape[2] * N_DEV
    cw, fw = g_cdw.shape[2], g_fdw.shape[2]

    def body(co_ref, so_ref, uq_ref, ukv_ref, wo_ref, cdw_ref, fdw_ref, o_co, o_so, o_uq, o_k, o_v, o_wo, o_cdw, o_fdw):
        o_cdw[CONV_K:CONV_HALO, :] = jnp.zeros((CONV_HALO - CONV_K, D_CONV), f32)
        o_fdw[FFN_K:FFN_HALO, :] = jnp.zeros((FFN_HALO - FFN_K, ff2), f32)
        for e in range(N_DEV):
            cs = e * HEAD_PAD
            o_co[:, cs:cs + HEAD_PAD] = co_ref[e]
            o_so[:, cs:cs + HEAD_PAD] = so_ref[e]
            o_uq[:, cs:cs + hq] = uq_ref[e]
            o_uq[:, cs + hq:cs + HEAD_PAD] = jnp.zeros((Q_LORA, HEAD_PAD - hq), bf16)
            o_k[:, cs:cs + QK_NOPE] = ukv_ref[e, :, 0:QK_NOPE]
            o_k[:, cs + QK_NOPE:cs + HEAD_PAD] = jnp.zeros((KV_LORA, HEAD_PAD - QK_NOPE), bf16)
            o_v[:, cs:cs + V_HEAD] = ukv_ref[e, :, QK_NOPE:QK_NOPE + V_HEAD]
            o_v[:, cs + V_HEAD:cs + HEAD_PAD] = jnp.zeros((KV_LORA, HEAD_PAD - V_HEAD), bf16)
            for h in range(N_HEADS):
                o_wo[h * HEAD_PAD:h * HEAD_PAD + V_HEAD, cs:cs + HEAD_PAD] = wo_ref[e, h * V_HEAD:(h + 1) * V_HEAD, :]
                o_wo[h * HEAD_PAD + V_HEAD:(h + 1) * HEAD_PAD, cs:cs + HEAD_PAD] = jnp.zeros((HEAD_PAD - V_HEAD, HEAD_PAD), bf16)
            o_cdw[0:CONV_K, e * cw:(e + 1) * cw] = cdw_ref[e]
            o_fdw[0:FFN_K, e * fw:(e + 1) * fw] = fdw_ref[e]

    ins = (g_conv_out, g_sg_out, g_uq, g_ukv, g_wo, g_cdw, g_fdw)
    out_shape = (jax.ShapeDtypeStruct((D_CONV, d), bf16), jax.ShapeDtypeStruct((D_SG, d), bf16), jax.ShapeDtypeStruct((Q_LORA, hw), bf16),
                 jax.ShapeDtypeStruct((KV_LORA, hw), bf16), jax.ShapeDtypeStruct((KV_LORA, hw), bf16), jax.ShapeDtypeStruct((hw, d), bf16),
                 jax.ShapeDtypeStruct((CONV_HALO, D_CONV), f32), jax.ShapeDtypeStruct((FFN_HALO, ff2), f32))
    return pl.pallas_call(
        body, name="asm_small", grid=(1,),
        in_specs=[_whole(a) for a in ins], out_specs=tuple(_whole(o) for o in out_shape), out_shape=out_shape,
        compiler_params=_cparams(1),
    )(*ins)


def _dis_small(d_co, d_so, d_uq, d_k, d_v, d_wo, d_cdw, d_fdw):
    d = d_co.shape[1]
    hq = QK_NOPE + QK_ROPE
    cw, fw = D_CONV // N_DEV, d_fdw.shape[1] // N_DEV

    def body(co_ref, so_ref, uq_ref, k_ref, v_ref, wo_ref, cdw_ref, fdw_ref, o_co, o_so, o_uq, o_ukv, o_wo, o_cdw, o_fdw):
        for e in range(N_DEV):
            cs = e * HEAD_PAD
            o_co[e] = co_ref[:, cs:cs + HEAD_PAD].astype(bf16)
            o_so[e] = so_ref[:, cs:cs + HEAD_PAD].astype(bf16)
            o_uq[e] = uq_ref[:, cs:cs + hq].astype(bf16)
            o_ukv[e, :, 0:QK_NOPE] = k_ref[:, cs:cs + QK_NOPE].astype(bf16)
            o_ukv[e, :, QK_NOPE:QK_NOPE + V_HEAD] = v_ref[:, cs:cs + V_HEAD].astype(bf16)
            for h in range(N_HEADS):
                o_wo[e, h * V_HEAD:(h + 1) * V_HEAD, :] = wo_ref[h * HEAD_PAD:h * HEAD_PAD + V_HEAD, cs:cs + HEAD_PAD].astype(bf16)
            o_cdw[e] = cdw_ref[0:CONV_K, e * cw:(e + 1) * cw]
            o_fdw[e] = fdw_ref[0:FFN_K, e * fw:(e + 1) * fw]

    ins = (d_co, d_so, d_uq, d_k, d_v, d_wo, d_cdw, d_fdw)
    out_shape = (jax.ShapeDtypeStruct((N_DEV, D_CONV, d // N_DEV), bf16), jax.ShapeDtypeStruct((N_DEV, D_SG, d // N_DEV), bf16),
                 jax.ShapeDtypeStruct((N_DEV, Q_LORA, hq), bf16), jax.ShapeDtypeStruct((N_DEV, KV_LORA, QK_NOPE + V_HEAD), bf16),
                 jax.ShapeDtypeStruct((N_DEV, N_HEADS * V_HEAD, d // N_DEV), bf16), jax.ShapeDtypeStruct((N_DEV, CONV_K, cw), f32),
                 jax.ShapeDtypeStruct((N_DEV, FFN_K, fw), f32))
    return pl.pallas_call(
        body, name="dis_small", grid=(1,),
        in_specs=[_whole(a) for a in ins], out_specs=tuple(_whole(o) for o in out_shape), out_shape=out_shape,
        compiler_params=_cparams(1),
    )(*ins)


def _pack_rep(vec_grads, dsgb):
    def body(*refs):
        o_ref = refs[-1]
        o_ref[...] = jnp.zeros((REP_ROWS, REP_W), f32)
        for (_, row, width, nrows), ref in zip(REP_VECS, refs[:-1]):
            o_ref[row:row + nrows, 0:width] = ref[0:nrows, :]

    ins = tuple(vec_grads) + (dsgb,)
    return pl.pallas_call(
        body, name="pack_rep", grid=(1,),
        in_specs=[_whole(a) for a in ins], out_specs=pl.BlockSpec((REP_ROWS, REP_W), lambda i: (0, 0)),
        out_shape=jax.ShapeDtypeStruct((REP_ROWS, REP_W), f32), compiler_params=_cparams(1),
    )(*ins)


def _mix_in_fwd(l, x, g1, w_in_p):
    t, d = x.shape
    tb = _tb(t)

    def body(x_ref, g_ref, w_ref, h_ref, *outs):
        h = _rms_fwd(x_ref[...], g_ref[l:l + 1, :]).astype(bf16)
        h_ref[...] = h
        for (off, wd), o_ref in zip(SEGS, outs):
            o_ref[...] = _dot(h, w_ref[:, off:off + wd])

    return pl.pallas_call(
        body, name="mix_in_fwd", grid=(t // tb,),
        in_specs=[_rows(tb, d), _const(g1.shape), _const((d, D_IN_PAD))],
        out_specs=tuple([_rows(tb, d)] + [_rows(tb, wd) for _, wd in SEGS]),
        out_shape=tuple([jax.ShapeDtypeStruct((t, d), bf16)] + [jax.ShapeDtypeStruct((t, wd), f32) for _, wd in SEGS]),
        compiler_params=_cparams(1),
    )(x, g1, w_in_p)


def _conv_fwd(l, a_in, dw_w, dw_b, ln_g, ln_b):
    t = a_in.shape[0]
    tb = _tb(t)

    def body(a_ref, w_ref, b_ref, g_ref, be_ref, s_ref, z1_ref, win):
        @pl.when(pl.program_id(0) == 0)
        def _():
            win[0:CONV_HALO, :] = jnp.zeros((CONV_HALO, D_CONV), f32)

        a = a_ref[...]
        win[CONV_HALO:CONV_HALO + tb, :] = a[:, :D_CONV] * jax.nn.sigmoid(a[:, D_CONV:])
        for r0 in range(0, tb, LANES):
            for c0 in range(0, D_CONV, LANES):
                cs = slice(c0, c0 + LANES)
                acc = jnp.broadcast_to(b_ref[l:l + 1, cs], (LANES, LANES))
                for j in range(CONV_K):
                    acc = acc + w_ref[j:j + 1, cs] * win[pl.ds(r0 + CONV_HALO - CONV_K + 1 + j, LANES), cs]
                z1_ref[r0:r0 + LANES, cs] = acc
        zh, _ = _ln_stats(z1_ref[...])
        zl = zh * g_ref[l:l + 1, :] + be_ref[l:l + 1, :]
        s_ref[...] = (zl * jax.nn.sigmoid(zl)).astype(bf16)
        win[0:CONV_HALO, :] = win[tb:tb + CONV_HALO, :]

    return pl.pallas_call(
        body, name="conv_fwd", grid=(t // tb,),
        in_specs=[_rows(tb, 2 * D_CONV), _const((CONV_HALO, D_CONV)), _const(dw_b.shape), _const(ln_g.shape), _const(ln_b.shape)],
        out_specs=(_rows(tb, D_CONV), _rows(tb, D_CONV)),
        out_shape=(jax.ShapeDtypeStruct((t, D_CONV), bf16), jax.ShapeDtypeStruct((t, D_CONV), f32)),
        scratch_shapes=[pltpu.VMEM((tb + CONV_HALO, D_CONV), f32)],
        compiler_params=_cparams(1),
    )(a_in, dw_w, dw_b, ln_g, ln_b)


def _tril_mask():
    r = lax.broadcasted_iota(jnp.int32, (SG_CHUNK, SG_CHUNK), 0)
    c = lax.broadcasted_iota(jnp.int32, (SG_CHUNK, SG_CHUNK), 1)
    return r >= c


def _sgu_fwd(l, b_in, ln_g, ln_b, sg_w, bexp):
    t = b_in.shape[0]
    tb = _tb(t)
    gw = D_SG // SG_GROUPS

    def body(b_ref, g_ref, be_ref, w_ref, bexp_ref, um_ref):
        gl = _gelu(b_ref[...])
        u = gl[:, :D_SG]
        vh, _ = _ln_stats(gl[:, D_SG:])
        vn = (vh * g_ref[l:l + 1, :] + be_ref[l:l + 1, :]).astype(bf16)
        tri = _tril_mask()
        for g in range(SG_GROUPS):
            wg = jnp.where(tri, w_ref[l, g], 0.0).astype(bf16)
            cs = slice(g * gw, (g + 1) * gw)
            for r0 in range(0, tb, SG_CHUNK):
                rs = slice(r0, r0 + SG_CHUNK)
                mixed = _dot(wg, vn[rs, cs]) + bexp_ref[:, cs]
                um_ref[rs, cs] = (u[rs, cs] * mixed).astype(bf16)

    return pl.pallas_call(
        body, name="sgu_fwd", grid=(t // tb,),
        in_specs=[_rows(tb, 2 * D_SG), _const(ln_g.shape), _const(ln_b.shape), _const(sg_w.shape), _const((SG_CHUNK, D_SG))],
        out_specs=_rows(tb, D_SG),
        out_shape=jax.ShapeDtypeStruct((t, D_SG), bf16),
        compiler_params=_cparams(1),
    )(b_in, ln_g, ln_b, sg_w, bexp)


def _mla_proj_fwd(l, qlat, kvlat, krope, cos_t, sin_t, gq, wuq_p, gkv, wk_p, wv_p):
    t = qlat.shape[0]
    tb = _tb(t)
    hw = N_HEADS * HEAD_PAD

    def body(ql_ref, kvl_ref, kr_ref, c_ref, s_ref, gq_ref, wq_ref, gkv_ref, wk_ref, wv_ref, q_ref, k_ref, v_ref, qn_ref, kvn_ref):
        cos_b, sin_b = c_ref[...], s_ref[...]
        qn = _rms_fwd(ql_ref[...], gq_ref[l:l + 1, :]).astype(bf16)
        qn_ref[...] = qn
        q = _dot(qn, wq_ref[...])
        sw = _swap_rope_halves(q)
        for h in range(N_HEADS):
            hs = slice(h * HEAD_PAD, (h + 1) * HEAD_PAD)
            q_ref[:, hs] = (q[:, hs] * cos_b + sw[:, hs] * sin_b).astype(bf16)
        kvn = _rms_fwd(kvl_ref[...], gkv_ref[l:l + 1, :]).astype(bf16)
        kvn_ref[...] = kvn
        kr = kr_ref[...]
        kpe = kr * cos_b + _swap_rope_halves(kr) * sin_b
        k = _dot(kvn, wk_ref[...])
        for h in range(N_HEADS):
            hs = slice(h * HEAD_PAD, (h + 1) * HEAD_PAD)
            k_ref[:, hs] = (k[:, hs] + kpe).astype(bf16)
        v_ref[...] = _dot(kvn, wv_ref[...]).astype(bf16)

    return pl.pallas_call(
        body, name="mla_proj_fwd", grid=(t // tb,),
        in_specs=[_rows(tb, Q_LORA), _rows(tb, KV_LORA), _rows(tb, HEAD_PAD), _rows(tb, HEAD_PAD), _rows(tb, HEAD_PAD),
                  _const(gq.shape), _const((Q_LORA, hw)), _const(gkv.shape), _const((KV_LORA, hw)), _const((KV_LORA, hw))],
        out_specs=(_rows(tb, hw), _rows(tb, hw), _rows(tb, hw), _rows(tb, Q_LORA), _rows(tb, KV_LORA)),
        out_shape=(jax.ShapeDtypeStruct((t, hw), bf16), jax.ShapeDtypeStruct((t, hw), bf16), jax.ShapeDtypeStruct((t, hw), bf16),
                   jax.ShapeDtypeStruct((t, Q_LORA), bf16), jax.ShapeDtypeStruct((t, KV_LORA), bf16)),
        compiler_params=_cparams(1),
    )(qlat, kvlat, krope, cos_t, sin_t, gq, wuq_p, gkv, wk_p, wv_p)


def _diag_mask(tq):
    return lax.broadcasted_iota(jnp.int32, (tq, tq), 0) >= lax.broadcasted_iota(jnp.int32, (tq, tq), 1)


def _attn_fwd(q, k, v):
    t = q.shape[0]
    tq = min(ATT_BLOCK, t)

    def body(q_ref, k_ref, v_ref, o_ref, lse_ref):
        qi = pl.program_id(1)
        for i in range(t // tq):
            @pl.when(qi == i)
            def _(i=i):
                lo, hi = i * tq, (i + 1) * tq
                for hh in range(ATT_HEADS):
                    hs = slice(hh * HEAD_PAD, (hh + 1) * HEAD_PAD)
                    qv = q_ref[:, hs]
                    s_d = jnp.where(_diag_mask(tq), _dot_nt(qv, k_ref[lo:hi, hs]) * ATT_SCALE, NEG)
                    m = jnp.max(s_d, axis=-1, keepdims=True)
                    if i > 0:
                        s_o = _dot_nt(qv, k_ref[0:lo, hs]) * ATT_SCALE
                        m = jnp.maximum(m, jnp.max(s_o, axis=-1, keepdims=True))
                    p_d = jnp.exp(s_d - m)
                    lsum = jnp.sum(p_d, axis=-1, keepdims=True)
                    acc = _dot(p_d.astype(bf16), v_ref[lo:hi, hs])
                    if i > 0:
                        p_o = jnp.exp(s_o - m)
                        lsum = lsum + jnp.sum(p_o, axis=-1, keepdims=True)
                        acc = acc + _dot(p_o.astype(bf16), v_ref[0:lo, hs])
                    o_ref[:, hs] = (acc / lsum).astype(bf16)
                    lse_ref[hh] = m + jnp.log(lsum)

    hw = ATT_HEADS * HEAD_PAD
    return pl.pallas_call(
        body, name="attn_fwd", grid=(N_HEADS // ATT_HEADS, t // tq),
        in_specs=[pl.BlockSpec((tq, hw), lambda h, i: (i, h)), pl.BlockSpec((t, hw), lambda h, i: (0, h)),
                  pl.BlockSpec((t, hw), lambda h, i: (0, h))],
        out_specs=(pl.BlockSpec((tq, hw), lambda h, i: (i, h)), pl.BlockSpec((ATT_HEADS, tq, 1), lambda h, i: (h, i, 0))),
        out_shape=(jax.ShapeDtypeStruct((t, N_HEADS * HEAD_PAD), bf16), jax.ShapeDtypeStruct((N_HEADS, t, 1), f32)),
        compiler_params=_cparams(2),
    )(q, k, v)


def _merge_out_fwd(l, x, s, um, o, gates, conv_out_w, sg_out_w, wo_p, w_out, g2):
    t, d = x.shape
    tb = _tb(t)

    def body(x_ref, s_ref, um_ref, o_ref, gt_ref, wa_ref, wb_ref, wc_ref, wout_ref, g_ref, x1_ref, mg_ref, om_ref):
        merged = (jax.nn.sigmoid(gt_ref[:, 0:d]) * _dot(s_ref[...], wa_ref[...])
                  + jax.nn.sigmoid(gt_ref[:, d:2 * d]) * _dot(um_ref[...], wb_ref[...])
                  + jax.nn.sigmoid(gt_ref[:, 2 * d:3 * d]) * _dot(o_ref[...], wc_ref[...]))
        mb = merged.astype(bf16)
        mg_ref[...] = mb
        om = _dot(mb, wout_ref[...])
        om_ref[...] = om
        x1_ref[...] = x_ref[...] + _rms_fwd(om, g_ref[l:l + 1, :])

    hw = N_HEADS * HEAD_PAD
    return pl.pallas_call(
        body, name="merge_out_fwd", grid=(t // tb,),
        in_specs=[_rows(tb, d), _rows(tb, D_CONV), _rows(tb, D_SG), _rows(tb, hw), _rows(tb, 3 * d),
                  _const((D_CONV, d)), _const((D_SG, d)), _const((hw, d)), _const((d, d)), _const(g2.shape)],
        out_specs=(_rows(tb, d), _rows(tb, d), _rows(tb, d)),
        out_shape=(jax.ShapeDtypeStruct((t, d), f32), jax.ShapeDtypeStruct((t, d), bf16), jax.ShapeDtypeStruct((t, d), f32)),
        compiler_params=_cparams(1),
    )(x, s, um, o, gates, conv_out_w, sg_out_w, wo_p, w_out, g2)


FF_CHUNK = 1408


def _ffn_conv_cols(zbuf, w_ref, b_ref, l, nrows, c0, c1):
    acc = b_ref[l:l + 1, c0:c1] + w_ref[0:1, c0:c1] * zbuf[pl.ds(FFN_HALO - 2, nrows), c0:c1]
    acc = acc + w_ref[1:2, c0:c1] * zbuf[pl.ds(FFN_HALO - 1, nrows), c0:c1]
    return acc + w_ref[2:3, c0:c1] * zbuf[pl.ds(FFN_HALO, nrows), c0:c1]


def _ffn_fwd(l, x1, g3, w_up, dw_w, dw_b, w_down, g4):
    t, d = x1.shape
    tb = _tb(t)
    ff2 = 2 * D_FF

    def body(x_ref, g3_ref, wup_ref, dww_ref, dwb_ref, wdn_ref, g4_ref, x2_ref, h2_ref, z_ref, zc_ref, act_ref, f_ref, zbuf):
        @pl.when(pl.program_id(0) == 0)
        def _():
            zbuf[0:FFN_HALO, :] = jnp.zeros((FFN_HALO, ff2), f32)

        xv = x_ref[...]
        h2 = _rms_fwd(xv, g3_ref[l:l + 1, :]).astype(bf16)
        h2_ref[...] = h2
        for c0 in range(0, ff2, FF_CHUNK):
            zv = _dot_nt(h2, wup_ref[c0:c0 + FF_CHUNK, :])
            z_ref[:, c0:c0 + FF_CHUNK] = zv.astype(bf16)
            zbuf[FFN_HALO:FFN_HALO + tb, c0:c0 + FF_CHUNK] = zv
        facc = jnp.zeros((tb, d), f32)
        for c0 in range(0, D_FF, FF_CHUNK):
            gg = _ffn_conv_cols(zbuf, dww_ref, dwb_ref, l, tb, c0, c0 + FF_CHUNK)
            vv = _ffn_conv_cols(zbuf, dww_ref, dwb_ref, l, tb, D_FF + c0, D_FF + c0 + FF_CHUNK)
            zc_ref[:, c0:c0 + FF_CHUNK] = gg.astype(bf16)
            zc_ref[:, D_FF + c0:D_FF + c0 + FF_CHUNK] = vv.astype(bf16)
            a = (_gelu(gg) * vv).astype(bf16)
            act_ref[:, c0:c0 + FF_CHUNK] = a
            facc = facc + _dot(a, wdn_ref[c0:c0 + FF_CHUNK, :])
        f_ref[...] = facc
        x2_ref[...] = xv + _rms_fwd(facc, g4_ref[l:l + 1, :])
        zbuf[0:FFN_HALO, :] = zbuf[tb:tb + FFN_HALO, :]

    return pl.pallas_call(
        body, name="ffn_fwd", grid=(t // tb,),
        in_specs=[_rows(tb, d), _const(g3.shape), _const((ff2, d)), _const((FFN_HALO, ff2)), _const(dw_b.shape), _const((D_FF, d)),
                  _const(g4.shape)],
        out_specs=(_rows(tb, d), _rows(tb, d), _rows(tb, ff2), _rows(tb, ff2), _rows(tb, D_FF), _rows(tb, d)),
        out_shape=(jax.ShapeDtypeStruct((t, d), f32), jax.ShapeDtypeStruct((t, d), bf16), jax.ShapeDtypeStruct((t, ff2), bf16),
                   jax.ShapeDtypeStruct((t, ff2), bf16), jax.ShapeDtypeStruct((t, D_FF), bf16), jax.ShapeDtypeStruct((t, d), f32)),
        scratch_shapes=[pltpu.VMEM((tb + FFN_HALO, ff2), f32)],
        compiler_params=_cparams(1),
    )(x1, g3, w_up, dw_w, dw_b, w_down, g4)


def _loss_fwd_bwd(y, target):
    t, d = y.shape
    tb = _tb(t)

    def body(y_ref, t_ref, dy_ref, loss_ref):
        @pl.when(pl.program_id(0) == 0)
        def _():
            loss_ref[...] = jnp.zeros((1, LANES), f32)

        e = y_ref[...] - t_ref[...]
        dy_ref[...] = e * (1.0 / d)
        loss_ref[...] += 0.5 * jnp.sum(_mean(e * e))

    dy, loss = pl.pallas_call(
        body, name="loss", grid=(t // tb,),
        in_specs=[_rows(tb, d), _rows(tb, d)],
        out_specs=(_rows(tb, d), _acc((1, LANES))),
        out_shape=(jax.ShapeDtypeStruct((t, d), f32), jax.ShapeDtypeStruct((1, LANES), f32)),
        compiler_params=_cparams(1),
    )(y, target)
    return dy, loss[0, 0]


def _ffn_bwd(l, dx2, x1, f, z, zc, w_up, dw_w, w_down, g3, g4, deps=()):
    t, d = x1.shape
    tb = min(128, t)
    nb = t // tb
    ff2 = 2 * D_FF
    hrows = 16
    per_h = tb // hrows

    def body(*refs):
        (dx2_ref, x1_ref, f_ref, z_ref, zp_ref, zc_ref, wup_ref, dww_ref, wdn_ref, g3_ref, g4_ref,
         dx1_ref, df_ref, dz_ref, dg4_ref, dg3_ref, ddwb_ref, ddww_ref, zbuf, dzc) = refs[len(deps):]
        i = pl.program_id(0)
        b = nb - 1 - i

        @pl.when(i == 0)
        def _():
            dg4_ref[...] = jnp.zeros_like(dg4_ref)
            dg3_ref[...] = jnp.zeros_like(dg3_ref)
            ddwb_ref[...] = jnp.zeros_like(ddwb_ref)
            ddww_ref[...] = jnp.zeros_like(ddww_ref)
            dzc[tb:tb + FFN_HALO, :] = jnp.zeros((FFN_HALO, ff2), f32)

        dout = dx2_ref[...]
        df, dg4 = _rms_bwd(f_ref[...], g4_ref[l:l + 1, :], dout)
        dg4_ref[...] += dg4
        dfb = df.astype(bf16)
        df_ref[...] = dfb
        zbuf[0:FFN_HALO, :] = jnp.where(b > 0, zp_ref[hrows - FFN_HALO:hrows, :].astype(f32), 0.0)
        zbuf[FFN_HALO:FFN_HALO + tb, :] = z_ref[...].astype(f32)
        for c0 in range(0, D_FF, FF_CHUNK):
            dact = _dot_nt(dfb, wdn_ref[c0:c0 + FF_CHUNK, :])
            gel, dgel = _gelu_and_grad(zc_ref[:, c0:c0 + FF_CHUNK].astype(f32))
            dzc[0:tb, c0:c0 + FF_CHUNK] = dact * zc_ref[:, D_FF + c0:D_FF + c0 + FF_CHUNK].astype(f32) * dgel
            dzc[0:tb, D_FF + c0:D_FF + c0 + FF_CHUNK] = dact * gel
        dh2 = jnp.zeros((tb, d), f32)
        for c0 in range(0, ff2, FF_CHUNK):
            cs = slice(c0, c0 + FF_CHUNK)
            d0 = dzc[0:tb, cs]
            ddwb_ref[:, cs] += _colsum(d0)
            for j in range(FFN_K):
                ddww_ref[j:j + 1, cs] += _colsum(d0 * zbuf[pl.ds(FFN_HALO - 2 + j, tb), cs])
            dzv = dww_ref[2:3, cs] * d0 + dww_ref[1:2, cs] * dzc[pl.ds(1, tb), cs] + dww_ref[0:1, cs] * dzc[pl.ds(2, tb), cs]
            dzb = dzv.astype(bf16)
            dz_ref[:, cs] = dzb
            dh2 = dh2 + _dot(dzb, wup_ref[cs, :])
        dzc[tb:tb + FFN_HALO, :] = dzc[0:FFN_HALO, :]
        dxn, dg3 = _rms_bwd(x1_ref[...], g3_ref[l:l + 1, :], dh2)
        dg3_ref[...] += dg3
        dx1_ref[...] = dout + dxn

    return pl.pallas_call(
        body, name="ffn_bwd", grid=(nb,),
        in_specs=[_ANY] * len(deps) + [_rows_rev(tb, d, nb), _rows_rev(tb, d, nb), _rows_rev(tb, d, nb), _rows_rev(tb, ff2, nb),
                  pl.BlockSpec((hrows, ff2), lambda i: (jnp.maximum((nb - 1 - i) * per_h - 1, 0), 0)), _rows_rev(tb, ff2, nb),
                  _const((ff2, d)), _const((FFN_HALO, ff2)), _const((D_FF, d)), _const(g3.shape), _const(g4.shape)],
        out_specs=(_rows_rev(tb, d, nb), _rows_rev(tb, d, nb), _rows_rev(tb, ff2, nb), _acc((1, d)), _acc((1, d)), _acc((1, ff2)),
                   _acc((FFN_HALO, ff2))),
        out_shape=(jax.ShapeDtypeStruct((t, d), f32), jax.ShapeDtypeStruct((t, d), bf16), jax.ShapeDtypeStruct((t, ff2), bf16),
                   jax.ShapeDtypeStruct((1, d), f32), jax.ShapeDtypeStruct((1, d), f32), jax.ShapeDtypeStruct((1, ff2), f32),
                   jax.ShapeDtypeStruct((FFN_HALO, ff2), f32)),
        scratch_shapes=[pltpu.VMEM((tb + FFN_HALO, ff2), f32), pltpu.VMEM((tb + FFN_HALO, ff2), f32)],
        compiler_params=_cparams(1),
    )(*deps, dx2, x1, f, z, z, zc, w_up, dw_w, w_down, g3, g4)


def _merge_out_bwd(l, dx1, om, s, um, o, gates, conv_out_w, sg_out_w, wo_p, w_out, g2, deps=()):
    t, d = dx1.shape
    tb = _tb(t)
    hw = N_HEADS * HEAD_PAD

    def body(*refs):
        (dx_ref, om_ref, s_ref, um_ref, o_ref, gt_ref, wa_ref, wb_ref, wc_ref, wout_ref, g_ref,
         dom_ref, dgt_ref, dya_ref, dyb_ref, dyc_ref, ds_ref, dum_ref, do_ref, dg2_ref) = refs[len(deps):]

        @pl.when(pl.program_id(0) == 0)
        def _():
            dg2_ref[...] = jnp.zeros_like(dg2_ref)

        dom, dg2 = _rms_bwd(om_ref[...], g_ref[l:l + 1, :], dx_ref[...])
        dg2_ref[...] += dg2
        domb = dom.astype(bf16)
        dom_ref[...] = domb
        dmerged = _dot_nt(domb, wout_ref[...])
        branches = ((s_ref, wa_ref, dya_ref, ds_ref), (um_ref, wb_ref, dyb_ref, dum_ref), (o_ref, wc_ref, dyc_ref, do_ref))
        for br, (in_ref, w_ref, dy_ref, din_ref) in enumerate(branches):
            yv = _dot(in_ref[...], w_ref[...])
            sg = jax.nn.sigmoid(gt_ref[:, br * d:(br + 1) * d])
            dyb = (dmerged * sg).astype(bf16)
            dy_ref[...] = dyb
            dgt_ref[:, br * d:(br + 1) * d] = (dmerged * yv * sg * (1.0 - sg)).astype(bf16)
            din_ref[...] = _dot_nt(dyb, w_ref[...]).astype(din_ref.dtype)

    return pl.pallas_call(
        body, name="merge_out_bwd", grid=(t // tb,),
        in_specs=[_ANY] * len(deps) + [_rows(tb, d), _rows(tb, d), _rows(tb, D_CONV), _rows(tb, D_SG), _rows(tb, hw), _rows(tb, 3 * d),
                  _const((D_CONV, d)), _const((D_SG, d)), _const((hw, d)), _const((d, d)), _const(g2.shape)],
        out_specs=(_rows(tb, d), _rows(tb, 3 * d), _rows(tb, d), _rows(tb, d), _rows(tb, d), _rows(tb, D_CONV), _rows(tb, D_SG),
                   _rows(tb, hw), _acc((1, d))),
        out_shape=(jax.ShapeDtypeStruct((t, d), bf16), jax.ShapeDtypeStruct((t, 3 * d), bf16), jax.ShapeDtypeStruct((t, d), bf16),
                   jax.ShapeDtypeStruct((t, d), bf16), jax.ShapeDtypeStruct((t, d), bf16), jax.ShapeDtypeStruct((t, D_CONV), f32),
                   jax.ShapeDtypeStruct((t, D_SG), f32), jax.ShapeDtypeStruct((t, hw), bf16), jax.ShapeDtypeStruct((1, d), f32)),
        compiler_params=_cparams(1),
    )(*deps, dx1, om, s, um, o, gates, conv_out_w, sg_out_w, wo_p, w_out, g2)


def _attn_bwd(q, k, v, o, lse, do):
    t = q.shape[0]
    tq = min(ATT_BLOCK, t)
    hw = N_HEADS * HEAD_PAD

    def body(q_ref, k_ref, v_ref, o_ref, lse_ref, do_ref, dq_ref, dk_ref, dv_ref):
        qi = pl.program_id(1)

        @pl.when(qi == 0)
        def _():
            dk_ref[...] = jnp.zeros_like(dk_ref)
            dv_ref[...] = jnp.zeros_like(dv_ref)

        def keys(lo, hi, hs, qv, dov, lse, delta, diagonal):
            kj, vj = k_ref[lo:hi, hs], v_ref[lo:hi, hs]
            p = jnp.exp(_dot_nt(qv, kj) * ATT_SCALE - lse)
            if diagonal:
                p = jnp.where(_diag_mask(tq), p, 0.0)
            ds = (p * (_dot_nt(dov, vj) - delta) * ATT_SCALE).astype(bf16)
            dk_ref[lo:hi, hs] += _dot_tn(ds, qv)
            dv_ref[lo:hi, hs] += _dot_tn(p.astype(bf16), dov)
            return _dot(ds, kj)

        for i in range(t // tq):
            @pl.when(qi == i)
            def _(i=i):
                for hh in range(ATT_HEADS):
                    hs = slice(hh * HEAD_PAD, (hh + 1) * HEAD_PAD)
                    qv, dov, lse = q_ref[:, hs], do_ref[:, hs], lse_ref[hh]
                    delta = jnp.sum(dov.astype(f32) * o_ref[:, hs].astype(f32), axis=-1, keepdims=True)
                    dq = keys(i * tq, (i + 1) * tq, hs, qv, dov, lse, delta, True)
                    if i > 0:
                        dq = dq + keys(0, i * tq, hs, qv, dov, lse, delta, False)
                    dq_ref[:, hs] = dq

    blk_q = pl.BlockSpec((tq, ATT_HEADS * HEAD_PAD), lambda h, i: (i, h))
    blk_kv = pl.BlockSpec((t, ATT_HEADS * HEAD_PAD), lambda h, i: (0, h))
    return pl.pallas_call(
        body, name="attn_bwd", grid=(N_HEADS // ATT_HEADS, t // tq),
        in_specs=[blk_q, blk_kv, blk_kv, blk_q, pl.BlockSpec((ATT_HEADS, tq, 1), lambda h, i: (h, i, 0)), blk_q],
        out_specs=(blk_q, blk_kv, blk_kv),
        out_shape=(jax.ShapeDtypeStruct((t, hw), f32), jax.ShapeDtypeStruct((t, hw), f32), jax.ShapeDtypeStruct((t, hw), f32)),
        compiler_params=_cparams(2),
    )(q, k, v, o, lse, do)


def _mla_proj_bwd(l, dq, dk, dv, qlat, kvlat, cos_t, sin_t, gq, wuq_p, gkv, wk_p, wv_p):
    t = qlat.shape[0]
    tb = _tb(t)
    hw = N_HEADS * HEAD_PAD

    def body(dq_ref, dk_ref, dv_ref, ql_ref, kvl_ref, c_ref, s_ref, gq_ref, wq_ref, gkv_ref, wk_ref, wv_ref,
             dqb_ref, dkb_ref, dvb_ref, dql_ref, dkvl_ref, dkr_ref, dgq_ref, dgkv_ref):
        @pl.when(pl.program_id(0) == 0)
        def _():
            dgq_ref[...] = jnp.zeros_like(dgq_ref)
            dgkv_ref[...] = jnp.zeros_like(dgkv_ref)

        cos_b, sin_b = c_ref[...], s_ref[...]
        for h in range(N_HEADS):
            hs = slice(h * HEAD_PAD, (h + 1) * HEAD_PAD)
            dqh = dq_ref[:, hs]
            dqb_ref[:, hs] = (dqh * cos_b + _swap_rope_halves(dqh * sin_b)).astype(bf16)
        dqn = _dot_nt(dqb_ref[...], wq_ref[...])
        dql, dgq = _rms_bwd(ql_ref[...], gq_ref[l:l + 1, :], dqn)
        dgq_ref[...] += dgq
        dql_ref[...] = dql.astype(bf16)
        dkv_full = dk_ref[...]
        dkb = dkv_full.astype(bf16)
        dkb_ref[...] = dkb
        dkpe = dkv_full[:, 0:HEAD_PAD]
        for h in range(1, N_HEADS):
            dkpe = dkpe + dkv_full[:, h * HEAD_PAD:(h + 1) * HEAD_PAD]
        dkr_ref[...] = (dkpe * cos_b + _swap_rope_halves(dkpe * sin_b)).astype(bf16)
        dvb = dv_ref[...].astype(bf16)
        dvb_ref[...] = dvb
        dkvn = _dot_nt(dkb, wk_ref[...]) + _dot_nt(dvb, wv_ref[...])
        dkvl, dgkv = _rms_bwd(kvl_ref[...], gkv_ref[l:l + 1, :], dkvn)
        dgkv_ref[...] += dgkv
        dkvl_ref[...] = dkvl.astype(bf16)

    return pl.pallas_call(
        body, name="mla_proj_bwd", grid=(t // tb,),
        in_specs=[_rows(tb, hw), _rows(tb, hw), _rows(tb, hw), _rows(tb, Q_LORA), _rows(tb, KV_LORA), _rows(tb, HEAD_PAD),
                  _rows(tb, HEAD_PAD), _const(gq.shape), _const((Q_LORA, hw)), _const(gkv.shape), _const((KV_LORA, hw)),
                  _const((KV_LORA, hw))],
        out_specs=(_rows(tb, hw), _rows(tb, hw), _rows(tb, hw), _rows(tb, Q_LORA), _rows(tb, KV_LORA), _rows(tb, HEAD_PAD),
                   _acc((1, Q_LORA)), _acc((1, KV_LORA))),
        out_shape=(jax.ShapeDtypeStruct((t, hw), bf16), jax.ShapeDtypeStruct((t, hw), bf16), jax.ShapeDtypeStruct((t, hw), bf16),
                   jax.ShapeDtypeStruct((t, Q_LORA), bf16), jax.ShapeDtypeStruct((t, KV_LORA), bf16),
                   jax.ShapeDtypeStruct((t, HEAD_PAD), bf16), jax.ShapeDtypeStruct((1, Q_LORA), f32),
                   jax.ShapeDtypeStruct((1, KV_LORA), f32)),
        compiler_params=_cparams(1),
    )(dq, dk, dv, qlat, kvlat, cos_t, sin_t, gq, wuq_p, gkv, wk_p, wv_p)


def _sgu_bwd(l, b_in, dum, ln_g, ln_b, sg_w, bexp):
    t = b_in.shape[0]
    tb = _tb(t)
    gw = D_SG // SG_GROUPS

    def body(b_ref, dum_ref, g_ref, be_ref, w_ref, bexp_ref, db_ref, dw_ref, dsgb_ref, dlg_ref, dlb_ref, dvn_s):
        @pl.when(pl.program_id(0) == 0)
        def _():
            dw_ref[...] = jnp.zeros_like(dw_ref)
            dsgb_ref[...] = jnp.zeros_like(dsgb_ref)
            dlg_ref[...] = jnp.zeros_like(dlg_ref)
            dlb_ref[...] = jnp.zeros_like(dlb_ref)

        gl, dgl = _gelu_and_grad(b_ref[...])
        u = gl[:, :D_SG]
        vh, rstd = _ln_stats(gl[:, D_SG:])
        ln_gain = g_ref[l:l + 1, :]
        vn = (vh * ln_gain + be_ref[l:l + 1, :]).astype(bf16)
        dumv = dum_ref[...]
        tri = _tril_mask()
        ones = jnp.ones((FFN_HALO, gw), f32)
        for g in range(SG_GROUPS):
            wg = jnp.where(tri, w_ref[l, g], 0.0).astype(bf16)
            cs = slice(g * gw, (g + 1) * gw)
            for r0 in range(0, tb, SG_CHUNK):
                rs = slice(r0, r0 + SG_CHUNK)
                vblk = vn[rs, cs]
                mixed = _dot(wg, vblk) + bexp_ref[:, cs]
                db_ref[rs, cs] = (dumv[rs, cs] * mixed * dgl[rs, cs]).astype(bf16)
                dmix = dumv[rs, cs] * u[rs, cs]
                dmb = dmix.astype(bf16)
                dw_ref[g] += jnp.where(tri, _dot_nt(dmb, vblk), 0.0)
                rowsum = lax.dot_general(ones, dmix, (((1,), (1,)), ((), ())), preferred_element_type=f32,
                                         precision=lax.Precision.HIGHEST)
                dsgb_ref[g:g + 1, :] += rowsum[0:1, :]
                dvn_s[rs, cs] = _dot_tn(wg, dmb)
        dvn = dvn_s[...]
        dlg_ref[...] += _colsum(dvn * vh)
        dlb_ref[...] += _colsum(dvn)
        db_ref[:, D_SG:] = (_ln_bwd(vh, rstd, ln_gain, dvn) * dgl[:, D_SG:]).astype(bf16)

    return pl.pallas_call(
        body, name="sgu_bwd", grid=(t // tb,),
        in_specs=[_rows(tb, 2 * D_SG), _rows(tb, D_SG), _const(ln_g.shape), _const(ln_b.shape), _const(sg_w.shape),
                  _const((SG_CHUNK, D_SG))],
        out_specs=(_rows(tb, 2 * D_SG), _acc((SG_GROUPS, SG_CHUNK, SG_CHUNK)), _acc((FFN_HALO, SG_CHUNK)), _acc((1, D_SG)),
                   _acc((1, D_SG))),
        out_shape=(jax.ShapeDtypeStruct((t, 2 * D_SG), bf16), jax.ShapeDtypeStruct((SG_GROUPS, SG_CHUNK, SG_CHUNK), f32),
                   jax.ShapeDtypeStruct((FFN_HALO, SG_CHUNK), f32), jax.ShapeDtypeStruct((1, D_SG), f32),
                   jax.ShapeDtypeStruct((1, D_SG), f32)),
        scratch_shapes=[pltpu.VMEM((tb, D_SG), f32)],
        compiler_params=_cparams(1),
    )(b_in, dum, ln_g, ln_b, sg_w, bexp)


def _conv_bwd(l, a_in, z1, ds, dw_w, ln_g, ln_b):
    t = a_in.shape[0]
    tb = _tb(t)
    nb = t // tb
    per_halo = tb // CONV_HALO

    def body(a_ref, ap_ref, z1_ref, ds_ref, w_ref, g_ref, be_ref, da_ref, ddww_ref, ddwb_ref, dlg_ref, dlb_ref, win, dzb):
        i = pl.program_id(0)
        b = nb - 1 - i

        @pl.when(i == 0)
        def _():
            ddww_ref[...] = jnp.zeros_like(ddww_ref)
            ddwb_ref[...] = jnp.zeros_like(ddwb_ref)
            dlg_ref[...] = jnp.zeros_like(dlg_ref)
            dlb_ref[...] = jnp.zeros_like(dlb_ref)
            dzb[tb:tb + CONV_HALO, :] = jnp.zeros((CONV_HALO, D_CONV), f32)

        a = a_ref[...]
        val = a[:, :D_CONV]
        sg = jax.nn.sigmoid(a[:, D_CONV:])
        ap = ap_ref[...]
        win[0:CONV_HALO, :] = jnp.where(b > 0, ap[:, :D_CONV] * jax.nn.sigmoid(ap[:, D_CONV:]), 0.0)
        win[CONV_HALO:CONV_HALO + tb, :] = val * sg
        zh, rstd = _ln_stats(z1_ref[...])
        ln_gain = g_ref[l:l + 1, :]
        zl = zh * ln_gain + be_ref[l:l + 1, :]
        sgl = jax.nn.sigmoid(zl)
        dzl = ds_ref[...] * (sgl * (1.0 + zl * (1.0 - sgl)))
        dlg_ref[...] += _colsum(dzl * zh)
        dlb_ref[...] += _colsum(dzl)
        dz1 = _ln_bwd(zh, rstd, ln_gain, dzl)
        dzb[0:tb, :] = dz1
        ddwb_ref[...] += _colsum(dz1)
        dgate_f = val * sg * (1.0 - sg)
        for c0 in range(0, D_CONV, LANES):
            cs = slice(c0, c0 + LANES)
            for r0 in range(0, tb, LANES):
                d1 = dzb[r0:r0 + LANES, cs]
                acc = jnp.zeros((LANES, LANES), f32)
                for j in range(CONV_K):
                    ddww_ref[j:j + 1, cs] += _colsum(d1 * win[pl.ds(r0 + CONV_HALO - CONV_K + 1 + j, LANES), cs])
                    acc = acc + w_ref[j:j + 1, cs] * dzb[pl.ds(r0 + CONV_K - 1 - j, LANES), cs]
                da_ref[r0:r0 + LANES, cs] = (acc * sg[r0:r0 + LANES, cs]).astype(bf16)
                da_ref[r0:r0 + LANES, c0 + D_CONV:c0 + D_CONV + LANES] = (acc * dgate_f[r0:r0 + LANES, cs]).astype(bf16)
        dzb[tb:tb + CONV_HALO, :] = dzb[0:CONV_HALO, :]

    return pl.pallas_call(
        body, name="conv_bwd", grid=(nb,),
        in_specs=[_rows_rev(tb, 2 * D_CONV, nb),
                  pl.BlockSpec((CONV_HALO, 2 * D_CONV), lambda i: (jnp.maximum((nb - 1 - i) * per_halo - 1, 0), 0)),
                  _rows_rev(tb, D_CONV, nb), _rows_rev(tb, D_CONV, nb), _const((CONV_HALO, D_CONV)), _const(ln_g.shape),
                  _const(ln_b.shape)],
        out_specs=(_rows_rev(tb, 2 * D_CONV, nb), _acc((CONV_HALO, D_CONV)), _acc((1, D_CONV)), _acc((1, D_CONV)), _acc((1, D_CONV))),
        out_shape=(jax.ShapeDtypeStruct((t, 2 * D_CONV), bf16), jax.ShapeDtypeStruct((CONV_HALO, D_CONV), f32),
                   jax.ShapeDtypeStruct((1, D_CONV), f32), jax.ShapeDtypeStruct((1, D_CONV), f32), jax.ShapeDtypeStruct((1, D_CONV), f32)),
        scratch_shapes=[pltpu.VMEM((tb + CONV_HALO, D_CONV), f32), pltpu.VMEM((tb + CONV_HALO, D_CONV), f32)],
        compiler_params=_cparams(1),
    )(a_in, a_in, z1, ds, dw_w, ln_g, ln_b)


def _mix_in_bwd(l, x, g1, dxres, dsegs, w_in_p, deps=()):
    t, d = x.shape
    tb = _tb(t)

    def body(*refs):
        x_ref, g_ref, dr_ref = refs[len(deps):len(deps) + 3]
        rest = refs[len(deps) + 3:]
        dseg_refs, w_ref, dx_ref, dg_ref = rest[:len(SEGS)], rest[len(SEGS)], rest[len(SEGS) + 1], rest[len(SEGS) + 2]

        @pl.when(pl.program_id(0) == 0)
        def _():
            dg_ref[...] = jnp.zeros_like(dg_ref)

        dh = jnp.zeros((tb, d), f32)
        for (off, wd), ds_ref in zip(SEGS, dseg_refs):
            dh = dh + _dot_nt(ds_ref[...], w_ref[:, off:off + wd])
        dxn, dg = _rms_bwd(x_ref[...], g_ref[l:l + 1, :], dh)
        dg_ref[...] += dg
        dx_ref[...] = dr_ref[...] + dxn

    return pl.pallas_call(
        body, name="mix_in_bwd", grid=(t // tb,),
        in_specs=([_ANY] * len(deps) + [_rows(tb, d), _const(g1.shape), _rows(tb, d)] + [_rows(tb, wd) for _, wd in SEGS]
                  + [_const((d, D_IN_PAD))]),
        out_specs=(_rows(tb, d), _acc((1, d))),
        out_shape=(jax.ShapeDtypeStruct((t, d), f32), jax.ShapeDtypeStruct((1, d), f32)),
        compiler_params=_cparams(1),
    )(*deps, x, g1, dxres, *dsegs, w_in_p)


def _pick_block(n, cap=512):
    for b in (cap, 384, 256, 128):
        if b <= cap and n % b == 0:
            return b
    return n


def _wgrad(a, b, name, out_dtype=f32):
    t, kdim = a.shape
    n = b.shape[1]
    bk, bn = _pick_block(kdim), _pick_block(n, cap=1024)

    def body(a_ref, b_ref, o_ref):
        o_ref[...] = _dot_tn(a_ref[...], b_ref[...]).astype(out_dtype)

    return pl.pallas_call(
        body, name=name, grid=(kdim // bk, n // bn),
        in_specs=[pl.BlockSpec((t, bk), lambda i, j: (0, i)), pl.BlockSpec((t, bn), lambda i, j: (0, j))],
        out_specs=pl.BlockSpec((bk, bn), lambda i, j: (i, j)),
        out_shape=jax.ShapeDtypeStruct((kdim, n), out_dtype),
        compiler_params=_cparams(2),
    )(a, b)


def _wgrad_multi(a, bs, name, deps=()):
    t, kdim = a.shape
    bk = _pick_block(kdim)
    nb = len(bs)

    def body(*refs):
        a_ref, refs = refs[len(deps)], refs[len(deps) + 1:]
        at = a_ref[...].T
        for b_ref, o_ref in zip(refs[:nb], refs[nb:]):
            o_ref[...] = _dot(at, b_ref[...])

    return pl.pallas_call(
        body, name=name, grid=(kdim // bk,),
        in_specs=[_ANY] * len(deps) + [pl.BlockSpec((t, bk), lambda i: (0, i))] + [_const(b.shape) for b in bs],
        out_specs=tuple(pl.BlockSpec((bk, b.shape[1]), lambda i: (i, 0)) for b in bs),
        out_shape=tuple(jax.ShapeDtypeStruct((kdim, b.shape[1]), f32) for b in bs),
        compiler_params=_cparams(1),
    )(*deps, a, *bs)


_BC1 = 1.0 - ADAM_B1 ** ADAM_STEP
_BC2 = 1.0 - ADAM_B2 ** ADAM_STEP


def _adam_math(g, w, m, v):
    nm = ADAM_B1 * m + (1.0 - ADAM_B1) * g
    nv = ADAM_B2 * v + (1.0 - ADAM_B2) * (g * g)
    delta = -ADAM_LR * ((nm / _BC1) / (jnp.sqrt(nv / _BC2) + ADAM_EPS) + ADAM_WD * w)
    return delta, nm, nv


def _slot_sum(r_ref, index=()):
    g = r_ref[(0,) + index].astype(f32)
    for s in range(1, N_DEV):
        g = g + r_ref[(s,) + index].astype(f32)
    return g


def _adamw_shard(l, recv, w, m, v, prev, name, deps=()):
    _, k, ns = recv.shape
    rb = next((c for c in (256, 192, 176, 128) if k % c == 0), k)
    blk = pl.BlockSpec((None, rb, ns), lambda i: (l, i, 0))

    def body(r_ref, w_ref, m_ref, v_ref, *rest):
        g_ref, d_ref, nm_ref, nv_ref = rest[-4:]
        g = _slot_sum(r_ref)
        g_ref[...] = g
        d_ref[...], nm_ref[...], nv_ref[...] = _adam_math(g, w_ref[...], m_ref[...], v_ref[...])

    out = jax.ShapeDtypeStruct(w.shape, f32)
    n_prev = 0 if prev is None else 4
    return pl.pallas_call(
        body, name=name, grid=(k // rb,),
        in_specs=[pl.BlockSpec((N_DEV, rb, ns), lambda i: (0, i, 0)), blk, blk, blk] + [_ANY] * (n_prev + len(deps)),
        out_specs=(blk, blk, blk, blk), out_shape=(out, out, out, out),
        input_output_aliases={4 + j: j for j in range(n_prev)},
        compiler_params=_cparams(1),
    )(recv, w, m, v, *(prev or ()), *deps)


def _adamw_rep(recvs, ws, ms, vs):
    depth = len(recvs)
    nt = len(REP_VECS)

    def body(*refs):
        r_refs = refs[:depth]
        w_refs, m_refs, v_refs = (refs[depth + i * nt:depth + (i + 1) * nt] for i in range(3))
        outs = refs[depth + 3 * nt:]
        for ti, (_, row, width, nrows) in enumerate(REP_VECS):
            for l in range(depth):
                g = r_refs[l][0, row:row + nrows, 0:width]
                for s in range(1, N_DEV):
                    g = g + r_refs[l][s, row:row + nrows, 0:width]
                pick = (lambda ref: ref[l]) if nrows > 1 else (lambda ref: ref[l:l + 1, :])
                delta, nm, nv = _adam_math(g, pick(w_refs[ti]), pick(m_refs[ti]), pick(v_refs[ti]))
                for o_ref, val in zip(outs[4 * ti:4 * ti + 4], (g, delta, nm, nv)):
                    if nrows > 1:
                        o_ref[l] = val
                    else:
                        o_ref[l:l + 1, :] = val

    ins = tuple(recvs) + tuple(ws) + tuple(ms) + tuple(vs)
    out_shape = tuple(jax.ShapeDtypeStruct(w.shape, f32) for w in ws for _ in range(4))
    return pl.pallas_call(
        body, name="adamw_rep", grid=(1,),
        in_specs=[_whole(a) for a in ins], out_specs=tuple(_whole(o) for o in out_shape), out_shape=out_shape,
        compiler_params=_cparams(1),
    )(*ins)


def _adamw_sg_w(recvs, w, m, v):
    depth = len(recvs)

    def body(*refs):
        r_refs = refs[:depth]
        w_ref, m_ref, v_ref = refs[depth:depth + 3]
        outs = refs[depth + 3:]
        for l in range(depth):
            for gi in range(SG_GROUPS):
                g = _slot_sum(r_refs[l], (gi,))
                delta, nm, nv = _adam_math(g, w_ref[l, gi], m_ref[l, gi], v_ref[l, gi])
                for o_ref, val in zip(outs, (g, delta, nm, nv)):
                    o_ref[l, gi] = val

    ins = tuple(recvs) + (w, m, v)
    out = jax.ShapeDtypeStruct(w.shape, f32)
    return pl.pallas_call(
        body, name="adamw_sg_w", grid=(1,),
        in_specs=[_whole(a) for a in ins], out_specs=tuple(_whole(out) for _ in range(4)), out_shape=(out,) * 4,
        compiler_params=_cparams(1),
    )(*ins)


def _rope_tables(positions):
    t = positions.shape[0]
    inv = 10000.0 ** (-jnp.arange(0, QK_ROPE, 2, dtype=f32) / QK_ROPE)
    ang = positions.astype(f32)[:, None] * inv
    cos, sin = jnp.cos(ang), jnp.sin(ang)
    tail = jnp.zeros((t, HEAD_PAD - KR_LO - QK_ROPE), f32)
    cos_t = jnp.concatenate([jnp.ones((t, KR_LO), f32), cos, cos, tail], axis=1)
    sin_t = jnp.concatenate([jnp.zeros((t, KR_LO), f32), -sin, sin, tail], axis=1)
    return cos_t, sin_t


def _bias_over_channels(sg_b_l):
    return jnp.broadcast_to(sg_b_l.T[:, :, None], (SG_CHUNK, SG_GROUPS, D_SG // SG_GROUPS)).reshape(SG_CHUNK, D_SG)


def _mixer_weights(gathered, rep, l):
    conv_out_w, sg_out_w, wuq_p, wk_p, wv_p, wo_p, conv_dw_w, ffn_dw_w = _asm_small(
        gathered["conv_out_w"], gathered["sg_out_w"], gathered["mla_w_uq"], gathered["mla_w_ukv"], gathered["mla_w_o"],
        gathered["conv_dw_w"], gathered["ffn_dw_w"])
    g_out = gathered["w_out"]
    w = dict(rep)
    w.update(
        l=l, w_in_p=_asm_w_in(gathered["w_in"]),
        conv_out_w=conv_out_w, sg_out_w=sg_out_w, wuq_p=wuq_p, wk_p=wk_p, wv_p=wv_p, wo_p=wo_p, conv_dw_w_p=conv_dw_w, ffn_dw_w_p=ffn_dw_w,
        w_out=g_out.reshape(g_out.shape[0] * g_out.shape[1], g_out.shape[2]),
        bexp=_bias_over_channels(rep["sg_b"][l]))
    return w


def _ffn_weights(gathered):
    stack = lambda g: g.reshape(g.shape[0] * g.shape[1], g.shape[2])
    return dict(w_up=stack(gathered["ffn_w_up"]), w_down=stack(gathered["ffn_w_down"]))


def _mixer_fwd(x, w, cos_t, sin_t):
    l = w["l"]
    h, a_in, b_in, qlat, kvlat, krope, gates = _mix_in_fwd(l, x, w["mix_pre_g"], w["w_in_p"])
    s, z1 = _conv_fwd(l, a_in, w["conv_dw_w_p"], w["conv_dw_b"], w["conv_ln_g"], w["conv_ln_b"])
    um = _sgu_fwd(l, b_in, w["sg_ln_g"], w["sg_ln_b"], w["sg_w"], w["bexp"])
    q, k, v, qn, kvn = _mla_proj_fwd(l, qlat, kvlat, krope, cos_t, sin_t, w["mla_q_norm_g"], w["wuq_p"], w["mla_kv_norm_g"],
                                     w["wk_p"], w["wv_p"])
    o, lse = _attn_fwd(q, k, v)
    x1, merged, om = _merge_out_fwd(l, x, s, um, o, gates, w["conv_out_w"], w["sg_out_w"], w["wo_p"], w["w_out"], w["mix_post_g"])
    saved = dict(x=x, h=h, a_in=a_in, b_in=b_in, qlat=qlat, kvlat=kvlat, gates=gates, s=s, z1=z1, um=um, q=q, k=k, v=v, qn=qn,
                 kvn=kvn, o=o, lse=lse, x1=x1, merged=merged, om=om)
    return x1, saved


def _ffn_layer_fwd(x1, w):
    x2, h2, z, zc, act, f = _ffn_fwd(w["l"], x1, w["ffn_pre_g"], w["w_up"], w["ffn_dw_w_p"], w["ffn_dw_b"], w["w_down"], w["ffn_post_g"])
    return x2, dict(h2=h2, z=z, zc=zc, act=act, f=f)


def _ffn_layer_bwd(dx2, sv, w, deps=()):
    l = w["l"]
    vec = {}
    dx1, df, dz, vec["ffn_post_g"], vec["ffn_pre_g"], vec["ffn_dw_b"], vec["d_fdw"] = _ffn_bwd(
        l, dx2, sv["x1"], sv["f"], sv["z"], sv["zc"], w["w_up"], w["ffn_dw_w_p"], w["w_down"], w["ffn_pre_g"], w["ffn_post_g"],
        deps=deps)
    d_down = _wgrad(sv["act"], df, "wgrad_ffn_down", bf16)
    d_up_t = _wgrad(dz, sv["h2"], "wgrad_ffn_up", bf16)
    unstack = lambda g: g.reshape(N_DEV, g.shape[0] // N_DEV, g.shape[1])
    send = dict(ffn_w_up=unstack(d_up_t), ffn_w_down=unstack(d_down))
    return dx1, send, vec


def _mixer_bwd(dx1, sv, w, cos_t, sin_t, shard_cols, vec, start, deps=()):
    l = w["l"]
    d_fdw = vec.pop("d_fdw")
    dom, dgates, dya, dyb, dyc, ds, dum, do, vec["mix_post_g"] = _merge_out_bwd(
        l, dx1, sv["om"], sv["s"], sv["um"], sv["o"], sv["gates"], w["conv_out_w"], w["sg_out_w"], w["wo_p"], w["w_out"], w["mix_post_g"],
        deps=deps)
    d_out = _wgrad(sv["merged"], dom, "wgrad_w_out", bf16)
    d_co = _wgrad(sv["s"], dya, "wgrad_conv_out")
    d_so = _wgrad(sv["um"], dyb, "wgrad_sg_out")
    d_wo = _wgrad(sv["o"], dyc, "wgrad_w_o")

    dq, dk, dv = _attn_bwd(sv["q"], sv["k"], sv["v"], sv["o"], sv["lse"], do)
    dqb, dkb, dvb, dqlat, dkvlat, dkrope, vec["mla_q_norm_g"], vec["mla_kv_norm_g"] = _mla_proj_bwd(
        l, dq, dk, dv, sv["qlat"], sv["kvlat"], cos_t, sin_t, w["mla_q_norm_g"], w["wuq_p"], w["mla_kv_norm_g"], w["wk_p"], w["wv_p"])
    d_uq = _wgrad(sv["qn"], dqb, "wgrad_w_uq")
    d_uk, d_uv = _wgrad_multi(sv["kvn"], [dkb, dvb], "wgrad_w_ukv")

    db_in, dsg_w, dsgb, vec["sg_ln_g"], vec["sg_ln_b"] = _sgu_bwd(l, sv["b_in"], dum, w["sg_ln_g"], w["sg_ln_b"], w["sg_w"], w["bexp"])
    da_in, d_cdw, vec["conv_dw_b"], vec["conv_ln_g"], vec["conv_ln_b"] = _conv_bwd(
        l, sv["a_in"], sv["z1"], ds, w["conv_dw_w_p"], w["conv_ln_g"], w["conv_ln_b"])

    send = {}
    (send["conv_out_w"], send["sg_out_w"], send["mla_w_uq"], send["mla_w_ukv"], send["mla_w_o"], send["conv_dw_w"],
     send["ffn_dw_w"]) = _dis_small(d_co, d_so, d_uq, d_uk, d_uv, d_wo, d_cdw, d_fdw)
    send["w_out"] = d_out.reshape(N_DEV, d_out.shape[0] // N_DEV, d_out.shape[1])
    token = start(MIX_SMALL, send)

    dsegs = (da_in, db_in, dqlat, dkvlat, dkrope, dgates)
    d_in_segs = list(_wgrad_multi(sv["h"], dsegs[:5], "wgrad_w_in_abqkr", deps=token)) + [_wgrad(sv["h"], dgates, "wgrad_w_in_g")]
    token = start(("w_in",), dict(w_in=_dis_w_in(d_in_segs, shard_cols["w_in"])))
    dx, vec["mix_pre_g"] = _mix_in_bwd(l, sv["x"], w["mix_pre_g"], dx1, dsegs, w["w_in_p"], deps=token)
    rep_pack = _pack_rep([vec[n] for n, _, _, _ in REP_VECS[:-1]], dsgb)
    return dx, rep_pack, dsg_w


def kernel(x, positions, mix_pre_g, mix_post_g, ffn_pre_g, ffn_post_g, w_in, conv_dw_w, conv_dw_b, conv_ln_g, conv_ln_b, conv_out_w, sg_ln_g, sg_ln_b, sg_w, sg_b, sg_out_w, mla_q_norm_g, mla_w_uq, mla_kv_norm_g, mla_w_ukv, mla_w_o, w_out, ffn_w_up, ffn_dw_w, ffn_dw_b, ffn_w_down, loss_target, m_mix_pre_g, m_mix_post_g, m_ffn_pre_g, m_ffn_post_g, m_w_in, m_conv_dw_w, m_conv_dw_b, m_conv_ln_g, m_conv_ln_b, m_conv_out_w, m_sg_ln_g, m_sg_ln_b, m_sg_w, m_sg_b, m_sg_out_w, m_mla_q_norm_g, m_mla_w_uq, m_mla_kv_norm_g, m_mla_w_ukv, m_mla_w_o, m_w_out, m_ffn_w_up, m_ffn_dw_w, m_ffn_dw_b, m_ffn_w_down, v_mix_pre_g, v_mix_post_g, v_ffn_pre_g, v_ffn_post_g, v_w_in, v_conv_dw_w, v_conv_dw_b, v_conv_ln_g, v_conv_ln_b, v_conv_out_w, v_sg_ln_g, v_sg_ln_b, v_sg_w, v_sg_b, v_sg_out_w, v_mla_q_norm_g, v_mla_w_uq, v_mla_kv_norm_g, v_mla_w_ukv, v_mla_w_o, v_w_out, v_ffn_w_up, v_ffn_dw_w, v_ffn_dw_b, v_ffn_w_down):
    args = (x, positions, mix_pre_g, mix_post_g, ffn_pre_g, ffn_post_g, w_in, conv_dw_w, conv_dw_b, conv_ln_g, conv_ln_b, conv_out_w, sg_ln_g, sg_ln_b, sg_w, sg_b, sg_out_w, mla_q_norm_g, mla_w_uq, mla_kv_norm_g, mla_w_ukv, mla_w_o, w_out, ffn_w_up, ffn_dw_w, ffn_dw_b, ffn_w_down, loss_target, m_mix_pre_g, m_mix_post_g, m_ffn_pre_g, m_ffn_post_g, m_w_in, m_conv_dw_w, m_conv_dw_b, m_conv_ln_g, m_conv_ln_b, m_conv_out_w, m_sg_ln_g, m_sg_ln_b, m_sg_w, m_sg_b, m_sg_out_w, m_mla_q_norm_g, m_mla_w_uq, m_mla_kv_norm_g, m_mla_w_ukv, m_mla_w_o, m_w_out, m_ffn_w_up, m_ffn_dw_w, m_ffn_dw_b, m_ffn_w_down, v_mix_pre_g, v_mix_post_g, v_ffn_pre_g, v_ffn_post_g, v_w_in, v_conv_dw_w, v_conv_dw_b, v_conv_ln_g, v_conv_ln_b, v_conv_out_w, v_sg_ln_g, v_sg_ln_b, v_sg_w, v_sg_b, v_sg_out_w, v_mla_q_norm_g, v_mla_w_uq, v_mla_kv_norm_g, v_mla_w_ukv, v_mla_w_o, v_w_out, v_ffn_w_up, v_ffn_dw_w, v_ffn_dw_b, v_ffn_w_down)
    n_in = len(IN_NAMES)
    a = dict(zip(IN_NAMES, args[:n_in]))
    target = args[n_in]
    n_w = len(WEIGHTS)
    m_in = dict(zip(WEIGHTS, args[n_in + 1:n_in + 1 + n_w]))
    v_in = dict(zip(WEIGHTS, args[n_in + 1 + n_w:n_in + 1 + 2 * n_w]))
    depth = a["mix_pre_g"].shape[0]
    rep = {n: a[n] for n in WEIGHTS if n not in SHARDED}
    shard_cols = {n: a[n].shape[2] for n in SHARDED}

    def view(arr, n):
        return jnp.swapaxes(arr, 1, 2) if n in TRANSPOSED else arr

    w_sh = {n: view(a[n], n) for n in SHARDED}
    m_sh = {n: view(m_in[n], n) for n in SHARDED}
    v_sh = {n: view(v_in[n], n) for n in SHARDED}

    def gather_groups(l):
        return (("mix", MIX_GROUP), ("ffn", FFN_BIG))

    keys = [(l, tag, group) for l in range(depth) for tag, group in gather_groups(l)]
    wires = [[w_sh[n][l] if n in WIRE_F32 else w_sh[n][l].astype(bf16) for n in group] for l, _, group in keys]
    handles = _gather_start_many(wires, ["gather_%s_weights_%d" % (tag, l) for l, tag, _ in keys])
    starts = {(l, tag): h for (l, tag, _), h in zip(keys, handles)}
    token = (handles[0]["token"],)

    cos_t, sin_t = _rope_tables(a["positions"][0])
    xl = a["x"][0]
    ws, saved = [], []
    passing = {}

    def pass_on(l, tag, after):
        passing[l, tag] = _gather_forward(starts[l, tag], after=after)
        return (passing[l, tag]["token"],)

    relaid = tuple(t[n] for n in SHARDED for t in (m_sh, v_sh))
    after = pass_on(0, gather_groups(0)[0][0], token + relaid)
    for l in range(depth):
        groups = gather_groups(l)
        got = dict(zip(groups[0][1], _gather_wait(passing[l, groups[0][0]], after=after)))
        w = _mixer_weights(got, rep, l)
        x1, sv = _mixer_fwd(xl, w, cos_t, sin_t)
        after = (x1,)
        if len(groups) > 1:
            after = pass_on(l, groups[1][0], after)
        if l + 1 < depth:
            after = pass_on(l + 1, gather_groups(l + 1)[0][0], after)
        if len(groups) > 1:
            got = dict(zip(groups[1][1], _gather_wait(passing[l, groups[1][0]], after=after)))
        w.update(_ffn_weights(got))
        xl, sv_ffn = _ffn_layer_fwd(x1, w)
        sv.update(sv_ffn)
        ws.append(w)
        saved.append(sv)
        after = (xl,)
    dx, loss_part = _loss_fwd_bwd(xl, target[0])
    loss = lax.psum(loss_part, AXES)

    outs = {}
    rep_recvs, sgw_recvs = [None] * depth, [None] * depth

    def finish(l, parts, after):
        for names, handle in parts:
            got = _exchange_wait(handle, after=after)
            for n, recv in zip(names, got):
                outs[n] = _adamw_shard(l, recv, w_sh[n], m_sh[n], v_sh[n], outs.get(n), "adamw_" + n)
            after = tuple(outs[n][0] for n in names) or after
        rep_recvs[l], sgw_recvs[l] = got[0], got[1]
        return after

    token, prev, after = (), None, ()
    for l in reversed(range(depth)):
        dx1, send_ffn, vec = _ffn_layer_bwd(dx, saved[l], ws[l], deps=token)
        parts = []

        def start(names, send, extra=(), l=l, parts=parts):
            tag = names[0] if names else "rep"
            handle = _exchange_start([(send[n], "scatter") for n in names] + list(extra), "exchange_%s_grads_%d" % (tag, l))
            parts.append((names, handle))
            return (handle["token"],)

        token = start(FFN_BIG, send_ffn)
        dx, rep_pack, dsg_w = _mixer_bwd(dx1, saved[l], ws[l], cos_t, sin_t, shard_cols, vec, start, deps=token)
        token = start((), {}, extra=[(rep_pack, "gather"), (dsg_w, "gather")])
        if prev is not None:
            after = finish(*prev, after=(dx,) + token)
        prev = (l, parts)
    finish(*prev, after=after)
    vec_names = [n for n, _, _, _ in REP_VECS]
    rep_outs = _adamw_rep(rep_recvs, [a[n] for n in vec_names], [m_in[n] for n in vec_names], [v_in[n] for n in vec_names])
    for i, n in enumerate(vec_names):
        outs[n] = rep_outs[4 * i:4 * i + 4]
    outs["sg_w"] = _adamw_sg_w(sgw_recvs, a["sg_w"], m_in["sg_w"], v_in["sg_w"])
    for n in TRANSPOSED:
        outs[n] = tuple(view(o, n) for o in outs[n])

    grad_w, delta_w, new_m, new_v = ([outs[n][j] for n in WEIGHTS] for j in range(4))
    return (loss, dx[None], *grad_w, *delta_w, *new_m, *new_v)
```
